```python
import math
import jax, jax.numpy as jnp
from jax import lax
import numpy as np

D_MODEL = 1024
BATCH = 8
SEQ = 8192
DEPTH = 2

N_EVEN = (DEPTH + 1) // 2
N_ODD = DEPTH // 2

CONV_A_CH = D_MODEL // 2
CONV_A_WIDTH = 31
LN_EPS = 1e-5

HEAD_DIM = 64
HEADS_PER_GROUP = 8
DILATED_GROUPS = ((128, 1), (512, 4), (2048, 16))
N_GROUPS = len(DILATED_GROUPS)
N_ATTN_HEADS = N_GROUPS * HEADS_PER_GROUP
ATTN_OUT = HEADS_PER_GROUP * HEAD_DIM
ATTN_IN = 3 * N_ATTN_HEADS * HEAD_DIM
AB_IN = 2 * CONV_A_CH + ATTN_IN
AB_CAT = CONV_A_CH + ATTN_OUT

NUM_BUCKETS = 32
REL_MAX_DISTANCE = 2048

SC_WIDTH = D_MODEL
SC_CONV_WIDTH = 3

D_FF = 4 * D_MODEL
RMS_EPS = 1e-6
NEG_INF = -1e30

kernel_name = 'hybrid_conformer_dilated_shortconv'


def rms_norm(x, g, eps=RMS_EPS):
    xf = x.astype(jnp.float32)
    y = xf * lax.rsqrt(jnp.mean(xf * xf, axis=-1, keepdims=True) + eps)
    return (y * g.astype(jnp.float32)).astype(x.dtype)


def layer_norm(x, g, b, eps=LN_EPS):
    xf = x.astype(jnp.float32)
    mu = jnp.mean(xf, axis=-1, keepdims=True)
    var = jnp.mean(jnp.square(xf - mu), axis=-1, keepdims=True)
    y = (xf - mu) * lax.rsqrt(var + eps) * g.astype(jnp.float32) + b.astype(jnp.float32)
    return y.astype(x.dtype)


def causal_depthwise_conv(x, w):
    k_width, ch = w.shape
    xp = jnp.pad(x, ((0, 0), (k_width - 1, 0), (0, 0)))
    return lax.conv_general_dilated(
        xp, w[:, None, :].astype(x.dtype), window_strides=(1,), padding='VALID',
        dimension_numbers=('NWC', 'WIO', 'NWC'), feature_group_count=ch)


def t5_causal_bucket(n):
    max_exact = NUM_BUCKETS // 2
    nf = jnp.maximum(n, 1).astype(jnp.float32)
    large = max_exact + (jnp.log(nf / max_exact) / math.log(REL_MAX_DISTANCE / max_exact)
                         * (NUM_BUCKETS - max_exact)).astype(jnp.int32)
    return jnp.where(n < max_exact, n, jnp.minimum(large, NUM_BUCKETS - 1))


def dilated_group_attention(q, k, v, bias_table, window, dilation):
    bsz, seq, heads, dh = q.shape
    steps = window // dilation
    span = steps * dilation
    seq_pad = -(-seq // span) * span
    n_blk = seq_pad // span

    def to_blocks(t):
        t = jnp.pad(t, ((0, 0), (0, seq_pad - seq), (0, 0), (0, 0)))
        t = t.reshape(bsz, n_blk, steps, dilation, heads, dh)
        return t.transpose(0, 3, 4, 1, 2, 5)

    def with_prev(t):
        prev = jnp.pad(t[:, :, :, :-1], ((0, 0), (0, 0), (0, 0), (1, 0), (0, 0), (0, 0)))
        return jnp.concatenate([prev, t], axis=4)

    qb = to_blocks(q)
    kc = with_prev(to_blocks(k))
    vc = with_prev(to_blocks(v))
    scores = jnp.einsum('bdhnqe,bdhnke->bdhnqk', qb, kc,
                        preferred_element_type=jnp.float32) * (dh ** -0.5)

    a_idx = jnp.arange(steps)[:, None]
    c_idx = jnp.arange(2 * steps)[None, :]
    m = a_idx + steps - c_idx
    band = (m >= 0) & (m <= steps)
    first = (jnp.arange(n_blk) == 0)[:, None, None] & (c_idx < steps)[None]
    valid = band[None] & ~first
    bucket = t5_causal_bucket(jnp.clip(m, 0, steps) * dilation)
    bias = bias_table[bucket].astype(jnp.float32).transpose(2, 0, 1)

    logits = jnp.where(valid, scores + bias[:, None], NEG_INF)
    mx = jnp.max(logits, axis=-1, keepdims=True)
    p = jnp.exp(logits - mx)
    den = jnp.sum(p, axis=-1)
    out = jnp.einsum('bdhnqk,bdhnke->bdhnqe', p, vc.astype(jnp.float32)) / den[..., None]
    lse = mx[..., 0] + jnp.log(den)
    out = out.transpose(0, 3, 4, 1, 2, 5).reshape(bsz, seq_pad, heads, dh)[:, :seq]
    lse = lse.transpose(0, 3, 4, 1, 2).reshape(bsz, seq_pad, heads)[:, :seq]
    return out, lse


def dilated_mixture_attention(qkv, rel_bias):
    bsz, seq, _ = qkv.shape
    qkv = qkv.reshape(bsz, seq, 3, N_GROUPS, HEADS_PER_GROUP, HEAD_DIM)
    outs, lses = [], []
    for g, (window, dilation) in enumerate(DILATED_GROUPS):
        o, l = dilated_group_attention(
            qkv[:, :, 0, g], qkv[:, :, 1, g], qkv[:, :, 2, g],
            rel_bias[:, g * HEADS_PER_GROUP:(g + 1) * HEADS_PER_GROUP], window, dilation)
        outs.append(o)
        lses.append(l)
    wts = jax.nn.softmax(jnp.stack(lses, axis=0), axis=0)
    out = jnp.sum(wts[..., None] * jnp.stack(outs, axis=0), axis=0)
    return out.reshape(bsz, seq, ATTN_OUT)


def conformer_conv_module(u, conv_w, conv_b, ln_g, ln_b):
    val, gate = jnp.split(u, 2, axis=-1)
    h = val * jax.nn.sigmoid(gate)
    h = causal_depthwise_conv(h, conv_w) + conv_b.astype(h.dtype)
    h = layer_norm(h, ln_g, ln_b)
    return jax.nn.silu(h)


def short_gated_conv(u, conv_w):
    b_gate, c_gate, val = jnp.split(u, 3, axis=-1)
    return b_gate * causal_depthwise_conv(c_gate * val, conv_w)


def _fwd_setup_inputs(seed: int = 0) -> dict:
    key = jax.random.key(seed)
    ks = jax.random.split(key, 17)
    D = D_MODEL

    def nrm(k, shape, scale):
        return scale * jax.random.normal(k, shape, jnp.float32)

    return {
        'x': nrm(ks[0], (BATCH, SEQ, D), 1.0),
        'rel_bias': nrm(ks[1], (NUM_BUCKETS, N_ATTN_HEADS), 0.5),
        'ab_norm': 1.0 + nrm(ks[2], (N_EVEN, D), 0.01),
        'ab_w_in': nrm(ks[3], (N_EVEN, D, AB_IN), D ** -0.5),
        'ab_conv_w': nrm(ks[4], (N_EVEN, CONV_A_WIDTH, CONV_A_CH), CONV_A_WIDTH ** -0.5),
        'ab_conv_b': nrm(ks[5], (N_EVEN, CONV_A_CH), 0.01),
        'ab_ln_g': 1.0 + nrm(ks[6], (N_EVEN, CONV_A_CH), 0.01),
        'ab_ln_b': nrm(ks[7], (N_EVEN, CONV_A_CH), 0.01),
        'ab_w_out': nrm(ks[8], (N_EVEN, AB_CAT, D), AB_CAT ** -0.5),
        'sc_norm': 1.0 + nrm(ks[9], (N_ODD, D), 0.01),
        'sc_w_in': nrm(ks[10], (N_ODD, D, 3 * SC_WIDTH), D ** -0.5),
        'sc_conv_w': nrm(ks[11], (N_ODD, SC_CONV_WIDTH, SC_WIDTH), SC_CONV_WIDTH ** -0.5),
        'sc_w_out': nrm(ks[12], (N_ODD, SC_WIDTH, D), SC_WIDTH ** -0.5),
        'mlp_norm': 1.0 + nrm(ks[13], (DEPTH, D), 0.01),
        'mlp_w_up': nrm(ks[14], (DEPTH, D, D_FF), D ** -0.5),
        'mlp_w_down': nrm(ks[15], (DEPTH, D_FF, D), D_FF ** -0.5),
        'final_norm': 1.0 + nrm(ks[16], (D,), 0.01),
    }


def _fwd_reference(x, rel_bias, ab_norm, ab_w_in, ab_conv_w, ab_conv_b, ab_ln_g, ab_ln_b,
              ab_w_out, sc_norm, sc_w_in, sc_conv_w, sc_w_out, mlp_norm, mlp_w_up,
              mlp_w_down, final_norm):
    h = x
    for layer in range(DEPTH):
        i = layer // 2
        if layer % 2 == 0:
            u = jnp.einsum('bsd,de->bse', rms_norm(h, ab_norm[i]), ab_w_in[i])
            ya = conformer_conv_module(u[..., :2 * CONV_A_CH], ab_conv_w[i], ab_conv_b[i],
                                       ab_ln_g[i], ab_ln_b[i])
            yb = dilated_mixture_attention(u[..., 2 * CONV_A_CH:], rel_bias).astype(ya.dtype)
            y = jnp.einsum('bsc,cd->bsd', jnp.concatenate([ya, yb], axis=-1), ab_w_out[i])
        else:
            u = jnp.einsum('bsd,de->bse', rms_norm(h, sc_norm[i]), sc_w_in[i])
            y = jnp.einsum('bsc,cd->bsd', short_gated_conv(u, sc_conv_w[i]), sc_w_out[i])
        h = h + y.astype(h.dtype)
        z = jnp.einsum('bsd,df->bsf', rms_norm(h, mlp_norm[layer]), mlp_w_up[layer])
        z = jnp.square(jax.nn.relu(z))
        h = h + jnp.einsum('bsf,fd->bsd', z, mlp_w_down[layer]).astype(h.dtype)
    return rms_norm(h, final_norm)


import jax as _jax
import jax.numpy as _jnp

TWIN_FORMAT = 'train_step'
FWD_PARAMS = ['x', 'rel_bias', 'ab_norm', 'ab_w_in', 'ab_conv_w', 'ab_conv_b', 'ab_ln_g', 'ab_ln_b', 'ab_w_out', 'sc_norm', 'sc_w_in', 'sc_conv_w', 'sc_w_out', 'mlp_norm', 'mlp_w_up', 'mlp_w_down', 'final_norm']
TWIN_WEIGHTS = ['rel_bias', 'ab_norm', 'ab_w_in', 'ab_conv_w', 'ab_conv_b', 'ab_ln_g', 'ab_ln_b', 'ab_w_out', 'sc_norm', 'sc_w_in', 'sc_conv_w', 'sc_w_out', 'mlp_norm', 'mlp_w_up', 'mlp_w_down', 'final_norm']
TWIN_DIFF_INPUT = 'x'
TWIN_INPUTS = ['x', 'rel_bias', 'ab_norm', 'ab_w_in', 'ab_conv_w', 'ab_conv_b', 'ab_ln_g', 'ab_ln_b', 'ab_w_out', 'sc_norm', 'sc_w_in', 'sc_conv_w', 'sc_w_out', 'mlp_norm', 'mlp_w_up', 'mlp_w_down', 'final_norm', 'loss_target', 'm_rel_bias', 'm_ab_norm', 'm_ab_w_in', 'm_ab_conv_w', 'm_ab_conv_b', 'm_ab_ln_g', 'm_ab_ln_b', 'm_ab_w_out', 'm_sc_norm', 'm_sc_w_in', 'm_sc_conv_w', 'm_sc_w_out', 'm_mlp_norm', 'm_mlp_w_up', 'm_mlp_w_down', 'm_final_norm', 'v_rel_bias', 'v_ab_norm', 'v_ab_w_in', 'v_ab_conv_w', 'v_ab_conv_b', 'v_ab_ln_g', 'v_ab_ln_b', 'v_ab_w_out', 'v_sc_norm', 'v_sc_w_in', 'v_sc_conv_w', 'v_sc_w_out', 'v_mlp_norm', 'v_mlp_w_up', 'v_mlp_w_down', 'v_final_norm']
TWIN_OUTPUTS = ['loss', 'grad_x', 'grad_rel_bias', 'grad_ab_norm', 'grad_ab_w_in', 'grad_ab_conv_w', 'grad_ab_conv_b', 'grad_ab_ln_g', 'grad_ab_ln_b', 'grad_ab_w_out', 'grad_sc_norm', 'grad_sc_w_in', 'grad_sc_conv_w', 'grad_sc_w_out', 'grad_mlp_norm', 'grad_mlp_w_up', 'grad_mlp_w_down', 'grad_final_norm', 'delta_rel_bias', 'delta_ab_norm', 'delta_ab_w_in', 'delta_ab_conv_w', 'delta_ab_conv_b', 'delta_ab_ln_g', 'delta_ab_ln_b', 'delta_ab_w_out', 'delta_sc_norm', 'delta_sc_w_in', 'delta_sc_conv_w', 'delta_sc_w_out', 'delta_mlp_norm', 'delta_mlp_w_up', 'delta_mlp_w_down', 'delta_final_norm', 'new_m_rel_bias', 'new_m_ab_norm', 'new_m_ab_w_in', 'new_m_ab_conv_w', 'new_m_ab_conv_b', 'new_m_ab_ln_g', 'new_m_ab_ln_b', 'new_m_ab_w_out', 'new_m_sc_norm', 'new_m_sc_w_in', 'new_m_sc_conv_w', 'new_m_sc_w_out', 'new_m_mlp_norm', 'new_m_mlp_w_up', 'new_m_mlp_w_down', 'new_m_final_norm', 'new_v_rel_bias', 'new_v_ab_norm', 'new_v_ab_w_in', 'new_v_ab_conv_w', 'new_v_ab_conv_b', 'new_v_ab_ln_g', 'new_v_ab_ln_b', 'new_v_ab_w_out', 'new_v_sc_norm', 'new_v_sc_w_in', 'new_v_sc_conv_w', 'new_v_sc_w_out', 'new_v_mlp_norm', 'new_v_mlp_w_up', 'new_v_mlp_w_down', 'new_v_final_norm']
TWIN_LEAF_KINDS = {'loss': 'loss', 'grad_x': 'grad_x', 'grad_rel_bias': 'grad_w', 'grad_ab_norm': 'grad_w', 'grad_ab_w_in': 'grad_w', 'grad_ab_conv_w': 'grad_w', 'grad_ab_conv_b': 'grad_w', 'grad_ab_ln_g': 'grad_w', 'grad_ab_ln_b': 'grad_w', 'grad_ab_w_out': 'grad_w', 'grad_sc_norm': 'grad_w', 'grad_sc_w_in': 'grad_w', 'grad_sc_conv_w': 'grad_w', 'grad_sc_w_out': 'grad_w', 'grad_mlp_norm': 'grad_w', 'grad_mlp_w_up': 'grad_w', 'grad_mlp_w_down': 'grad_w', 'grad_final_norm': 'grad_w', 'delta_rel_bias': 'delta_w', 'delta_ab_norm': 'delta_w', 'delta_ab_w_in': 'delta_w', 'delta_ab_conv_w': 'delta_w', 'delta_ab_conv_b': 'delta_w', 'delta_ab_ln_g': 'delta_w', 'delta_ab_ln_b': 'delta_w', 'delta_ab_w_out': 'delta_w', 'delta_sc_norm': 'delta_w', 'delta_sc_w_in': 'delta_w', 'delta_sc_conv_w': 'delta_w', 'delta_sc_w_out': 'delta_w', 'delta_mlp_norm': 'delta_w', 'delta_mlp_w_up': 'delta_w', 'delta_mlp_w_down': 'delta_w', 'delta_final_norm': 'delta_w', 'new_m_rel_bias': 'new_m', 'new_m_ab_norm': 'new_m', 'new_m_ab_w_in': 'new_m', 'new_m_ab_conv_w': 'new_m', 'new_m_ab_conv_b': 'new_m', 'new_m_ab_ln_g': 'new_m', 'new_m_ab_ln_b': 'new_m', 'new_m_ab_w_out': 'new_m', 'new_m_sc_norm': 'new_m', 'new_m_sc_w_in': 'new_m', 'new_m_sc_conv_w': 'new_m', 'new_m_sc_w_out': 'new_m', 'new_m_mlp_norm': 'new_m', 'new_m_mlp_w_up': 'new_m', 'new_m_mlp_w_down': 'new_m', 'new_m_final_norm': 'new_m', 'new_v_rel_bias': 'new_v', 'new_v_ab_norm': 'new_v', 'new_v_ab_w_in': 'new_v', 'new_v_ab_conv_w': 'new_v', 'new_v_ab_conv_b': 'new_v', 'new_v_ab_ln_g': 'new_v', 'new_v_ab_ln_b': 'new_v', 'new_v_ab_w_out': 'new_v', 'new_v_sc_norm': 'new_v', 'new_v_sc_w_in': 'new_v', 'new_v_sc_conv_w': 'new_v', 'new_v_sc_w_out': 'new_v', 'new_v_mlp_norm': 'new_v', 'new_v_mlp_w_up': 'new_v', 'new_v_mlp_w_down': 'new_v', 'new_v_final_norm': 'new_v'}


def _forward(args):
    return _fwd_reference(*[args[k] for k in FWD_PARAMS])


def _output_shape():
    def fwd():
        inp = _fwd_setup_inputs(0)
        return _fwd_reference(*[inp[k] for k in FWD_PARAMS])
    out = _jax.eval_shape(fwd)
    return out.shape, out.dtype

N_MICROBATCH = 1
ADAM_LR = 0.001
ADAM_B1 = 0.9
ADAM_B2 = 0.999
ADAM_EPS = 1e-08
ADAM_WD = 0.01
ADAM_STEP = 10
PER_EXAMPLE_BATCH_AXIS = {'x': 0, 'loss_target': 0}
SHARED_INPUTS = []
_WEIGHT_DTYPES = {'rel_bias': _jnp.float32, 'ab_norm': _jnp.float32, 'ab_w_in': _jnp.float32, 'ab_conv_w': _jnp.float32, 'ab_conv_b': _jnp.float32, 'ab_ln_g': _jnp.float32, 'ab_ln_b': _jnp.float32, 'ab_w_out': _jnp.float32, 'sc_norm': _jnp.float32, 'sc_w_in': _jnp.float32, 'sc_conv_w': _jnp.float32, 'sc_w_out': _jnp.float32, 'mlp_norm': _jnp.float32, 'mlp_w_up': _jnp.float32, 'mlp_w_down': _jnp.float32, 'final_norm': _jnp.float32}
MOMENT_SCALE = {'rel_bias': 4.285274e-02, 'ab_norm': 1.724618e-01, 'ab_w_in': 7.172616e-02, 'ab_conv_w': 2.001832e-01, 'ab_conv_b': 3.834889e-01, 'ab_ln_g': 2.290557e-01, 'ab_ln_b': 2.277474e-01, 'ab_w_out': 1.435058e-01, 'sc_norm': 2.535647e-01, 'sc_w_in': 1.363678e-01, 'sc_conv_w': 1.409459e-01, 'sc_w_out': 1.343003e-01, 'mlp_norm': 2.275078e-01, 'mlp_w_up': 1.144348e-01, 'mlp_w_down': 2.081133e-01, 'final_norm': 6.473620e+01}


def _to_microbatches(a, axis):
    t = _jnp.moveaxis(a, axis, 0)
    t = t.reshape((N_MICROBATCH, t.shape[0] // N_MICROBATCH) + t.shape[1:])
    return _jnp.moveaxis(t, 1, axis + 1)


def setup_inputs(seed: int = 0) -> dict:
    inp = _fwd_setup_inputs(seed)
    key = _jax.random.fold_in(_jax.random.key(seed), 7919)
    shape, _ = _output_shape()
    out = dict(inp)
    out["loss_target"] = _jax.random.normal(_jax.random.fold_in(key, 0), shape, _jnp.float32)
    for i, name in enumerate(TWIN_WEIGHTS):
        w = inp[name].astype(_jnp.float32)
        if MOMENT_SCALE is None:
            s = _jnp.sqrt(_jnp.mean(_jnp.square(w)) + 1e-30)
        else:
            s = MOMENT_SCALE[name]
        km, kv = _jax.random.split(_jax.random.fold_in(key, i + 1))
        out[name] = w
        out["m_" + name] = s * _jax.random.normal(km, w.shape, _jnp.float32)
        out["v_" + name] = (s * s) * _jax.random.uniform(kv, w.shape, _jnp.float32, 0.5, 1.5)
    if N_MICROBATCH > 1:
        for name, axis in PER_EXAMPLE_BATCH_AXIS.items():
            out[name] = _to_microbatches(out[name], axis)
    return {'x': out['x'], 'rel_bias': out['rel_bias'], 'ab_norm': out['ab_norm'], 'ab_w_in': out['ab_w_in'], 'ab_conv_w': out['ab_conv_w'], 'ab_conv_b': out['ab_conv_b'], 'ab_ln_g': out['ab_ln_g'], 'ab_ln_b': out['ab_ln_b'], 'ab_w_out': out['ab_w_out'], 'sc_norm': out['sc_norm'], 'sc_w_in': out['sc_w_in'], 'sc_conv_w': out['sc_conv_w'], 'sc_w_out': out['sc_w_out'], 'mlp_norm': out['mlp_norm'], 'mlp_w_up': out['mlp_w_up'], 'mlp_w_down': out['mlp_w_down'], 'final_norm': out['final_norm'], 'loss_target': out['loss_target'], 'm_rel_bias': out['m_rel_bias'], 'm_ab_norm': out['m_ab_norm'], 'm_ab_w_in': out['m_ab_w_in'], 'm_ab_conv_w': out['m_ab_conv_w'], 'm_ab_conv_b': out['m_ab_conv_b'], 'm_ab_ln_g': out['m_ab_ln_g'], 'm_ab_ln_b': out['m_ab_ln_b'], 'm_ab_w_out': out['m_ab_w_out'], 'm_sc_norm': out['m_sc_norm'], 'm_sc_w_in': out['m_sc_w_in'], 'm_sc_conv_w': out['m_sc_conv_w'], 'm_sc_w_out': out['m_sc_w_out'], 'm_mlp_norm': out['m_mlp_norm'], 'm_mlp_w_up': out['m_mlp_w_up'], 'm_mlp_w_down': out['m_mlp_w_down'], 'm_final_norm': out['m_final_norm'], 'v_rel_bias': out['v_rel_bias'], 'v_ab_norm': out['v_ab_norm'], 'v_ab_w_in': out['v_ab_w_in'], 'v_ab_conv_w': out['v_ab_conv_w'], 'v_ab_conv_b': out['v_ab_conv_b'], 'v_ab_ln_g': out['v_ab_ln_g'], 'v_ab_ln_b': out['v_ab_ln_b'], 'v_ab_w_out': out['v_ab_w_out'], 'v_sc_norm': out['v_sc_norm'], 'v_sc_w_in': out['v_sc_w_in'], 'v_sc_conv_w': out['v_sc_conv_w'], 'v_sc_w_out': out['v_sc_w_out'], 'v_mlp_norm': out['v_mlp_norm'], 'v_mlp_w_up': out['v_mlp_w_up'], 'v_mlp_w_down': out['v_mlp_w_down'], 'v_final_norm': out['v_final_norm']}


def _loss(weights, diff, rest, loss_target):
    with _jax.named_scope("forward"):
        args = {**rest, TWIN_DIFF_INPUT: diff, **{k: w.astype(_WEIGHT_DTYPES[k]) for k, w in weights.items()}}
        y = _forward(args)
    with _jax.named_scope("loss_head"):
        err = _jnp.square(y.astype(_jnp.float32) - loss_target)
        return 0.5 * _jnp.sum(_jnp.mean(err, axis=-1)) if err.ndim else 0.5 * err


def _adamw(w, g, m, v):
    m = ADAM_B1 * m + (1.0 - ADAM_B1) * g
    v = ADAM_B2 * v + (1.0 - ADAM_B2) * _jnp.square(g)
    m_hat = m / (1.0 - ADAM_B1 ** ADAM_STEP)
    v_hat = v / (1.0 - ADAM_B2 ** ADAM_STEP)
    delta = -ADAM_LR * (m_hat / (_jnp.sqrt(v_hat) + ADAM_EPS) + ADAM_WD * w)
    return delta, m, v


def reference(x, rel_bias, ab_norm, ab_w_in, ab_conv_w, ab_conv_b, ab_ln_g, ab_ln_b, ab_w_out, sc_norm, sc_w_in, sc_conv_w, sc_w_out, mlp_norm, mlp_w_up, mlp_w_down, final_norm, loss_target, m_rel_bias, m_ab_norm, m_ab_w_in, m_ab_conv_w, m_ab_conv_b, m_ab_ln_g, m_ab_ln_b, m_ab_w_out, m_sc_norm, m_sc_w_in, m_sc_conv_w, m_sc_w_out, m_mlp_norm, m_mlp_w_up, m_mlp_w_down, m_final_norm, v_rel_bias, v_ab_norm, v_ab_w_in, v_ab_conv_w, v_ab_conv_b, v_ab_ln_g, v_ab_ln_b, v_ab_w_out, v_sc_norm, v_sc_w_in, v_sc_conv_w, v_sc_w_out, v_mlp_norm, v_mlp_w_up, v_mlp_w_down, v_final_norm):
    given = dict(x=x, rel_bias=rel_bias, ab_norm=ab_norm, ab_w_in=ab_w_in, ab_conv_w=ab_conv_w, ab_conv_b=ab_conv_b, ab_ln_g=ab_ln_g, ab_ln_b=ab_ln_b, ab_w_out=ab_w_out, sc_norm=sc_norm, sc_w_in=sc_w_in, sc_conv_w=sc_conv_w, sc_w_out=sc_w_out, mlp_norm=mlp_norm, mlp_w_up=mlp_w_up, mlp_w_down=mlp_w_down, final_norm=final_norm, loss_target=loss_target, m_rel_bias=m_rel_bias, m_ab_norm=m_ab_norm, m_ab_w_in=m_ab_w_in, m_ab_conv_w=m_ab_conv_w, m_ab_conv_b=m_ab_conv_b, m_ab_ln_g=m_ab_ln_g, m_ab_ln_b=m_ab_ln_b, m_ab_w_out=m_ab_w_out, m_sc_norm=m_sc_norm, m_sc_w_in=m_sc_w_in, m_sc_conv_w=m_sc_conv_w, m_sc_w_out=m_sc_w_out, m_mlp_norm=m_mlp_norm, m_mlp_w_up=m_mlp_w_up, m_mlp_w_down=m_mlp_w_down, m_final_norm=m_final_norm, v_rel_bias=v_rel_bias, v_ab_norm=v_ab_norm, v_ab_w_in=v_ab_w_in, v_ab_conv_w=v_ab_conv_w, v_ab_conv_b=v_ab_conv_b, v_ab_ln_g=v_ab_ln_g, v_ab_ln_b=v_ab_ln_b, v_ab_w_out=v_ab_w_out, v_sc_norm=v_sc_norm, v_sc_w_in=v_sc_w_in, v_sc_conv_w=v_sc_conv_w, v_sc_w_out=v_sc_w_out, v_mlp_norm=v_mlp_norm, v_mlp_w_up=v_mlp_w_up, v_mlp_w_down=v_mlp_w_down, v_final_norm=v_final_norm)
    weights = {n: given[n] for n in TWIN_WEIGHTS}
    shared = {n: given[n] for n in SHARED_INPUTS}
    per_example = {n: given[n] for n in ['x']}
    grad_fn = _jax.value_and_grad(_loss, argnums=(0, 1))

    def one_microbatch(ex, loss_target):
        ex = dict(ex)
        diff = ex.pop(TWIN_DIFF_INPUT)
        return grad_fn(weights, diff, {**shared, **ex}, loss_target)

    if N_MICROBATCH == 1:
        loss, (grad_w, grad_x) = one_microbatch(per_example, given["loss_target"])
    else:
        def body(carry, xs):
            loss_sum, grad_sum = carry
            l_k, (gw_k, gx_k) = one_microbatch(xs[0], xs[1])
            with _jax.named_scope("update"):
                return (loss_sum + l_k, _jax.tree.map(_jnp.add, grad_sum, gw_k)), gx_k

        init = (_jnp.zeros((), _jnp.float32), _jax.tree.map(_jnp.zeros_like, weights))
        (loss, grad_w), grad_x = _jax.lax.scan(body, init, (per_example, given["loss_target"]))
    with _jax.named_scope("update"):
        delta_w, new_m, new_v = {}, {}, {}
        for n in TWIN_WEIGHTS:
            delta_w[n], new_m[n], new_v[n] = _adamw(weights[n], grad_w[n], given["m_" + n], given["v_" + n])
    return (loss, grad_x, *[grad_w[n] for n in TWIN_WEIGHTS], *[delta_w[n] for n in TWIN_WEIGHTS],
            *[new_m[n] for n in TWIN_WEIGHTS], *[new_v[n] for n in TWIN_WEIGHTS])
```

```python
import functools
import math

import jax
import jax.numpy as jnp
from jax import lax
from jax.experimental import pallas as pl
from jax.experimental.pallas import tpu as pltpu

F32 = jnp.float32
BF16 = jnp.bfloat16
MESH = pl.DeviceIdType.MESH

_GROUPS = ((128, 1), (512, 4), (2048, 16))
_STEPS = 128
_HEAD_DIM = 64
_GROUP_COLS = 512
_NUM_BUCKETS = 32
_MAX_DISTANCE = 2048
_CONV_K = 31
_HALO = 32
_SC_HALO = 16
_RMS_EPS = 1e-6
_LN_EPS = 1e-5
_NEG = -1e30
_LANES = 128
_VMEM_LIMIT = 56 * 1024 * 1024

_LR, _B1, _B2, _EPS, _WD, _STEP = 0.001, 0.9, 0.999, 1e-08, 0.01, 10


def _pcall(body, *, name, out_shape, in_specs, out_specs, grid=None, scratch=(), aliases=None):
    kw = {} if grid is None else {"grid": grid}
    return pl.pallas_call(
        body, name=name, out_shape=out_shape, in_specs=in_specs, out_specs=out_specs,
        scratch_shapes=list(scratch), input_output_aliases=aliases or {},
        compiler_params=pltpu.CompilerParams(vmem_limit_bytes=_VMEM_LIMIT), **kw)


def _sig(x):
    return 1.0 / (1.0 + jnp.exp(-x))


_ANY = pl.BlockSpec(memory_space=pl.ANY)


def _mm(a, b, mode, *, m, n, k, tm, tn, tk, out_dtype, name, epi=None, extras=(), b_off=(0, 0)):
    nk = k // tk
    assert m % tm == 0 and n % tn == 0 and k % tk == 0
    o0, o1 = b_off
    if mode == "nn":
        a_spec = pl.BlockSpec((tm, tk), lambda i, j, kk: (i, kk))
        b_spec = pl.BlockSpec((tk, tn), lambda i, j, kk: (kk + o0, j + o1))
        dn = (((1,), (0,)), ((), ()))
    elif mode == "nt":
        a_spec = pl.BlockSpec((tm, tk), lambda i, j, kk: (i, kk))
        b_spec = pl.BlockSpec((tn, tk), lambda i, j, kk: (j + o0, kk + o1))
        dn = (((1,), (1,)), ((), ()))
    else:
        a_spec = pl.BlockSpec((tk, tm), lambda i, j, kk: (kk, i))
        b_spec = pl.BlockSpec((tk, tn), lambda i, j, kk: (kk + o0, j + o1))
        dn = (((0,), (0,)), ((), ()))
    o_spec = pl.BlockSpec((tm, tn), lambda i, j, kk: (i, j))
    ne = len(extras)

    def body(a_ref, b_ref, *rest):
        ex, o_ref = rest[:ne], rest[ne]
        av, bv = a_ref[...], b_ref[...]
        if av.dtype != BF16:
            av = av.astype(BF16)
        if bv.dtype != BF16:
            bv = bv.astype(BF16)
        p = lax.dot_general(av, bv, dn, preferred_element_type=F32)

        def fin(x):
            if epi is not None:
                x = epi(x, *[e[...] for e in ex])
            return x.astype(out_dtype)

        if nk == 1:
            o_ref[...] = fin(p)
        else:
            acc = rest[ne + 1]
            kk = pl.program_id(2)

            @pl.when(kk == 0)
            def _():
                acc[...] = p

            @pl.when(kk > 0)
            def _():
                acc[...] += p

            @pl.when(kk == nk - 1)
            def _():
                o_ref[...] = fin(acc[...])

    return _pcall(
        body, name=name, grid=(m // tm, n // tn, nk),
        in_specs=[a_spec, b_spec] + [o_spec] * ne, out_specs=o_spec,
        out_shape=jax.ShapeDtypeStruct((m, n), out_dtype),
        scratch=[pltpu.VMEM((tm, tn), F32)] if nk > 1 else [],
    )(a, b, *extras)


def _epi_add(x, r):
    return x + r


def _epi_relu_sq(x):
    x = jnp.maximum(x, 0.0)
    return x * x


def _epi_relu_sq_bwd(da, act):
    return da * (2.0 * jnp.sqrt(act.astype(F32)))


_ROW_T = 512


def _rms_fwd(h, g, name):
    s, d = h.shape

    def body(h_ref, g_ref, o_ref):
        x = h_ref[...]
        r = lax.rsqrt(jnp.mean(x * x, axis=-1, keepdims=True) + _RMS_EPS)
        o_ref[...] = (x * r * g_ref[...]).astype(BF16)

    row = pl.BlockSpec((_ROW_T, d), lambda i: (i, 0))
    vec = pl.BlockSpec((1, d), lambda i: (0, 0))
    return _pcall(body, name=name, grid=(s // _ROW_T,), in_specs=[row, vec], out_specs=row,
                  out_shape=jax.ShapeDtypeStruct((s, d), BF16))(h, g)


def _rms_bwd_math(dn, x, g):
    r = lax.rsqrt(jnp.mean(x * x, axis=-1, keepdims=True) + _RMS_EPS)
    xhat = x * r
    dg = jnp.sum(dn * xhat, axis=0, keepdims=True)
    t = dn * g
    dx = r * (t - xhat * jnp.mean(t * xhat, axis=-1, keepdims=True))
    return dx, dg


def _rms_bwd(dn, h, g, dh_in, name):
    s, d = h.shape

    def body(dn_ref, h_ref, g_ref, dhi_ref, dh_ref, dhb_ref, dg_ref):
        dx, dg = _rms_bwd_math(dn_ref[...], h_ref[...], g_ref[...])
        dh = dhi_ref[...] + dx
        dh_ref[...] = dh
        dhb_ref[...] = dh.astype(BF16)

        @pl.when(pl.program_id(0) == 0)
        def _():
            dg_ref[...] = jnp.zeros_like(dg_ref)

        dg_ref[...] += dg

    row = pl.BlockSpec((_ROW_T, d), lambda i: (i, 0))
    vec = pl.BlockSpec((1, d), lambda i: (0, 0))
    return _pcall(
        body, name=name, grid=(s // _ROW_T,), in_specs=[row, row, vec, row], out_specs=[row, row, vec],
        out_shape=[jax.ShapeDtypeStruct((s, d), F32), jax.ShapeDtypeStruct((s, d), BF16),
                   jax.ShapeDtypeStruct((1, d), F32)])(dn, h, g, dh_in)


def _loss_head(h, tgt, g):
    s, d = h.shape

    def body(h_ref, t_ref, g_ref, dh_ref, dhb_ref, dg_ref, loss_ref):
        x, gv = h_ref[...], g_ref[...]
        r = lax.rsqrt(jnp.mean(x * x, axis=-1, keepdims=True) + _RMS_EPS)
        err = x * r * gv - t_ref[...]
        part = 0.5 * jnp.sum(jnp.mean(err * err, axis=-1, keepdims=True))
        dx, dg = _rms_bwd_math(err * (1.0 / d), x, gv)
        dh_ref[...] = dx
        dhb_ref[...] = dx.astype(BF16)

        @pl.when(pl.program_id(0) == 0)
        def _():
            dg_ref[...] = jnp.zeros_like(dg_ref)
            loss_ref[...] = jnp.zeros_like(loss_ref)

        dg_ref[...] += dg
        loss_ref[...] += jnp.full(loss_ref.shape, part, F32)

    row = pl.BlockSpec((_ROW_T, d), lambda i: (i, 0))
    vec = pl.BlockSpec((1, d), lambda i: (0, 0))
    one = pl.BlockSpec((1, _LANES), lambda i: (0, 0))
    return _pcall(
        body, name="loss_head", grid=(s // _ROW_T,), in_specs=[row, row, vec], out_specs=[row, row, vec, one],
        out_shape=[jax.ShapeDtypeStruct((s, d), F32), jax.ShapeDtypeStruct((s, d), BF16),
                   jax.ShapeDtypeStruct((1, d), F32), jax.ShapeDtypeStruct((1, _LANES), F32)])(h, tgt, g)


_CONV_T = 256
_CONV_RC = 64


def _conv_a_specs(s):
    c = _GROUP_COLS
    hb = _CONV_T // _HALO
    val = pl.BlockSpec((_CONV_T, c), lambda i: (i, 0))
    gate = pl.BlockSpec((_CONV_T, c), lambda i: (i, 1))
    hval = pl.BlockSpec((_HALO, c), lambda i: (jnp.maximum(i * hb - 1, 0), 0))
    hgate = pl.BlockSpec((_HALO, c), lambda i: (jnp.maximum(i * hb - 1, 0), 1))
    return val, gate, hval, hgate


def _fill_glu(val_ref, gate_ref, hval_ref, hgate_ref, hs_ref):
    i = pl.program_id(0)
    hs_ref[pl.ds(_HALO, _CONV_T), :] = val_ref[...].astype(F32) * _sig(gate_ref[...].astype(F32))
    halo = hval_ref[...].astype(F32) * _sig(hgate_ref[...].astype(F32))
    hs_ref[pl.ds(0, _HALO), :] = jnp.where(i > 0, halo, 0.0)


def _conv_rows(hs_ref, w_ref, r0, rows):
    off = _HALO - (_CONV_K - 1)
    acc = jnp.zeros((rows, _GROUP_COLS), F32)
    for kk in range(_CONV_K):
        acc = acc + w_ref[kk:kk + 1, :] * hs_ref[pl.ds(r0 + off + kk, rows), :]
    return acc


def _ln_fwd(ca, g, b):
    mu = jnp.mean(ca, axis=-1, keepdims=True)
    xc = ca - mu
    rstd = lax.rsqrt(jnp.mean(xc * xc, axis=-1, keepdims=True) + _LN_EPS)
    xhat = xc * rstd
    return xhat, rstd, xhat * g + b


def _conv_a_fwd(uc, w, cb, lg, lb):
    s = uc.shape[0]
    c = _GROUP_COLS

    def body(val_ref, gate_ref, hval_ref, hgate_ref, w_ref, cb_ref, lg_ref, lb_ref, o_ref, hs_ref):
        _fill_glu(val_ref, gate_ref, hval_ref, hgate_ref, hs_ref)
        for rc in range(_CONV_T // _CONV_RC):
            r0 = rc * _CONV_RC
            ca = _conv_rows(hs_ref, w_ref, r0, _CONV_RC) + cb_ref[...]
            _, _, ln = _ln_fwd(ca, lg_ref[...], lb_ref[...])
            o_ref[pl.ds(r0, _CONV_RC), :] = (ln * _sig(ln)).astype(BF16)

    val, gate, hval, hgate = _conv_a_specs(s)
    wspec = pl.BlockSpec((_CONV_K, c), lambda i: (0, 0))
    vec = pl.BlockSpec((1, c), lambda i: (0, 0))
    return _pcall(
        body, name="conv_a_fwd", grid=(s // _CONV_T,),
        in_specs=[val, gate, hval, hgate, wspec, vec, vec, vec],
        out_specs=pl.BlockSpec((_CONV_T, c), lambda i: (i, 0)),
        out_shape=jax.ShapeDtypeStruct((s, 2 * c), BF16),
        scratch=[pltpu.VMEM((_CONV_T + _HALO, c), F32)])(uc, uc, uc, uc, w, cb, lg, lb)


def _conv_a_bwd_ln(uc, dcat, w, cb, lg, lb):
    s = uc.shape[0]
    c = _GROUP_COLS

    def body(val_ref, gate_ref, hval_ref, hgate_ref, dy_ref, w_ref, cb_ref, lg_ref, lb_ref, dca_ref, st_ref, hs_ref):
        _fill_glu(val_ref, gate_ref, hval_ref, hgate_ref, hs_ref)

        @pl.when(pl.program_id(0) == 0)
        def _():
            st_ref[...] = jnp.zeros_like(st_ref)

        for rc in range(_CONV_T // _CONV_RC):
            r0 = rc * _CONV_RC
            ca = _conv_rows(hs_ref, w_ref, r0, _CONV_RC) + cb_ref[...]
            xhat, rstd, ln = _ln_fwd(ca, lg_ref[...], lb_ref[...])
            sg = _sig(ln)
            dln = dy_ref[pl.ds(r0, _CONV_RC), :] * (sg * (1.0 + ln * (1.0 - sg)))
            dxh = dln * lg_ref[...]
            dca = rstd * (dxh - jnp.mean(dxh, axis=-1, keepdims=True)
                          - xhat * jnp.mean(dxh * xhat, axis=-1, keepdims=True))
            dca_ref[pl.ds(r0, _CONV_RC), :] = dca
            st_ref[0:1, :] += jnp.sum(dca, axis=0, keepdims=True)
            st_ref[1:2, :] += jnp.sum(dln * xhat, axis=0, keepdims=True)
            st_ref[2:3, :] += jnp.sum(dln, axis=0, keepdims=True)

    val, gate, hval, hgate = _conv_a_specs(s)
    blk = pl.BlockSpec((_CONV_T, c), lambda i: (i, 0))
    wspec = pl.BlockSpec((_CONV_K, c), lambda i: (0, 0))
    vec = pl.BlockSpec((1, c), lambda i: (0, 0))
    st = pl.BlockSpec((8, c), lambda i: (0, 0))
    return _pcall(
        body, name="conv_a_bwd_ln", grid=(s // _CONV_T,),
        in_specs=[val, gate, hval, hgate, blk, wspec, vec, vec, vec], out_specs=[blk, st],
        out_shape=[jax.ShapeDtypeStruct((s, c), F32), jax.ShapeDtypeStruct((8, c), F32)],
        scratch=[pltpu.VMEM((_CONV_T + _HALO, c), F32)])(uc, uc, uc, uc, dcat, w, cb, lg, lb)


def _conv_a_bwd_conv(uc, dca, w):
    s = uc.shape[0]
    c = _GROUP_COLS
    nblk = s // _CONV_T
    hb = _CONV_T // _HALO
    off = _HALO - (_CONV_K - 1)

    def body(val_ref, gate_ref, hval_ref, hgate_ref, d_ref, dn_ref, w_ref, du_ref, dw_ref, hs_ref, ds_ref):
        i = pl.program_id(0)
        _fill_glu(val_ref, gate_ref, hval_ref, hgate_ref, hs_ref)
        ds_ref[pl.ds(0, _CONV_T), :] = d_ref[...]
        ds_ref[pl.ds(_CONV_T, _HALO), :] = jnp.where(i < nblk - 1, dn_ref[...], 0.0)

        @pl.when(i == 0)
        def _():
            dw_ref[...] = jnp.zeros_like(dw_ref)

        for rc in range(_CONV_T // _CONV_RC):
            r0 = rc * _CONV_RC
            dcur = ds_ref[pl.ds(r0, _CONV_RC), :]
            dh = jnp.zeros((_CONV_RC, c), F32)
            for kk in range(_CONV_K):
                dh = dh + w_ref[kk:kk + 1, :] * ds_ref[pl.ds(r0 + _CONV_K - 1 - kk, _CONV_RC), :]
                dw_ref[kk:kk + 1, :] += jnp.sum(dcur * hs_ref[pl.ds(r0 + off + kk, _CONV_RC), :],
                                                 axis=0, keepdims=True)
            v = val_ref[pl.ds(r0, _CONV_RC), :].astype(F32)
            sg = _sig(gate_ref[pl.ds(r0, _CONV_RC), :].astype(F32))
            du_ref[pl.ds(r0, _CONV_RC), pl.ds(0, c)] = (dh * sg).astype(BF16)
            du_ref[pl.ds(r0, _CONV_RC), pl.ds(c, c)] = (dh * v * sg * (1.0 - sg)).astype(BF16)

    val, gate, hval, hgate = _conv_a_specs(s)
    blk = pl.BlockSpec((_CONV_T, c), lambda i: (i, 0))
    nxt = pl.BlockSpec((_HALO, c), lambda i: (jnp.minimum((i + 1) * hb, s // _HALO - 1), 0))
    wspec = pl.BlockSpec((_CONV_K, c), lambda i: (0, 0))
    return _pcall(
        body, name="conv_a_bwd_conv", grid=(nblk,),
        in_specs=[val, gate, hval, hgate, blk, nxt, wspec],
        out_specs=[pl.BlockSpec((_CONV_T, 2 * c), lambda i: (i, 0)), pl.BlockSpec((_HALO, c), lambda i: (0, 0))],
        out_shape=[jax.ShapeDtypeStruct((s, 2 * c), BF16), jax.ShapeDtypeStruct((_HALO, c), F32)],
        scratch=[pltpu.VMEM((_CONV_T + _HALO, c), F32), pltpu.VMEM((_CONV_T + _HALO, c), F32)],
    )(uc, uc, uc, uc, dca, dca, w)


_SC_T = 256


def _short_conv_fwd(u2, w):
    s, d3 = u2.shape
    d = d3 // 3
    hb = _SC_T // _SC_HALO

    def body(b_ref, c_ref, v_ref, hc_ref, hv_ref, w_ref, o_ref, cs_ref):
        i = pl.program_id(0)
        cs_ref[pl.ds(_SC_HALO, _SC_T), :] = c_ref[...].astype(F32) * v_ref[...].astype(F32)
        cs_ref[pl.ds(0, _SC_HALO), :] = jnp.where(i > 0, hc_ref[...].astype(F32) * hv_ref[...].astype(F32), 0.0)
        conv = (w_ref[0:1, :] * cs_ref[pl.ds(_SC_HALO - 2, _SC_T), :]
                + w_ref[1:2, :] * cs_ref[pl.ds(_SC_HALO - 1, _SC_T), :]
                + w_ref[2:3, :] * cs_ref[pl.ds(_SC_HALO, _SC_T), :])
        o_ref[...] = (b_ref[...].astype(F32) * conv).astype(BF16)

    def col(j):
        return pl.BlockSpec((_SC_T, d), lambda i: (i, j))

    def halo(j):
        return pl.BlockSpec((_SC_HALO, d), lambda i: (jnp.maximum(i * hb - 1, 0), j))

    return _pcall(
        body, name="short_conv_fwd", grid=(s // _SC_T,),
        in_specs=[col(0), col(1), col(2), halo(1), halo(2), pl.BlockSpec((3, d), lambda i: (0, 0))],
        out_specs=pl.BlockSpec((_SC_T, d), lambda i: (i, 0)),
        out_shape=jax.ShapeDtypeStruct((s, d), BF16),
        scratch=[pltpu.VMEM((_SC_T + _SC_HALO, d), F32)])(u2, u2, u2, u2, u2, w)


def _short_conv_bwd(u2, dsc, w):
    s, d3 = u2.shape
    d = d3 // 3
    hb = _SC_T // _SC_HALO
    nblk = s // _SC_T

    def body(b_ref, c_ref, v_ref, hc_ref, hv_ref, nb_ref, d_ref, nd_ref, w_ref, du_ref, dw_ref, cs_ref, ds_ref):
        i = pl.program_id(0)
        cval, vval, bval = c_ref[...].astype(F32), v_ref[...].astype(F32), b_ref[...].astype(F32)
        cs_ref[pl.ds(_SC_HALO, _SC_T), :] = cval * vval
        cs_ref[pl.ds(0, _SC_HALO), :] = jnp.where(i > 0, hc_ref[...].astype(F32) * hv_ref[...].astype(F32), 0.0)
        dsc_cur = d_ref[...]
        dconv = dsc_cur * bval
        ds_ref[pl.ds(0, _SC_T), :] = dconv
        ds_ref[pl.ds(_SC_T, _SC_HALO), :] = jnp.where(i < nblk - 1, nd_ref[...] * nb_ref[...].astype(F32), 0.0)
        taps = [cs_ref[pl.ds(_SC_HALO - 2 + kk, _SC_T), :] for kk in range(3)]
        conv = w_ref[0:1, :] * taps[0] + w_ref[1:2, :] * taps[1] + w_ref[2:3, :] * taps[2]
        dcv = (w_ref[2:3, :] * dconv + w_ref[1:2, :] * ds_ref[pl.ds(1, _SC_T), :]
               + w_ref[0:1, :] * ds_ref[pl.ds(2, _SC_T), :])
        du_ref[:, pl.ds(0, d)] = (dsc_cur * conv).astype(BF16)
        du_ref[:, pl.ds(d, d)] = (dcv * vval).astype(BF16)
        du_ref[:, pl.ds(2 * d, d)] = (dcv * cval).astype(BF16)

        @pl.when(i == 0)
        def _():
            dw_ref[...] = jnp.zeros_like(dw_ref)

        for kk in range(3):
            dw_ref[kk:kk + 1, :] += jnp.sum(dconv * taps[kk], axis=0, keepdims=True)

    def col(j):
        return pl.BlockSpec((_SC_T, d), lambda i: (i, j))

    def halo(j):
        return pl.BlockSpec((_SC_HALO, d), lambda i: (jnp.maximum(i * hb - 1, 0), j))

    def nxt(j):
        return pl.BlockSpec((_SC_HALO, d), lambda i: (jnp.minimum((i + 1) * hb, s // _SC_HALO - 1), j))

    return _pcall(
        body, name="short_conv_bwd", grid=(nblk,),
        in_specs=[col(0), col(1), col(2), halo(1), halo(2), nxt(0), col(0), nxt(0),
                  pl.BlockSpec((3, d), lambda i: (0, 0))],
        out_specs=[pl.BlockSpec((_SC_T, d3), lambda i: (i, 0)), pl.BlockSpec((8, d), lambda i: (0, 0))],
        out_shape=[jax.ShapeDtypeStruct((s, d3), BF16), jax.ShapeDtypeStruct((8, d), F32)],
        scratch=[pltpu.VMEM((_SC_T + _SC_HALO, d), F32), pltpu.VMEM((_SC_T + _SC_HALO, d), F32)],
    )(u2, u2, u2, u2, u2, u2, dsc, dsc, w)


def _bucket_maps():
    a_idx = jnp.arange(_STEPS)[:, None]
    c_idx = jnp.arange(2 * _STEPS)[None, :]
    mdist = jnp.clip(a_idx + _STEPS - c_idx, 0, _STEPS)
    max_exact = _NUM_BUCKETS // 2
    maps = []
    for _, dil in _GROUPS:
        nn = mdist * dil
        nf = jnp.maximum(nn, 1).astype(F32)
        large = max_exact + (jnp.log(nf / max_exact) / math.log(_MAX_DISTANCE / max_exact)
                             * (_NUM_BUCKETS - max_exact)).astype(jnp.int32)
        maps.append(jnp.where(nn < max_exact, nn, jnp.minimum(large, _NUM_BUCKETS - 1)).astype(jnp.int32))
    return jnp.stack(maps, axis=0)


def _bias_expand(rel_bias, buckets):
    nh = rel_bias.shape[1]

    def body(rb_ref, bk_ref, o_ref):
        h = pl.program_id(0)
        bk = bk_ref[0]
        acc = jnp.zeros(bk.shape, F32)
        for b in range(_NUM_BUCKETS):
            acc = jnp.where(bk == b, rb_ref[b, h], acc)
        a = lax.broadcasted_iota(jnp.int32, bk.shape, 0)
        c = lax.broadcasted_iota(jnp.int32, bk.shape, 1)
        mdist = a + _STEPS - c
        o_ref[0] = jnp.where((mdist >= 0) & (mdist <= _STEPS), acc, _NEG)

    return _pcall(
        body, name="bias_expand", grid=(nh,),
        in_specs=[pl.BlockSpec(memory_space=pltpu.SMEM),
                  pl.BlockSpec((1, _STEPS, 2 * _STEPS), lambda h: (h // 8, 0, 0))],
        out_specs=pl.BlockSpec((1, _STEPS, 2 * _STEPS), lambda h: (h, 0, 0)),
        out_shape=jax.ShapeDtypeStruct((nh, _STEPS, 2 * _STEPS), F32))(rel_bias, buckets)


def _bias_reduce(ds_all, buckets):
    nh = ds_all.shape[0]

    def body(ds_ref, bk_ref, o_ref):
        t, bk = ds_ref[0], bk_ref[0]
        rows = lax.broadcasted_iota(jnp.int32, (_NUM_BUCKETS, _LANES), 0)
        out = jnp.zeros((_NUM_BUCKETS, _LANES), F32)
        for b in range(_NUM_BUCKETS):
            out = jnp.where(rows == b, jnp.sum(jnp.where(bk == b, t, 0.0)), out)
        o_ref[0] = out

    blk = pl.BlockSpec((1, _STEPS, 2 * _STEPS), lambda h: (h, 0, 0))
    return _pcall(
        body, name="bias_reduce", grid=(nh,),
        in_specs=[blk, pl.BlockSpec((1, _STEPS, 2 * _STEPS), lambda h: (h // 8, 0, 0))],
        out_specs=pl.BlockSpec((1, _NUM_BUCKETS, _LANES), lambda h: (h, 0, 0)),
        out_shape=jax.ShapeDtypeStruct((nh, _NUM_BUCKETS, _LANES), F32))(ds_all, buckets)


def _strided_rows(ref, r, dil):
    if dil == 1:
        return ref[...]
    return ref[pl.ds(r, _STEPS, stride=dil), :]


def _store_strided(ref, r, dil, val):
    if dil == 1:
        ref[...] = val
    else:
        ref[pl.ds(r, _STEPS, stride=dil), :] = val


def _head_masks():
    lane = lax.broadcasted_iota(jnp.int32, (1, _LANES), 1)
    return [lane < _HEAD_DIM, lane >= _HEAD_DIM]


def _scores(qm, k2, bias, first):
    sc = lax.dot_general(qm, k2, (((1,), (1,)), ((), ())), preferred_element_type=F32)
    sc = sc * (_HEAD_DIM ** -0.5) + bias
    col = lax.broadcasted_iota(jnp.int32, sc.shape, 1)
    return jnp.where(jnp.logical_and(first, col < _STEPS), _NEG, sc)


def _attn_fwd(uq, uk, uv, bias, g, dil, hbw):
    s = uq.shape[0]
    rb = _STEPS * dil
    nb = s // rb
    nhb = _GROUP_COLS // hbw
    cb0 = g * nhb
    hpb = hbw // _HEAD_DIM

    def body(q_ref, kc_ref, kp_ref, vc_ref, vp_ref, b_ref, o_ref, l_ref):
        n, r = pl.program_id(1), pl.program_id(2)
        first = n == 0
        q = _strided_rows(q_ref, r, dil).astype(BF16)
        kk = jnp.concatenate([_strided_rows(kp_ref, r, dil), _strided_rows(kc_ref, r, dil)], axis=0).astype(BF16)
        vv = jnp.concatenate([_strided_rows(vp_ref, r, dil), _strided_rows(vc_ref, r, dil)], axis=0).astype(BF16)
        masks = _head_masks()
        outs, lses = [], []
        for j in range(hbw // _LANES):
            sl = slice(j * _LANES, (j + 1) * _LANES)
            q2, k2, v2 = q[:, sl], kk[:, sl], vv[:, sl]
            o_pair = jnp.zeros((_STEPS, _LANES), F32)
            l_pair = jnp.zeros((_STEPS, _LANES), F32)
            for hh in range(2):
                mk = masks[hh]
                sc = _scores(jnp.where(mk, q2, 0), k2, b_ref[2 * j + hh], first)
                mx = jnp.max(sc, axis=-1, keepdims=True)
                p = jnp.exp(sc - mx)
                den = jnp.sum(p, axis=-1, keepdims=True)
                oh = jnp.dot(p.astype(BF16), jnp.where(mk, v2, 0), preferred_element_type=F32)
                o_pair = o_pair + oh / den
                l_pair = jnp.where(mk, mx + jnp.log(den), l_pair)
            outs.append(o_pair)
            lses.append(l_pair)
        _store_strided(o_ref, r, dil, jnp.concatenate(outs, axis=1) if len(outs) > 1 else outs[0])
        _store_strided(l_ref, r, dil, jnp.concatenate(lses, axis=1) if len(lses) > 1 else lses[0])

    cur = pl.BlockSpec((rb, hbw), lambda hb, n, r: (n, cb0 + hb))
    prev = pl.BlockSpec((rb, hbw), lambda hb, n, r: (jnp.maximum(n - 1, 0), cb0 + hb))
    bspec = pl.BlockSpec((hpb, _STEPS, 2 * _STEPS), lambda hb, n, r: (g * nhb + hb, 0, 0))
    ospec = pl.BlockSpec((rb, hbw), lambda hb, n, r: (n, hb))
    return _pcall(
        body, name=f"attn_fwd_g{g}", grid=(nhb, nb, dil),
        in_specs=[cur, cur, prev, cur, prev, bspec], out_specs=[ospec, ospec],
        out_shape=[jax.ShapeDtypeStruct((s, _GROUP_COLS), F32), jax.ShapeDtypeStruct((s, _GROUP_COLS), F32)],
    )(uq, uk, uk, uv, uv, bias)


def _attn_merge(outs, lses, cat):
    s, c = outs[0].shape

    def body(o0, o1, o2, l0, l1, l2, cat_in, cat_ref, lse_ref):
        del cat_in
        a0, a1, a2 = l0[...], l1[...], l2[...]
        mx = jnp.maximum(jnp.maximum(a0, a1), a2)
        w0, w1, w2 = jnp.exp(a0 - mx), jnp.exp(a1 - mx), jnp.exp(a2 - mx)
        den = w0 + w1 + w2
        cat_ref[...] = ((w0 * o0[...] + w1 * o1[...] + w2 * o2[...]) / den).astype(BF16)
        lse_ref[...] = mx + jnp.log(den)

    blk = pl.BlockSpec((_ROW_T, c), lambda i: (i, 0))
    return _pcall(
        body, name="attn_merge", grid=(s // _ROW_T,),
        in_specs=[blk] * 6 + [_ANY],
        out_specs=[pl.BlockSpec((_ROW_T, c), lambda i: (i, 1)), blk],
        out_shape=[jax.ShapeDtypeStruct(cat.shape, BF16), jax.ShapeDtypeStruct((s, c), F32)],
        aliases={6: 0})(*outs, *lses, cat)


def _attn_delta(dcat, cat):
    s = dcat.shape[0]
    c = _GROUP_COLS
    seg = (jnp.arange(c)[:, None] // _HEAD_DIM == jnp.arange(c)[None, :] // _HEAD_DIM).astype(BF16)

    def body(dy_ref, y_ref, seg_ref, o_ref):
        prod = dy_ref[...] * y_ref[...].astype(F32)
        hi = prod.astype(BF16)
        lo = (prod - hi.astype(F32)).astype(BF16)
        o_ref[...] = (jnp.dot(hi, seg_ref[...], preferred_element_type=F32)
                      + jnp.dot(lo, seg_ref[...], preferred_element_type=F32))

    right = pl.BlockSpec((_ROW_T, c), lambda i: (i, 1))
    return _pcall(
        body, name="attn_delta", grid=(s // _ROW_T,),
        in_specs=[right, right, pl.BlockSpec((c, c), lambda i: (0, 0))],
        out_specs=pl.BlockSpec((_ROW_T, c), lambda i: (i, 0)),
        out_shape=jax.ShapeDtypeStruct((s, c), F32))(dcat, cat, seg)


def _attn_bwd(uq, uk, uv, dcat, lse, delta, bias, prev_grads, g, dil, hbw):
    s = uq.shape[0]
    rb = _STEPS * dil
    nb = s // rb
    nhb = _GROUP_COLS // hbw
    cb0 = g * nhb
    hpb = hbw // _HEAD_DIM
    dy0 = _GROUP_COLS // hbw
    scale = _HEAD_DIM ** -0.5

    def body(q_ref, kc_ref, kp_ref, vc_ref, vp_ref, dy_ref, l_ref, dl_ref, b_ref, *rest):
        rest = rest[len(prev_grads):]
        dq_ref, dk_ref, dv_ref, dsa_ref, dkc_ref, dvc_ref = rest
        n, r = pl.program_id(1), pl.program_id(2)

        @pl.when(jnp.logical_and(n == 0, r == 0))
        def _():
            dsa_ref[...] = jnp.zeros_like(dsa_ref)

        @pl.when(n == 0)
        def _():
            dkc_ref[r] = jnp.zeros((_STEPS, hbw), F32)
            dvc_ref[r] = jnp.zeros((_STEPS, hbw), F32)

        @pl.when(n < nb)
        def _():
            first = n == 0
            q = _strided_rows(q_ref, r, dil).astype(BF16)
            kk = jnp.concatenate([_strided_rows(kp_ref, r, dil), _strided_rows(kc_ref, r, dil)],
                                 axis=0).astype(BF16)
            vv = jnp.concatenate([_strided_rows(vp_ref, r, dil), _strided_rows(vc_ref, r, dil)],
                                 axis=0).astype(BF16)
            dy = _strided_rows(dy_ref, r, dil).astype(BF16)
            lse_v = _strided_rows(l_ref, r, dil)
            dl_v = _strided_rows(dl_ref, r, dil)
            masks = _head_masks()
            dqs, dks, dvs = [], [], []
            for j in range(hbw // _LANES):
                sl = slice(j * _LANES, (j + 1) * _LANES)
                q2, k2, v2, dy2 = q[:, sl], kk[:, sl], vv[:, sl], dy[:, sl]
                dq_p = jnp.zeros((_STEPS, _LANES), F32)
                dk_p = jnp.zeros((2 * _STEPS, _LANES), F32)
                dv_p = jnp.zeros((2 * _STEPS, _LANES), F32)
                for hh in range(2):
                    mk = masks[hh]
                    lane0 = j * _LANES + hh * _HEAD_DIM
                    qm, km, dym = jnp.where(mk, q2, 0), jnp.where(mk, k2, 0), jnp.where(mk, dy2, 0)
                    sc = _scores(qm, k2, b_ref[2 * j + hh], first)
                    p = jnp.exp(sc - lse_v[:, lane0:lane0 + 1])
                    dp = lax.dot_general(dym, v2, (((1,), (1,)), ((), ())), preferred_element_type=F32)
                    ds = p * (dp - dl_v[:, lane0:lane0 + 1])
                    dsa_ref[2 * j + hh] += ds
                    dsb = ds.astype(BF16)
                    dq_p = dq_p + jnp.dot(dsb, km, preferred_element_type=F32)
                    dk_p = dk_p + lax.dot_general(dsb, qm, (((0,), (0,)), ((), ())), preferred_element_type=F32)
                    dv_p = dv_p + lax.dot_general(p.astype(BF16), dym, (((0,), (0,)), ((), ())),
                                                  preferred_element_type=F32)
                dqs.append(dq_p * scale)
                dks.append(dk_p * scale)
                dvs.append(dv_p)
            cat = lambda xs: jnp.concatenate(xs, axis=1) if len(xs) > 1 else xs[0]
            _store_strided(dq_ref, r, dil, cat(dqs))
            dk_new, dv_new = cat(dks), cat(dvs)
            _store_strided(dk_ref, r, dil, dkc_ref[r] + dk_new[:_STEPS])
            _store_strided(dv_ref, r, dil, dvc_ref[r] + dv_new[:_STEPS])
            dkc_ref[r] = dk_new[_STEPS:]
            dvc_ref[r] = dv_new[_STEPS:]

        @pl.when(n == nb)
        def _():
            _store_strided(dk_ref, r, dil, dkc_ref[r])
            _store_strided(dv_ref, r, dil, dvc_ref[r])

    def clamp(n):
        return jnp.minimum(n, nb - 1)

    cur = pl.BlockSpec((rb, hbw), lambda hb, n, r: (clamp(n), cb0 + hb))
    prev = pl.BlockSpec((rb, hbw), lambda hb, n, r: (jnp.maximum(clamp(n) - 1, 0), cb0 + hb))
    dyspec = pl.BlockSpec((rb, hbw), lambda hb, n, r: (clamp(n), dy0 + hb))
    stat = pl.BlockSpec((rb, hbw), lambda hb, n, r: (clamp(n), hb))
    bspec = pl.BlockSpec((hpb, _STEPS, 2 * _STEPS), lambda hb, n, r: (g * nhb + hb, 0, 0))
    dqspec = cur
    dkspec = pl.BlockSpec((rb, hbw), lambda hb, n, r: (jnp.maximum(n - 1, 0), cb0 + hb))
    dsspec = pl.BlockSpec((hpb, _STEPS, 2 * _STEPS), lambda hb, n, r: (hb, 0, 0))
    wide = jax.ShapeDtypeStruct((s, 3 * _GROUP_COLS), F32)
    np_ = len(prev_grads)
    return _pcall(
        body, name=f"attn_bwd_g{g}", grid=(nhb, nb + 1, dil),
        in_specs=[cur, cur, prev, cur, prev, dyspec, stat, stat, bspec] + [_ANY] * np_,
        out_specs=[dqspec, dkspec, dkspec, dsspec],
        out_shape=[wide, wide, wide, jax.ShapeDtypeStruct((8, _STEPS, 2 * _STEPS), F32)],
        scratch=[pltpu.VMEM((dil, _STEPS, hbw), F32), pltpu.VMEM((dil, _STEPS, hbw), F32)],
        aliases={9 + t: t for t in range(np_)},
    )(uq, uk, uk, uv, uv, dcat, lse, delta, bias, *prev_grads)


def _place():
    x, y, c = lax.axis_index("x"), lax.axis_index("y"), lax.axis_index("c")
    chips = [(1 - x, y), (x, 1 - y), (1 - x, 1 - y)]
    return x, y, c, chips


def _slab(ref, axis, chip, width):
    start = pl.multiple_of(chip * width, width)
    if axis == 0:
        return ref.at[pl.ds(start, width), :]
    return ref.at[:, pl.ds(start, width)]


def _gather_weights(shards, axes):
    nw = len(shards)
    fulls = []
    for sh, ax in zip(shards, axes):
        shape = list(sh.shape)
        shape[ax] *= 4
        fulls.append(jax.ShapeDtypeStruct(tuple(shape), sh.dtype))

    def body(*refs):
        ins, outs = refs[:nw], refs[nw:2 * nw]
        send, recv, loc = refs[2 * nw:]
        x, y, c, chips = _place()
        mine = 2 * x + y
        started = []
        for t in range(nw):
            width = ins[t].shape[axes[t]]
            own = pltpu.make_async_copy(ins[t], _slab(outs[t], axes[t], mine, width), loc.at[t])
            own.start()
            started.append(own)
            for j, (px, py) in enumerate(chips):
                cp = pltpu.make_async_remote_copy(
                    src_ref=ins[t], dst_ref=_slab(outs[t], axes[t], mine, width),
                    send_sem=send.at[3 * t + j], recv_sem=recv.at[3 * t + j],
                    device_id=(px, py, c), device_id_type=MESH)
                cp.start()
                started.append(cp)
        for t in range(nw):
            width = ins[t].shape[axes[t]]
            for j, (px, py) in enumerate(chips):
                pltpu.make_async_remote_copy(
                    src_ref=ins[t], dst_ref=_slab(outs[t], axes[t], 2 * px + py, width),
                    send_sem=send.at[3 * t + j], recv_sem=recv.at[3 * t + j],
                    device_id=(px, py, c), device_id_type=MESH).wait_recv()
        for t in range(nw):
            started[4 * t].wait()
            for j in range(3):
                started[4 * t + 1 + j].wait_send()

    return _pcall(
        body, name="gather_weights", in_specs=[_ANY] * nw, out_specs=[_ANY] * nw, out_shape=fulls,
        scratch=[pltpu.SemaphoreType.DMA((3 * nw,)), pltpu.SemaphoreType.DMA((3 * nw,)),
                 pltpu.SemaphoreType.DMA((nw,))])(*shards)


def _scatter_grads(grads, axes):
    nw = len(grads)
    outs_shape = []
    for gr, ax in zip(grads, axes):
        shape = list(gr.shape)
        shape[ax] //= 4
        outs_shape.append(jax.ShapeDtypeStruct((3,) + tuple(shape), gr.dtype))

    def body(*refs):
        ins, outs = refs[:nw], refs[nw:2 * nw]
        send, recv = refs[2 * nw:]
        x, y, c, chips = _place()
        started = []
        for t in range(nw):
            width = ins[t].shape[axes[t]] // 4
            for j, (px, py) in enumerate(chips):
                cp = pltpu.make_async_remote_copy(
                    src_ref=_slab(ins[t], axes[t], 2 * px + py, width), dst_ref=outs[t].at[j],
                    send_sem=send.at[3 * t + j], recv_sem=recv.at[3 * t + j],
                    device_id=(px, py, c), device_id_type=MESH)
                cp.start()
                started.append(cp)
        for cp in started:
            cp.wait_recv()
        for cp in started:
            cp.wait_send()

    return _pcall(
        body, name="scatter_grads", in_specs=[_ANY] * nw, out_specs=[_ANY] * nw, out_shape=outs_shape,
        scratch=[pltpu.SemaphoreType.DMA((3 * nw,)), pltpu.SemaphoreType.DMA((3 * nw,))])(*grads)


def _swap_sibling(parts):
    nw = len(parts)

    def body(*refs):
        ins, outs = refs[:nw], refs[nw:2 * nw]
        send, recv = refs[2 * nw:]
        x, y, c, _ = _place()
        started = []
        for t in range(nw):
            cp = pltpu.make_async_remote_copy(
                src_ref=ins[t], dst_ref=outs[t], send_sem=send.at[t], recv_sem=recv.at[t],
                device_id=(x, y, 1 - c), device_id_type=MESH)
            cp.start()
            started.append(cp)
        for cp in started:
            cp.wait_recv()
        for cp in started:
            cp.wait_send()

    return _pcall(
        body, name="swap_sibling", in_specs=[_ANY] * nw, out_specs=[_ANY] * nw,
        out_shape=[jax.ShapeDtypeStruct(p.shape, p.dtype) for p in parts],
        scratch=[pltpu.SemaphoreType.DMA((nw,)), pltpu.SemaphoreType.DMA((nw,))])(*parts)


def _sum_all_devices(buf, name):
    rows, cols = buf.shape

    def body(in_ref, o_ref, gat_ref, send, recv):
        x, y, c, _ = _place()
        me = 4 * x + 2 * y + c
        gat_ref[me] = in_ref[...]
        started = []
        for mask in range(1, 8):
            fx, fy, fc = (mask >> 2) & 1, (mask >> 1) & 1, mask & 1
            peer = (x + fx * (1 - 2 * x), y + fy * (1 - 2 * y), c + fc * (1 - 2 * c))
            cp = pltpu.make_async_remote_copy(
                src_ref=in_ref, dst_ref=gat_ref.at[me], send_sem=send.at[mask - 1], recv_sem=recv.at[mask - 1],
                device_id=peer, device_id_type=MESH)
            cp.start()
            started.append(cp)
        for cp in started:
            cp.wait_recv()
        for cp in started:
            cp.wait_send()
        acc = gat_ref[0]
        for t in range(1, 8):
            acc = acc + gat_ref[t]
        o_ref[...] = acc

    vm = pl.BlockSpec(memory_space=pltpu.VMEM)
    return _pcall(
        body, name=name, in_specs=[vm], out_specs=vm, out_shape=jax.ShapeDtypeStruct((rows, cols), F32),
        scratch=[pltpu.VMEM((8, rows, cols), F32), pltpu.SemaphoreType.DMA((7,)), pltpu.SemaphoreType.DMA((7,))],
    )(buf)


_UPD_T = 256


def _sum_partials(own, got, name):
    rows, cols = own.shape
    tr = min(_UPD_T, rows)

    def body(own_ref, got_ref, o_ref):
        acc = own_ref[...].astype(F32)
        for j in range(3):
            acc = acc + got_ref[j].astype(F32)
        o_ref[...] = acc

    blk = pl.BlockSpec((tr, cols), lambda i: (i, 0))
    return _pcall(
        body, name=name, grid=(rows // tr,),
        in_specs=[blk, pl.BlockSpec((3, tr, cols), lambda i: (0, i, 0))], out_specs=blk,
        out_shape=jax.ShapeDtypeStruct((rows, cols), F32))(own, got)


def _adamw_math(w, gr, m, v):
    m = _B1 * m + (1.0 - _B1) * gr
    v = _B2 * v + (1.0 - _B2) * (gr * gr)
    m_hat = m / (1.0 - _B1 ** _STEP)
    v_hat = v / (1.0 - _B2 ** _STEP)
    delta = -_LR * (m_hat / (jnp.sqrt(v_hat) + _EPS) + _WD * w)
    return delta, m, v


def _adamw(w, m, v, parts, name):
    rows, cols = w.shape
    tr = min(_UPD_T, rows)
    npart = len(parts)

    def body(w_ref, m_ref, v_ref, *rest):
        p_refs, (g_ref, d_ref, nm_ref, nv_ref) = rest[:npart], rest[npart:]
        gr = p_refs[0][...]
        for p in p_refs[1:]:
            gr = gr + p[...]
        delta, nm, nv = _adamw_math(w_ref[...], gr, m_ref[...], v_ref[...])
        g_ref[...] = gr
        d_ref[...] = delta
        nm_ref[...] = nm
        nv_ref[...] = nv

    blk = pl.BlockSpec((tr, cols), lambda i: (i, 0))
    sh = jax.ShapeDtypeStruct((rows, cols), F32)
    return _pcall(body, name=name, grid=(rows // tr,), in_specs=[blk] * (3 + npart), out_specs=[blk] * 4,
                  out_shape=[sh] * 4)(w, m, v, *parts)


_PACK_W = 1024


def _pack(arrs, rows):
    flat = []
    for a in arrs:
        f = a.reshape(-1).astype(F32)
        pad = (-f.shape[0]) % _PACK_W
        flat.append(jnp.pad(f, (0, pad)))
    f = jnp.concatenate(flat)
    f = jnp.pad(f, (0, rows * _PACK_W - f.shape[0]))
    return f.reshape(rows, _PACK_W)


def _unpack(buf, shapes):
    flat = buf.reshape(-1)
    out, pos = [], 0
    for sh in shapes:
        size = math.prod(sh)
        out.append(flat[pos:pos + size].reshape(sh))
        pos += size + ((-size) % _PACK_W)
    return out


def _pack_rows(shapes):
    total = sum(-(-math.prod(sh) // _PACK_W) for sh in shapes)
    return -(-total // 8) * 8


def kernel(x, rel_bias, ab_norm, ab_w_in, ab_conv_w, ab_conv_b, ab_ln_g, ab_ln_b, ab_w_out, sc_norm, sc_w_in, sc_conv_w, sc_w_out, mlp_norm, mlp_w_up, mlp_w_down, final_norm, loss_target, m_rel_bias, m_ab_norm, m_ab_w_in, m_ab_conv_w, m_ab_conv_b, m_ab_ln_g, m_ab_ln_b, m_ab_w_out, m_sc_norm, m_sc_w_in, m_sc_conv_w, m_sc_w_out, m_mlp_norm, m_mlp_w_up, m_mlp_w_down, m_final_norm, v_rel_bias, v_ab_norm, v_ab_w_in, v_ab_conv_w, v_ab_conv_b, v_ab_ln_g, v_ab_ln_b, v_ab_w_out, v_sc_norm, v_sc_w_in, v_sc_conv_w, v_sc_w_out, v_mlp_norm, v_mlp_w_up, v_mlp_w_down, v_final_norm):
    s, d = x.shape[1], x.shape[2]
    dff = 4 * d
    c = _GROUP_COLS
    chip = 2 * lax.axis_index("x") + lax.axis_index("y")
    on_c0 = (lax.axis_index("c") == 0).astype(F32)
    h0 = x[0]
    tgt = loss_target[0]

    cw_sh, scn_sh, scw_sh = ab_conv_w[0], sc_norm, sc_conv_w[0]
    conv_w_full = lax.dynamic_update_slice(jnp.zeros((_CONV_K, c), F32), cw_sh * on_c0, (0, chip * cw_sh.shape[1]))
    scn_full = lax.dynamic_update_slice(jnp.zeros((1, d), F32), scn_sh * on_c0, (0, chip * scn_sh.shape[1]))
    scw_full = lax.dynamic_update_slice(jnp.zeros((3, d), F32), scw_sh * on_c0, (0, chip * scw_sh.shape[1]))
    small_shapes = [(_CONV_K, c), (1, d), (3, d)]
    small = _sum_all_devices(_pack([conv_w_full, scn_full, scw_full], _pack_rows(small_shapes)), "gather_small")
    conv_w, sc_g, sc_cw = _unpack(small, small_shapes)

    w_shards = [ab_w_in[0], ab_w_out[0], sc_w_in[0], sc_w_out[0], mlp_w_up[0], mlp_w_up[1],
                mlp_w_down[0], mlp_w_down[1]]
    w_axes = [1, 0, 1, 0, 1, 1, 0, 0]
    w_in, w_out, w_si, w_so, w_up0, w_up1, w_dn0, w_dn1 = _gather_weights(
        [w.astype(BF16) for w in w_shards], w_axes)
    w_up, w_dn = [w_up0, w_up1], [w_dn0, w_dn1]

    buckets = _bucket_maps()
    bias = _bias_expand(rel_bias, buckets)
    n0 = _rms_fwd(h0, ab_norm, "rms_fwd_ab")
    tm = 512
    uc = _mm(n0, w_in, "nn", m=s, n=2 * c, k=d, tm=tm, tn=c, tk=d, out_dtype=BF16, name="proj_conv")
    uq, uk, uv = [
        _mm(n0, w_in, "nn", m=s, n=3 * c, k=d, tm=tm, tn=c, tk=d, out_dtype=F32, name=f"proj_{nm}",
            b_off=(0, 2 + 3 * t))
        for t, nm in enumerate("qkv")]
    cat = _conv_a_fwd(uc, conv_w, ab_conv_b, ab_ln_g, ab_ln_b)
    outs, lses = [], []
    for g, (_, dil) in enumerate(_GROUPS):
        o, l = _attn_fwd(uq, uk, uv, bias, g, dil, _LANES)
        outs.append(o)
        lses.append(l)
    cat, lse = _attn_merge(outs, lses, cat)
    h1 = _mm(cat, w_out, "nn", m=s, n=d, k=d, tm=tm, tn=d, tk=d, out_dtype=F32, name="out_ab",
             epi=_epi_add, extras=(h0,))

    def mlp_fwd(h, layer):
        nrm = _rms_fwd(h, mlp_norm[layer:layer + 1], f"rms_fwd_mlp{layer}")
        act = _mm(nrm, w_up[layer], "nn", m=s, n=dff, k=d, tm=tm, tn=1024, tk=d, out_dtype=BF16,
                  name=f"mlp_up{layer}", epi=_epi_relu_sq)
        hn = _mm(act, w_dn[layer], "nn", m=s, n=d, k=dff, tm=tm, tn=d, tk=1024, out_dtype=F32,
                 name=f"mlp_down{layer}", epi=_epi_add, extras=(h,))
        return nrm, act, hn

    n1, act0, h2 = mlp_fwd(h1, 0)
    n2 = _rms_fwd(h2, sc_g, "rms_fwd_sc")
    u2 = _mm(n2, w_si, "nn", m=s, n=3 * d, k=d, tm=tm, tn=1024, tk=d, out_dtype=BF16, name="proj_sc")
    scv = _short_conv_fwd(u2, sc_cw)
    h3 = _mm(scv, w_so, "nn", m=s, n=d, k=d, tm=tm, tn=d, tk=d, out_dtype=F32, name="out_sc",
             epi=_epi_add, extras=(h2,))
    n3, act1, h4 = mlp_fwd(h3, 1)

    dh4, dh4b, g_final, loss_part = _loss_head(h4, tgt, final_norm.reshape(1, d))

    def mlp_bwd(dh, dhb, h, nrm, act, layer):
        dz = _mm(dhb, w_dn[layer], "nt", m=s, n=dff, k=d, tm=tm, tn=1024, tk=d, out_dtype=BF16,
                 name=f"mlp_down{layer}_dx", epi=_epi_relu_sq_bwd, extras=(act,))
        g_dn = _mm(act, dhb, "tn", m=dff, n=d, k=s, tm=1024, tn=d, tk=512, out_dtype=BF16,
                   name=f"mlp_down{layer}_dw")
        g_up = _mm(nrm, dz, "tn", m=d, n=dff, k=s, tm=d, tn=1024, tk=512, out_dtype=BF16,
                   name=f"mlp_up{layer}_dw")
        dn = _mm(dz, w_up[layer], "nt", m=s, n=d, k=dff, tm=tm, tn=d, tk=1024, out_dtype=F32,
                 name=f"mlp_up{layer}_dx")
        dh_n, dhb_n, g_norm = _rms_bwd(dn, h, mlp_norm[layer:layer + 1], dh, f"rms_bwd_mlp{layer}")
        return dh_n, dhb_n, g_norm, g_up, g_dn

    dh3, dh3b, g_mn1, g_up1, g_dn1 = mlp_bwd(dh4, dh4b, h3, n3, act1, 1)

    dsc = _mm(dh3b, w_so, "nt", m=s, n=d, k=d, tm=tm, tn=d, tk=d, out_dtype=F32, name="out_sc_dx")
    g_so = _mm(scv, dh3b, "tn", m=d, n=d, k=s, tm=d, tn=d, tk=512, out_dtype=BF16, name="out_sc_dw")
    du2, g_sccw8 = _short_conv_bwd(u2, dsc, sc_cw)
    g_si = _mm(n2, du2, "tn", m=d, n=3 * d, k=s, tm=d, tn=1024, tk=512, out_dtype=BF16, name="proj_sc_dw")
    dn2 = _mm(du2, w_si, "nt", m=s, n=d, k=3 * d, tm=tm, tn=d, tk=1024, out_dtype=F32, name="proj_sc_dx")
    dh2, dh2b, g_scn = _rms_bwd(dn2, h2, sc_g, dh3, "rms_bwd_sc")

    dh1, dh1b, g_mn0, g_up0, g_dn0 = mlp_bwd(dh2, dh2b, h1, n1, act0, 0)

    dcat = _mm(dh1b, w_out, "nt", m=s, n=d, k=d, tm=tm, tn=d, tk=d, out_dtype=F32, name="out_ab_dx")
    g_out = _mm(cat, dh1b, "tn", m=d, n=d, k=s, tm=d, tn=d, tk=512, out_dtype=BF16, name="out_ab_dw")
    dca, conv_stats = _conv_a_bwd_ln(uc, dcat, conv_w, ab_conv_b, ab_ln_g, ab_ln_b)
    duc, g_cw32 = _conv_a_bwd_conv(uc, dca, conv_w)
    delta = _attn_delta(dcat, cat)
    grads_qkv, ds_list = [], []
    for g, (_, dil) in enumerate(_GROUPS):
        dq, dk, dv, dsa = _attn_bwd(uq, uk, uv, dcat, lse, delta, bias, grads_qkv, g, dil, _LANES)
        grads_qkv = [dq, dk, dv]
        ds_list.append(dsa)
    g_bias = _bias_reduce(jnp.concatenate(ds_list, axis=0), buckets)[:, :, 0].T

    secs = [(duc, 2 * c, 0)] + [(grads_qkv[t], 3 * c, 2 + 3 * t) for t in range(3)]
    g_in_parts, dn0 = [], None
    for t, (du, width, off) in enumerate(secs):
        g_in_parts.append(_mm(n0, du, "tn", m=d, n=width, k=s, tm=d, tn=c, tk=512, out_dtype=BF16,
                              name=f"proj_ab_dw{t}"))
        dn0 = _mm(du, w_in, "nt", m=s, n=d, k=width, tm=tm, tn=d, tk=c, out_dtype=F32, name=f"proj_ab_dx{t}",
                  b_off=(0, off), **({} if dn0 is None else dict(epi=_epi_add, extras=(dn0,))))
    g_in = jnp.concatenate(g_in_parts, axis=1)
    grad_x, _, g_abn = _rms_bwd(dn0, h0, ab_norm, dh1, "rms_bwd_ab")

    big_grads = [g_in, g_out, g_si, g_so, g_up0, g_up1, g_dn0, g_dn1]
    got = _scatter_grads(big_grads, w_axes)
    sums = []
    for t, (gr, ax) in enumerate(zip(big_grads, w_axes)):
        width = gr.shape[ax] // 4
        own = lax.dynamic_slice_in_dim(gr, chip * width, width, axis=ax)
        sums.append(_sum_partials(own, got[t], f"sum_partials{t}"))
    sib = _swap_sibling(sums)
    big_m = [m_ab_w_in[0], m_ab_w_out[0], m_sc_w_in[0], m_sc_w_out[0], m_mlp_w_up[0], m_mlp_w_up[1],
             m_mlp_w_down[0], m_mlp_w_down[1]]
    big_v = [v_ab_w_in[0], v_ab_w_out[0], v_sc_w_in[0], v_sc_w_out[0], v_mlp_w_up[0], v_mlp_w_up[1],
             v_mlp_w_down[0], v_mlp_w_down[1]]
    upd = [_adamw(w_shards[t], big_m[t], big_v[t], [sums[t], sib[t]], f"adamw{t}") for t in range(8)]

    full_shapes = [(_NUM_BUCKETS, rel_bias.shape[1]), (1, d), (_CONV_K, c), (1, c), (1, c), (1, c), (1, d),
                   (3, d), (2, d), (d,)]
    small_grads = [g_bias, g_abn, g_cw32[:_CONV_K], conv_stats[0:1], conv_stats[1:2], conv_stats[2:3], g_scn,
                   g_sccw8[:3], jnp.concatenate([g_mn0, g_mn1], axis=0), g_final.reshape(d)]
    tot = _unpack(_sum_all_devices(_pack(small_grads, _pack_rows(full_shapes)), "sum_small"), full_shapes)
    for idx, sh in ((2, cw_sh), (6, scn_sh), (7, scw_sh)):
        width = sh.shape[1]
        tot[idx] = lax.dynamic_slice_in_dim(tot[idx], chip * width, width, axis=1)
    sm_w = [rel_bias, ab_norm, cw_sh, ab_conv_b, ab_ln_g, ab_ln_b, scn_sh, scw_sh, mlp_norm, final_norm]
    sm_m = [m_rel_bias, m_ab_norm, m_ab_conv_w[0], m_ab_conv_b, m_ab_ln_g, m_ab_ln_b, m_sc_norm, m_sc_conv_w[0],
            m_mlp_norm, m_final_norm]
    sm_v = [v_rel_bias, v_ab_norm, v_ab_conv_w[0], v_ab_conv_b, v_ab_ln_g, v_ab_ln_b, v_sc_norm, v_sc_conv_w[0],
            v_mlp_norm, v_final_norm]
    sh_shapes = [tuple(t.shape) for t in tot]
    rows = _pack_rows(sh_shapes)
    sm_upd = _adamw(_pack(sm_w, rows), _pack(sm_m, rows), _pack(sm_v, rows), [_pack(tot, rows)], "adamw_small")
    sm_g, sm_d, sm_nm, sm_nv = [_unpack(buf, sh_shapes) for buf in sm_upd]

    loss = lax.psum(loss_part[0, 0], ("x", "y", "c"))

    def assemble(big, sm):
        up = jnp.stack([big[4], big[5]], axis=0)
        dn = jnp.stack([big[6], big[7]], axis=0)
        return [sm[0], sm[1], big[0][None], sm[2][None], sm[3], sm[4], sm[5], big[1][None], sm[6], big[2][None],
                sm[7][None], big[3][None], sm[8], up, dn, sm[9]]

    res = [loss, grad_x[None]]
    for kind, sm in enumerate((sm_g, sm_d, sm_nm, sm_nv)):
        res += assemble([u[kind] for u in upd], sm)
    return tuple(res)
```

```python
import functools
import math

import jax
import jax.numpy as jnp
from jax import lax
from jax.experimental import pallas as pl
from jax.experimental.pallas import tpu as pltpu

F32 = jnp.float32
BF16 = jnp.bfloat16
MESH = pl.DeviceIdType.MESH

_GROUPS = ((128, 1), (512, 4), (2048, 16))
_STEPS = 128
_HEAD_DIM = 64
_GROUP_COLS = 512
_NUM_BUCKETS = 32
_MAX_DISTANCE = 2048
_CONV_K = 31
_HALO = 32
_SC_HALO = 16
_RMS_EPS = 1e-6
_LN_EPS = 1e-5
_NEG = -1e30
_LANES = 128
_VMEM_LIMIT = 56 * 1024 * 1024

_LR, _B1, _B2, _EPS, _WD, _STEP = 0.001, 0.9, 0.999, 1e-08, 0.01, 10


class _Rider:
    def __init__(self, ins, out_shapes, scratch, start, finish):
        self.ins, self.out_shapes, self.scratch = list(ins), list(out_shapes), list(scratch)
        self.start, self.finish = start, finish


def _pcall(body, *, name, out_shape, in_specs, out_specs, grid=None, scratch=(), aliases=None, rider=None):
    kw = {} if grid is None else {"grid": grid}
    cparams = pltpu.CompilerParams(vmem_limit_bytes=_VMEM_LIMIT)
    if rider is None:
        return pl.pallas_call(
            body, name=name, out_shape=out_shape, in_specs=in_specs, out_specs=out_specs,
            scratch_shapes=list(scratch), input_output_aliases=aliases or {},
            compiler_params=cparams, **kw)
    single = not isinstance(out_specs, (list, tuple))
    ospecs = [out_specs] if single else list(out_specs)
    oshapes = [out_shape] if single else list(out_shape)
    nin, nout, nscr = len(in_specs), len(ospecs), len(scratch)
    rin, rout = len(rider.ins), len(rider.out_shapes)

    def wrapped(*refs):
        h_in, r_in = refs[:nin], refs[nin:nin + rin]
        p = nin + rin
        h_out, r_out = refs[p:p + nout], refs[p + nout:p + nout + rout]
        p += nout + rout
        h_scr, r_scr = refs[p:p + nscr], refs[p + nscr:]
        ids = [pl.program_id(a) for a in range(len(grid))]
        first = functools.reduce(jnp.logical_and, [i == 0 for i in ids])
        last = functools.reduce(jnp.logical_and, [i == g - 1 for i, g in zip(ids, grid)])

        @pl.when(first)
        def _():
            rider.start(r_in, r_out, r_scr)

        body(*h_in, *h_out, *h_scr)

        @pl.when(last)
        def _():
            rider.finish(r_in, r_out, r_scr)

    call = pl.pallas_call(
        wrapped, name=name, out_shape=oshapes + rider.out_shapes,
        in_specs=list(in_specs) + [_ANY] * rin, out_specs=ospecs + [_ANY] * rout,
        scratch_shapes=list(scratch) + rider.scratch, input_output_aliases=aliases or {},
        compiler_params=cparams, **kw)

    def run(*operands):
        res = call(*operands, *rider.ins)
        host = res[0] if single else list(res[:nout])
        return host, list(res[nout:])

    return run


def _sig(x):
    return 1.0 / (1.0 + jnp.exp(-x))


_ANY = pl.BlockSpec(memory_space=pl.ANY)


def _mm(a, b, mode, *, m, n, k, tm, tn, tk, out_dtype, name, epi=None, extras=(), b_off=(0, 0), rider=None):
    nk = k // tk
    assert m % tm == 0 and n % tn == 0 and k % tk == 0
    o0, o1 = b_off
    if mode == "nn":
        a_spec = pl.BlockSpec((tm, tk), lambda i, j, kk: (i, kk))
        b_spec = pl.BlockSpec((tk, tn), lambda i, j, kk: (kk + o0, j + o1))
        dn = (((1,), (0,)), ((), ()))
    elif mode == "nt":
        a_spec = pl.BlockSpec((tm, tk), lambda i, j, kk: (i, kk))
        b_spec = pl.BlockSpec((tn, tk), lambda i, j, kk: (j + o0, kk + o1))
        dn = (((1,), (1,)), ((), ()))
    else:
        a_spec = pl.BlockSpec((tk, tm), lambda i, j, kk: (kk, i))
        b_spec = pl.BlockSpec((tk, tn), lambda i, j, kk: (kk + o0, j + o1))
        dn = (((0,), (0,)), ((), ()))
    o_spec = pl.BlockSpec((tm, tn), lambda i, j, kk: (i, j))
    ne = len(extras)
    multi = isinstance(out_dtype, tuple)
    dts = out_dtype if multi else (out_dtype,)
    no = len(dts)

    def body(a_ref, b_ref, *rest):
        ex, o_refs = rest[:ne], rest[ne:ne + no]
        av, bv = a_ref[...], b_ref[...]
        if av.dtype != BF16:
            av = av.astype(BF16)
        if bv.dtype != BF16:
            bv = bv.astype(BF16)
        p = lax.dot_general(av, bv, dn, preferred_element_type=F32)

        def fin(x):
            if epi is not None:
                x = epi(x, *[e[...] for e in ex])
            for o_ref, val, dt in zip(o_refs, x if multi else (x,), dts):
                o_ref[...] = val.astype(dt)

        if nk == 1:
            fin(p)
        else:
            acc = rest[ne + no]
            kk = pl.program_id(2)

            @pl.when(kk == 0)
            def _():
                acc[...] = p

            @pl.when(kk > 0)
            def _():
                acc[...] += p

            @pl.when(kk == nk - 1)
            def _():
                fin(acc[...])

    shapes = [jax.ShapeDtypeStruct((m, n), dt) for dt in dts]
    return _pcall(
        body, name=name, grid=(m // tm, n // tn, nk),
        in_specs=[a_spec, b_spec] + [o_spec] * ne, out_specs=[o_spec] * no if multi else o_spec,
        out_shape=shapes if multi else shapes[0],
        scratch=[pltpu.VMEM((tm, tn), F32)] if nk > 1 else [], rider=rider,
    )(a, b, *extras)


def _epi_add(x, r):
    return x + r


def _epi_relu_sq(x):
    x = jnp.maximum(x, 0.0)
    return x * x, x


def _epi_relu_sq_bwd(da, zr):
    return da * (2.0 * zr.astype(F32))


_ROW_T = 512


def _rms_fwd(h, g, name, rider=None):
    s, d = h.shape

    def body(h_ref, g_ref, o_ref):
        x = h_ref[...]
        r = lax.rsqrt(jnp.mean(x * x, axis=-1, keepdims=True) + _RMS_EPS)
        o_ref[...] = (x * r * g_ref[...]).astype(BF16)

    row = pl.BlockSpec((_ROW_T, d), lambda i: (i, 0))
    vec = pl.BlockSpec((1, d), lambda i: (0, 0))
    return _pcall(body, name=name, grid=(s // _ROW_T,), in_specs=[row, vec], out_specs=row,
                  out_shape=jax.ShapeDtypeStruct((s, d), BF16), rider=rider)(h, g)


def _rms_bwd_math(dn, x, g):
    r = lax.rsqrt(jnp.mean(x * x, axis=-1, keepdims=True) + _RMS_EPS)
    xhat = x * r
    dg = jnp.sum(dn * xhat, axis=0, keepdims=True)
    t = dn * g
    dx = r * (t - xhat * jnp.mean(t * xhat, axis=-1, keepdims=True))
    return dx, dg


def _rms_bwd(dn, h, g, dh_in, name):
    s, d = h.shape

    def body(dn_ref, h_ref, g_ref, dhi_ref, dh_ref, dhb_ref, dg_ref):
        dx, dg = _rms_bwd_math(dn_ref[...], h_ref[...], g_ref[...])
        dh = dhi_ref[...] + dx
        dh_ref[...] = dh
        dhb_ref[...] = dh.astype(BF16)

        @pl.when(pl.program_id(0) == 0)
        def _():
            dg_ref[...] = jnp.zeros_like(dg_ref)

        dg_ref[...] += dg

    row = pl.BlockSpec((_ROW_T, d), lambda i: (i, 0))
    vec = pl.BlockSpec((1, d), lambda i: (0, 0))
    return _pcall(
        body, name=name, grid=(s // _ROW_T,), in_specs=[row, row, vec, row], out_specs=[row, row, vec],
        out_shape=[jax.ShapeDtypeStruct((s, d), F32), jax.ShapeDtypeStruct((s, d), BF16),
                   jax.ShapeDtypeStruct((1, d), F32)])(dn, h, g, dh_in)


def _loss_head(h, tgt, g):
    s, d = h.shape

    def body(h_ref, t_ref, g_ref, dh_ref, dhb_ref, dg_ref, loss_ref):
        x, gv = h_ref[...], g_ref[...]
        r = lax.rsqrt(jnp.mean(x * x, axis=-1, keepdims=True) + _RMS_EPS)
        err = x * r * gv - t_ref[...]
        part = 0.5 * jnp.sum(jnp.mean(err * err, axis=-1, keepdims=True))
        dx, dg = _rms_bwd_math(err * (1.0 / d), x, gv)
        dh_ref[...] = dx
        dhb_ref[...] = dx.astype(BF16)

        @pl.when(pl.program_id(0) == 0)
        def _():
            dg_ref[...] = jnp.zeros_like(dg_ref)
            loss_ref[...] = jnp.zeros_like(loss_ref)

        dg_ref[...] += dg
        loss_ref[...] += jnp.full(loss_ref.shape, part, F32)

    row = pl.BlockSpec((_ROW_T, d), lambda i: (i, 0))
    vec = pl.BlockSpec((1, d), lambda i: (0, 0))
    one = pl.BlockSpec((1, _LANES), lambda i: (0, 0))
    return _pcall(
        body, name="loss_head", grid=(s // _ROW_T,), in_specs=[row, row, vec], out_specs=[row, row, vec, one],
        out_shape=[jax.ShapeDtypeStruct((s, d), F32), jax.ShapeDtypeStruct((s, d), BF16),
                   jax.ShapeDtypeStruct((1, d), F32), jax.ShapeDtypeStruct((1, _LANES), F32)])(h, tgt, g)


_CONV_T = 256
_CONV_RC = 64


def _conv_a_specs(s):
    c = _GROUP_COLS
    hb = _CONV_T // _HALO
    val = pl.BlockSpec((_CONV_T, c), lambda i: (i, 0))
    gate = pl.BlockSpec((_CONV_T, c), lambda i: (i, 1))
    hval = pl.BlockSpec((_HALO, c), lambda i: (jnp.maximum(i * hb - 1, 0), 0))
    hgate = pl.BlockSpec((_HALO, c), lambda i: (jnp.maximum(i * hb - 1, 0), 1))
    return val, gate, hval, hgate


def _fill_glu(val_ref, gate_ref, hval_ref, hgate_ref, hs_ref):
    i = pl.program_id(0)
    hs_ref[pl.ds(_HALO, _CONV_T), :] = val_ref[...].astype(F32) * _sig(gate_ref[...].astype(F32))
    halo = hval_ref[...].astype(F32) * _sig(hgate_ref[...].astype(F32))
    hs_ref[pl.ds(0, _HALO), :] = jnp.where(i > 0, halo, 0.0)


def _conv_rows(hs_ref, w_ref, r0, rows):
    off = _HALO - (_CONV_K - 1)
    acc = jnp.zeros((rows, _GROUP_COLS), F32)
    for kk in range(_CONV_K):
        acc = acc + w_ref[kk:kk + 1, :] * hs_ref[pl.ds(r0 + off + kk, rows), :]
    return acc


def _ln_fwd(ca, g, b):
    mu = jnp.mean(ca, axis=-1, keepdims=True)
    xc = ca - mu
    rstd = lax.rsqrt(jnp.mean(xc * xc, axis=-1, keepdims=True) + _LN_EPS)
    xhat = xc * rstd
    return xhat, rstd, xhat * g + b


def _conv_a_fwd(uc, w, cb, lg, lb):
    s = uc.shape[0]
    c = _GROUP_COLS

    def body(val_ref, gate_ref, hval_ref, hgate_ref, w_ref, cb_ref, lg_ref, lb_ref, o_ref, hs_ref):
        _fill_glu(val_ref, gate_ref, hval_ref, hgate_ref, hs_ref)
        for rc in range(_CONV_T // _CONV_RC):
            r0 = rc * _CONV_RC
            ca = _conv_rows(hs_ref, w_ref, r0, _CONV_RC) + cb_ref[...]
            _, _, ln = _ln_fwd(ca, lg_ref[...], lb_ref[...])
            o_ref[pl.ds(r0, _CONV_RC), :] = (ln * _sig(ln)).astype(BF16)

    val, gate, hval, hgate = _conv_a_specs(s)
    wspec = pl.BlockSpec((_CONV_K, c), lambda i: (0, 0))
    vec = pl.BlockSpec((1, c), lambda i: (0, 0))
    return _pcall(
        body, name="conv_a_fwd", grid=(s // _CONV_T,),
        in_specs=[val, gate, hval, hgate, wspec, vec, vec, vec],
        out_specs=pl.BlockSpec((_CONV_T, c), lambda i: (i, 0)),
        out_shape=jax.ShapeDtypeStruct((s, 2 * c), BF16),
        scratch=[pltpu.VMEM((_CONV_T + _HALO, c), F32)])(uc, uc, uc, uc, w, cb, lg, lb)


def _conv_a_bwd_ln(uc, dcat, w, cb, lg, lb):
    s = uc.shape[0]
    c = _GROUP_COLS

    def body(val_ref, gate_ref, hval_ref, hgate_ref, dy_ref, w_ref, cb_ref, lg_ref, lb_ref, dca_ref, st_ref, hs_ref):
        _fill_glu(val_ref, gate_ref, hval_ref, hgate_ref, hs_ref)

        @pl.when(pl.program_id(0) == 0)
        def _():
            st_ref[...] = jnp.zeros_like(st_ref)

        for rc in range(_CONV_T // _CONV_RC):
            r0 = rc * _CONV_RC
            ca = _conv_rows(hs_ref, w_ref, r0, _CONV_RC) + cb_ref[...]
            xhat, rstd, ln = _ln_fwd(ca, lg_ref[...], lb_ref[...])
            sg = _sig(ln)
            dln = dy_ref[pl.ds(r0, _CONV_RC), :] * (sg * (1.0 + ln * (1.0 - sg)))
            dxh = dln * lg_ref[...]
            dca = rstd * (dxh - jnp.mean(dxh, axis=-1, keepdims=True)
                          - xhat * jnp.mean(dxh * xhat, axis=-1, keepdims=True))
            dca_ref[pl.ds(r0, _CONV_RC), :] = dca
            st_ref[0:1, :] += jnp.sum(dca, axis=0, keepdims=True)
            st_ref[1:2, :] += jnp.sum(dln * xhat, axis=0, keepdims=True)
            st_ref[2:3, :] += jnp.sum(dln, axis=0, keepdims=True)

    val, gate, hval, hgate = _conv_a_specs(s)
    blk = pl.BlockSpec((_CONV_T, c), lambda i: (i, 0))
    wspec = pl.BlockSpec((_CONV_K, c), lambda i: (0, 0))
    vec = pl.BlockSpec((1, c), lambda i: (0, 0))
    st = pl.BlockSpec((8, c), lambda i: (0, 0))
    return _pcall(
        body, name="conv_a_bwd_ln", grid=(s // _CONV_T,),
        in_specs=[val, gate, hval, hgate, blk, wspec, vec, vec, vec], out_specs=[blk, st],
        out_shape=[jax.ShapeDtypeStruct((s, c), F32), jax.ShapeDtypeStruct((8, c), F32)],
        scratch=[pltpu.VMEM((_CONV_T + _HALO, c), F32)])(uc, uc, uc, uc, dcat, w, cb, lg, lb)


def _conv_a_bwd_conv(uc, dca, w):
    s = uc.shape[0]
    c = _GROUP_COLS
    nblk = s // _CONV_T
    hb = _CONV_T // _HALO
    off = _HALO - (_CONV_K - 1)

    def body(val_ref, gate_ref, hval_ref, hgate_ref, d_ref, dn_ref, w_ref, du_ref, dw_ref, hs_ref, ds_ref):
        i = pl.program_id(0)
        _fill_glu(val_ref, gate_ref, hval_ref, hgate_ref, hs_ref)
        ds_ref[pl.ds(0, _CONV_T), :] = d_ref[...]
        ds_ref[pl.ds(_CONV_T, _HALO), :] = jnp.where(i < nblk - 1, dn_ref[...], 0.0)

        @pl.when(i == 0)
        def _():
            dw_ref[...] = jnp.zeros_like(dw_ref)

        for rc in range(_CONV_T // _CONV_RC):
            r0 = rc * _CONV_RC
            dcur = ds_ref[pl.ds(r0, _CONV_RC), :]
            dh = jnp.zeros((_CONV_RC, c), F32)
            for kk in range(_CONV_K):
                dh = dh + w_ref[kk:kk + 1, :] * ds_ref[pl.ds(r0 + _CONV_K - 1 - kk, _CONV_RC), :]
                dw_ref[kk:kk + 1, :] += jnp.sum(dcur * hs_ref[pl.ds(r0 + off + kk, _CONV_RC), :],
                                                 axis=0, keepdims=True)
            v = val_ref[pl.ds(r0, _CONV_RC), :].astype(F32)
            sg = _sig(gate_ref[pl.ds(r0, _CONV_RC), :].astype(F32))
            du_ref[pl.ds(r0, _CONV_RC), pl.ds(0, c)] = (dh * sg).astype(BF16)
            du_ref[pl.ds(r0, _CONV_RC), pl.ds(c, c)] = (dh * v * sg * (1.0 - sg)).astype(BF16)

    val, gate, hval, hgate = _conv_a_specs(s)
    blk = pl.BlockSpec((_CONV_T, c), lambda i: (i, 0))
    nxt = pl.BlockSpec((_HALO, c), lambda i: (jnp.minimum((i + 1) * hb, s // _HALO - 1), 0))
    wspec = pl.BlockSpec((_CONV_K, c), lambda i: (0, 0))
    return _pcall(
        body, name="conv_a_bwd_conv", grid=(nblk,),
        in_specs=[val, gate, hval, hgate, blk, nxt, wspec],
        out_specs=[pl.BlockSpec((_CONV_T, 2 * c), lambda i: (i, 0)), pl.BlockSpec((_HALO, c), lambda i: (0, 0))],
        out_shape=[jax.ShapeDtypeStruct((s, 2 * c), BF16), jax.ShapeDtypeStruct((_HALO, c), F32)],
        scratch=[pltpu.VMEM((_CONV_T + _HALO, c), F32), pltpu.VMEM((_CONV_T + _HALO, c), F32)],
    )(uc, uc, uc, uc, dca, dca, w)


_SC_T = 256


def _short_conv_fwd(u2, w):
    s, d3 = u2.shape
    d = d3 // 3
    hb = _SC_T // _SC_HALO

    def body(b_ref, c_ref, v_ref, hc_ref, hv_ref, w_ref, o_ref, cs_ref):
        i = pl.program_id(0)
        cs_ref[pl.ds(_SC_HALO, _SC_T), :] = c_ref[...].astype(F32) * v_ref[...].astype(F32)
        cs_ref[pl.ds(0, _SC_HALO), :] = jnp.where(i > 0, hc_ref[...].astype(F32) * hv_ref[...].astype(F32), 0.0)
        conv = (w_ref[0:1, :] * cs_ref[pl.ds(_SC_HALO - 2, _SC_T), :]
                + w_ref[1:2, :] * cs_ref[pl.ds(_SC_HALO - 1, _SC_T), :]
                + w_ref[2:3, :] * cs_ref[pl.ds(_SC_HALO, _SC_T), :])
        o_ref[...] = (b_ref[...].astype(F32) * conv).astype(BF16)

    def col(j):
        return pl.BlockSpec((_SC_T, d), lambda i: (i, j))

    def halo(j):
        return pl.BlockSpec((_SC_HALO, d), lambda i: (jnp.maximum(i * hb - 1, 0), j))

    return _pcall(
        body, name="short_conv_fwd", grid=(s // _SC_T,),
        in_specs=[col(0), col(1), col(2), halo(1), halo(2), pl.BlockSpec((3, d), lambda i: (0, 0))],
        out_specs=pl.BlockSpec((_SC_T, d), lambda i: (i, 0)),
        out_shape=jax.ShapeDtypeStruct((s, d), BF16),
        scratch=[pltpu.VMEM((_SC_T + _SC_HALO, d), F32)])(u2, u2, u2, u2, u2, w)


def _short_conv_bwd(u2, dsc, w):
    s, d3 = u2.shape
    d = d3 // 3
    hb = _SC_T // _SC_HALO
    nblk = s // _SC_T

    def body(b_ref, c_ref, v_ref, hc_ref, hv_ref, nb_ref, d_ref, nd_ref, w_ref, du_ref, dw_ref, cs_ref, ds_ref):
        i = pl.program_id(0)
        cval, vval, bval = c_ref[...].astype(F32), v_ref[...].astype(F32), b_ref[...].astype(F32)
        cs_ref[pl.ds(_SC_HALO, _SC_T), :] = cval * vval
        cs_ref[pl.ds(0, _SC_HALO), :] = jnp.where(i > 0, hc_ref[...].astype(F32) * hv_ref[...].astype(F32), 0.0)
        dsc_cur = d_ref[...]
        dconv = dsc_cur * bval
        ds_ref[pl.ds(0, _SC_T), :] = dconv
        ds_ref[pl.ds(_SC_T, _SC_HALO), :] = jnp.where(i < nblk - 1, nd_ref[...] * nb_ref[...].astype(F32), 0.0)
        taps = [cs_ref[pl.ds(_SC_HALO - 2 + kk, _SC_T), :] for kk in range(3)]
        conv = w_ref[0:1, :] * taps[0] + w_ref[1:2, :] * taps[1] + w_ref[2:3, :] * taps[2]
        dcv = (w_ref[2:3, :] * dconv + w_ref[1:2, :] * ds_ref[pl.ds(1, _SC_T), :]
               + w_ref[0:1, :] * ds_ref[pl.ds(2, _SC_T), :])
        du_ref[:, pl.ds(0, d)] = (dsc_cur * conv).astype(BF16)
        du_ref[:, pl.ds(d, d)] = (dcv * vval).astype(BF16)
        du_ref[:, pl.ds(2 * d, d)] = (dcv * cval).astype(BF16)

        @pl.when(i == 0)
        def _():
            dw_ref[...] = jnp.zeros_like(dw_ref)

        for kk in range(3):
            dw_ref[kk:kk + 1, :] += jnp.sum(dconv * taps[kk], axis=0, keepdims=True)

    def col(j):
        return pl.BlockSpec((_SC_T, d), lambda i: (i, j))

    def halo(j):
        return pl.BlockSpec((_SC_HALO, d), lambda i: (jnp.maximum(i * hb - 1, 0), j))

    def nxt(j):
        return pl.BlockSpec((_SC_HALO, d), lambda i: (jnp.minimum((i + 1) * hb, s // _SC_HALO - 1), j))

    return _pcall(
        body, name="short_conv_bwd", grid=(nblk,),
        in_specs=[col(0), col(1), col(2), halo(1), halo(2), nxt(0), col(0), nxt(0),
                  pl.BlockSpec((3, d), lambda i: (0, 0))],
        out_specs=[pl.BlockSpec((_SC_T, d3), lambda i: (i, 0)), pl.BlockSpec((8, d), lambda i: (0, 0))],
        out_shape=[jax.ShapeDtypeStruct((s, d3), BF16), jax.ShapeDtypeStruct((8, d), F32)],
        scratch=[pltpu.VMEM((_SC_T + _SC_HALO, d), F32), pltpu.VMEM((_SC_T + _SC_HALO, d), F32)],
    )(u2, u2, u2, u2, u2, u2, dsc, dsc, w)


def _bucket_maps():
    a_idx = jnp.arange(_STEPS)[:, None]
    c_idx = jnp.arange(2 * _STEPS)[None, :]
    mdist = jnp.clip(a_idx + _STEPS - c_idx, 0, _STEPS)
    max_exact = _NUM_BUCKETS // 2
    maps = []
    for _, dil in _GROUPS:
        nn = mdist * dil
        nf = jnp.maximum(nn, 1).astype(F32)
        large = max_exact + (jnp.log(nf / max_exact) / math.log(_MAX_DISTANCE / max_exact)
                             * (_NUM_BUCKETS - max_exact)).astype(jnp.int32)
        maps.append(jnp.where(nn < max_exact, nn, jnp.minimum(large, _NUM_BUCKETS - 1)).astype(jnp.int32))
    return jnp.stack(maps, axis=0)


def _bias_expand(rel_bias, buckets):
    nh = rel_bias.shape[1]

    def body(rb_ref, bk_ref, o_ref):
        h = pl.program_id(0)
        bk = bk_ref[0]
        acc = jnp.zeros(bk.shape, F32)
        for b in range(_NUM_BUCKETS):
            acc = jnp.where(bk == b, rb_ref[b, h], acc)
        a = lax.broadcasted_iota(jnp.int32, bk.shape, 0)
        c = lax.broadcasted_iota(jnp.int32, bk.shape, 1)
        mdist = a + _STEPS - c
        o_ref[0] = jnp.where((mdist >= 0) & (mdist <= _STEPS), acc, _NEG)

    return _pcall(
        body, name="bias_expand", grid=(nh,),
        in_specs=[pl.BlockSpec(memory_space=pltpu.SMEM),
                  pl.BlockSpec((1, _STEPS, 2 * _STEPS), lambda h: (h // 8, 0, 0))],
        out_specs=pl.BlockSpec((1, _STEPS, 2 * _STEPS), lambda h: (h, 0, 0)),
        out_shape=jax.ShapeDtypeStruct((nh, _STEPS, 2 * _STEPS), F32))(rel_bias, buckets)


def _bias_reduce(ds_all, buckets):
    nh = ds_all.shape[0]

    def body(ds_ref, bk_ref, o_ref):
        t, bk = ds_ref[0], bk_ref[0]
        rows = lax.broadcasted_iota(jnp.int32, (_NUM_BUCKETS, _LANES), 0)
        out = jnp.zeros((_NUM_BUCKETS, _LANES), F32)
        for b in range(_NUM_BUCKETS):
            out = jnp.where(rows == b, jnp.sum(jnp.where(bk == b, t, 0.0)), out)
        o_ref[0] = out

    blk = pl.BlockSpec((1, _STEPS, 2 * _STEPS), lambda h: (h, 0, 0))
    return _pcall(
        body, name="bias_reduce", grid=(nh,),
        in_specs=[blk, pl.BlockSpec((1, _STEPS, 2 * _STEPS), lambda h: (h // 8, 0, 0))],
        out_specs=pl.BlockSpec((1, _NUM_BUCKETS, _LANES), lambda h: (h, 0, 0)),
        out_shape=jax.ShapeDtypeStruct((nh, _NUM_BUCKETS, _LANES), F32))(ds_all, buckets)


def _strided_rows(ref, r, dil):
    if dil == 1:
        return ref[...]
    return ref[pl.ds(r, _STEPS, stride=dil), :]


def _store_strided(ref, r, dil, val):
    if dil == 1:
        ref[...] = val
    else:
        ref[pl.ds(r, _STEPS, stride=dil), :] = val


def _head_masks():
    lane = lax.broadcasted_iota(jnp.int32, (1, _LANES), 1)
    return [lane < _HEAD_DIM, lane >= _HEAD_DIM]


def _scores(qm, k2, bias, first):
    sc = lax.dot_general(qm, k2, (((1,), (1,)), ((), ())), preferred_element_type=F32)
    sc = sc * (_HEAD_DIM ** -0.5) + bias
    col = lax.broadcasted_iota(jnp.int32, sc.shape, 1)
    return jnp.where(jnp.logical_and(first, col < _STEPS), _NEG, sc)


def _attn_fwd(uq, uk, uv, bias, g, dil, hbw, rider=None):
    s = uq.shape[0]
    rb = _STEPS * dil
    nb = s // rb
    nhb = _GROUP_COLS // hbw
    cb0 = g * nhb
    hpb = hbw // _HEAD_DIM

    def body(q_ref, kc_ref, kp_ref, vc_ref, vp_ref, b_ref, o_ref, l_ref):
        n, r = pl.program_id(1), pl.program_id(2)
        first = n == 0
        q = _strided_rows(q_ref, r, dil).astype(BF16)
        kk = jnp.concatenate([_strided_rows(kp_ref, r, dil), _strided_rows(kc_ref, r, dil)], axis=0).astype(BF16)
        vv = jnp.concatenate([_strided_rows(vp_ref, r, dil), _strided_rows(vc_ref, r, dil)], axis=0).astype(BF16)
        masks = _head_masks()
        outs, lses = [], []
        for j in range(hbw // _LANES):
            sl = slice(j * _LANES, (j + 1) * _LANES)
            q2, k2, v2 = q[:, sl], kk[:, sl], vv[:, sl]
            o_pair = jnp.zeros((_STEPS, _LANES), F32)
            l_pair = jnp.zeros((_STEPS, _LANES), F32)
            for hh in range(2):
                mk = masks[hh]
                sc = _scores(jnp.where(mk, q2, 0), k2, b_ref[2 * j + hh], first)
                mx = jnp.max(sc, axis=-1, keepdims=True)
                p = jnp.exp(sc - mx)
                den = jnp.sum(p, axis=-1, keepdims=True)
                oh = jnp.dot(p.astype(BF16), jnp.where(mk, v2, 0), preferred_element_type=F32)
                o_pair = o_pair + oh / den
                l_pair = jnp.where(mk, mx + jnp.log(den), l_pair)
            outs.append(o_pair)
            lses.append(l_pair)
        _store_strided(o_ref, r, dil, jnp.concatenate(outs, axis=1) if len(outs) > 1 else outs[0])
        _store_strided(l_ref, r, dil, jnp.concatenate(lses, axis=1) if len(lses) > 1 else lses[0])

    cur = pl.BlockSpec((rb, hbw), lambda hb, n, r: (n, cb0 + hb))
    prev = pl.BlockSpec((rb, hbw), lambda hb, n, r: (jnp.maximum(n - 1, 0), cb0 + hb))
    bspec = pl.BlockSpec((hpb, _STEPS, 2 * _STEPS), lambda hb, n, r: (g * nhb + hb, 0, 0))
    ospec = pl.BlockSpec((rb, hbw), lambda hb, n, r: (n, hb))
    return _pcall(
        body, name=f"attn_fwd_g{g}", grid=(nhb, nb, dil),
        in_specs=[cur, cur, prev, cur, prev, bspec], out_specs=[ospec, ospec],
        out_shape=[jax.ShapeDtypeStruct((s, _GROUP_COLS), F32), jax.ShapeDtypeStruct((s, _GROUP_COLS), F32)],
        rider=rider,
    )(uq, uk, uk, uv, uv, bias)


def _attn_merge(outs, lses, cat):
    s, c = outs[0].shape

    def body(o0, o1, o2, l0, l1, l2, cat_in, cat_ref, lse_ref):
        del cat_in
        a0, a1, a2 = l0[...], l1[...], l2[...]
        mx = jnp.maximum(jnp.maximum(a0, a1), a2)
        w0, w1, w2 = jnp.exp(a0 - mx), jnp.exp(a1 - mx), jnp.exp(a2 - mx)
        den = w0 + w1 + w2
        cat_ref[...] = ((w0 * o0[...] + w1 * o1[...] + w2 * o2[...]) / den).astype(BF16)
        lse_ref[...] = mx + jnp.log(den)

    blk = pl.BlockSpec((_ROW_T, c), lambda i: (i, 0))
    return _pcall(
        body, name="attn_merge", grid=(s // _ROW_T,),
        in_specs=[blk] * 6 + [_ANY],
        out_specs=[pl.BlockSpec((_ROW_T, c), lambda i: (i, 1)), blk],
        out_shape=[jax.ShapeDtypeStruct(cat.shape, BF16), jax.ShapeDtypeStruct((s, c), F32)],
        aliases={6: 0})(*outs, *lses, cat)


def _attn_delta(dcat, cat):
    s = dcat.shape[0]
    c = _GROUP_COLS
    seg = (jnp.arange(c)[:, None] // _HEAD_DIM == jnp.arange(c)[None, :] // _HEAD_DIM).astype(BF16)

    def body(dy_ref, y_ref, seg_ref, o_ref):
        prod = dy_ref[...] * y_ref[...].astype(F32)
        hi = prod.astype(BF16)
        lo = (prod - hi.astype(F32)).astype(BF16)
        o_ref[...] = (jnp.dot(hi, seg_ref[...], preferred_element_type=F32)
                      + jnp.dot(lo, seg_ref[...], preferred_element_type=F32))

    right = pl.BlockSpec((_ROW_T, c), lambda i: (i, 1))
    return _pcall(
        body, name="attn_delta", grid=(s // _ROW_T,),
        in_specs=[right, right, pl.BlockSpec((c, c), lambda i: (0, 0))],
        out_specs=pl.BlockSpec((_ROW_T, c), lambda i: (i, 0)),
        out_shape=jax.ShapeDtypeStruct((s, c), F32))(dcat, cat, seg)


def _attn_bwd(uq, uk, uv, dcat, lse, delta, bias, prev_grads, g, dil, hbw, rider=None):
    s = uq.shape[0]
    rb = _STEPS * dil
    nb = s // rb
    nhb = _GROUP_COLS // hbw
    cb0 = g * nhb
    hpb = hbw // _HEAD_DIM
    dy0 = _GROUP_COLS // hbw
    scale = _HEAD_DIM ** -0.5

    def body(q_ref, kc_ref, kp_ref, vc_ref, vp_ref, dy_ref, l_ref, dl_ref, b_ref, *rest):
        rest = rest[len(prev_grads):]
        dq_ref, dk_ref, dv_ref, dsa_ref, dkc_ref, dvc_ref = rest
        n, r = pl.program_id(1), pl.program_id(2)

        @pl.when(jnp.logical_and(n == 0, r == 0))
        def _():
            dsa_ref[...] = jnp.zeros_like(dsa_ref)

        @pl.when(n == 0)
        def _():
            dkc_ref[r] = jnp.zeros((_STEPS, hbw), F32)
            dvc_ref[r] = jnp.zeros((_STEPS, hbw), F32)

        @pl.when(n < nb)
        def _():
            first = n == 0
            q = _strided_rows(q_ref, r, dil).astype(BF16)
            kk = jnp.concatenate([_strided_rows(kp_ref, r, dil), _strided_rows(kc_ref, r, dil)],
                                 axis=0).astype(BF16)
            vv = jnp.concatenate([_strided_rows(vp_ref, r, dil), _strided_rows(vc_ref, r, dil)],
                                 axis=0).astype(BF16)
            dy = _strided_rows(dy_ref, r, dil).astype(BF16)
            lse_v = _strided_rows(l_ref, r, dil)
            dl_v = _strided_rows(dl_ref, r, dil)
            masks = _head_masks()
            dqs, dks, dvs = [], [], []
            for j in range(hbw // _LANES):
                sl = slice(j * _LANES, (j + 1) * _LANES)
                q2, k2, v2, dy2 = q[:, sl], kk[:, sl], vv[:, sl], dy[:, sl]
                dq_p = jnp.zeros((_STEPS, _LANES), F32)
                dk_p = jnp.zeros((2 * _STEPS, _LANES), F32)
                dv_p = jnp.zeros((2 * _STEPS, _LANES), F32)
                for hh in range(2):
                    mk = masks[hh]
                    lane0 = j * _LANES + hh * _HEAD_DIM
                    qm, km, dym = jnp.where(mk, q2, 0), jnp.where(mk, k2, 0), jnp.where(mk, dy2, 0)
                    sc = _scores(qm, k2, b_ref[2 * j + hh], first)
                    p = jnp.exp(sc - lse_v[:, lane0:lane0 + 1])
                    dp = lax.dot_general(dym, v2, (((1,), (1,)), ((), ())), preferred_element_type=F32)
                    ds = p * (dp - dl_v[:, lane0:lane0 + 1])
                    dsa_ref[2 * j + hh] += ds
                    dsb = ds.astype(BF16)
                    dq_p = dq_p + jnp.dot(dsb, km, preferred_element_type=F32)
                    dk_p = dk_p + lax.dot_general(dsb, qm, (((0,), (0,)), ((), ())), preferred_element_type=F32)
                    dv_p = dv_p + lax.dot_general(p.astype(BF16), dym, (((0,), (0,)), ((), ())),
                                                  preferred_element_type=F32)
                dqs.append(dq_p * scale)
                dks.append(dk_p * scale)
                dvs.append(dv_p)
            cat = lambda xs: jnp.concatenate(xs, axis=1) if len(xs) > 1 else xs[0]
            _store_strided(dq_ref, r, dil, cat(dqs))
            dk_new, dv_new = cat(dks), cat(dvs)
            _store_strided(dk_ref, r, dil, dkc_ref[r] + dk_new[:_STEPS])
            _store_strided(dv_ref, r, dil, dvc_ref[r] + dv_new[:_STEPS])
            dkc_ref[r] = dk_new[_STEPS:]
            dvc_ref[r] = dv_new[_STEPS:]

        @pl.when(n == nb)
        def _():
            _store_strided(dk_ref, r, dil, dkc_ref[r])
            _store_strided(dv_ref, r, dil, dvc_ref[r])

    def clamp(n):
        return jnp.minimum(n, nb - 1)

    cur = pl.BlockSpec((rb, hbw), lambda hb, n, r: (clamp(n), cb0 + hb))
    prev = pl.BlockSpec((rb, hbw), lambda hb, n, r: (jnp.maximum(clamp(n) - 1, 0), cb0 + hb))
    dyspec = pl.BlockSpec((rb, hbw), lambda hb, n, r: (clamp(n), dy0 + hb))
    stat = pl.BlockSpec((rb, hbw), lambda hb, n, r: (clamp(n), hb))
    bspec = pl.BlockSpec((hpb, _STEPS, 2 * _STEPS), lambda hb, n, r: (g * nhb + hb, 0, 0))
    dqspec = cur
    dkspec = pl.BlockSpec((rb, hbw), lambda hb, n, r: (jnp.maximum(n - 1, 0), cb0 + hb))
    dsspec = pl.BlockSpec((hpb, _STEPS, 2 * _STEPS), lambda hb, n, r: (hb, 0, 0))
    wide = jax.ShapeDtypeStruct((s, 3 * _GROUP_COLS), F32)
    np_ = len(prev_grads)
    return _pcall(
        body, name=f"attn_bwd_g{g}", grid=(nhb, nb + 1, dil),
        in_specs=[cur, cur, prev, cur, prev, dyspec, stat, stat, bspec] + [_ANY] * np_,
        out_specs=[dqspec, dkspec, dkspec, dsspec],
        out_shape=[wide, wide, wide, jax.ShapeDtypeStruct((8, _STEPS, 2 * _STEPS), F32)],
        scratch=[pltpu.VMEM((dil, _STEPS, hbw), F32), pltpu.VMEM((dil, _STEPS, hbw), F32)],
        aliases={9 + t: t for t in range(np_)}, rider=rider,
    )(uq, uk, uk, uv, uv, dcat, lse, delta, bias, *prev_grads)


def _place():
    x, y, c = lax.axis_index("x"), lax.axis_index("y"), lax.axis_index("c")
    chips = [(1 - x, y), (x, 1 - y), (1 - x, 1 - y)]
    return x, y, c, chips


def _slab(ref, axis, chip, width):
    start = pl.multiple_of(chip * width, width)
    if axis == 0:
        return ref.at[pl.ds(start, width), :]
    return ref.at[:, pl.ds(start, width)]


def _gather_rider(shards, axes):
    nw = len(shards)
    fulls = []
    for sh, ax in zip(shards, axes):
        shape = list(sh.shape)
        shape[ax] *= 4
        fulls.append(jax.ShapeDtypeStruct(tuple(shape), sh.dtype))

    def copies(ins, outs, scr):
        send, recv, loc = scr
        x, y, c, chips = _place()
        mine = 2 * x + y
        own, sends, arrivals = [], [], []
        for t in range(nw):
            width = ins[t].shape[axes[t]]
            own.append(pltpu.make_async_copy(ins[t], _slab(outs[t], axes[t], mine, width), loc.at[t]))
            for j, (px, py) in enumerate(chips):
                sems = dict(send_sem=send.at[3 * t + j], recv_sem=recv.at[3 * t + j],
                            device_id=(px, py, c), device_id_type=MESH)
                sends.append(pltpu.make_async_remote_copy(
                    src_ref=ins[t], dst_ref=_slab(outs[t], axes[t], mine, width), **sems))
                arrivals.append(pltpu.make_async_remote_copy(
                    src_ref=ins[t], dst_ref=_slab(outs[t], axes[t], 2 * px + py, width), **sems))
        return own, sends, arrivals

    def start(ins, outs, scr):
        own, sends, _ = copies(ins, outs, scr)
        for cp in own + sends:
            cp.start()

    def finish(ins, outs, scr):
        own, sends, arrivals = copies(ins, outs, scr)
        for cp in arrivals:
            cp.wait_recv()
        for cp in own:
            cp.wait()
        for cp in sends:
            cp.wait_send()

    return _Rider(shards, fulls, [pltpu.SemaphoreType.DMA((3 * nw,)), pltpu.SemaphoreType.DMA((3 * nw,)),
                                  pltpu.SemaphoreType.DMA((nw,))], start, finish)


def _run_rider(rider, name):
    nin, nout = len(rider.ins), len(rider.out_shapes)

    def body(*refs):
        ins, outs, scr = refs[:nin], refs[nin:nin + nout], refs[nin + nout:]
        rider.start(ins, outs, scr)
        rider.finish(ins, outs, scr)

    return _pcall(body, name=name, in_specs=[_ANY] * nin, out_specs=[_ANY] * nout, out_shape=rider.out_shapes,
                  scratch=rider.scratch)(*rider.ins)


def _scatter_rider(grads, axes):
    nw = len(grads)
    outs_shape = []
    for gr, ax in zip(grads, axes):
        shape = list(gr.shape)
        shape[ax] //= 4
        outs_shape.append(jax.ShapeDtypeStruct((3,) + tuple(shape), gr.dtype))

    def copies(ins, outs, scr):
        send, recv = scr
        x, y, c, chips = _place()
        cps = []
        for t in range(nw):
            width = ins[t].shape[axes[t]] // 4
            for j, (px, py) in enumerate(chips):
                cps.append(pltpu.make_async_remote_copy(
                    src_ref=_slab(ins[t], axes[t], 2 * px + py, width), dst_ref=outs[t].at[j],
                    send_sem=send.at[3 * t + j], recv_sem=recv.at[3 * t + j],
                    device_id=(px, py, c), device_id_type=MESH))
        return cps

    def start(ins, outs, scr):
        for cp in copies(ins, outs, scr):
            cp.start()

    def finish(ins, outs, scr):
        cps = copies(ins, outs, scr)
        for cp in cps:
            cp.wait_recv()
        for cp in cps:
            cp.wait_send()

    return _Rider(grads, outs_shape, [pltpu.SemaphoreType.DMA((3 * nw,)), pltpu.SemaphoreType.DMA((3 * nw,))],
                  start, finish)


def _swap_rider(parts):
    nw = len(parts)

    def copies(ins, outs, scr):
        send, recv = scr
        x, y, c, _ = _place()
        return [pltpu.make_async_remote_copy(
            src_ref=ins[t], dst_ref=outs[t], send_sem=send.at[t], recv_sem=recv.at[t],
            device_id=(x, y, 1 - c), device_id_type=MESH) for t in range(nw)]

    def start(ins, outs, scr):
        for cp in copies(ins, outs, scr):
            cp.start()

    def finish(ins, outs, scr):
        cps = copies(ins, outs, scr)
        for cp in cps:
            cp.wait_recv()
        for cp in cps:
            cp.wait_send()

    return _Rider(parts, [jax.ShapeDtypeStruct(p.shape, p.dtype) for p in parts],
                  [pltpu.SemaphoreType.DMA((nw,)), pltpu.SemaphoreType.DMA((nw,))], start, finish)


def _sum_all_devices(buf, name):
    rows, cols = buf.shape

    def body(in_ref, o_ref, gat_ref, send, recv):
        x, y, c, _ = _place()
        me = 4 * x + 2 * y + c
        gat_ref[me] = in_ref[...]
        started = []
        for mask in range(1, 8):
            fx, fy, fc = (mask >> 2) & 1, (mask >> 1) & 1, mask & 1
            peer = (x + fx * (1 - 2 * x), y + fy * (1 - 2 * y), c + fc * (1 - 2 * c))
            cp = pltpu.make_async_remote_copy(
                src_ref=in_ref, dst_ref=gat_ref.at[me], send_sem=send.at[mask - 1], recv_sem=recv.at[mask - 1],
                device_id=peer, device_id_type=MESH)
            cp.start()
            started.append(cp)
        for cp in started:
            cp.wait_recv()
        for cp in started:
            cp.wait_send()
        acc = gat_ref[0]
        for t in range(1, 8):
            acc = acc + gat_ref[t]
        o_ref[...] = acc

    vm = pl.BlockSpec(memory_space=pltpu.VMEM)
    return _pcall(
        body, name=name, in_specs=[vm], out_specs=vm, out_shape=jax.ShapeDtypeStruct((rows, cols), F32),
        scratch=[pltpu.VMEM((8, rows, cols), F32), pltpu.SemaphoreType.DMA((7,)), pltpu.SemaphoreType.DMA((7,))],
    )(buf)


_UPD_T = 256


def _sum_partials(own, got, name):
    rows, cols = own.shape
    tr = min(_UPD_T, rows)

    def body(own_ref, got_ref, o_ref):
        acc = own_ref[...].astype(F32)
        for j in range(3):
            acc = acc + got_ref[j].astype(F32)
        o_ref[...] = acc

    blk = pl.BlockSpec((tr, cols), lambda i: (i, 0))
    return _pcall(
        body, name=name, grid=(rows // tr,),
        in_specs=[blk, pl.BlockSpec((3, tr, cols), lambda i: (0, i, 0))], out_specs=blk,
        out_shape=jax.ShapeDtypeStruct((rows, cols), F32))(own, got)


def _adamw_math(w, gr, m, v):
    m = _B1 * m + (1.0 - _B1) * gr
    v = _B2 * v + (1.0 - _B2) * (gr * gr)
    m_hat = m / (1.0 - _B1 ** _STEP)
    v_hat = v / (1.0 - _B2 ** _STEP)
    delta = -_LR * (m_hat / (jnp.sqrt(v_hat) + _EPS) + _WD * w)
    return delta, m, v


def _adamw(w, m, v, parts, name):
    rows, cols = w.shape
    tr = min(_UPD_T, rows)
    npart = len(parts)

    def body(w_ref, m_ref, v_ref, *rest):
        p_refs, (g_ref, d_ref, nm_ref, nv_ref) = rest[:npart], rest[npart:]
        gr = p_refs[0][...]
        for p in p_refs[1:]:
            gr = gr + p[...]
        delta, nm, nv = _adamw_math(w_ref[...], gr, m_ref[...], v_ref[...])
        g_ref[...] = gr
        d_ref[...] = delta
        nm_ref[...] = nm
        nv_ref[...] = nv

    blk = pl.BlockSpec((tr, cols), lambda i: (i, 0))
    sh = jax.ShapeDtypeStruct((rows, cols), F32)
    return _pcall(body, name=name, grid=(rows // tr,), in_specs=[blk] * (3 + npart), out_specs=[blk] * 4,
                  out_shape=[sh] * 4)(w, m, v, *parts)


_PACK_W = 1024


def _pack(arrs, rows):
    flat = []
    for a in arrs:
        f = a.reshape(-1).astype(F32)
        pad = (-f.shape[0]) % _PACK_W
        flat.append(jnp.pad(f, (0, pad)))
    f = jnp.concatenate(flat)
    f = jnp.pad(f, (0, rows * _PACK_W - f.shape[0]))
    return f.reshape(rows, _PACK_W)


def _unpack(buf, shapes):
    flat = buf.reshape(-1)
    out, pos = [], 0
    for sh in shapes:
        size = math.prod(sh)
        out.append(flat[pos:pos + size].reshape(sh))
        pos += size + ((-size) % _PACK_W)
    return out


def _pack_rows(shapes):
    total = sum(-(-math.prod(sh) // _PACK_W) for sh in shapes)
    return -(-total // 8) * 8


def kernel(x, rel_bias, ab_norm, ab_w_in, ab_conv_w, ab_conv_b, ab_ln_g, ab_ln_b, ab_w_out, sc_norm, sc_w_in, sc_conv_w, sc_w_out, mlp_norm, mlp_w_up, mlp_w_down, final_norm, loss_target, m_rel_bias, m_ab_norm, m_ab_w_in, m_ab_conv_w, m_ab_conv_b, m_ab_ln_g, m_ab_ln_b, m_ab_w_out, m_sc_norm, m_sc_w_in, m_sc_conv_w, m_sc_w_out, m_mlp_norm, m_mlp_w_up, m_mlp_w_down, m_final_norm, v_rel_bias, v_ab_norm, v_ab_w_in, v_ab_conv_w, v_ab_conv_b, v_ab_ln_g, v_ab_ln_b, v_ab_w_out, v_sc_norm, v_sc_w_in, v_sc_conv_w, v_sc_w_out, v_mlp_norm, v_mlp_w_up, v_mlp_w_down, v_final_norm):
    s, d = x.shape[1], x.shape[2]
    dff = 4 * d
    c = _GROUP_COLS
    chip = 2 * lax.axis_index("x") + lax.axis_index("y")
    on_c0 = (lax.axis_index("c") == 0).astype(F32)
    h0 = x[0]
    tgt = loss_target[0]

    cw_sh, scn_sh, scw_sh = ab_conv_w[0], sc_norm, sc_conv_w[0]
    conv_w_full = lax.dynamic_update_slice(jnp.zeros((_CONV_K, c), F32), cw_sh * on_c0, (0, chip * cw_sh.shape[1]))
    scn_full = lax.dynamic_update_slice(jnp.zeros((1, d), F32), scn_sh * on_c0, (0, chip * scn_sh.shape[1]))
    scw_full = lax.dynamic_update_slice(jnp.zeros((3, d), F32), scw_sh * on_c0, (0, chip * scw_sh.shape[1]))
    small_shapes = [(_CONV_K, c), (1, d), (3, d)]
    small = _sum_all_devices(_pack([conv_w_full, scn_full, scw_full], _pack_rows(small_shapes)), "gather_small")
    conv_w, sc_g, sc_cw = _unpack(small, small_shapes)

    w_shards = [ab_w_in[0], ab_w_out[0], sc_w_in[0], sc_w_out[0], mlp_w_up[0], mlp_w_up[1],
                mlp_w_down[0], mlp_w_down[1]]
    w_axes = [1, 0, 1, 0, 1, 1, 0, 0]
    wb = [w.astype(BF16) for w in w_shards]
    full_w = [None] * 8

    def gather(idx):
        return _gather_rider([wb[t] for t in idx], [w_axes[t] for t in idx])

    def put(idx, got_w):
        for t, w in zip(idx, got_w):
            full_w[t] = w

    buckets = _bucket_maps()
    bias = _bias_expand(rel_bias, buckets)
    n0, got_w = _rms_fwd(h0, ab_norm, "rms_fwd_ab", rider=gather([0]))
    put([0], got_w)
    w_in = full_w[0]
    tm = min(1024, s)
    uc = _mm(n0, w_in, "nn", m=s, n=2 * c, k=d, tm=tm, tn=2 * c, tk=d, out_dtype=BF16, name="proj_conv")
    uq, uk, uv = [
        _mm(n0, w_in, "nn", m=s, n=3 * c, k=d, tm=tm, tn=c, tk=d, out_dtype=F32, name=f"proj_{nm}",
            b_off=(0, 2 + 3 * t))
        for t, nm in enumerate("qkv")]
    cat = _conv_a_fwd(uc, conv_w, ab_conv_b, ab_ln_g, ab_ln_b)
    outs, lses = [], []
    for g, (_, dil) in enumerate(_GROUPS):
        idx = ([4, 2], [6, 1, 3], [5, 7])[g]
        (o, l), got_w = _attn_fwd(uq, uk, uv, bias, g, dil, _LANES, rider=gather(idx))
        put(idx, got_w)
        outs.append(o)
        lses.append(l)
    _, w_out, w_si, w_so, w_up0, w_up1, w_dn0, w_dn1 = full_w
    w_up, w_dn = [w_up0, w_up1], [w_dn0, w_dn1]
    cat, lse = _attn_merge(outs, lses, cat)
    h1 = _mm(cat, w_out, "nn", m=s, n=d, k=d, tm=tm, tn=d, tk=d, out_dtype=F32, name="out_ab",
             epi=_epi_add, extras=(h0,))

    def mlp_fwd(h, layer):
        nrm = _rms_fwd(h, mlp_norm[layer:layer + 1], f"rms_fwd_mlp{layer}")
        act, zr = _mm(nrm, w_up[layer], "nn", m=s, n=dff, k=d, tm=tm, tn=1024, tk=d, out_dtype=(BF16, BF16),
                      name=f"mlp_up{layer}", epi=_epi_relu_sq)
        hn = _mm(act, w_dn[layer], "nn", m=s, n=d, k=dff, tm=tm, tn=d, tk=2048, out_dtype=F32,
                 name=f"mlp_down{layer}", epi=_epi_add, extras=(h,))
        return nrm, act, zr, hn

    n1, act0, zr0, h2 = mlp_fwd(h1, 0)
    n2 = _rms_fwd(h2, sc_g, "rms_fwd_sc")
    u2 = _mm(n2, w_si, "nn", m=s, n=3 * d, k=d, tm=tm, tn=1024, tk=d, out_dtype=BF16, name="proj_sc")
    scv = _short_conv_fwd(u2, sc_cw)
    h3 = _mm(scv, w_so, "nn", m=s, n=d, k=d, tm=tm, tn=d, tk=d, out_dtype=F32, name="out_sc",
             epi=_epi_add, extras=(h2,))
    n3, act1, zr1, h4 = mlp_fwd(h3, 1)

    dh4, dh4b, g_final, loss_part = _loss_head(h4, tgt, final_norm.reshape(1, d))
    tkw = min(2048, s)

    def mlp_bwd(dh, dhb, h, nrm, act, zr, layer):
        dz = _mm(dhb, w_dn[layer], "nt", m=s, n=dff, k=d, tm=tm, tn=1024, tk=d, out_dtype=BF16,
                 name=f"mlp_down{layer}_dx", epi=_epi_relu_sq_bwd, extras=(zr,))
        g_dn = _mm(act, dhb, "tn", m=dff, n=d, k=s, tm=1024, tn=d, tk=tkw, out_dtype=BF16,
                   name=f"mlp_down{layer}_dw")
        g_up = _mm(nrm, dz, "tn", m=d, n=dff, k=s, tm=d, tn=1024, tk=tkw, out_dtype=BF16,
                   name=f"mlp_up{layer}_dw")
        dn = _mm(dz, w_up[layer], "nt", m=s, n=d, k=dff, tm=tm, tn=d, tk=2048, out_dtype=F32,
                 name=f"mlp_up{layer}_dx")
        dh_n, dhb_n, g_norm = _rms_bwd(dn, h, mlp_norm[layer:layer + 1], dh, f"rms_bwd_mlp{layer}")
        return dh_n, dhb_n, g_norm, g_up, g_dn

    dh3, dh3b, g_mn1, g_up1, g_dn1 = mlp_bwd(dh4, dh4b, h3, n3, act1, zr1, 1)

    dsc = _mm(dh3b, w_so, "nt", m=s, n=d, k=d, tm=tm, tn=d, tk=d, out_dtype=F32, name="out_sc_dx")
    g_so = _mm(scv, dh3b, "tn", m=d, n=d, k=s, tm=d, tn=d, tk=tkw, out_dtype=BF16, name="out_sc_dw")
    du2, g_sccw8 = _short_conv_bwd(u2, dsc, sc_cw)
    g_si = _mm(n2, du2, "tn", m=d, n=3 * d, k=s, tm=d, tn=1024, tk=tkw, out_dtype=BF16, name="proj_sc_dw")
    dn2 = _mm(du2, w_si, "nt", m=s, n=d, k=3 * d, tm=tm, tn=d, tk=1536, out_dtype=F32, name="proj_sc_dx")
    dh2, dh2b, g_scn = _rms_bwd(dn2, h2, sc_g, dh3, "rms_bwd_sc")

    dh1, dh1b, g_mn0, g_up0, g_dn0 = mlp_bwd(dh2, dh2b, h1, n1, act0, zr0, 0)

    dcat = _mm(dh1b, w_out, "nt", m=s, n=d, k=d, tm=tm, tn=d, tk=d, out_dtype=F32, name="out_ab_dx")
    g_out = _mm(cat, dh1b, "tn", m=d, n=d, k=s, tm=d, tn=d, tk=tkw, out_dtype=BF16, name="out_ab_dw")
    dca, conv_stats = _conv_a_bwd_ln(uc, dcat, conv_w, ab_conv_b, ab_ln_g, ab_ln_b)
    duc, g_cw32 = _conv_a_bwd_conv(uc, dca, conv_w)
    delta = _attn_delta(dcat, cat)

    big_grads = [None, g_out, g_si, g_so, g_up0, g_up1, g_dn0, g_dn1]
    got = [None] * 8
    grads_qkv, ds_list = [], []
    for g, (_, dil) in enumerate(_GROUPS):
        idx = ([7, 5], [2, 3, 1], [4, 6])[g]
        rider = _scatter_rider([big_grads[t] for t in idx], [w_axes[t] for t in idx])
        (dq, dk, dv, dsa), got_g = _attn_bwd(uq, uk, uv, dcat, lse, delta, bias, grads_qkv, g, dil, _LANES,
                                             rider=rider)
        for t, gg in zip(idx, got_g):
            got[t] = gg
        grads_qkv = [dq, dk, dv]
        ds_list.append(dsa)
    g_bias = _bias_reduce(jnp.concatenate(ds_list, axis=0), buckets)[:, :, 0].T

    def own_slab(t):
        width = big_grads[t].shape[w_axes[t]] // 4
        return lax.dynamic_slice_in_dim(big_grads[t], chip * width, width, axis=w_axes[t])

    sums = [None] * 8
    for t in range(1, 8):
        sums[t] = _sum_partials(own_slab(t), got[t], f"sum_partials{t}")

    secs = [(duc, 2 * c, 0)] + [(grads_qkv[t], 3 * c, 2 + 3 * t) for t in range(3)]
    g_in_parts = []
    for t, (du, width, off) in enumerate(secs):
        rider = _swap_rider(sums[1:]) if t == 1 else None
        part = _mm(n0, du, "tn", m=d, n=width, k=s, tm=d, tn=c, tk=tkw, out_dtype=BF16, name=f"proj_ab_dw{t}",
                   rider=rider)
        if rider is not None:
            part, sib_late = part
        g_in_parts.append(part)
    big_grads[0] = jnp.concatenate(g_in_parts, axis=1)
    dn0 = None
    for t, (du, width, off) in enumerate(secs):
        rider = _scatter_rider([big_grads[0]], [w_axes[0]]) if t == 1 else None
        dn0 = _mm(du, w_in, "nt", m=s, n=d, k=width, tm=tm, tn=d, tk=c, out_dtype=F32, name=f"proj_ab_dx{t}",
                  b_off=(0, off), rider=rider, **({} if dn0 is None else dict(epi=_epi_add, extras=(dn0,))))
        if rider is not None:
            dn0, (got[0],) = dn0
    grad_x, _, g_abn = _rms_bwd(dn0, h0, ab_norm, dh1, "rms_bwd_ab")
    sums[0] = _sum_partials(own_slab(0), got[0], "sum_partials0")
    sib = list(_run_rider(_swap_rider([sums[0]]), "swap_sibling_w_in")) + sib_late

    big_m = [m_ab_w_in[0], m_ab_w_out[0], m_sc_w_in[0], m_sc_w_out[0], m_mlp_w_up[0], m_mlp_w_up[1],
             m_mlp_w_down[0], m_mlp_w_down[1]]
    big_v = [v_ab_w_in[0], v_ab_w_out[0], v_sc_w_in[0], v_sc_w_out[0], v_mlp_w_up[0], v_mlp_w_up[1],
             v_mlp_w_down[0], v_mlp_w_down[1]]
    upd = [_adamw(w_shards[t], big_m[t], big_v[t], [sums[t], sib[t]], f"adamw{t}") for t in range(8)]

    full_shapes = [(_NUM_BUCKETS, rel_bias.shape[1]), (1, d), (_CONV_K, c), (1, c), (1, c), (1, c), (1, d),
                   (3, d), (2, d), (d,)]
    small_grads = [g_bias, g_abn, g_cw32[:_CONV_K], conv_stats[0:1], conv_stats[1:2], conv_stats[2:3], g_scn,
                   g_sccw8[:3], jnp.concatenate([g_mn0, g_mn1], axis=0), g_final.reshape(d)]
    tot = _unpack(_sum_all_devices(_pack(small_grads, _pack_rows(full_shapes)), "sum_small"), full_shapes)
    for idx, sh in ((2, cw_sh), (6, scn_sh), (7, scw_sh)):
        width = sh.shape[1]
        tot[idx] = lax.dynamic_slice_in_dim(tot[idx], chip * width, width, axis=1)
    sm_w = [rel_bias, ab_norm, cw_sh, ab_conv_b, ab_ln_g, ab_ln_b, scn_sh, scw_sh, mlp_norm, final_norm]
    sm_m = [m_rel_bias, m_ab_norm, m_ab_conv_w[0], m_ab_conv_b, m_ab_ln_g, m_ab_ln_b, m_sc_norm, m_sc_conv_w[0],
            m_mlp_norm, m_final_norm]
    sm_v = [v_rel_bias, v_ab_norm, v_ab_conv_w[0], v_ab_conv_b, v_ab_ln_g, v_ab_ln_b, v_sc_norm, v_sc_conv_w[0],
            v_mlp_norm, v_final_norm]
    sh_shapes = [tuple(t.shape) for t in tot]
    rows = _pack_rows(sh_shapes)
    sm_upd = _adamw(_pack(sm_w, rows), _pack(sm_m, rows), _pack(sm_v, rows), [_pack(tot, rows)], "adamw_small")
    sm_g, sm_d, sm_nm, sm_nv = [_unpack(buf, sh_shapes) for buf in sm_upd]

    loss = lax.psum(loss_part[0, 0], ("x", "y", "c"))

    def assemble(big, sm):
        up = jnp.stack([big[4], big[5]], axis=0)
        dn = jnp.stack([big[6], big[7]], axis=0)
        return [sm[0], sm[1], big[0][None], sm[2][None], sm[3], sm[4], sm[5], big[1][None], sm[6], big[2][None],
                sm[7][None], big[3][None], sm[8], up, dn, sm[9]]

    res = [loss, grad_x[None]]
    for kind, sm in enumerate((sm_g, sm_d, sm_nm, sm_nv)):
        res += assemble([u[kind] for u in upd], sm)
    return tuple(res)
```

```python
import functools
import math

import jax
import jax.numpy as jnp
from jax import lax
from jax.experimental import pallas as pl
from jax.experimental.pallas import tpu as pltpu

F32 = jnp.float32
BF16 = jnp.bfloat16
MESH = pl.DeviceIdType.MESH

_GROUPS = ((128, 1), (512, 4), (2048, 16))
_STEPS = 128
_HEAD_DIM = 64
_GROUP_COLS = 512
_NUM_BUCKETS = 32
_MAX_DISTANCE = 2048
_CONV_K = 31
_HALO = 32
_SC_HALO = 16
_RMS_EPS = 1e-6
_LN_EPS = 1e-5
_NEG = -1e30
_LANES = 128
_VMEM_LIMIT = 56 * 1024 * 1024

_LR, _B1, _B2, _EPS, _WD, _STEP = 0.001, 0.9, 0.999, 1e-08, 0.01, 10


class _Rider:
    def __init__(self, ins, out_shapes, scratch, start, finish):
        self.ins, self.out_shapes, self.scratch = list(ins), list(out_shapes), list(scratch)
        self.start, self.finish = start, finish


def _pcall(body, *, name, out_shape, in_specs, out_specs, grid=None, scratch=(), aliases=None, rider=None):
    kw = {} if grid is None else {"grid": grid}
    cparams = pltpu.CompilerParams(vmem_limit_bytes=_VMEM_LIMIT)
    if rider is None:
        return pl.pallas_call(
            body, name=name, out_shape=out_shape, in_specs=in_specs, out_specs=out_specs,
            scratch_shapes=list(scratch), input_output_aliases=aliases or {},
            compiler_params=cparams, **kw)
    single = not isinstance(out_specs, (list, tuple))
    ospecs = [out_specs] if single else list(out_specs)
    oshapes = [out_shape] if single else list(out_shape)
    nin, nout, nscr = len(in_specs), len(ospecs), len(scratch)
    rin, rout = len(rider.ins), len(rider.out_shapes)

    def wrapped(*refs):
        h_in, r_in = refs[:nin], refs[nin:nin + rin]
        p = nin + rin
        h_out, r_out = refs[p:p + nout], refs[p + nout:p + nout + rout]
        p += nout + rout
        h_scr, r_scr = refs[p:p + nscr], refs[p + nscr:]
        ids = [pl.program_id(a) for a in range(len(grid))]
        first = functools.reduce(jnp.logical_and, [i == 0 for i in ids])
        last = functools.reduce(jnp.logical_and, [i == g - 1 for i, g in zip(ids, grid)])

        @pl.when(first)
        def _():
            rider.start(r_in, r_out, r_scr)

        body(*h_in, *h_out, *h_scr)

        @pl.when(last)
        def _():
            rider.finish(r_in, r_out, r_scr)

    call = pl.pallas_call(
        wrapped, name=name, out_shape=oshapes + rider.out_shapes,
        in_specs=list(in_specs) + [_ANY] * rin, out_specs=ospecs + [_ANY] * rout,
        scratch_shapes=list(scratch) + rider.scratch, input_output_aliases=aliases or {},
        compiler_params=cparams, **kw)

    def run(*operands):
        res = call(*operands, *rider.ins)
        host = res[0] if single else list(res[:nout])
        return host, list(res[nout:])

    return run


def _sig(x):
    return 1.0 / (1.0 + jnp.exp(-x))


_ANY = pl.BlockSpec(memory_space=pl.ANY)


def _lanes_of(ref):
    parts = [ref[p] for p in range(ref.shape[0])]
    return parts[0] if len(parts) == 1 else jnp.concatenate(parts, axis=1)


def _mm(a, b, mode, *, m, n, k, tm, tn, tk, out_dtype, name, epi=None, extras=(), b_off=(0, 0), rider=None,
        split=""):
    nk = k // tk
    assert m % tm == 0 and n % tn == 0 and k % tk == 0
    o0, o1 = b_off
    if mode == "nn":
        a_spec = pl.BlockSpec((tm, tk), lambda i, j, kk: (i, kk))
        b_spec = pl.BlockSpec((tk, tn), lambda i, j, kk: (kk + o0, j + o1))
        dn = (((1,), (0,)), ((), ()))
    elif mode == "nt":
        a_spec = pl.BlockSpec((tm, tk), lambda i, j, kk: (i, kk))
        if "a" in split:
            a_spec = pl.BlockSpec((tk // _LANES, tm, _LANES), lambda i, j, kk: (kk, i, 0))
        b_spec = pl.BlockSpec((tn, tk), lambda i, j, kk: (j + o0, kk + o1))
        dn = (((1,), (1,)), ((), ()))
    else:
        a_spec = pl.BlockSpec((tk, tm), lambda i, j, kk: (kk, i))
        b_spec = pl.BlockSpec((tk, tn), lambda i, j, kk: (kk + o0, j + o1))
        if "b" in split:
            b_spec = pl.BlockSpec((tn // _LANES, tk, _LANES), lambda i, j, kk: (j, kk, 0))
        dn = (((0,), (0,)), ((), ()))
    o_spec = pl.BlockSpec((tm, tn), lambda i, j, kk: (i, j))
    e_spec = o_spec
    if "o" in split:
        o_spec = pl.BlockSpec((tn // _LANES, tm, _LANES), lambda i, j, kk: (j, i, 0))
    ne = len(extras)
    multi = isinstance(out_dtype, tuple)
    dts = out_dtype if multi else (out_dtype,)
    no = len(dts)

    def body(a_ref, b_ref, *rest):
        ex, o_refs = rest[:ne], rest[ne:ne + no]
        av = _lanes_of(a_ref) if "a" in split else a_ref[...]
        bv = _lanes_of(b_ref) if "b" in split else b_ref[...]
        if av.dtype != BF16:
            av = av.astype(BF16)
        if bv.dtype != BF16:
            bv = bv.astype(BF16)
        p = lax.dot_general(av, bv, dn, preferred_element_type=F32)

        def fin(x):
            if epi is not None:
                x = epi(x, *[e[...] for e in ex])
            for o_ref, val, dt in zip(o_refs, x if multi else (x,), dts):
                if "o" in split:
                    for p in range(tn // _LANES):
                        o_ref[p] = val[:, p * _LANES:(p + 1) * _LANES].astype(dt)
                else:
                    o_ref[...] = val.astype(dt)

        if nk == 1:
            fin(p)
        else:
            acc = rest[ne + no]
            kk = pl.program_id(2)

            @pl.when(kk == 0)
            def _():
                acc[...] = p

            @pl.when(kk > 0)
            def _():
                acc[...] += p

            @pl.when(kk == nk - 1)
            def _():
                fin(acc[...])

    oshape = (n // _LANES, m, _LANES) if "o" in split else (m, n)
    shapes = [jax.ShapeDtypeStruct(oshape, dt) for dt in dts]
    return _pcall(
        body, name=name, grid=(m // tm, n // tn, nk),
        in_specs=[a_spec, b_spec] + [e_spec] * ne, out_specs=[o_spec] * no if multi else o_spec,
        out_shape=shapes if multi else shapes[0],
        scratch=[pltpu.VMEM((tm, tn), F32)] if nk > 1 else [], rider=rider,
    )(a, b, *extras)


def _epi_add(x, r):
    return x + r


def _epi_relu_sq(x):
    x = jnp.maximum(x, 0.0)
    return x * x, x


def _epi_relu_sq_bwd(da, zr):
    return da * (2.0 * zr.astype(F32))


_ROW_T = 512


def _rms_fwd(h, g, name, rider=None):
    s, d = h.shape

    def body(h_ref, g_ref, o_ref):
        x = h_ref[...]
        r = lax.rsqrt(jnp.mean(x * x, axis=-1, keepdims=True) + _RMS_EPS)
        o_ref[...] = (x * r * g_ref[...]).astype(BF16)

    row = pl.BlockSpec((_ROW_T, d), lambda i: (i, 0))
    vec = pl.BlockSpec((1, d), lambda i: (0, 0))
    return _pcall(body, name=name, grid=(s // _ROW_T,), in_specs=[row, vec], out_specs=row,
                  out_shape=jax.ShapeDtypeStruct((s, d), BF16), rider=rider)(h, g)


def _rms_bwd_math(dn, x, g):
    r = lax.rsqrt(jnp.mean(x * x, axis=-1, keepdims=True) + _RMS_EPS)
    xhat = x * r
    dg = jnp.sum(dn * xhat, axis=0, keepdims=True)
    t = dn * g
    dx = r * (t - xhat * jnp.mean(t * xhat, axis=-1, keepdims=True))
    return dx, dg


def _rms_bwd(dn, h, g, dh_in, name):
    s, d = h.shape

    def body(dn_ref, h_ref, g_ref, dhi_ref, dh_ref, dhb_ref, dg_ref):
        dx, dg = _rms_bwd_math(dn_ref[...], h_ref[...], g_ref[...])
        dh = dhi_ref[...] + dx
        dh_ref[...] = dh
        dhb_ref[...] = dh.astype(BF16)

        @pl.when(pl.program_id(0) == 0)
        def _():
            dg_ref[...] = jnp.zeros_like(dg_ref)

        dg_ref[...] += dg

    row = pl.BlockSpec((_ROW_T, d), lambda i: (i, 0))
    vec = pl.BlockSpec((1, d), lambda i: (0, 0))
    return _pcall(
        body, name=name, grid=(s // _ROW_T,), in_specs=[row, row, vec, row], out_specs=[row, row, vec],
        out_shape=[jax.ShapeDtypeStruct((s, d), F32), jax.ShapeDtypeStruct((s, d), BF16),
                   jax.ShapeDtypeStruct((1, d), F32)])(dn, h, g, dh_in)


def _loss_head(h, tgt, g):
    s, d = h.shape

    def body(h_ref, t_ref, g_ref, dh_ref, dhb_ref, dg_ref, loss_ref):
        x, gv = h_ref[...], g_ref[...]
        r = lax.rsqrt(jnp.mean(x * x, axis=-1, keepdims=True) + _RMS_EPS)
        err = x * r * gv - t_ref[...]
        part = 0.5 * jnp.sum(jnp.mean(err * err, axis=-1, keepdims=True))
        dx, dg = _rms_bwd_math(err * (1.0 / d), x, gv)
        dh_ref[...] = dx
        dhb_ref[...] = dx.astype(BF16)

        @pl.when(pl.program_id(0) == 0)
        def _():
            dg_ref[...] = jnp.zeros_like(dg_ref)
            loss_ref[...] = jnp.zeros_like(loss_ref)

        dg_ref[...] += dg
        loss_ref[...] += jnp.full(loss_ref.shape, part, F32)

    row = pl.BlockSpec((_ROW_T, d), lambda i: (i, 0))
    vec = pl.BlockSpec((1, d), lambda i: (0, 0))
    one = pl.BlockSpec((1, _LANES), lambda i: (0, 0))
    return _pcall(
        body, name="loss_head", grid=(s // _ROW_T,), in_specs=[row, row, vec], out_specs=[row, row, vec, one],
        out_shape=[jax.ShapeDtypeStruct((s, d), F32), jax.ShapeDtypeStruct((s, d), BF16),
                   jax.ShapeDtypeStruct((1, d), F32), jax.ShapeDtypeStruct((1, _LANES), F32)])(h, tgt, g)


_CONV_T = 256
_CONV_RC = 64


def _conv_a_specs(s):
    c = _GROUP_COLS
    hb = _CONV_T // _HALO
    val = pl.BlockSpec((_CONV_T, c), lambda i: (i, 0))
    gate = pl.BlockSpec((_CONV_T, c), lambda i: (i, 1))
    hval = pl.BlockSpec((_HALO, c), lambda i: (jnp.maximum(i * hb - 1, 0), 0))
    hgate = pl.BlockSpec((_HALO, c), lambda i: (jnp.maximum(i * hb - 1, 0), 1))
    return val, gate, hval, hgate


def _fill_glu(val_ref, gate_ref, hval_ref, hgate_ref, hs_ref):
    i = pl.program_id(0)
    hs_ref[pl.ds(_HALO, _CONV_T), :] = val_ref[...].astype(F32) * _sig(gate_ref[...].astype(F32))
    halo = hval_ref[...].astype(F32) * _sig(hgate_ref[...].astype(F32))
    hs_ref[pl.ds(0, _HALO), :] = jnp.where(i > 0, halo, 0.0)


def _conv_rows(hs_ref, w_ref, r0, rows):
    off = _HALO - (_CONV_K - 1)
    acc = jnp.zeros((rows, _GROUP_COLS), F32)
    for kk in range(_CONV_K):
        acc = acc + w_ref[kk:kk + 1, :] * hs_ref[pl.ds(r0 + off + kk, rows), :]
    return acc


def _ln_fwd(ca, g, b):
    mu = jnp.mean(ca, axis=-1, keepdims=True)
    xc = ca - mu
    rstd = lax.rsqrt(jnp.mean(xc * xc, axis=-1, keepdims=True) + _LN_EPS)
    xhat = xc * rstd
    return xhat, rstd, xhat * g + b


def _conv_a_fwd(uc, w, cb, lg, lb):
    s = uc.shape[0]
    c = _GROUP_COLS

    def body(val_ref, gate_ref, hval_ref, hgate_ref, w_ref, cb_ref, lg_ref, lb_ref, o_ref, hs_ref):
        _fill_glu(val_ref, gate_ref, hval_ref, hgate_ref, hs_ref)
        for rc in range(_CONV_T // _CONV_RC):
            r0 = rc * _CONV_RC
            ca = _conv_rows(hs_ref, w_ref, r0, _CONV_RC) + cb_ref[...]
            _, _, ln = _ln_fwd(ca, lg_ref[...], lb_ref[...])
            o_ref[pl.ds(r0, _CONV_RC), :] = (ln * _sig(ln)).astype(BF16)

    val, gate, hval, hgate = _conv_a_specs(s)
    wspec = pl.BlockSpec((_CONV_K, c), lambda i: (0, 0))
    vec = pl.BlockSpec((1, c), lambda i: (0, 0))
    return _pcall(
        body, name="conv_a_fwd", grid=(s // _CONV_T,),
        in_specs=[val, gate, hval, hgate, wspec, vec, vec, vec],
        out_specs=pl.BlockSpec((_CONV_T, c), lambda i: (i, 0)),
        out_shape=jax.ShapeDtypeStruct((s, 2 * c), BF16),
        scratch=[pltpu.VMEM((_CONV_T + _HALO, c), F32)])(uc, uc, uc, uc, w, cb, lg, lb)


def _conv_a_bwd_ln(uc, dcat, w, cb, lg, lb):
    s = uc.shape[0]
    c = _GROUP_COLS

    def body(val_ref, gate_ref, hval_ref, hgate_ref, dy_ref, w_ref, cb_ref, lg_ref, lb_ref, dca_ref, st_ref, hs_ref):
        _fill_glu(val_ref, gate_ref, hval_ref, hgate_ref, hs_ref)

        @pl.when(pl.program_id(0) == 0)
        def _():
            st_ref[...] = jnp.zeros_like(st_ref)

        for rc in range(_CONV_T // _CONV_RC):
            r0 = rc * _CONV_RC
            ca = _conv_rows(hs_ref, w_ref, r0, _CONV_RC) + cb_ref[...]
            xhat, rstd, ln = _ln_fwd(ca, lg_ref[...], lb_ref[...])
            sg = _sig(ln)
            dln = dy_ref[pl.ds(r0, _CONV_RC), :] * (sg * (1.0 + ln * (1.0 - sg)))
            dxh = dln * lg_ref[...]
            dca = rstd * (dxh - jnp.mean(dxh, axis=-1, keepdims=True)
                          - xhat * jnp.mean(dxh * xhat, axis=-1, keepdims=True))
            dca_ref[pl.ds(r0, _CONV_RC), :] = dca
            st_ref[0:1, :] += jnp.sum(dca, axis=0, keepdims=True)
            st_ref[1:2, :] += jnp.sum(dln * xhat, axis=0, keepdims=True)
            st_ref[2:3, :] += jnp.sum(dln, axis=0, keepdims=True)

    val, gate, hval, hgate = _conv_a_specs(s)
    blk = pl.BlockSpec((_CONV_T, c), lambda i: (i, 0))
    wspec = pl.BlockSpec((_CONV_K, c), lambda i: (0, 0))
    vec = pl.BlockSpec((1, c), lambda i: (0, 0))
    st = pl.BlockSpec((8, c), lambda i: (0, 0))
    return _pcall(
        body, name="conv_a_bwd_ln", grid=(s // _CONV_T,),
        in_specs=[val, gate, hval, hgate, blk, wspec, vec, vec, vec], out_specs=[blk, st],
        out_shape=[jax.ShapeDtypeStruct((s, c), F32), jax.ShapeDtypeStruct((8, c), F32)],
        scratch=[pltpu.VMEM((_CONV_T + _HALO, c), F32)])(uc, uc, uc, uc, dcat, w, cb, lg, lb)


def _conv_a_bwd_conv(uc, dca, w):
    s = uc.shape[0]
    c = _GROUP_COLS
    nblk = s // _CONV_T
    hb = _CONV_T // _HALO
    off = _HALO - (_CONV_K - 1)

    def body(val_ref, gate_ref, hval_ref, hgate_ref, d_ref, dn_ref, w_ref, du_ref, dw_ref, hs_ref, ds_ref):
        i = pl.program_id(0)
        _fill_glu(val_ref, gate_ref, hval_ref, hgate_ref, hs_ref)
        ds_ref[pl.ds(0, _CONV_T), :] = d_ref[...]
        ds_ref[pl.ds(_CONV_T, _HALO), :] = jnp.where(i < nblk - 1, dn_ref[...], 0.0)

        @pl.when(i == 0)
        def _():
            dw_ref[...] = jnp.zeros_like(dw_ref)

        for rc in range(_CONV_T // _CONV_RC):
            r0 = rc * _CONV_RC
            dcur = ds_ref[pl.ds(r0, _CONV_RC), :]
            dh = jnp.zeros((_CONV_RC, c), F32)
            for kk in range(_CONV_K):
                dh = dh + w_ref[kk:kk + 1, :] * ds_ref[pl.ds(r0 + _CONV_K - 1 - kk, _CONV_RC), :]
                dw_ref[kk:kk + 1, :] += jnp.sum(dcur * hs_ref[pl.ds(r0 + off + kk, _CONV_RC), :],
                                                 axis=0, keepdims=True)
            v = val_ref[pl.ds(r0, _CONV_RC), :].astype(F32)
            sg = _sig(gate_ref[pl.ds(r0, _CONV_RC), :].astype(F32))
            du_ref[pl.ds(r0, _CONV_RC), pl.ds(0, c)] = (dh * sg).astype(BF16)
            du_ref[pl.ds(r0, _CONV_RC), pl.ds(c, c)] = (dh * v * sg * (1.0 - sg)).astype(BF16)

    val, gate, hval, hgate = _conv_a_specs(s)
    blk = pl.BlockSpec((_CONV_T, c), lambda i: (i, 0))
    nxt = pl.BlockSpec((_HALO, c), lambda i: (jnp.minimum((i + 1) * hb, s // _HALO - 1), 0))
    wspec = pl.BlockSpec((_CONV_K, c), lambda i: (0, 0))
    return _pcall(
        body, name="conv_a_bwd_conv", grid=(nblk,),
        in_specs=[val, gate, hval, hgate, blk, nxt, wspec],
        out_specs=[pl.BlockSpec((_CONV_T, 2 * c), lambda i: (i, 0)), pl.BlockSpec((_HALO, c), lambda i: (0, 0))],
        out_shape=[jax.ShapeDtypeStruct((s, 2 * c), BF16), jax.ShapeDtypeStruct((_HALO, c), F32)],
        scratch=[pltpu.VMEM((_CONV_T + _HALO, c), F32), pltpu.VMEM((_CONV_T + _HALO, c), F32)],
    )(uc, uc, uc, uc, dca, dca, w)


_SC_T = 256


def _short_conv_fwd(u2, w):
    s, d3 = u2.shape
    d = d3 // 3
    hb = _SC_T // _SC_HALO

    def body(b_ref, c_ref, v_ref, hc_ref, hv_ref, w_ref, o_ref, cs_ref):
        i = pl.program_id(0)
        cs_ref[pl.ds(_SC_HALO, _SC_T), :] = c_ref[...].astype(F32) * v_ref[...].astype(F32)
        cs_ref[pl.ds(0, _SC_HALO), :] = jnp.where(i > 0, hc_ref[...].astype(F32) * hv_ref[...].astype(F32), 0.0)
        conv = (w_ref[0:1, :] * cs_ref[pl.ds(_SC_HALO - 2, _SC_T), :]
                + w_ref[1:2, :] * cs_ref[pl.ds(_SC_HALO - 1, _SC_T), :]
                + w_ref[2:3, :] * cs_ref[pl.ds(_SC_HALO, _SC_T), :])
        o_ref[...] = (b_ref[...].astype(F32) * conv).astype(BF16)

    def col(j):
        return pl.BlockSpec((_SC_T, d), lambda i: (i, j))

    def halo(j):
        return pl.BlockSpec((_SC_HALO, d), lambda i: (jnp.maximum(i * hb - 1, 0), j))

    return _pcall(
        body, name="short_conv_fwd", grid=(s // _SC_T,),
        in_specs=[col(0), col(1), col(2), halo(1), halo(2), pl.BlockSpec((3, d), lambda i: (0, 0))],
        out_specs=pl.BlockSpec((_SC_T, d), lambda i: (i, 0)),
        out_shape=jax.ShapeDtypeStruct((s, d), BF16),
        scratch=[pltpu.VMEM((_SC_T + _SC_HALO, d), F32)])(u2, u2, u2, u2, u2, w)


def _short_conv_bwd(u2, dsc, w):
    s, d3 = u2.shape
    d = d3 // 3
    hb = _SC_T // _SC_HALO
    nblk = s // _SC_T

    def body(b_ref, c_ref, v_ref, hc_ref, hv_ref, nb_ref, d_ref, nd_ref, w_ref, du_ref, dw_ref, cs_ref, ds_ref):
        i = pl.program_id(0)
        cval, vval, bval = c_ref[...].astype(F32), v_ref[...].astype(F32), b_ref[...].astype(F32)
        cs_ref[pl.ds(_SC_HALO, _SC_T), :] = cval * vval
        cs_ref[pl.ds(0, _SC_HALO), :] = jnp.where(i > 0, hc_ref[...].astype(F32) * hv_ref[...].astype(F32), 0.0)
        dsc_cur = d_ref[...]
        dconv = dsc_cur * bval
        ds_ref[pl.ds(0, _SC_T), :] = dconv
        ds_ref[pl.ds(_SC_T, _SC_HALO), :] = jnp.where(i < nblk - 1, nd_ref[...] * nb_ref[...].astype(F32), 0.0)
        taps = [cs_ref[pl.ds(_SC_HALO - 2 + kk, _SC_T), :] for kk in range(3)]
        conv = w_ref[0:1, :] * taps[0] + w_ref[1:2, :] * taps[1] + w_ref[2:3, :] * taps[2]
        dcv = (w_ref[2:3, :] * dconv + w_ref[1:2, :] * ds_ref[pl.ds(1, _SC_T), :]
               + w_ref[0:1, :] * ds_ref[pl.ds(2, _SC_T), :])
        du_ref[:, pl.ds(0, d)] = (dsc_cur * conv).astype(BF16)
        du_ref[:, pl.ds(d, d)] = (dcv * vval).astype(BF16)
        du_ref[:, pl.ds(2 * d, d)] = (dcv * cval).astype(BF16)

        @pl.when(i == 0)
        def _():
            dw_ref[...] = jnp.zeros_like(dw_ref)

        for kk in range(3):
            dw_ref[kk:kk + 1, :] += jnp.sum(dconv * taps[kk], axis=0, keepdims=True)

    def col(j):
        return pl.BlockSpec((_SC_T, d), lambda i: (i, j))

    def halo(j):
        return pl.BlockSpec((_SC_HALO, d), lambda i: (jnp.maximum(i * hb - 1, 0), j))

    def nxt(j):
        return pl.BlockSpec((_SC_HALO, d), lambda i: (jnp.minimum((i + 1) * hb, s // _SC_HALO - 1), j))

    return _pcall(
        body, name="short_conv_bwd", grid=(nblk,),
        in_specs=[col(0), col(1), col(2), halo(1), halo(2), nxt(0), col(0), nxt(0),
                  pl.BlockSpec((3, d), lambda i: (0, 0))],
        out_specs=[pl.BlockSpec((_SC_T, d3), lambda i: (i, 0)), pl.BlockSpec((8, d), lambda i: (0, 0))],
        out_shape=[jax.ShapeDtypeStruct((s, d3), BF16), jax.ShapeDtypeStruct((8, d), F32)],
        scratch=[pltpu.VMEM((_SC_T + _SC_HALO, d), F32), pltpu.VMEM((_SC_T + _SC_HALO, d), F32)],
    )(u2, u2, u2, u2, u2, u2, dsc, dsc, w)


def _bucket_maps():
    a_idx = jnp.arange(_STEPS)[:, None]
    c_idx = jnp.arange(2 * _STEPS)[None, :]
    mdist = jnp.clip(a_idx + _STEPS - c_idx, 0, _STEPS)
    max_exact = _NUM_BUCKETS // 2
    maps = []
    for _, dil in _GROUPS:
        nn = mdist * dil
        nf = jnp.maximum(nn, 1).astype(F32)
        large = max_exact + (jnp.log(nf / max_exact) / math.log(_MAX_DISTANCE / max_exact)
                             * (_NUM_BUCKETS - max_exact)).astype(jnp.int32)
        maps.append(jnp.where(nn < max_exact, nn, jnp.minimum(large, _NUM_BUCKETS - 1)).astype(jnp.int32))
    return jnp.stack(maps, axis=0)


def _bias_expand(rel_bias, buckets):
    nh = rel_bias.shape[1]

    def body(rb_ref, bk_ref, o_ref):
        h = pl.program_id(0)
        bk = bk_ref[0]
        acc = jnp.zeros(bk.shape, F32)
        for b in range(_NUM_BUCKETS):
            acc = jnp.where(bk == b, rb_ref[b, h], acc)
        a = lax.broadcasted_iota(jnp.int32, bk.shape, 0)
        c = lax.broadcasted_iota(jnp.int32, bk.shape, 1)
        mdist = a + _STEPS - c
        o_ref[0] = jnp.where((mdist >= 0) & (mdist <= _STEPS), acc, _NEG)

    return _pcall(
        body, name="bias_expand", grid=(nh,),
        in_specs=[pl.BlockSpec(memory_space=pltpu.SMEM),
                  pl.BlockSpec((1, _STEPS, 2 * _STEPS), lambda h: (h // 8, 0, 0))],
        out_specs=pl.BlockSpec((1, _STEPS, 2 * _STEPS), lambda h: (h, 0, 0)),
        out_shape=jax.ShapeDtypeStruct((nh, _STEPS, 2 * _STEPS), F32))(rel_bias, buckets)


def _bias_reduce(ds_all, buckets):
    nh = ds_all.shape[0]

    def body(ds_ref, bk_ref, o_ref):
        t, bk = ds_ref[0], bk_ref[0]
        rows = lax.broadcasted_iota(jnp.int32, (_NUM_BUCKETS, _LANES), 0)
        out = jnp.zeros((_NUM_BUCKETS, _LANES), F32)
        for b in range(_NUM_BUCKETS):
            out = jnp.where(rows == b, jnp.sum(jnp.where(bk == b, t, 0.0)), out)
        o_ref[0] = out

    blk = pl.BlockSpec((1, _STEPS, 2 * _STEPS), lambda h: (h, 0, 0))
    return _pcall(
        body, name="bias_reduce", grid=(nh,),
        in_specs=[blk, pl.BlockSpec((1, _STEPS, 2 * _STEPS), lambda h: (h // 8, 0, 0))],
        out_specs=pl.BlockSpec((1, _NUM_BUCKETS, _LANES), lambda h: (h, 0, 0)),
        out_shape=jax.ShapeDtypeStruct((nh, _NUM_BUCKETS, _LANES), F32))(ds_all, buckets)


def _strided_rows(ref, p, r, dil):
    if dil == 1:
        return ref[p]
    return ref[p, pl.ds(r, _STEPS, stride=dil), :]


def _store_strided(ref, p, r, dil, val):
    if dil == 1:
        ref[p] = val
    else:
        ref[p, pl.ds(r, _STEPS, stride=dil), :] = val


def _head_masks():
    lane = lax.broadcasted_iota(jnp.int32, (1, _LANES), 1)
    return [lane < _HEAD_DIM, lane >= _HEAD_DIM]


def _scores(qm, k2, bias, first):
    sc = lax.dot_general(qm, k2, (((1,), (1,)), ((), ())), preferred_element_type=F32)
    sc = sc * (_HEAD_DIM ** -0.5) + bias
    col = lax.broadcasted_iota(jnp.int32, sc.shape, 1)
    return jnp.where(jnp.logical_and(first, col < _STEPS), _NEG, sc)


_PAIRS = _GROUP_COLS // _LANES


def _attn_fwd(uq, uk, uv, bias, g, dil, pp, rider=None):
    s = uq.shape[1]
    rb = _STEPS * dil
    nb = s // rb
    npb = _PAIRS // pp

    def body(q_ref, kc_ref, kp_ref, vc_ref, vp_ref, b_ref, o_ref, l_ref):
        n, r = pl.program_id(1), pl.program_id(2)
        first = n == 0
        masks = _head_masks()
        for j in range(pp):
            q2 = _strided_rows(q_ref, j, r, dil).astype(BF16)
            k2 = jnp.concatenate([_strided_rows(kp_ref, j, r, dil), _strided_rows(kc_ref, j, r, dil)],
                                 axis=0).astype(BF16)
            v2 = jnp.concatenate([_strided_rows(vp_ref, j, r, dil), _strided_rows(vc_ref, j, r, dil)],
                                 axis=0).astype(BF16)
            o_pair = jnp.zeros((_STEPS, _LANES), F32)
            l_pair = jnp.zeros((_STEPS, _LANES), F32)
            for hh in range(2):
                mk = masks[hh]
                sc = _scores(jnp.where(mk, q2, 0), k2, b_ref[2 * j + hh], first)
                mx = jnp.max(sc, axis=-1, keepdims=True)
                p = jnp.exp(sc - mx)
                den = jnp.sum(p, axis=-1, keepdims=True)
                oh = jnp.dot(p.astype(BF16), jnp.where(mk, v2, 0), preferred_element_type=F32)
                o_pair = o_pair + oh / den
                l_pair = jnp.where(mk, mx + jnp.log(den), l_pair)
            _store_strided(o_ref, j, r, dil, o_pair)
            _store_strided(l_ref, j, r, dil, l_pair)

    cur = pl.BlockSpec((pp, rb, _LANES), lambda hb, n, r: (g * npb + hb, n, 0))
    prev = pl.BlockSpec((pp, rb, _LANES), lambda hb, n, r: (g * npb + hb, jnp.maximum(n - 1, 0), 0))
    bspec = pl.BlockSpec((2 * pp, _STEPS, 2 * _STEPS), lambda hb, n, r: (g * npb + hb, 0, 0))
    ospec = pl.BlockSpec((pp, rb, _LANES), lambda hb, n, r: (hb, n, 0))
    sh = jax.ShapeDtypeStruct((_PAIRS, s, _LANES), F32)
    return _pcall(
        body, name=f"attn_fwd_g{g}", grid=(npb, nb, dil),
        in_specs=[cur, cur, prev, cur, prev, bspec], out_specs=[ospec, ospec], out_shape=[sh, sh],
        rider=rider,
    )(uq, uk, uk, uv, uv, bias)


def _attn_merge(outs, lses, cat):
    s = outs[0].shape[1]
    c = _GROUP_COLS

    def body(o0, o1, o2, l0, l1, l2, cat_in, cat_ref, lse_ref):
        del cat_in
        a0, a1, a2 = l0[...], l1[...], l2[...]
        mx = jnp.maximum(jnp.maximum(a0, a1), a2)
        w0, w1, w2 = jnp.exp(a0 - mx), jnp.exp(a1 - mx), jnp.exp(a2 - mx)
        den = w0 + w1 + w2
        y = ((w0 * o0[...] + w1 * o1[...] + w2 * o2[...]) / den).astype(BF16)
        for p in range(_PAIRS):
            cat_ref[:, p * _LANES:(p + 1) * _LANES] = y[p]
        lse_ref[...] = mx + jnp.log(den)

    blk = pl.BlockSpec((_PAIRS, _ROW_T, _LANES), lambda i: (0, i, 0))
    return _pcall(
        body, name="attn_merge", grid=(s // _ROW_T,),
        in_specs=[blk] * 6 + [_ANY],
        out_specs=[pl.BlockSpec((_ROW_T, c), lambda i: (i, 1)), blk],
        out_shape=[jax.ShapeDtypeStruct(cat.shape, BF16), jax.ShapeDtypeStruct((_PAIRS, s, _LANES), F32)],
        aliases={6: 0})(*outs, *lses, cat)


def _attn_delta(dcat, cat):
    s = dcat.shape[0]
    c = _GROUP_COLS
    seg = (jnp.arange(c)[:, None] // _HEAD_DIM == jnp.arange(c)[None, :] // _HEAD_DIM).astype(BF16)

    def body(dy_ref, y_ref, seg_ref, dl_ref, dys_ref):
        dy = dy_ref[...]
        prod = dy * y_ref[...].astype(F32)
        hi = prod.astype(BF16)
        lo = (prod - hi.astype(F32)).astype(BF16)
        dl = (jnp.dot(hi, seg_ref[...], preferred_element_type=F32)
              + jnp.dot(lo, seg_ref[...], preferred_element_type=F32))
        for p in range(_PAIRS):
            dl_ref[p] = dl[:, p * _LANES:(p + 1) * _LANES]
            dys_ref[p] = dy[:, p * _LANES:(p + 1) * _LANES]

    right = pl.BlockSpec((_ROW_T, c), lambda i: (i, 1))
    blk = pl.BlockSpec((_PAIRS, _ROW_T, _LANES), lambda i: (0, i, 0))
    sh = jax.ShapeDtypeStruct((_PAIRS, s, _LANES), F32)
    return _pcall(
        body, name="attn_delta", grid=(s // _ROW_T,),
        in_specs=[right, right, pl.BlockSpec((c, c), lambda i: (0, 0))],
        out_specs=[blk, blk], out_shape=[sh, sh])(dcat, cat, seg)


def _attn_bwd(uq, uk, uv, dys, lse, delta, bias, prev_grads, g, dil, pp, rider=None):
    s = uq.shape[1]
    rb = _STEPS * dil
    nb = s // rb
    npb = _PAIRS // pp
    scale = _HEAD_DIM ** -0.5

    def body(q_ref, kc_ref, kp_ref, vc_ref, vp_ref, dy_ref, l_ref, dl_ref, b_ref, *rest):
        rest = rest[len(prev_grads):]
        dq_ref, dk_ref, dv_ref, dsa_ref, dkc_ref, dvc_ref = rest
        n, r = pl.program_id(1), pl.program_id(2)

        @pl.when(jnp.logical_and(n == 0, r == 0))
        def _():
            dsa_ref[...] = jnp.zeros_like(dsa_ref)

        @pl.when(n == 0)
        def _():
            for j in range(pp):
                dkc_ref[r * pp + j] = jnp.zeros((_STEPS, _LANES), F32)
                dvc_ref[r * pp + j] = jnp.zeros((_STEPS, _LANES), F32)

        @pl.when(n < nb)
        def _():
            first = n == 0
            masks = _head_masks()
            for j in range(pp):
                q2 = _strided_rows(q_ref, j, r, dil).astype(BF16)
                k2 = jnp.concatenate([_strided_rows(kp_ref, j, r, dil), _strided_rows(kc_ref, j, r, dil)],
                                     axis=0).astype(BF16)
                v2 = jnp.concatenate([_strided_rows(vp_ref, j, r, dil), _strided_rows(vc_ref, j, r, dil)],
                                     axis=0).astype(BF16)
                dy2 = _strided_rows(dy_ref, j, r, dil).astype(BF16)
                lse_v = _strided_rows(l_ref, j, r, dil)
                dl_v = _strided_rows(dl_ref, j, r, dil)
                dq_p = jnp.zeros((_STEPS, _LANES), F32)
                dk_p = jnp.zeros((2 * _STEPS, _LANES), F32)
                dv_p = jnp.zeros((2 * _STEPS, _LANES), F32)
                for hh in range(2):
                    mk = masks[hh]
                    lane0 = hh * _HEAD_DIM
                    qm, km, dym = jnp.where(mk, q2, 0), jnp.where(mk, k2, 0), jnp.where(mk, dy2, 0)
                    sc = _scores(qm, k2, b_ref[2 * j + hh], first)
                    p = jnp.exp(sc - lse_v[:, lane0:lane0 + 1])
                    dp = lax.dot_general(dym, v2, (((1,), (1,)), ((), ())), preferred_element_type=F32)
                    ds = p * (dp - dl_v[:, lane0:lane0 + 1])
                    dsa_ref[2 * j + hh] += ds
                    dsb = ds.astype(BF16)
                    dq_p = dq_p + jnp.dot(dsb, km, preferred_element_type=F32)
                    dk_p = dk_p + lax.dot_general(dsb, qm, (((0,), (0,)), ((), ())), preferred_element_type=F32)
                    dv_p = dv_p + lax.dot_general(p.astype(BF16), dym, (((0,), (0,)), ((), ())),
                                                  preferred_element_type=F32)
                dk_p = dk_p * scale
                _store_strided(dq_ref, j, r, dil, dq_p * scale)
                _store_strided(dk_ref, j, r, dil, dkc_ref[r * pp + j] + dk_p[:_STEPS])
                _store_strided(dv_ref, j, r, dil, dvc_ref[r * pp + j] + dv_p[:_STEPS])
                dkc_ref[r * pp + j] = dk_p[_STEPS:]
                dvc_ref[r * pp + j] = dv_p[_STEPS:]

        @pl.when(n == nb)
        def _():
            for j in range(pp):
                _store_strided(dk_ref, j, r, dil, dkc_ref[r * pp + j])
                _store_strided(dv_ref, j, r, dil, dvc_ref[r * pp + j])

    def clamp(n):
        return jnp.minimum(n, nb - 1)

    cur = pl.BlockSpec((pp, rb, _LANES), lambda hb, n, r: (g * npb + hb, clamp(n), 0))
    prev = pl.BlockSpec((pp, rb, _LANES), lambda hb, n, r: (g * npb + hb, jnp.maximum(clamp(n) - 1, 0), 0))
    stat = pl.BlockSpec((pp, rb, _LANES), lambda hb, n, r: (hb, clamp(n), 0))
    bspec = pl.BlockSpec((2 * pp, _STEPS, 2 * _STEPS), lambda hb, n, r: (g * npb + hb, 0, 0))
    dkspec = pl.BlockSpec((pp, rb, _LANES), lambda hb, n, r: (g * npb + hb, jnp.maximum(n - 1, 0), 0))
    dsspec = pl.BlockSpec((2 * pp, _STEPS, 2 * _STEPS), lambda hb, n, r: (hb, 0, 0))
    wide = jax.ShapeDtypeStruct((3 * _PAIRS, s, _LANES), F32)
    np_ = len(prev_grads)
    return _pcall(
        body, name=f"attn_bwd_g{g}", grid=(npb, nb + 1, dil),
        in_specs=[cur, cur, prev, cur, prev, stat, stat, stat, bspec] + [_ANY] * np_,
        out_specs=[cur, dkspec, dkspec, dsspec],
        out_shape=[wide, wide, wide, jax.ShapeDtypeStruct((8, _STEPS, 2 * _STEPS), F32)],
        scratch=[pltpu.VMEM((dil * pp, _STEPS, _LANES), F32), pltpu.VMEM((dil * pp, _STEPS, _LANES), F32)],
        aliases={9 + t: t for t in range(np_)}, rider=rider,
    )(uq, uk, uk, uv, uv, dys, lse, delta, bias, *prev_grads)


def _place():
    x, y, c = lax.axis_index("x"), lax.axis_index("y"), lax.axis_index("c")
    chips = [(1 - x, y), (x, 1 - y), (1 - x, 1 - y)]
    return x, y, c, chips


def _slab(ref, axis, chip, width):
    start = pl.multiple_of(chip * width, width)
    if axis == 0:
        return ref.at[pl.ds(start, width), :]
    return ref.at[:, pl.ds(start, width)]


def _gather_rider(shards, axes):
    nw = len(shards)
    fulls = []
    for sh, ax in zip(shards, axes):
        shape = list(sh.shape)
        shape[ax] *= 4
        fulls.append(jax.ShapeDtypeStruct(tuple(shape), sh.dtype))

    def copies(ins, outs, scr):
        send, recv, loc = scr
        x, y, c, chips = _place()
        mine = 2 * x + y
        own, sends, arrivals = [], [], []
        for t in range(nw):
            width = ins[t].shape[axes[t]]
            own.append(pltpu.make_async_copy(ins[t], _slab(outs[t], axes[t], mine, width), loc.at[t]))
            for j, (px, py) in enumerate(chips):
                sems = dict(send_sem=send.at[3 * t + j], recv_sem=recv.at[3 * t + j],
                            device_id=(px, py, c), device_id_type=MESH)
                sends.append(pltpu.make_async_remote_copy(
                    src_ref=ins[t], dst_ref=_slab(outs[t], axes[t], mine, width), **sems))
                arrivals.append(pltpu.make_async_remote_copy(
                    src_ref=ins[t], dst_ref=_slab(outs[t], axes[t], 2 * px + py, width), **sems))
        return own, sends, arrivals

    def start(ins, outs, scr):
        own, sends, _ = copies(ins, outs, scr)
        for cp in own + sends:
            cp.start()

    def finish(ins, outs, scr):
        own, sends, arrivals = copies(ins, outs, scr)
        for cp in arrivals:
            cp.wait_recv()
        for cp in own:
            cp.wait()
        for cp in sends:
            cp.wait_send()

    return _Rider(shards, fulls, [pltpu.SemaphoreType.DMA((3 * nw,)), pltpu.SemaphoreType.DMA((3 * nw,)),
                                  pltpu.SemaphoreType.DMA((nw,))], start, finish)


def _run_rider(rider, name):
    nin, nout = len(rider.ins), len(rider.out_shapes)

    def body(*refs):
        ins, outs, scr = refs[:nin], refs[nin:nin + nout], refs[nin + nout:]
        rider.start(ins, outs, scr)
        rider.finish(ins, outs, scr)

    return _pcall(body, name=name, in_specs=[_ANY] * nin, out_specs=[_ANY] * nout, out_shape=rider.out_shapes,
                  scratch=rider.scratch)(*rider.ins)


def _scatter_rider(grads, axes):
    nw = len(grads)
    outs_shape = []
    for gr, ax in zip(grads, axes):
        shape = list(gr.shape)
        shape[ax] //= 4
        outs_shape.append(jax.ShapeDtypeStruct((3,) + tuple(shape), gr.dtype))

    def copies(ins, outs, scr):
        send, recv = scr
        x, y, c, chips = _place()
        cps = []
        for t in range(nw):
            width = ins[t].shape[axes[t]] // 4
            for j, (px, py) in enumerate(chips):
                cps.append(pltpu.make_async_remote_copy(
                    src_ref=_slab(ins[t], axes[t], 2 * px + py, width), dst_ref=outs[t].at[j],
                    send_sem=send.at[3 * t + j], recv_sem=recv.at[3 * t + j],
                    device_id=(px, py, c), device_id_type=MESH))
        return cps

    def start(ins, outs, scr):
        for cp in copies(ins, outs, scr):
            cp.start()

    def finish(ins, outs, scr):
        cps = copies(ins, outs, scr)
        for cp in cps:
            cp.wait_recv()
        for cp in cps:
            cp.wait_send()

    return _Rider(grads, outs_shape, [pltpu.SemaphoreType.DMA((3 * nw,)), pltpu.SemaphoreType.DMA((3 * nw,))],
                  start, finish)


def _swap_rider(parts):
    nw = len(parts)

    def copies(ins, outs, scr):
        send, recv = scr
        x, y, c, _ = _place()
        return [pltpu.make_async_remote_copy(
            src_ref=ins[t], dst_ref=outs[t], send_sem=send.at[t], recv_sem=recv.at[t],
            device_id=(x, y, 1 - c), device_id_type=MESH) for t in range(nw)]

    def start(ins, outs, scr):
        for cp in copies(ins, outs, scr):
            cp.start()

    def finish(ins, outs, scr):
        cps = copies(ins, outs, scr)
        for cp in cps:
            cp.wait_recv()
        for cp in cps:
            cp.wait_send()

    return _Rider(parts, [jax.ShapeDtypeStruct(p.shape, p.dtype) for p in parts],
                  [pltpu.SemaphoreType.DMA((nw,)), pltpu.SemaphoreType.DMA((nw,))], start, finish)


def _sum_all_devices(buf, name):
    rows, cols = buf.shape

    def body(in_ref, o_ref, gat_ref, send, recv):
        x, y, c, _ = _place()
        me = 4 * x + 2 * y + c
        gat_ref[me] = in_ref[...]
        started = []
        for mask in range(1, 8):
            fx, fy, fc = (mask >> 2) & 1, (mask >> 1) & 1, mask & 1
            peer = (x + fx * (1 - 2 * x), y + fy * (1 - 2 * y), c + fc * (1 - 2 * c))
            cp = pltpu.make_async_remote_copy(
                src_ref=in_ref, dst_ref=gat_ref.at[me], send_sem=send.at[mask - 1], recv_sem=recv.at[mask - 1],
                device_id=peer, device_id_type=MESH)
            cp.start()
            started.append(cp)
        for cp in started:
            cp.wait_recv()
        for cp in started:
            cp.wait_send()
        acc = gat_ref[0]
        for t in range(1, 8):
            acc = acc + gat_ref[t]
        o_ref[...] = acc

    vm = pl.BlockSpec(memory_space=pltpu.VMEM)
    return _pcall(
        body, name=name, in_specs=[vm], out_specs=vm, out_shape=jax.ShapeDtypeStruct((rows, cols), F32),
        scratch=[pltpu.VMEM((8, rows, cols), F32), pltpu.SemaphoreType.DMA((7,)), pltpu.SemaphoreType.DMA((7,))],
    )(buf)


_UPD_T = 256


def _sum_partials(own, got, name):
    rows, cols = own.shape
    tr = min(_UPD_T, rows)

    def body(own_ref, got_ref, o_ref):
        acc = own_ref[...].astype(F32)
        for j in range(3):
            acc = acc + got_ref[j].astype(F32)
        o_ref[...] = acc

    blk = pl.BlockSpec((tr, cols), lambda i: (i, 0))
    return _pcall(
        body, name=name, grid=(rows // tr,),
        in_specs=[blk, pl.BlockSpec((3, tr, cols), lambda i: (0, i, 0))], out_specs=blk,
        out_shape=jax.ShapeDtypeStruct((rows, cols), F32))(own, got)


def _adamw_math(w, gr, m, v):
    m = _B1 * m + (1.0 - _B1) * gr
    v = _B2 * v + (1.0 - _B2) * (gr * gr)
    m_hat = m / (1.0 - _B1 ** _STEP)
    v_hat = v / (1.0 - _B2 ** _STEP)
    delta = -_LR * (m_hat / (jnp.sqrt(v_hat) + _EPS) + _WD * w)
    return delta, m, v


def _adamw(w, m, v, parts, name):
    rows, cols = w.shape
    tr = min(_UPD_T, rows)
    npart = len(parts)

    def body(w_ref, m_ref, v_ref, *rest):
        p_refs, (g_ref, d_ref, nm_ref, nv_ref) = rest[:npart], rest[npart:]
        gr = p_refs[0][...]
        for p in p_refs[1:]:
            gr = gr + p[...]
        delta, nm, nv = _adamw_math(w_ref[...], gr, m_ref[...], v_ref[...])
        g_ref[...] = gr
        d_ref[...] = delta
        nm_ref[...] = nm
        nv_ref[...] = nv

    blk = pl.BlockSpec((tr, cols), lambda i: (i, 0))
    sh = jax.ShapeDtypeStruct((rows, cols), F32)
    return _pcall(body, name=name, grid=(rows // tr,), in_specs=[blk] * (3 + npart), out_specs=[blk] * 4,
                  out_shape=[sh] * 4)(w, m, v, *parts)


_PACK_W = 1024


def _pack(arrs, rows):
    flat = []
    for a in arrs:
        f = a.reshape(-1).astype(F32)
        pad = (-f.shape[0]) % _PACK_W
        flat.append(jnp.pad(f, (0, pad)))
    f = jnp.concatenate(flat)
    f = jnp.pad(f, (0, rows * _PACK_W - f.shape[0]))
    return f.reshape(rows, _PACK_W)


def _unpack(buf, shapes):
    flat = buf.reshape(-1)
    out, pos = [], 0
    for sh in shapes:
        size = math.prod(sh)
        out.append(flat[pos:pos + size].reshape(sh))
        pos += size + ((-size) % _PACK_W)
    return out


def _pack_rows(shapes):
    total = sum(-(-math.prod(sh) // _PACK_W) for sh in shapes)
    return -(-total // 8) * 8


def kernel(x, rel_bias, ab_norm, ab_w_in, ab_conv_w, ab_conv_b, ab_ln_g, ab_ln_b, ab_w_out, sc_norm, sc_w_in, sc_conv_w, sc_w_out, mlp_norm, mlp_w_up, mlp_w_down, final_norm, loss_target, m_rel_bias, m_ab_norm, m_ab_w_in, m_ab_conv_w, m_ab_conv_b, m_ab_ln_g, m_ab_ln_b, m_ab_w_out, m_sc_norm, m_sc_w_in, m_sc_conv_w, m_sc_w_out, m_mlp_norm, m_mlp_w_up, m_mlp_w_down, m_final_norm, v_rel_bias, v_ab_norm, v_ab_w_in, v_ab_conv_w, v_ab_conv_b, v_ab_ln_g, v_ab_ln_b, v_ab_w_out, v_sc_norm, v_sc_w_in, v_sc_conv_w, v_sc_w_out, v_mlp_norm, v_mlp_w_up, v_mlp_w_down, v_final_norm):
    s, d = x.shape[1], x.shape[2]
    dff = 4 * d
    c = _GROUP_COLS
    chip = 2 * lax.axis_index("x") + lax.axis_index("y")
    on_c0 = (lax.axis_index("c") == 0).astype(F32)
    h0 = x[0]
    tgt = loss_target[0]

    cw_sh, scn_sh, scw_sh = ab_conv_w[0], sc_norm, sc_conv_w[0]
    conv_w_full = lax.dynamic_update_slice(jnp.zeros((_CONV_K, c), F32), cw_sh * on_c0, (0, chip * cw_sh.shape[1]))
    scn_full = lax.dynamic_update_slice(jnp.zeros((1, d), F32), scn_sh * on_c0, (0, chip * scn_sh.shape[1]))
    scw_full = lax.dynamic_update_slice(jnp.zeros((3, d), F32), scw_sh * on_c0, (0, chip * scw_sh.shape[1]))
    small_shapes = [(_CONV_K, c), (1, d), (3, d)]
    small = _sum_all_devices(_pack([conv_w_full, scn_full, scw_full], _pack_rows(small_shapes)), "gather_small")
    conv_w, sc_g, sc_cw = _unpack(small, small_shapes)

    w_shards = [ab_w_in[0], ab_w_out[0], sc_w_in[0], sc_w_out[0], mlp_w_up[0], mlp_w_up[1],
                mlp_w_down[0], mlp_w_down[1]]
    w_axes = [1, 0, 1, 0, 1, 1, 0, 0]
    wb = [w.astype(BF16) for w in w_shards]
    full_w = [None] * 8

    def gather(idx):
        return _gather_rider([wb[t] for t in idx], [w_axes[t] for t in idx])

    def put(idx, got_w):
        for t, w in zip(idx, got_w):
            full_w[t] = w

    buckets = _bucket_maps()
    bias = _bias_expand(rel_bias, buckets)
    n0, got_w = _rms_fwd(h0, ab_norm, "rms_fwd_ab", rider=gather([0]))
    put([0], got_w)
    w_in = full_w[0]
    tm = min(1024, s)
    uc = _mm(n0, w_in, "nn", m=s, n=2 * c, k=d, tm=tm, tn=2 * c, tk=d, out_dtype=BF16, name="proj_conv")
    uq, uk, uv = [
        _mm(n0, w_in, "nn", m=s, n=3 * c, k=d, tm=tm, tn=c, tk=d, out_dtype=F32, name=f"proj_{nm}",
            b_off=(0, 2 + 3 * t), split="o")
        for t, nm in enumerate("qkv")]
    cat = _conv_a_fwd(uc, conv_w, ab_conv_b, ab_ln_g, ab_ln_b)
    outs, lses = [], []
    for g, (_, dil) in enumerate(_GROUPS):
        idx = ([4, 2], [6, 1, 3], [5, 7])[g]
        (o, l), got_w = _attn_fwd(uq, uk, uv, bias, g, dil, 4 if dil <= 4 else 2, rider=gather(idx))
        put(idx, got_w)
        outs.append(o)
        lses.append(l)
    _, w_out, w_si, w_so, w_up0, w_up1, w_dn0, w_dn1 = full_w
    w_up, w_dn = [w_up0, w_up1], [w_dn0, w_dn1]
    cat, lse = _attn_merge(outs, lses, cat)
    h1 = _mm(cat, w_out, "nn", m=s, n=d, k=d, tm=tm, tn=d, tk=d, out_dtype=F32, name="out_ab",
             epi=_epi_add, extras=(h0,))

    def mlp_fwd(h, layer):
        nrm = _rms_fwd(h, mlp_norm[layer:layer + 1], f"rms_fwd_mlp{layer}")
        act, zr = _mm(nrm, w_up[layer], "nn", m=s, n=dff, k=d, tm=tm, tn=1024, tk=d, out_dtype=(BF16, BF16),
                      name=f"mlp_up{layer}", epi=_epi_relu_sq)
        hn = _mm(act, w_dn[layer], "nn", m=s, n=d, k=dff, tm=tm, tn=d, tk=2048, out_dtype=F32,
                 name=f"mlp_down{layer}", epi=_epi_add, extras=(h,))
        return nrm, act, zr, hn

    n1, act0, zr0, h2 = mlp_fwd(h1, 0)
    n2 = _rms_fwd(h2, sc_g, "rms_fwd_sc")
    u2 = _mm(n2, w_si, "nn", m=s, n=3 * d, k=d, tm=tm, tn=1024, tk=d, out_dtype=BF16, name="proj_sc")
    scv = _short_conv_fwd(u2, sc_cw)
    h3 = _mm(scv, w_so, "nn", m=s, n=d, k=d, tm=tm, tn=d, tk=d, out_dtype=F32, name="out_sc",
             epi=_epi_add, extras=(h2,))
    n3, act1, zr1, h4 = mlp_fwd(h3, 1)

    dh4, dh4b, g_final, loss_part = _loss_head(h4, tgt, final_norm.reshape(1, d))
    tkw = min(2048, s)

    def mlp_bwd(dh, dhb, h, nrm, act, zr, layer):
        dz = _mm(dhb, w_dn[layer], "nt", m=s, n=dff, k=d, tm=tm, tn=1024, tk=d, out_dtype=BF16,
                 name=f"mlp_down{layer}_dx", epi=_epi_relu_sq_bwd, extras=(zr,))
        g_dn = _mm(act, dhb, "tn", m=dff, n=d, k=s, tm=1024, tn=d, tk=tkw, out_dtype=BF16,
                   name=f"mlp_down{layer}_dw")
        g_up = _mm(nrm, dz, "tn", m=d, n=dff, k=s, tm=d, tn=1024, tk=tkw, out_dtype=BF16,
                   name=f"mlp_up{layer}_dw")
        dn = _mm(dz, w_up[layer], "nt", m=s, n=d, k=dff, tm=tm, tn=d, tk=2048, out_dtype=F32,
                 name=f"mlp_up{layer}_dx")
        dh_n, dhb_n, g_norm = _rms_bwd(dn, h, mlp_norm[layer:layer + 1], dh, f"rms_bwd_mlp{layer}")
        return dh_n, dhb_n, g_norm, g_up, g_dn

    dh3, dh3b, g_mn1, g_up1, g_dn1 = mlp_bwd(dh4, dh4b, h3, n3, act1, zr1, 1)

    dsc = _mm(dh3b, w_so, "nt", m=s, n=d, k=d, tm=tm, tn=d, tk=d, out_dtype=F32, name="out_sc_dx")
    g_so = _mm(scv, dh3b, "tn", m=d, n=d, k=s, tm=d, tn=d, tk=tkw, out_dtype=BF16, name="out_sc_dw")
    du2, g_sccw8 = _short_conv_bwd(u2, dsc, sc_cw)
    g_si = _mm(n2, du2, "tn", m=d, n=3 * d, k=s, tm=d, tn=1024, tk=tkw, out_dtype=BF16, name="proj_sc_dw")
    dn2 = _mm(du2, w_si, "nt", m=s, n=d, k=3 * d, tm=tm, tn=d, tk=1536, out_dtype=F32, name="proj_sc_dx")
    dh2, dh2b, g_scn = _rms_bwd(dn2, h2, sc_g, dh3, "rms_bwd_sc")

    dh1, dh1b, g_mn0, g_up0, g_dn0 = mlp_bwd(dh2, dh2b, h1, n1, act0, zr0, 0)

    dcat = _mm(dh1b, w_out, "nt", m=s, n=d, k=d, tm=tm, tn=d, tk=d, out_dtype=F32, name="out_ab_dx")
    g_out = _mm(cat, dh1b, "tn", m=d, n=d, k=s, tm=d, tn=d, tk=tkw, out_dtype=BF16, name="out_ab_dw")
    dca, conv_stats = _conv_a_bwd_ln(uc, dcat, conv_w, ab_conv_b, ab_ln_g, ab_ln_b)
    duc, g_cw32 = _conv_a_bwd_conv(uc, dca, conv_w)
    delta, dys = _attn_delta(dcat, cat)

    big_grads = [None, g_out, g_si, g_so, g_up0, g_up1, g_dn0, g_dn1]
    got = [None] * 8
    grads_qkv, ds_list = [], []
    for g, (_, dil) in enumerate(_GROUPS):
        idx = ([7, 5], [2, 3, 1], [4, 6])[g]
        rider = _scatter_rider([big_grads[t] for t in idx], [w_axes[t] for t in idx])
        (dq, dk, dv, dsa), got_g = _attn_bwd(uq, uk, uv, dys, lse, delta, bias, grads_qkv, g, dil,
                                             4 if dil <= 4 else 1, rider=rider)
        for t, gg in zip(idx, got_g):
            got[t] = gg
        grads_qkv = [dq, dk, dv]
        ds_list.append(dsa)
    g_bias = _bias_reduce(jnp.concatenate(ds_list, axis=0), buckets)[:, :, 0].T

    def own_slab(t):
        width = big_grads[t].shape[w_axes[t]] // 4
        return lax.dynamic_slice_in_dim(big_grads[t], chip * width, width, axis=w_axes[t])

    sums = [None] * 8
    for t in range(1, 8):
        sums[t] = _sum_partials(own_slab(t), got[t], f"sum_partials{t}")

    secs = [(duc, 2 * c, 0)] + [(grads_qkv[t], 3 * c, 2 + 3 * t) for t in range(3)]
    g_in_parts = []
    for t, (du, width, off) in enumerate(secs):
        rider = _swap_rider(sums[1:]) if t == 1 else None
        part = _mm(n0, du, "tn", m=d, n=width, k=s, tm=d, tn=c, tk=tkw, out_dtype=BF16, name=f"proj_ab_dw{t}",
                   rider=rider, split="b" if t else "")
        if rider is not None:
            part, sib_late = part
        g_in_parts.append(part)
    big_grads[0] = jnp.concatenate(g_in_parts, axis=1)
    dn0 = None
    for t, (du, width, off) in enumerate(secs):
        rider = _scatter_rider([big_grads[0]], [w_axes[0]]) if t == 1 else None
        dn0 = _mm(du, w_in, "nt", m=s, n=d, k=width, tm=tm, tn=d, tk=c, out_dtype=F32, name=f"proj_ab_dx{t}",
                  b_off=(0, off), rider=rider, split="a" if t else "",
                  **({} if dn0 is None else dict(epi=_epi_add, extras=(dn0,))))
        if rider is not None:
            dn0, (got[0],) = dn0
    grad_x, _, g_abn = _rms_bwd(dn0, h0, ab_norm, dh1, "rms_bwd_ab")
    sums[0] = _sum_partials(own_slab(0), got[0], "sum_partials0")
    sib = list(_run_rider(_swap_rider([sums[0]]), "swap_sibling_w_in")) + sib_late

    big_m = [m_ab_w_in[0], m_ab_w_out[0], m_sc_w_in[0], m_sc_w_out[0], m_mlp_w_up[0], m_mlp_w_up[1],
             m_mlp_w_down[0], m_mlp_w_down[1]]
    big_v = [v_ab_w_in[0], v_ab_w_out[0], v_sc_w_in[0], v_sc_w_out[0], v_mlp_w_up[0], v_mlp_w_up[1],
             v_mlp_w_down[0], v_mlp_w_down[1]]
    upd = [_adamw(w_shards[t], big_m[t], big_v[t], [sums[t], sib[t]], f"adamw{t}") for t in range(8)]

    full_shapes = [(_NUM_BUCKETS, rel_bias.shape[1]), (1, d), (_CONV_K, c), (1, c), (1, c), (1, c), (1, d),
                   (3, d), (2, d), (d,)]
    small_grads = [g_bias, g_abn, g_cw32[:_CONV_K], conv_stats[0:1], conv_stats[1:2], conv_stats[2:3], g_scn,
                   g_sccw8[:3], jnp.concatenate([g_mn0, g_mn1], axis=0), g_final.reshape(d)]
    tot = _unpack(_sum_all_devices(_pack(small_grads, _pack_rows(full_shapes)), "sum_small"), full_shapes)
    for idx, sh in ((2, cw_sh), (6, scn_sh), (7, scw_sh)):
        width = sh.shape[1]
        tot[idx] = lax.dynamic_slice_in_dim(tot[idx], chip * width, width, axis=1)
    sm_w = [rel_bias, ab_norm, cw_sh, ab_conv_b, ab_ln_g, ab_ln_b, scn_sh, scw_sh, mlp_norm, final_norm]
    sm_m = [m_rel_bias, m_ab_norm, m_ab_conv_w[0], m_ab_conv_b, m_ab_ln_g, m_ab_ln_b, m_sc_norm, m_sc_conv_w[0],
            m_mlp_norm, m_final_norm]
    sm_v = [v_rel_bias, v_ab_norm, v_ab_conv_w[0], v_ab_conv_b, v_ab_ln_g, v_ab_ln_b, v_sc_norm, v_sc_conv_w[0],
            v_mlp_norm, v_final_norm]
    sh_shapes = [tuple(t.shape) for t in tot]
    rows = _pack_rows(sh_shapes)
    sm_upd = _adamw(_pack(sm_w, rows), _pack(sm_m, rows), _pack(sm_v, rows), [_pack(tot, rows)], "adamw_small")
    sm_g, sm_d, sm_nm, sm_nv = [_unpack(buf, sh_shapes) for buf in sm_upd]

    loss = lax.psum(loss_part[0, 0], ("x", "y", "c"))

    def assemble(big, sm):
        up = jnp.stack([big[4], big[5]], axis=0)
        dn = jnp.stack([big[6], big[7]], axis=0)
        return [sm[0], sm[1], big[0][None], sm[2][None], sm[3], sm[4], sm[5], big[1][None], sm[6], big[2][None],
                sm[7][None], big[3][None], sm[8], up, dn, sm[9]]

    res = [loss, grad_x[None]]
    for kind, sm in enumerate((sm_g, sm_d, sm_nm, sm_nv)):
        res += assemble([u[kind] for u in upd], sm)
    return tuple(res)
```

```python
import functools
import math

import jax
import jax.numpy as jnp
from jax import lax
from jax.experimental import pallas as pl
from jax.experimental.pallas import tpu as pltpu

F32 = jnp.float32
BF16 = jnp.bfloat16
MESH = pl.DeviceIdType.MESH

_GROUPS = ((128, 1), (512, 4), (2048, 16))
_STEPS = 128
_HEAD_DIM = 64
_GROUP_COLS = 512
_NUM_BUCKETS = 32
_MAX_DISTANCE = 2048
_CONV_K = 31
_HALO = 32
_SC_HALO = 16
_RMS_EPS = 1e-6
_LN_EPS = 1e-5
_NEG = -1e30
_LANES = 128
_VMEM_LIMIT = 56 * 1024 * 1024

_LR, _B1, _B2, _EPS, _WD, _STEP = 0.001, 0.9, 0.999, 1e-08, 0.01, 10


class _Rider:
    def __init__(self, ins, out_shapes, scratch, start, finish):
        self.ins, self.out_shapes, self.scratch = list(ins), list(out_shapes), list(scratch)
        self.start, self.finish = start, finish


def _pcall(body, *, name, out_shape, in_specs, out_specs, grid=None, scratch=(), aliases=None, rider=None):
    kw = {} if grid is None else {"grid": grid}
    cparams = pltpu.CompilerParams(vmem_limit_bytes=_VMEM_LIMIT)
    if rider is None:
        return pl.pallas_call(
            body, name=name, out_shape=out_shape, in_specs=in_specs, out_specs=out_specs,
            scratch_shapes=list(scratch), input_output_aliases=aliases or {},
            compiler_params=cparams, **kw)
    single = not isinstance(out_specs, (list, tuple))
    ospecs = [out_specs] if single else list(out_specs)
    oshapes = [out_shape] if single else list(out_shape)
    nin, nout, nscr = len(in_specs), len(ospecs), len(scratch)
    rin, rout = len(rider.ins), len(rider.out_shapes)

    def wrapped(*refs):
        h_in, r_in = refs[:nin], refs[nin:nin + rin]
        p = nin + rin
        h_out, r_out = refs[p:p + nout], refs[p + nout:p + nout + rout]
        p += nout + rout
        h_scr, r_scr = refs[p:p + nscr], refs[p + nscr:]
        ids = [pl.program_id(a) for a in range(len(grid))]
        first = functools.reduce(jnp.logical_and, [i == 0 for i in ids])
        last = functools.reduce(jnp.logical_and, [i == g - 1 for i, g in zip(ids, grid)])

        @pl.when(first)
        def _():
            rider.start(r_in, r_out, r_scr)

        body(*h_in, *h_out, *h_scr)

        @pl.when(last)
        def _():
            rider.finish(r_in, r_out, r_scr)

    call = pl.pallas_call(
        wrapped, name=name, out_shape=oshapes + rider.out_shapes,
        in_specs=list(in_specs) + [_ANY] * rin, out_specs=ospecs + [_ANY] * rout,
        scratch_shapes=list(scratch) + rider.scratch, input_output_aliases=aliases or {},
        compiler_params=cparams, **kw)

    def run(*operands):
        res = call(*operands, *rider.ins)
        host = res[0] if single else list(res[:nout])
        return host, list(res[nout:])

    return run


def _sig(x):
    return 1.0 / (1.0 + jnp.exp(-x))


_ANY = pl.BlockSpec(memory_space=pl.ANY)


def _lanes_of(ref):
    parts = [ref[p] for p in range(ref.shape[0])]
    return parts[0] if len(parts) == 1 else jnp.concatenate(parts, axis=1)


def _mm(a, b, mode, *, m, n, k, tm, tn, tk, out_dtype, name, epi=None, extras=(), b_off=(0, 0), rider=None,
        split=""):
    nk = k // tk
    assert m % tm == 0 and n % tn == 0 and k % tk == 0
    o0, o1 = b_off
    if mode == "nn":
        a_spec = pl.BlockSpec((tm, tk), lambda i, j, kk: (i, kk))
        b_spec = pl.BlockSpec((tk, tn), lambda i, j, kk: (kk + o0, j + o1))
        dn = (((1,), (0,)), ((), ()))
    elif mode == "nt":
        a_spec = pl.BlockSpec((tm, tk), lambda i, j, kk: (i, kk))
        if "a" in split:
            a_spec = pl.BlockSpec((tk // _LANES, tm, _LANES), lambda i, j, kk: (kk, i, 0))
        b_spec = pl.BlockSpec((tn, tk), lambda i, j, kk: (j + o0, kk + o1))
        dn = (((1,), (1,)), ((), ()))
    else:
        a_spec = pl.BlockSpec((tk, tm), lambda i, j, kk: (kk, i))
        b_spec = pl.BlockSpec((tk, tn), lambda i, j, kk: (kk + o0, j + o1))
        if "b" in split:
            b_spec = pl.BlockSpec((tn // _LANES, tk, _LANES), lambda i, j, kk: (j, kk, 0))
        dn = (((0,), (0,)), ((), ()))
    o_spec = pl.BlockSpec((tm, tn), lambda i, j, kk: (i, j))
    e_spec = o_spec
    if "o" in split:
        o_spec = pl.BlockSpec((tn // _LANES, tm, _LANES), lambda i, j, kk: (j, i, 0))
    ne = len(extras)
    multi = isinstance(out_dtype, tuple)
    dts = out_dtype if multi else (out_dtype,)
    no = len(dts)

    def body(a_ref, b_ref, *rest):
        ex, o_refs = rest[:ne], rest[ne:ne + no]
        av = _lanes_of(a_ref) if "a" in split else a_ref[...]
        bv = _lanes_of(b_ref) if "b" in split else b_ref[...]
        if av.dtype != BF16:
            av = av.astype(BF16)
        if bv.dtype != BF16:
            bv = bv.astype(BF16)
        p = lax.dot_general(av, bv, dn, preferred_element_type=F32)

        def fin(x):
            if epi is not None:
                x = epi(x, *[e[...] for e in ex])
            for o_ref, val, dt in zip(o_refs, x if multi else (x,), dts):
                if "o" in split:
                    for p in range(tn // _LANES):
                        o_ref[p] = val[:, p * _LANES:(p + 1) * _LANES].astype(dt)
                else:
                    o_ref[...] = val.astype(dt)

        if nk == 1:
            fin(p)
        else:
            acc = rest[ne + no]
            kk = pl.program_id(2)

            @pl.when(kk == 0)
            def _():
                acc[...] = p

            @pl.when(kk > 0)
            def _():
                acc[...] += p

            @pl.when(kk == nk - 1)
            def _():
                fin(acc[...])

    oshape = (n // _LANES, m, _LANES) if "o" in split else (m, n)
    shapes = [jax.ShapeDtypeStruct(oshape, dt) for dt in dts]
    return _pcall(
        body, name=name, grid=(m // tm, n // tn, nk),
        in_specs=[a_spec, b_spec] + [e_spec] * ne, out_specs=[o_spec] * no if multi else o_spec,
        out_shape=shapes if multi else shapes[0],
        scratch=[pltpu.VMEM((tm, tn), F32)] if nk > 1 else [], rider=rider,
    )(a, b, *extras)


def _epi_add(x, r):
    return x + r


def _epi_relu_sq(x):
    x = jnp.maximum(x, 0.0)
    return x * x, x


def _epi_relu_sq_bwd(da, zr):
    return da * (2.0 * zr.astype(F32))


_ROW_T = 512


def _rms_fwd(h, g, name, rider=None):
    s, d = h.shape

    def body(h_ref, g_ref, o_ref):
        x = h_ref[...]
        r = lax.rsqrt(jnp.mean(x * x, axis=-1, keepdims=True) + _RMS_EPS)
        o_ref[...] = (x * r * g_ref[...]).astype(BF16)

    row = pl.BlockSpec((_ROW_T, d), lambda i: (i, 0))
    vec = pl.BlockSpec((1, d), lambda i: (0, 0))
    return _pcall(body, name=name, grid=(s // _ROW_T,), in_specs=[row, vec], out_specs=row,
                  out_shape=jax.ShapeDtypeStruct((s, d), BF16), rider=rider)(h, g)


def _rms_bwd_math(dn, x, g):
    r = lax.rsqrt(jnp.mean(x * x, axis=-1, keepdims=True) + _RMS_EPS)
    xhat = x * r
    dg = jnp.sum(dn * xhat, axis=0, keepdims=True)
    t = dn * g
    dx = r * (t - xhat * jnp.mean(t * xhat, axis=-1, keepdims=True))
    return dx, dg


def _rms_bwd(dn, h, g, dh_in, name):
    s, d = h.shape

    def body(dn_ref, h_ref, g_ref, dhi_ref, dh_ref, dhb_ref, dg_ref):
        dx, dg = _rms_bwd_math(dn_ref[...], h_ref[...], g_ref[...])
        dh = dhi_ref[...] + dx
        dh_ref[...] = dh
        dhb_ref[...] = dh.astype(BF16)

        @pl.when(pl.program_id(0) == 0)
        def _():
            dg_ref[...] = jnp.zeros_like(dg_ref)

        dg_ref[...] += dg

    row = pl.BlockSpec((_ROW_T, d), lambda i: (i, 0))
    vec = pl.BlockSpec((1, d), lambda i: (0, 0))
    return _pcall(
        body, name=name, grid=(s // _ROW_T,), in_specs=[row, row, vec, row], out_specs=[row, row, vec],
        out_shape=[jax.ShapeDtypeStruct((s, d), F32), jax.ShapeDtypeStruct((s, d), BF16),
                   jax.ShapeDtypeStruct((1, d), F32)])(dn, h, g, dh_in)


def _loss_head(h, tgt, g):
    s, d = h.shape

    def body(h_ref, t_ref, g_ref, dh_ref, dhb_ref, dg_ref, loss_ref):
        x, gv = h_ref[...], g_ref[...]
        r = lax.rsqrt(jnp.mean(x * x, axis=-1, keepdims=True) + _RMS_EPS)
        err = x * r * gv - t_ref[...]
        part = 0.5 * jnp.sum(jnp.mean(err * err, axis=-1, keepdims=True))
        dx, dg = _rms_bwd_math(err * (1.0 / d), x, gv)
        dh_ref[...] = dx
        dhb_ref[...] = dx.astype(BF16)

        @pl.when(pl.program_id(0) == 0)
        def _():
            dg_ref[...] = jnp.zeros_like(dg_ref)
            loss_ref[...] = jnp.zeros_like(loss_ref)

        dg_ref[...] += dg
        loss_ref[...] += jnp.full(loss_ref.shape, part, F32)

    row = pl.BlockSpec((_ROW_T, d), lambda i: (i, 0))
    vec = pl.BlockSpec((1, d), lambda i: (0, 0))
    one = pl.BlockSpec((1, _LANES), lambda i: (0, 0))
    return _pcall(
        body, name="loss_head", grid=(s // _ROW_T,), in_specs=[row, row, vec], out_specs=[row, row, vec, one],
        out_shape=[jax.ShapeDtypeStruct((s, d), F32), jax.ShapeDtypeStruct((s, d), BF16),
                   jax.ShapeDtypeStruct((1, d), F32), jax.ShapeDtypeStruct((1, _LANES), F32)])(h, tgt, g)


_CONV_T = 256
_CONV_RC = 64


def _conv_a_specs(s):
    c = _GROUP_COLS
    hb = _CONV_T // _HALO
    val = pl.BlockSpec((_CONV_T, c), lambda i: (i, 0))
    gate = pl.BlockSpec((_CONV_T, c), lambda i: (i, 1))
    hval = pl.BlockSpec((_HALO, c), lambda i: (jnp.maximum(i * hb - 1, 0), 0))
    hgate = pl.BlockSpec((_HALO, c), lambda i: (jnp.maximum(i * hb - 1, 0), 1))
    return val, gate, hval, hgate


def _fill_glu(val_ref, gate_ref, hval_ref, hgate_ref, hs_ref):
    i = pl.program_id(0)
    hs_ref[pl.ds(_HALO, _CONV_T), :] = val_ref[...].astype(F32) * _sig(gate_ref[...].astype(F32))
    halo = hval_ref[...].astype(F32) * _sig(hgate_ref[...].astype(F32))
    hs_ref[pl.ds(0, _HALO), :] = jnp.where(i > 0, halo, 0.0)


_SHIFT_ROWS = _CONV_T + _HALO - 8


def _fill_shifts(src_ref, sh_ref):
    for b in range(1, 8):
        sh_ref[b - 1] = src_ref[pl.ds(b, _SHIFT_ROWS), :]


def _tap_rows(src_ref, sh_ref, start, rows):
    b = start % 8
    if b == 0:
        return src_ref[pl.ds(start, rows), :]
    return sh_ref[b - 1, pl.ds(start - b, rows), :]


def _conv_rows(hs_ref, sh_ref, w_ref, r0, rows):
    off = _HALO - (_CONV_K - 1)
    acc = jnp.zeros((rows, _GROUP_COLS), F32)
    for kk in range(_CONV_K):
        acc = acc + w_ref[kk:kk + 1, :] * _tap_rows(hs_ref, sh_ref, r0 + off + kk, rows)
    return acc


def _ln_fwd(ca, g, b):
    mu = jnp.mean(ca, axis=-1, keepdims=True)
    xc = ca - mu
    rstd = lax.rsqrt(jnp.mean(xc * xc, axis=-1, keepdims=True) + _LN_EPS)
    xhat = xc * rstd
    return xhat, rstd, xhat * g + b


def _conv_a_fwd(uc, w, cb, lg, lb, rider=None):
    s = uc.shape[0]
    c = _GROUP_COLS

    def body(val_ref, gate_ref, hval_ref, hgate_ref, w_ref, cb_ref, lg_ref, lb_ref, o_ref, ca_ref, hs_ref, sh_ref):
        _fill_glu(val_ref, gate_ref, hval_ref, hgate_ref, hs_ref)
        _fill_shifts(hs_ref, sh_ref)
        for rc in range(_CONV_T // _CONV_RC):
            r0 = rc * _CONV_RC
            ca = _conv_rows(hs_ref, sh_ref, w_ref, r0, _CONV_RC) + cb_ref[...]
            ca_ref[pl.ds(r0, _CONV_RC), :] = ca
            _, _, ln = _ln_fwd(ca, lg_ref[...], lb_ref[...])
            o_ref[pl.ds(r0, _CONV_RC), :] = (ln * _sig(ln)).astype(BF16)

    val, gate, hval, hgate = _conv_a_specs(s)
    wspec = pl.BlockSpec((_CONV_K, c), lambda i: (0, 0))
    vec = pl.BlockSpec((1, c), lambda i: (0, 0))
    blk = pl.BlockSpec((_CONV_T, c), lambda i: (i, 0))
    return _pcall(
        body, name="conv_a_fwd", grid=(s // _CONV_T,),
        in_specs=[val, gate, hval, hgate, wspec, vec, vec, vec],
        out_specs=[blk, blk],
        out_shape=[jax.ShapeDtypeStruct((s, 2 * c), BF16), jax.ShapeDtypeStruct((s, c), F32)],
        scratch=[pltpu.VMEM((_CONV_T + _HALO, c), F32), pltpu.VMEM((7, _SHIFT_ROWS, c), F32)],
        rider=rider)(uc, uc, uc, uc, w, cb, lg, lb)


def _conv_a_bwd_ln(ca_all, dcat, lg, lb, rider=None):
    s = ca_all.shape[0]
    c = _GROUP_COLS

    def body(ca_ref, dy_ref, lg_ref, lb_ref, dca_ref, st_ref):
        @pl.when(pl.program_id(0) == 0)
        def _():
            st_ref[...] = jnp.zeros_like(st_ref)

        for rc in range(_CONV_T // _CONV_RC):
            r0 = rc * _CONV_RC
            ca = ca_ref[pl.ds(r0, _CONV_RC), :]
            xhat, rstd, ln = _ln_fwd(ca, lg_ref[...], lb_ref[...])
            sg = _sig(ln)
            dln = dy_ref[pl.ds(r0, _CONV_RC), :] * (sg * (1.0 + ln * (1.0 - sg)))
            dxh = dln * lg_ref[...]
            dca = rstd * (dxh - jnp.mean(dxh, axis=-1, keepdims=True)
                          - xhat * jnp.mean(dxh * xhat, axis=-1, keepdims=True))
            dca_ref[pl.ds(r0, _CONV_RC), :] = dca
            st_ref[0:1, :] += jnp.sum(dca, axis=0, keepdims=True)
            st_ref[1:2, :] += jnp.sum(dln * xhat, axis=0, keepdims=True)
            st_ref[2:3, :] += jnp.sum(dln, axis=0, keepdims=True)

    blk = pl.BlockSpec((_CONV_T, c), lambda i: (i, 0))
    vec = pl.BlockSpec((1, c), lambda i: (0, 0))
    st = pl.BlockSpec((8, c), lambda i: (0, 0))
    return _pcall(
        body, name="conv_a_bwd_ln", grid=(s // _CONV_T,),
        in_specs=[blk, blk, vec, vec], out_specs=[blk, st],
        out_shape=[jax.ShapeDtypeStruct((s, c), F32), jax.ShapeDtypeStruct((8, c), F32)],
        rider=rider)(ca_all, dcat, lg, lb)


def _conv_a_bwd_conv(uc, dca, w):
    s = uc.shape[0]
    c = _GROUP_COLS
    nblk = s // _CONV_T
    hb = _CONV_T // _HALO
    off = _HALO - (_CONV_K - 1)

    def body(val_ref, gate_ref, hval_ref, hgate_ref, d_ref, dn_ref, w_ref, du_ref, dw_ref, hs_ref, ds_ref,
             hsh_ref, dsh_ref):
        i = pl.program_id(0)
        _fill_glu(val_ref, gate_ref, hval_ref, hgate_ref, hs_ref)
        ds_ref[pl.ds(0, _CONV_T), :] = d_ref[...]
        ds_ref[pl.ds(_CONV_T, _HALO), :] = jnp.where(i < nblk - 1, dn_ref[...], 0.0)
        _fill_shifts(hs_ref, hsh_ref)
        _fill_shifts(ds_ref, dsh_ref)

        @pl.when(i == 0)
        def _():
            dw_ref[...] = jnp.zeros_like(dw_ref)

        for rc in range(_CONV_T // _CONV_RC):
            r0 = rc * _CONV_RC
            dcur = ds_ref[pl.ds(r0, _CONV_RC), :]
            dh = jnp.zeros((_CONV_RC, c), F32)
            for kk in range(_CONV_K):
                dh = dh + w_ref[kk:kk + 1, :] * _tap_rows(ds_ref, dsh_ref, r0 + _CONV_K - 1 - kk, _CONV_RC)
                dw_ref[kk:kk + 1, :] += jnp.sum(dcur * _tap_rows(hs_ref, hsh_ref, r0 + off + kk, _CONV_RC),
                                                 axis=0, keepdims=True)
            v = val_ref[pl.ds(r0, _CONV_RC), :].astype(F32)
            sg = _sig(gate_ref[pl.ds(r0, _CONV_RC), :].astype(F32))
            du_ref[pl.ds(r0, _CONV_RC), pl.ds(0, c)] = (dh * sg).astype(BF16)
            du_ref[pl.ds(r0, _CONV_RC), pl.ds(c, c)] = (dh * v * sg * (1.0 - sg)).astype(BF16)

    val, gate, hval, hgate = _conv_a_specs(s)
    blk = pl.BlockSpec((_CONV_T, c), lambda i: (i, 0))
    nxt = pl.BlockSpec((_HALO, c), lambda i: (jnp.minimum((i + 1) * hb, s // _HALO - 1), 0))
    wspec = pl.BlockSpec((_CONV_K, c), lambda i: (0, 0))
    return _pcall(
        body, name="conv_a_bwd_conv", grid=(nblk,),
        in_specs=[val, gate, hval, hgate, blk, nxt, wspec],
        out_specs=[pl.BlockSpec((_CONV_T, 2 * c), lambda i: (i, 0)), pl.BlockSpec((_HALO, c), lambda i: (0, 0))],
        out_shape=[jax.ShapeDtypeStruct((s, 2 * c), BF16), jax.ShapeDtypeStruct((_HALO, c), F32)],
        scratch=[pltpu.VMEM((_CONV_T + _HALO, c), F32), pltpu.VMEM((_CONV_T + _HALO, c), F32),
                 pltpu.VMEM((7, _SHIFT_ROWS, c), F32), pltpu.VMEM((7, _SHIFT_ROWS, c), F32)],
    )(uc, uc, uc, uc, dca, dca, w)


_SC_T = 256


def _short_conv_fwd(u2, w):
    s, d3 = u2.shape
    d = d3 // 3
    hb = _SC_T // _SC_HALO

    def body(b_ref, c_ref, v_ref, hc_ref, hv_ref, w_ref, o_ref, cs_ref):
        i = pl.program_id(0)
        cs_ref[pl.ds(_SC_HALO, _SC_T), :] = c_ref[...].astype(F32) * v_ref[...].astype(F32)
        cs_ref[pl.ds(0, _SC_HALO), :] = jnp.where(i > 0, hc_ref[...].astype(F32) * hv_ref[...].astype(F32), 0.0)
        conv = (w_ref[0:1, :] * cs_ref[pl.ds(_SC_HALO - 2, _SC_T), :]
                + w_ref[1:2, :] * cs_ref[pl.ds(_SC_HALO - 1, _SC_T), :]
                + w_ref[2:3, :] * cs_ref[pl.ds(_SC_HALO, _SC_T), :])
        o_ref[...] = (b_ref[...].astype(F32) * conv).astype(BF16)

    def col(j):
        return pl.BlockSpec((_SC_T, d), lambda i: (i, j))

    def halo(j):
        return pl.BlockSpec((_SC_HALO, d), lambda i: (jnp.maximum(i * hb - 1, 0), j))

    return _pcall(
        body, name="short_conv_fwd", grid=(s // _SC_T,),
        in_specs=[col(0), col(1), col(2), halo(1), halo(2), pl.BlockSpec((3, d), lambda i: (0, 0))],
        out_specs=pl.BlockSpec((_SC_T, d), lambda i: (i, 0)),
        out_shape=jax.ShapeDtypeStruct((s, d), BF16),
        scratch=[pltpu.VMEM((_SC_T + _SC_HALO, d), F32)])(u2, u2, u2, u2, u2, w)


def _short_conv_bwd(u2, dsc, w, rider=None):
    s, d3 = u2.shape
    d = d3 // 3
    hb = _SC_T // _SC_HALO
    nblk = s // _SC_T

    def body(b_ref, c_ref, v_ref, hc_ref, hv_ref, nb_ref, d_ref, nd_ref, w_ref, du_ref, dw_ref, cs_ref, ds_ref):
        i = pl.program_id(0)
        cval, vval, bval = c_ref[...].astype(F32), v_ref[...].astype(F32), b_ref[...].astype(F32)
        cs_ref[pl.ds(_SC_HALO, _SC_T), :] = cval * vval
        cs_ref[pl.ds(0, _SC_HALO), :] = jnp.where(i > 0, hc_ref[...].astype(F32) * hv_ref[...].astype(F32), 0.0)
        dsc_cur = d_ref[...]
        dconv = dsc_cur * bval
        ds_ref[pl.ds(0, _SC_T), :] = dconv
        ds_ref[pl.ds(_SC_T, _SC_HALO), :] = jnp.where(i < nblk - 1, nd_ref[...] * nb_ref[...].astype(F32), 0.0)
        taps = [cs_ref[pl.ds(_SC_HALO - 2 + kk, _SC_T), :] for kk in range(3)]
        conv = w_ref[0:1, :] * taps[0] + w_ref[1:2, :] * taps[1] + w_ref[2:3, :] * taps[2]
        dcv = (w_ref[2:3, :] * dconv + w_ref[1:2, :] * ds_ref[pl.ds(1, _SC_T), :]
               + w_ref[0:1, :] * ds_ref[pl.ds(2, _SC_T), :])
        du_ref[:, pl.ds(0, d)] = (dsc_cur * conv).astype(BF16)
        du_ref[:, pl.ds(d, d)] = (dcv * vval).astype(BF16)
        du_ref[:, pl.ds(2 * d, d)] = (dcv * cval).astype(BF16)

        @pl.when(i == 0)
        def _():
            dw_ref[...] = jnp.zeros_like(dw_ref)

        for kk in range(3):
            dw_ref[kk:kk + 1, :] += jnp.sum(dconv * taps[kk], axis=0, keepdims=True)

    def col(j):
        return pl.BlockSpec((_SC_T, d), lambda i: (i, j))

    def halo(j):
        return pl.BlockSpec((_SC_HALO, d), lambda i: (jnp.maximum(i * hb - 1, 0), j))

    def nxt(j):
        return pl.BlockSpec((_SC_HALO, d), lambda i: (jnp.minimum((i + 1) * hb, s // _SC_HALO - 1), j))

    return _pcall(
        body, name="short_conv_bwd", grid=(nblk,),
        in_specs=[col(0), col(1), col(2), halo(1), halo(2), nxt(0), col(0), nxt(0),
                  pl.BlockSpec((3, d), lambda i: (0, 0))],
        out_specs=[pl.BlockSpec((_SC_T, d3), lambda i: (i, 0)), pl.BlockSpec((8, d), lambda i: (0, 0))],
        out_shape=[jax.ShapeDtypeStruct((s, d3), BF16), jax.ShapeDtypeStruct((8, d), F32)],
        scratch=[pltpu.VMEM((_SC_T + _SC_HALO, d), F32), pltpu.VMEM((_SC_T + _SC_HALO, d), F32)],
        rider=rider,
    )(u2, u2, u2, u2, u2, u2, dsc, dsc, w)


def _bucket_maps():
    a_idx = jnp.arange(_STEPS)[:, None]
    c_idx = jnp.arange(2 * _STEPS)[None, :]
    mdist = jnp.clip(a_idx + _STEPS - c_idx, 0, _STEPS)
    max_exact = _NUM_BUCKETS // 2
    maps = []
    for _, dil in _GROUPS:
        nn = mdist * dil
        nf = jnp.maximum(nn, 1).astype(F32)
        large = max_exact + (jnp.log(nf / max_exact) / math.log(_MAX_DISTANCE / max_exact)
                             * (_NUM_BUCKETS - max_exact)).astype(jnp.int32)
        maps.append(jnp.where(nn < max_exact, nn, jnp.minimum(large, _NUM_BUCKETS - 1)).astype(jnp.int32))
    return jnp.stack(maps, axis=0)


def _bias_expand(rel_bias, buckets):
    nh = rel_bias.shape[1]

    def body(rb_ref, bk_ref, o_ref):
        h = pl.program_id(0)
        bk = bk_ref[0]
        acc = jnp.zeros(bk.shape, F32)
        for b in range(_NUM_BUCKETS):
            acc = jnp.where(bk == b, rb_ref[b, h], acc)
        a = lax.broadcasted_iota(jnp.int32, bk.shape, 0)
        c = lax.broadcasted_iota(jnp.int32, bk.shape, 1)
        mdist = a + _STEPS - c
        o_ref[0] = jnp.where((mdist >= 0) & (mdist <= _STEPS), acc, _NEG)

    return _pcall(
        body, name="bias_expand", grid=(nh,),
        in_specs=[pl.BlockSpec(memory_space=pltpu.SMEM),
                  pl.BlockSpec((1, _STEPS, 2 * _STEPS), lambda h: (h // 8, 0, 0))],
        out_specs=pl.BlockSpec((1, _STEPS, 2 * _STEPS), lambda h: (h, 0, 0)),
        out_shape=jax.ShapeDtypeStruct((nh, _STEPS, 2 * _STEPS), F32))(rel_bias, buckets)


def _bias_reduce(ds_all, buckets):
    nh = ds_all.shape[0]

    def body(ds_ref, bk_ref, o_ref):
        t, bk = ds_ref[0], bk_ref[0]
        rows = lax.broadcasted_iota(jnp.int32, (_NUM_BUCKETS, _LANES), 0)
        out = jnp.zeros((_NUM_BUCKETS, _LANES), F32)
        for b in range(_NUM_BUCKETS):
            out = jnp.where(rows == b, jnp.sum(jnp.where(bk == b, t, 0.0)), out)
        o_ref[0] = out

    blk = pl.BlockSpec((1, _STEPS, 2 * _STEPS), lambda h: (h, 0, 0))
    return _pcall(
        body, name="bias_reduce", grid=(nh,),
        in_specs=[blk, pl.BlockSpec((1, _STEPS, 2 * _STEPS), lambda h: (h // 8, 0, 0))],
        out_specs=pl.BlockSpec((1, _NUM_BUCKETS, _LANES), lambda h: (h, 0, 0)),
        out_shape=jax.ShapeDtypeStruct((nh, _NUM_BUCKETS, _LANES), F32))(ds_all, buckets)


def _strided_rows(ref, p, r, dil):
    if dil == 1:
        return ref[p]
    return ref[p, pl.ds(r, _STEPS, stride=dil), :]


def _store_strided(ref, p, r, dil, val):
    if dil == 1:
        ref[p] = val
    else:
        ref[p, pl.ds(r, _STEPS, stride=dil), :] = val


def _head_masks():
    lane = lax.broadcasted_iota(jnp.int32, (1, _LANES), 1)
    return [lane < _HEAD_DIM, lane >= _HEAD_DIM]


def _scores(qm, k2, bias, first):
    sc = lax.dot_general(qm, k2, (((1,), (1,)), ((), ())), preferred_element_type=F32)
    sc = sc * (_HEAD_DIM ** -0.5) + bias
    col = lax.broadcasted_iota(jnp.int32, sc.shape, 1)
    return jnp.where(jnp.logical_and(first, col < _STEPS), _NEG, sc)


_PAIRS = _GROUP_COLS // _LANES


def _attn_fwd(uq, uk, uv, bias, g, dil, pp, rider=None):
    s = uq.shape[1]
    rb = _STEPS * dil
    nb = s // rb
    npb = _PAIRS // pp

    def body(q_ref, kc_ref, kp_ref, vc_ref, vp_ref, b_ref, o_ref, l_ref):
        n, r = pl.program_id(1), pl.program_id(2)
        first = n == 0
        masks = _head_masks()
        for j in range(pp):
            q2 = _strided_rows(q_ref, j, r, dil).astype(BF16)
            k2 = jnp.concatenate([_strided_rows(kp_ref, j, r, dil), _strided_rows(kc_ref, j, r, dil)],
                                 axis=0).astype(BF16)
            v2 = jnp.concatenate([_strided_rows(vp_ref, j, r, dil), _strided_rows(vc_ref, j, r, dil)],
                                 axis=0).astype(BF16)
            o_pair = jnp.zeros((_STEPS, _LANES), F32)
            l_pair = jnp.zeros((_STEPS, _LANES), F32)
            for hh in range(2):
                mk = masks[hh]
                sc = _scores(jnp.where(mk, q2, 0), k2, b_ref[2 * j + hh], first)
                mx = jnp.max(sc, axis=-1, keepdims=True)
                p = jnp.exp(sc - mx)
                den = jnp.sum(p, axis=-1, keepdims=True)
                oh = jnp.dot(p.astype(BF16), jnp.where(mk, v2, 0), preferred_element_type=F32)
                o_pair = o_pair + oh / den
                l_pair = jnp.where(mk, mx + jnp.log(den), l_pair)
            _store_strided(o_ref, j, r, dil, o_pair)
            _store_strided(l_ref, j, r, dil, l_pair)

    cur = pl.BlockSpec((pp, rb, _LANES), lambda hb, n, r: (g * npb + hb, n, 0))
    prev = pl.BlockSpec((pp, rb, _LANES), lambda hb, n, r: (g * npb + hb, jnp.maximum(n - 1, 0), 0))
    bspec = pl.BlockSpec((2 * pp, _STEPS, 2 * _STEPS), lambda hb, n, r: (g * npb + hb, 0, 0))
    ospec = pl.BlockSpec((pp, rb, _LANES), lambda hb, n, r: (hb, n, 0))
    sh = jax.ShapeDtypeStruct((_PAIRS, s, _LANES), F32)
    return _pcall(
        body, name=f"attn_fwd_g{g}", grid=(npb, nb, dil),
        in_specs=[cur, cur, prev, cur, prev, bspec], out_specs=[ospec, ospec], out_shape=[sh, sh],
        rider=rider,
    )(uq, uk, uk, uv, uv, bias)


def _attn_merge(outs, lses, cat):
    s = outs[0].shape[1]
    c = _GROUP_COLS

    def body(o0, o1, o2, l0, l1, l2, cat_in, cat_ref, lse_ref):
        del cat_in
        a0, a1, a2 = l0[...], l1[...], l2[...]
        mx = jnp.maximum(jnp.maximum(a0, a1), a2)
        w0, w1, w2 = jnp.exp(a0 - mx), jnp.exp(a1 - mx), jnp.exp(a2 - mx)
        den = w0 + w1 + w2
        y = ((w0 * o0[...] + w1 * o1[...] + w2 * o2[...]) / den).astype(BF16)
        for p in range(_PAIRS):
            cat_ref[:, p * _LANES:(p + 1) * _LANES] = y[p]
        lse_ref[...] = mx + jnp.log(den)

    blk = pl.BlockSpec((_PAIRS, _ROW_T, _LANES), lambda i: (0, i, 0))
    return _pcall(
        body, name="attn_merge", grid=(s // _ROW_T,),
        in_specs=[blk] * 6 + [_ANY],
        out_specs=[pl.BlockSpec((_ROW_T, c), lambda i: (i, 1)), blk],
        out_shape=[jax.ShapeDtypeStruct(cat.shape, BF16), jax.ShapeDtypeStruct((_PAIRS, s, _LANES), F32)],
        aliases={6: 0})(*outs, *lses, cat)


def _attn_delta(dcat, cat):
    s = dcat.shape[0]
    c = _GROUP_COLS
    seg = (jnp.arange(c)[:, None] // _HEAD_DIM == jnp.arange(c)[None, :] // _HEAD_DIM).astype(BF16)

    def body(dy_ref, y_ref, seg_ref, dl_ref, dys_ref):
        dy = dy_ref[...]
        prod = dy * y_ref[...].astype(F32)
        hi = prod.astype(BF16)
        lo = (prod - hi.astype(F32)).astype(BF16)
        dl = (jnp.dot(hi, seg_ref[...], preferred_element_type=F32)
              + jnp.dot(lo, seg_ref[...], preferred_element_type=F32))
        for p in range(_PAIRS):
            dl_ref[p] = dl[:, p * _LANES:(p + 1) * _LANES]
            dys_ref[p] = dy[:, p * _LANES:(p + 1) * _LANES]

    right = pl.BlockSpec((_ROW_T, c), lambda i: (i, 1))
    blk = pl.BlockSpec((_PAIRS, _ROW_T, _LANES), lambda i: (0, i, 0))
    sh = jax.ShapeDtypeStruct((_PAIRS, s, _LANES), F32)
    return _pcall(
        body, name="attn_delta", grid=(s // _ROW_T,),
        in_specs=[right, right, pl.BlockSpec((c, c), lambda i: (0, 0))],
        out_specs=[blk, blk], out_shape=[sh, sh])(dcat, cat, seg)


def _attn_bwd(uq, uk, uv, dys, lse, delta, bias, prev_grads, g, dil, pp, rider=None):
    s = uq.shape[1]
    rb = _STEPS * dil
    nb = s // rb
    npb = _PAIRS // pp
    scale = _HEAD_DIM ** -0.5

    def body(q_ref, kc_ref, kp_ref, vc_ref, vp_ref, dy_ref, l_ref, dl_ref, b_ref, *rest):
        rest = rest[len(prev_grads):]
        dq_ref, dk_ref, dv_ref, dsa_ref, dkc_ref, dvc_ref = rest
        n, r = pl.program_id(1), pl.program_id(2)

        @pl.when(jnp.logical_and(n == 0, r == 0))
        def _():
            dsa_ref[...] = jnp.zeros_like(dsa_ref)

        @pl.when(n == 0)
        def _():
            for j in range(pp):
                dkc_ref[r * pp + j] = jnp.zeros((_STEPS, _LANES), F32)
                dvc_ref[r * pp + j] = jnp.zeros((_STEPS, _LANES), F32)

        @pl.when(n < nb)
        def _():
            first = n == 0
            masks = _head_masks()
            for j in range(pp):
                q2 = _strided_rows(q_ref, j, r, dil).astype(BF16)
                k2 = jnp.concatenate([_strided_rows(kp_ref, j, r, dil), _strided_rows(kc_ref, j, r, dil)],
                                     axis=0).astype(BF16)
                v2 = jnp.concatenate([_strided_rows(vp_ref, j, r, dil), _strided_rows(vc_ref, j, r, dil)],
                                     axis=0).astype(BF16)
                dy2 = _strided_rows(dy_ref, j, r, dil).astype(BF16)
                lse_v = _strided_rows(l_ref, j, r, dil)
                dl_v = _strided_rows(dl_ref, j, r, dil)
                dq_p = jnp.zeros((_STEPS, _LANES), F32)
                dk_p = jnp.zeros((2 * _STEPS, _LANES), F32)
                dv_p = jnp.zeros((2 * _STEPS, _LANES), F32)
                for hh in range(2):
                    mk = masks[hh]
                    lane0 = hh * _HEAD_DIM
                    qm, km, dym = jnp.where(mk, q2, 0), jnp.where(mk, k2, 0), jnp.where(mk, dy2, 0)
                    sc = _scores(qm, k2, b_ref[2 * j + hh], first)
                    p = jnp.exp(sc - lse_v[:, lane0:lane0 + 1])
                    dp = lax.dot_general(dym, v2, (((1,), (1,)), ((), ())), preferred_element_type=F32)
                    ds = p * (dp - dl_v[:, lane0:lane0 + 1])
                    dsa_ref[2 * j + hh] += ds
                    dsb = ds.astype(BF16)
                    dq_p = dq_p + jnp.dot(dsb, km, preferred_element_type=F32)
                    dk_p = dk_p + lax.dot_general(dsb, qm, (((0,), (0,)), ((), ())), preferred_element_type=F32)
                    dv_p = dv_p + lax.dot_general(p.astype(BF16), dym, (((0,), (0,)), ((), ())),
                                                  preferred_element_type=F32)
                dk_p = dk_p * scale
                _store_strided(dq_ref, j, r, dil, dq_p * scale)
                _store_strided(dk_ref, j, r, dil, dkc_ref[r * pp + j] + dk_p[:_STEPS])
                _store_strided(dv_ref, j, r, dil, dvc_ref[r * pp + j] + dv_p[:_STEPS])
                dkc_ref[r * pp + j] = dk_p[_STEPS:]
                dvc_ref[r * pp + j] = dv_p[_STEPS:]

        @pl.when(n == nb)
        def _():
            for j in range(pp):
                _store_strided(dk_ref, j, r, dil, dkc_ref[r * pp + j])
                _store_strided(dv_ref, j, r, dil, dvc_ref[r * pp + j])

    def clamp(n):
        return jnp.minimum(n, nb - 1)

    cur = pl.BlockSpec((pp, rb, _LANES), lambda hb, n, r: (g * npb + hb, clamp(n), 0))
    prev = pl.BlockSpec((pp, rb, _LANES), lambda hb, n, r: (g * npb + hb, jnp.maximum(clamp(n) - 1, 0), 0))
    stat = pl.BlockSpec((pp, rb, _LANES), lambda hb, n, r: (hb, clamp(n), 0))
    bspec = pl.BlockSpec((2 * pp, _STEPS, 2 * _STEPS), lambda hb, n, r: (g * npb + hb, 0, 0))
    dkspec = pl.BlockSpec((pp, rb, _LANES), lambda hb, n, r: (g * npb + hb, jnp.maximum(n - 1, 0), 0))
    dsspec = pl.BlockSpec((2 * pp, _STEPS, 2 * _STEPS), lambda hb, n, r: (hb, 0, 0))
    wide = jax.ShapeDtypeStruct((3 * _PAIRS, s, _LANES), F32)
    np_ = len(prev_grads)
    return _pcall(
        body, name=f"attn_bwd_g{g}", grid=(npb, nb + 1, dil),
        in_specs=[cur, cur, prev, cur, prev, stat, stat, stat, bspec] + [_ANY] * np_,
        out_specs=[cur, dkspec, dkspec, dsspec],
        out_shape=[wide, wide, wide, jax.ShapeDtypeStruct((8, _STEPS, 2 * _STEPS), F32)],
        scratch=[pltpu.VMEM((dil * pp, _STEPS, _LANES), F32), pltpu.VMEM((dil * pp, _STEPS, _LANES), F32)],
        aliases={9 + t: t for t in range(np_)}, rider=rider,
    )(uq, uk, uk, uv, uv, dys, lse, delta, bias, *prev_grads)


def _place():
    x, y, c = lax.axis_index("x"), lax.axis_index("y"), lax.axis_index("c")
    chips = [(1 - x, y), (x, 1 - y), (1 - x, 1 - y)]
    return x, y, c, chips


def _slab(ref, axis, chip, width):
    start = pl.multiple_of(chip * width, width)
    if axis == 0:
        return ref.at[pl.ds(start, width), :]
    return ref.at[:, pl.ds(start, width)]


def _gather_rider(shards, axes):
    nw = len(shards)
    fulls = []
    for sh, ax in zip(shards, axes):
        shape = list(sh.shape)
        shape[ax] *= 4
        fulls.append(jax.ShapeDtypeStruct(tuple(shape), sh.dtype))

    def copies(ins, outs, scr):
        send, recv, loc = scr
        x, y, c, chips = _place()
        mine = 2 * x + y
        own, sends, arrivals = [], [], []
        for t in range(nw):
            width = ins[t].shape[axes[t]]
            own.append(pltpu.make_async_copy(ins[t], _slab(outs[t], axes[t], mine, width), loc.at[t]))
            for j, (px, py) in enumerate(chips):
                sems = dict(send_sem=send.at[3 * t + j], recv_sem=recv.at[3 * t + j],
                            device_id=(px, py, c), device_id_type=MESH)
                sends.append(pltpu.make_async_remote_copy(
                    src_ref=ins[t], dst_ref=_slab(outs[t], axes[t], mine, width), **sems))
                arrivals.append(pltpu.make_async_remote_copy(
                    src_ref=ins[t], dst_ref=_slab(outs[t], axes[t], 2 * px + py, width), **sems))
        return own, sends, arrivals

    def start(ins, outs, scr):
        own, sends, _ = copies(ins, outs, scr)
        for cp in own + sends:
            cp.start()

    def finish(ins, outs, scr):
        own, sends, arrivals = copies(ins, outs, scr)
        for cp in arrivals:
            cp.wait_recv()
        for cp in own:
            cp.wait()
        for cp in sends:
            cp.wait_send()

    return _Rider(shards, fulls, [pltpu.SemaphoreType.DMA((3 * nw,)), pltpu.SemaphoreType.DMA((3 * nw,)),
                                  pltpu.SemaphoreType.DMA((nw,))], start, finish)


def _run_rider(rider, name):
    nin, nout = len(rider.ins), len(rider.out_shapes)

    def body(*refs):
        ins, outs, scr = refs[:nin], refs[nin:nin + nout], refs[nin + nout:]
        rider.start(ins, outs, scr)
        rider.finish(ins, outs, scr)

    return _pcall(body, name=name, in_specs=[_ANY] * nin, out_specs=[_ANY] * nout, out_shape=rider.out_shapes,
                  scratch=rider.scratch)(*rider.ins)


def _scatter_rider(grads, axes):
    nw = len(grads)
    outs_shape = []
    for gr, ax in zip(grads, axes):
        shape = list(gr.shape)
        shape[ax] //= 4
        outs_shape.append(jax.ShapeDtypeStruct((3,) + tuple(shape), gr.dtype))

    def copies(ins, outs, scr):
        send, recv = scr
        x, y, c, chips = _place()
        cps = []
        for t in range(nw):
            width = ins[t].shape[axes[t]] // 4
            for j, (px, py) in enumerate(chips):
                cps.append(pltpu.make_async_remote_copy(
                    src_ref=_slab(ins[t], axes[t], 2 * px + py, width), dst_ref=outs[t].at[j],
                    send_sem=send.at[3 * t + j], recv_sem=recv.at[3 * t + j],
                    device_id=(px, py, c), device_id_type=MESH))
        return cps

    def start(ins, outs, scr):
        for cp in copies(ins, outs, scr):
            cp.start()

    def finish(ins, outs, scr):
        cps = copies(ins, outs, scr)
        for cp in cps:
            cp.wait_recv()
        for cp in cps:
            cp.wait_send()

    return _Rider(grads, outs_shape, [pltpu.SemaphoreType.DMA((3 * nw,)), pltpu.SemaphoreType.DMA((3 * nw,))],
                  start, finish)


def _swap_rider(parts):
    nw = len(parts)

    def copies(ins, outs, scr):
        send, recv = scr
        x, y, c, _ = _place()
        return [pltpu.make_async_remote_copy(
            src_ref=ins[t], dst_ref=outs[t], send_sem=send.at[t], recv_sem=recv.at[t],
            device_id=(x, y, 1 - c), device_id_type=MESH) for t in range(nw)]

    def start(ins, outs, scr):
        for cp in copies(ins, outs, scr):
            cp.start()

    def finish(ins, outs, scr):
        cps = copies(ins, outs, scr)
        for cp in cps:
            cp.wait_recv()
        for cp in cps:
            cp.wait_send()

    return _Rider(parts, [jax.ShapeDtypeStruct(p.shape, p.dtype) for p in parts],
                  [pltpu.SemaphoreType.DMA((nw,)), pltpu.SemaphoreType.DMA((nw,))], start, finish)


def _sum_all_devices(buf, name):
    rows, cols = buf.shape

    def body(in_ref, o_ref, gat_ref, send, recv):
        x, y, c, _ = _place()
        me = 4 * x + 2 * y + c
        gat_ref[me] = in_ref[...]
        started = []
        for mask in range(1, 8):
            fx, fy, fc = (mask >> 2) & 1, (mask >> 1) & 1, mask & 1
            peer = (x + fx * (1 - 2 * x), y + fy * (1 - 2 * y), c + fc * (1 - 2 * c))
            cp = pltpu.make_async_remote_copy(
                src_ref=in_ref, dst_ref=gat_ref.at[me], send_sem=send.at[mask - 1], recv_sem=recv.at[mask - 1],
                device_id=peer, device_id_type=MESH)
            cp.start()
            started.append(cp)
        for cp in started:
            cp.wait_recv()
        for cp in started:
            cp.wait_send()
        acc = gat_ref[0]
        for t in range(1, 8):
            acc = acc + gat_ref[t]
        o_ref[...] = acc

    vm = pl.BlockSpec(memory_space=pltpu.VMEM)
    return _pcall(
        body, name=name, in_specs=[vm], out_specs=vm, out_shape=jax.ShapeDtypeStruct((rows, cols), F32),
        scratch=[pltpu.VMEM((8, rows, cols), F32), pltpu.SemaphoreType.DMA((7,)), pltpu.SemaphoreType.DMA((7,))],
    )(buf)


_UPD_T = 256


def _sum_partials(own, got, name):
    rows, cols = own.shape
    tr = min(_UPD_T, rows)

    def body(own_ref, got_ref, o_ref):
        acc = own_ref[...].astype(F32)
        for j in range(3):
            acc = acc + got_ref[j].astype(F32)
        o_ref[...] = acc

    blk = pl.BlockSpec((tr, cols), lambda i: (i, 0))
    return _pcall(
        body, name=name, grid=(rows // tr,),
        in_specs=[blk, pl.BlockSpec((3, tr, cols), lambda i: (0, i, 0))], out_specs=blk,
        out_shape=jax.ShapeDtypeStruct((rows, cols), F32))(own, got)


def _adamw_math(w, gr, m, v):
    m = _B1 * m + (1.0 - _B1) * gr
    v = _B2 * v + (1.0 - _B2) * (gr * gr)
    m_hat = m / (1.0 - _B1 ** _STEP)
    v_hat = v / (1.0 - _B2 ** _STEP)
    delta = -_LR * (m_hat / (jnp.sqrt(v_hat) + _EPS) + _WD * w)
    return delta, m, v


def _adamw(w, m, v, parts, name):
    rows, cols = w.shape
    tr = min(_UPD_T, rows)
    npart = len(parts)

    def body(w_ref, m_ref, v_ref, *rest):
        p_refs, (g_ref, d_ref, nm_ref, nv_ref) = rest[:npart], rest[npart:]
        gr = p_refs[0][...]
        for p in p_refs[1:]:
            gr = gr + p[...]
        delta, nm, nv = _adamw_math(w_ref[...], gr, m_ref[...], v_ref[...])
        g_ref[...] = gr
        d_ref[...] = delta
        nm_ref[...] = nm
        nv_ref[...] = nv

    blk = pl.BlockSpec((tr, cols), lambda i: (i, 0))
    sh = jax.ShapeDtypeStruct((rows, cols), F32)
    return _pcall(body, name=name, grid=(rows // tr,), in_specs=[blk] * (3 + npart), out_specs=[blk] * 4,
                  out_shape=[sh] * 4)(w, m, v, *parts)


_PACK_W = 1024


def _pack(arrs, rows):
    flat = []
    for a in arrs:
        f = a.reshape(-1).astype(F32)
        pad = (-f.shape[0]) % _PACK_W
        flat.append(jnp.pad(f, (0, pad)))
    f = jnp.concatenate(flat)
    f = jnp.pad(f, (0, rows * _PACK_W - f.shape[0]))
    return f.reshape(rows, _PACK_W)


def _unpack(buf, shapes):
    flat = buf.reshape(-1)
    out, pos = [], 0
    for sh in shapes:
        size = math.prod(sh)
        out.append(flat[pos:pos + size].reshape(sh))
        pos += size + ((-size) % _PACK_W)
    return out


def _pack_rows(shapes):
    total = sum(-(-math.prod(sh) // _PACK_W) for sh in shapes)
    return -(-total // 8) * 8


def kernel(x, rel_bias, ab_norm, ab_w_in, ab_conv_w, ab_conv_b, ab_ln_g, ab_ln_b, ab_w_out, sc_norm, sc_w_in, sc_conv_w, sc_w_out, mlp_norm, mlp_w_up, mlp_w_down, final_norm, loss_target, m_rel_bias, m_ab_norm, m_ab_w_in, m_ab_conv_w, m_ab_conv_b, m_ab_ln_g, m_ab_ln_b, m_ab_w_out, m_sc_norm, m_sc_w_in, m_sc_conv_w, m_sc_w_out, m_mlp_norm, m_mlp_w_up, m_mlp_w_down, m_final_norm, v_rel_bias, v_ab_norm, v_ab_w_in, v_ab_conv_w, v_ab_conv_b, v_ab_ln_g, v_ab_ln_b, v_ab_w_out, v_sc_norm, v_sc_w_in, v_sc_conv_w, v_sc_w_out, v_mlp_norm, v_mlp_w_up, v_mlp_w_down, v_final_norm):
    s, d = x.shape[1], x.shape[2]
    dff = 4 * d
    c = _GROUP_COLS
    chip = 2 * lax.axis_index("x") + lax.axis_index("y")
    on_c0 = (lax.axis_index("c") == 0).astype(F32)
    h0 = x[0]
    tgt = loss_target[0]

    cw_sh, scn_sh, scw_sh = ab_conv_w[0], sc_norm, sc_conv_w[0]
    conv_w_full = lax.dynamic_update_slice(jnp.zeros((_CONV_K, c), F32), cw_sh * on_c0, (0, chip * cw_sh.shape[1]))
    scn_full = lax.dynamic_update_slice(jnp.zeros((1, d), F32), scn_sh * on_c0, (0, chip * scn_sh.shape[1]))
    scw_full = lax.dynamic_update_slice(jnp.zeros((3, d), F32), scw_sh * on_c0, (0, chip * scw_sh.shape[1]))
    small_shapes = [(_CONV_K, c), (1, d), (3, d)]
    small = _sum_all_devices(_pack([conv_w_full, scn_full, scw_full], _pack_rows(small_shapes)), "gather_small")
    conv_w, sc_g, sc_cw = _unpack(small, small_shapes)

    w_shards = [ab_w_in[0], ab_w_out[0], sc_w_in[0], sc_w_out[0], mlp_w_up[0], mlp_w_up[1],
                mlp_w_down[0], mlp_w_down[1]]
    w_axes = [1, 0, 1, 0, 1, 1, 0, 0]
    wb = [w.astype(BF16) for w in w_shards]
    full_w = [None] * 8

    def gather(idx):
        return _gather_rider([wb[t] for t in idx], [w_axes[t] for t in idx])

    def put(idx, got_w):
        for t, w in zip(idx, got_w):
            full_w[t] = w

    buckets = _bucket_maps()
    bias = _bias_expand(rel_bias, buckets)
    n0, got_w = _rms_fwd(h0, ab_norm, "rms_fwd_ab", rider=gather([0]))
    put([0], got_w)
    w_in = full_w[0]
    tm = min(1024, s)
    uc = _mm(n0, w_in, "nn", m=s, n=2 * c, k=d, tm=tm, tn=2 * c, tk=d, out_dtype=BF16, name="proj_conv")
    uq, uk, uv = [
        _mm(n0, w_in, "nn", m=s, n=3 * c, k=d, tm=tm, tn=c, tk=d, out_dtype=F32, name=f"proj_{nm}",
            b_off=(0, 2 + 3 * t), split="o")
        for t, nm in enumerate("qkv")]
    (cat, ca), got_w = _conv_a_fwd(uc, conv_w, ab_conv_b, ab_ln_g, ab_ln_b, rider=gather([1, 4]))
    put([1, 4], got_w)
    outs, lses = [], []
    for g, (_, dil) in enumerate(_GROUPS):
        idx = ([6], [2], [3, 5])[g]
        (o, l), got_w = _attn_fwd(uq, uk, uv, bias, g, dil, 4 if dil <= 4 else 2, rider=gather(idx))
        put(idx, got_w)
        outs.append(o)
        lses.append(l)
    cat, lse = _attn_merge(outs, lses, cat)
    h1 = _mm(cat, full_w[1], "nn", m=s, n=d, k=d, tm=tm, tn=d, tk=d, out_dtype=F32, name="out_ab",
             epi=_epi_add, extras=(h0,))

    def mlp_fwd(h, layer, rider=None):
        nrm = _rms_fwd(h, mlp_norm[layer:layer + 1], f"rms_fwd_mlp{layer}")
        res = _mm(nrm, full_w[4 + layer], "nn", m=s, n=dff, k=d, tm=tm, tn=1024, tk=d, out_dtype=(BF16, BF16),
                  name=f"mlp_up{layer}", epi=_epi_relu_sq, rider=rider)
        if rider is not None:
            res, got_r = res
            put([7], got_r)
        act, zr = res
        hn = _mm(act, full_w[6 + layer], "nn", m=s, n=d, k=dff, tm=tm, tn=d, tk=2048, out_dtype=F32,
                 name=f"mlp_down{layer}", epi=_epi_add, extras=(h,))
        return nrm, act, zr, hn

    n1, act0, zr0, h2 = mlp_fwd(h1, 0, rider=gather([7]))
    _, w_out, w_si, w_so, w_up0, w_up1, w_dn0, w_dn1 = full_w
    w_up, w_dn = [w_up0, w_up1], [w_dn0, w_dn1]
    n2 = _rms_fwd(h2, sc_g, "rms_fwd_sc")
    u2 = _mm(n2, w_si, "nn", m=s, n=3 * d, k=d, tm=tm, tn=1024, tk=d, out_dtype=BF16, name="proj_sc")
    scv = _short_conv_fwd(u2, sc_cw)
    h3 = _mm(scv, w_so, "nn", m=s, n=d, k=d, tm=tm, tn=d, tk=d, out_dtype=F32, name="out_sc",
             epi=_epi_add, extras=(h2,))
    n3, act1, zr1, h4 = mlp_fwd(h3, 1)

    dh4, dh4b, g_final, loss_part = _loss_head(h4, tgt, final_norm.reshape(1, d))
    tkw = min(2048, s)

    big_grads, got, sums = [None] * 8, [None] * 8, [None] * 8

    def scatter(t):
        return _scatter_rider([big_grads[t]], [w_axes[t]])

    def own_slab(t):
        width = big_grads[t].shape[w_axes[t]] // 4
        return lax.dynamic_slice_in_dim(big_grads[t], chip * width, width, axis=w_axes[t])

    def arrived(t, got_t):
        got[t] = got_t[0]
        sums[t] = _sum_partials(own_slab(t), got[t], f"sum_partials{t}")

    def mlp_bwd(dh, dhb, h, nrm, act, zr, layer):
        dz = _mm(dhb, w_dn[layer], "nt", m=s, n=dff, k=d, tm=tm, tn=1024, tk=d, out_dtype=BF16,
                 name=f"mlp_down{layer}_dx", epi=_epi_relu_sq_bwd, extras=(zr,))
        big_grads[6 + layer] = _mm(act, dhb, "tn", m=dff, n=d, k=s, tm=1024, tn=d, tk=tkw, out_dtype=BF16,
                                   name=f"mlp_down{layer}_dw")
        big_grads[4 + layer], got_t = _mm(nrm, dz, "tn", m=d, n=dff, k=s, tm=d, tn=1024, tk=tkw, out_dtype=BF16,
                                          name=f"mlp_up{layer}_dw", rider=scatter(6 + layer))
        arrived(6 + layer, got_t)
        dn, got_t = _mm(dz, w_up[layer], "nt", m=s, n=d, k=dff, tm=tm, tn=d, tk=2048, out_dtype=F32,
                        name=f"mlp_up{layer}_dx", rider=scatter(4 + layer))
        arrived(4 + layer, got_t)
        return _rms_bwd(dn, h, mlp_norm[layer:layer + 1], dh, f"rms_bwd_mlp{layer}")

    dh3, dh3b, g_mn1 = mlp_bwd(dh4, dh4b, h3, n3, act1, zr1, 1)

    dsc = _mm(dh3b, w_so, "nt", m=s, n=d, k=d, tm=tm, tn=d, tk=d, out_dtype=F32, name="out_sc_dx")
    big_grads[3] = _mm(scv, dh3b, "tn", m=d, n=d, k=s, tm=d, tn=d, tk=tkw, out_dtype=BF16, name="out_sc_dw")
    (du2, g_sccw8), got_t = _short_conv_bwd(u2, dsc, sc_cw, rider=scatter(3))
    arrived(3, got_t)
    big_grads[2] = _mm(n2, du2, "tn", m=d, n=3 * d, k=s, tm=d, tn=1024, tk=tkw, out_dtype=BF16,
                       name="proj_sc_dw")
    dn2, got_t = _mm(du2, w_si, "nt", m=s, n=d, k=3 * d, tm=tm, tn=d, tk=1536, out_dtype=F32, name="proj_sc_dx",
                     rider=scatter(2))
    arrived(2, got_t)
    dh2, dh2b, g_scn = _rms_bwd(dn2, h2, sc_g, dh3, "rms_bwd_sc")

    dh1, dh1b, g_mn0 = mlp_bwd(dh2, dh2b, h1, n1, act0, zr0, 0)

    dcat = _mm(dh1b, w_out, "nt", m=s, n=d, k=d, tm=tm, tn=d, tk=d, out_dtype=F32, name="out_ab_dx")
    big_grads[1] = _mm(cat, dh1b, "tn", m=d, n=d, k=s, tm=d, tn=d, tk=tkw, out_dtype=BF16, name="out_ab_dw")
    (dca, conv_stats), got_t = _conv_a_bwd_ln(ca, dcat, ab_ln_g, ab_ln_b, rider=scatter(1))
    arrived(1, got_t)
    duc, g_cw32 = _conv_a_bwd_conv(uc, dca, conv_w)
    delta, dys = _attn_delta(dcat, cat)

    grads_qkv, ds_list = [], []
    for g, (_, dil) in enumerate(_GROUPS):
        dq, dk, dv, dsa = _attn_bwd(uq, uk, uv, dys, lse, delta, bias, grads_qkv, g, dil, 4 if dil <= 4 else 1)
        grads_qkv = [dq, dk, dv]
        ds_list.append(dsa)
    g_bias = _bias_reduce(jnp.concatenate(ds_list, axis=0), buckets)[:, :, 0].T

    secs = [(duc, 2 * c, 0)] + [(grads_qkv[t], 3 * c, 2 + 3 * t) for t in range(3)]
    g_in_parts = []
    for t, (du, width, off) in enumerate(secs):
        rider = _swap_rider(sums[1:]) if t == 1 else None
        part = _mm(n0, du, "tn", m=d, n=width, k=s, tm=d, tn=c, tk=tkw, out_dtype=BF16, name=f"proj_ab_dw{t}",
                   rider=rider, split="b" if t else "")
        if rider is not None:
            part, sib_late = part
        g_in_parts.append(part)
    big_grads[0] = jnp.concatenate(g_in_parts, axis=1)
    dn0 = None
    for t, (du, width, off) in enumerate(secs):
        rider = _scatter_rider([big_grads[0]], [w_axes[0]]) if t == 1 else None
        dn0 = _mm(du, w_in, "nt", m=s, n=d, k=width, tm=tm, tn=d, tk=c, out_dtype=F32, name=f"proj_ab_dx{t}",
                  b_off=(0, off), rider=rider, split="a" if t else "",
                  **({} if dn0 is None else dict(epi=_epi_add, extras=(dn0,))))
        if rider is not None:
            dn0, (got[0],) = dn0
    grad_x, _, g_abn = _rms_bwd(dn0, h0, ab_norm, dh1, "rms_bwd_ab")
    sums[0] = _sum_partials(own_slab(0), got[0], "sum_partials0")
    sib = list(_run_rider(_swap_rider([sums[0]]), "swap_sibling_w_in")) + sib_late

    big_m = [m_ab_w_in[0], m_ab_w_out[0], m_sc_w_in[0], m_sc_w_out[0], m_mlp_w_up[0], m_mlp_w_up[1],
             m_mlp_w_down[0], m_mlp_w_down[1]]
    big_v = [v_ab_w_in[0], v_ab_w_out[0], v_sc_w_in[0], v_sc_w_out[0], v_mlp_w_up[0], v_mlp_w_up[1],
             v_mlp_w_down[0], v_mlp_w_down[1]]
    upd = [_adamw(w_shards[t], big_m[t], big_v[t], [sums[t], sib[t]], f"adamw{t}") for t in range(8)]

    full_shapes = [(_NUM_BUCKETS, rel_bias.shape[1]), (1, d), (_CONV_K, c), (1, c), (1, c), (1, c), (1, d),
                   (3, d), (2, d), (d,)]
    small_grads = [g_bias, g_abn, g_cw32[:_CONV_K], conv_stats[0:1], conv_stats[1:2], conv_stats[2:3], g_scn,
                   g_sccw8[:3], jnp.concatenate([g_mn0, g_mn1], axis=0), g_final.reshape(d)]
    tot = _unpack(_sum_all_devices(_pack(small_grads, _pack_rows(full_shapes)), "sum_small"), full_shapes)
    for idx, sh in ((2, cw_sh), (6, scn_sh), (7, scw_sh)):
        width = sh.shape[1]
        tot[idx] = lax.dynamic_slice_in_dim(tot[idx], chip * width, width, axis=1)
    sm_w = [rel_bias, ab_norm, cw_sh, ab_conv_b, ab_ln_g, ab_ln_b, scn_sh, scw_sh, mlp_norm, final_norm]
    sm_m = [m_rel_bias, m_ab_norm, m_ab_conv_w[0], m_ab_conv_b, m_ab_ln_g, m_ab_ln_b, m_sc_norm, m_sc_conv_w[0],
            m_mlp_norm, m_final_norm]
    sm_v = [v_rel_bias, v_ab_norm, v_ab_conv_w[0], v_ab_conv_b, v_ab_ln_g, v_ab_ln_b, v_sc_norm, v_sc_conv_w[0],
            v_mlp_norm, v_final_norm]
    sh_shapes = [tuple(t.shape) for t in tot]
    rows = _pack_rows(sh_shapes)
    sm_upd = _adamw(_pack(sm_w, rows), _pack(sm_m, rows), _pack(sm_v, rows), [_pack(tot, rows)], "adamw_small")
    sm_g, sm_d, sm_nm, sm_nv = [_unpack(buf, sh_shapes) for buf in sm_upd]

    loss = lax.psum(loss_part[0, 0], ("x", "y", "c"))

    def assemble(big, sm):
        up = jnp.stack([big[4], big[5]], axis=0)
        dn = jnp.stack([big[6], big[7]], axis=0)
        return [sm[0], sm[1], big[0][None], sm[2][None], sm[3], sm[4], sm[5], big[1][None], sm[6], big[2][None],
                sm[7][None], big[3][None], sm[8], up, dn, sm[9]]

    res = [loss, grad_x[None]]
    for kind, sm in enumerate((sm_g, sm_d, sm_nm, sm_nv)):
        res += assemble([u[kind] for u in upd], sm)
    return tuple(res)
```

```python
import functools
import math

import jax
import jax.numpy as jnp
from jax import lax
from jax.experimental import pallas as pl
from jax.experimental.pallas import tpu as pltpu

F32 = jnp.float32
BF16 = jnp.bfloat16
MESH = pl.DeviceIdType.MESH

_GROUPS = ((128, 1), (512, 4), (2048, 16))
_STEPS = 128
_HEAD_DIM = 64
_GROUP_COLS = 512
_NUM_BUCKETS = 32
_MAX_DISTANCE = 2048
_CONV_K = 31
_HALO = 32
_SC_HALO = 16
_RMS_EPS = 1e-6
_LN_EPS = 1e-5
_NEG = -1e30
_LANES = 128
_VMEM_LIMIT = 56 * 1024 * 1024

_LR, _B1, _B2, _EPS, _WD, _STEP = 0.001, 0.9, 0.999, 1e-08, 0.01, 10


class _Rider:
    def __init__(self, ins, out_shapes, scratch, start, finish):
        self.ins, self.out_shapes, self.scratch = list(ins), list(out_shapes), list(scratch)
        self.start, self.finish = start, finish


def _pcall(body, *, name, out_shape, in_specs, out_specs, grid=None, scratch=(), aliases=None, rider=None):
    kw = {} if grid is None else {"grid": grid}
    cparams = pltpu.CompilerParams(vmem_limit_bytes=_VMEM_LIMIT)
    if rider is None:
        return pl.pallas_call(
            body, name=name, out_shape=out_shape, in_specs=in_specs, out_specs=out_specs,
            scratch_shapes=list(scratch), input_output_aliases=aliases or {},
            compiler_params=cparams, **kw)
    single = not isinstance(out_specs, (list, tuple))
    ospecs = [out_specs] if single else list(out_specs)
    oshapes = [out_shape] if single else list(out_shape)
    nin, nout, nscr = len(in_specs), len(ospecs), len(scratch)
    rin, rout = len(rider.ins), len(rider.out_shapes)

    def wrapped(*refs):
        h_in, r_in = refs[:nin], refs[nin:nin + rin]
        p = nin + rin
        h_out, r_out = refs[p:p + nout], refs[p + nout:p + nout + rout]
        p += nout + rout
        h_scr, r_scr = refs[p:p + nscr], refs[p + nscr:]
        ids = [pl.program_id(a) for a in range(len(grid))]
        first = functools.reduce(jnp.logical_and, [i == 0 for i in ids])
        last = functools.reduce(jnp.logical_and, [i == g - 1 for i, g in zip(ids, grid)])

        @pl.when(first)
        def _():
            rider.start(r_in, r_out, r_scr)

        body(*h_in, *h_out, *h_scr)

        @pl.when(last)
        def _():
            rider.finish(r_in, r_out, r_scr)

    call = pl.pallas_call(
        wrapped, name=name, out_shape=oshapes + rider.out_shapes,
        in_specs=list(in_specs) + [_ANY] * rin, out_specs=ospecs + [_ANY] * rout,
        scratch_shapes=list(scratch) + rider.scratch, input_output_aliases=aliases or {},
        compiler_params=cparams, **kw)

    def run(*operands):
        res = call(*operands, *rider.ins)
        host = res[0] if single else list(res[:nout])
        return host, list(res[nout:])

    return run


def _sig(x):
    return 1.0 / (1.0 + jnp.exp(-x))


_ANY = pl.BlockSpec(memory_space=pl.ANY)


def _lanes_of(ref):
    parts = [ref[p] for p in range(ref.shape[0])]
    return parts[0] if len(parts) == 1 else jnp.concatenate(parts, axis=1)


def _mm(a, b, mode, *, m, n, k, tm, tn, tk, out_dtype, name, epi=None, extras=(), b_off=(0, 0), rider=None,
        split="", vecs=(), a_pro=None, row_sum=False):
    nk = k // tk
    assert m % tm == 0 and n % tn == 0 and k % tk == 0
    o0, o1 = b_off
    if mode == "nn":
        a_spec = pl.BlockSpec((tm, tk), lambda i, j, kk: (i, kk))
        b_spec = pl.BlockSpec((tk, tn), lambda i, j, kk: (kk + o0, j + o1))
        dn = (((1,), (0,)), ((), ()))
    elif mode == "nt":
        a_spec = pl.BlockSpec((tm, tk), lambda i, j, kk: (i, kk))
        if "a" in split:
            a_spec = pl.BlockSpec((tk // _LANES, tm, _LANES), lambda i, j, kk: (kk, i, 0))
        b_spec = pl.BlockSpec((tn, tk), lambda i, j, kk: (j + o0, kk + o1))
        dn = (((1,), (1,)), ((), ()))
    else:
        a_spec = pl.BlockSpec((tk, tm), lambda i, j, kk: (kk, i))
        b_spec = pl.BlockSpec((tk, tn), lambda i, j, kk: (kk + o0, j + o1))
        if "b" in split:
            b_spec = pl.BlockSpec((tn // _LANES, tk, _LANES), lambda i, j, kk: (j, kk, 0))
        dn = (((0,), (0,)), ((), ()))
    o_spec = pl.BlockSpec((tm, tn), lambda i, j, kk: (i, j))
    e_spec = o_spec
    if "o" in split:
        o_spec = pl.BlockSpec((tn // _LANES, tm, _LANES), lambda i, j, kk: (j, i, 0))
    v_spec = pl.BlockSpec((1, tn), lambda i, j, kk: (0, j))
    ne = len(extras) + len(vecs)
    multi = isinstance(out_dtype, tuple)
    dts = out_dtype if multi else (out_dtype,)
    no = len(dts)
    nr = 1 if row_sum else 0
    assert not row_sum or tn == n

    def body(a_ref, b_ref, *rest):
        ex, o_refs = rest[:ne], rest[ne:ne + no]
        av = _lanes_of(a_ref) if "a" in split else a_ref[...]
        bv = _lanes_of(b_ref) if "b" in split else b_ref[...]
        if av.dtype != BF16:
            av = av.astype(BF16)
        if bv.dtype != BF16:
            bv = bv.astype(BF16)
        if a_pro is not None:
            av = a_pro(av)
        p = lax.dot_general(av, bv, dn, preferred_element_type=F32)

        def fin(x):
            if epi is not None:
                x = epi(x, *[e[...] for e in ex])
            if row_sum:
                row, x = x[-1], (x[:-1] if multi else x[0])
                row_ref = rest[ne + no]

                @pl.when(pl.program_id(0) == 0)
                def _():
                    row_ref[...] = row

                @pl.when(pl.program_id(0) > 0)
                def _():
                    row_ref[...] += row

            for o_ref, val, dt in zip(o_refs, x if multi else (x,), dts):
                if "o" in split:
                    for p in range(tn // _LANES):
                        o_ref[p] = val[:, p * _LANES:(p + 1) * _LANES].astype(dt)
                else:
                    o_ref[...] = val.astype(dt)

        if nk == 1:
            fin(p)
        else:
            acc = rest[ne + no + nr]
            kk = pl.program_id(2)

            @pl.when(kk == 0)
            def _():
                acc[...] = p

            @pl.when(kk > 0)
            def _():
                acc[...] += p

            @pl.when(kk == nk - 1)
            def _():
                fin(acc[...])

    oshape = (n // _LANES, m, _LANES) if "o" in split else (m, n)
    shapes = [jax.ShapeDtypeStruct(oshape, dt) for dt in dts]
    ospecs = [o_spec] * no
    if row_sum:
        shapes.append(jax.ShapeDtypeStruct((1, n), F32))
        ospecs.append(v_spec)
    lone = not multi and not row_sum
    return _pcall(
        body, name=name, grid=(m // tm, n // tn, nk),
        in_specs=[a_spec, b_spec] + [e_spec] * len(extras) + [v_spec] * len(vecs),
        out_specs=ospecs[0] if lone else ospecs, out_shape=shapes[0] if lone else shapes,
        scratch=[pltpu.VMEM((tm, tn), F32)] if nk > 1 else [], rider=rider,
    )(a, b, *extras, *vecs)


def _epi_add(x, r):
    return x + r


def _epi_relu(x):
    return jnp.maximum(x, 0.0)


def _square(x):
    return x * x


def _epi_relu_sq_bwd(da, zr):
    return da * (2.0 * zr.astype(F32))


def _epi_add_rms(x, r, g):
    h = x + r
    return h, h * lax.rsqrt(jnp.mean(h * h, axis=-1, keepdims=True) + _RMS_EPS) * g


def _epi_rms_bwd(dn, h, dh_in, g):
    dx, dg = _rms_bwd_math(dn, h, g)
    dh = dh_in + dx
    return dh, dh, dg


_ROW_T = 512


def _rms_fwd(h, g, name, rider=None):
    s, d = h.shape

    def body(h_ref, g_ref, o_ref):
        x = h_ref[...]
        r = lax.rsqrt(jnp.mean(x * x, axis=-1, keepdims=True) + _RMS_EPS)
        o_ref[...] = (x * r * g_ref[...]).astype(BF16)

    row = pl.BlockSpec((_ROW_T, d), lambda i: (i, 0))
    vec = pl.BlockSpec((1, d), lambda i: (0, 0))
    return _pcall(body, name=name, grid=(s // _ROW_T,), in_specs=[row, vec], out_specs=row,
                  out_shape=jax.ShapeDtypeStruct((s, d), BF16), rider=rider)(h, g)


def _rms_bwd_math(dn, x, g):
    r = lax.rsqrt(jnp.mean(x * x, axis=-1, keepdims=True) + _RMS_EPS)
    xhat = x * r
    dg = jnp.sum(dn * xhat, axis=0, keepdims=True)
    t = dn * g
    dx = r * (t - xhat * jnp.mean(t * xhat, axis=-1, keepdims=True))
    return dx, dg


def _rms_bwd(dn, h, g, dh_in, name):
    s, d = h.shape

    def body(dn_ref, h_ref, g_ref, dhi_ref, dh_ref, dhb_ref, dg_ref):
        dx, dg = _rms_bwd_math(dn_ref[...], h_ref[...], g_ref[...])
        dh = dhi_ref[...] + dx
        dh_ref[...] = dh
        dhb_ref[...] = dh.astype(BF16)

        @pl.when(pl.program_id(0) == 0)
        def _():
            dg_ref[...] = jnp.zeros_like(dg_ref)

        dg_ref[...] += dg

    row = pl.BlockSpec((_ROW_T, d), lambda i: (i, 0))
    vec = pl.BlockSpec((1, d), lambda i: (0, 0))
    return _pcall(
        body, name=name, grid=(s // _ROW_T,), in_specs=[row, row, vec, row], out_specs=[row, row, vec],
        out_shape=[jax.ShapeDtypeStruct((s, d), F32), jax.ShapeDtypeStruct((s, d), BF16),
                   jax.ShapeDtypeStruct((1, d), F32)])(dn, h, g, dh_in)


def _loss_head(h, tgt, g):
    s, d = h.shape

    def body(h_ref, t_ref, g_ref, dh_ref, dhb_ref, dg_ref, loss_ref):
        x, gv = h_ref[...], g_ref[...]
        r = lax.rsqrt(jnp.mean(x * x, axis=-1, keepdims=True) + _RMS_EPS)
        err = x * r * gv - t_ref[...]
        part = 0.5 * jnp.sum(jnp.mean(err * err, axis=-1, keepdims=True))
        dx, dg = _rms_bwd_math(err * (1.0 / d), x, gv)
        dh_ref[...] = dx
        dhb_ref[...] = dx.astype(BF16)

        @pl.when(pl.program_id(0) == 0)
        def _():
            dg_ref[...] = jnp.zeros_like(dg_ref)
            loss_ref[...] = jnp.zeros_like(loss_ref)

        dg_ref[...] += dg
        loss_ref[...] += jnp.full(loss_ref.shape, part, F32)

    row = pl.BlockSpec((_ROW_T, d), lambda i: (i, 0))
    vec = pl.BlockSpec((1, d), lambda i: (0, 0))
    one = pl.BlockSpec((1, _LANES), lambda i: (0, 0))
    return _pcall(
        body, name="loss_head", grid=(s // _ROW_T,), in_specs=[row, row, vec], out_specs=[row, row, vec, one],
        out_shape=[jax.ShapeDtypeStruct((s, d), F32), jax.ShapeDtypeStruct((s, d), BF16),
                   jax.ShapeDtypeStruct((1, d), F32), jax.ShapeDtypeStruct((1, _LANES), F32)])(h, tgt, g)


_CONV_T = 256
_CONV_RC = 64


def _conv_a_specs(s):
    c = _GROUP_COLS
    hb = _CONV_T // _HALO
    val = pl.BlockSpec((_CONV_T, c), lambda i: (i, 0))
    gate = pl.BlockSpec((_CONV_T, c), lambda i: (i, 1))
    hval = pl.BlockSpec((_HALO, c), lambda i: (jnp.maximum(i * hb - 1, 0), 0))
    hgate = pl.BlockSpec((_HALO, c), lambda i: (jnp.maximum(i * hb - 1, 0), 1))
    return val, gate, hval, hgate


def _fill_glu(val_ref, gate_ref, hval_ref, hgate_ref, hs_ref):
    i = pl.program_id(0)
    hs_ref[pl.ds(_HALO, _CONV_T), :] = val_ref[...].astype(F32) * _sig(gate_ref[...].astype(F32))
    halo = hval_ref[...].astype(F32) * _sig(hgate_ref[...].astype(F32))
    hs_ref[pl.ds(0, _HALO), :] = jnp.where(i > 0, halo, 0.0)


_SHIFT_ROWS = _CONV_T + _HALO - 8


def _fill_shifts(src_ref, sh_ref):
    for b in range(1, 8):
        sh_ref[b - 1] = src_ref[pl.ds(b, _SHIFT_ROWS), :]


def _tap_rows(src_ref, sh_ref, start, rows):
    b = start % 8
    if b == 0:
        return src_ref[pl.ds(start, rows), :]
    return sh_ref[b - 1, pl.ds(start - b, rows), :]


def _conv_rows(hs_ref, sh_ref, w_ref, r0, rows):
    off = _HALO - (_CONV_K - 1)
    acc = jnp.zeros((rows, _GROUP_COLS), F32)
    for kk in range(_CONV_K):
        acc = acc + w_ref[kk:kk + 1, :] * _tap_rows(hs_ref, sh_ref, r0 + off + kk, rows)
    return acc


def _ln_fwd(ca, g, b):
    mu = jnp.mean(ca, axis=-1, keepdims=True)
    xc = ca - mu
    rstd = lax.rsqrt(jnp.mean(xc * xc, axis=-1, keepdims=True) + _LN_EPS)
    xhat = xc * rstd
    return xhat, rstd, xhat * g + b


def _conv_a_fwd(uc, w, cb, lg, lb, rider=None):
    s = uc.shape[0]
    c = _GROUP_COLS

    def body(val_ref, gate_ref, hval_ref, hgate_ref, w_ref, cb_ref, lg_ref, lb_ref, o_ref, ca_ref, hs_ref, sh_ref):
        _fill_glu(val_ref, gate_ref, hval_ref, hgate_ref, hs_ref)
        _fill_shifts(hs_ref, sh_ref)
        for rc in range(_CONV_T // _CONV_RC):
            r0 = rc * _CONV_RC
            ca = _conv_rows(hs_ref, sh_ref, w_ref, r0, _CONV_RC) + cb_ref[...]
            ca_ref[pl.ds(r0, _CONV_RC), :] = ca
            _, _, ln = _ln_fwd(ca, lg_ref[...], lb_ref[...])
            o_ref[pl.ds(r0, _CONV_RC), :] = (ln * _sig(ln)).astype(BF16)

    val, gate, hval, hgate = _conv_a_specs(s)
    wspec = pl.BlockSpec((_CONV_K, c), lambda i: (0, 0))
    vec = pl.BlockSpec((1, c), lambda i: (0, 0))
    blk = pl.BlockSpec((_CONV_T, c), lambda i: (i, 0))
    return _pcall(
        body, name="conv_a_fwd", grid=(s // _CONV_T,),
        in_specs=[val, gate, hval, hgate, wspec, vec, vec, vec],
        out_specs=[blk, blk],
        out_shape=[jax.ShapeDtypeStruct((s, 2 * c), BF16), jax.ShapeDtypeStruct((s, c), F32)],
        scratch=[pltpu.VMEM((_CONV_T + _HALO, c), F32), pltpu.VMEM((7, _SHIFT_ROWS, c), F32)],
        rider=rider)(uc, uc, uc, uc, w, cb, lg, lb)


def _conv_a_bwd_ln(ca_all, dcat, lg, lb, rider=None):
    s = ca_all.shape[0]
    c = _GROUP_COLS

    def body(ca_ref, dy_ref, lg_ref, lb_ref, dca_ref, st_ref):
        @pl.when(pl.program_id(0) == 0)
        def _():
            st_ref[...] = jnp.zeros_like(st_ref)

        for rc in range(_CONV_T // _CONV_RC):
            r0 = rc * _CONV_RC
            ca = ca_ref[pl.ds(r0, _CONV_RC), :]
            xhat, rstd, ln = _ln_fwd(ca, lg_ref[...], lb_ref[...])
            sg = _sig(ln)
            dln = dy_ref[pl.ds(r0, _CONV_RC), :] * (sg * (1.0 + ln * (1.0 - sg)))
            dxh = dln * lg_ref[...]
            dca = rstd * (dxh - jnp.mean(dxh, axis=-1, keepdims=True)
                          - xhat * jnp.mean(dxh * xhat, axis=-1, keepdims=True))
            dca_ref[pl.ds(r0, _CONV_RC), :] = dca
            st_ref[0:1, :] += jnp.sum(dca, axis=0, keepdims=True)
            st_ref[1:2, :] += jnp.sum(dln * xhat, axis=0, keepdims=True)
            st_ref[2:3, :] += jnp.sum(dln, axis=0, keepdims=True)

    blk = pl.BlockSpec((_CONV_T, c), lambda i: (i, 0))
    vec = pl.BlockSpec((1, c), lambda i: (0, 0))
    st = pl.BlockSpec((8, c), lambda i: (0, 0))
    return _pcall(
        body, name="conv_a_bwd_ln", grid=(s // _CONV_T,),
        in_specs=[blk, blk, vec, vec], out_specs=[blk, st],
        out_shape=[jax.ShapeDtypeStruct((s, c), F32), jax.ShapeDtypeStruct((8, c), F32)],
        rider=rider)(ca_all, dcat, lg, lb)


def _conv_a_bwd_conv(uc, dca, w):
    s = uc.shape[0]
    c = _GROUP_COLS
    nblk = s // _CONV_T
    hb = _CONV_T // _HALO
    off = _HALO - (_CONV_K - 1)

    def body(val_ref, gate_ref, hval_ref, hgate_ref, d_ref, dn_ref, w_ref, du_ref, dw_ref, hs_ref, ds_ref,
             hsh_ref, dsh_ref):
        i = pl.program_id(0)
        _fill_glu(val_ref, gate_ref, hval_ref, hgate_ref, hs_ref)
        ds_ref[pl.ds(0, _CONV_T), :] = d_ref[...]
        ds_ref[pl.ds(_CONV_T, _HALO), :] = jnp.where(i < nblk - 1, dn_ref[...], 0.0)
        _fill_shifts(hs_ref, hsh_ref)
        _fill_shifts(ds_ref, dsh_ref)

        @pl.when(i == 0)
        def _():
            dw_ref[...] = jnp.zeros_like(dw_ref)

        for rc in range(_CONV_T // _CONV_RC):
            r0 = rc * _CONV_RC
            dcur = ds_ref[pl.ds(r0, _CONV_RC), :]
            dh = jnp.zeros((_CONV_RC, c), F32)
            for kk in range(_CONV_K):
                dh = dh + w_ref[kk:kk + 1, :] * _tap_rows(ds_ref, dsh_ref, r0 + _CONV_K - 1 - kk, _CONV_RC)
                dw_ref[kk:kk + 1, :] += jnp.sum(dcur * _tap_rows(hs_ref, hsh_ref, r0 + off + kk, _CONV_RC),
                                                 axis=0, keepdims=True)
            v = val_ref[pl.ds(r0, _CONV_RC), :].astype(F32)
            sg = _sig(gate_ref[pl.ds(r0, _CONV_RC), :].astype(F32))
            du_ref[pl.ds(r0, _CONV_RC), pl.ds(0, c)] = (dh * sg).astype(BF16)
            du_ref[pl.ds(r0, _CONV_RC), pl.ds(c, c)] = (dh * v * sg * (1.0 - sg)).astype(BF16)

    val, gate, hval, hgate = _conv_a_specs(s)
    blk = pl.BlockSpec((_CONV_T, c), lambda i: (i, 0))
    nxt = pl.BlockSpec((_HALO, c), lambda i: (jnp.minimum((i + 1) * hb, s // _HALO - 1), 0))
    wspec = pl.BlockSpec((_CONV_K, c), lambda i: (0, 0))
    return _pcall(
        body, name="conv_a_bwd_conv", grid=(nblk,),
        in_specs=[val, gate, hval, hgate, blk, nxt, wspec],
        out_specs=[pl.BlockSpec((_CONV_T, 2 * c), lambda i: (i, 0)), pl.BlockSpec((_HALO, c), lambda i: (0, 0))],
        out_shape=[jax.ShapeDtypeStruct((s, 2 * c), BF16), jax.ShapeDtypeStruct((_HALO, c), F32)],
        scratch=[pltpu.VMEM((_CONV_T + _HALO, c), F32), pltpu.VMEM((_CONV_T + _HALO, c), F32),
                 pltpu.VMEM((7, _SHIFT_ROWS, c), F32), pltpu.VMEM((7, _SHIFT_ROWS, c), F32)],
    )(uc, uc, uc, uc, dca, dca, w)


_SC_T = 256


def _short_conv_fwd(u2, w):
    s, d3 = u2.shape
    d = d3 // 3
    hb = _SC_T // _SC_HALO

    def body(b_ref, c_ref, v_ref, hc_ref, hv_ref, w_ref, o_ref, cs_ref):
        i = pl.program_id(0)
        cs_ref[pl.ds(_SC_HALO, _SC_T), :] = c_ref[...].astype(F32) * v_ref[...].astype(F32)
        cs_ref[pl.ds(0, _SC_HALO), :] = jnp.where(i > 0, hc_ref[...].astype(F32) * hv_ref[...].astype(F32), 0.0)
        conv = (w_ref[0:1, :] * cs_ref[pl.ds(_SC_HALO - 2, _SC_T), :]
                + w_ref[1:2, :] * cs_ref[pl.ds(_SC_HALO - 1, _SC_T), :]
                + w_ref[2:3, :] * cs_ref[pl.ds(_SC_HALO, _SC_T), :])
        o_ref[...] = (b_ref[...].astype(F32) * conv).astype(BF16)

    def col(j):
        return pl.BlockSpec((_SC_T, d), lambda i: (i, j))

    def halo(j):
        return pl.BlockSpec((_SC_HALO, d), lambda i: (jnp.maximum(i * hb - 1, 0), j))

    return _pcall(
        body, name="short_conv_fwd", grid=(s // _SC_T,),
        in_specs=[col(0), col(1), col(2), halo(1), halo(2), pl.BlockSpec((3, d), lambda i: (0, 0))],
        out_specs=pl.BlockSpec((_SC_T, d), lambda i: (i, 0)),
        out_shape=jax.ShapeDtypeStruct((s, d), BF16),
        scratch=[pltpu.VMEM((_SC_T + _SC_HALO, d), F32)])(u2, u2, u2, u2, u2, w)


def _short_conv_bwd(u2, dsc, w, rider=None):
    s, d3 = u2.shape
    d = d3 // 3
    hb = _SC_T // _SC_HALO
    nblk = s // _SC_T

    def body(b_ref, c_ref, v_ref, hc_ref, hv_ref, nb_ref, d_ref, nd_ref, w_ref, du_ref, dw_ref, cs_ref, ds_ref):
        i = pl.program_id(0)
        cval, vval, bval = c_ref[...].astype(F32), v_ref[...].astype(F32), b_ref[...].astype(F32)
        cs_ref[pl.ds(_SC_HALO, _SC_T), :] = cval * vval
        cs_ref[pl.ds(0, _SC_HALO), :] = jnp.where(i > 0, hc_ref[...].astype(F32) * hv_ref[...].astype(F32), 0.0)
        dsc_cur = d_ref[...]
        dconv = dsc_cur * bval
        ds_ref[pl.ds(0, _SC_T), :] = dconv
        ds_ref[pl.ds(_SC_T, _SC_HALO), :] = jnp.where(i < nblk - 1, nd_ref[...] * nb_ref[...].astype(F32), 0.0)
        taps = [cs_ref[pl.ds(_SC_HALO - 2 + kk, _SC_T), :] for kk in range(3)]
        conv = w_ref[0:1, :] * taps[0] + w_ref[1:2, :] * taps[1] + w_ref[2:3, :] * taps[2]
        dcv = (w_ref[2:3, :] * dconv + w_ref[1:2, :] * ds_ref[pl.ds(1, _SC_T), :]
               + w_ref[0:1, :] * ds_ref[pl.ds(2, _SC_T), :])
        du_ref[:, pl.ds(0, d)] = (dsc_cur * conv).astype(BF16)
        du_ref[:, pl.ds(d, d)] = (dcv * vval).astype(BF16)
        du_ref[:, pl.ds(2 * d, d)] = (dcv * cval).astype(BF16)

        @pl.when(i == 0)
        def _():
            dw_ref[...] = jnp.zeros_like(dw_ref)

        for kk in range(3):
            dw_ref[kk:kk + 1, :] += jnp.sum(dconv * taps[kk], axis=0, keepdims=True)

    def col(j):
        return pl.BlockSpec((_SC_T, d), lambda i: (i, j))

    def halo(j):
        return pl.BlockSpec((_SC_HALO, d), lambda i: (jnp.maximum(i * hb - 1, 0), j))

    def nxt(j):
        return pl.BlockSpec((_SC_HALO, d), lambda i: (jnp.minimum((i + 1) * hb, s // _SC_HALO - 1), j))

    return _pcall(
        body, name="short_conv_bwd", grid=(nblk,),
        in_specs=[col(0), col(1), col(2), halo(1), halo(2), nxt(0), col(0), nxt(0),
                  pl.BlockSpec((3, d), lambda i: (0, 0))],
        out_specs=[pl.BlockSpec((_SC_T, d3), lambda i: (i, 0)), pl.BlockSpec((8, d), lambda i: (0, 0))],
        out_shape=[jax.ShapeDtypeStruct((s, d3), BF16), jax.ShapeDtypeStruct((8, d), F32)],
        scratch=[pltpu.VMEM((_SC_T + _SC_HALO, d), F32), pltpu.VMEM((_SC_T + _SC_HALO, d), F32)],
        rider=rider,
    )(u2, u2, u2, u2, u2, u2, dsc, dsc, w)


def _bucket_maps():
    a_idx = jnp.arange(_STEPS)[:, None]
    c_idx = jnp.arange(2 * _STEPS)[None, :]
    mdist = jnp.clip(a_idx + _STEPS - c_idx, 0, _STEPS)
    max_exact = _NUM_BUCKETS // 2
    maps = []
    for _, dil in _GROUPS:
        nn = mdist * dil
        nf = jnp.maximum(nn, 1).astype(F32)
        large = max_exact + (jnp.log(nf / max_exact) / math.log(_MAX_DISTANCE / max_exact)
                             * (_NUM_BUCKETS - max_exact)).astype(jnp.int32)
        maps.append(jnp.where(nn < max_exact, nn, jnp.minimum(large, _NUM_BUCKETS - 1)).astype(jnp.int32))
    return jnp.stack(maps, axis=0)


def _bias_expand(rel_bias, buckets):
    nh = rel_bias.shape[1]

    def body(rb_ref, bk_ref, o_ref):
        h = pl.program_id(0)
        bk = bk_ref[0]
        acc = jnp.zeros(bk.shape, F32)
        for b in range(_NUM_BUCKETS):
            acc = jnp.where(bk == b, rb_ref[b, h], acc)
        a = lax.broadcasted_iota(jnp.int32, bk.shape, 0)
        c = lax.broadcasted_iota(jnp.int32, bk.shape, 1)
        mdist = a + _STEPS - c
        o_ref[0] = jnp.where((mdist >= 0) & (mdist <= _STEPS), acc, _NEG)

    return _pcall(
        body, name="bias_expand", grid=(nh,),
        in_specs=[pl.BlockSpec(memory_space=pltpu.SMEM),
                  pl.BlockSpec((1, _STEPS, 2 * _STEPS), lambda h: (h // 8, 0, 0))],
        out_specs=pl.BlockSpec((1, _STEPS, 2 * _STEPS), lambda h: (h, 0, 0)),
        out_shape=jax.ShapeDtypeStruct((nh, _STEPS, 2 * _STEPS), F32))(rel_bias, buckets)


def _bias_reduce(ds_all, buckets):
    nh = ds_all.shape[0]

    def body(ds_ref, bk_ref, o_ref):
        t, bk = ds_ref[0], bk_ref[0]
        rows = lax.broadcasted_iota(jnp.int32, (_NUM_BUCKETS, _LANES), 0)
        out = jnp.zeros((_NUM_BUCKETS, _LANES), F32)
        for b in range(_NUM_BUCKETS):
            out = jnp.where(rows == b, jnp.sum(jnp.where(bk == b, t, 0.0)), out)
        o_ref[0] = out

    blk = pl.BlockSpec((1, _STEPS, 2 * _STEPS), lambda h: (h, 0, 0))
    return _pcall(
        body, name="bias_reduce", grid=(nh,),
        in_specs=[blk, pl.BlockSpec((1, _STEPS, 2 * _STEPS), lambda h: (h // 8, 0, 0))],
        out_specs=pl.BlockSpec((1, _NUM_BUCKETS, _LANES), lambda h: (h, 0, 0)),
        out_shape=jax.ShapeDtypeStruct((nh, _NUM_BUCKETS, _LANES), F32))(ds_all, buckets)


def _strided_rows(ref, p, r, dil):
    if dil == 1:
        return ref[p]
    return ref[p, pl.ds(r, _STEPS, stride=dil), :]


def _store_strided(ref, p, r, dil, val):
    if dil == 1:
        ref[p] = val
    else:
        ref[p, pl.ds(r, _STEPS, stride=dil), :] = val


def _head_masks():
    lane = lax.broadcasted_iota(jnp.int32, (1, _LANES), 1)
    return [lane < _HEAD_DIM, lane >= _HEAD_DIM]


def _scores(qm, k2, bias, first):
    sc = lax.dot_general(qm, k2, (((1,), (1,)), ((), ())), preferred_element_type=F32)
    sc = sc * (_HEAD_DIM ** -0.5) + bias
    col = lax.broadcasted_iota(jnp.int32, sc.shape, 1)
    return jnp.where(jnp.logical_and(first, col < _STEPS), _NEG, sc)


_PAIRS = _GROUP_COLS // _LANES


def _attn_fwd(uq, uk, uv, bias, g, dil, pp, rider=None):
    s = uq.shape[1]
    rb = _STEPS * dil
    nb = s // rb
    npb = _PAIRS // pp

    def body(q_ref, kc_ref, kp_ref, vc_ref, vp_ref, b_ref, o_ref, l_ref):
        n, r = pl.program_id(1), pl.program_id(2)
        first = n == 0
        masks = _head_masks()
        for j in range(pp):
            q2 = _strided_rows(q_ref, j, r, dil).astype(BF16)
            k2 = jnp.concatenate([_strided_rows(kp_ref, j, r, dil), _strided_rows(kc_ref, j, r, dil)],
                                 axis=0).astype(BF16)
            v2 = jnp.concatenate([_strided_rows(vp_ref, j, r, dil), _strided_rows(vc_ref, j, r, dil)],
                                 axis=0).astype(BF16)
            o_pair = jnp.zeros((_STEPS, _LANES), F32)
            l_pair = jnp.zeros((_STEPS, _LANES), F32)
            for hh in range(2):
                mk = masks[hh]
                sc = _scores(jnp.where(mk, q2, 0), k2, b_ref[2 * j + hh], first)
                mx = jnp.max(sc, axis=-1, keepdims=True)
                p = jnp.exp(sc - mx)
                den = jnp.sum(p, axis=-1, keepdims=True)
                oh = jnp.dot(p.astype(BF16), jnp.where(mk, v2, 0), preferred_element_type=F32)
                o_pair = o_pair + oh / den
                l_pair = jnp.where(mk, mx + jnp.log(den), l_pair)
            _store_strided(o_ref, j, r, dil, o_pair)
            _store_strided(l_ref, j, r, dil, l_pair)

    cur = pl.BlockSpec((pp, rb, _LANES), lambda hb, n, r: (g * npb + hb, n, 0))
    prev = pl.BlockSpec((pp, rb, _LANES), lambda hb, n, r: (g * npb + hb, jnp.maximum(n - 1, 0), 0))
    bspec = pl.BlockSpec((2 * pp, _STEPS, 2 * _STEPS), lambda hb, n, r: (g * npb + hb, 0, 0))
    ospec = pl.BlockSpec((pp, rb, _LANES), lambda hb, n, r: (hb, n, 0))
    sh = jax.ShapeDtypeStruct((_PAIRS, s, _LANES), F32)
    return _pcall(
        body, name=f"attn_fwd_g{g}", grid=(npb, nb, dil),
        in_specs=[cur, cur, prev, cur, prev, bspec], out_specs=[ospec, ospec], out_shape=[sh, sh],
        rider=rider,
    )(uq, uk, uk, uv, uv, bias)


def _attn_merge(outs, lses, cat):
    s = outs[0].shape[1]
    c = _GROUP_COLS

    def body(o0, o1, o2, l0, l1, l2, cat_in, cat_ref, lse_ref):
        del cat_in
        a0, a1, a2 = l0[...], l1[...], l2[...]
        mx = jnp.maximum(jnp.maximum(a0, a1), a2)
        w0, w1, w2 = jnp.exp(a0 - mx), jnp.exp(a1 - mx), jnp.exp(a2 - mx)
        den = w0 + w1 + w2
        y = ((w0 * o0[...] + w1 * o1[...] + w2 * o2[...]) / den).astype(BF16)
        for p in range(_PAIRS):
            cat_ref[:, p * _LANES:(p + 1) * _LANES] = y[p]
        lse_ref[...] = mx + jnp.log(den)

    blk = pl.BlockSpec((_PAIRS, _ROW_T, _LANES), lambda i: (0, i, 0))
    return _pcall(
        body, name="attn_merge", grid=(s // _ROW_T,),
        in_specs=[blk] * 6 + [_ANY],
        out_specs=[pl.BlockSpec((_ROW_T, c), lambda i: (i, 1)), blk],
        out_shape=[jax.ShapeDtypeStruct(cat.shape, BF16), jax.ShapeDtypeStruct((_PAIRS, s, _LANES), F32)],
        aliases={6: 0})(*outs, *lses, cat)


def _attn_delta(dcat, cat):
    s = dcat.shape[0]
    c = _GROUP_COLS
    seg = (jnp.arange(c)[:, None] // _HEAD_DIM == jnp.arange(c)[None, :] // _HEAD_DIM).astype(BF16)

    def body(dy_ref, y_ref, seg_ref, dl_ref, dys_ref):
        dy = dy_ref[...]
        prod = dy * y_ref[...].astype(F32)
        hi = prod.astype(BF16)
        lo = (prod - hi.astype(F32)).astype(BF16)
        dl = (jnp.dot(hi, seg_ref[...], preferred_element_type=F32)
              + jnp.dot(lo, seg_ref[...], preferred_element_type=F32))
        for p in range(_PAIRS):
            dl_ref[p] = dl[:, p * _LANES:(p + 1) * _LANES]
            dys_ref[p] = dy[:, p * _LANES:(p + 1) * _LANES]

    right = pl.BlockSpec((_ROW_T, c), lambda i: (i, 1))
    blk = pl.BlockSpec((_PAIRS, _ROW_T, _LANES), lambda i: (0, i, 0))
    sh = jax.ShapeDtypeStruct((_PAIRS, s, _LANES), F32)
    return _pcall(
        body, name="attn_delta", grid=(s // _ROW_T,),
        in_specs=[right, right, pl.BlockSpec((c, c), lambda i: (0, 0))],
        out_specs=[blk, blk], out_shape=[sh, sh])(dcat, cat, seg)


def _attn_bwd(uq, uk, uv, dys, lse, delta, bias, prev_grads, g, dil, pp, rider=None):
    s = uq.shape[1]
    rb = _STEPS * dil
    nb = s // rb
    npb = _PAIRS // pp
    scale = _HEAD_DIM ** -0.5

    def body(q_ref, kc_ref, kp_ref, vc_ref, vp_ref, dy_ref, l_ref, dl_ref, b_ref, *rest):
        rest = rest[len(prev_grads):]
        dq_ref, dk_ref, dv_ref, dsa_ref, dkc_ref, dvc_ref = rest
        n, r = pl.program_id(1), pl.program_id(2)

        @pl.when(jnp.logical_and(n == 0, r == 0))
        def _():
            dsa_ref[...] = jnp.zeros_like(dsa_ref)

        @pl.when(n == 0)
        def _():
            for j in range(pp):
                dkc_ref[r * pp + j] = jnp.zeros((_STEPS, _LANES), F32)
                dvc_ref[r * pp + j] = jnp.zeros((_STEPS, _LANES), F32)

        @pl.when(n < nb)
        def _():
            first = n == 0
            masks = _head_masks()
            for j in range(pp):
                q2 = _strided_rows(q_ref, j, r, dil).astype(BF16)
                k2 = jnp.concatenate([_strided_rows(kp_ref, j, r, dil), _strided_rows(kc_ref, j, r, dil)],
                                     axis=0).astype(BF16)
                v2 = jnp.concatenate([_strided_rows(vp_ref, j, r, dil), _strided_rows(vc_ref, j, r, dil)],
                                     axis=0).astype(BF16)
                dy2 = _strided_rows(dy_ref, j, r, dil).astype(BF16)
                lse_v = _strided_rows(l_ref, j, r, dil)
                dl_v = _strided_rows(dl_ref, j, r, dil)
                dq_p = jnp.zeros((_STEPS, _LANES), F32)
                dk_p = jnp.zeros((2 * _STEPS, _LANES), F32)
                dv_p = jnp.zeros((2 * _STEPS, _LANES), F32)
                for hh in range(2):
                    mk = masks[hh]
                    lane0 = hh * _HEAD_DIM
                    qm, km, dym = jnp.where(mk, q2, 0), jnp.where(mk, k2, 0), jnp.where(mk, dy2, 0)
                    sc = _scores(qm, k2, b_ref[2 * j + hh], first)
                    p = jnp.exp(sc - lse_v[:, lane0:lane0 + 1])
                    dp = lax.dot_general(dym, v2, (((1,), (1,)), ((), ())), preferred_element_type=F32)
                    ds = p * (dp - dl_v[:, lane0:lane0 + 1])
                    dsa_ref[2 * j + hh] += ds
                    dsb = ds.astype(BF16)
                    dq_p = dq_p + jnp.dot(dsb, km, preferred_element_type=F32)
                    dk_p = dk_p + lax.dot_general(dsb, qm, (((0,), (0,)), ((), ())), preferred_element_type=F32)
                    dv_p = dv_p + lax.dot_general(p.astype(BF16), dym, (((0,), (0,)), ((), ())),
                                                  preferred_element_type=F32)
                dk_p = dk_p * scale
                _store_strided(dq_ref, j, r, dil, dq_p * scale)
                _store_strided(dk_ref, j, r, dil, dkc_ref[r * pp + j] + dk_p[:_STEPS])
                _store_strided(dv_ref, j, r, dil, dvc_ref[r * pp + j] + dv_p[:_STEPS])
                dkc_ref[r * pp + j] = dk_p[_STEPS:]
                dvc_ref[r * pp + j] = dv_p[_STEPS:]

        @pl.when(n == nb)
        def _():
            for j in range(pp):
                _store_strided(dk_ref, j, r, dil, dkc_ref[r * pp + j])
                _store_strided(dv_ref, j, r, dil, dvc_ref[r * pp + j])

    def clamp(n):
        return jnp.minimum(n, nb - 1)

    cur = pl.BlockSpec((pp, rb, _LANES), lambda hb, n, r: (g * npb + hb, clamp(n), 0))
    prev = pl.BlockSpec((pp, rb, _LANES), lambda hb, n, r: (g * npb + hb, jnp.maximum(clamp(n) - 1, 0), 0))
    stat = pl.BlockSpec((pp, rb, _LANES), lambda hb, n, r: (hb, clamp(n), 0))
    bspec = pl.BlockSpec((2 * pp, _STEPS, 2 * _STEPS), lambda hb, n, r: (g * npb + hb, 0, 0))
    dkspec = pl.BlockSpec((pp, rb, _LANES), lambda hb, n, r: (g * npb + hb, jnp.maximum(n - 1, 0), 0))
    dsspec = pl.BlockSpec((2 * pp, _STEPS, 2 * _STEPS), lambda hb, n, r: (hb, 0, 0))
    wide = jax.ShapeDtypeStruct((3 * _PAIRS, s, _LANES), F32)
    np_ = len(prev_grads)
    return _pcall(
        body, name=f"attn_bwd_g{g}", grid=(npb, nb + 1, dil),
        in_specs=[cur, cur, prev, cur, prev, stat, stat, stat, bspec] + [_ANY] * np_,
        out_specs=[cur, dkspec, dkspec, dsspec],
        out_shape=[wide, wide, wide, jax.ShapeDtypeStruct((8, _STEPS, 2 * _STEPS), F32)],
        scratch=[pltpu.VMEM((dil * pp, _STEPS, _LANES), F32), pltpu.VMEM((dil * pp, _STEPS, _LANES), F32)],
        aliases={9 + t: t for t in range(np_)}, rider=rider,
    )(uq, uk, uk, uv, uv, dys, lse, delta, bias, *prev_grads)


def _place():
    x, y, c = lax.axis_index("x"), lax.axis_index("y"), lax.axis_index("c")
    chips = [(1 - x, y), (x, 1 - y), (1 - x, 1 - y)]
    return x, y, c, chips


def _slab(ref, axis, chip, width):
    start = pl.multiple_of(chip * width, width)
    if axis == 0:
        return ref.at[pl.ds(start, width), :]
    return ref.at[:, pl.ds(start, width)]


def _gather_rider(shards, axes):
    nw = len(shards)
    fulls = []
    for sh, ax in zip(shards, axes):
        shape = list(sh.shape)
        shape[ax] *= 4
        fulls.append(jax.ShapeDtypeStruct(tuple(shape), sh.dtype))

    def copies(ins, outs, scr):
        send, recv, loc = scr
        x, y, c, chips = _place()
        mine = 2 * x + y
        own, sends, arrivals = [], [], []
        for t in range(nw):
            width = ins[t].shape[axes[t]]
            own.append(pltpu.make_async_copy(ins[t], _slab(outs[t], axes[t], mine, width), loc.at[t]))
            for j, (px, py) in enumerate(chips):
                sems = dict(send_sem=send.at[3 * t + j], recv_sem=recv.at[3 * t + j],
                            device_id=(px, py, c), device_id_type=MESH)
                sends.append(pltpu.make_async_remote_copy(
                    src_ref=ins[t], dst_ref=_slab(outs[t], axes[t], mine, width), **sems))
                arrivals.append(pltpu.make_async_remote_copy(
                    src_ref=ins[t], dst_ref=_slab(outs[t], axes[t], 2 * px + py, width), **sems))
        return own, sends, arrivals

    def start(ins, outs, scr):
        own, sends, _ = copies(ins, outs, scr)
        for cp in own + sends:
            cp.start()

    def finish(ins, outs, scr):
        own, sends, arrivals = copies(ins, outs, scr)
        for cp in arrivals:
            cp.wait_recv()
        for cp in own:
            cp.wait()
        for cp in sends:
            cp.wait_send()

    return _Rider(shards, fulls, [pltpu.SemaphoreType.DMA((3 * nw,)), pltpu.SemaphoreType.DMA((3 * nw,)),
                                  pltpu.SemaphoreType.DMA((nw,))], start, finish)


def _run_rider(rider, name):
    nin, nout = len(rider.ins), len(rider.out_shapes)

    def body(*refs):
        ins, outs, scr = refs[:nin], refs[nin:nin + nout], refs[nin + nout:]
        rider.start(ins, outs, scr)
        rider.finish(ins, outs, scr)

    return _pcall(body, name=name, in_specs=[_ANY] * nin, out_specs=[_ANY] * nout, out_shape=rider.out_shapes,
                  scratch=rider.scratch)(*rider.ins)


def _scatter_rider(grads, axes):
    nw = len(grads)
    outs_shape = []
    for gr, ax in zip(grads, axes):
        shape = list(gr.shape)
        shape[ax] //= 4
        outs_shape.append(jax.ShapeDtypeStruct((3,) + tuple(shape), gr.dtype))

    def copies(ins, outs, scr):
        send, recv = scr
        x, y, c, chips = _place()
        cps = []
        for t in range(nw):
            width = ins[t].shape[axes[t]] // 4
            for j, (px, py) in enumerate(chips):
                cps.append(pltpu.make_async_remote_copy(
                    src_ref=_slab(ins[t], axes[t], 2 * px + py, width), dst_ref=outs[t].at[j],
                    send_sem=send.at[3 * t + j], recv_sem=recv.at[3 * t + j],
                    device_id=(px, py, c), device_id_type=MESH))
        return cps

    def start(ins, outs, scr):
        for cp in copies(ins, outs, scr):
            cp.start()

    def finish(ins, outs, scr):
        cps = copies(ins, outs, scr)
        for cp in cps:
            cp.wait_recv()
        for cp in cps:
            cp.wait_send()

    return _Rider(grads, outs_shape, [pltpu.SemaphoreType.DMA((3 * nw,)), pltpu.SemaphoreType.DMA((3 * nw,))],
                  start, finish)


def _swap_rider(parts):
    nw = len(parts)

    def copies(ins, outs, scr):
        send, recv = scr
        x, y, c, _ = _place()
        return [pltpu.make_async_remote_copy(
            src_ref=ins[t], dst_ref=outs[t], send_sem=send.at[t], recv_sem=recv.at[t],
            device_id=(x, y, 1 - c), device_id_type=MESH) for t in range(nw)]

    def start(ins, outs, scr):
        for cp in copies(ins, outs, scr):
            cp.start()

    def finish(ins, outs, scr):
        cps = copies(ins, outs, scr)
        for cp in cps:
            cp.wait_recv()
        for cp in cps:
            cp.wait_send()

    return _Rider(parts, [jax.ShapeDtypeStruct(p.shape, p.dtype) for p in parts],
                  [pltpu.SemaphoreType.DMA((nw,)), pltpu.SemaphoreType.DMA((nw,))], start, finish)


def _sum_all_devices(buf, name):
    rows, cols = buf.shape

    def body(in_ref, o_ref, gat_ref, send, recv):
        x, y, c, _ = _place()
        me = 4 * x + 2 * y + c
        gat_ref[me] = in_ref[...]
        started = []
        for mask in range(1, 8):
            fx, fy, fc = (mask >> 2) & 1, (mask >> 1) & 1, mask & 1
            peer = (x + fx * (1 - 2 * x), y + fy * (1 - 2 * y), c + fc * (1 - 2 * c))
            cp = pltpu.make_async_remote_copy(
                src_ref=in_ref, dst_ref=gat_ref.at[me], send_sem=send.at[mask - 1], recv_sem=recv.at[mask - 1],
                device_id=peer, device_id_type=MESH)
            cp.start()
            started.append(cp)
        for cp in started:
            cp.wait_recv()
        for cp in started:
            cp.wait_send()
        acc = gat_ref[0]
        for t in range(1, 8):
            acc = acc + gat_ref[t]
        o_ref[...] = acc

    vm = pl.BlockSpec(memory_space=pltpu.VMEM)
    return _pcall(
        body, name=name, in_specs=[vm], out_specs=vm, out_shape=jax.ShapeDtypeStruct((rows, cols), F32),
        scratch=[pltpu.VMEM((8, rows, cols), F32), pltpu.SemaphoreType.DMA((7,)), pltpu.SemaphoreType.DMA((7,))],
    )(buf)


_UPD_T = 256


def _sum_partials(own, got, name):
    rows, cols = own.shape
    tr = min(_UPD_T, rows)

    def body(own_ref, got_ref, o_ref):
        acc = own_ref[...].astype(F32)
        for j in range(3):
            acc = acc + got_ref[j].astype(F32)
        o_ref[...] = acc

    blk = pl.BlockSpec((tr, cols), lambda i: (i, 0))
    return _pcall(
        body, name=name, grid=(rows // tr,),
        in_specs=[blk, pl.BlockSpec((3, tr, cols), lambda i: (0, i, 0))], out_specs=blk,
        out_shape=jax.ShapeDtypeStruct((rows, cols), F32))(own, got)


def _adamw_math(w, gr, m, v):
    m = _B1 * m + (1.0 - _B1) * gr
    v = _B2 * v + (1.0 - _B2) * (gr * gr)
    m_hat = m / (1.0 - _B1 ** _STEP)
    v_hat = v / (1.0 - _B2 ** _STEP)
    delta = -_LR * (m_hat / (jnp.sqrt(v_hat) + _EPS) + _WD * w)
    return delta, m, v


def _adamw(w, m, v, parts, name):
    rows, cols = w.shape
    tr = min(_UPD_T, rows)
    npart = len(parts)

    def body(w_ref, m_ref, v_ref, *rest):
        p_refs, (g_ref, d_ref, nm_ref, nv_ref) = rest[:npart], rest[npart:]
        gr = p_refs[0][...]
        for p in p_refs[1:]:
            gr = gr + p[...]
        delta, nm, nv = _adamw_math(w_ref[...], gr, m_ref[...], v_ref[...])
        g_ref[...] = gr
        d_ref[...] = delta
        nm_ref[...] = nm
        nv_ref[...] = nv

    blk = pl.BlockSpec((tr, cols), lambda i: (i, 0))
    sh = jax.ShapeDtypeStruct((rows, cols), F32)
    return _pcall(body, name=name, grid=(rows // tr,), in_specs=[blk] * (3 + npart), out_specs=[blk] * 4,
                  out_shape=[sh] * 4)(w, m, v, *parts)


_PACK_W = 1024


def _pack(arrs, rows):
    flat = []
    for a in arrs:
        f = a.reshape(-1).astype(F32)
        pad = (-f.shape[0]) % _PACK_W
        flat.append(jnp.pad(f, (0, pad)))
    f = jnp.concatenate(flat)
    f = jnp.pad(f, (0, rows * _PACK_W - f.shape[0]))
    return f.reshape(rows, _PACK_W)


def _unpack(buf, shapes):
    flat = buf.reshape(-1)
    out, pos = [], 0
    for sh in shapes:
        size = math.prod(sh)
        out.append(flat[pos:pos + size].reshape(sh))
        pos += size + ((-size) % _PACK_W)
    return out


def _pack_rows(shapes):
    total = sum(-(-math.prod(sh) // _PACK_W) for sh in shapes)
    return -(-total // 8) * 8


def kernel(x, rel_bias, ab_norm, ab_w_in, ab_conv_w, ab_conv_b, ab_ln_g, ab_ln_b, ab_w_out, sc_norm, sc_w_in, sc_conv_w, sc_w_out, mlp_norm, mlp_w_up, mlp_w_down, final_norm, loss_target, m_rel_bias, m_ab_norm, m_ab_w_in, m_ab_conv_w, m_ab_conv_b, m_ab_ln_g, m_ab_ln_b, m_ab_w_out, m_sc_norm, m_sc_w_in, m_sc_conv_w, m_sc_w_out, m_mlp_norm, m_mlp_w_up, m_mlp_w_down, m_final_norm, v_rel_bias, v_ab_norm, v_ab_w_in, v_ab_conv_w, v_ab_conv_b, v_ab_ln_g, v_ab_ln_b, v_ab_w_out, v_sc_norm, v_sc_w_in, v_sc_conv_w, v_sc_w_out, v_mlp_norm, v_mlp_w_up, v_mlp_w_down, v_final_norm):
    s, d = x.shape[1], x.shape[2]
    dff = 4 * d
    c = _GROUP_COLS
    chip = 2 * lax.axis_index("x") + lax.axis_index("y")
    on_c0 = (lax.axis_index("c") == 0).astype(F32)
    h0 = x[0]
    tgt = loss_target[0]

    cw_sh, scn_sh, scw_sh = ab_conv_w[0], sc_norm, sc_conv_w[0]
    conv_w_full = lax.dynamic_update_slice(jnp.zeros((_CONV_K, c), F32), cw_sh * on_c0, (0, chip * cw_sh.shape[1]))
    scn_full = lax.dynamic_update_slice(jnp.zeros((1, d), F32), scn_sh * on_c0, (0, chip * scn_sh.shape[1]))
    scw_full = lax.dynamic_update_slice(jnp.zeros((3, d), F32), scw_sh * on_c0, (0, chip * scw_sh.shape[1]))
    small_shapes = [(_CONV_K, c), (1, d), (3, d)]
    small = _sum_all_devices(_pack([conv_w_full, scn_full, scw_full], _pack_rows(small_shapes)), "gather_small")
    conv_w, sc_g, sc_cw = _unpack(small, small_shapes)

    w_shards = [ab_w_in[0], ab_w_out[0], sc_w_in[0], sc_w_out[0], mlp_w_up[0], mlp_w_up[1],
                mlp_w_down[0], mlp_w_down[1]]
    w_axes = [1, 0, 1, 0, 1, 1, 0, 0]
    wb = [w.astype(BF16) for w in w_shards]
    full_w = [None] * 8

    def gather(idx):
        return _gather_rider([wb[t] for t in idx], [w_axes[t] for t in idx])

    def put(idx, got_w):
        for t, w in zip(idx, got_w):
            full_w[t] = w

    buckets = _bucket_maps()
    bias = _bias_expand(rel_bias, buckets)
    n0, got_w = _rms_fwd(h0, ab_norm, "rms_fwd_ab", rider=gather([0]))
    put([0], got_w)
    w_in = full_w[0]
    tm = min(1024, s)
    tm2 = min(2048, s)
    uc = _mm(n0, w_in, "nn", m=s, n=2 * c, k=d, tm=tm2, tn=2 * c, tk=d, out_dtype=BF16, name="proj_conv")
    uq, uk, uv = [
        _mm(n0, w_in, "nn", m=s, n=3 * c, k=d, tm=tm2, tn=c, tk=d, out_dtype=F32, name=f"proj_{nm}",
            b_off=(0, 2 + 3 * t), split="o")
        for t, nm in enumerate("qkv")]
    (cat, ca), got_w = _conv_a_fwd(uc, conv_w, ab_conv_b, ab_ln_g, ab_ln_b, rider=gather([1, 4]))
    put([1, 4], got_w)
    outs, lses = [], []
    for g, (_, dil) in enumerate(_GROUPS):
        idx = ([6], [2], [3, 5])[g]
        (o, l), got_w = _attn_fwd(uq, uk, uv, bias, g, dil, 4 if dil <= 4 else 2, rider=gather(idx))
        put(idx, got_w)
        outs.append(o)
        lses.append(l)
    cat, lse = _attn_merge(outs, lses, cat)
    h1, n1 = _mm(cat, full_w[1], "nn", m=s, n=d, k=d, tm=tm, tn=d, tk=d, out_dtype=(F32, BF16), name="out_ab",
                 epi=_epi_add_rms, extras=(h0,), vecs=(mlp_norm[0:1],))

    def mlp_fwd(h, nrm, layer, next_gain=None, rider=None):
        zr = _mm(nrm, full_w[4 + layer], "nn", m=s, n=dff, k=d, tm=tm, tn=1024, tk=d, out_dtype=BF16,
                 name=f"mlp_up{layer}", epi=_epi_relu, rider=rider)
        if rider is not None:
            zr, got_r = zr
            put([7], got_r)
        kw = dict(m=s, n=d, k=dff, tm=tm, tn=d, tk=1024, name=f"mlp_down{layer}", a_pro=_square, extras=(h,))
        if next_gain is None:
            return zr, _mm(zr, full_w[6 + layer], "nn", out_dtype=F32, epi=_epi_add, **kw), None
        hn, nn = _mm(zr, full_w[6 + layer], "nn", out_dtype=(F32, BF16), epi=_epi_add_rms, vecs=(next_gain,), **kw)
        return zr, hn, nn

    zr0, h2, n2 = mlp_fwd(h1, n1, 0, next_gain=sc_g, rider=gather([7]))
    _, w_out, w_si, w_so, w_up0, w_up1, w_dn0, w_dn1 = full_w
    w_up, w_dn = [w_up0, w_up1], [w_dn0, w_dn1]
    u2 = _mm(n2, w_si, "nn", m=s, n=3 * d, k=d, tm=tm, tn=1024, tk=d, out_dtype=BF16, name="proj_sc")
    scv = _short_conv_fwd(u2, sc_cw)
    h3, n3 = _mm(scv, w_so, "nn", m=s, n=d, k=d, tm=tm, tn=d, tk=d, out_dtype=(F32, BF16), name="out_sc",
                 epi=_epi_add_rms, extras=(h2,), vecs=(mlp_norm[1:2],))
    zr1, h4, _ = mlp_fwd(h3, n3, 1)

    dh4, dh4b, g_final, loss_part = _loss_head(h4, tgt, final_norm.reshape(1, d))
    tkw = min(2048, s)

    big_grads, got, sums = [None] * 8, [None] * 8, [None] * 8

    def scatter(t):
        return _scatter_rider([big_grads[t]], [w_axes[t]])

    def own_slab(t):
        width = big_grads[t].shape[w_axes[t]] // 4
        return lax.dynamic_slice_in_dim(big_grads[t], chip * width, width, axis=w_axes[t])

    def arrived(t, got_t):
        got[t] = got_t[0]
        sums[t] = _sum_partials(own_slab(t), got[t], f"sum_partials{t}")

    tmh = min(512, s)

    def mlp_bwd(dh, dhb, h, nrm, zr, layer):
        dz = _mm(dhb, w_dn[layer], "nt", m=s, n=dff, k=d, tm=tm, tn=1024, tk=d, out_dtype=BF16,
                 name=f"mlp_down{layer}_dx", epi=_epi_relu_sq_bwd, extras=(zr,))
        big_grads[6 + layer] = _mm(zr, dhb, "tn", m=dff, n=d, k=s, tm=1024, tn=d, tk=tkw, out_dtype=BF16,
                                   name=f"mlp_down{layer}_dw", a_pro=_square)
        big_grads[4 + layer], got_t = _mm(nrm, dz, "tn", m=d, n=dff, k=s, tm=d, tn=1024, tk=tkw, out_dtype=BF16,
                                          name=f"mlp_up{layer}_dw", rider=scatter(6 + layer))
        arrived(6 + layer, got_t)
        res, got_t = _mm(dz, w_up[layer], "nt", m=s, n=d, k=dff, tm=tmh, tn=d, tk=2048, out_dtype=(F32, BF16),
                         name=f"mlp_up{layer}_dx", rider=scatter(4 + layer), epi=_epi_rms_bwd,
                         extras=(h, dh), vecs=(mlp_norm[layer:layer + 1],), row_sum=True)
        arrived(4 + layer, got_t)
        return res

    dh3, dh3b, g_mn1 = mlp_bwd(dh4, dh4b, h3, n3, zr1, 1)

    dsc = _mm(dh3b, w_so, "nt", m=s, n=d, k=d, tm=tm, tn=d, tk=d, out_dtype=F32, name="out_sc_dx")
    big_grads[3] = _mm(scv, dh3b, "tn", m=d, n=d, k=s, tm=d, tn=d, tk=tkw, out_dtype=BF16, name="out_sc_dw")
    (du2, g_sccw8), got_t = _short_conv_bwd(u2, dsc, sc_cw, rider=scatter(3))
    arrived(3, got_t)
    big_grads[2] = _mm(n2, du2, "tn", m=d, n=3 * d, k=s, tm=d, tn=1024, tk=tkw, out_dtype=BF16,
                       name="proj_sc_dw")
    (dh2, dh2b, g_scn), got_t = _mm(
        du2, w_si, "nt", m=s, n=d, k=3 * d, tm=tmh, tn=d, tk=1536, out_dtype=(F32, BF16), name="proj_sc_dx",
        rider=scatter(2), epi=_epi_rms_bwd, extras=(h2, dh3), vecs=(sc_g,), row_sum=True)
    arrived(2, got_t)

    dh1, dh1b, g_mn0 = mlp_bwd(dh2, dh2b, h1, n1, zr0, 0)

    dcat = _mm(dh1b, w_out, "nt", m=s, n=d, k=d, tm=tm, tn=d, tk=d, out_dtype=F32, name="out_ab_dx")
    big_grads[1] = _mm(cat, dh1b, "tn", m=d, n=d, k=s, tm=d, tn=d, tk=tkw, out_dtype=BF16, name="out_ab_dw")
    (dca, conv_stats), got_t = _conv_a_bwd_ln(ca, dcat, ab_ln_g, ab_ln_b, rider=scatter(1))
    arrived(1, got_t)
    duc, g_cw32 = _conv_a_bwd_conv(uc, dca, conv_w)
    delta, dys = _attn_delta(dcat, cat)

    grads_qkv, ds_list = [], []
    for g, (_, dil) in enumerate(_GROUPS):
        dq, dk, dv, dsa = _attn_bwd(uq, uk, uv, dys, lse, delta, bias, grads_qkv, g, dil, 4 if dil <= 4 else 1)
        grads_qkv = [dq, dk, dv]
        ds_list.append(dsa)
    g_bias = _bias_reduce(jnp.concatenate(ds_list, axis=0), buckets)[:, :, 0].T

    secs = [(duc, 2 * c, 0)] + [(grads_qkv[t], 3 * c, 2 + 3 * t) for t in range(3)]
    g_in_parts = []
    for t, (du, width, off) in enumerate(secs):
        rider = _swap_rider(sums[1:]) if t == 1 else None
        part = _mm(n0, du, "tn", m=d, n=width, k=s, tm=d, tn=c, tk=tkw, out_dtype=BF16, name=f"proj_ab_dw{t}",
                   rider=rider, split="b" if t else "")
        if rider is not None:
            part, sib_late = part
        g_in_parts.append(part)
    big_grads[0] = jnp.concatenate(g_in_parts, axis=1)
    dn0 = None
    for t, (du, width, off) in enumerate(secs):
        rider = _scatter_rider([big_grads[0]], [w_axes[0]]) if t == 1 else None
        dn0 = _mm(du, w_in, "nt", m=s, n=d, k=width, tm=tm, tn=d, tk=c, out_dtype=F32, name=f"proj_ab_dx{t}",
                  b_off=(0, off), rider=rider, split="a" if t else "",
                  **({} if dn0 is None else dict(epi=_epi_add, extras=(dn0,))))
        if rider is not None:
            dn0, (got[0],) = dn0
    grad_x, _, g_abn = _rms_bwd(dn0, h0, ab_norm, dh1, "rms_bwd_ab")
    sums[0] = _sum_partials(own_slab(0), got[0], "sum_partials0")
    sib = list(_run_rider(_swap_rider([sums[0]]), "swap_sibling_w_in")) + sib_late

    big_m = [m_ab_w_in[0], m_ab_w_out[0], m_sc_w_in[0], m_sc_w_out[0], m_mlp_w_up[0], m_mlp_w_up[1],
             m_mlp_w_down[0], m_mlp_w_down[1]]
    big_v = [v_ab_w_in[0], v_ab_w_out[0], v_sc_w_in[0], v_sc_w_out[0], v_mlp_w_up[0], v_mlp_w_up[1],
             v_mlp_w_down[0], v_mlp_w_down[1]]
    upd = [_adamw(w_shards[t], big_m[t], big_v[t], [sums[t], sib[t]], f"adamw{t}") for t in range(8)]

    full_shapes = [(_NUM_BUCKETS, rel_bias.shape[1]), (1, d), (_CONV_K, c), (1, c), (1, c), (1, c), (1, d),
                   (3, d), (2, d), (d,)]
    small_grads = [g_bias, g_abn, g_cw32[:_CONV_K], conv_stats[0:1], conv_stats[1:2], conv_stats[2:3], g_scn,
                   g_sccw8[:3], jnp.concatenate([g_mn0, g_mn1], axis=0), g_final.reshape(d)]
    tot = _unpack(_sum_all_devices(_pack(small_grads, _pack_rows(full_shapes)), "sum_small"), full_shapes)
    for idx, sh in ((2, cw_sh), (6, scn_sh), (7, scw_sh)):
        width = sh.shape[1]
        tot[idx] = lax.dynamic_slice_in_dim(tot[idx], chip * width, width, axis=1)
    sm_w = [rel_bias, ab_norm, cw_sh, ab_conv_b, ab_ln_g, ab_ln_b, scn_sh, scw_sh, mlp_norm, final_norm]
    sm_m = [m_rel_bias, m_ab_norm, m_ab_conv_w[0], m_ab_conv_b, m_ab_ln_g, m_ab_ln_b, m_sc_norm, m_sc_conv_w[0],
            m_mlp_norm, m_final_norm]
    sm_v = [v_rel_bias, v_ab_norm, v_ab_conv_w[0], v_ab_conv_b, v_ab_ln_g, v_ab_ln_b, v_sc_norm, v_sc_conv_w[0],
            v_mlp_norm, v_final_norm]
    sh_shapes = [tuple(t.shape) for t in tot]
    rows = _pack_rows(sh_shapes)
    sm_upd = _adamw(_pack(sm_w, rows), _pack(sm_m, rows), _pack(sm_v, rows), [_pack(tot, rows)], "adamw_small")
    sm_g, sm_d, sm_nm, sm_nv = [_unpack(buf, sh_shapes) for buf in sm_upd]

    loss = lax.psum(loss_part[0, 0], ("x", "y", "c"))

    def assemble(big, sm):
        up = jnp.stack([big[4], big[5]], axis=0)
        dn = jnp.stack([big[6], big[7]], axis=0)
        return [sm[0], sm[1], big[0][None], sm[2][None], sm[3], sm[4], sm[5], big[1][None], sm[6], big[2][None],
                sm[7][None], big[3][None], sm[8], up, dn, sm[9]]

    res = [loss, grad_x[None]]
    for kind, sm in enumerate((sm_g, sm_d, sm_nm, sm_nv)):
        res += assemble([u[kind] for u in upd], sm)
    return tuple(res)
```

```python
import functools
import math

import jax
import jax.numpy as jnp
from jax import lax
from jax.experimental import pallas as pl
from jax.experimental.pallas import tpu as pltpu

F32 = jnp.float32
BF16 = jnp.bfloat16
MESH = pl.DeviceIdType.MESH

_GROUPS = ((128, 1), (512, 4), (2048, 16))
_STEPS = 128
_HEAD_DIM = 64
_GROUP_COLS = 512
_NUM_BUCKETS = 32
_MAX_DISTANCE = 2048
_CONV_K = 31
_HALO = 32
_SC_HALO = 16
_RMS_EPS = 1e-6
_LN_EPS = 1e-5
_NEG = -1e30
_LANES = 128
_VMEM_LIMIT = 56 * 1024 * 1024

_LR, _B1, _B2, _EPS, _WD, _STEP = 0.001, 0.9, 0.999, 1e-08, 0.01, 10


class _Rider:
    def __init__(self, ins, out_shapes, scratch, start, finish):
        self.ins, self.out_shapes, self.scratch = list(ins), list(out_shapes), list(scratch)
        self.start, self.finish = start, finish


def _pcall(body, *, name, out_shape, in_specs, out_specs, grid=None, scratch=(), aliases=None, rider=None):
    kw = {} if grid is None else {"grid": grid}
    cparams = pltpu.CompilerParams(vmem_limit_bytes=_VMEM_LIMIT)
    if rider is None:
        return pl.pallas_call(
            body, name=name, out_shape=out_shape, in_specs=in_specs, out_specs=out_specs,
            scratch_shapes=list(scratch), input_output_aliases=aliases or {},
            compiler_params=cparams, **kw)
    single = not isinstance(out_specs, (list, tuple))
    ospecs = [out_specs] if single else list(out_specs)
    oshapes = [out_shape] if single else list(out_shape)
    nin, nout, nscr = len(in_specs), len(ospecs), len(scratch)
    rin, rout = len(rider.ins), len(rider.out_shapes)

    def wrapped(*refs):
        h_in, r_in = refs[:nin], refs[nin:nin + rin]
        p = nin + rin
        h_out, r_out = refs[p:p + nout], refs[p + nout:p + nout + rout]
        p += nout + rout
        h_scr, r_scr = refs[p:p + nscr], refs[p + nscr:]
        ids = [pl.program_id(a) for a in range(len(grid))]
        first = functools.reduce(jnp.logical_and, [i == 0 for i in ids])
        last = functools.reduce(jnp.logical_and, [i == g - 1 for i, g in zip(ids, grid)])

        @pl.when(first)
        def _():
            rider.start(r_in, r_out, r_scr)

        body(*h_in, *h_out, *h_scr)

        @pl.when(last)
        def _():
            rider.finish(r_in, r_out, r_scr)

    call = pl.pallas_call(
        wrapped, name=name, out_shape=oshapes + rider.out_shapes,
        in_specs=list(in_specs) + [_ANY] * rin, out_specs=ospecs + [_ANY] * rout,
        scratch_shapes=list(scratch) + rider.scratch, input_output_aliases=aliases or {},
        compiler_params=cparams, **kw)

    def run(*operands):
        res = call(*operands, *rider.ins)
        host = res[0] if single else list(res[:nout])
        return host, list(res[nout:])

    return run


def _sig(x):
    return 1.0 / (1.0 + jnp.exp(-x))


_ANY = pl.BlockSpec(memory_space=pl.ANY)


def _lanes_of(ref):
    parts = [ref[p] for p in range(ref.shape[0])]
    return parts[0] if len(parts) == 1 else jnp.concatenate(parts, axis=1)


def _mm(a, b, mode, *, m, n, k, tm, tn, tk, out_dtype, name, epi=None, extras=(), b_off=(0, 0), rider=None,
        split="", vecs=(), a_pro=None, row_sum=False):
    nk = k // tk
    assert m % tm == 0 and n % tn == 0 and k % tk == 0
    o0, o1 = b_off
    if mode == "nn":
        a_spec = pl.BlockSpec((tm, tk), lambda i, j, kk: (i, kk))
        b_spec = pl.BlockSpec((tk, tn), lambda i, j, kk: (kk + o0, j + o1))
        dn = (((1,), (0,)), ((), ()))
    elif mode == "nt":
        a_spec = pl.BlockSpec((tm, tk), lambda i, j, kk: (i, kk))
        if "a" in split:
            a_spec = pl.BlockSpec((tk // _LANES, tm, _LANES), lambda i, j, kk: (kk, i, 0))
        b_spec = pl.BlockSpec((tn, tk), lambda i, j, kk: (j + o0, kk + o1))
        dn = (((1,), (1,)), ((), ()))
    else:
        a_spec = pl.BlockSpec((tk, tm), lambda i, j, kk: (kk, i))
        b_spec = pl.BlockSpec((tk, tn), lambda i, j, kk: (kk + o0, j + o1))
        if "b" in split:
            b_spec = pl.BlockSpec((tn // _LANES, tk, _LANES), lambda i, j, kk: (j, kk, 0))
        dn = (((0,), (0,)), ((), ()))
    o_spec = pl.BlockSpec((tm, tn), lambda i, j, kk: (i, j))
    e_spec = o_spec
    if "o" in split:
        o_spec = pl.BlockSpec((tn // _LANES, tm, _LANES), lambda i, j, kk: (j, i, 0))
    v_spec = pl.BlockSpec((1, tn), lambda i, j, kk: (0, j))
    ne = len(extras) + len(vecs)
    multi = isinstance(out_dtype, tuple)
    dts = out_dtype if multi else (out_dtype,)
    no = len(dts)
    nr = 1 if row_sum else 0
    assert not row_sum or tn == n

    def body(a_ref, b_ref, *rest):
        ex, o_refs = rest[:ne], rest[ne:ne + no]
        av = _lanes_of(a_ref) if "a" in split else a_ref[...]
        bv = _lanes_of(b_ref) if "b" in split else b_ref[...]
        if av.dtype != BF16:
            av = av.astype(BF16)
        if bv.dtype != BF16:
            bv = bv.astype(BF16)
        if a_pro is not None:
            av = a_pro(av)
        p = lax.dot_general(av, bv, dn, preferred_element_type=F32)

        def fin(x):
            if epi is not None:
                x = epi(x, *[e[...] for e in ex])
            if row_sum:
                row, x = x[-1], (x[:-1] if multi else x[0])
                row_ref = rest[ne + no]

                @pl.when(pl.program_id(0) == 0)
                def _():
                    row_ref[...] = row

                @pl.when(pl.program_id(0) > 0)
                def _():
                    row_ref[...] += row

            for o_ref, val, dt in zip(o_refs, x if multi else (x,), dts):
                if "o" in split:
                    for p in range(tn // _LANES):
                        o_ref[p] = val[:, p * _LANES:(p + 1) * _LANES].astype(dt)
                else:
                    o_ref[...] = val.astype(dt)

        if nk == 1:
            fin(p)
        else:
            acc = rest[ne + no + nr]
            kk = pl.program_id(2)

            @pl.when(kk == 0)
            def _():
                acc[...] = p

            @pl.when(kk > 0)
            def _():
                acc[...] += p

            @pl.when(kk == nk - 1)
            def _():
                fin(acc[...])

    oshape = (n // _LANES, m, _LANES) if "o" in split else (m, n)
    shapes = [jax.ShapeDtypeStruct(oshape, dt) for dt in dts]
    ospecs = [o_spec] * no
    if row_sum:
        shapes.append(jax.ShapeDtypeStruct((1, n), F32))
        ospecs.append(v_spec)
    lone = not multi and not row_sum
    return _pcall(
        body, name=name, grid=(m // tm, n // tn, nk),
        in_specs=[a_spec, b_spec] + [e_spec] * len(extras) + [v_spec] * len(vecs),
        out_specs=ospecs[0] if lone else ospecs, out_shape=shapes[0] if lone else shapes,
        scratch=[pltpu.VMEM((tm, tn), F32)] if nk > 1 else [], rider=rider,
    )(a, b, *extras, *vecs)


def _epi_add(x, r):
    return x + r


def _epi_relu(x):
    return jnp.maximum(x, 0.0)


def _square(x):
    return x * x


def _epi_relu_sq_bwd(da, zr):
    return da * (2.0 * zr.astype(F32))


def _epi_add_rms(x, r, g):
    h = x + r
    return h, h * lax.rsqrt(jnp.mean(h * h, axis=-1, keepdims=True) + _RMS_EPS) * g


def _epi_rms_bwd(dn, h, dh_in, g):
    dx, dg = _rms_bwd_math(dn, h, g)
    dh = dh_in + dx
    return dh, dh, dg


_ROW_T = 512


def _rms_fwd(h, g, name, rider=None):
    s, d = h.shape

    def body(h_ref, g_ref, o_ref):
        x = h_ref[...]
        r = lax.rsqrt(jnp.mean(x * x, axis=-1, keepdims=True) + _RMS_EPS)
        o_ref[...] = (x * r * g_ref[...]).astype(BF16)

    row = pl.BlockSpec((_ROW_T, d), lambda i: (i, 0))
    vec = pl.BlockSpec((1, d), lambda i: (0, 0))
    return _pcall(body, name=name, grid=(s // _ROW_T,), in_specs=[row, vec], out_specs=row,
                  out_shape=jax.ShapeDtypeStruct((s, d), BF16), rider=rider)(h, g)


def _rms_bwd_math(dn, x, g):
    r = lax.rsqrt(jnp.mean(x * x, axis=-1, keepdims=True) + _RMS_EPS)
    xhat = x * r
    dg = jnp.sum(dn * xhat, axis=0, keepdims=True)
    t = dn * g
    dx = r * (t - xhat * jnp.mean(t * xhat, axis=-1, keepdims=True))
    return dx, dg


def _rms_bwd(dn, h, g, dh_in, name):
    s, d = h.shape

    def body(dn_ref, h_ref, g_ref, dhi_ref, dh_ref, dhb_ref, dg_ref):
        dx, dg = _rms_bwd_math(dn_ref[...], h_ref[...], g_ref[...])
        dh = dhi_ref[...] + dx
        dh_ref[...] = dh
        dhb_ref[...] = dh.astype(BF16)

        @pl.when(pl.program_id(0) == 0)
        def _():
            dg_ref[...] = jnp.zeros_like(dg_ref)

        dg_ref[...] += dg

    row = pl.BlockSpec((_ROW_T, d), lambda i: (i, 0))
    vec = pl.BlockSpec((1, d), lambda i: (0, 0))
    return _pcall(
        body, name=name, grid=(s // _ROW_T,), in_specs=[row, row, vec, row], out_specs=[row, row, vec],
        out_shape=[jax.ShapeDtypeStruct((s, d), F32), jax.ShapeDtypeStruct((s, d), BF16),
                   jax.ShapeDtypeStruct((1, d), F32)])(dn, h, g, dh_in)


def _loss_head(h, tgt, g):
    s, d = h.shape

    def body(h_ref, t_ref, g_ref, dh_ref, dhb_ref, dg_ref, loss_ref):
        x, gv = h_ref[...], g_ref[...]
        r = lax.rsqrt(jnp.mean(x * x, axis=-1, keepdims=True) + _RMS_EPS)
        err = x * r * gv - t_ref[...]
        part = 0.5 * jnp.sum(jnp.mean(err * err, axis=-1, keepdims=True))
        dx, dg = _rms_bwd_math(err * (1.0 / d), x, gv)
        dh_ref[...] = dx
        dhb_ref[...] = dx.astype(BF16)

        @pl.when(pl.program_id(0) == 0)
        def _():
            dg_ref[...] = jnp.zeros_like(dg_ref)
            loss_ref[...] = jnp.zeros_like(loss_ref)

        dg_ref[...] += dg
        loss_ref[...] += jnp.full(loss_ref.shape, part, F32)

    row = pl.BlockSpec((_ROW_T, d), lambda i: (i, 0))
    vec = pl.BlockSpec((1, d), lambda i: (0, 0))
    one = pl.BlockSpec((1, _LANES), lambda i: (0, 0))
    return _pcall(
        body, name="loss_head", grid=(s // _ROW_T,), in_specs=[row, row, vec], out_specs=[row, row, vec, one],
        out_shape=[jax.ShapeDtypeStruct((s, d), F32), jax.ShapeDtypeStruct((s, d), BF16),
                   jax.ShapeDtypeStruct((1, d), F32), jax.ShapeDtypeStruct((1, _LANES), F32)])(h, tgt, g)


_CONV_T = 256
_CONV_RC = 64


def _conv_a_specs(s):
    c = _GROUP_COLS
    hb = _CONV_T // _HALO
    val = pl.BlockSpec((_CONV_T, c), lambda i: (i, 0))
    gate = pl.BlockSpec((_CONV_T, c), lambda i: (i, 1))
    hval = pl.BlockSpec((_HALO, c), lambda i: (jnp.maximum(i * hb - 1, 0), 0))
    hgate = pl.BlockSpec((_HALO, c), lambda i: (jnp.maximum(i * hb - 1, 0), 1))
    return val, gate, hval, hgate


def _fill_glu(val_ref, gate_ref, hval_ref, hgate_ref, hs_ref):
    i = pl.program_id(0)
    hs_ref[pl.ds(_HALO, _CONV_T), :] = val_ref[...].astype(F32) * _sig(gate_ref[...].astype(F32))
    halo = hval_ref[...].astype(F32) * _sig(hgate_ref[...].astype(F32))
    hs_ref[pl.ds(0, _HALO), :] = jnp.where(i > 0, halo, 0.0)


_SHIFT_ROWS = _CONV_T + _HALO - 8


def _fill_shifts(src_ref, sh_ref):
    for b in range(1, 8):
        sh_ref[b - 1] = src_ref[pl.ds(b, _SHIFT_ROWS), :]


def _tap_rows(src_ref, sh_ref, start, rows):
    b = start % 8
    if b == 0:
        return src_ref[pl.ds(start, rows), :]
    return sh_ref[b - 1, pl.ds(start - b, rows), :]


def _conv_rows(hs_ref, sh_ref, w_ref, r0, rows):
    off = _HALO - (_CONV_K - 1)
    acc = jnp.zeros((rows, _GROUP_COLS), F32)
    for kk in range(_CONV_K):
        acc = acc + w_ref[kk:kk + 1, :] * _tap_rows(hs_ref, sh_ref, r0 + off + kk, rows)
    return acc


def _ln_fwd(ca, g, b):
    mu = jnp.mean(ca, axis=-1, keepdims=True)
    xc = ca - mu
    rstd = lax.rsqrt(jnp.mean(xc * xc, axis=-1, keepdims=True) + _LN_EPS)
    xhat = xc * rstd
    return xhat, rstd, xhat * g + b


def _conv_a_fwd(uc, w, cb, lg, lb, rider=None):
    s = uc.shape[0]
    c = _GROUP_COLS

    def body(val_ref, gate_ref, hval_ref, hgate_ref, w_ref, cb_ref, lg_ref, lb_ref, o_ref, ca_ref, hs_ref, sh_ref):
        _fill_glu(val_ref, gate_ref, hval_ref, hgate_ref, hs_ref)
        _fill_shifts(hs_ref, sh_ref)
        for rc in range(_CONV_T // _CONV_RC):
            r0 = rc * _CONV_RC
            ca = _conv_rows(hs_ref, sh_ref, w_ref, r0, _CONV_RC) + cb_ref[...]
            ca_ref[pl.ds(r0, _CONV_RC), :] = ca
            _, _, ln = _ln_fwd(ca, lg_ref[...], lb_ref[...])
            o_ref[pl.ds(r0, _CONV_RC), :] = (ln * _sig(ln)).astype(BF16)

    val, gate, hval, hgate = _conv_a_specs(s)
    wspec = pl.BlockSpec((_CONV_K, c), lambda i: (0, 0))
    vec = pl.BlockSpec((1, c), lambda i: (0, 0))
    blk = pl.BlockSpec((_CONV_T, c), lambda i: (i, 0))
    return _pcall(
        body, name="conv_a_fwd", grid=(s // _CONV_T,),
        in_specs=[val, gate, hval, hgate, wspec, vec, vec, vec],
        out_specs=[blk, blk],
        out_shape=[jax.ShapeDtypeStruct((s, 2 * c), BF16), jax.ShapeDtypeStruct((s, c), F32)],
        scratch=[pltpu.VMEM((_CONV_T + _HALO, c), F32), pltpu.VMEM((7, _SHIFT_ROWS, c), F32)],
        rider=rider)(uc, uc, uc, uc, w, cb, lg, lb)


def _conv_a_bwd_ln(ca_all, dcat, lg, lb, rider=None):
    s = ca_all.shape[0]
    c = _GROUP_COLS

    def body(ca_ref, dy_ref, lg_ref, lb_ref, dca_ref, st_ref):
        @pl.when(pl.program_id(0) == 0)
        def _():
            st_ref[...] = jnp.zeros_like(st_ref)

        for rc in range(_CONV_T // _CONV_RC):
            r0 = rc * _CONV_RC
            ca = ca_ref[pl.ds(r0, _CONV_RC), :]
            xhat, rstd, ln = _ln_fwd(ca, lg_ref[...], lb_ref[...])
            sg = _sig(ln)
            dln = dy_ref[pl.ds(r0, _CONV_RC), :] * (sg * (1.0 + ln * (1.0 - sg)))
            dxh = dln * lg_ref[...]
            dca = rstd * (dxh - jnp.mean(dxh, axis=-1, keepdims=True)
                          - xhat * jnp.mean(dxh * xhat, axis=-1, keepdims=True))
            dca_ref[pl.ds(r0, _CONV_RC), :] = dca
            st_ref[0:1, :] += jnp.sum(dca, axis=0, keepdims=True)
            st_ref[1:2, :] += jnp.sum(dln * xhat, axis=0, keepdims=True)
            st_ref[2:3, :] += jnp.sum(dln, axis=0, keepdims=True)

    blk = pl.BlockSpec((_CONV_T, c), lambda i: (i, 0))
    vec = pl.BlockSpec((1, c), lambda i: (0, 0))
    st = pl.BlockSpec((8, c), lambda i: (0, 0))
    return _pcall(
        body, name="conv_a_bwd_ln", grid=(s // _CONV_T,),
        in_specs=[blk, blk, vec, vec], out_specs=[blk, st],
        out_shape=[jax.ShapeDtypeStruct((s, c), F32), jax.ShapeDtypeStruct((8, c), F32)],
        rider=rider)(ca_all, dcat, lg, lb)


def _conv_a_bwd_conv(uc, dca, w):
    s = uc.shape[0]
    c = _GROUP_COLS
    nblk = s // _CONV_T
    hb = _CONV_T // _HALO
    off = _HALO - (_CONV_K - 1)

    def body(val_ref, gate_ref, hval_ref, hgate_ref, d_ref, dn_ref, w_ref, du_ref, dw_ref, hs_ref, ds_ref,
             hsh_ref, dsh_ref):
        i = pl.program_id(0)
        _fill_glu(val_ref, gate_ref, hval_ref, hgate_ref, hs_ref)
        ds_ref[pl.ds(0, _CONV_T), :] = d_ref[...]
        ds_ref[pl.ds(_CONV_T, _HALO), :] = jnp.where(i < nblk - 1, dn_ref[...], 0.0)
        _fill_shifts(hs_ref, hsh_ref)
        _fill_shifts(ds_ref, dsh_ref)

        @pl.when(i == 0)
        def _():
            dw_ref[...] = jnp.zeros_like(dw_ref)

        for rc in range(_CONV_T // _CONV_RC):
            r0 = rc * _CONV_RC
            dcur = ds_ref[pl.ds(r0, _CONV_RC), :]
            dh = jnp.zeros((_CONV_RC, c), F32)
            for kk in range(_CONV_K):
                dh = dh + w_ref[kk:kk + 1, :] * _tap_rows(ds_ref, dsh_ref, r0 + _CONV_K - 1 - kk, _CONV_RC)
                dw_ref[kk:kk + 1, :] += jnp.sum(dcur * _tap_rows(hs_ref, hsh_ref, r0 + off + kk, _CONV_RC),
                                                 axis=0, keepdims=True)
            v = val_ref[pl.ds(r0, _CONV_RC), :].astype(F32)
            sg = _sig(gate_ref[pl.ds(r0, _CONV_RC), :].astype(F32))
            du_ref[pl.ds(r0, _CONV_RC), pl.ds(0, c)] = (dh * sg).astype(BF16)
            du_ref[pl.ds(r0, _CONV_RC), pl.ds(c, c)] = (dh * v * sg * (1.0 - sg)).astype(BF16)

    val, gate, hval, hgate = _conv_a_specs(s)
    blk = pl.BlockSpec((_CONV_T, c), lambda i: (i, 0))
    nxt = pl.BlockSpec((_HALO, c), lambda i: (jnp.minimum((i + 1) * hb, s // _HALO - 1), 0))
    wspec = pl.BlockSpec((_CONV_K, c), lambda i: (0, 0))
    return _pcall(
        body, name="conv_a_bwd_conv", grid=(nblk,),
        in_specs=[val, gate, hval, hgate, blk, nxt, wspec],
        out_specs=[pl.BlockSpec((_CONV_T, 2 * c), lambda i: (i, 0)), pl.BlockSpec((_HALO, c), lambda i: (0, 0))],
        out_shape=[jax.ShapeDtypeStruct((s, 2 * c), BF16), jax.ShapeDtypeStruct((_HALO, c), F32)],
        scratch=[pltpu.VMEM((_CONV_T + _HALO, c), F32), pltpu.VMEM((_CONV_T + _HALO, c), F32),
                 pltpu.VMEM((7, _SHIFT_ROWS, c), F32), pltpu.VMEM((7, _SHIFT_ROWS, c), F32)],
    )(uc, uc, uc, uc, dca, dca, w)


_SC_T = 256


def _short_conv_fwd(u2, w):
    s, d3 = u2.shape
    d = d3 // 3
    hb = _SC_T // _SC_HALO

    def body(b_ref, c_ref, v_ref, hc_ref, hv_ref, w_ref, o_ref, cs_ref):
        i = pl.program_id(0)
        cs_ref[pl.ds(_SC_HALO, _SC_T), :] = c_ref[...].astype(F32) * v_ref[...].astype(F32)
        cs_ref[pl.ds(0, _SC_HALO), :] = jnp.where(i > 0, hc_ref[...].astype(F32) * hv_ref[...].astype(F32), 0.0)
        conv = (w_ref[0:1, :] * cs_ref[pl.ds(_SC_HALO - 2, _SC_T), :]
                + w_ref[1:2, :] * cs_ref[pl.ds(_SC_HALO - 1, _SC_T), :]
                + w_ref[2:3, :] * cs_ref[pl.ds(_SC_HALO, _SC_T), :])
        o_ref[...] = (b_ref[...].astype(F32) * conv).astype(BF16)

    def col(j):
        return pl.BlockSpec((_SC_T, d), lambda i: (i, j))

    def halo(j):
        return pl.BlockSpec((_SC_HALO, d), lambda i: (jnp.maximum(i * hb - 1, 0), j))

    return _pcall(
        body, name="short_conv_fwd", grid=(s // _SC_T,),
        in_specs=[col(0), col(1), col(2), halo(1), halo(2), pl.BlockSpec((3, d), lambda i: (0, 0))],
        out_specs=pl.BlockSpec((_SC_T, d), lambda i: (i, 0)),
        out_shape=jax.ShapeDtypeStruct((s, d), BF16),
        scratch=[pltpu.VMEM((_SC_T + _SC_HALO, d), F32)])(u2, u2, u2, u2, u2, w)


def _short_conv_bwd(u2, dsc, w, rider=None):
    s, d3 = u2.shape
    d = d3 // 3
    hb = _SC_T // _SC_HALO
    nblk = s // _SC_T

    def body(b_ref, c_ref, v_ref, hc_ref, hv_ref, nb_ref, d_ref, nd_ref, w_ref, du_ref, dw_ref, cs_ref, ds_ref):
        i = pl.program_id(0)
        cval, vval, bval = c_ref[...].astype(F32), v_ref[...].astype(F32), b_ref[...].astype(F32)
        cs_ref[pl.ds(_SC_HALO, _SC_T), :] = cval * vval
        cs_ref[pl.ds(0, _SC_HALO), :] = jnp.where(i > 0, hc_ref[...].astype(F32) * hv_ref[...].astype(F32), 0.0)
        dsc_cur = d_ref[...]
        dconv = dsc_cur * bval
        ds_ref[pl.ds(0, _SC_T), :] = dconv
        ds_ref[pl.ds(_SC_T, _SC_HALO), :] = jnp.where(i < nblk - 1, nd_ref[...] * nb_ref[...].astype(F32), 0.0)
        taps = [cs_ref[pl.ds(_SC_HALO - 2 + kk, _SC_T), :] for kk in range(3)]
        conv = w_ref[0:1, :] * taps[0] + w_ref[1:2, :] * taps[1] + w_ref[2:3, :] * taps[2]
        dcv = (w_ref[2:3, :] * dconv + w_ref[1:2, :] * ds_ref[pl.ds(1, _SC_T), :]
               + w_ref[0:1, :] * ds_ref[pl.ds(2, _SC_T), :])
        du_ref[:, pl.ds(0, d)] = (dsc_cur * conv).astype(BF16)
        du_ref[:, pl.ds(d, d)] = (dcv * vval).astype(BF16)
        du_ref[:, pl.ds(2 * d, d)] = (dcv * cval).astype(BF16)

        @pl.when(i == 0)
        def _():
            dw_ref[...] = jnp.zeros_like(dw_ref)

        for kk in range(3):
            dw_ref[kk:kk + 1, :] += jnp.sum(dconv * taps[kk], axis=0, keepdims=True)

    def col(j):
        return pl.BlockSpec((_SC_T, d), lambda i: (i, j))

    def halo(j):
        return pl.BlockSpec((_SC_HALO, d), lambda i: (jnp.maximum(i * hb - 1, 0), j))

    def nxt(j):
        return pl.BlockSpec((_SC_HALO, d), lambda i: (jnp.minimum((i + 1) * hb, s // _SC_HALO - 1), j))

    return _pcall(
        body, name="short_conv_bwd", grid=(nblk,),
        in_specs=[col(0), col(1), col(2), halo(1), halo(2), nxt(0), col(0), nxt(0),
                  pl.BlockSpec((3, d), lambda i: (0, 0))],
        out_specs=[pl.BlockSpec((_SC_T, d3), lambda i: (i, 0)), pl.BlockSpec((8, d), lambda i: (0, 0))],
        out_shape=[jax.ShapeDtypeStruct((s, d3), BF16), jax.ShapeDtypeStruct((8, d), F32)],
        scratch=[pltpu.VMEM((_SC_T + _SC_HALO, d), F32), pltpu.VMEM((_SC_T + _SC_HALO, d), F32)],
        rider=rider,
    )(u2, u2, u2, u2, u2, u2, dsc, dsc, w)


def _bucket_maps():
    a_idx = jnp.arange(_STEPS)[:, None]
    c_idx = jnp.arange(2 * _STEPS)[None, :]
    mdist = jnp.clip(a_idx + _STEPS - c_idx, 0, _STEPS)
    max_exact = _NUM_BUCKETS // 2
    maps = []
    for _, dil in _GROUPS:
        nn = mdist * dil
        nf = jnp.maximum(nn, 1).astype(F32)
        large = max_exact + (jnp.log(nf / max_exact) / math.log(_MAX_DISTANCE / max_exact)
                             * (_NUM_BUCKETS - max_exact)).astype(jnp.int32)
        maps.append(jnp.where(nn < max_exact, nn, jnp.minimum(large, _NUM_BUCKETS - 1)).astype(jnp.int32))
    return jnp.stack(maps, axis=0)


def _bias_expand(rel_bias, buckets):
    nh = rel_bias.shape[1]

    def body(rb_ref, bk_ref, o_ref):
        h = pl.program_id(0)
        bk = bk_ref[0]
        acc = jnp.zeros(bk.shape, F32)
        for b in range(_NUM_BUCKETS):
            acc = jnp.where(bk == b, rb_ref[b, h], acc)
        a = lax.broadcasted_iota(jnp.int32, bk.shape, 0)
        c = lax.broadcasted_iota(jnp.int32, bk.shape, 1)
        mdist = a + _STEPS - c
        o_ref[0] = jnp.where((mdist >= 0) & (mdist <= _STEPS), acc, _NEG)

    return _pcall(
        body, name="bias_expand", grid=(nh,),
        in_specs=[pl.BlockSpec(memory_space=pltpu.SMEM),
                  pl.BlockSpec((1, _STEPS, 2 * _STEPS), lambda h: (h // 8, 0, 0))],
        out_specs=pl.BlockSpec((1, _STEPS, 2 * _STEPS), lambda h: (h, 0, 0)),
        out_shape=jax.ShapeDtypeStruct((nh, _STEPS, 2 * _STEPS), F32))(rel_bias, buckets)


def _bias_reduce(ds_all, buckets):
    nh = ds_all.shape[0]

    def body(ds_ref, bk_ref, o_ref):
        t, bk = ds_ref[0], bk_ref[0]
        rows = lax.broadcasted_iota(jnp.int32, (_NUM_BUCKETS, _LANES), 0)
        out = jnp.zeros((_NUM_BUCKETS, _LANES), F32)
        for b in range(_NUM_BUCKETS):
            out = jnp.where(rows == b, jnp.sum(jnp.where(bk == b, t, 0.0)), out)
        o_ref[0] = out

    blk = pl.BlockSpec((1, _STEPS, 2 * _STEPS), lambda h: (h, 0, 0))
    return _pcall(
        body, name="bias_reduce", grid=(nh,),
        in_specs=[blk, pl.BlockSpec((1, _STEPS, 2 * _STEPS), lambda h: (h // 8, 0, 0))],
        out_specs=pl.BlockSpec((1, _NUM_BUCKETS, _LANES), lambda h: (h, 0, 0)),
        out_shape=jax.ShapeDtypeStruct((nh, _NUM_BUCKETS, _LANES), F32))(ds_all, buckets)


def _sub_residues(dil):
    return 4 if dil % 16 == 0 else 1


def _strided_rows(ref, tmp_ref, p, r, dil):
    sub = _sub_residues(dil)
    if dil == 1:
        return [ref[p]]
    if sub == 1:
        return [ref[p, pl.ds(r, _STEPS, stride=dil), :]]
    tmp_ref[...] = ref[p, pl.ds(r, _STEPS * sub, stride=dil // sub), :]
    return [tmp_ref[pl.ds(q, _STEPS, stride=sub), :] for q in range(sub)]


def _store_strided(ref, tmp_ref, p, r, dil, vals):
    sub = _sub_residues(dil)
    if dil == 1:
        ref[p] = vals[0]
    elif sub == 1:
        ref[p, pl.ds(r, _STEPS, stride=dil), :] = vals[0]
    else:
        for q, val in enumerate(vals):
            tmp_ref[pl.ds(q, _STEPS, stride=sub), :] = val
        ref[p, pl.ds(r, _STEPS * sub, stride=dil // sub), :] = tmp_ref[...]


def _tmp_rows(dil, count):
    sub = _sub_residues(dil)
    return [pltpu.VMEM((_STEPS * sub, _LANES), F32)] * count if sub > 1 else []


def _head_masks():
    lane = lax.broadcasted_iota(jnp.int32, (1, _LANES), 1)
    return [lane < _HEAD_DIM, lane >= _HEAD_DIM]


def _scores(qm, k2, bias, first):
    sc = lax.dot_general(qm, k2, (((1,), (1,)), ((), ())), preferred_element_type=F32)
    sc = sc * (_HEAD_DIM ** -0.5) + bias
    col = lax.broadcasted_iota(jnp.int32, sc.shape, 1)
    return jnp.where(jnp.logical_and(first, col < _STEPS), _NEG, sc)


_PAIRS = _GROUP_COLS // _LANES


def _attn_fwd(uq, uk, uv, bias, g, dil, pp, rider=None):
    s = uq.shape[1]
    rb = _STEPS * dil
    nb = s // rb
    npb = _PAIRS // pp

    sub = _sub_residues(dil)

    def body(q_ref, kc_ref, kp_ref, vc_ref, vp_ref, b_ref, o_ref, l_ref, *tmp):
        tmp = tmp + (None,) * 7
        n, r = pl.program_id(1), pl.program_id(2)
        first = n == 0
        masks = _head_masks()
        for j in range(pp):
            qs = _strided_rows(q_ref, tmp[0], j, r, dil)
            kps, kcs = _strided_rows(kp_ref, tmp[1], j, r, dil), _strided_rows(kc_ref, tmp[2], j, r, dil)
            vps, vcs = _strided_rows(vp_ref, tmp[3], j, r, dil), _strided_rows(vc_ref, tmp[4], j, r, dil)
            o_res, l_res = [], []
            for q in range(sub):
                q2 = qs[q].astype(BF16)
                k2 = jnp.concatenate([kps[q], kcs[q]], axis=0).astype(BF16)
                v2 = jnp.concatenate([vps[q], vcs[q]], axis=0).astype(BF16)
                o_pair = jnp.zeros((_STEPS, _LANES), F32)
                l_pair = jnp.zeros((_STEPS, _LANES), F32)
                for hh in range(2):
                    mk = masks[hh]
                    sc = _scores(jnp.where(mk, q2, 0), k2, b_ref[2 * j + hh], first)
                    mx = jnp.max(sc, axis=-1, keepdims=True)
                    p = jnp.exp(sc - mx)
                    den = jnp.sum(p, axis=-1, keepdims=True)
                    oh = jnp.dot(p.astype(BF16), jnp.where(mk, v2, 0), preferred_element_type=F32)
                    o_pair = o_pair + oh / den
                    l_pair = jnp.where(mk, mx + jnp.log(den), l_pair)
                o_res.append(o_pair)
                l_res.append(l_pair)
            _store_strided(o_ref, tmp[5], j, r, dil, o_res)
            _store_strided(l_ref, tmp[6], j, r, dil, l_res)

    cur = pl.BlockSpec((pp, rb, _LANES), lambda hb, n, r: (g * npb + hb, n, 0))
    prev = pl.BlockSpec((pp, rb, _LANES), lambda hb, n, r: (g * npb + hb, jnp.maximum(n - 1, 0), 0))
    bspec = pl.BlockSpec((2 * pp, _STEPS, 2 * _STEPS), lambda hb, n, r: (g * npb + hb, 0, 0))
    ospec = pl.BlockSpec((pp, rb, _LANES), lambda hb, n, r: (hb, n, 0))
    sh = jax.ShapeDtypeStruct((_PAIRS, s, _LANES), F32)
    return _pcall(
        body, name=f"attn_fwd_g{g}", grid=(npb, nb, dil // sub),
        in_specs=[cur, cur, prev, cur, prev, bspec], out_specs=[ospec, ospec], out_shape=[sh, sh],
        scratch=_tmp_rows(dil, 7), rider=rider,
    )(uq, uk, uk, uv, uv, bias)


def _attn_merge(outs, lses, cat):
    s = outs[0].shape[1]
    c = _GROUP_COLS

    def body(o0, o1, o2, l0, l1, l2, cat_in, cat_ref, lse_ref):
        del cat_in
        a0, a1, a2 = l0[...], l1[...], l2[...]
        mx = jnp.maximum(jnp.maximum(a0, a1), a2)
        w0, w1, w2 = jnp.exp(a0 - mx), jnp.exp(a1 - mx), jnp.exp(a2 - mx)
        den = w0 + w1 + w2
        y = ((w0 * o0[...] + w1 * o1[...] + w2 * o2[...]) / den).astype(BF16)
        for p in range(_PAIRS):
            cat_ref[:, p * _LANES:(p + 1) * _LANES] = y[p]
        lse_ref[...] = mx + jnp.log(den)

    blk = pl.BlockSpec((_PAIRS, _ROW_T, _LANES), lambda i: (0, i, 0))
    return _pcall(
        body, name="attn_merge", grid=(s // _ROW_T,),
        in_specs=[blk] * 6 + [_ANY],
        out_specs=[pl.BlockSpec((_ROW_T, c), lambda i: (i, 1)), blk],
        out_shape=[jax.ShapeDtypeStruct(cat.shape, BF16), jax.ShapeDtypeStruct((_PAIRS, s, _LANES), F32)],
        aliases={6: 0})(*outs, *lses, cat)


def _attn_delta(dcat, cat):
    s = dcat.shape[0]
    c = _GROUP_COLS
    seg = (jnp.arange(c)[:, None] // _HEAD_DIM == jnp.arange(c)[None, :] // _HEAD_DIM).astype(BF16)

    def body(dy_ref, y_ref, seg_ref, dl_ref, dys_ref):
        dy = dy_ref[...]
        prod = dy * y_ref[...].astype(F32)
        hi = prod.astype(BF16)
        lo = (prod - hi.astype(F32)).astype(BF16)
        dl = (jnp.dot(hi, seg_ref[...], preferred_element_type=F32)
              + jnp.dot(lo, seg_ref[...], preferred_element_type=F32))
        for p in range(_PAIRS):
            dl_ref[p] = dl[:, p * _LANES:(p + 1) * _LANES]
            dys_ref[p] = dy[:, p * _LANES:(p + 1) * _LANES]

    right = pl.BlockSpec((_ROW_T, c), lambda i: (i, 1))
    blk = pl.BlockSpec((_PAIRS, _ROW_T, _LANES), lambda i: (0, i, 0))
    sh = jax.ShapeDtypeStruct((_PAIRS, s, _LANES), F32)
    return _pcall(
        body, name="attn_delta", grid=(s // _ROW_T,),
        in_specs=[right, right, pl.BlockSpec((c, c), lambda i: (0, 0))],
        out_specs=[blk, blk], out_shape=[sh, sh])(dcat, cat, seg)


def _attn_bwd(uq, uk, uv, dys, lse, delta, bias, prev_grads, g, dil, pp, rider=None):
    s = uq.shape[1]
    rb = _STEPS * dil
    nb = s // rb
    npb = _PAIRS // pp
    scale = _HEAD_DIM ** -0.5

    sub = _sub_residues(dil)

    def body(q_ref, kc_ref, kp_ref, vc_ref, vp_ref, dy_ref, l_ref, dl_ref, b_ref, *rest):
        rest = rest[len(prev_grads):]
        dq_ref, dk_ref, dv_ref, dsa_ref, dkc_ref, dvc_ref = rest[:6]
        tmp = rest[6:] + (None,) * 11
        n, r = pl.program_id(1), pl.program_id(2)

        def carry_slot(j, q):
            return ((r + (dil // sub) * q) * pp + j) if sub > 1 else r * pp + j

        @pl.when(jnp.logical_and(n == 0, r == 0))
        def _():
            dsa_ref[...] = jnp.zeros_like(dsa_ref)

        @pl.when(n == 0)
        def _():
            for j in range(pp):
                for q in range(sub):
                    dkc_ref[carry_slot(j, q)] = jnp.zeros((_STEPS, _LANES), F32)
                    dvc_ref[carry_slot(j, q)] = jnp.zeros((_STEPS, _LANES), F32)

        @pl.when(n < nb)
        def _():
            first = n == 0
            masks = _head_masks()
            for j in range(pp):
                qs = _strided_rows(q_ref, tmp[0], j, r, dil)
                kps, kcs = _strided_rows(kp_ref, tmp[1], j, r, dil), _strided_rows(kc_ref, tmp[2], j, r, dil)
                vps, vcs = _strided_rows(vp_ref, tmp[3], j, r, dil), _strided_rows(vc_ref, tmp[4], j, r, dil)
                dys_ = _strided_rows(dy_ref, tmp[5], j, r, dil)
                lses = _strided_rows(l_ref, tmp[6], j, r, dil)
                dls = _strided_rows(dl_ref, tmp[7], j, r, dil)
                ds_sum = [jnp.zeros((_STEPS, 2 * _STEPS), F32)] * 2
                dq_res, dk_res, dv_res = [], [], []
                for q in range(sub):
                    q2 = qs[q].astype(BF16)
                    k2 = jnp.concatenate([kps[q], kcs[q]], axis=0).astype(BF16)
                    v2 = jnp.concatenate([vps[q], vcs[q]], axis=0).astype(BF16)
                    dy2 = dys_[q].astype(BF16)
                    dq_p = jnp.zeros((_STEPS, _LANES), F32)
                    dk_p = jnp.zeros((2 * _STEPS, _LANES), F32)
                    dv_p = jnp.zeros((2 * _STEPS, _LANES), F32)
                    for hh in range(2):
                        mk = masks[hh]
                        lane0 = hh * _HEAD_DIM
                        qm, km, dym = jnp.where(mk, q2, 0), jnp.where(mk, k2, 0), jnp.where(mk, dy2, 0)
                        sc = _scores(qm, k2, b_ref[2 * j + hh], first)
                        p = jnp.exp(sc - lses[q][:, lane0:lane0 + 1])
                        dp = lax.dot_general(dym, v2, (((1,), (1,)), ((), ())), preferred_element_type=F32)
                        ds = p * (dp - dls[q][:, lane0:lane0 + 1])
                        ds_sum[hh] = ds_sum[hh] + ds
                        dsb = ds.astype(BF16)
                        dq_p = dq_p + jnp.dot(dsb, km, preferred_element_type=F32)
                        dk_p = dk_p + lax.dot_general(dsb, qm, (((0,), (0,)), ((), ())),
                                                      preferred_element_type=F32)
                        dv_p = dv_p + lax.dot_general(p.astype(BF16), dym, (((0,), (0,)), ((), ())),
                                                      preferred_element_type=F32)
                    dk_p = dk_p * scale
                    slot = carry_slot(j, q)
                    dq_res.append(dq_p * scale)
                    dk_res.append(dkc_ref[slot] + dk_p[:_STEPS])
                    dv_res.append(dvc_ref[slot] + dv_p[:_STEPS])
                    dkc_ref[slot] = dk_p[_STEPS:]
                    dvc_ref[slot] = dv_p[_STEPS:]
                for hh in range(2):
                    dsa_ref[2 * j + hh] += ds_sum[hh]
                _store_strided(dq_ref, tmp[8], j, r, dil, dq_res)
                _store_strided(dk_ref, tmp[9], j, r, dil, dk_res)
                _store_strided(dv_ref, tmp[10], j, r, dil, dv_res)

        @pl.when(n == nb)
        def _():
            for j in range(pp):
                _store_strided(dk_ref, tmp[9], j, r, dil, [dkc_ref[carry_slot(j, q)] for q in range(sub)])
                _store_strided(dv_ref, tmp[10], j, r, dil, [dvc_ref[carry_slot(j, q)] for q in range(sub)])

    def clamp(n):
        return jnp.minimum(n, nb - 1)

    cur = pl.BlockSpec((pp, rb, _LANES), lambda hb, n, r: (g * npb + hb, clamp(n), 0))
    prev = pl.BlockSpec((pp, rb, _LANES), lambda hb, n, r: (g * npb + hb, jnp.maximum(clamp(n) - 1, 0), 0))
    stat = pl.BlockSpec((pp, rb, _LANES), lambda hb, n, r: (hb, clamp(n), 0))
    bspec = pl.BlockSpec((2 * pp, _STEPS, 2 * _STEPS), lambda hb, n, r: (g * npb + hb, 0, 0))
    dkspec = pl.BlockSpec((pp, rb, _LANES), lambda hb, n, r: (g * npb + hb, jnp.maximum(n - 1, 0), 0))
    dsspec = pl.BlockSpec((2 * pp, _STEPS, 2 * _STEPS), lambda hb, n, r: (hb, 0, 0))
    wide = jax.ShapeDtypeStruct((3 * _PAIRS, s, _LANES), F32)
    np_ = len(prev_grads)
    return _pcall(
        body, name=f"attn_bwd_g{g}", grid=(npb, nb + 1, dil // sub),
        in_specs=[cur, cur, prev, cur, prev, stat, stat, stat, bspec] + [_ANY] * np_,
        out_specs=[cur, dkspec, dkspec, dsspec],
        out_shape=[wide, wide, wide, jax.ShapeDtypeStruct((8, _STEPS, 2 * _STEPS), F32)],
        scratch=[pltpu.VMEM((dil * pp, _STEPS, _LANES), F32), pltpu.VMEM((dil * pp, _STEPS, _LANES), F32)]
        + _tmp_rows(dil, 11),
        aliases={9 + t: t for t in range(np_)}, rider=rider,
    )(uq, uk, uk, uv, uv, dys, lse, delta, bias, *prev_grads)


def _place():
    x, y, c = lax.axis_index("x"), lax.axis_index("y"), lax.axis_index("c")
    chips = [(1 - x, y), (x, 1 - y), (1 - x, 1 - y)]
    return x, y, c, chips


def _slab(ref, axis, chip, width):
    start = pl.multiple_of(chip * width, width)
    if axis == 0:
        return ref.at[pl.ds(start, width), :]
    return ref.at[:, pl.ds(start, width)]


def _gather_rider(shards, axes):
    nw = len(shards)
    fulls = []
    for sh, ax in zip(shards, axes):
        shape = list(sh.shape)
        shape[ax] *= 4
        fulls.append(jax.ShapeDtypeStruct(tuple(shape), sh.dtype))

    def copies(ins, outs, scr):
        send, recv, loc = scr
        x, y, c, chips = _place()
        mine = 2 * x + y
        own, sends, arrivals = [], [], []
        for t in range(nw):
            width = ins[t].shape[axes[t]]
            own.append(pltpu.make_async_copy(ins[t], _slab(outs[t], axes[t], mine, width), loc.at[t]))
            for j, (px, py) in enumerate(chips):
                sems = dict(send_sem=send.at[3 * t + j], recv_sem=recv.at[3 * t + j],
                            device_id=(px, py, c), device_id_type=MESH)
                sends.append(pltpu.make_async_remote_copy(
                    src_ref=ins[t], dst_ref=_slab(outs[t], axes[t], mine, width), **sems))
                arrivals.append(pltpu.make_async_remote_copy(
                    src_ref=ins[t], dst_ref=_slab(outs[t], axes[t], 2 * px + py, width), **sems))
        return own, sends, arrivals

    def start(ins, outs, scr):
        own, sends, _ = copies(ins, outs, scr)
        for cp in own + sends:
            cp.start()

    def finish(ins, outs, scr):
        own, sends, arrivals = copies(ins, outs, scr)
        for cp in arrivals:
            cp.wait_recv()
        for cp in own:
            cp.wait()
        for cp in sends:
            cp.wait_send()

    return _Rider(shards, fulls, [pltpu.SemaphoreType.DMA((3 * nw,)), pltpu.SemaphoreType.DMA((3 * nw,)),
                                  pltpu.SemaphoreType.DMA((nw,))], start, finish)


def _run_rider(rider, name):
    nin, nout = len(rider.ins), len(rider.out_shapes)

    def body(*refs):
        ins, outs, scr = refs[:nin], refs[nin:nin + nout], refs[nin + nout:]
        rider.start(ins, outs, scr)
        rider.finish(ins, outs, scr)

    return _pcall(body, name=name, in_specs=[_ANY] * nin, out_specs=[_ANY] * nout, out_shape=rider.out_shapes,
                  scratch=rider.scratch)(*rider.ins)


def _scatter_rider(grads, axes):
    nw = len(grads)
    outs_shape = []
    for gr, ax in zip(grads, axes):
        shape = list(gr.shape)
        shape[ax] //= 4
        outs_shape.append(jax.ShapeDtypeStruct((3,) + tuple(shape), gr.dtype))

    def copies(ins, outs, scr):
        send, recv = scr
        x, y, c, chips = _place()
        cps = []
        for t in range(nw):
            width = ins[t].shape[axes[t]] // 4
            for j, (px, py) in enumerate(chips):
                cps.append(pltpu.make_async_remote_copy(
                    src_ref=_slab(ins[t], axes[t], 2 * px + py, width), dst_ref=outs[t].at[j],
                    send_sem=send.at[3 * t + j], recv_sem=recv.at[3 * t + j],
                    device_id=(px, py, c), device_id_type=MESH))
        return cps

    def start(ins, outs, scr):
        for cp in copies(ins, outs, scr):
            cp.start()

    def finish(ins, outs, scr):
        cps = copies(ins, outs, scr)
        for cp in cps:
            cp.wait_recv()
        for cp in cps:
            cp.wait_send()

    return _Rider(grads, outs_shape, [pltpu.SemaphoreType.DMA((3 * nw,)), pltpu.SemaphoreType.DMA((3 * nw,))],
                  start, finish)


def _swap_rider(parts):
    nw = len(parts)

    def copies(ins, outs, scr):
        send, recv = scr
        x, y, c, _ = _place()
        return [pltpu.make_async_remote_copy(
            src_ref=ins[t], dst_ref=outs[t], send_sem=send.at[t], recv_sem=recv.at[t],
            device_id=(x, y, 1 - c), device_id_type=MESH) for t in range(nw)]

    def start(ins, outs, scr):
        for cp in copies(ins, outs, scr):
            cp.start()

    def finish(ins, outs, scr):
        cps = copies(ins, outs, scr)
        for cp in cps:
            cp.wait_recv()
        for cp in cps:
            cp.wait_send()

    return _Rider(parts, [jax.ShapeDtypeStruct(p.shape, p.dtype) for p in parts],
                  [pltpu.SemaphoreType.DMA((nw,)), pltpu.SemaphoreType.DMA((nw,))], start, finish)


def _sum_all_devices(buf, name):
    rows, cols = buf.shape

    def body(in_ref, o_ref, gat_ref, send, recv):
        x, y, c, _ = _place()
        me = 4 * x + 2 * y + c
        gat_ref[me] = in_ref[...]
        started = []
        for mask in range(1, 8):
            fx, fy, fc = (mask >> 2) & 1, (mask >> 1) & 1, mask & 1
            peer = (x + fx * (1 - 2 * x), y + fy * (1 - 2 * y), c + fc * (1 - 2 * c))
            cp = pltpu.make_async_remote_copy(
                src_ref=in_ref, dst_ref=gat_ref.at[me], send_sem=send.at[mask - 1], recv_sem=recv.at[mask - 1],
                device_id=peer, device_id_type=MESH)
            cp.start()
            started.append(cp)
        for cp in started:
            cp.wait_recv()
        for cp in started:
            cp.wait_send()
        acc = gat_ref[0]
        for t in range(1, 8):
            acc = acc + gat_ref[t]
        o_ref[...] = acc

    vm = pl.BlockSpec(memory_space=pltpu.VMEM)
    return _pcall(
        body, name=name, in_specs=[vm], out_specs=vm, out_shape=jax.ShapeDtypeStruct((rows, cols), F32),
        scratch=[pltpu.VMEM((8, rows, cols), F32), pltpu.SemaphoreType.DMA((7,)), pltpu.SemaphoreType.DMA((7,))],
    )(buf)


_UPD_T = 256


def _sum_partials(own, got, name):
    rows, cols = own.shape
    tr = min(_UPD_T, rows)

    def body(own_ref, got_ref, o_ref):
        acc = own_ref[...].astype(F32)
        for j in range(3):
            acc = acc + got_ref[j].astype(F32)
        o_ref[...] = acc

    blk = pl.BlockSpec((tr, cols), lambda i: (i, 0))
    return _pcall(
        body, name=name, grid=(rows // tr,),
        in_specs=[blk, pl.BlockSpec((3, tr, cols), lambda i: (0, i, 0))], out_specs=blk,
        out_shape=jax.ShapeDtypeStruct((rows, cols), F32))(own, got)


def _adamw_math(w, gr, m, v):
    m = _B1 * m + (1.0 - _B1) * gr
    v = _B2 * v + (1.0 - _B2) * (gr * gr)
    m_hat = m / (1.0 - _B1 ** _STEP)
    v_hat = v / (1.0 - _B2 ** _STEP)
    delta = -_LR * (m_hat / (jnp.sqrt(v_hat) + _EPS) + _WD * w)
    return delta, m, v


def _adamw(w, m, v, parts, name):
    rows, cols = w.shape
    tr = min(_UPD_T, rows)
    npart = len(parts)

    def body(w_ref, m_ref, v_ref, *rest):
        p_refs, (g_ref, d_ref, nm_ref, nv_ref) = rest[:npart], rest[npart:]
        gr = p_refs[0][...]
        for p in p_refs[1:]:
            gr = gr + p[...]
        delta, nm, nv = _adamw_math(w_ref[...], gr, m_ref[...], v_ref[...])
        g_ref[...] = gr
        d_ref[...] = delta
        nm_ref[...] = nm
        nv_ref[...] = nv

    blk = pl.BlockSpec((tr, cols), lambda i: (i, 0))
    sh = jax.ShapeDtypeStruct((rows, cols), F32)
    return _pcall(body, name=name, grid=(rows // tr,), in_specs=[blk] * (3 + npart), out_specs=[blk] * 4,
                  out_shape=[sh] * 4)(w, m, v, *parts)


_PACK_W = 1024


def _pack(arrs, rows):
    flat = []
    for a in arrs:
        f = a.reshape(-1).astype(F32)
        pad = (-f.shape[0]) % _PACK_W
        flat.append(jnp.pad(f, (0, pad)))
    f = jnp.concatenate(flat)
    f = jnp.pad(f, (0, rows * _PACK_W - f.shape[0]))
    return f.reshape(rows, _PACK_W)


def _unpack(buf, shapes):
    flat = buf.reshape(-1)
    out, pos = [], 0
    for sh in shapes:
        size = math.prod(sh)
        out.append(flat[pos:pos + size].reshape(sh))
        pos += size + ((-size) % _PACK_W)
    return out


def _pack_rows(shapes):
    total = sum(-(-math.prod(sh) // _PACK_W) for sh in shapes)
    return -(-total // 8) * 8


def kernel(x, rel_bias, ab_norm, ab_w_in, ab_conv_w, ab_conv_b, ab_ln_g, ab_ln_b, ab_w_out, sc_norm, sc_w_in, sc_conv_w, sc_w_out, mlp_norm, mlp_w_up, mlp_w_down, final_norm, loss_target, m_rel_bias, m_ab_norm, m_ab_w_in, m_ab_conv_w, m_ab_conv_b, m_ab_ln_g, m_ab_ln_b, m_ab_w_out, m_sc_norm, m_sc_w_in, m_sc_conv_w, m_sc_w_out, m_mlp_norm, m_mlp_w_up, m_mlp_w_down, m_final_norm, v_rel_bias, v_ab_norm, v_ab_w_in, v_ab_conv_w, v_ab_conv_b, v_ab_ln_g, v_ab_ln_b, v_ab_w_out, v_sc_norm, v_sc_w_in, v_sc_conv_w, v_sc_w_out, v_mlp_norm, v_mlp_w_up, v_mlp_w_down, v_final_norm):
    s, d = x.shape[1], x.shape[2]
    dff = 4 * d
    c = _GROUP_COLS
    chip = 2 * lax.axis_index("x") + lax.axis_index("y")
    on_c0 = (lax.axis_index("c") == 0).astype(F32)
    h0 = x[0]
    tgt = loss_target[0]

    cw_sh, scn_sh, scw_sh = ab_conv_w[0], sc_norm, sc_conv_w[0]
    conv_w_full = lax.dynamic_update_slice(jnp.zeros((_CONV_K, c), F32), cw_sh * on_c0, (0, chip * cw_sh.shape[1]))
    scn_full = lax.dynamic_update_slice(jnp.zeros((1, d), F32), scn_sh * on_c0, (0, chip * scn_sh.shape[1]))
    scw_full = lax.dynamic_update_slice(jnp.zeros((3, d), F32), scw_sh * on_c0, (0, chip * scw_sh.shape[1]))
    small_shapes = [(_CONV_K, c), (1, d), (3, d)]
    small = _sum_all_devices(_pack([conv_w_full, scn_full, scw_full], _pack_rows(small_shapes)), "gather_small")
    conv_w, sc_g, sc_cw = _unpack(small, small_shapes)

    w_shards = [ab_w_in[0], ab_w_out[0], sc_w_in[0], sc_w_out[0], mlp_w_up[0], mlp_w_up[1],
                mlp_w_down[0], mlp_w_down[1]]
    w_axes = [1, 0, 1, 0, 1, 1, 0, 0]
    wb = [w.astype(BF16) for w in w_shards]
    full_w = [None] * 8

    def gather(idx):
        return _gather_rider([wb[t] for t in idx], [w_axes[t] for t in idx])

    def put(idx, got_w):
        for t, w in zip(idx, got_w):
            full_w[t] = w

    buckets = _bucket_maps()
    bias = _bias_expand(rel_bias, buckets)
    n0, got_w = _rms_fwd(h0, ab_norm, "rms_fwd_ab", rider=gather([0]))
    put([0], got_w)
    w_in = full_w[0]
    tm = min(1024, s)
    tm2 = min(2048, s)
    uc = _mm(n0, w_in, "nn", m=s, n=2 * c, k=d, tm=tm2, tn=2 * c, tk=d, out_dtype=BF16, name="proj_conv")
    uq, uk, uv = [
        _mm(n0, w_in, "nn", m=s, n=3 * c, k=d, tm=tm2, tn=c, tk=d, out_dtype=F32, name=f"proj_{nm}",
            b_off=(0, 2 + 3 * t), split="o")
        for t, nm in enumerate("qkv")]
    (cat, ca), got_w = _conv_a_fwd(uc, conv_w, ab_conv_b, ab_ln_g, ab_ln_b, rider=gather([1, 4]))
    put([1, 4], got_w)
    outs, lses = [], []
    for g, (_, dil) in enumerate(_GROUPS):
        idx = ([6], [2], [3, 5])[g]
        (o, l), got_w = _attn_fwd(uq, uk, uv, bias, g, dil, 4 if dil <= 4 else 2, rider=gather(idx))
        put(idx, got_w)
        outs.append(o)
        lses.append(l)
    cat, lse = _attn_merge(outs, lses, cat)
    h1, n1 = _mm(cat, full_w[1], "nn", m=s, n=d, k=d, tm=tm, tn=d, tk=d, out_dtype=(F32, BF16), name="out_ab",
                 epi=_epi_add_rms, extras=(h0,), vecs=(mlp_norm[0:1],))

    def mlp_fwd(h, nrm, layer, next_gain=None, rider=None):
        zr = _mm(nrm, full_w[4 + layer], "nn", m=s, n=dff, k=d, tm=tm, tn=1024, tk=d, out_dtype=BF16,
                 name=f"mlp_up{layer}", epi=_epi_relu, rider=rider)
        if rider is not None:
            zr, got_r = zr
            put([7], got_r)
        kw = dict(m=s, n=d, k=dff, tm=tm, tn=d, tk=1024, name=f"mlp_down{layer}", a_pro=_square, extras=(h,))
        if next_gain is None:
            return zr, _mm(zr, full_w[6 + layer], "nn", out_dtype=F32, epi=_epi_add, **kw), None
        hn, nn = _mm(zr, full_w[6 + layer], "nn", out_dtype=(F32, BF16), epi=_epi_add_rms, vecs=(next_gain,), **kw)
        return zr, hn, nn

    zr0, h2, n2 = mlp_fwd(h1, n1, 0, next_gain=sc_g, rider=gather([7]))
    _, w_out, w_si, w_so, w_up0, w_up1, w_dn0, w_dn1 = full_w
    w_up, w_dn = [w_up0, w_up1], [w_dn0, w_dn1]
    u2 = _mm(n2, w_si, "nn", m=s, n=3 * d, k=d, tm=tm, tn=1024, tk=d, out_dtype=BF16, name="proj_sc")
    scv = _short_conv_fwd(u2, sc_cw)
    h3, n3 = _mm(scv, w_so, "nn", m=s, n=d, k=d, tm=tm, tn=d, tk=d, out_dtype=(F32, BF16), name="out_sc",
                 epi=_epi_add_rms, extras=(h2,), vecs=(mlp_norm[1:2],))
    zr1, h4, _ = mlp_fwd(h3, n3, 1)

    dh4, dh4b, g_final, loss_part = _loss_head(h4, tgt, final_norm.reshape(1, d))
    tkw = min(2048, s)

    big_grads, got, sums = [None] * 8, [None] * 8, [None] * 8

    def scatter(t):
        return _scatter_rider([big_grads[t]], [w_axes[t]])

    def own_slab(t):
        width = big_grads[t].shape[w_axes[t]] // 4
        return lax.dynamic_slice_in_dim(big_grads[t], chip * width, width, axis=w_axes[t])

    def arrived(t, got_t):
        got[t] = got_t[0]
        sums[t] = _sum_partials(own_slab(t), got[t], f"sum_partials{t}")

    tmh = min(512, s)

    def mlp_bwd(dh, dhb, h, nrm, zr, layer):
        dz = _mm(dhb, w_dn[layer], "nt", m=s, n=dff, k=d, tm=tm, tn=1024, tk=d, out_dtype=BF16,
                 name=f"mlp_down{layer}_dx", epi=_epi_relu_sq_bwd, extras=(zr,))
        big_grads[6 + layer] = _mm(zr, dhb, "tn", m=dff, n=d, k=s, tm=1024, tn=d, tk=tkw, out_dtype=BF16,
                                   name=f"mlp_down{layer}_dw", a_pro=_square)
        big_grads[4 + layer], got_t = _mm(nrm, dz, "tn", m=d, n=dff, k=s, tm=d, tn=1024, tk=tkw, out_dtype=BF16,
                                          name=f"mlp_up{layer}_dw", rider=scatter(6 + layer))
        arrived(6 + layer, got_t)
        res, got_t = _mm(dz, w_up[layer], "nt", m=s, n=d, k=dff, tm=tmh, tn=d, tk=2048, out_dtype=(F32, BF16),
                         name=f"mlp_up{layer}_dx", rider=scatter(4 + layer), epi=_epi_rms_bwd,
                         extras=(h, dh), vecs=(mlp_norm[layer:layer + 1],), row_sum=True)
        arrived(4 + layer, got_t)
        return res

    dh3, dh3b, g_mn1 = mlp_bwd(dh4, dh4b, h3, n3, zr1, 1)

    dsc = _mm(dh3b, w_so, "nt", m=s, n=d, k=d, tm=tm, tn=d, tk=d, out_dtype=F32, name="out_sc_dx")
    big_grads[3] = _mm(scv, dh3b, "tn", m=d, n=d, k=s, tm=d, tn=d, tk=tkw, out_dtype=BF16, name="out_sc_dw")
    (du2, g_sccw8), got_t = _short_conv_bwd(u2, dsc, sc_cw, rider=scatter(3))
    arrived(3, got_t)
    big_grads[2] = _mm(n2, du2, "tn", m=d, n=3 * d, k=s, tm=d, tn=1024, tk=tkw, out_dtype=BF16,
                       name="proj_sc_dw")
    (dh2, dh2b, g_scn), got_t = _mm(
        du2, w_si, "nt", m=s, n=d, k=3 * d, tm=tmh, tn=d, tk=1536, out_dtype=(F32, BF16), name="proj_sc_dx",
        rider=scatter(2), epi=_epi_rms_bwd, extras=(h2, dh3), vecs=(sc_g,), row_sum=True)
    arrived(2, got_t)

    dh1, dh1b, g_mn0 = mlp_bwd(dh2, dh2b, h1, n1, zr0, 0)

    dcat = _mm(dh1b, w_out, "nt", m=s, n=d, k=d, tm=tm, tn=d, tk=d, out_dtype=F32, name="out_ab_dx")
    big_grads[1] = _mm(cat, dh1b, "tn", m=d, n=d, k=s, tm=d, tn=d, tk=tkw, out_dtype=BF16, name="out_ab_dw")
    (dca, conv_stats), got_t = _conv_a_bwd_ln(ca, dcat, ab_ln_g, ab_ln_b, rider=scatter(1))
    arrived(1, got_t)
    duc, g_cw32 = _conv_a_bwd_conv(uc, dca, conv_w)
    delta, dys = _attn_delta(dcat, cat)

    grads_qkv, ds_list = [], []
    for g, (_, dil) in enumerate(_GROUPS):
        dq, dk, dv, dsa = _attn_bwd(uq, uk, uv, dys, lse, delta, bias, grads_qkv, g, dil, 4 if dil <= 4 else 1)
        grads_qkv = [dq, dk, dv]
        ds_list.append(dsa)
    g_bias = _bias_reduce(jnp.concatenate(ds_list, axis=0), buckets)[:, :, 0].T

    secs = [(duc, 2 * c, 0)] + [(grads_qkv[t], 3 * c, 2 + 3 * t) for t in range(3)]
    g_in_parts = []
    for t, (du, width, off) in enumerate(secs):
        rider = _swap_rider(sums[1:]) if t == 1 else None
        part = _mm(n0, du, "tn", m=d, n=width, k=s, tm=d, tn=c, tk=tkw, out_dtype=BF16, name=f"proj_ab_dw{t}",
                   rider=rider, split="b" if t else "")
        if rider is not None:
            part, sib_late = part
        g_in_parts.append(part)
    big_grads[0] = jnp.concatenate(g_in_parts, axis=1)
    dn0 = None
    for t, (du, width, off) in enumerate(secs):
        rider = _scatter_rider([big_grads[0]], [w_axes[0]]) if t == 1 else None
        dn0 = _mm(du, w_in, "nt", m=s, n=d, k=width, tm=tm, tn=d, tk=c, out_dtype=F32, name=f"proj_ab_dx{t}",
                  b_off=(0, off), rider=rider, split="a" if t else "",
                  **({} if dn0 is None else dict(epi=_epi_add, extras=(dn0,))))
        if rider is not None:
            dn0, (got[0],) = dn0
    grad_x, _, g_abn = _rms_bwd(dn0, h0, ab_norm, dh1, "rms_bwd_ab")
    sums[0] = _sum_partials(own_slab(0), got[0], "sum_partials0")
    sib = list(_run_rider(_swap_rider([sums[0]]), "swap_sibling_w_in")) + sib_late

    big_m = [m_ab_w_in[0], m_ab_w_out[0], m_sc_w_in[0], m_sc_w_out[0], m_mlp_w_up[0], m_mlp_w_up[1],
             m_mlp_w_down[0], m_mlp_w_down[1]]
    big_v = [v_ab_w_in[0], v_ab_w_out[0], v_sc_w_in[0], v_sc_w_out[0], v_mlp_w_up[0], v_mlp_w_up[1],
             v_mlp_w_down[0], v_mlp_w_down[1]]
    upd = [_adamw(w_shards[t], big_m[t], big_v[t], [sums[t], sib[t]], f"adamw{t}") for t in range(8)]

    full_shapes = [(_NUM_BUCKETS, rel_bias.shape[1]), (1, d), (_CONV_K, c), (1, c), (1, c), (1, c), (1, d),
                   (3, d), (2, d), (d,)]
    small_grads = [g_bias, g_abn, g_cw32[:_CONV_K], conv_stats[0:1], conv_stats[1:2], conv_stats[2:3], g_scn,
                   g_sccw8[:3], jnp.concatenate([g_mn0, g_mn1], axis=0), g_final.reshape(d)]
    tot = _unpack(_sum_all_devices(_pack(small_grads, _pack_rows(full_shapes)), "sum_small"), full_shapes)
    for idx, sh in ((2, cw_sh), (6, scn_sh), (7, scw_sh)):
        width = sh.shape[1]
        tot[idx] = lax.dynamic_slice_in_dim(tot[idx], chip * width, width, axis=1)
    sm_w = [rel_bias, ab_norm, cw_sh, ab_conv_b, ab_ln_g, ab_ln_b, scn_sh, scw_sh, mlp_norm, final_norm]
    sm_m = [m_rel_bias, m_ab_norm, m_ab_conv_w[0], m_ab_conv_b, m_ab_ln_g, m_ab_ln_b, m_sc_norm, m_sc_conv_w[0],
            m_mlp_norm, m_final_norm]
    sm_v = [v_rel_bias, v_ab_norm, v_ab_conv_w[0], v_ab_conv_b, v_ab_ln_g, v_ab_ln_b, v_sc_norm, v_sc_conv_w[0],
            v_mlp_norm, v_final_norm]
    sh_shapes = [tuple(t.shape) for t in tot]
    rows = _pack_rows(sh_shapes)
    sm_upd = _adamw(_pack(sm_w, rows), _pack(sm_m, rows), _pack(sm_v, rows), [_pack(tot, rows)], "adamw_small")
    sm_g, sm_d, sm_nm, sm_nv = [_unpack(buf, sh_shapes) for buf in sm_upd]

    loss = lax.psum(loss_part[0, 0], ("x", "y", "c"))

    def assemble(big, sm):
        up = jnp.stack([big[4], big[5]], axis=0)
        dn = jnp.stack([big[6], big[7]], axis=0)
        return [sm[0], sm[1], big[0][None], sm[2][None], sm[3], sm[4], sm[5], big[1][None], sm[6], big[2][None],
                sm[7][None], big[3][None], sm[8], up, dn, sm[9]]

    res = [loss, grad_x[None]]
    for kind, sm in enumerate((sm_g, sm_d, sm_nm, sm_nv)):
        res += assemble([u[kind] for u in upd], sm)
    return tuple(res)
```

```python
import functools
import math

import jax
import jax.numpy as jnp
from jax import lax
from jax.experimental import pallas as pl
from jax.experimental.pallas import tpu as pltpu

F32 = jnp.float32
BF16 = jnp.bfloat16
MESH = pl.DeviceIdType.MESH

_GROUPS = ((128, 1), (512, 4), (2048, 16))
_STEPS = 128
_HEAD_DIM = 64
_GROUP_COLS = 512
_NUM_BUCKETS = 32
_MAX_DISTANCE = 2048
_CONV_K = 31
_HALO = 32
_SC_HALO = 16
_RMS_EPS = 1e-6
_LN_EPS = 1e-5
_NEG = -1e30
_LANES = 128
_VMEM_LIMIT = 56 * 1024 * 1024

_LR, _B1, _B2, _EPS, _WD, _STEP = 0.001, 0.9, 0.999, 1e-08, 0.01, 10


class _Rider:
    def __init__(self, ins, out_shapes, scratch, start, finish):
        self.ins, self.out_shapes, self.scratch = list(ins), list(out_shapes), list(scratch)
        self.start, self.finish = start, finish


def _pcall(body, *, name, out_shape, in_specs, out_specs, grid=None, scratch=(), aliases=None, rider=None):
    kw = {} if grid is None else {"grid": grid}
    cparams = pltpu.CompilerParams(vmem_limit_bytes=_VMEM_LIMIT)
    if rider is None:
        return pl.pallas_call(
            body, name=name, out_shape=out_shape, in_specs=in_specs, out_specs=out_specs,
            scratch_shapes=list(scratch), input_output_aliases=aliases or {},
            compiler_params=cparams, **kw)
    single = not isinstance(out_specs, (list, tuple))
    ospecs = [out_specs] if single else list(out_specs)
    oshapes = [out_shape] if single else list(out_shape)
    nin, nout, nscr = len(in_specs), len(ospecs), len(scratch)
    rin, rout = len(rider.ins), len(rider.out_shapes)

    def wrapped(*refs):
        h_in, r_in = refs[:nin], refs[nin:nin + rin]
        p = nin + rin
        h_out, r_out = refs[p:p + nout], refs[p + nout:p + nout + rout]
        p += nout + rout
        h_scr, r_scr = refs[p:p + nscr], refs[p + nscr:]
        ids = [pl.program_id(a) for a in range(len(grid))]
        first = functools.reduce(jnp.logical_and, [i == 0 for i in ids])
        last = functools.reduce(jnp.logical_and, [i == g - 1 for i, g in zip(ids, grid)])

        @pl.when(first)
        def _():
            rider.start(r_in, r_out, r_scr)

        body(*h_in, *h_out, *h_scr)

        @pl.when(last)
        def _():
            rider.finish(r_in, r_out, r_scr)

    call = pl.pallas_call(
        wrapped, name=name, out_shape=oshapes + rider.out_shapes,
        in_specs=list(in_specs) + [_ANY] * rin, out_specs=ospecs + [_ANY] * rout,
        scratch_shapes=list(scratch) + rider.scratch, input_output_aliases=aliases or {},
        compiler_params=cparams, **kw)

    def run(*operands):
        res = call(*operands, *rider.ins)
        host = res[0] if single else list(res[:nout])
        return host, list(res[nout:])

    return run


def _sig(x):
    return 1.0 / (1.0 + jnp.exp(-x))


_ANY = pl.BlockSpec(memory_space=pl.ANY)


def _lanes_of(ref):
    parts = [ref[p] for p in range(ref.shape[0])]
    return parts[0] if len(parts) == 1 else jnp.concatenate(parts, axis=1)


def _mm(a, b, mode, *, m, n, k, tm, tn, tk, out_dtype, name, epi=None, extras=(), b_off=(0, 0), rider=None,
        split="", vecs=(), a_pro=None, row_sum=False):
    nk = k // tk
    assert m % tm == 0 and n % tn == 0 and k % tk == 0
    o0, o1 = b_off
    if mode == "nn":
        a_spec = pl.BlockSpec((tm, tk), lambda i, j, kk: (i, kk))
        b_spec = pl.BlockSpec((tk, tn), lambda i, j, kk: (kk + o0, j + o1))
        dn = (((1,), (0,)), ((), ()))
    elif mode == "nt":
        a_spec = pl.BlockSpec((tm, tk), lambda i, j, kk: (i, kk))
        if "a" in split:
            a_spec = pl.BlockSpec((tk // _LANES, tm, _LANES), lambda i, j, kk: (kk, i, 0))
        b_spec = pl.BlockSpec((tn, tk), lambda i, j, kk: (j + o0, kk + o1))
        dn = (((1,), (1,)), ((), ()))
    else:
        a_spec = pl.BlockSpec((tk, tm), lambda i, j, kk: (kk, i))
        b_spec = pl.BlockSpec((tk, tn), lambda i, j, kk: (kk + o0, j + o1))
        if "b" in split:
            b_spec = pl.BlockSpec((tn // _LANES, tk, _LANES), lambda i, j, kk: (j, kk, 0))
        dn = (((0,), (0,)), ((), ()))
    o_spec = pl.BlockSpec((tm, tn), lambda i, j, kk: (i, j))
    e_spec = o_spec
    if "o" in split:
        o_spec = pl.BlockSpec((tn // _LANES, tm, _LANES), lambda i, j, kk: (j, i, 0))
    v_spec = pl.BlockSpec((1, tn), lambda i, j, kk: (0, j))
    ne = len(extras) + len(vecs)
    multi = isinstance(out_dtype, tuple)
    dts = out_dtype if multi else (out_dtype,)
    no = len(dts)
    nr = 1 if row_sum else 0
    assert not row_sum or tn == n

    def body(a_ref, b_ref, *rest):
        ex, o_refs = rest[:ne], rest[ne:ne + no]
        av = _lanes_of(a_ref) if "a" in split else a_ref[...]
        bv = _lanes_of(b_ref) if "b" in split else b_ref[...]
        if av.dtype != BF16:
            av = av.astype(BF16)
        if bv.dtype != BF16:
            bv = bv.astype(BF16)
        if a_pro is not None:
            av = a_pro(av)
        p = lax.dot_general(av, bv, dn, preferred_element_type=F32)

        def fin(x):
            if epi is not None:
                x = epi(x, *[e[...] for e in ex])
            if row_sum:
                row, x = x[-1], (x[:-1] if multi else x[0])
                row_ref = rest[ne + no]

                @pl.when(pl.program_id(0) == 0)
                def _():
                    row_ref[...] = row

                @pl.when(pl.program_id(0) > 0)
                def _():
                    row_ref[...] += row

            for o_ref, val, dt in zip(o_refs, x if multi else (x,), dts):
                if "o" in split:
                    for p in range(tn // _LANES):
                        o_ref[p] = val[:, p * _LANES:(p + 1) * _LANES].astype(dt)
                else:
                    o_ref[...] = val.astype(dt)

        if nk == 1:
            fin(p)
        else:
            acc = rest[ne + no + nr]
            kk = pl.program_id(2)

            @pl.when(kk == 0)
            def _():
                acc[...] = p

            @pl.when(kk > 0)
            def _():
                acc[...] += p

            @pl.when(kk == nk - 1)
            def _():
                fin(acc[...])

    oshape = (n // _LANES, m, _LANES) if "o" in split else (m, n)
    shapes = [jax.ShapeDtypeStruct(oshape, dt) for dt in dts]
    ospecs = [o_spec] * no
    if row_sum:
        shapes.append(jax.ShapeDtypeStruct((1, n), F32))
        ospecs.append(v_spec)
    lone = not multi and not row_sum
    return _pcall(
        body, name=name, grid=(m // tm, n // tn, nk),
        in_specs=[a_spec, b_spec] + [e_spec] * len(extras) + [v_spec] * len(vecs),
        out_specs=ospecs[0] if lone else ospecs, out_shape=shapes[0] if lone else shapes,
        scratch=[pltpu.VMEM((tm, tn), F32)] if nk > 1 else [], rider=rider,
    )(a, b, *extras, *vecs)


def _epi_add(x, r):
    return x + r


def _epi_relu(x):
    return jnp.maximum(x, 0.0)


def _square(x):
    return x * x


def _epi_relu_sq_bwd(da, zr):
    return da * (2.0 * zr.astype(F32))


def _epi_add_rms(x, r, g):
    h = x + r
    return h, h * lax.rsqrt(jnp.mean(h * h, axis=-1, keepdims=True) + _RMS_EPS) * g


def _epi_rms_bwd(dn, h, dh_in, g):
    dx, dg = _rms_bwd_math(dn, h, g)
    dh = dh_in + dx
    return dh, dh, dg


_ROW_T = 512


def _rms_fwd(h, g, name, rider=None):
    s, d = h.shape

    def body(h_ref, g_ref, o_ref):
        x = h_ref[...]
        r = lax.rsqrt(jnp.mean(x * x, axis=-1, keepdims=True) + _RMS_EPS)
        o_ref[...] = (x * r * g_ref[...]).astype(BF16)

    row = pl.BlockSpec((_ROW_T, d), lambda i: (i, 0))
    vec = pl.BlockSpec((1, d), lambda i: (0, 0))
    return _pcall(body, name=name, grid=(s // _ROW_T,), in_specs=[row, vec], out_specs=row,
                  out_shape=jax.ShapeDtypeStruct((s, d), BF16), rider=rider)(h, g)


def _rms_bwd_math(dn, x, g):
    r = lax.rsqrt(jnp.mean(x * x, axis=-1, keepdims=True) + _RMS_EPS)
    xhat = x * r
    dg = jnp.sum(dn * xhat, axis=0, keepdims=True)
    t = dn * g
    dx = r * (t - xhat * jnp.mean(t * xhat, axis=-1, keepdims=True))
    return dx, dg


def _rms_bwd(dn, h, g, dh_in, name):
    s, d = h.shape

    def body(dn_ref, h_ref, g_ref, dhi_ref, dh_ref, dhb_ref, dg_ref):
        dx, dg = _rms_bwd_math(dn_ref[...], h_ref[...], g_ref[...])
        dh = dhi_ref[...] + dx
        dh_ref[...] = dh
        dhb_ref[...] = dh.astype(BF16)

        @pl.when(pl.program_id(0) == 0)
        def _():
            dg_ref[...] = jnp.zeros_like(dg_ref)

        dg_ref[...] += dg

    row = pl.BlockSpec((_ROW_T, d), lambda i: (i, 0))
    vec = pl.BlockSpec((1, d), lambda i: (0, 0))
    return _pcall(
        body, name=name, grid=(s // _ROW_T,), in_specs=[row, row, vec, row], out_specs=[row, row, vec],
        out_shape=[jax.ShapeDtypeStruct((s, d), F32), jax.ShapeDtypeStruct((s, d), BF16),
                   jax.ShapeDtypeStruct((1, d), F32)])(dn, h, g, dh_in)


def _loss_head(h, tgt, g):
    s, d = h.shape

    def body(h_ref, t_ref, g_ref, dh_ref, dhb_ref, dg_ref, loss_ref):
        x, gv = h_ref[...], g_ref[...]
        r = lax.rsqrt(jnp.mean(x * x, axis=-1, keepdims=True) + _RMS_EPS)
        err = x * r * gv - t_ref[...]
        part = 0.5 * jnp.sum(jnp.mean(err * err, axis=-1, keepdims=True))
        dx, dg = _rms_bwd_math(err * (1.0 / d), x, gv)
        dh_ref[...] = dx
        dhb_ref[...] = dx.astype(BF16)

        @pl.when(pl.program_id(0) == 0)
        def _():
            dg_ref[...] = jnp.zeros_like(dg_ref)
            loss_ref[...] = jnp.zeros_like(loss_ref)

        dg_ref[...] += dg
        loss_ref[...] += jnp.full(loss_ref.shape, part, F32)

    row = pl.BlockSpec((_ROW_T, d), lambda i: (i, 0))
    vec = pl.BlockSpec((1, d), lambda i: (0, 0))
    one = pl.BlockSpec((1, _LANES), lambda i: (0, 0))
    return _pcall(
        body, name="loss_head", grid=(s // _ROW_T,), in_specs=[row, row, vec], out_specs=[row, row, vec, one],
        out_shape=[jax.ShapeDtypeStruct((s, d), F32), jax.ShapeDtypeStruct((s, d), BF16),
                   jax.ShapeDtypeStruct((1, d), F32), jax.ShapeDtypeStruct((1, _LANES), F32)])(h, tgt, g)


_CONV_T = 256
_CONV_RC = 64


def _conv_a_specs(s):
    c = _GROUP_COLS
    hb = _CONV_T // _HALO
    val = pl.BlockSpec((_CONV_T, c), lambda i: (i, 0))
    gate = pl.BlockSpec((_CONV_T, c), lambda i: (i, 1))
    hval = pl.BlockSpec((_HALO, c), lambda i: (jnp.maximum(i * hb - 1, 0), 0))
    hgate = pl.BlockSpec((_HALO, c), lambda i: (jnp.maximum(i * hb - 1, 0), 1))
    return val, gate, hval, hgate


def _fill_glu(val_ref, gate_ref, hval_ref, hgate_ref, hs_ref):
    i = pl.program_id(0)
    hs_ref[pl.ds(_HALO, _CONV_T), :] = val_ref[...].astype(F32) * _sig(gate_ref[...].astype(F32))
    halo = hval_ref[...].astype(F32) * _sig(hgate_ref[...].astype(F32))
    hs_ref[pl.ds(0, _HALO), :] = jnp.where(i > 0, halo, 0.0)


_SHIFT_ROWS = _CONV_T + _HALO - 8


def _fill_shifts(src_ref, sh_ref):
    for b in range(1, 8):
        sh_ref[b - 1] = src_ref[pl.ds(b, _SHIFT_ROWS), :]


def _tap_rows(src_ref, sh_ref, start, rows):
    b = start % 8
    if b == 0:
        return src_ref[pl.ds(start, rows), :]
    return sh_ref[b - 1, pl.ds(start - b, rows), :]


def _conv_rows(hs_ref, sh_ref, w_ref, r0, rows):
    off = _HALO - (_CONV_K - 1)
    acc = jnp.zeros((rows, _GROUP_COLS), F32)
    for kk in range(_CONV_K):
        acc = acc + w_ref[kk:kk + 1, :] * _tap_rows(hs_ref, sh_ref, r0 + off + kk, rows)
    return acc


def _ln_fwd(ca, g, b):
    mu = jnp.mean(ca, axis=-1, keepdims=True)
    xc = ca - mu
    rstd = lax.rsqrt(jnp.mean(xc * xc, axis=-1, keepdims=True) + _LN_EPS)
    xhat = xc * rstd
    return xhat, rstd, xhat * g + b


def _conv_a_fwd(uc, w, cb, lg, lb, rider=None):
    s = uc.shape[0]
    c = _GROUP_COLS

    def body(val_ref, gate_ref, hval_ref, hgate_ref, w_ref, cb_ref, lg_ref, lb_ref, o_ref, ca_ref, hs_ref, sh_ref):
        _fill_glu(val_ref, gate_ref, hval_ref, hgate_ref, hs_ref)
        _fill_shifts(hs_ref, sh_ref)
        for rc in range(_CONV_T // _CONV_RC):
            r0 = rc * _CONV_RC
            ca = _conv_rows(hs_ref, sh_ref, w_ref, r0, _CONV_RC) + cb_ref[...]
            ca_ref[pl.ds(r0, _CONV_RC), :] = ca
            _, _, ln = _ln_fwd(ca, lg_ref[...], lb_ref[...])
            o_ref[pl.ds(r0, _CONV_RC), :] = (ln * _sig(ln)).astype(BF16)

    val, gate, hval, hgate = _conv_a_specs(s)
    wspec = pl.BlockSpec((_CONV_K, c), lambda i: (0, 0))
    vec = pl.BlockSpec((1, c), lambda i: (0, 0))
    blk = pl.BlockSpec((_CONV_T, c), lambda i: (i, 0))
    return _pcall(
        body, name="conv_a_fwd", grid=(s // _CONV_T,),
        in_specs=[val, gate, hval, hgate, wspec, vec, vec, vec],
        out_specs=[blk, blk],
        out_shape=[jax.ShapeDtypeStruct((s, 2 * c), BF16), jax.ShapeDtypeStruct((s, c), F32)],
        scratch=[pltpu.VMEM((_CONV_T + _HALO, c), F32), pltpu.VMEM((7, _SHIFT_ROWS, c), F32)],
        rider=rider)(uc, uc, uc, uc, w, cb, lg, lb)


def _conv_a_bwd_ln(ca_all, dcat, lg, lb, rider=None):
    s = ca_all.shape[0]
    c = _GROUP_COLS

    def body(ca_ref, dy_ref, lg_ref, lb_ref, dca_ref, st_ref):
        @pl.when(pl.program_id(0) == 0)
        def _():
            st_ref[...] = jnp.zeros_like(st_ref)

        for rc in range(_CONV_T // _CONV_RC):
            r0 = rc * _CONV_RC
            ca = ca_ref[pl.ds(r0, _CONV_RC), :]
            xhat, rstd, ln = _ln_fwd(ca, lg_ref[...], lb_ref[...])
            sg = _sig(ln)
            dln = dy_ref[pl.ds(r0, _CONV_RC), :] * (sg * (1.0 + ln * (1.0 - sg)))
            dxh = dln * lg_ref[...]
            dca = rstd * (dxh - jnp.mean(dxh, axis=-1, keepdims=True)
                          - xhat * jnp.mean(dxh * xhat, axis=-1, keepdims=True))
            dca_ref[pl.ds(r0, _CONV_RC), :] = dca
            st_ref[0:1, :] += jnp.sum(dca, axis=0, keepdims=True)
            st_ref[1:2, :] += jnp.sum(dln * xhat, axis=0, keepdims=True)
            st_ref[2:3, :] += jnp.sum(dln, axis=0, keepdims=True)

    blk = pl.BlockSpec((_CONV_T, c), lambda i: (i, 0))
    vec = pl.BlockSpec((1, c), lambda i: (0, 0))
    st = pl.BlockSpec((8, c), lambda i: (0, 0))
    return _pcall(
        body, name="conv_a_bwd_ln", grid=(s // _CONV_T,),
        in_specs=[blk, blk, vec, vec], out_specs=[blk, st],
        out_shape=[jax.ShapeDtypeStruct((s, c), F32), jax.ShapeDtypeStruct((8, c), F32)],
        rider=rider)(ca_all, dcat, lg, lb)


def _conv_a_bwd_conv(uc, dca, w, rider=None):
    s = uc.shape[0]
    c = _GROUP_COLS
    nblk = s // _CONV_T
    hb = _CONV_T // _HALO
    off = _HALO - (_CONV_K - 1)

    def body(val_ref, gate_ref, hval_ref, hgate_ref, d_ref, dn_ref, w_ref, du_ref, dw_ref, hs_ref, ds_ref,
             hsh_ref, dsh_ref):
        i = pl.program_id(0)
        _fill_glu(val_ref, gate_ref, hval_ref, hgate_ref, hs_ref)
        ds_ref[pl.ds(0, _CONV_T), :] = d_ref[...]
        ds_ref[pl.ds(_CONV_T, _HALO), :] = jnp.where(i < nblk - 1, dn_ref[...], 0.0)
        _fill_shifts(hs_ref, hsh_ref)
        _fill_shifts(ds_ref, dsh_ref)

        @pl.when(i == 0)
        def _():
            dw_ref[...] = jnp.zeros_like(dw_ref)

        for rc in range(_CONV_T // _CONV_RC):
            r0 = rc * _CONV_RC
            dcur = ds_ref[pl.ds(r0, _CONV_RC), :]
            dh = jnp.zeros((_CONV_RC, c), F32)
            for kk in range(_CONV_K):
                dh = dh + w_ref[kk:kk + 1, :] * _tap_rows(ds_ref, dsh_ref, r0 + _CONV_K - 1 - kk, _CONV_RC)
                dw_ref[kk:kk + 1, :] += jnp.sum(dcur * _tap_rows(hs_ref, hsh_ref, r0 + off + kk, _CONV_RC),
                                                 axis=0, keepdims=True)
            v = val_ref[pl.ds(r0, _CONV_RC), :].astype(F32)
            sg = _sig(gate_ref[pl.ds(r0, _CONV_RC), :].astype(F32))
            du_ref[pl.ds(r0, _CONV_RC), pl.ds(0, c)] = (dh * sg).astype(BF16)
            du_ref[pl.ds(r0, _CONV_RC), pl.ds(c, c)] = (dh * v * sg * (1.0 - sg)).astype(BF16)

    val, gate, hval, hgate = _conv_a_specs(s)
    blk = pl.BlockSpec((_CONV_T, c), lambda i: (i, 0))
    nxt = pl.BlockSpec((_HALO, c), lambda i: (jnp.minimum((i + 1) * hb, s // _HALO - 1), 0))
    wspec = pl.BlockSpec((_CONV_K, c), lambda i: (0, 0))
    return _pcall(
        body, name="conv_a_bwd_conv", grid=(nblk,),
        in_specs=[val, gate, hval, hgate, blk, nxt, wspec],
        out_specs=[pl.BlockSpec((_CONV_T, 2 * c), lambda i: (i, 0)), pl.BlockSpec((_HALO, c), lambda i: (0, 0))],
        out_shape=[jax.ShapeDtypeStruct((s, 2 * c), BF16), jax.ShapeDtypeStruct((_HALO, c), F32)],
        scratch=[pltpu.VMEM((_CONV_T + _HALO, c), F32), pltpu.VMEM((_CONV_T + _HALO, c), F32),
                 pltpu.VMEM((7, _SHIFT_ROWS, c), F32), pltpu.VMEM((7, _SHIFT_ROWS, c), F32)],
        rider=rider,
    )(uc, uc, uc, uc, dca, dca, w)


_SC_T = 256


def _short_conv_fwd(u2, w):
    s, d3 = u2.shape
    d = d3 // 3
    hb = _SC_T // _SC_HALO

    def body(b_ref, c_ref, v_ref, hc_ref, hv_ref, w_ref, o_ref, cs_ref):
        i = pl.program_id(0)
        cs_ref[pl.ds(_SC_HALO, _SC_T), :] = c_ref[...].astype(F32) * v_ref[...].astype(F32)
        cs_ref[pl.ds(0, _SC_HALO), :] = jnp.where(i > 0, hc_ref[...].astype(F32) * hv_ref[...].astype(F32), 0.0)
        conv = (w_ref[0:1, :] * cs_ref[pl.ds(_SC_HALO - 2, _SC_T), :]
                + w_ref[1:2, :] * cs_ref[pl.ds(_SC_HALO - 1, _SC_T), :]
                + w_ref[2:3, :] * cs_ref[pl.ds(_SC_HALO, _SC_T), :])
        o_ref[...] = (b_ref[...].astype(F32) * conv).astype(BF16)

    def col(j):
        return pl.BlockSpec((_SC_T, d), lambda i: (i, j))

    def halo(j):
        return pl.BlockSpec((_SC_HALO, d), lambda i: (jnp.maximum(i * hb - 1, 0), j))

    return _pcall(
        body, name="short_conv_fwd", grid=(s // _SC_T,),
        in_specs=[col(0), col(1), col(2), halo(1), halo(2), pl.BlockSpec((3, d), lambda i: (0, 0))],
        out_specs=pl.BlockSpec((_SC_T, d), lambda i: (i, 0)),
        out_shape=jax.ShapeDtypeStruct((s, d), BF16),
        scratch=[pltpu.VMEM((_SC_T + _SC_HALO, d), F32)])(u2, u2, u2, u2, u2, w)


def _short_conv_bwd(u2, dsc, w, rider=None):
    s, d3 = u2.shape
    d = d3 // 3
    hb = _SC_T // _SC_HALO
    nblk = s // _SC_T

    def body(b_ref, c_ref, v_ref, hc_ref, hv_ref, nb_ref, d_ref, nd_ref, w_ref, du_ref, dw_ref, cs_ref, ds_ref):
        i = pl.program_id(0)
        cval, vval, bval = c_ref[...].astype(F32), v_ref[...].astype(F32), b_ref[...].astype(F32)
        cs_ref[pl.ds(_SC_HALO, _SC_T), :] = cval * vval
        cs_ref[pl.ds(0, _SC_HALO), :] = jnp.where(i > 0, hc_ref[...].astype(F32) * hv_ref[...].astype(F32), 0.0)
        dsc_cur = d_ref[...]
        dconv = dsc_cur * bval
        ds_ref[pl.ds(0, _SC_T), :] = dconv
        ds_ref[pl.ds(_SC_T, _SC_HALO), :] = jnp.where(i < nblk - 1, nd_ref[...] * nb_ref[...].astype(F32), 0.0)
        taps = [cs_ref[pl.ds(_SC_HALO - 2 + kk, _SC_T), :] for kk in range(3)]
        conv = w_ref[0:1, :] * taps[0] + w_ref[1:2, :] * taps[1] + w_ref[2:3, :] * taps[2]
        dcv = (w_ref[2:3, :] * dconv + w_ref[1:2, :] * ds_ref[pl.ds(1, _SC_T), :]
               + w_ref[0:1, :] * ds_ref[pl.ds(2, _SC_T), :])
        du_ref[:, pl.ds(0, d)] = (dsc_cur * conv).astype(BF16)
        du_ref[:, pl.ds(d, d)] = (dcv * vval).astype(BF16)
        du_ref[:, pl.ds(2 * d, d)] = (dcv * cval).astype(BF16)

        @pl.when(i == 0)
        def _():
            dw_ref[...] = jnp.zeros_like(dw_ref)

        for kk in range(3):
            dw_ref[kk:kk + 1, :] += jnp.sum(dconv * taps[kk], axis=0, keepdims=True)

    def col(j):
        return pl.BlockSpec((_SC_T, d), lambda i: (i, j))

    def halo(j):
        return pl.BlockSpec((_SC_HALO, d), lambda i: (jnp.maximum(i * hb - 1, 0), j))

    def nxt(j):
        return pl.BlockSpec((_SC_HALO, d), lambda i: (jnp.minimum((i + 1) * hb, s // _SC_HALO - 1), j))

    return _pcall(
        body, name="short_conv_bwd", grid=(nblk,),
        in_specs=[col(0), col(1), col(2), halo(1), halo(2), nxt(0), col(0), nxt(0),
                  pl.BlockSpec((3, d), lambda i: (0, 0))],
        out_specs=[pl.BlockSpec((_SC_T, d3), lambda i: (i, 0)), pl.BlockSpec((8, d), lambda i: (0, 0))],
        out_shape=[jax.ShapeDtypeStruct((s, d3), BF16), jax.ShapeDtypeStruct((8, d), F32)],
        scratch=[pltpu.VMEM((_SC_T + _SC_HALO, d), F32), pltpu.VMEM((_SC_T + _SC_HALO, d), F32)],
        rider=rider,
    )(u2, u2, u2, u2, u2, u2, dsc, dsc, w)


def _bucket_maps():
    a_idx = jnp.arange(_STEPS)[:, None]
    c_idx = jnp.arange(2 * _STEPS)[None, :]
    mdist = jnp.clip(a_idx + _STEPS - c_idx, 0, _STEPS)
    max_exact = _NUM_BUCKETS // 2
    maps = []
    for _, dil in _GROUPS:
        nn = mdist * dil
        nf = jnp.maximum(nn, 1).astype(F32)
        large = max_exact + (jnp.log(nf / max_exact) / math.log(_MAX_DISTANCE / max_exact)
                             * (_NUM_BUCKETS - max_exact)).astype(jnp.int32)
        maps.append(jnp.where(nn < max_exact, nn, jnp.minimum(large, _NUM_BUCKETS - 1)).astype(jnp.int32))
    return jnp.stack(maps, axis=0)


def _bias_expand(rel_bias, buckets):
    nh = rel_bias.shape[1]

    def body(rb_ref, bk_ref, o_ref):
        h = pl.program_id(0)
        bk = bk_ref[0]
        acc = jnp.zeros(bk.shape, F32)
        for b in range(_NUM_BUCKETS):
            acc = jnp.where(bk == b, rb_ref[b, h], acc)
        a = lax.broadcasted_iota(jnp.int32, bk.shape, 0)
        c = lax.broadcasted_iota(jnp.int32, bk.shape, 1)
        mdist = a + _STEPS - c
        o_ref[0] = jnp.where((mdist >= 0) & (mdist <= _STEPS), acc, _NEG)

    return _pcall(
        body, name="bias_expand", grid=(nh,),
        in_specs=[pl.BlockSpec(memory_space=pltpu.SMEM),
                  pl.BlockSpec((1, _STEPS, 2 * _STEPS), lambda h: (h // 8, 0, 0))],
        out_specs=pl.BlockSpec((1, _STEPS, 2 * _STEPS), lambda h: (h, 0, 0)),
        out_shape=jax.ShapeDtypeStruct((nh, _STEPS, 2 * _STEPS), F32))(rel_bias, buckets)


def _bias_reduce(ds_all, buckets):
    nh = ds_all.shape[0]

    def body(ds_ref, bk_ref, o_ref):
        t, bk = ds_ref[0], bk_ref[0]
        rows = lax.broadcasted_iota(jnp.int32, (_NUM_BUCKETS, _LANES), 0)
        out = jnp.zeros((_NUM_BUCKETS, _LANES), F32)
        for b in range(_NUM_BUCKETS):
            out = jnp.where(rows == b, jnp.sum(jnp.where(bk == b, t, 0.0)), out)
        o_ref[0] = out

    blk = pl.BlockSpec((1, _STEPS, 2 * _STEPS), lambda h: (h, 0, 0))
    return _pcall(
        body, name="bias_reduce", grid=(nh,),
        in_specs=[blk, pl.BlockSpec((1, _STEPS, 2 * _STEPS), lambda h: (h // 8, 0, 0))],
        out_specs=pl.BlockSpec((1, _NUM_BUCKETS, _LANES), lambda h: (h, 0, 0)),
        out_shape=jax.ShapeDtypeStruct((nh, _NUM_BUCKETS, _LANES), F32))(ds_all, buckets)


def _sub_residues(dil):
    return 4 if dil % 16 == 0 else 1


def _strided_rows(ref, tmp_ref, p, r, dil):
    sub = _sub_residues(dil)
    if dil == 1:
        return [ref[p]]
    if sub == 1:
        return [ref[p, pl.ds(r, _STEPS, stride=dil), :]]
    tmp_ref[...] = ref[p, pl.ds(r, _STEPS * sub, stride=dil // sub), :]
    return [tmp_ref[pl.ds(q, _STEPS, stride=sub), :] for q in range(sub)]


def _store_strided(ref, tmp_ref, p, r, dil, vals):
    sub = _sub_residues(dil)
    if dil == 1:
        ref[p] = vals[0]
    elif sub == 1:
        ref[p, pl.ds(r, _STEPS, stride=dil), :] = vals[0]
    else:
        for q, val in enumerate(vals):
            tmp_ref[pl.ds(q, _STEPS, stride=sub), :] = val
        ref[p, pl.ds(r, _STEPS * sub, stride=dil // sub), :] = tmp_ref[...]


def _tmp_rows(dil, count):
    sub = _sub_residues(dil)
    return [pltpu.VMEM((_STEPS * sub, _LANES), F32)] * count if sub > 1 else []


def _head_masks():
    lane = lax.broadcasted_iota(jnp.int32, (1, _LANES), 1)
    return [lane < _HEAD_DIM, lane >= _HEAD_DIM]


def _scores(qm, k2, bias, first):
    sc = lax.dot_general(qm, k2, (((1,), (1,)), ((), ())), preferred_element_type=F32)
    sc = sc * (_HEAD_DIM ** -0.5) + bias
    col = lax.broadcasted_iota(jnp.int32, sc.shape, 1)
    return jnp.where(jnp.logical_and(first, col < _STEPS), _NEG, sc)


_PAIRS = _GROUP_COLS // _LANES


def _attn_fwd(uq, uk, uv, bias, g, dil, pp, rider=None):
    s = uq.shape[1]
    rb = _STEPS * dil
    nb = s // rb
    npb = _PAIRS // pp

    sub = _sub_residues(dil)

    def body(q_ref, kc_ref, kp_ref, vc_ref, vp_ref, b_ref, o_ref, l_ref, *tmp):
        tmp = tmp + (None,) * 7
        n, r = pl.program_id(1), pl.program_id(2)
        first = n == 0
        masks = _head_masks()
        for j in range(pp):
            qs = _strided_rows(q_ref, tmp[0], j, r, dil)
            kps, kcs = _strided_rows(kp_ref, tmp[1], j, r, dil), _strided_rows(kc_ref, tmp[2], j, r, dil)
            vps, vcs = _strided_rows(vp_ref, tmp[3], j, r, dil), _strided_rows(vc_ref, tmp[4], j, r, dil)
            o_res, l_res = [], []
            for q in range(sub):
                q2 = qs[q].astype(BF16)
                k2 = jnp.concatenate([kps[q], kcs[q]], axis=0).astype(BF16)
                v2 = jnp.concatenate([vps[q], vcs[q]], axis=0).astype(BF16)
                o_pair = jnp.zeros((_STEPS, _LANES), F32)
                l_pair = jnp.zeros((_STEPS, _LANES), F32)
                for hh in range(2):
                    mk = masks[hh]
                    sc = _scores(jnp.where(mk, q2, 0), k2, b_ref[2 * j + hh], first)
                    mx = jnp.max(sc, axis=-1, keepdims=True)
                    p = jnp.exp(sc - mx)
                    den = jnp.sum(p, axis=-1, keepdims=True)
                    oh = jnp.dot(p.astype(BF16), jnp.where(mk, v2, 0), preferred_element_type=F32)
                    o_pair = o_pair + oh / den
                    l_pair = jnp.where(mk, mx + jnp.log(den), l_pair)
                o_res.append(o_pair)
                l_res.append(l_pair)
            _store_strided(o_ref, tmp[5], j, r, dil, o_res)
            _store_strided(l_ref, tmp[6], j, r, dil, l_res)

    cur = pl.BlockSpec((pp, rb, _LANES), lambda hb, n, r: (g * npb + hb, n, 0))
    prev = pl.BlockSpec((pp, rb, _LANES), lambda hb, n, r: (g * npb + hb, jnp.maximum(n - 1, 0), 0))
    bspec = pl.BlockSpec((2 * pp, _STEPS, 2 * _STEPS), lambda hb, n, r: (g * npb + hb, 0, 0))
    ospec = pl.BlockSpec((pp, rb, _LANES), lambda hb, n, r: (hb, n, 0))
    sh = jax.ShapeDtypeStruct((_PAIRS, s, _LANES), F32)
    return _pcall(
        body, name=f"attn_fwd_g{g}", grid=(npb, nb, dil // sub),
        in_specs=[cur, cur, prev, cur, prev, bspec], out_specs=[ospec, ospec], out_shape=[sh, sh],
        scratch=_tmp_rows(dil, 7), rider=rider,
    )(uq, uk, uk, uv, uv, bias)


def _attn_merge(outs, lses, cat):
    s = outs[0].shape[1]
    c = _GROUP_COLS

    def body(o0, o1, o2, l0, l1, l2, cat_in, cat_ref, lse_ref):
        del cat_in
        a0, a1, a2 = l0[...], l1[...], l2[...]
        mx = jnp.maximum(jnp.maximum(a0, a1), a2)
        w0, w1, w2 = jnp.exp(a0 - mx), jnp.exp(a1 - mx), jnp.exp(a2 - mx)
        den = w0 + w1 + w2
        y = ((w0 * o0[...] + w1 * o1[...] + w2 * o2[...]) / den).astype(BF16)
        for p in range(_PAIRS):
            cat_ref[:, p * _LANES:(p + 1) * _LANES] = y[p]
        lse_ref[...] = mx + jnp.log(den)

    blk = pl.BlockSpec((_PAIRS, _ROW_T, _LANES), lambda i: (0, i, 0))
    return _pcall(
        body, name="attn_merge", grid=(s // _ROW_T,),
        in_specs=[blk] * 6 + [_ANY],
        out_specs=[pl.BlockSpec((_ROW_T, c), lambda i: (i, 1)), blk],
        out_shape=[jax.ShapeDtypeStruct(cat.shape, BF16), jax.ShapeDtypeStruct((_PAIRS, s, _LANES), F32)],
        aliases={6: 0})(*outs, *lses, cat)


def _attn_delta(dcat, cat):
    s = dcat.shape[0]
    c = _GROUP_COLS
    seg = (jnp.arange(c)[:, None] // _HEAD_DIM == jnp.arange(c)[None, :] // _HEAD_DIM).astype(BF16)

    def body(dy_ref, y_ref, seg_ref, dl_ref, dys_ref):
        dy = dy_ref[...]
        prod = dy * y_ref[...].astype(F32)
        hi = prod.astype(BF16)
        lo = (prod - hi.astype(F32)).astype(BF16)
        dl = (jnp.dot(hi, seg_ref[...], preferred_element_type=F32)
              + jnp.dot(lo, seg_ref[...], preferred_element_type=F32))
        for p in range(_PAIRS):
            dl_ref[p] = dl[:, p * _LANES:(p + 1) * _LANES]
            dys_ref[p] = dy[:, p * _LANES:(p + 1) * _LANES]

    right = pl.BlockSpec((_ROW_T, c), lambda i: (i, 1))
    blk = pl.BlockSpec((_PAIRS, _ROW_T, _LANES), lambda i: (0, i, 0))
    sh = jax.ShapeDtypeStruct((_PAIRS, s, _LANES), F32)
    return _pcall(
        body, name="attn_delta", grid=(s // _ROW_T,),
        in_specs=[right, right, pl.BlockSpec((c, c), lambda i: (0, 0))],
        out_specs=[blk, blk], out_shape=[sh, sh])(dcat, cat, seg)


def _attn_bwd(uq, uk, uv, dys, lse, delta, bias, prev_grads, g, dil, pp, rider=None):
    s = uq.shape[1]
    rb = _STEPS * dil
    nb = s // rb
    npb = _PAIRS // pp
    scale = _HEAD_DIM ** -0.5

    sub = _sub_residues(dil)

    def body(q_ref, kc_ref, kp_ref, vc_ref, vp_ref, dy_ref, l_ref, dl_ref, b_ref, *rest):
        rest = rest[len(prev_grads):]
        dq_ref, dk_ref, dv_ref, dsa_ref, dkc_ref, dvc_ref = rest[:6]
        tmp = rest[6:] + (None,) * 11
        n, r = pl.program_id(1), pl.program_id(2)

        def carry_slot(j, q):
            return ((r + (dil // sub) * q) * pp + j) if sub > 1 else r * pp + j

        @pl.when(jnp.logical_and(n == 0, r == 0))
        def _():
            dsa_ref[...] = jnp.zeros_like(dsa_ref)

        @pl.when(n == 0)
        def _():
            for j in range(pp):
                for q in range(sub):
                    dkc_ref[carry_slot(j, q)] = jnp.zeros((_STEPS, _LANES), F32)
                    dvc_ref[carry_slot(j, q)] = jnp.zeros((_STEPS, _LANES), F32)

        @pl.when(n < nb)
        def _():
            first = n == 0
            masks = _head_masks()
            for j in range(pp):
                qs = _strided_rows(q_ref, tmp[0], j, r, dil)
                kps, kcs = _strided_rows(kp_ref, tmp[1], j, r, dil), _strided_rows(kc_ref, tmp[2], j, r, dil)
                vps, vcs = _strided_rows(vp_ref, tmp[3], j, r, dil), _strided_rows(vc_ref, tmp[4], j, r, dil)
                dys_ = _strided_rows(dy_ref, tmp[5], j, r, dil)
                lses = _strided_rows(l_ref, tmp[6], j, r, dil)
                dls = _strided_rows(dl_ref, tmp[7], j, r, dil)
                ds_sum = [jnp.zeros((_STEPS, 2 * _STEPS), F32)] * 2
                dq_res, dk_res, dv_res = [], [], []
                for q in range(sub):
                    q2 = qs[q].astype(BF16)
                    k2 = jnp.concatenate([kps[q], kcs[q]], axis=0).astype(BF16)
                    v2 = jnp.concatenate([vps[q], vcs[q]], axis=0).astype(BF16)
                    dy2 = dys_[q].astype(BF16)
                    dq_p = jnp.zeros((_STEPS, _LANES), F32)
                    dk_p = jnp.zeros((2 * _STEPS, _LANES), F32)
                    dv_p = jnp.zeros((2 * _STEPS, _LANES), F32)
                    for hh in range(2):
                        mk = masks[hh]
                        lane0 = hh * _HEAD_DIM
                        qm, km, dym = jnp.where(mk, q2, 0), jnp.where(mk, k2, 0), jnp.where(mk, dy2, 0)
                        sc = _scores(qm, k2, b_ref[2 * j + hh], first)
                        p = jnp.exp(sc - lses[q][:, lane0:lane0 + 1])
                        dp = lax.dot_general(dym, v2, (((1,), (1,)), ((), ())), preferred_element_type=F32)
                        ds = p * (dp - dls[q][:, lane0:lane0 + 1])
                        ds_sum[hh] = ds_sum[hh] + ds
                        dsb = ds.astype(BF16)
                        dq_p = dq_p + jnp.dot(dsb, km, preferred_element_type=F32)
                        dk_p = dk_p + lax.dot_general(dsb, qm, (((0,), (0,)), ((), ())),
                                                      preferred_element_type=F32)
                        dv_p = dv_p + lax.dot_general(p.astype(BF16), dym, (((0,), (0,)), ((), ())),
                                                      preferred_element_type=F32)
                    dk_p = dk_p * scale
                    slot = carry_slot(j, q)
                    dq_res.append(dq_p * scale)
                    dk_res.append(dkc_ref[slot] + dk_p[:_STEPS])
                    dv_res.append(dvc_ref[slot] + dv_p[:_STEPS])
                    dkc_ref[slot] = dk_p[_STEPS:]
                    dvc_ref[slot] = dv_p[_STEPS:]
                for hh in range(2):
                    dsa_ref[2 * j + hh] += ds_sum[hh]
                _store_strided(dq_ref, tmp[8], j, r, dil, dq_res)
                _store_strided(dk_ref, tmp[9], j, r, dil, dk_res)
                _store_strided(dv_ref, tmp[10], j, r, dil, dv_res)

        @pl.when(n == nb)
        def _():
            for j in range(pp):
                _store_strided(dk_ref, tmp[9], j, r, dil, [dkc_ref[carry_slot(j, q)] for q in range(sub)])
                _store_strided(dv_ref, tmp[10], j, r, dil, [dvc_ref[carry_slot(j, q)] for q in range(sub)])

    def clamp(n):
        return jnp.minimum(n, nb - 1)

    cur = pl.BlockSpec((pp, rb, _LANES), lambda hb, n, r: (g * npb + hb, clamp(n), 0))
    prev = pl.BlockSpec((pp, rb, _LANES), lambda hb, n, r: (g * npb + hb, jnp.maximum(clamp(n) - 1, 0), 0))
    stat = pl.BlockSpec((pp, rb, _LANES), lambda hb, n, r: (hb, clamp(n), 0))
    bspec = pl.BlockSpec((2 * pp, _STEPS, 2 * _STEPS), lambda hb, n, r: (g * npb + hb, 0, 0))
    dkspec = pl.BlockSpec((pp, rb, _LANES), lambda hb, n, r: (g * npb + hb, jnp.maximum(n - 1, 0), 0))
    dsspec = pl.BlockSpec((2 * pp, _STEPS, 2 * _STEPS), lambda hb, n, r: (hb, 0, 0))
    wide = jax.ShapeDtypeStruct((3 * _PAIRS, s, _LANES), F32)
    np_ = len(prev_grads)
    return _pcall(
        body, name=f"attn_bwd_g{g}", grid=(npb, nb + 1, dil // sub),
        in_specs=[cur, cur, prev, cur, prev, stat, stat, stat, bspec] + [_ANY] * np_,
        out_specs=[cur, dkspec, dkspec, dsspec],
        out_shape=[wide, wide, wide, jax.ShapeDtypeStruct((8, _STEPS, 2 * _STEPS), F32)],
        scratch=[pltpu.VMEM((dil * pp, _STEPS, _LANES), F32), pltpu.VMEM((dil * pp, _STEPS, _LANES), F32)]
        + _tmp_rows(dil, 11),
        aliases={9 + t: t for t in range(np_)}, rider=rider,
    )(uq, uk, uk, uv, uv, dys, lse, delta, bias, *prev_grads)


def _place():
    x, y, c = lax.axis_index("x"), lax.axis_index("y"), lax.axis_index("c")
    chips = [(1 - x, y), (x, 1 - y), (1 - x, 1 - y)]
    return x, y, c, chips


def _slab(ref, axis, chip, width):
    start = pl.multiple_of(chip * width, width)
    if axis == 0:
        return ref.at[pl.ds(start, width), :]
    return ref.at[:, pl.ds(start, width)]


def _gather_rider(shards, axes):
    nw = len(shards)
    fulls = []
    for sh, ax in zip(shards, axes):
        shape = list(sh.shape)
        shape[ax] *= 4
        fulls.append(jax.ShapeDtypeStruct(tuple(shape), sh.dtype))

    def copies(ins, outs, scr):
        send, recv, loc = scr
        x, y, c, chips = _place()
        mine = 2 * x + y
        own, sends, arrivals = [], [], []
        for t in range(nw):
            width = ins[t].shape[axes[t]]
            own.append(pltpu.make_async_copy(ins[t], _slab(outs[t], axes[t], mine, width), loc.at[t]))
            for j, (px, py) in enumerate(chips):
                sems = dict(send_sem=send.at[3 * t + j], recv_sem=recv.at[3 * t + j],
                            device_id=(px, py, c), device_id_type=MESH)
                sends.append(pltpu.make_async_remote_copy(
                    src_ref=ins[t], dst_ref=_slab(outs[t], axes[t], mine, width), **sems))
                arrivals.append(pltpu.make_async_remote_copy(
                    src_ref=ins[t], dst_ref=_slab(outs[t], axes[t], 2 * px + py, width), **sems))
        return own, sends, arrivals

    def start(ins, outs, scr):
        own, sends, _ = copies(ins, outs, scr)
        for cp in own + sends:
            cp.start()

    def finish(ins, outs, scr):
        own, sends, arrivals = copies(ins, outs, scr)
        for cp in arrivals:
            cp.wait_recv()
        for cp in own:
            cp.wait()
        for cp in sends:
            cp.wait_send()

    return _Rider(shards, fulls, [pltpu.SemaphoreType.DMA((3 * nw,)), pltpu.SemaphoreType.DMA((3 * nw,)),
                                  pltpu.SemaphoreType.DMA((nw,))], start, finish)


def _run_rider(rider, name):
    nin, nout = len(rider.ins), len(rider.out_shapes)

    def body(*refs):
        ins, outs, scr = refs[:nin], refs[nin:nin + nout], refs[nin + nout:]
        rider.start(ins, outs, scr)
        rider.finish(ins, outs, scr)

    return _pcall(body, name=name, in_specs=[_ANY] * nin, out_specs=[_ANY] * nout, out_shape=rider.out_shapes,
                  scratch=rider.scratch)(*rider.ins)


def _gather_halves_rider(shard, axis):
    shape = list(shard.shape)
    shape[axis] *= 4
    full = jax.ShapeDtypeStruct(tuple(shape), shard.dtype)
    half = shard.shape[0] // 2
    width = shard.shape[axis]

    def region(out, chip, core):
        if axis == 0:
            return out.at[pl.ds(pl.multiple_of(chip * width + core * half, half), half), :]
        return out.at[pl.ds(pl.multiple_of(core * half, half), half), pl.ds(pl.multiple_of(chip * width, width), width)]

    def copies(ins, outs, scr):
        send, recv, loc = scr
        (src,), (out,) = ins, outs
        x, y, c, chips = _place()
        mine = 2 * x + y
        own = pltpu.make_async_copy(src, _slab(out, axis, mine, width), loc.at[0])
        my_half = src.at[pl.ds(pl.multiple_of(c * half, half), half), :]
        over_ici, ici_in, to_sib, sib_in = [], [], [], []
        for j, (px, py) in enumerate(chips):
            theirs = 2 * px + py
            ici = dict(send_sem=send.at[j], recv_sem=recv.at[j], device_id=(px, py, c), device_id_type=MESH)
            d2d = dict(send_sem=send.at[3 + j], recv_sem=recv.at[3 + j], device_id=(x, y, 1 - c),
                       device_id_type=MESH)
            over_ici.append(pltpu.make_async_remote_copy(src_ref=my_half, dst_ref=region(out, mine, c), **ici))
            ici_in.append(pltpu.make_async_remote_copy(src_ref=my_half, dst_ref=region(out, theirs, c), **ici))
            to_sib.append(pltpu.make_async_remote_copy(
                src_ref=region(out, theirs, c), dst_ref=region(out, theirs, c), **d2d))
            sib_in.append(pltpu.make_async_remote_copy(
                src_ref=region(out, theirs, c), dst_ref=region(out, theirs, 1 - c), **d2d))
        return own, over_ici, ici_in, to_sib, sib_in

    def start(ins, outs, scr):
        own, over_ici, _, _, _ = copies(ins, outs, scr)
        own.start()
        for cp in over_ici:
            cp.start()

    def finish(ins, outs, scr):
        own, over_ici, ici_in, to_sib, sib_in = copies(ins, outs, scr)
        for j in range(3):
            ici_in[j].wait_recv()
            to_sib[j].start()
        for cp in sib_in:
            cp.wait_recv()
        own.wait()
        for cp in over_ici + to_sib:
            cp.wait_send()

    return _Rider([shard], [full], [pltpu.SemaphoreType.DMA((6,)), pltpu.SemaphoreType.DMA((6,)),
                                    pltpu.SemaphoreType.DMA((1,))], start, finish)


def _scatter_rider(grads, axes, rows=None):
    nw = len(grads)
    outs_shape = []
    for gr, ax in zip(grads, axes):
        shape = list(gr.shape)
        shape[ax] //= 4
        if rows is not None:
            assert ax == 1
            shape[0] = rows[1] - rows[0]
        outs_shape.append(jax.ShapeDtypeStruct((3,) + tuple(shape), gr.dtype))

    def copies(ins, outs, scr):
        send, recv = scr
        x, y, c, chips = _place()
        cps = []
        for t in range(nw):
            width = ins[t].shape[axes[t]] // 4
            src = ins[t] if rows is None else ins[t].at[pl.ds(rows[0], rows[1] - rows[0]), :]
            for j, (px, py) in enumerate(chips):
                cps.append(pltpu.make_async_remote_copy(
                    src_ref=_slab(src, axes[t], 2 * px + py, width), dst_ref=outs[t].at[j],
                    send_sem=send.at[3 * t + j], recv_sem=recv.at[3 * t + j],
                    device_id=(px, py, c), device_id_type=MESH))
        return cps

    def start(ins, outs, scr):
        for cp in copies(ins, outs, scr):
            cp.start()

    def finish(ins, outs, scr):
        cps = copies(ins, outs, scr)
        for cp in cps:
            cp.wait_recv()
        for cp in cps:
            cp.wait_send()

    return _Rider(grads, outs_shape, [pltpu.SemaphoreType.DMA((3 * nw,)), pltpu.SemaphoreType.DMA((3 * nw,))],
                  start, finish)


def _swap_rider(parts):
    nw = len(parts)

    def copies(ins, outs, scr):
        send, recv = scr
        x, y, c, _ = _place()
        return [pltpu.make_async_remote_copy(
            src_ref=ins[t], dst_ref=outs[t], send_sem=send.at[t], recv_sem=recv.at[t],
            device_id=(x, y, 1 - c), device_id_type=MESH) for t in range(nw)]

    def start(ins, outs, scr):
        for cp in copies(ins, outs, scr):
            cp.start()

    def finish(ins, outs, scr):
        cps = copies(ins, outs, scr)
        for cp in cps:
            cp.wait_recv()
        for cp in cps:
            cp.wait_send()

    return _Rider(parts, [jax.ShapeDtypeStruct(p.shape, p.dtype) for p in parts],
                  [pltpu.SemaphoreType.DMA((nw,)), pltpu.SemaphoreType.DMA((nw,))], start, finish)


def _sum_all_devices(buf, name):
    rows, cols = buf.shape

    def body(in_ref, o_ref, gat_ref, send, recv):
        x, y, c, _ = _place()
        me = 4 * x + 2 * y + c
        gat_ref[me] = in_ref[...]
        started = []
        for mask in range(1, 8):
            fx, fy, fc = (mask >> 2) & 1, (mask >> 1) & 1, mask & 1
            peer = (x + fx * (1 - 2 * x), y + fy * (1 - 2 * y), c + fc * (1 - 2 * c))
            cp = pltpu.make_async_remote_copy(
                src_ref=in_ref, dst_ref=gat_ref.at[me], send_sem=send.at[mask - 1], recv_sem=recv.at[mask - 1],
                device_id=peer, device_id_type=MESH)
            cp.start()
            started.append(cp)
        for cp in started:
            cp.wait_recv()
        for cp in started:
            cp.wait_send()
        acc = gat_ref[0]
        for t in range(1, 8):
            acc = acc + gat_ref[t]
        o_ref[...] = acc

    vm = pl.BlockSpec(memory_space=pltpu.VMEM)
    return _pcall(
        body, name=name, in_specs=[vm], out_specs=vm, out_shape=jax.ShapeDtypeStruct((rows, cols), F32),
        scratch=[pltpu.VMEM((8, rows, cols), F32), pltpu.SemaphoreType.DMA((7,)), pltpu.SemaphoreType.DMA((7,))],
    )(buf)


_UPD_T = 256


def _sum_partials(own, got, name):
    rows, cols = own.shape
    tr = min(_UPD_T, rows)

    def body(own_ref, got_ref, o_ref):
        acc = own_ref[...].astype(F32)
        for j in range(3):
            acc = acc + got_ref[j].astype(F32)
        o_ref[...] = acc

    blk = pl.BlockSpec((tr, cols), lambda i: (i, 0))
    return _pcall(
        body, name=name, grid=(rows // tr,),
        in_specs=[blk, pl.BlockSpec((3, tr, cols), lambda i: (0, i, 0))], out_specs=blk,
        out_shape=jax.ShapeDtypeStruct((rows, cols), F32))(own, got)


def _adamw_math(w, gr, m, v):
    m = _B1 * m + (1.0 - _B1) * gr
    v = _B2 * v + (1.0 - _B2) * (gr * gr)
    m_hat = m / (1.0 - _B1 ** _STEP)
    v_hat = v / (1.0 - _B2 ** _STEP)
    delta = -_LR * (m_hat / (jnp.sqrt(v_hat) + _EPS) + _WD * w)
    return delta, m, v


def _adamw(w, m, v, parts, name):
    rows, cols = w.shape
    tr = min(_UPD_T, rows)
    npart = len(parts)

    def body(w_ref, m_ref, v_ref, *rest):
        p_refs, (g_ref, d_ref, nm_ref, nv_ref) = rest[:npart], rest[npart:]
        gr = p_refs[0][...]
        for p in p_refs[1:]:
            gr = gr + p[...]
        delta, nm, nv = _adamw_math(w_ref[...], gr, m_ref[...], v_ref[...])
        g_ref[...] = gr
        d_ref[...] = delta
        nm_ref[...] = nm
        nv_ref[...] = nv

    blk = pl.BlockSpec((tr, cols), lambda i: (i, 0))
    sh = jax.ShapeDtypeStruct((rows, cols), F32)
    return _pcall(body, name=name, grid=(rows // tr,), in_specs=[blk] * (3 + npart), out_specs=[blk] * 4,
                  out_shape=[sh] * 4)(w, m, v, *parts)


_PACK_W = 1024


def _pack(arrs, rows):
    flat = []
    for a in arrs:
        f = a.reshape(-1).astype(F32)
        pad = (-f.shape[0]) % _PACK_W
        flat.append(jnp.pad(f, (0, pad)))
    f = jnp.concatenate(flat)
    f = jnp.pad(f, (0, rows * _PACK_W - f.shape[0]))
    return f.reshape(rows, _PACK_W)


def _unpack(buf, shapes):
    flat = buf.reshape(-1)
    out, pos = [], 0
    for sh in shapes:
        size = math.prod(sh)
        out.append(flat[pos:pos + size].reshape(sh))
        pos += size + ((-size) % _PACK_W)
    return out


def _pack_rows(shapes):
    total = sum(-(-math.prod(sh) // _PACK_W) for sh in shapes)
    return -(-total // 8) * 8


def kernel(x, rel_bias, ab_norm, ab_w_in, ab_conv_w, ab_conv_b, ab_ln_g, ab_ln_b, ab_w_out, sc_norm, sc_w_in, sc_conv_w, sc_w_out, mlp_norm, mlp_w_up, mlp_w_down, final_norm, loss_target, m_rel_bias, m_ab_norm, m_ab_w_in, m_ab_conv_w, m_ab_conv_b, m_ab_ln_g, m_ab_ln_b, m_ab_w_out, m_sc_norm, m_sc_w_in, m_sc_conv_w, m_sc_w_out, m_mlp_norm, m_mlp_w_up, m_mlp_w_down, m_final_norm, v_rel_bias, v_ab_norm, v_ab_w_in, v_ab_conv_w, v_ab_conv_b, v_ab_ln_g, v_ab_ln_b, v_ab_w_out, v_sc_norm, v_sc_w_in, v_sc_conv_w, v_sc_w_out, v_mlp_norm, v_mlp_w_up, v_mlp_w_down, v_final_norm):
    s, d = x.shape[1], x.shape[2]
    dff = 4 * d
    c = _GROUP_COLS
    chip = 2 * lax.axis_index("x") + lax.axis_index("y")
    on_c0 = (lax.axis_index("c") == 0).astype(F32)
    h0 = x[0]
    tgt = loss_target[0]

    cw_sh, scn_sh, scw_sh = ab_conv_w[0], sc_norm, sc_conv_w[0]
    conv_w_full = lax.dynamic_update_slice(jnp.zeros((_CONV_K, c), F32), cw_sh * on_c0, (0, chip * cw_sh.shape[1]))
    scn_full = lax.dynamic_update_slice(jnp.zeros((1, d), F32), scn_sh * on_c0, (0, chip * scn_sh.shape[1]))
    scw_full = lax.dynamic_update_slice(jnp.zeros((3, d), F32), scw_sh * on_c0, (0, chip * scw_sh.shape[1]))
    small_shapes = [(_CONV_K, c), (1, d), (3, d)]
    small = _sum_all_devices(_pack([conv_w_full, scn_full, scw_full], _pack_rows(small_shapes)), "gather_small")
    conv_w, sc_g, sc_cw = _unpack(small, small_shapes)

    w_shards = [ab_w_in[0], ab_w_out[0], sc_w_in[0], sc_w_out[0], mlp_w_up[0], mlp_w_up[1],
                mlp_w_down[0], mlp_w_down[1]]
    w_axes = [1, 0, 1, 0, 1, 1, 0, 0]
    wb = [w.astype(BF16) for w in w_shards]
    full_w = [None] * 8

    def gather(idx):
        return _gather_rider([wb[t] for t in idx], [w_axes[t] for t in idx])

    def put(idx, got_w):
        for t, w in zip(idx, got_w):
            full_w[t] = w

    buckets = _bucket_maps()
    bias = _bias_expand(rel_bias, buckets)
    n0, got_w = _rms_fwd(h0, ab_norm, "rms_fwd_ab", rider=_gather_halves_rider(wb[0], w_axes[0]))
    put([0], got_w)
    w_in = full_w[0]
    tm = min(1024, s)
    tm2 = min(2048, s)
    tmh = min(512, s)
    uc = _mm(n0, w_in, "nn", m=s, n=2 * c, k=d, tm=tm2, tn=2 * c, tk=d, out_dtype=BF16, name="proj_conv")
    uq, uk, uv = [], [], []
    for t, (nm, dst) in enumerate(zip("qkv", (uq, uk, uv))):
        res = _mm(n0, w_in, "nn", m=s, n=3 * c, k=d, tm=tm2, tn=c, tk=d, out_dtype=F32, name=f"proj_{nm}",
                  b_off=(0, 2 + 3 * t), split="o", rider=gather([1]) if t == 0 else None)
        if t == 0:
            res, got_w = res
            put([1], got_w)
        dst.append(res)
    uq, uk, uv = uq[0], uk[0], uv[0]
    (cat, ca), got_w = _conv_a_fwd(uc, conv_w, ab_conv_b, ab_ln_g, ab_ln_b, rider=gather([2]))
    put([2], got_w)
    outs, lses = [], []
    for g, (_, dil) in enumerate(_GROUPS):
        idx = ([4], [6], [3, 5])[g]
        (o, l), got_w = _attn_fwd(uq, uk, uv, bias, g, dil, 4 if dil <= 4 else 2, rider=gather(idx))
        put(idx, got_w)
        outs.append(o)
        lses.append(l)
    cat, lse = _attn_merge(outs, lses, cat)
    h1, n1 = _mm(cat, full_w[1], "nn", m=s, n=d, k=d, tm=tm, tn=d, tk=d, out_dtype=(F32, BF16), name="out_ab",
                 epi=_epi_add_rms, extras=(h0,), vecs=(mlp_norm[0:1],))

    def mlp_fwd(h, nrm, layer, next_gain=None, rider=None):
        zr = _mm(nrm, full_w[4 + layer], "nn", m=s, n=dff, k=d, tm=tmh, tn=dff, tk=d, out_dtype=BF16,
                 name=f"mlp_up{layer}", epi=_epi_relu, rider=rider)
        if rider is not None:
            zr, got_r = zr
            put([7], got_r)
        kw = dict(m=s, n=d, k=dff, tm=tmh, tn=d, tk=dff, name=f"mlp_down{layer}", a_pro=_square, extras=(h,))
        if next_gain is None:
            return zr, _mm(zr, full_w[6 + layer], "nn", out_dtype=F32, epi=_epi_add, **kw), None
        hn, nn = _mm(zr, full_w[6 + layer], "nn", out_dtype=(F32, BF16), epi=_epi_add_rms, vecs=(next_gain,), **kw)
        return zr, hn, nn

    zr0, h2, n2 = mlp_fwd(h1, n1, 0, next_gain=sc_g, rider=gather([7]))
    _, w_out, w_si, w_so, w_up0, w_up1, w_dn0, w_dn1 = full_w
    w_up, w_dn = [w_up0, w_up1], [w_dn0, w_dn1]
    u2 = _mm(n2, w_si, "nn", m=s, n=3 * d, k=d, tm=tm, tn=1024, tk=d, out_dtype=BF16, name="proj_sc")
    scv = _short_conv_fwd(u2, sc_cw)
    h3, n3 = _mm(scv, w_so, "nn", m=s, n=d, k=d, tm=tm, tn=d, tk=d, out_dtype=(F32, BF16), name="out_sc",
                 epi=_epi_add_rms, extras=(h2,), vecs=(mlp_norm[1:2],))
    zr1, h4, _ = mlp_fwd(h3, n3, 1)

    dh4, dh4b, g_final, loss_part = _loss_head(h4, tgt, final_norm.reshape(1, d))
    tkw = min(2048, s)

    big_grads, got, sums = [None] * 8, [None] * 8, [None] * 8

    def scatter(t):
        return _scatter_rider([big_grads[t]], [w_axes[t]])

    def own_slab(t):
        width = big_grads[t].shape[w_axes[t]] // 4
        return lax.dynamic_slice_in_dim(big_grads[t], chip * width, width, axis=w_axes[t])

    def arrived(t, got_t):
        got[t] = got_t[0]
        sums[t] = _sum_partials(own_slab(t), got[t], f"sum_partials{t}")

    def mlp_bwd(dh, dhb, h, nrm, zr, layer):
        dz = _mm(dhb, w_dn[layer], "nt", m=s, n=dff, k=d, tm=tmh, tn=dff, tk=d, out_dtype=BF16,
                 name=f"mlp_down{layer}_dx", epi=_epi_relu_sq_bwd, extras=(zr,))
        big_grads[6 + layer] = _mm(zr, dhb, "tn", m=dff, n=d, k=s, tm=1024, tn=d, tk=tkw, out_dtype=BF16,
                                   name=f"mlp_down{layer}_dw", a_pro=_square)
        big_grads[4 + layer] = _mm(nrm, dz, "tn", m=d, n=dff, k=s, tm=d, tn=1024, tk=tkw, out_dtype=BF16,
                                   name=f"mlp_up{layer}_dw")
        res, got_t = _mm(dz, w_up[layer], "nt", m=s, n=d, k=dff, tm=tmh, tn=d, tk=dff, out_dtype=(F32, BF16),
                         name=f"mlp_up{layer}_dx", rider=scatter(6 + layer), epi=_epi_rms_bwd,
                         extras=(h, dh), vecs=(mlp_norm[layer:layer + 1],), row_sum=True)
        arrived(6 + layer, got_t)
        return res

    dh3, dh3b, g_mn1 = mlp_bwd(dh4, dh4b, h3, n3, zr1, 1)

    dsc = _mm(dh3b, w_so, "nt", m=s, n=d, k=d, tm=tm, tn=d, tk=d, out_dtype=F32, name="out_sc_dx")
    big_grads[3] = _mm(scv, dh3b, "tn", m=d, n=d, k=s, tm=d, tn=d, tk=tkw, out_dtype=BF16, name="out_sc_dw")
    (du2, g_sccw8), got_t = _short_conv_bwd(u2, dsc, sc_cw, rider=scatter(5))
    arrived(5, got_t)
    big_grads[2], got_t = _mm(n2, du2, "tn", m=d, n=3 * d, k=s, tm=d, tn=1024, tk=tkw, out_dtype=BF16,
                              name="proj_sc_dw", rider=scatter(3))
    arrived(3, got_t)
    (dh2, dh2b, g_scn), got_t = _mm(
        du2, w_si, "nt", m=s, n=d, k=3 * d, tm=tmh, tn=d, tk=1536, out_dtype=(F32, BF16), name="proj_sc_dx",
        rider=scatter(2), epi=_epi_rms_bwd, extras=(h2, dh3), vecs=(sc_g,), row_sum=True)
    arrived(2, got_t)

    dh1, dh1b, g_mn0 = mlp_bwd(dh2, dh2b, h1, n1, zr0, 0)

    dcat = _mm(dh1b, w_out, "nt", m=s, n=d, k=d, tm=tm, tn=d, tk=d, out_dtype=F32, name="out_ab_dx")
    big_grads[1] = _mm(cat, dh1b, "tn", m=d, n=d, k=s, tm=d, tn=d, tk=tkw, out_dtype=BF16, name="out_ab_dw")
    (dca, conv_stats), got_t = _conv_a_bwd_ln(ca, dcat, ab_ln_g, ab_ln_b, rider=scatter(1))
    arrived(1, got_t)
    (duc, g_cw32), got_t = _conv_a_bwd_conv(uc, dca, conv_w, rider=scatter(4))
    arrived(4, got_t)
    delta, dys = _attn_delta(dcat, cat)

    grads_qkv, ds_list = [], []
    for g, (_, dil) in enumerate(_GROUPS):
        dq, dk, dv, dsa = _attn_bwd(uq, uk, uv, dys, lse, delta, bias, grads_qkv, g, dil, 4 if dil <= 4 else 1)
        grads_qkv = [dq, dk, dv]
        ds_list.append(dsa)
    g_bias = _bias_reduce(jnp.concatenate(ds_list, axis=0), buckets)[:, :, 0].T

    secs = [(duc, 2 * c, 0)] + [(grads_qkv[t], 3 * c, 2 + 3 * t) for t in range(3)]
    g_in_parts = []
    for t, (du, width, off) in enumerate(secs):
        rider = _swap_rider(sums[1:]) if t == 1 else None
        part = _mm(n0, du, "tn", m=d, n=width, k=s, tm=d, tn=c, tk=tkw, out_dtype=BF16, name=f"proj_ab_dw{t}",
                   rider=rider, split="b" if t else "")
        if rider is not None:
            part, sib_late = part
        g_in_parts.append(part)
    big_grads[0] = jnp.concatenate(g_in_parts, axis=1)
    dn0, got_half = None, {}
    for t, (du, width, off) in enumerate(secs):
        rider = None
        if t in (1, 2):
            rider = _scatter_rider([big_grads[0]], [w_axes[0]], rows=((t - 1) * d // 2, t * d // 2))
        dn0 = _mm(du, w_in, "nt", m=s, n=d, k=width, tm=tm, tn=d, tk=c, out_dtype=F32, name=f"proj_ab_dx{t}",
                  b_off=(0, off), rider=rider, split="a" if t else "",
                  **({} if dn0 is None else dict(epi=_epi_add, extras=(dn0,))))
        if rider is not None:
            dn0, (got_half[t],) = dn0
    grad_x, _, g_abn = _rms_bwd(dn0, h0, ab_norm, dh1, "rms_bwd_ab")
    own0 = own_slab(0)
    sums[0] = jnp.concatenate(
        [_sum_partials(own0[(t - 1) * d // 2:t * d // 2], got_half[t], f"sum_partials0_{t}") for t in (1, 2)], axis=0)
    sib = list(_run_rider(_swap_rider([sums[0]]), "swap_sibling_w_in")) + sib_late

    big_m = [m_ab_w_in[0], m_ab_w_out[0], m_sc_w_in[0], m_sc_w_out[0], m_mlp_w_up[0], m_mlp_w_up[1],
             m_mlp_w_down[0], m_mlp_w_down[1]]
    big_v = [v_ab_w_in[0], v_ab_w_out[0], v_sc_w_in[0], v_sc_w_out[0], v_mlp_w_up[0], v_mlp_w_up[1],
             v_mlp_w_down[0], v_mlp_w_down[1]]
    upd = [_adamw(w_shards[t], big_m[t], big_v[t], [sums[t], sib[t]], f"adamw{t}") for t in range(8)]

    full_shapes = [(_NUM_BUCKETS, rel_bias.shape[1]), (1, d), (_CONV_K, c), (1, c), (1, c), (1, c), (1, d),
                   (3, d), (2, d), (d,)]
    small_grads = [g_bias, g_abn, g_cw32[:_CONV_K], conv_stats[0:1], conv_stats[1:2], conv_stats[2:3], g_scn,
                   g_sccw8[:3], jnp.concatenate([g_mn0, g_mn1], axis=0), g_final.reshape(d)]
    tot = _unpack(_sum_all_devices(_pack(small_grads, _pack_rows(full_shapes)), "sum_small"), full_shapes)
    for idx, sh in ((2, cw_sh), (6, scn_sh), (7, scw_sh)):
        width = sh.shape[1]
        tot[idx] = lax.dynamic_slice_in_dim(tot[idx], chip * width, width, axis=1)
    sm_w = [rel_bias, ab_norm, cw_sh, ab_conv_b, ab_ln_g, ab_ln_b, scn_sh, scw_sh, mlp_norm, final_norm]
    sm_m = [m_rel_bias, m_ab_norm, m_ab_conv_w[0], m_ab_conv_b, m_ab_ln_g, m_ab_ln_b, m_sc_norm, m_sc_conv_w[0],
            m_mlp_norm, m_final_norm]
    sm_v = [v_rel_bias, v_ab_norm, v_ab_conv_w[0], v_ab_conv_b, v_ab_ln_g, v_ab_ln_b, v_sc_norm, v_sc_conv_w[0],
            v_mlp_norm, v_final_norm]
    sh_shapes = [tuple(t.shape) for t in tot]
    rows = _pack_rows(sh_shapes)
    sm_upd = _adamw(_pack(sm_w, rows), _pack(sm_m, rows), _pack(sm_v, rows), [_pack(tot, rows)], "adamw_small")
    sm_g, sm_d, sm_nm, sm_nv = [_unpack(buf, sh_shapes) for buf in sm_upd]

    loss = lax.psum(loss_part[0, 0], ("x", "y", "c"))

    def assemble(big, sm):
        up = jnp.stack([big[4], big[5]], axis=0)
        dn = jnp.stack([big[6], big[7]], axis=0)
        return [sm[0], sm[1], big[0][None], sm[2][None], sm[3], sm[4], sm[5], big[1][None], sm[6], big[2][None],
                sm[7][None], big[3][None], sm[8], up, dn, sm[9]]

    res = [loss, grad_x[None]]
    for kind, sm in enumerate((sm_g, sm_d, sm_nm, sm_nv)):
        res += assemble([u[kind] for u in upd], sm)
    return tuple(res)
```

```python
import functools
import math

import jax
import jax.numpy as jnp
from jax import lax
from jax.experimental import pallas as pl
from jax.experimental.pallas import tpu as pltpu

F32 = jnp.float32
BF16 = jnp.bfloat16
MESH = pl.DeviceIdType.MESH

_GROUPS = ((128, 1), (512, 4), (2048, 16))
_STEPS = 128
_HEAD_DIM = 64
_GROUP_COLS = 512
_NUM_BUCKETS = 32
_MAX_DISTANCE = 2048
_CONV_K = 31
_HALO = 32
_SC_HALO = 16
_RMS_EPS = 1e-6
_LN_EPS = 1e-5
_NEG = -1e30
_LANES = 128
_VMEM_LIMIT = 56 * 1024 * 1024

_LR, _B1, _B2, _EPS, _WD, _STEP = 0.001, 0.9, 0.999, 1e-08, 0.01, 10


class _Rider:
    def __init__(self, ins, out_shapes, scratch, start, finish):
        self.ins, self.out_shapes, self.scratch = list(ins), list(out_shapes), list(scratch)
        self.start, self.finish = start, finish


def _pcall(body, *, name, out_shape, in_specs, out_specs, grid=None, scratch=(), aliases=None, rider=None):
    kw = {} if grid is None else {"grid": grid}
    cparams = pltpu.CompilerParams(vmem_limit_bytes=_VMEM_LIMIT)
    if rider is None:
        return pl.pallas_call(
            body, name=name, out_shape=out_shape, in_specs=in_specs, out_specs=out_specs,
            scratch_shapes=list(scratch), input_output_aliases=aliases or {},
            compiler_params=cparams, **kw)
    single = not isinstance(out_specs, (list, tuple))
    ospecs = [out_specs] if single else list(out_specs)
    oshapes = [out_shape] if single else list(out_shape)
    nin, nout, nscr = len(in_specs), len(ospecs), len(scratch)
    rin, rout = len(rider.ins), len(rider.out_shapes)

    def wrapped(*refs):
        h_in, r_in = refs[:nin], refs[nin:nin + rin]
        p = nin + rin
        h_out, r_out = refs[p:p + nout], refs[p + nout:p + nout + rout]
        p += nout + rout
        h_scr, r_scr = refs[p:p + nscr], refs[p + nscr:]
        ids = [pl.program_id(a) for a in range(len(grid))]
        first = functools.reduce(jnp.logical_and, [i == 0 for i in ids])
        last = functools.reduce(jnp.logical_and, [i == g - 1 for i, g in zip(ids, grid)])

        @pl.when(first)
        def _():
            rider.start(r_in, r_out, r_scr)

        body(*h_in, *h_out, *h_scr)

        @pl.when(last)
        def _():
            rider.finish(r_in, r_out, r_scr)

    call = pl.pallas_call(
        wrapped, name=name, out_shape=oshapes + rider.out_shapes,
        in_specs=list(in_specs) + [_ANY] * rin, out_specs=ospecs + [_ANY] * rout,
        scratch_shapes=list(scratch) + rider.scratch, input_output_aliases=aliases or {},
        compiler_params=cparams, **kw)

    def run(*operands):
        res = call(*operands, *rider.ins)
        host = res[0] if single else list(res[:nout])
        return host, list(res[nout:])

    return run


def _sig(x):
    return 1.0 / (1.0 + jnp.exp(-x))


_ANY = pl.BlockSpec(memory_space=pl.ANY)


def _lanes_of(ref):
    parts = [ref[p] for p in range(ref.shape[0])]
    return parts[0] if len(parts) == 1 else jnp.concatenate(parts, axis=1)


def _mm(a, b, mode, *, m, n, k, tm, tn, tk, out_dtype, name, epi=None, extras=(), b_off=(0, 0), rider=None,
        split="", vecs=(), a_pro=None, row_sum=False):
    nk = k // tk
    assert m % tm == 0 and n % tn == 0 and k % tk == 0
    o0, o1 = b_off
    if mode == "nn":
        a_spec = pl.BlockSpec((tm, tk), lambda i, j, kk: (i, kk))
        b_spec = pl.BlockSpec((tk, tn), lambda i, j, kk: (kk + o0, j + o1))
        dn = (((1,), (0,)), ((), ()))
    elif mode == "nt":
        a_spec = pl.BlockSpec((tm, tk), lambda i, j, kk: (i, kk))
        if "a" in split:
            a_spec = pl.BlockSpec((tk // _LANES, tm, _LANES), lambda i, j, kk: (kk, i, 0))
        b_spec = pl.BlockSpec((tn, tk), lambda i, j, kk: (j + o0, kk + o1))
        dn = (((1,), (1,)), ((), ()))
    else:
        a_spec = pl.BlockSpec((tk, tm), lambda i, j, kk: (kk, i))
        b_spec = pl.BlockSpec((tk, tn), lambda i, j, kk: (kk + o0, j + o1))
        if "b" in split:
            b_spec = pl.BlockSpec((tn // _LANES, tk, _LANES), lambda i, j, kk: (j, kk, 0))
        dn = (((0,), (0,)), ((), ()))
    o_spec = pl.BlockSpec((tm, tn), lambda i, j, kk: (i, j))
    e_spec = o_spec
    if "o" in split:
        o_spec = pl.BlockSpec((tn // _LANES, tm, _LANES), lambda i, j, kk: (j, i, 0))
    v_spec = pl.BlockSpec((1, tn), lambda i, j, kk: (0, j))
    ne = len(extras) + len(vecs)
    multi = isinstance(out_dtype, tuple)
    dts = out_dtype if multi else (out_dtype,)
    no = len(dts)
    nr = 1 if row_sum else 0
    assert not row_sum or tn == n

    def body(a_ref, b_ref, *rest):
        ex, o_refs = rest[:ne], rest[ne:ne + no]
        av = _lanes_of(a_ref) if "a" in split else a_ref[...]
        bv = _lanes_of(b_ref) if "b" in split else b_ref[...]
        if av.dtype != BF16:
            av = av.astype(BF16)
        if bv.dtype != BF16:
            bv = bv.astype(BF16)
        if a_pro is not None:
            av = a_pro(av)
        p = lax.dot_general(av, bv, dn, preferred_element_type=F32)

        def fin(x):
            if epi is not None:
                x = epi(x, *[e[...] for e in ex])
            if row_sum:
                row, x = x[-1], (x[:-1] if multi else x[0])
                row_ref = rest[ne + no]

                @pl.when(pl.program_id(0) == 0)
                def _():
                    row_ref[...] = row

                @pl.when(pl.program_id(0) > 0)
                def _():
                    row_ref[...] += row

            for o_ref, val, dt in zip(o_refs, x if multi else (x,), dts):
                if "o" in split:
                    for p in range(tn // _LANES):
                        o_ref[p] = val[:, p * _LANES:(p + 1) * _LANES].astype(dt)
                else:
                    o_ref[...] = val.astype(dt)

        if nk == 1:
            fin(p)
        else:
            acc = rest[ne + no + nr]
            kk = pl.program_id(2)

            @pl.when(kk == 0)
            def _():
                acc[...] = p

            @pl.when(kk > 0)
            def _():
                acc[...] += p

            @pl.when(kk == nk - 1)
            def _():
                fin(acc[...])

    oshape = (n // _LANES, m, _LANES) if "o" in split else (m, n)
    shapes = [jax.ShapeDtypeStruct(oshape, dt) for dt in dts]
    ospecs = [o_spec] * no
    if row_sum:
        shapes.append(jax.ShapeDtypeStruct((1, n), F32))
        ospecs.append(v_spec)
    lone = not multi and not row_sum
    return _pcall(
        body, name=name, grid=(m // tm, n // tn, nk),
        in_specs=[a_spec, b_spec] + [e_spec] * len(extras) + [v_spec] * len(vecs),
        out_specs=ospecs[0] if lone else ospecs, out_shape=shapes[0] if lone else shapes,
        scratch=[pltpu.VMEM((tm, tn), F32)] if nk > 1 else [], rider=rider,
    )(a, b, *extras, *vecs)


def _dx_qkv(dq, dk, dv, w, prev, *, tm, tk, col0, name, rider=None):
    npair, s, _ = dq.shape
    n = w.shape[0]
    per = npair * _LANES // tk
    nk = 3 * per
    off = col0 // tk
    nt_dims = (((1,), (1,)), ((), ()))

    def a_spec(t):
        return pl.BlockSpec((tk // _LANES, tm, _LANES), lambda i, kk: (jnp.clip(kk - per * t, 0, per - 1), i, 0))

    def body(a0, a1, a2, b_ref, e_ref, o_ref, acc):
        kk = pl.program_id(1)
        bv = b_ref[...]

        @pl.when(kk == 0)
        def _():
            acc[...] = e_ref[...]

        for t, a_ref in enumerate((a0, a1, a2)):
            @pl.when(jnp.logical_and(kk >= per * t, kk < per * (t + 1)))
            def _():
                acc[...] += lax.dot_general(_lanes_of(a_ref).astype(BF16), bv, nt_dims, preferred_element_type=F32)

        @pl.when(kk == nk - 1)
        def _():
            o_ref[...] = acc[...]

    row = pl.BlockSpec((tm, n), lambda i, kk: (i, 0))
    return _pcall(
        body, name=name, grid=(s // tm, nk),
        in_specs=[a_spec(0), a_spec(1), a_spec(2), pl.BlockSpec((n, tk), lambda i, kk: (0, kk + off)), row],
        out_specs=row, out_shape=jax.ShapeDtypeStruct((s, n), F32),
        scratch=[pltpu.VMEM((tm, n), F32)], rider=rider)(dq, dk, dv, w, prev)


def _epi_add(x, r):
    return x + r


def _epi_relu(x):
    return jnp.maximum(x, 0.0)


def _square(x):
    return x * x


def _epi_relu_sq_bwd(da, zr):
    return da * (2.0 * zr.astype(F32))


def _epi_add_rms(x, r, g):
    h = x + r
    return h, h * lax.rsqrt(jnp.mean(h * h, axis=-1, keepdims=True) + _RMS_EPS) * g


def _epi_rms_bwd(dn, h, dh_in, g):
    dx, dg = _rms_bwd_math(dn, h, g)
    dh = dh_in + dx
    return dh, dh, dg


_ROW_T = 512


def _rms_fwd(h, g, name, rider=None):
    s, d = h.shape

    def body(h_ref, g_ref, o_ref):
        x = h_ref[...]
        r = lax.rsqrt(jnp.mean(x * x, axis=-1, keepdims=True) + _RMS_EPS)
        o_ref[...] = (x * r * g_ref[...]).astype(BF16)

    row = pl.BlockSpec((_ROW_T, d), lambda i: (i, 0))
    vec = pl.BlockSpec((1, d), lambda i: (0, 0))
    return _pcall(body, name=name, grid=(s // _ROW_T,), in_specs=[row, vec], out_specs=row,
                  out_shape=jax.ShapeDtypeStruct((s, d), BF16), rider=rider)(h, g)


def _rms_bwd_math(dn, x, g):
    r = lax.rsqrt(jnp.mean(x * x, axis=-1, keepdims=True) + _RMS_EPS)
    xhat = x * r
    dg = jnp.sum(dn * xhat, axis=0, keepdims=True)
    t = dn * g
    dx = r * (t - xhat * jnp.mean(t * xhat, axis=-1, keepdims=True))
    return dx, dg


def _rms_bwd(dn, h, g, dh_in, name):
    s, d = h.shape

    def body(dn_ref, h_ref, g_ref, dhi_ref, dh_ref, dhb_ref, dg_ref):
        dx, dg = _rms_bwd_math(dn_ref[...], h_ref[...], g_ref[...])
        dh = dhi_ref[...] + dx
        dh_ref[...] = dh
        dhb_ref[...] = dh.astype(BF16)

        @pl.when(pl.program_id(0) == 0)
        def _():
            dg_ref[...] = jnp.zeros_like(dg_ref)

        dg_ref[...] += dg

    row = pl.BlockSpec((_ROW_T, d), lambda i: (i, 0))
    vec = pl.BlockSpec((1, d), lambda i: (0, 0))
    return _pcall(
        body, name=name, grid=(s // _ROW_T,), in_specs=[row, row, vec, row], out_specs=[row, row, vec],
        out_shape=[jax.ShapeDtypeStruct((s, d), F32), jax.ShapeDtypeStruct((s, d), BF16),
                   jax.ShapeDtypeStruct((1, d), F32)])(dn, h, g, dh_in)


def _loss_head(h, tgt, g):
    s, d = h.shape

    def body(h_ref, t_ref, g_ref, dh_ref, dhb_ref, dg_ref, loss_ref):
        x, gv = h_ref[...], g_ref[...]
        r = lax.rsqrt(jnp.mean(x * x, axis=-1, keepdims=True) + _RMS_EPS)
        err = x * r * gv - t_ref[...]
        part = 0.5 * jnp.sum(jnp.mean(err * err, axis=-1, keepdims=True))
        dx, dg = _rms_bwd_math(err * (1.0 / d), x, gv)
        dh_ref[...] = dx
        dhb_ref[...] = dx.astype(BF16)

        @pl.when(pl.program_id(0) == 0)
        def _():
            dg_ref[...] = jnp.zeros_like(dg_ref)
            loss_ref[...] = jnp.zeros_like(loss_ref)

        dg_ref[...] += dg
        loss_ref[...] += jnp.full(loss_ref.shape, part, F32)

    row = pl.BlockSpec((_ROW_T, d), lambda i: (i, 0))
    vec = pl.BlockSpec((1, d), lambda i: (0, 0))
    one = pl.BlockSpec((1, _LANES), lambda i: (0, 0))
    return _pcall(
        body, name="loss_head", grid=(s // _ROW_T,), in_specs=[row, row, vec], out_specs=[row, row, vec, one],
        out_shape=[jax.ShapeDtypeStruct((s, d), F32), jax.ShapeDtypeStruct((s, d), BF16),
                   jax.ShapeDtypeStruct((1, d), F32), jax.ShapeDtypeStruct((1, _LANES), F32)])(h, tgt, g)


_CONV_T = 256
_CONV_RC = 64


def _conv_a_specs(s):
    c = _GROUP_COLS
    hb = _CONV_T // _HALO
    val = pl.BlockSpec((_CONV_T, c), lambda i: (i, 0))
    gate = pl.BlockSpec((_CONV_T, c), lambda i: (i, 1))
    hval = pl.BlockSpec((_HALO, c), lambda i: (jnp.maximum(i * hb - 1, 0), 0))
    hgate = pl.BlockSpec((_HALO, c), lambda i: (jnp.maximum(i * hb - 1, 0), 1))
    return val, gate, hval, hgate


def _fill_glu(val_ref, gate_ref, hval_ref, hgate_ref, hs_ref):
    i = pl.program_id(0)
    hs_ref[pl.ds(_HALO, _CONV_T), :] = val_ref[...].astype(F32) * _sig(gate_ref[...].astype(F32))
    halo = hval_ref[...].astype(F32) * _sig(hgate_ref[...].astype(F32))
    hs_ref[pl.ds(0, _HALO), :] = jnp.where(i > 0, halo, 0.0)


_SHIFT_ROWS = _CONV_T + _HALO - 8


def _fill_shifts(src_ref, sh_ref):
    for b in range(1, 8):
        sh_ref[b - 1] = src_ref[pl.ds(b, _SHIFT_ROWS), :]


def _tap_rows(src_ref, sh_ref, start, rows):
    b = start % 8
    if b == 0:
        return src_ref[pl.ds(start, rows), :]
    return sh_ref[b - 1, pl.ds(start - b, rows), :]


def _conv_rows(hs_ref, sh_ref, w_ref, r0, rows):
    off = _HALO - (_CONV_K - 1)
    acc = jnp.zeros((rows, _GROUP_COLS), F32)
    for kk in range(_CONV_K):
        acc = acc + w_ref[kk:kk + 1, :] * _tap_rows(hs_ref, sh_ref, r0 + off + kk, rows)
    return acc


def _ln_fwd(ca, g, b):
    mu = jnp.mean(ca, axis=-1, keepdims=True)
    xc = ca - mu
    rstd = lax.rsqrt(jnp.mean(xc * xc, axis=-1, keepdims=True) + _LN_EPS)
    xhat = xc * rstd
    return xhat, rstd, xhat * g + b


def _conv_a_fwd(uc, w, cb, lg, lb, rider=None):
    s = uc.shape[0]
    c = _GROUP_COLS

    def body(val_ref, gate_ref, hval_ref, hgate_ref, w_ref, cb_ref, lg_ref, lb_ref, o_ref, ca_ref, hs_ref, sh_ref):
        _fill_glu(val_ref, gate_ref, hval_ref, hgate_ref, hs_ref)
        _fill_shifts(hs_ref, sh_ref)
        for rc in range(_CONV_T // _CONV_RC):
            r0 = rc * _CONV_RC
            ca = _conv_rows(hs_ref, sh_ref, w_ref, r0, _CONV_RC) + cb_ref[...]
            ca_ref[pl.ds(r0, _CONV_RC), :] = ca
            _, _, ln = _ln_fwd(ca, lg_ref[...], lb_ref[...])
            o_ref[pl.ds(r0, _CONV_RC), :] = (ln * _sig(ln)).astype(BF16)

    val, gate, hval, hgate = _conv_a_specs(s)
    wspec = pl.BlockSpec((_CONV_K, c), lambda i: (0, 0))
    vec = pl.BlockSpec((1, c), lambda i: (0, 0))
    blk = pl.BlockSpec((_CONV_T, c), lambda i: (i, 0))
    return _pcall(
        body, name="conv_a_fwd", grid=(s // _CONV_T,),
        in_specs=[val, gate, hval, hgate, wspec, vec, vec, vec],
        out_specs=[blk, blk],
        out_shape=[jax.ShapeDtypeStruct((s, 2 * c), BF16), jax.ShapeDtypeStruct((s, c), F32)],
        scratch=[pltpu.VMEM((_CONV_T + _HALO, c), F32), pltpu.VMEM((7, _SHIFT_ROWS, c), F32)],
        rider=rider)(uc, uc, uc, uc, w, cb, lg, lb)


def _conv_a_bwd_ln(ca_all, dcat, lg, lb, rider=None):
    s = ca_all.shape[0]
    c = _GROUP_COLS

    def body(ca_ref, dy_ref, lg_ref, lb_ref, dca_ref, st_ref):
        @pl.when(pl.program_id(0) == 0)
        def _():
            st_ref[...] = jnp.zeros_like(st_ref)

        for rc in range(_CONV_T // _CONV_RC):
            r0 = rc * _CONV_RC
            ca = ca_ref[pl.ds(r0, _CONV_RC), :]
            xhat, rstd, ln = _ln_fwd(ca, lg_ref[...], lb_ref[...])
            sg = _sig(ln)
            dln = dy_ref[pl.ds(r0, _CONV_RC), :] * (sg * (1.0 + ln * (1.0 - sg)))
            dxh = dln * lg_ref[...]
            dca = rstd * (dxh - jnp.mean(dxh, axis=-1, keepdims=True)
                          - xhat * jnp.mean(dxh * xhat, axis=-1, keepdims=True))
            dca_ref[pl.ds(r0, _CONV_RC), :] = dca
            st_ref[0:1, :] += jnp.sum(dca, axis=0, keepdims=True)
            st_ref[1:2, :] += jnp.sum(dln * xhat, axis=0, keepdims=True)
            st_ref[2:3, :] += jnp.sum(dln, axis=0, keepdims=True)

    blk = pl.BlockSpec((_CONV_T, c), lambda i: (i, 0))
    vec = pl.BlockSpec((1, c), lambda i: (0, 0))
    st = pl.BlockSpec((8, c), lambda i: (0, 0))
    return _pcall(
        body, name="conv_a_bwd_ln", grid=(s // _CONV_T,),
        in_specs=[blk, blk, vec, vec], out_specs=[blk, st],
        out_shape=[jax.ShapeDtypeStruct((s, c), F32), jax.ShapeDtypeStruct((8, c), F32)],
        rider=rider)(ca_all, dcat, lg, lb)


def _conv_a_bwd_conv(uc, dca, w, rider=None):
    s = uc.shape[0]
    c = _GROUP_COLS
    nblk = s // _CONV_T
    hb = _CONV_T // _HALO
    off = _HALO - (_CONV_K - 1)

    def body(val_ref, gate_ref, hval_ref, hgate_ref, d_ref, dn_ref, w_ref, du_ref, dw_ref, hs_ref, ds_ref,
             hsh_ref, dsh_ref):
        i = pl.program_id(0)
        _fill_glu(val_ref, gate_ref, hval_ref, hgate_ref, hs_ref)
        ds_ref[pl.ds(0, _CONV_T), :] = d_ref[...]
        ds_ref[pl.ds(_CONV_T, _HALO), :] = jnp.where(i < nblk - 1, dn_ref[...], 0.0)
        _fill_shifts(hs_ref, hsh_ref)
        _fill_shifts(ds_ref, dsh_ref)

        @pl.when(i == 0)
        def _():
            dw_ref[...] = jnp.zeros_like(dw_ref)

        for rc in range(_CONV_T // _CONV_RC):
            r0 = rc * _CONV_RC
            dcur = ds_ref[pl.ds(r0, _CONV_RC), :]
            dh = jnp.zeros((_CONV_RC, c), F32)
            for kk in range(_CONV_K):
                dh = dh + w_ref[kk:kk + 1, :] * _tap_rows(ds_ref, dsh_ref, r0 + _CONV_K - 1 - kk, _CONV_RC)
                dw_ref[kk:kk + 1, :] += jnp.sum(dcur * _tap_rows(hs_ref, hsh_ref, r0 + off + kk, _CONV_RC),
                                                 axis=0, keepdims=True)
            v = val_ref[pl.ds(r0, _CONV_RC), :].astype(F32)
            sg = _sig(gate_ref[pl.ds(r0, _CONV_RC), :].astype(F32))
            du_ref[pl.ds(r0, _CONV_RC), pl.ds(0, c)] = (dh * sg).astype(BF16)
            du_ref[pl.ds(r0, _CONV_RC), pl.ds(c, c)] = (dh * v * sg * (1.0 - sg)).astype(BF16)

    val, gate, hval, hgate = _conv_a_specs(s)
    blk = pl.BlockSpec((_CONV_T, c), lambda i: (i, 0))
    nxt = pl.BlockSpec((_HALO, c), lambda i: (jnp.minimum((i + 1) * hb, s // _HALO - 1), 0))
    wspec = pl.BlockSpec((_CONV_K, c), lambda i: (0, 0))
    return _pcall(
        body, name="conv_a_bwd_conv", grid=(nblk,),
        in_specs=[val, gate, hval, hgate, blk, nxt, wspec],
        out_specs=[pl.BlockSpec((_CONV_T, 2 * c), lambda i: (i, 0)), pl.BlockSpec((_HALO, c), lambda i: (0, 0))],
        out_shape=[jax.ShapeDtypeStruct((s, 2 * c), BF16), jax.ShapeDtypeStruct((_HALO, c), F32)],
        scratch=[pltpu.VMEM((_CONV_T + _HALO, c), F32), pltpu.VMEM((_CONV_T + _HALO, c), F32),
                 pltpu.VMEM((7, _SHIFT_ROWS, c), F32), pltpu.VMEM((7, _SHIFT_ROWS, c), F32)],
        rider=rider,
    )(uc, uc, uc, uc, dca, dca, w)


_SC_T = 256


def _short_conv_fwd(u2, w):
    s, d3 = u2.shape
    d = d3 // 3
    hb = _SC_T // _SC_HALO

    def body(b_ref, c_ref, v_ref, hc_ref, hv_ref, w_ref, o_ref, cs_ref):
        i = pl.program_id(0)
        cs_ref[pl.ds(_SC_HALO, _SC_T), :] = c_ref[...].astype(F32) * v_ref[...].astype(F32)
        cs_ref[pl.ds(0, _SC_HALO), :] = jnp.where(i > 0, hc_ref[...].astype(F32) * hv_ref[...].astype(F32), 0.0)
        conv = (w_ref[0:1, :] * cs_ref[pl.ds(_SC_HALO - 2, _SC_T), :]
                + w_ref[1:2, :] * cs_ref[pl.ds(_SC_HALO - 1, _SC_T), :]
                + w_ref[2:3, :] * cs_ref[pl.ds(_SC_HALO, _SC_T), :])
        o_ref[...] = (b_ref[...].astype(F32) * conv).astype(BF16)

    def col(j):
        return pl.BlockSpec((_SC_T, d), lambda i: (i, j))

    def halo(j):
        return pl.BlockSpec((_SC_HALO, d), lambda i: (jnp.maximum(i * hb - 1, 0), j))

    return _pcall(
        body, name="short_conv_fwd", grid=(s // _SC_T,),
        in_specs=[col(0), col(1), col(2), halo(1), halo(2), pl.BlockSpec((3, d), lambda i: (0, 0))],
        out_specs=pl.BlockSpec((_SC_T, d), lambda i: (i, 0)),
        out_shape=jax.ShapeDtypeStruct((s, d), BF16),
        scratch=[pltpu.VMEM((_SC_T + _SC_HALO, d), F32)])(u2, u2, u2, u2, u2, w)


def _short_conv_bwd(u2, dsc, w, rider=None):
    s, d3 = u2.shape
    d = d3 // 3
    hb = _SC_T // _SC_HALO
    nblk = s // _SC_T

    def body(b_ref, c_ref, v_ref, hc_ref, hv_ref, nb_ref, d_ref, nd_ref, w_ref, du_ref, dw_ref, cs_ref, ds_ref):
        i = pl.program_id(0)
        cval, vval, bval = c_ref[...].astype(F32), v_ref[...].astype(F32), b_ref[...].astype(F32)
        cs_ref[pl.ds(_SC_HALO, _SC_T), :] = cval * vval
        cs_ref[pl.ds(0, _SC_HALO), :] = jnp.where(i > 0, hc_ref[...].astype(F32) * hv_ref[...].astype(F32), 0.0)
        dsc_cur = d_ref[...]
        dconv = dsc_cur * bval
        ds_ref[pl.ds(0, _SC_T), :] = dconv
        ds_ref[pl.ds(_SC_T, _SC_HALO), :] = jnp.where(i < nblk - 1, nd_ref[...] * nb_ref[...].astype(F32), 0.0)
        taps = [cs_ref[pl.ds(_SC_HALO - 2 + kk, _SC_T), :] for kk in range(3)]
        conv = w_ref[0:1, :] * taps[0] + w_ref[1:2, :] * taps[1] + w_ref[2:3, :] * taps[2]
        dcv = (w_ref[2:3, :] * dconv + w_ref[1:2, :] * ds_ref[pl.ds(1, _SC_T), :]
               + w_ref[0:1, :] * ds_ref[pl.ds(2, _SC_T), :])
        du_ref[:, pl.ds(0, d)] = (dsc_cur * conv).astype(BF16)
        du_ref[:, pl.ds(d, d)] = (dcv * vval).astype(BF16)
        du_ref[:, pl.ds(2 * d, d)] = (dcv * cval).astype(BF16)

        @pl.when(i == 0)
        def _():
            dw_ref[...] = jnp.zeros_like(dw_ref)

        for kk in range(3):
            dw_ref[kk:kk + 1, :] += jnp.sum(dconv * taps[kk], axis=0, keepdims=True)

    def col(j):
        return pl.BlockSpec((_SC_T, d), lambda i: (i, j))

    def halo(j):
        return pl.BlockSpec((_SC_HALO, d), lambda i: (jnp.maximum(i * hb - 1, 0), j))

    def nxt(j):
        return pl.BlockSpec((_SC_HALO, d), lambda i: (jnp.minimum((i + 1) * hb, s // _SC_HALO - 1), j))

    return _pcall(
        body, name="short_conv_bwd", grid=(nblk,),
        in_specs=[col(0), col(1), col(2), halo(1), halo(2), nxt(0), col(0), nxt(0),
                  pl.BlockSpec((3, d), lambda i: (0, 0))],
        out_specs=[pl.BlockSpec((_SC_T, d3), lambda i: (i, 0)), pl.BlockSpec((8, d), lambda i: (0, 0))],
        out_shape=[jax.ShapeDtypeStruct((s, d3), BF16), jax.ShapeDtypeStruct((8, d), F32)],
        scratch=[pltpu.VMEM((_SC_T + _SC_HALO, d), F32), pltpu.VMEM((_SC_T + _SC_HALO, d), F32)],
        rider=rider,
    )(u2, u2, u2, u2, u2, u2, dsc, dsc, w)


def _bucket_maps():
    a_idx = jnp.arange(_STEPS)[:, None]
    c_idx = jnp.arange(2 * _STEPS)[None, :]
    mdist = jnp.clip(a_idx + _STEPS - c_idx, 0, _STEPS)
    max_exact = _NUM_BUCKETS // 2
    maps = []
    for _, dil in _GROUPS:
        nn = mdist * dil
        nf = jnp.maximum(nn, 1).astype(F32)
        large = max_exact + (jnp.log(nf / max_exact) / math.log(_MAX_DISTANCE / max_exact)
                             * (_NUM_BUCKETS - max_exact)).astype(jnp.int32)
        maps.append(jnp.where(nn < max_exact, nn, jnp.minimum(large, _NUM_BUCKETS - 1)).astype(jnp.int32))
    return jnp.stack(maps, axis=0)


def _bias_expand(rel_bias, buckets):
    nh = rel_bias.shape[1]

    def body(rb_ref, bk_ref, o_ref):
        h = pl.program_id(0)
        bk = bk_ref[0]
        acc = jnp.zeros(bk.shape, F32)
        for b in range(_NUM_BUCKETS):
            acc = jnp.where(bk == b, rb_ref[b, h], acc)
        a = lax.broadcasted_iota(jnp.int32, bk.shape, 0)
        c = lax.broadcasted_iota(jnp.int32, bk.shape, 1)
        mdist = a + _STEPS - c
        o_ref[0] = jnp.where((mdist >= 0) & (mdist <= _STEPS), acc, _NEG)

    return _pcall(
        body, name="bias_expand", grid=(nh,),
        in_specs=[pl.BlockSpec(memory_space=pltpu.SMEM),
                  pl.BlockSpec((1, _STEPS, 2 * _STEPS), lambda h: (h // 8, 0, 0))],
        out_specs=pl.BlockSpec((1, _STEPS, 2 * _STEPS), lambda h: (h, 0, 0)),
        out_shape=jax.ShapeDtypeStruct((nh, _STEPS, 2 * _STEPS), F32))(rel_bias, buckets)


def _bias_reduce(ds_all, buckets):
    nh = ds_all.shape[0]

    def body(ds_ref, bk_ref, o_ref):
        t, bk = ds_ref[0], bk_ref[0]
        rows = lax.broadcasted_iota(jnp.int32, (_NUM_BUCKETS, _LANES), 0)
        out = jnp.zeros((_NUM_BUCKETS, _LANES), F32)
        for b in range(_NUM_BUCKETS):
            out = jnp.where(rows == b, jnp.sum(jnp.where(bk == b, t, 0.0)), out)
        o_ref[0] = out

    blk = pl.BlockSpec((1, _STEPS, 2 * _STEPS), lambda h: (h, 0, 0))
    return _pcall(
        body, name="bias_reduce", grid=(nh,),
        in_specs=[blk, pl.BlockSpec((1, _STEPS, 2 * _STEPS), lambda h: (h // 8, 0, 0))],
        out_specs=pl.BlockSpec((1, _NUM_BUCKETS, _LANES), lambda h: (h, 0, 0)),
        out_shape=jax.ShapeDtypeStruct((nh, _NUM_BUCKETS, _LANES), F32))(ds_all, buckets)


def _sub_residues(dil):
    return 4 if dil % 16 == 0 else 1


def _strided_rows(ref, tmp_ref, p, r, dil):
    sub = _sub_residues(dil)
    if dil == 1:
        return [ref[p]]
    if sub == 1:
        return [ref[p, pl.ds(r, _STEPS, stride=dil), :]]
    tmp_ref[...] = ref[p, pl.ds(r, _STEPS * sub, stride=dil // sub), :]
    return [tmp_ref[pl.ds(q, _STEPS, stride=sub), :] for q in range(sub)]


def _store_strided(ref, tmp_ref, p, r, dil, vals):
    sub = _sub_residues(dil)
    if dil == 1:
        ref[p] = vals[0]
    elif sub == 1:
        ref[p, pl.ds(r, _STEPS, stride=dil), :] = vals[0]
    else:
        for q, val in enumerate(vals):
            tmp_ref[pl.ds(q, _STEPS, stride=sub), :] = val
        ref[p, pl.ds(r, _STEPS * sub, stride=dil // sub), :] = tmp_ref[...]


def _tmp_rows(dil, count):
    sub = _sub_residues(dil)
    return [pltpu.VMEM((_STEPS * sub, _LANES), F32)] * count if sub > 1 else []


def _head_masks():
    lane = lax.broadcasted_iota(jnp.int32, (1, _LANES), 1)
    return [lane < _HEAD_DIM, lane >= _HEAD_DIM]


def _scores(qm, k2, bias, first):
    sc = lax.dot_general(qm, k2, (((1,), (1,)), ((), ())), preferred_element_type=F32)
    sc = sc * (_HEAD_DIM ** -0.5) + bias
    col = lax.broadcasted_iota(jnp.int32, sc.shape, 1)
    return jnp.where(jnp.logical_and(first, col < _STEPS), _NEG, sc)


_PAIRS = _GROUP_COLS // _LANES


def _attn_fwd(uq, uk, uv, bias, g, dil, pp, rider=None):
    s = uq.shape[1]
    rb = _STEPS * dil
    nb = s // rb
    npb = _PAIRS // pp

    sub = _sub_residues(dil)

    def body(q_ref, kc_ref, kp_ref, vc_ref, vp_ref, b_ref, o_ref, l_ref, *tmp):
        tmp = tmp + (None,) * 7
        n, r = pl.program_id(1), pl.program_id(2)
        first = n == 0
        masks = _head_masks()
        for j in range(pp):
            qs = _strided_rows(q_ref, tmp[0], j, r, dil)
            kps, kcs = _strided_rows(kp_ref, tmp[1], j, r, dil), _strided_rows(kc_ref, tmp[2], j, r, dil)
            vps, vcs = _strided_rows(vp_ref, tmp[3], j, r, dil), _strided_rows(vc_ref, tmp[4], j, r, dil)
            o_res, l_res = [], []
            for q in range(sub):
                q2 = qs[q].astype(BF16)
                k2 = jnp.concatenate([kps[q], kcs[q]], axis=0).astype(BF16)
                v2 = jnp.concatenate([vps[q], vcs[q]], axis=0).astype(BF16)
                o_pair = jnp.zeros((_STEPS, _LANES), F32)
                l_pair = jnp.zeros((_STEPS, _LANES), F32)
                for hh in range(2):
                    mk = masks[hh]
                    sc = _scores(jnp.where(mk, q2, 0), k2, b_ref[2 * j + hh], first)
                    mx = jnp.max(sc, axis=-1, keepdims=True)
                    p = jnp.exp(sc - mx)
                    den = jnp.sum(p, axis=-1, keepdims=True)
                    oh = jnp.dot(p.astype(BF16), jnp.where(mk, v2, 0), preferred_element_type=F32)
                    o_pair = o_pair + oh / den
                    l_pair = jnp.where(mk, mx + jnp.log(den), l_pair)
                o_res.append(o_pair)
                l_res.append(l_pair)
            _store_strided(o_ref, tmp[5], j, r, dil, o_res)
            _store_strided(l_ref, tmp[6], j, r, dil, l_res)

    cur = pl.BlockSpec((pp, rb, _LANES), lambda hb, n, r: (g * npb + hb, n, 0))
    prev = pl.BlockSpec((pp, rb, _LANES), lambda hb, n, r: (g * npb + hb, jnp.maximum(n - 1, 0), 0))
    bspec = pl.BlockSpec((2 * pp, _STEPS, 2 * _STEPS), lambda hb, n, r: (g * npb + hb, 0, 0))
    ospec = pl.BlockSpec((pp, rb, _LANES), lambda hb, n, r: (hb, n, 0))
    sh = jax.ShapeDtypeStruct((_PAIRS, s, _LANES), F32)
    return _pcall(
        body, name=f"attn_fwd_g{g}", grid=(npb, nb, dil // sub),
        in_specs=[cur, cur, prev, cur, prev, bspec], out_specs=[ospec, ospec], out_shape=[sh, sh],
        scratch=_tmp_rows(dil, 7), rider=rider,
    )(uq, uk, uk, uv, uv, bias)


def _attn_merge(outs, lses, cat):
    s = outs[0].shape[1]
    c = _GROUP_COLS

    def body(o0, o1, o2, l0, l1, l2, cat_in, cat_ref, lse_ref):
        del cat_in
        a0, a1, a2 = l0[...], l1[...], l2[...]
        mx = jnp.maximum(jnp.maximum(a0, a1), a2)
        w0, w1, w2 = jnp.exp(a0 - mx), jnp.exp(a1 - mx), jnp.exp(a2 - mx)
        den = w0 + w1 + w2
        y = ((w0 * o0[...] + w1 * o1[...] + w2 * o2[...]) / den).astype(BF16)
        for p in range(_PAIRS):
            cat_ref[:, p * _LANES:(p + 1) * _LANES] = y[p]
        lse_ref[...] = mx + jnp.log(den)

    blk = pl.BlockSpec((_PAIRS, _ROW_T, _LANES), lambda i: (0, i, 0))
    return _pcall(
        body, name="attn_merge", grid=(s // _ROW_T,),
        in_specs=[blk] * 6 + [_ANY],
        out_specs=[pl.BlockSpec((_ROW_T, c), lambda i: (i, 1)), blk],
        out_shape=[jax.ShapeDtypeStruct(cat.shape, BF16), jax.ShapeDtypeStruct((_PAIRS, s, _LANES), F32)],
        aliases={6: 0})(*outs, *lses, cat)


def _attn_delta(dcat, cat):
    s = dcat.shape[0]
    c = _GROUP_COLS
    seg = (jnp.arange(c)[:, None] // _HEAD_DIM == jnp.arange(c)[None, :] // _HEAD_DIM).astype(BF16)

    def body(dy_ref, y_ref, seg_ref, dl_ref, dys_ref):
        dy = dy_ref[...]
        prod = dy * y_ref[...].astype(F32)
        hi = prod.astype(BF16)
        lo = (prod - hi.astype(F32)).astype(BF16)
        dl = (jnp.dot(hi, seg_ref[...], preferred_element_type=F32)
              + jnp.dot(lo, seg_ref[...], preferred_element_type=F32))
        for p in range(_PAIRS):
            dl_ref[p] = dl[:, p * _LANES:(p + 1) * _LANES]
            dys_ref[p] = dy[:, p * _LANES:(p + 1) * _LANES]

    right = pl.BlockSpec((_ROW_T, c), lambda i: (i, 1))
    blk = pl.BlockSpec((_PAIRS, _ROW_T, _LANES), lambda i: (0, i, 0))
    sh = jax.ShapeDtypeStruct((_PAIRS, s, _LANES), F32)
    return _pcall(
        body, name="attn_delta", grid=(s // _ROW_T,),
        in_specs=[right, right, pl.BlockSpec((c, c), lambda i: (0, 0))],
        out_specs=[blk, blk], out_shape=[sh, sh])(dcat, cat, seg)


def _attn_bwd(uq, uk, uv, dys, lse, delta, bias, prev_grads, g, dil, pp, rider=None):
    s = uq.shape[1]
    rb = _STEPS * dil
    nb = s // rb
    npb = _PAIRS // pp
    scale = _HEAD_DIM ** -0.5

    sub = _sub_residues(dil)

    def body(q_ref, kc_ref, kp_ref, vc_ref, vp_ref, dy_ref, l_ref, dl_ref, b_ref, *rest):
        rest = rest[len(prev_grads):]
        dq_ref, dk_ref, dv_ref, dsa_ref, dkc_ref, dvc_ref = rest[:6]
        tmp = rest[6:] + (None,) * 11
        n, r = pl.program_id(1), pl.program_id(2)

        def carry_slot(j, q):
            return ((r + (dil // sub) * q) * pp + j) if sub > 1 else r * pp + j

        @pl.when(jnp.logical_and(n == 0, r == 0))
        def _():
            dsa_ref[...] = jnp.zeros_like(dsa_ref)

        @pl.when(n == 0)
        def _():
            for j in range(pp):
                for q in range(sub):
                    dkc_ref[carry_slot(j, q)] = jnp.zeros((_STEPS, _LANES), F32)
                    dvc_ref[carry_slot(j, q)] = jnp.zeros((_STEPS, _LANES), F32)

        @pl.when(n < nb)
        def _():
            first = n == 0
            masks = _head_masks()
            for j in range(pp):
                qs = _strided_rows(q_ref, tmp[0], j, r, dil)
                kps, kcs = _strided_rows(kp_ref, tmp[1], j, r, dil), _strided_rows(kc_ref, tmp[2], j, r, dil)
                vps, vcs = _strided_rows(vp_ref, tmp[3], j, r, dil), _strided_rows(vc_ref, tmp[4], j, r, dil)
                dys_ = _strided_rows(dy_ref, tmp[5], j, r, dil)
                lses = _strided_rows(l_ref, tmp[6], j, r, dil)
                dls = _strided_rows(dl_ref, tmp[7], j, r, dil)
                ds_sum = [jnp.zeros((_STEPS, 2 * _STEPS), F32)] * 2
                dq_res, dk_res, dv_res = [], [], []
                for q in range(sub):
                    q2 = qs[q].astype(BF16)
                    k2 = jnp.concatenate([kps[q], kcs[q]], axis=0).astype(BF16)
                    v2 = jnp.concatenate([vps[q], vcs[q]], axis=0).astype(BF16)
                    dy2 = dys_[q].astype(BF16)
                    dq_p = jnp.zeros((_STEPS, _LANES), F32)
                    qms, dyms, dsbs, pbs = [], [], [], []
                    for hh in range(2):
                        mk = masks[hh]
                        lane0 = hh * _HEAD_DIM
                        qm, km, dym = jnp.where(mk, q2, 0), jnp.where(mk, k2, 0), jnp.where(mk, dy2, 0)
                        sc = _scores(qm, k2, b_ref[2 * j + hh], first)
                        p = jnp.exp(sc - lses[q][:, lane0:lane0 + 1])
                        dp = lax.dot_general(dym, v2, (((1,), (1,)), ((), ())), preferred_element_type=F32)
                        ds = p * (dp - dls[q][:, lane0:lane0 + 1])
                        ds_sum[hh] = ds_sum[hh] + ds
                        dsb = ds.astype(BF16)
                        dq_p = dq_p + jnp.dot(dsb, km, preferred_element_type=F32)
                        qms.append(qm)
                        dyms.append(dym)
                        dsbs.append(dsb)
                        pbs.append(p.astype(BF16))
                    tdn = (((0,), (0,)), ((), ()))
                    dk_p = lax.dot_general(jnp.concatenate(dsbs, axis=0), jnp.concatenate(qms, axis=0), tdn,
                                           preferred_element_type=F32) * scale
                    dv_p = lax.dot_general(jnp.concatenate(pbs, axis=0), jnp.concatenate(dyms, axis=0), tdn,
                                           preferred_element_type=F32)
                    slot = carry_slot(j, q)
                    dq_res.append(dq_p * scale)
                    dk_res.append(dkc_ref[slot] + dk_p[:_STEPS])
                    dv_res.append(dvc_ref[slot] + dv_p[:_STEPS])
                    dkc_ref[slot] = dk_p[_STEPS:]
                    dvc_ref[slot] = dv_p[_STEPS:]
                for hh in range(2):
                    dsa_ref[2 * j + hh] += ds_sum[hh]
                _store_strided(dq_ref, tmp[8], j, r, dil, dq_res)
                _store_strided(dk_ref, tmp[9], j, r, dil, dk_res)
                _store_strided(dv_ref, tmp[10], j, r, dil, dv_res)

        @pl.when(n == nb)
        def _():
            for j in range(pp):
                _store_strided(dk_ref, tmp[9], j, r, dil, [dkc_ref[carry_slot(j, q)] for q in range(sub)])
                _store_strided(dv_ref, tmp[10], j, r, dil, [dvc_ref[carry_slot(j, q)] for q in range(sub)])

    def clamp(n):
        return jnp.minimum(n, nb - 1)

    cur = pl.BlockSpec((pp, rb, _LANES), lambda hb, n, r: (g * npb + hb, clamp(n), 0))
    prev = pl.BlockSpec((pp, rb, _LANES), lambda hb, n, r: (g * npb + hb, jnp.maximum(clamp(n) - 1, 0), 0))
    stat = pl.BlockSpec((pp, rb, _LANES), lambda hb, n, r: (hb, clamp(n), 0))
    bspec = pl.BlockSpec((2 * pp, _STEPS, 2 * _STEPS), lambda hb, n, r: (g * npb + hb, 0, 0))
    dkspec = pl.BlockSpec((pp, rb, _LANES), lambda hb, n, r: (g * npb + hb, jnp.maximum(n - 1, 0), 0))
    dsspec = pl.BlockSpec((2 * pp, _STEPS, 2 * _STEPS), lambda hb, n, r: (hb, 0, 0))
    wide = jax.ShapeDtypeStruct((3 * _PAIRS, s, _LANES), F32)
    np_ = len(prev_grads)
    return _pcall(
        body, name=f"attn_bwd_g{g}", grid=(npb, nb + 1, dil // sub),
        in_specs=[cur, cur, prev, cur, prev, stat, stat, stat, bspec] + [_ANY] * np_,
        out_specs=[cur, dkspec, dkspec, dsspec],
        out_shape=[wide, wide, wide, jax.ShapeDtypeStruct((8, _STEPS, 2 * _STEPS), F32)],
        scratch=[pltpu.VMEM((dil * pp, _STEPS, _LANES), F32), pltpu.VMEM((dil * pp, _STEPS, _LANES), F32)]
        + _tmp_rows(dil, 11),
        aliases={9 + t: t for t in range(np_)}, rider=rider,
    )(uq, uk, uk, uv, uv, dys, lse, delta, bias, *prev_grads)


def _place():
    x, y, c = lax.axis_index("x"), lax.axis_index("y"), lax.axis_index("c")
    chips = [(1 - x, y), (x, 1 - y), (1 - x, 1 - y)]
    return x, y, c, chips


def _slab(ref, axis, chip, width):
    start = pl.multiple_of(chip * width, width)
    if axis == 0:
        return ref.at[pl.ds(start, width), :]
    return ref.at[:, pl.ds(start, width)]


def _gather_rider(shards, axes):
    nw = len(shards)
    fulls = []
    for sh, ax in zip(shards, axes):
        shape = list(sh.shape)
        shape[ax] *= 4
        fulls.append(jax.ShapeDtypeStruct(tuple(shape), sh.dtype))

    def copies(ins, outs, scr):
        send, recv, loc = scr
        x, y, c, chips = _place()
        mine = 2 * x + y
        own, sends, arrivals = [], [], []
        for t in range(nw):
            width = ins[t].shape[axes[t]]
            own.append(pltpu.make_async_copy(ins[t], _slab(outs[t], axes[t], mine, width), loc.at[t]))
            for j, (px, py) in enumerate(chips):
                sems = dict(send_sem=send.at[3 * t + j], recv_sem=recv.at[3 * t + j],
                            device_id=(px, py, c), device_id_type=MESH)
                sends.append(pltpu.make_async_remote_copy(
                    src_ref=ins[t], dst_ref=_slab(outs[t], axes[t], mine, width), **sems))
                arrivals.append(pltpu.make_async_remote_copy(
                    src_ref=ins[t], dst_ref=_slab(outs[t], axes[t], 2 * px + py, width), **sems))
        return own, sends, arrivals

    def start(ins, outs, scr):
        own, sends, _ = copies(ins, outs, scr)
        for cp in own + sends:
            cp.start()

    def finish(ins, outs, scr):
        own, sends, arrivals = copies(ins, outs, scr)
        for cp in arrivals:
            cp.wait_recv()
        for cp in own:
            cp.wait()
        for cp in sends:
            cp.wait_send()

    return _Rider(shards, fulls, [pltpu.SemaphoreType.DMA((3 * nw,)), pltpu.SemaphoreType.DMA((3 * nw,)),
                                  pltpu.SemaphoreType.DMA((nw,))], start, finish)


def _run_rider(rider, name):
    nin, nout = len(rider.ins), len(rider.out_shapes)

    def body(*refs):
        ins, outs, scr = refs[:nin], refs[nin:nin + nout], refs[nin + nout:]
        rider.start(ins, outs, scr)
        rider.finish(ins, outs, scr)

    return _pcall(body, name=name, in_specs=[_ANY] * nin, out_specs=[_ANY] * nout, out_shape=rider.out_shapes,
                  scratch=rider.scratch)(*rider.ins)


def _gather_halves_rider(shard, axis):
    shape = list(shard.shape)
    shape[axis] *= 4
    full = jax.ShapeDtypeStruct(tuple(shape), shard.dtype)
    half = shard.shape[0] // 2
    width = shard.shape[axis]

    def region(out, chip, core):
        if axis == 0:
            return out.at[pl.ds(pl.multiple_of(chip * width + core * half, half), half), :]
        return out.at[pl.ds(pl.multiple_of(core * half, half), half), pl.ds(pl.multiple_of(chip * width, width), width)]

    def copies(ins, outs, scr):
        send, recv, loc = scr
        (src,), (out,) = ins, outs
        x, y, c, chips = _place()
        mine = 2 * x + y
        own = pltpu.make_async_copy(src, _slab(out, axis, mine, width), loc.at[0])
        my_half = src.at[pl.ds(pl.multiple_of(c * half, half), half), :]
        over_ici, ici_in, to_sib, sib_in = [], [], [], []
        for j, (px, py) in enumerate(chips):
            theirs = 2 * px + py
            ici = dict(send_sem=send.at[j], recv_sem=recv.at[j], device_id=(px, py, c), device_id_type=MESH)
            d2d = dict(send_sem=send.at[3 + j], recv_sem=recv.at[3 + j], device_id=(x, y, 1 - c),
                       device_id_type=MESH)
            over_ici.append(pltpu.make_async_remote_copy(src_ref=my_half, dst_ref=region(out, mine, c), **ici))
            ici_in.append(pltpu.make_async_remote_copy(src_ref=my_half, dst_ref=region(out, theirs, c), **ici))
            to_sib.append(pltpu.make_async_remote_copy(
                src_ref=region(out, theirs, c), dst_ref=region(out, theirs, c), **d2d))
            sib_in.append(pltpu.make_async_remote_copy(
                src_ref=region(out, theirs, c), dst_ref=region(out, theirs, 1 - c), **d2d))
        return own, over_ici, ici_in, to_sib, sib_in

    def start(ins, outs, scr):
        own, over_ici, _, _, _ = copies(ins, outs, scr)
        own.start()
        for cp in over_ici:
            cp.start()

    def finish(ins, outs, scr):
        own, over_ici, ici_in, to_sib, sib_in = copies(ins, outs, scr)
        for j in range(3):
            ici_in[j].wait_recv()
            to_sib[j].start()
        for cp in sib_in:
            cp.wait_recv()
        own.wait()
        for cp in over_ici + to_sib:
            cp.wait_send()

    return _Rider([shard], [full], [pltpu.SemaphoreType.DMA((6,)), pltpu.SemaphoreType.DMA((6,)),
                                    pltpu.SemaphoreType.DMA((1,))], start, finish)


def _scatter_rider(grads, axes, rows=None):
    nw = len(grads)
    outs_shape = []
    for gr, ax in zip(grads, axes):
        shape = list(gr.shape)
        shape[ax] //= 4
        if rows is not None:
            assert ax == 1
            shape[0] = rows[1] - rows[0]
        outs_shape.append(jax.ShapeDtypeStruct((3,) + tuple(shape), gr.dtype))

    def copies(ins, outs, scr):
        send, recv = scr
        x, y, c, chips = _place()
        cps = []
        for t in range(nw):
            width = ins[t].shape[axes[t]] // 4
            src = ins[t] if rows is None else ins[t].at[pl.ds(rows[0], rows[1] - rows[0]), :]
            for j, (px, py) in enumerate(chips):
                cps.append(pltpu.make_async_remote_copy(
                    src_ref=_slab(src, axes[t], 2 * px + py, width), dst_ref=outs[t].at[j],
                    send_sem=send.at[3 * t + j], recv_sem=recv.at[3 * t + j],
                    device_id=(px, py, c), device_id_type=MESH))
        return cps

    def start(ins, outs, scr):
        for cp in copies(ins, outs, scr):
            cp.start()

    def finish(ins, outs, scr):
        cps = copies(ins, outs, scr)
        for cp in cps:
            cp.wait_recv()
        for cp in cps:
            cp.wait_send()

    return _Rider(grads, outs_shape, [pltpu.SemaphoreType.DMA((3 * nw,)), pltpu.SemaphoreType.DMA((3 * nw,))],
                  start, finish)


def _swap_rider(parts):
    nw = len(parts)

    def copies(ins, outs, scr):
        send, recv = scr
        x, y, c, _ = _place()
        return [pltpu.make_async_remote_copy(
            src_ref=ins[t], dst_ref=outs[t], send_sem=send.at[t], recv_sem=recv.at[t],
            device_id=(x, y, 1 - c), device_id_type=MESH) for t in range(nw)]

    def start(ins, outs, scr):
        for cp in copies(ins, outs, scr):
            cp.start()

    def finish(ins, outs, scr):
        cps = copies(ins, outs, scr)
        for cp in cps:
            cp.wait_recv()
        for cp in cps:
            cp.wait_send()

    return _Rider(parts, [jax.ShapeDtypeStruct(p.shape, p.dtype) for p in parts],
                  [pltpu.SemaphoreType.DMA((nw,)), pltpu.SemaphoreType.DMA((nw,))], start, finish)


def _sum_all_devices(buf, name):
    rows, cols = buf.shape

    def body(in_ref, o_ref, gat_ref, send, recv):
        x, y, c, _ = _place()
        me = 4 * x + 2 * y + c
        gat_ref[me] = in_ref[...]
        started = []
        for mask in range(1, 8):
            fx, fy, fc = (mask >> 2) & 1, (mask >> 1) & 1, mask & 1
            peer = (x + fx * (1 - 2 * x), y + fy * (1 - 2 * y), c + fc * (1 - 2 * c))
            cp = pltpu.make_async_remote_copy(
                src_ref=in_ref, dst_ref=gat_ref.at[me], send_sem=send.at[mask - 1], recv_sem=recv.at[mask - 1],
                device_id=peer, device_id_type=MESH)
            cp.start()
            started.append(cp)
        for cp in started:
            cp.wait_recv()
        for cp in started:
            cp.wait_send()
        acc = gat_ref[0]
        for t in range(1, 8):
            acc = acc + gat_ref[t]
        o_ref[...] = acc

    vm = pl.BlockSpec(memory_space=pltpu.VMEM)
    return _pcall(
        body, name=name, in_specs=[vm], out_specs=vm, out_shape=jax.ShapeDtypeStruct((rows, cols), F32),
        scratch=[pltpu.VMEM((8, rows, cols), F32), pltpu.SemaphoreType.DMA((7,)), pltpu.SemaphoreType.DMA((7,))],
    )(buf)


_UPD_T = 256


def _sum_partials(own, got, name):
    rows, cols = own.shape
    tr = min(_UPD_T, rows)

    def body(own_ref, got_ref, o_ref):
        acc = own_ref[...].astype(F32)
        for j in range(3):
            acc = acc + got_ref[j].astype(F32)
        o_ref[...] = acc

    blk = pl.BlockSpec((tr, cols), lambda i: (i, 0))
    return _pcall(
        body, name=name, grid=(rows // tr,),
        in_specs=[blk, pl.BlockSpec((3, tr, cols), lambda i: (0, i, 0))], out_specs=blk,
        out_shape=jax.ShapeDtypeStruct((rows, cols), F32))(own, got)


def _adamw_math(w, gr, m, v):
    m = _B1 * m + (1.0 - _B1) * gr
    v = _B2 * v + (1.0 - _B2) * (gr * gr)
    m_hat = m / (1.0 - _B1 ** _STEP)
    v_hat = v / (1.0 - _B2 ** _STEP)
    delta = -_LR * (m_hat / (jnp.sqrt(v_hat) + _EPS) + _WD * w)
    return delta, m, v


def _adamw(w, m, v, parts, name):
    rows, cols = w.shape
    tr = min(_UPD_T, rows)
    npart = len(parts)

    def body(w_ref, m_ref, v_ref, *rest):
        p_refs, (g_ref, d_ref, nm_ref, nv_ref) = rest[:npart], rest[npart:]
        gr = p_refs[0][...]
        for p in p_refs[1:]:
            gr = gr + p[...]
        delta, nm, nv = _adamw_math(w_ref[...], gr, m_ref[...], v_ref[...])
        g_ref[...] = gr
        d_ref[...] = delta
        nm_ref[...] = nm
        nv_ref[...] = nv

    blk = pl.BlockSpec((tr, cols), lambda i: (i, 0))
    sh = jax.ShapeDtypeStruct((rows, cols), F32)
    return _pcall(body, name=name, grid=(rows // tr,), in_specs=[blk] * (3 + npart), out_specs=[blk] * 4,
                  out_shape=[sh] * 4)(w, m, v, *parts)


def _adamw_layers(w, m, v, parts, name):
    _, rows, cols = w.shape
    tr = min(_UPD_T, rows)
    npart = len(parts[0])

    def body(w_ref, m_ref, v_ref, *rest):
        p_refs, (g_ref, d_ref, nm_ref, nv_ref) = rest[:2 * npart], rest[2 * npart:]
        grs = []
        for layer in range(2):
            gr = p_refs[layer * npart][...]
            for p in p_refs[layer * npart + 1:(layer + 1) * npart]:
                gr = gr + p[...]
            grs.append(gr)
        gr = jnp.where(pl.program_id(0) == 0, grs[0], grs[1])
        delta, nm, nv = _adamw_math(w_ref[...], gr, m_ref[...], v_ref[...])
        g_ref[...] = gr
        d_ref[...] = delta
        nm_ref[...] = nm
        nv_ref[...] = nv

    blk = pl.BlockSpec((None, tr, cols), lambda l, i: (l, i, 0))

    def part_spec(layer):
        return pl.BlockSpec((tr, cols), lambda l, i: (jnp.where(l == layer, i, 0), 0))

    sh = jax.ShapeDtypeStruct(w.shape, F32)
    return _pcall(
        body, name=name, grid=(2, rows // tr),
        in_specs=[blk] * 3 + [part_spec(0)] * npart + [part_spec(1)] * npart, out_specs=[blk] * 4,
        out_shape=[sh] * 4)(w, m, v, *parts[0], *parts[1])


_PACK_W = 1024


def _pack(arrs, rows):
    flat = []
    for a in arrs:
        f = a.reshape(-1).astype(F32)
        pad = (-f.shape[0]) % _PACK_W
        flat.append(jnp.pad(f, (0, pad)))
    f = jnp.concatenate(flat)
    f = jnp.pad(f, (0, rows * _PACK_W - f.shape[0]))
    return f.reshape(rows, _PACK_W)


def _unpack(buf, shapes):
    flat = buf.reshape(-1)
    out, pos = [], 0
    for sh in shapes:
        size = math.prod(sh)
        out.append(flat[pos:pos + size].reshape(sh))
        pos += size + ((-size) % _PACK_W)
    return out


def _pack_rows(shapes):
    total = sum(-(-math.prod(sh) // _PACK_W) for sh in shapes)
    return -(-total // 8) * 8


def kernel(x, rel_bias, ab_norm, ab_w_in, ab_conv_w, ab_conv_b, ab_ln_g, ab_ln_b, ab_w_out, sc_norm, sc_w_in, sc_conv_w, sc_w_out, mlp_norm, mlp_w_up, mlp_w_down, final_norm, loss_target, m_rel_bias, m_ab_norm, m_ab_w_in, m_ab_conv_w, m_ab_conv_b, m_ab_ln_g, m_ab_ln_b, m_ab_w_out, m_sc_norm, m_sc_w_in, m_sc_conv_w, m_sc_w_out, m_mlp_norm, m_mlp_w_up, m_mlp_w_down, m_final_norm, v_rel_bias, v_ab_norm, v_ab_w_in, v_ab_conv_w, v_ab_conv_b, v_ab_ln_g, v_ab_ln_b, v_ab_w_out, v_sc_norm, v_sc_w_in, v_sc_conv_w, v_sc_w_out, v_mlp_norm, v_mlp_w_up, v_mlp_w_down, v_final_norm):
    s, d = x.shape[1], x.shape[2]
    dff = 4 * d
    c = _GROUP_COLS
    chip = 2 * lax.axis_index("x") + lax.axis_index("y")
    on_c0 = (lax.axis_index("c") == 0).astype(F32)
    h0 = x[0]
    tgt = loss_target[0]

    cw_sh, scn_sh, scw_sh = ab_conv_w[0], sc_norm, sc_conv_w[0]
    conv_w_full = lax.dynamic_update_slice(jnp.zeros((_CONV_K, c), F32), cw_sh * on_c0, (0, chip * cw_sh.shape[1]))
    scn_full = lax.dynamic_update_slice(jnp.zeros((1, d), F32), scn_sh * on_c0, (0, chip * scn_sh.shape[1]))
    scw_full = lax.dynamic_update_slice(jnp.zeros((3, d), F32), scw_sh * on_c0, (0, chip * scw_sh.shape[1]))
    small_shapes = [(_CONV_K, c), (1, d), (3, d)]
    small = _sum_all_devices(_pack([conv_w_full, scn_full, scw_full], _pack_rows(small_shapes)), "gather_small")
    conv_w, sc_g, sc_cw = _unpack(small, small_shapes)

    w_shards = [ab_w_in[0], ab_w_out[0], sc_w_in[0], sc_w_out[0], mlp_w_up[0], mlp_w_up[1],
                mlp_w_down[0], mlp_w_down[1]]
    w_axes = [1, 0, 1, 0, 1, 1, 0, 0]
    wb = [w.astype(BF16) for w in w_shards]
    full_w = [None] * 8

    def gather(idx):
        return _gather_rider([wb[t] for t in idx], [w_axes[t] for t in idx])

    def put(idx, got_w):
        for t, w in zip(idx, got_w):
            full_w[t] = w

    buckets = _bucket_maps()
    bias = _bias_expand(rel_bias, buckets)
    n0, got_w = _rms_fwd(h0, ab_norm, "rms_fwd_ab", rider=_gather_halves_rider(wb[0], w_axes[0]))
    put([0], got_w)
    w_in = full_w[0]
    tm = min(1024, s)
    tm2 = min(2048, s)
    tmh = min(512, s)
    uc = _mm(n0, w_in, "nn", m=s, n=2 * c, k=d, tm=tm2, tn=2 * c, tk=d, out_dtype=BF16, name="proj_conv")
    uq, uk, uv = [], [], []
    for t, (nm, dst) in enumerate(zip("qkv", (uq, uk, uv))):
        res = _mm(n0, w_in, "nn", m=s, n=3 * c, k=d, tm=tm2, tn=c, tk=d, out_dtype=F32, name=f"proj_{nm}",
                  b_off=(0, 2 + 3 * t), split="o", rider=gather([1]) if t == 0 else None)
        if t == 0:
            res, got_w = res
            put([1], got_w)
        dst.append(res)
    uq, uk, uv = uq[0], uk[0], uv[0]
    (cat, ca), got_w = _conv_a_fwd(uc, conv_w, ab_conv_b, ab_ln_g, ab_ln_b, rider=gather([2]))
    put([2], got_w)
    outs, lses = [], []
    for g, (_, dil) in enumerate(_GROUPS):
        idx = ([4], [6], [3, 5])[g]
        (o, l), got_w = _attn_fwd(uq, uk, uv, bias, g, dil, 4 if dil <= 4 else 2, rider=gather(idx))
        put(idx, got_w)
        outs.append(o)
        lses.append(l)
    cat, lse = _attn_merge(outs, lses, cat)
    h1, n1 = _mm(cat, full_w[1], "nn", m=s, n=d, k=d, tm=tm, tn=d, tk=d, out_dtype=(F32, BF16), name="out_ab",
                 epi=_epi_add_rms, extras=(h0,), vecs=(mlp_norm[0:1],))

    def mlp_fwd(h, nrm, layer, next_gain=None, rider=None):
        zr = _mm(nrm, full_w[4 + layer], "nn", m=s, n=dff, k=d, tm=tmh, tn=dff, tk=d, out_dtype=BF16,
                 name=f"mlp_up{layer}", epi=_epi_relu, rider=rider)
        if rider is not None:
            zr, got_r = zr
            put([7], got_r)
        kw = dict(m=s, n=d, k=dff, tm=tmh, tn=d, tk=dff, name=f"mlp_down{layer}", a_pro=_square, extras=(h,))
        if next_gain is None:
            return zr, _mm(zr, full_w[6 + layer], "nn", out_dtype=F32, epi=_epi_add, **kw), None
        hn, nn = _mm(zr, full_w[6 + layer], "nn", out_dtype=(F32, BF16), epi=_epi_add_rms, vecs=(next_gain,), **kw)
        return zr, hn, nn

    zr0, h2, n2 = mlp_fwd(h1, n1, 0, next_gain=sc_g, rider=gather([7]))
    _, w_out, w_si, w_so, w_up0, w_up1, w_dn0, w_dn1 = full_w
    w_up, w_dn = [w_up0, w_up1], [w_dn0, w_dn1]
    u2 = _mm(n2, w_si, "nn", m=s, n=3 * d, k=d, tm=tmh, tn=3 * d, tk=d, out_dtype=BF16, name="proj_sc")
    scv = _short_conv_fwd(u2, sc_cw)
    h3, n3 = _mm(scv, w_so, "nn", m=s, n=d, k=d, tm=tm, tn=d, tk=d, out_dtype=(F32, BF16), name="out_sc",
                 epi=_epi_add_rms, extras=(h2,), vecs=(mlp_norm[1:2],))
    zr1, h4, _ = mlp_fwd(h3, n3, 1)

    dh4, dh4b, g_final, loss_part = _loss_head(h4, tgt, final_norm.reshape(1, d))
    tkw = min(2048, s)

    big_grads, got, sums = [None] * 8, [None] * 8, [None] * 8

    def scatter(t):
        return _scatter_rider([big_grads[t]], [w_axes[t]])

    def own_slab(t):
        width = big_grads[t].shape[w_axes[t]] // 4
        return lax.dynamic_slice_in_dim(big_grads[t], chip * width, width, axis=w_axes[t])

    def arrived(t, got_t):
        got[t] = got_t[0]
        sums[t] = _sum_partials(own_slab(t), got[t], f"sum_partials{t}")

    def mlp_bwd(dh, dhb, h, nrm, zr, layer):
        dz = _mm(dhb, w_dn[layer], "nt", m=s, n=dff, k=d, tm=tmh, tn=dff, tk=d, out_dtype=BF16,
                 name=f"mlp_down{layer}_dx", epi=_epi_relu_sq_bwd, extras=(zr,))
        big_grads[6 + layer] = _mm(zr, dhb, "tn", m=dff, n=d, k=s, tm=1024, tn=d, tk=tkw, out_dtype=BF16,
                                   name=f"mlp_down{layer}_dw", a_pro=_square)
        big_grads[4 + layer] = _mm(nrm, dz, "tn", m=d, n=dff, k=s, tm=d, tn=1024, tk=tkw, out_dtype=BF16,
                                   name=f"mlp_up{layer}_dw")
        res, got_t = _mm(dz, w_up[layer], "nt", m=s, n=d, k=dff, tm=tmh, tn=d, tk=dff, out_dtype=(F32, BF16),
                         name=f"mlp_up{layer}_dx", rider=scatter(6 + layer), epi=_epi_rms_bwd,
                         extras=(h, dh), vecs=(mlp_norm[layer:layer + 1],), row_sum=True)
        arrived(6 + layer, got_t)
        return res

    dh3, dh3b, g_mn1 = mlp_bwd(dh4, dh4b, h3, n3, zr1, 1)

    dsc = _mm(dh3b, w_so, "nt", m=s, n=d, k=d, tm=tm, tn=d, tk=d, out_dtype=F32, name="out_sc_dx")
    big_grads[3] = _mm(scv, dh3b, "tn", m=d, n=d, k=s, tm=d, tn=d, tk=tkw, out_dtype=BF16, name="out_sc_dw")
    (du2, g_sccw8), got_t = _short_conv_bwd(u2, dsc, sc_cw, rider=scatter(5))
    arrived(5, got_t)
    big_grads[2], got_t = _mm(n2, du2, "tn", m=d, n=3 * d, k=s, tm=d, tn=1024, tk=tkw, out_dtype=BF16,
                              name="proj_sc_dw", rider=scatter(3))
    arrived(3, got_t)
    (dh2, dh2b, g_scn), got_t = _mm(
        du2, w_si, "nt", m=s, n=d, k=3 * d, tm=tmh, tn=d, tk=3 * d, out_dtype=(F32, BF16), name="proj_sc_dx",
        rider=scatter(2), epi=_epi_rms_bwd, extras=(h2, dh3), vecs=(sc_g,), row_sum=True)
    arrived(2, got_t)

    dh1, dh1b, g_mn0 = mlp_bwd(dh2, dh2b, h1, n1, zr0, 0)

    dcat = _mm(dh1b, w_out, "nt", m=s, n=d, k=d, tm=tm, tn=d, tk=d, out_dtype=F32, name="out_ab_dx")
    big_grads[1] = _mm(cat, dh1b, "tn", m=d, n=d, k=s, tm=d, tn=d, tk=tkw, out_dtype=BF16, name="out_ab_dw")
    (dca, conv_stats), got_t = _conv_a_bwd_ln(ca, dcat, ab_ln_g, ab_ln_b, rider=scatter(1))
    arrived(1, got_t)
    (duc, g_cw32), got_t = _conv_a_bwd_conv(uc, dca, conv_w, rider=scatter(4))
    arrived(4, got_t)
    delta, dys = _attn_delta(dcat, cat)

    grads_qkv, ds_list = [], []
    for g, (_, dil) in enumerate(_GROUPS):
        dq, dk, dv, dsa = _attn_bwd(uq, uk, uv, dys, lse, delta, bias, grads_qkv, g, dil, 4 if dil <= 4 else 1)
        grads_qkv = [dq, dk, dv]
        ds_list.append(dsa)
    g_bias = _bias_reduce(jnp.concatenate(ds_list, axis=0), buckets)[:, :, 0].T

    secs = [(duc, 2 * c, 0)] + [(grads_qkv[t], 3 * c, 2 + 3 * t) for t in range(3)]
    g_in_parts = []
    for t, (du, width, off) in enumerate(secs):
        rider = _swap_rider(sums[1:]) if t == 1 else None
        part = _mm(n0, du, "tn", m=d, n=width, k=s, tm=d, tn=c, tk=tkw, out_dtype=BF16, name=f"proj_ab_dw{t}",
                   rider=rider, split="b" if t else "")
        if rider is not None:
            part, sib_late = part
        g_in_parts.append(part)
    big_grads[0] = jnp.concatenate(g_in_parts, axis=1)
    dn0 = _mm(duc, w_in, "nt", m=s, n=d, k=2 * c, tm=tm, tn=d, tk=2 * c, out_dtype=F32, name="proj_ab_dx_conv")
    dn0, (got[0],) = _dx_qkv(*grads_qkv, w_in, dn0, tm=tm, tk=c, col0=2 * c, name="proj_ab_dx_qkv",
                             rider=scatter(0))
    grad_x, _, g_abn = _rms_bwd(dn0, h0, ab_norm, dh1, "rms_bwd_ab")
    sums[0] = _sum_partials(own_slab(0), got[0], "sum_partials0")
    sib = list(_run_rider(_swap_rider([sums[0]]), "swap_sibling_w_in")) + sib_late

    upd = [_adamw(w_shards[t], mm[0], vv[0], [sums[t], sib[t]], f"adamw{t}")
           for t, (mm, vv) in enumerate(((m_ab_w_in, v_ab_w_in), (m_ab_w_out, v_ab_w_out),
                                         (m_sc_w_in, v_sc_w_in), (m_sc_w_out, v_sc_w_out)))]
    upd_up = _adamw_layers(mlp_w_up, m_mlp_w_up, v_mlp_w_up, [[sums[4], sib[4]], [sums[5], sib[5]]], "adamw_up")
    upd_dn = _adamw_layers(mlp_w_down, m_mlp_w_down, v_mlp_w_down, [[sums[6], sib[6]], [sums[7], sib[7]]],
                           "adamw_down")

    full_shapes = [(_NUM_BUCKETS, rel_bias.shape[1]), (1, d), (_CONV_K, c), (1, c), (1, c), (1, c), (1, d),
                   (3, d), (2, d), (d,), (1, 1)]
    small_grads = [g_bias, g_abn, g_cw32[:_CONV_K], conv_stats[0:1], conv_stats[1:2], conv_stats[2:3], g_scn,
                   g_sccw8[:3], jnp.concatenate([g_mn0, g_mn1], axis=0), g_final.reshape(d), loss_part[0:1, 0:1]]
    tot = _unpack(_sum_all_devices(_pack(small_grads, _pack_rows(full_shapes)), "sum_small"), full_shapes)
    loss = tot.pop()[0, 0]
    for idx, sh in ((2, cw_sh), (6, scn_sh), (7, scw_sh)):
        width = sh.shape[1]
        tot[idx] = lax.dynamic_slice_in_dim(tot[idx], chip * width, width, axis=1)
    sm_w = [rel_bias, ab_norm, cw_sh, ab_conv_b, ab_ln_g, ab_ln_b, scn_sh, scw_sh, mlp_norm, final_norm]
    sm_m = [m_rel_bias, m_ab_norm, m_ab_conv_w[0], m_ab_conv_b, m_ab_ln_g, m_ab_ln_b, m_sc_norm, m_sc_conv_w[0],
            m_mlp_norm, m_final_norm]
    sm_v = [v_rel_bias, v_ab_norm, v_ab_conv_w[0], v_ab_conv_b, v_ab_ln_g, v_ab_ln_b, v_sc_norm, v_sc_conv_w[0],
            v_mlp_norm, v_final_norm]
    sh_shapes = [tuple(t.shape) for t in tot]
    rows = _pack_rows(sh_shapes)
    sm_upd = _adamw(_pack(sm_w, rows), _pack(sm_m, rows), _pack(sm_v, rows), [_pack(tot, rows)], "adamw_small")
    sm_g, sm_d, sm_nm, sm_nv = [_unpack(buf, sh_shapes) for buf in sm_upd]

    def assemble(kind, sm):
        big = [u[kind] for u in upd]
        return [sm[0], sm[1], big[0][None], sm[2][None], sm[3], sm[4], sm[5], big[1][None], sm[6], big[2][None],
                sm[7][None], big[3][None], sm[8], upd_up[kind], upd_dn[kind], sm[9]]

    res = [loss, grad_x[None]]
    for kind, sm in enumerate((sm_g, sm_d, sm_nm, sm_nv)):
        res += assemble(kind, sm)
    return tuple(res)
```

```python
import functools
import math

import jax
import jax.numpy as jnp
from jax import lax
from jax.experimental import pallas as pl
from jax.experimental.pallas import tpu as pltpu

F32 = jnp.float32
BF16 = jnp.bfloat16
MESH = pl.DeviceIdType.MESH

_GROUPS = ((128, 1), (512, 4), (2048, 16))
_STEPS = 128
_HEAD_DIM = 64
_GROUP_COLS = 512
_NUM_BUCKETS = 32
_MAX_DISTANCE = 2048
_CONV_K = 31
_HALO = 32
_SC_HALO = 16
_RMS_EPS = 1e-6
_LN_EPS = 1e-5
_NEG = -1e30
_LANES = 128
_VMEM_LIMIT = 56 * 1024 * 1024

_LR, _B1, _B2, _EPS, _WD, _STEP = 0.001, 0.9, 0.999, 1e-08, 0.01, 10


class _Rider:
    def __init__(self, ins, out_shapes, scratch, start, finish):
        self.ins, self.out_shapes, self.scratch = list(ins), list(out_shapes), list(scratch)
        self.start, self.finish = start, finish


def _pcall(body, *, name, out_shape, in_specs, out_specs, grid=None, scratch=(), aliases=None, rider=None):
    kw = {} if grid is None else {"grid": grid}
    cparams = pltpu.CompilerParams(vmem_limit_bytes=_VMEM_LIMIT)
    if rider is None:
        return pl.pallas_call(
            body, name=name, out_shape=out_shape, in_specs=in_specs, out_specs=out_specs,
            scratch_shapes=list(scratch), input_output_aliases=aliases or {},
            compiler_params=cparams, **kw)
    single = not isinstance(out_specs, (list, tuple))
    ospecs = [out_specs] if single else list(out_specs)
    oshapes = [out_shape] if single else list(out_shape)
    nin, nout, nscr = len(in_specs), len(ospecs), len(scratch)
    rin, rout = len(rider.ins), len(rider.out_shapes)

    def wrapped(*refs):
        h_in, r_in = refs[:nin], refs[nin:nin + rin]
        p = nin + rin
        h_out, r_out = refs[p:p + nout], refs[p + nout:p + nout + rout]
        p += nout + rout
        h_scr, r_scr = refs[p:p + nscr], refs[p + nscr:]
        ids = [pl.program_id(a) for a in range(len(grid))]
        first = functools.reduce(jnp.logical_and, [i == 0 for i in ids])
        last = functools.reduce(jnp.logical_and, [i == g - 1 for i, g in zip(ids, grid)])

        @pl.when(first)
        def _():
            rider.start(r_in, r_out, r_scr)

        body(*h_in, *h_out, *h_scr)

        @pl.when(last)
        def _():
            rider.finish(r_in, r_out, r_scr)

    call = pl.pallas_call(
        wrapped, name=name, out_shape=oshapes + rider.out_shapes,
        in_specs=list(in_specs) + [_ANY] * rin, out_specs=ospecs + [_ANY] * rout,
        scratch_shapes=list(scratch) + rider.scratch, input_output_aliases=aliases or {},
        compiler_params=cparams, **kw)

    def run(*operands):
        res = call(*operands, *rider.ins)
        host = res[0] if single else list(res[:nout])
        return host, list(res[nout:])

    return run


def _sig(x):
    return 1.0 / (1.0 + jnp.exp(-x))


_ANY = pl.BlockSpec(memory_space=pl.ANY)


def _lanes_of(ref):
    parts = [ref[p] for p in range(ref.shape[0])]
    return parts[0] if len(parts) == 1 else jnp.concatenate(parts, axis=1)


def _mm(a, b, mode, *, m, n, k, tm, tn, tk, out_dtype, name, epi=None, extras=(), b_off=(0, 0), rider=None,
        split="", vecs=(), a_pro=None, row_sum=False):
    nk = k // tk
    assert m % tm == 0 and n % tn == 0 and k % tk == 0
    o0, o1 = b_off
    if mode == "nn":
        a_spec = pl.BlockSpec((tm, tk), lambda i, j, kk: (i, kk))
        b_spec = pl.BlockSpec((tk, tn), lambda i, j, kk: (kk + o0, j + o1))
        dn = (((1,), (0,)), ((), ()))
    elif mode == "nt":
        a_spec = pl.BlockSpec((tm, tk), lambda i, j, kk: (i, kk))
        if "a" in split:
            a_spec = pl.BlockSpec((tk // _LANES, tm, _LANES), lambda i, j, kk: (kk, i, 0))
        b_spec = pl.BlockSpec((tn, tk), lambda i, j, kk: (j + o0, kk + o1))
        dn = (((1,), (1,)), ((), ()))
    else:
        a_spec = pl.BlockSpec((tk, tm), lambda i, j, kk: (kk, i))
        b_spec = pl.BlockSpec((tk, tn), lambda i, j, kk: (kk + o0, j + o1))
        if "b" in split:
            b_spec = pl.BlockSpec((tn // _LANES, tk, _LANES), lambda i, j, kk: (j, kk, 0))
        dn = (((0,), (0,)), ((), ()))
    o_spec = pl.BlockSpec((tm, tn), lambda i, j, kk: (i, j))
    e_spec = o_spec
    if "o" in split:
        o_spec = pl.BlockSpec((tn // _LANES, tm, _LANES), lambda i, j, kk: (j, i, 0))
    v_spec = pl.BlockSpec((1, tn), lambda i, j, kk: (0, j))
    ne = len(extras) + len(vecs)
    multi = isinstance(out_dtype, tuple)
    dts = out_dtype if multi else (out_dtype,)
    no = len(dts)
    nr = 1 if row_sum else 0
    assert not row_sum or tn == n

    def body(a_ref, b_ref, *rest):
        ex, o_refs = rest[:ne], rest[ne:ne + no]
        av = _lanes_of(a_ref) if "a" in split else a_ref[...]
        bv = _lanes_of(b_ref) if "b" in split else b_ref[...]
        if av.dtype != BF16:
            av = av.astype(BF16)
        if bv.dtype != BF16:
            bv = bv.astype(BF16)
        if a_pro is not None:
            av = a_pro(av)
        p = lax.dot_general(av, bv, dn, preferred_element_type=F32)

        def fin(x):
            if epi is not None:
                x = epi(x, *[e[...] for e in ex])
            if row_sum:
                row, x = x[-1], (x[:-1] if multi else x[0])
                row_ref = rest[ne + no]

                @pl.when(pl.program_id(0) == 0)
                def _():
                    row_ref[...] = row

                @pl.when(pl.program_id(0) > 0)
                def _():
                    row_ref[...] += row

            for o_ref, val, dt in zip(o_refs, x if multi else (x,), dts):
                if "o" in split:
                    for p in range(tn // _LANES):
                        o_ref[p] = val[:, p * _LANES:(p + 1) * _LANES].astype(dt)
                else:
                    o_ref[...] = val.astype(dt)

        if nk == 1:
            fin(p)
        else:
            acc = rest[ne + no + nr]
            kk = pl.program_id(2)

            @pl.when(kk == 0)
            def _():
                acc[...] = p

            @pl.when(kk > 0)
            def _():
                acc[...] += p

            @pl.when(kk == nk - 1)
            def _():
                fin(acc[...])

    oshape = (n // _LANES, m, _LANES) if "o" in split else (m, n)
    shapes = [jax.ShapeDtypeStruct(oshape, dt) for dt in dts]
    ospecs = [o_spec] * no
    if row_sum:
        shapes.append(jax.ShapeDtypeStruct((1, n), F32))
        ospecs.append(v_spec)
    lone = not multi and not row_sum
    return _pcall(
        body, name=name, grid=(m // tm, n // tn, nk),
        in_specs=[a_spec, b_spec] + [e_spec] * len(extras) + [v_spec] * len(vecs),
        out_specs=ospecs[0] if lone else ospecs, out_shape=shapes[0] if lone else shapes,
        scratch=[pltpu.VMEM((tm, tn), F32)] if nk > 1 else [], rider=rider,
    )(a, b, *extras, *vecs)


def _dx_qkv(dq, dk, dv, w, prev, *, tm, tk, col0, name, rider=None):
    npair, s, _ = dq.shape
    n = w.shape[0]
    per = npair * _LANES // tk
    nk = 3 * per
    off = col0 // tk
    nt_dims = (((1,), (1,)), ((), ()))

    def a_spec(t):
        return pl.BlockSpec((tk // _LANES, tm, _LANES), lambda i, kk: (jnp.clip(kk - per * t, 0, per - 1), i, 0))

    def body(a0, a1, a2, b_ref, e_ref, o_ref, acc):
        kk = pl.program_id(1)
        bv = b_ref[...]

        @pl.when(kk == 0)
        def _():
            acc[...] = e_ref[...]

        for t, a_ref in enumerate((a0, a1, a2)):
            @pl.when(jnp.logical_and(kk >= per * t, kk < per * (t + 1)))
            def _():
                acc[...] += lax.dot_general(_lanes_of(a_ref).astype(BF16), bv, nt_dims, preferred_element_type=F32)

        @pl.when(kk == nk - 1)
        def _():
            o_ref[...] = acc[...]

    row = pl.BlockSpec((tm, n), lambda i, kk: (i, 0))
    return _pcall(
        body, name=name, grid=(s // tm, nk),
        in_specs=[a_spec(0), a_spec(1), a_spec(2), pl.BlockSpec((n, tk), lambda i, kk: (0, kk + off)), row],
        out_specs=row, out_shape=jax.ShapeDtypeStruct((s, n), F32),
        scratch=[pltpu.VMEM((tm, n), F32)], rider=rider)(dq, dk, dv, w, prev)


def _epi_add(x, r):
    return x + r


def _epi_relu(x):
    return jnp.maximum(x, 0.0)


def _square(x):
    return x * x


def _epi_relu_sq_bwd(da, zr):
    return da * (2.0 * zr.astype(F32))


def _epi_add_rms(x, r, g):
    h = x + r
    return h, h * lax.rsqrt(jnp.mean(h * h, axis=-1, keepdims=True) + _RMS_EPS) * g


def _epi_rms_bwd(dn, h, dh_in, g):
    dx, dg = _rms_bwd_math(dn, h, g)
    dh = dh_in + dx
    return dh, dh, dg


_ROW_T = 512


def _rms_fwd(h, g, name, rider=None):
    s, d = h.shape

    def body(h_ref, g_ref, o_ref):
        x = h_ref[...]
        r = lax.rsqrt(jnp.mean(x * x, axis=-1, keepdims=True) + _RMS_EPS)
        o_ref[...] = (x * r * g_ref[...]).astype(BF16)

    row = pl.BlockSpec((_ROW_T, d), lambda i: (i, 0))
    vec = pl.BlockSpec((1, d), lambda i: (0, 0))
    return _pcall(body, name=name, grid=(s // _ROW_T,), in_specs=[row, vec], out_specs=row,
                  out_shape=jax.ShapeDtypeStruct((s, d), BF16), rider=rider)(h, g)


def _rms_bwd_math(dn, x, g):
    r = lax.rsqrt(jnp.mean(x * x, axis=-1, keepdims=True) + _RMS_EPS)
    xhat = x * r
    dg = jnp.sum(dn * xhat, axis=0, keepdims=True)
    t = dn * g
    dx = r * (t - xhat * jnp.mean(t * xhat, axis=-1, keepdims=True))
    return dx, dg


def _rms_bwd(dn, h, g, dh_in, name):
    s, d = h.shape

    def body(dn_ref, h_ref, g_ref, dhi_ref, dh_ref, dhb_ref, dg_ref):
        dx, dg = _rms_bwd_math(dn_ref[...], h_ref[...], g_ref[...])
        dh = dhi_ref[...] + dx
        dh_ref[...] = dh
        dhb_ref[...] = dh.astype(BF16)

        @pl.when(pl.program_id(0) == 0)
        def _():
            dg_ref[...] = jnp.zeros_like(dg_ref)

        dg_ref[...] += dg

    row = pl.BlockSpec((_ROW_T, d), lambda i: (i, 0))
    vec = pl.BlockSpec((1, d), lambda i: (0, 0))
    return _pcall(
        body, name=name, grid=(s // _ROW_T,), in_specs=[row, row, vec, row], out_specs=[row, row, vec],
        out_shape=[jax.ShapeDtypeStruct((s, d), F32), jax.ShapeDtypeStruct((s, d), BF16),
                   jax.ShapeDtypeStruct((1, d), F32)])(dn, h, g, dh_in)


def _loss_head(h, tgt, g):
    s, d = h.shape

    def body(h_ref, t_ref, g_ref, dh_ref, dhb_ref, dg_ref, loss_ref):
        x, gv = h_ref[...], g_ref[...]
        r = lax.rsqrt(jnp.mean(x * x, axis=-1, keepdims=True) + _RMS_EPS)
        err = x * r * gv - t_ref[...]
        part = 0.5 * jnp.sum(jnp.mean(err * err, axis=-1, keepdims=True))
        dx, dg = _rms_bwd_math(err * (1.0 / d), x, gv)
        dh_ref[...] = dx
        dhb_ref[...] = dx.astype(BF16)

        @pl.when(pl.program_id(0) == 0)
        def _():
            dg_ref[...] = jnp.zeros_like(dg_ref)
            loss_ref[...] = jnp.zeros_like(loss_ref)

        dg_ref[...] += dg
        loss_ref[...] += jnp.full(loss_ref.shape, part, F32)

    row = pl.BlockSpec((_ROW_T, d), lambda i: (i, 0))
    vec = pl.BlockSpec((1, d), lambda i: (0, 0))
    one = pl.BlockSpec((1, _LANES), lambda i: (0, 0))
    return _pcall(
        body, name="loss_head", grid=(s // _ROW_T,), in_specs=[row, row, vec], out_specs=[row, row, vec, one],
        out_shape=[jax.ShapeDtypeStruct((s, d), F32), jax.ShapeDtypeStruct((s, d), BF16),
                   jax.ShapeDtypeStruct((1, d), F32), jax.ShapeDtypeStruct((1, _LANES), F32)])(h, tgt, g)


_CONV_T = 256
_CONV_RC = 64


def _conv_a_specs(s):
    c = _GROUP_COLS
    hb = _CONV_T // _HALO
    val = pl.BlockSpec((_CONV_T, c), lambda i: (i, 0))
    gate = pl.BlockSpec((_CONV_T, c), lambda i: (i, 1))
    hval = pl.BlockSpec((_HALO, c), lambda i: (jnp.maximum(i * hb - 1, 0), 0))
    hgate = pl.BlockSpec((_HALO, c), lambda i: (jnp.maximum(i * hb - 1, 0), 1))
    return val, gate, hval, hgate


def _fill_glu(val_ref, gate_ref, hval_ref, hgate_ref, hs_ref):
    i = pl.program_id(0)
    hs_ref[pl.ds(_HALO, _CONV_T), :] = val_ref[...].astype(F32) * _sig(gate_ref[...].astype(F32))
    halo = hval_ref[...].astype(F32) * _sig(hgate_ref[...].astype(F32))
    hs_ref[pl.ds(0, _HALO), :] = jnp.where(i > 0, halo, 0.0)


_SHIFT_ROWS = _CONV_T + _HALO - 8


def _fill_shifts(src_ref, sh_ref):
    for b in range(1, 8):
        sh_ref[b - 1] = src_ref[pl.ds(b, _SHIFT_ROWS), :]


def _tap_rows(src_ref, sh_ref, start, rows):
    b = start % 8
    if b == 0:
        return src_ref[pl.ds(start, rows), :]
    return sh_ref[b - 1, pl.ds(start - b, rows), :]


def _conv_rows(hs_ref, sh_ref, w_ref, r0, rows):
    off = _HALO - (_CONV_K - 1)
    acc = jnp.zeros((rows, _GROUP_COLS), F32)
    for kk in range(_CONV_K):
        acc = acc + w_ref[kk:kk + 1, :] * _tap_rows(hs_ref, sh_ref, r0 + off + kk, rows)
    return acc


def _ln_fwd(ca, g, b):
    mu = jnp.mean(ca, axis=-1, keepdims=True)
    xc = ca - mu
    rstd = lax.rsqrt(jnp.mean(xc * xc, axis=-1, keepdims=True) + _LN_EPS)
    xhat = xc * rstd
    return xhat, rstd, xhat * g + b


def _conv_a_fwd(uc, w, cb, lg, lb, rider=None):
    s = uc.shape[0]
    c = _GROUP_COLS

    def body(val_ref, gate_ref, hval_ref, hgate_ref, w_ref, cb_ref, lg_ref, lb_ref, o_ref, ca_ref, hs_ref, sh_ref):
        _fill_glu(val_ref, gate_ref, hval_ref, hgate_ref, hs_ref)
        _fill_shifts(hs_ref, sh_ref)
        for rc in range(_CONV_T // _CONV_RC):
            r0 = rc * _CONV_RC
            ca = _conv_rows(hs_ref, sh_ref, w_ref, r0, _CONV_RC) + cb_ref[...]
            ca_ref[pl.ds(r0, _CONV_RC), :] = ca
            _, _, ln = _ln_fwd(ca, lg_ref[...], lb_ref[...])
            o_ref[pl.ds(r0, _CONV_RC), :] = (ln * _sig(ln)).astype(BF16)

    val, gate, hval, hgate = _conv_a_specs(s)
    wspec = pl.BlockSpec((_CONV_K, c), lambda i: (0, 0))
    vec = pl.BlockSpec((1, c), lambda i: (0, 0))
    blk = pl.BlockSpec((_CONV_T, c), lambda i: (i, 0))
    return _pcall(
        body, name="conv_a_fwd", grid=(s // _CONV_T,),
        in_specs=[val, gate, hval, hgate, wspec, vec, vec, vec],
        out_specs=[blk, blk],
        out_shape=[jax.ShapeDtypeStruct((s, 2 * c), BF16), jax.ShapeDtypeStruct((s, c), F32)],
        scratch=[pltpu.VMEM((_CONV_T + _HALO, c), F32), pltpu.VMEM((7, _SHIFT_ROWS, c), F32)],
        rider=rider)(uc, uc, uc, uc, w, cb, lg, lb)


def _conv_a_bwd_ln(ca_all, dcat, lg, lb, rider=None):
    s = ca_all.shape[0]
    c = _GROUP_COLS

    def body(ca_ref, dy_ref, lg_ref, lb_ref, dca_ref, st_ref):
        @pl.when(pl.program_id(0) == 0)
        def _():
            st_ref[...] = jnp.zeros_like(st_ref)

        for rc in range(_CONV_T // _CONV_RC):
            r0 = rc * _CONV_RC
            ca = ca_ref[pl.ds(r0, _CONV_RC), :]
            xhat, rstd, ln = _ln_fwd(ca, lg_ref[...], lb_ref[...])
            sg = _sig(ln)
            dln = dy_ref[pl.ds(r0, _CONV_RC), :] * (sg * (1.0 + ln * (1.0 - sg)))
            dxh = dln * lg_ref[...]
            dca = rstd * (dxh - jnp.mean(dxh, axis=-1, keepdims=True)
                          - xhat * jnp.mean(dxh * xhat, axis=-1, keepdims=True))
            dca_ref[pl.ds(r0, _CONV_RC), :] = dca
            st_ref[0:1, :] += jnp.sum(dca, axis=0, keepdims=True)
            st_ref[1:2, :] += jnp.sum(dln * xhat, axis=0, keepdims=True)
            st_ref[2:3, :] += jnp.sum(dln, axis=0, keepdims=True)

    blk = pl.BlockSpec((_CONV_T, c), lambda i: (i, 0))
    vec = pl.BlockSpec((1, c), lambda i: (0, 0))
    st = pl.BlockSpec((8, c), lambda i: (0, 0))
    return _pcall(
        body, name="conv_a_bwd_ln", grid=(s // _CONV_T,),
        in_specs=[blk, blk, vec, vec], out_specs=[blk, st],
        out_shape=[jax.ShapeDtypeStruct((s, c), F32), jax.ShapeDtypeStruct((8, c), F32)],
        rider=rider)(ca_all, dcat, lg, lb)


def _conv_a_bwd_conv(uc, dca, w, rider=None):
    s = uc.shape[0]
    c = _GROUP_COLS
    nblk = s // _CONV_T
    hb = _CONV_T // _HALO
    off = _HALO - (_CONV_K - 1)

    def body(val_ref, gate_ref, hval_ref, hgate_ref, d_ref, dn_ref, w_ref, du_ref, dw_ref, hs_ref, ds_ref,
             hsh_ref, dsh_ref):
        i = pl.program_id(0)
        _fill_glu(val_ref, gate_ref, hval_ref, hgate_ref, hs_ref)
        ds_ref[pl.ds(0, _CONV_T), :] = d_ref[...]
        ds_ref[pl.ds(_CONV_T, _HALO), :] = jnp.where(i < nblk - 1, dn_ref[...], 0.0)
        _fill_shifts(hs_ref, hsh_ref)
        _fill_shifts(ds_ref, dsh_ref)

        @pl.when(i == 0)
        def _():
            dw_ref[...] = jnp.zeros_like(dw_ref)

        for rc in range(_CONV_T // _CONV_RC):
            r0 = rc * _CONV_RC
            dcur = ds_ref[pl.ds(r0, _CONV_RC), :]
            dh = jnp.zeros((_CONV_RC, c), F32)
            for kk in range(_CONV_K):
                dh = dh + w_ref[kk:kk + 1, :] * _tap_rows(ds_ref, dsh_ref, r0 + _CONV_K - 1 - kk, _CONV_RC)
                dw_ref[kk:kk + 1, :] += jnp.sum(dcur * _tap_rows(hs_ref, hsh_ref, r0 + off + kk, _CONV_RC),
                                                 axis=0, keepdims=True)
            v = val_ref[pl.ds(r0, _CONV_RC), :].astype(F32)
            sg = _sig(gate_ref[pl.ds(r0, _CONV_RC), :].astype(F32))
            du_ref[pl.ds(r0, _CONV_RC), pl.ds(0, c)] = (dh * sg).astype(BF16)
            du_ref[pl.ds(r0, _CONV_RC), pl.ds(c, c)] = (dh * v * sg * (1.0 - sg)).astype(BF16)

    val, gate, hval, hgate = _conv_a_specs(s)
    blk = pl.BlockSpec((_CONV_T, c), lambda i: (i, 0))
    nxt = pl.BlockSpec((_HALO, c), lambda i: (jnp.minimum((i + 1) * hb, s // _HALO - 1), 0))
    wspec = pl.BlockSpec((_CONV_K, c), lambda i: (0, 0))
    return _pcall(
        body, name="conv_a_bwd_conv", grid=(nblk,),
        in_specs=[val, gate, hval, hgate, blk, nxt, wspec],
        out_specs=[pl.BlockSpec((_CONV_T, 2 * c), lambda i: (i, 0)), pl.BlockSpec((_HALO, c), lambda i: (0, 0))],
        out_shape=[jax.ShapeDtypeStruct((s, 2 * c), BF16), jax.ShapeDtypeStruct((_HALO, c), F32)],
        scratch=[pltpu.VMEM((_CONV_T + _HALO, c), F32), pltpu.VMEM((_CONV_T + _HALO, c), F32),
                 pltpu.VMEM((7, _SHIFT_ROWS, c), F32), pltpu.VMEM((7, _SHIFT_ROWS, c), F32)],
        rider=rider,
    )(uc, uc, uc, uc, dca, dca, w)


_SC_T = 256
_SC_RC = 32
_SC_LC = 512


def _sc_chunks(d):
    return [(pl.ds(r0, _SC_RC), pl.ds(l0, _SC_LC)) for r0 in range(0, _SC_T, _SC_RC) for l0 in range(0, d, _SC_LC)]


def _short_conv_fwd(u2, w):
    s, d3 = u2.shape
    d = d3 // 3
    hb = _SC_T // _SC_HALO

    def body(b_ref, c_ref, v_ref, hc_ref, hv_ref, w_ref, o_ref, cs_ref):
        i = pl.program_id(0)
        cs_ref[pl.ds(0, _SC_HALO), :] = jnp.where(i > 0, hc_ref[...].astype(F32) * hv_ref[...].astype(F32), 0.0)
        for rows, lanes in _sc_chunks(d):
            cs_ref[pl.ds(_SC_HALO + rows.start, _SC_RC), lanes] = (
                c_ref[rows, lanes].astype(F32) * v_ref[rows, lanes].astype(F32))
        for rows, lanes in _sc_chunks(d):
            taps = [cs_ref[pl.ds(_SC_HALO - 2 + kk + rows.start, _SC_RC), lanes] for kk in range(3)]
            conv = w_ref[0:1, lanes] * taps[0] + w_ref[1:2, lanes] * taps[1] + w_ref[2:3, lanes] * taps[2]
            o_ref[rows, lanes] = (b_ref[rows, lanes].astype(F32) * conv).astype(BF16)

    def col(j):
        return pl.BlockSpec((_SC_T, d), lambda i: (i, j))

    def halo(j):
        return pl.BlockSpec((_SC_HALO, d), lambda i: (jnp.maximum(i * hb - 1, 0), j))

    return _pcall(
        body, name="short_conv_fwd", grid=(s // _SC_T,),
        in_specs=[col(0), col(1), col(2), halo(1), halo(2), pl.BlockSpec((3, d), lambda i: (0, 0))],
        out_specs=pl.BlockSpec((_SC_T, d), lambda i: (i, 0)),
        out_shape=jax.ShapeDtypeStruct((s, d), BF16),
        scratch=[pltpu.VMEM((_SC_T + _SC_HALO, d), F32)])(u2, u2, u2, u2, u2, w)


def _short_conv_bwd(u2, dsc, w, rider=None):
    s, d3 = u2.shape
    d = d3 // 3
    hb = _SC_T // _SC_HALO
    nblk = s // _SC_T

    def body(b_ref, c_ref, v_ref, hc_ref, hv_ref, nb_ref, d_ref, nd_ref, w_ref, du_ref, dw_ref, cs_ref, ds_ref):
        i = pl.program_id(0)
        cs_ref[pl.ds(0, _SC_HALO), :] = jnp.where(i > 0, hc_ref[...].astype(F32) * hv_ref[...].astype(F32), 0.0)
        ds_ref[pl.ds(_SC_T, _SC_HALO), :] = jnp.where(i < nblk - 1, nd_ref[...] * nb_ref[...].astype(F32), 0.0)
        for rows, lanes in _sc_chunks(d):
            cs_ref[pl.ds(_SC_HALO + rows.start, _SC_RC), lanes] = (
                c_ref[rows, lanes].astype(F32) * v_ref[rows, lanes].astype(F32))
            ds_ref[rows, lanes] = d_ref[rows, lanes] * b_ref[rows, lanes].astype(F32)

        @pl.when(i == 0)
        def _():
            dw_ref[...] = jnp.zeros_like(dw_ref)

        for l0 in range(0, d, _SC_LC):
            lanes = pl.ds(l0, _SC_LC)
            dw_acc = [jnp.zeros((8, _SC_LC), F32)] * 3
            for r0 in range(0, _SC_T, _SC_RC):
                rows = pl.ds(r0, _SC_RC)
                taps = [cs_ref[pl.ds(_SC_HALO - 2 + kk + r0, _SC_RC), lanes] for kk in range(3)]
                conv = w_ref[0:1, lanes] * taps[0] + w_ref[1:2, lanes] * taps[1] + w_ref[2:3, lanes] * taps[2]
                dconv = ds_ref[rows, lanes]
                dcv = (w_ref[2:3, lanes] * dconv + w_ref[1:2, lanes] * ds_ref[pl.ds(r0 + 1, _SC_RC), lanes]
                       + w_ref[0:1, lanes] * ds_ref[pl.ds(r0 + 2, _SC_RC), lanes])
                du_ref[rows, lanes] = (d_ref[rows, lanes] * conv).astype(BF16)
                du_ref[rows, pl.ds(d + l0, _SC_LC)] = (dcv * v_ref[rows, lanes].astype(F32)).astype(BF16)
                du_ref[rows, pl.ds(2 * d + l0, _SC_LC)] = (dcv * c_ref[rows, lanes].astype(F32)).astype(BF16)
                for kk in range(3):
                    prod = dconv * taps[kk]
                    dw_acc[kk] = dw_acc[kk] + sum(prod[t:t + 8] for t in range(0, _SC_RC, 8))
            for kk in range(3):
                dw_ref[kk:kk + 1, lanes] += jnp.sum(dw_acc[kk], axis=0, keepdims=True)

    def col(j):
        return pl.BlockSpec((_SC_T, d), lambda i: (i, j))

    def halo(j):
        return pl.BlockSpec((_SC_HALO, d), lambda i: (jnp.maximum(i * hb - 1, 0), j))

    def nxt(j):
        return pl.BlockSpec((_SC_HALO, d), lambda i: (jnp.minimum((i + 1) * hb, s // _SC_HALO - 1), j))

    return _pcall(
        body, name="short_conv_bwd", grid=(nblk,),
        in_specs=[col(0), col(1), col(2), halo(1), halo(2), nxt(0), col(0), nxt(0),
                  pl.BlockSpec((3, d), lambda i: (0, 0))],
        out_specs=[pl.BlockSpec((_SC_T, d3), lambda i: (i, 0)), pl.BlockSpec((8, d), lambda i: (0, 0))],
        out_shape=[jax.ShapeDtypeStruct((s, d3), BF16), jax.ShapeDtypeStruct((8, d), F32)],
        scratch=[pltpu.VMEM((_SC_T + _SC_HALO, d), F32), pltpu.VMEM((_SC_T + _SC_HALO, d), F32)],
        rider=rider,
    )(u2, u2, u2, u2, u2, u2, dsc, dsc, w)


def _bucket_maps():
    a_idx = jnp.arange(_STEPS)[:, None]
    c_idx = jnp.arange(2 * _STEPS)[None, :]
    mdist = jnp.clip(a_idx + _STEPS - c_idx, 0, _STEPS)
    max_exact = _NUM_BUCKETS // 2
    maps = []
    for _, dil in _GROUPS:
        nn = mdist * dil
        nf = jnp.maximum(nn, 1).astype(F32)
        large = max_exact + (jnp.log(nf / max_exact) / math.log(_MAX_DISTANCE / max_exact)
                             * (_NUM_BUCKETS - max_exact)).astype(jnp.int32)
        maps.append(jnp.where(nn < max_exact, nn, jnp.minimum(large, _NUM_BUCKETS - 1)).astype(jnp.int32))
    return jnp.stack(maps, axis=0)


def _bias_expand(rel_bias, buckets):
    nh = rel_bias.shape[1]

    def body(rb_ref, bk_ref, o_ref):
        h = pl.program_id(0)
        bk = bk_ref[0]
        acc = jnp.zeros(bk.shape, F32)
        for b in range(_NUM_BUCKETS):
            acc = jnp.where(bk == b, rb_ref[b, h], acc)
        a = lax.broadcasted_iota(jnp.int32, bk.shape, 0)
        c = lax.broadcasted_iota(jnp.int32, bk.shape, 1)
        mdist = a + _STEPS - c
        o_ref[0] = jnp.where((mdist >= 0) & (mdist <= _STEPS), acc, _NEG)

    return _pcall(
        body, name="bias_expand", grid=(nh,),
        in_specs=[pl.BlockSpec(memory_space=pltpu.SMEM),
                  pl.BlockSpec((1, _STEPS, 2 * _STEPS), lambda h: (h // 8, 0, 0))],
        out_specs=pl.BlockSpec((1, _STEPS, 2 * _STEPS), lambda h: (h, 0, 0)),
        out_shape=jax.ShapeDtypeStruct((nh, _STEPS, 2 * _STEPS), F32))(rel_bias, buckets)


def _bias_reduce(ds_all, buckets):
    nh = ds_all.shape[0]

    def body(ds_ref, bk_ref, o_ref):
        t, bk = ds_ref[0], bk_ref[0]
        rows = lax.broadcasted_iota(jnp.int32, (_NUM_BUCKETS, _LANES), 0)
        out = jnp.zeros((_NUM_BUCKETS, _LANES), F32)
        for b in range(_NUM_BUCKETS):
            out = jnp.where(rows == b, jnp.sum(jnp.where(bk == b, t, 0.0)), out)
        o_ref[0] = out

    blk = pl.BlockSpec((1, _STEPS, 2 * _STEPS), lambda h: (h, 0, 0))
    return _pcall(
        body, name="bias_reduce", grid=(nh,),
        in_specs=[blk, pl.BlockSpec((1, _STEPS, 2 * _STEPS), lambda h: (h // 8, 0, 0))],
        out_specs=pl.BlockSpec((1, _NUM_BUCKETS, _LANES), lambda h: (h, 0, 0)),
        out_shape=jax.ShapeDtypeStruct((nh, _NUM_BUCKETS, _LANES), F32))(ds_all, buckets)


def _sub_residues(dil):
    return 4 if dil % 16 == 0 else 1


def _strided_rows(ref, tmp_ref, p, r, dil):
    sub = _sub_residues(dil)
    if dil == 1:
        return [ref[p]]
    if sub == 1:
        return [ref[p, pl.ds(r, _STEPS, stride=dil), :]]
    tmp_ref[...] = ref[p, pl.ds(r, _STEPS * sub, stride=dil // sub), :]
    return [tmp_ref[pl.ds(q, _STEPS, stride=sub), :] for q in range(sub)]


def _store_strided(ref, tmp_ref, p, r, dil, vals):
    sub = _sub_residues(dil)
    if dil == 1:
        ref[p] = vals[0]
    elif sub == 1:
        ref[p, pl.ds(r, _STEPS, stride=dil), :] = vals[0]
    else:
        for q, val in enumerate(vals):
            tmp_ref[pl.ds(q, _STEPS, stride=sub), :] = val
        ref[p, pl.ds(r, _STEPS * sub, stride=dil // sub), :] = tmp_ref[...]


def _tmp_rows(dil, count):
    sub = _sub_residues(dil)
    return [pltpu.VMEM((_STEPS * sub, _LANES), F32)] * count if sub > 1 else []


def _head_masks():
    lane = lax.broadcasted_iota(jnp.int32, (1, _LANES), 1)
    return [lane < _HEAD_DIM, lane >= _HEAD_DIM]


def _stack_heads(x2, masks):
    return jnp.concatenate([jnp.where(masks[0], x2, 0), jnp.where(masks[1], x2, 0)], axis=0)


def _unstack_heads(y, masks):
    return jnp.where(masks[0], y[:_STEPS], y[_STEPS:])


def _scores(qs2, k2, b_ref, j, first):
    sc = lax.dot_general(qs2, k2, (((1,), (1,)), ((), ())), preferred_element_type=F32)
    sc = sc * (_HEAD_DIM ** -0.5) + jnp.concatenate([b_ref[2 * j], b_ref[2 * j + 1]], axis=0)
    col = lax.broadcasted_iota(jnp.int32, sc.shape, 1)
    return jnp.where(jnp.logical_and(first, col < _STEPS), _NEG, sc)


_PAIRS = _GROUP_COLS // _LANES


def _attn_fwd(uq, uk, uv, bias, g, dil, pp, rider=None):
    s = uq.shape[1]
    rb = _STEPS * dil
    nb = s // rb
    npb = _PAIRS // pp

    sub = _sub_residues(dil)

    def body(q_ref, kc_ref, kp_ref, vc_ref, vp_ref, b_ref, o_ref, l_ref, *tmp):
        tmp = tmp + (None,) * 7
        n, r = pl.program_id(1), pl.program_id(2)
        first = n == 0
        masks = _head_masks()
        for j in range(pp):
            qs = _strided_rows(q_ref, tmp[0], j, r, dil)
            kps, kcs = _strided_rows(kp_ref, tmp[1], j, r, dil), _strided_rows(kc_ref, tmp[2], j, r, dil)
            vps, vcs = _strided_rows(vp_ref, tmp[3], j, r, dil), _strided_rows(vc_ref, tmp[4], j, r, dil)
            o_res, l_res = [], []
            for q in range(sub):
                q2 = qs[q].astype(BF16)
                k2 = jnp.concatenate([kps[q], kcs[q]], axis=0).astype(BF16)
                v2 = jnp.concatenate([vps[q], vcs[q]], axis=0).astype(BF16)
                sc = _scores(_stack_heads(q2, masks), k2, b_ref, j, first)
                mx = jnp.max(sc, axis=-1, keepdims=True)
                p = jnp.exp(sc - mx)
                den = jnp.sum(p, axis=-1, keepdims=True)
                o2 = jnp.dot(p.astype(BF16), v2, preferred_element_type=F32) / den
                o_res.append(_unstack_heads(o2, masks))
                l_res.append(_unstack_heads(jnp.broadcast_to(mx + jnp.log(den), o2.shape), masks))
            _store_strided(o_ref, tmp[5], j, r, dil, o_res)
            _store_strided(l_ref, tmp[6], j, r, dil, l_res)

    cur = pl.BlockSpec((pp, rb, _LANES), lambda hb, n, r: (g * npb + hb, n, 0))
    prev = pl.BlockSpec((pp, rb, _LANES), lambda hb, n, r: (g * npb + hb, jnp.maximum(n - 1, 0), 0))
    bspec = pl.BlockSpec((2 * pp, _STEPS, 2 * _STEPS), lambda hb, n, r: (g * npb + hb, 0, 0))
    ospec = pl.BlockSpec((pp, rb, _LANES), lambda hb, n, r: (hb, n, 0))
    sh = jax.ShapeDtypeStruct((_PAIRS, s, _LANES), F32)
    return _pcall(
        body, name=f"attn_fwd_g{g}", grid=(npb, nb, dil // sub),
        in_specs=[cur, cur, prev, cur, prev, bspec], out_specs=[ospec, ospec], out_shape=[sh, sh],
        scratch=_tmp_rows(dil, 7), rider=rider,
    )(uq, uk, uk, uv, uv, bias)


def _attn_merge(outs, lses, cat):
    s = outs[0].shape[1]
    c = _GROUP_COLS

    def body(o0, o1, o2, l0, l1, l2, cat_in, cat_ref, lse_ref):
        del cat_in
        a0, a1, a2 = l0[...], l1[...], l2[...]
        mx = jnp.maximum(jnp.maximum(a0, a1), a2)
        w0, w1, w2 = jnp.exp(a0 - mx), jnp.exp(a1 - mx), jnp.exp(a2 - mx)
        den = w0 + w1 + w2
        y = ((w0 * o0[...] + w1 * o1[...] + w2 * o2[...]) / den).astype(BF16)
        for p in range(_PAIRS):
            cat_ref[:, p * _LANES:(p + 1) * _LANES] = y[p]
        lse_ref[...] = mx + jnp.log(den)

    blk = pl.BlockSpec((_PAIRS, _ROW_T, _LANES), lambda i: (0, i, 0))
    return _pcall(
        body, name="attn_merge", grid=(s // _ROW_T,),
        in_specs=[blk] * 6 + [_ANY],
        out_specs=[pl.BlockSpec((_ROW_T, c), lambda i: (i, 1)), blk],
        out_shape=[jax.ShapeDtypeStruct(cat.shape, BF16), jax.ShapeDtypeStruct((_PAIRS, s, _LANES), F32)],
        aliases={6: 0})(*outs, *lses, cat)


def _attn_delta(dcat, cat):
    s = dcat.shape[0]
    c = _GROUP_COLS
    seg = (jnp.arange(c)[:, None] // _HEAD_DIM == jnp.arange(c)[None, :] // _HEAD_DIM).astype(BF16)

    def body(dy_ref, y_ref, seg_ref, dl_ref, dys_ref):
        dy = dy_ref[...]
        prod = dy * y_ref[...].astype(F32)
        hi = prod.astype(BF16)
        lo = (prod - hi.astype(F32)).astype(BF16)
        dl = (jnp.dot(hi, seg_ref[...], preferred_element_type=F32)
              + jnp.dot(lo, seg_ref[...], preferred_element_type=F32))
        for p in range(_PAIRS):
            dl_ref[p] = dl[:, p * _LANES:(p + 1) * _LANES]
            dys_ref[p] = dy[:, p * _LANES:(p + 1) * _LANES]

    right = pl.BlockSpec((_ROW_T, c), lambda i: (i, 1))
    blk = pl.BlockSpec((_PAIRS, _ROW_T, _LANES), lambda i: (0, i, 0))
    sh = jax.ShapeDtypeStruct((_PAIRS, s, _LANES), F32)
    return _pcall(
        body, name="attn_delta", grid=(s // _ROW_T,),
        in_specs=[right, right, pl.BlockSpec((c, c), lambda i: (0, 0))],
        out_specs=[blk, blk], out_shape=[sh, sh])(dcat, cat, seg)


def _attn_bwd(uq, uk, uv, dys, lse, delta, bias, prev_grads, g, dil, pp, rider=None):
    s = uq.shape[1]
    rb = _STEPS * dil
    nb = s // rb
    npb = _PAIRS // pp
    scale = _HEAD_DIM ** -0.5

    sub = _sub_residues(dil)

    def body(q_ref, kc_ref, kp_ref, vc_ref, vp_ref, dy_ref, l_ref, dl_ref, b_ref, *rest):
        rest = rest[len(prev_grads):]
        dq_ref, dk_ref, dv_ref, dsa_ref, dkc_ref, dvc_ref = rest[:6]
        tmp = rest[6:] + (None,) * 11
        n, r = pl.program_id(1), pl.program_id(2)

        def carry_slot(j, q):
            return ((r + (dil // sub) * q) * pp + j) if sub > 1 else r * pp + j

        @pl.when(jnp.logical_and(n == 0, r == 0))
        def _():
            dsa_ref[...] = jnp.zeros_like(dsa_ref)

        @pl.when(n == 0)
        def _():
            for j in range(pp):
                for q in range(sub):
                    dkc_ref[carry_slot(j, q)] = jnp.zeros((_STEPS, _LANES), F32)
                    dvc_ref[carry_slot(j, q)] = jnp.zeros((_STEPS, _LANES), F32)

        @pl.when(n < nb)
        def _():
            first = n == 0
            masks = _head_masks()
            for j in range(pp):
                qs = _strided_rows(q_ref, tmp[0], j, r, dil)
                kps, kcs = _strided_rows(kp_ref, tmp[1], j, r, dil), _strided_rows(kc_ref, tmp[2], j, r, dil)
                vps, vcs = _strided_rows(vp_ref, tmp[3], j, r, dil), _strided_rows(vc_ref, tmp[4], j, r, dil)
                dys_ = _strided_rows(dy_ref, tmp[5], j, r, dil)
                lses = _strided_rows(l_ref, tmp[6], j, r, dil)
                dls = _strided_rows(dl_ref, tmp[7], j, r, dil)
                ds_sum = [jnp.zeros((_STEPS, 2 * _STEPS), F32)] * 2
                dq_res, dk_res, dv_res = [], [], []
                for q in range(sub):
                    q2 = qs[q].astype(BF16)
                    k2 = jnp.concatenate([kps[q], kcs[q]], axis=0).astype(BF16)
                    v2 = jnp.concatenate([vps[q], vcs[q]], axis=0).astype(BF16)
                    dy2 = dys_[q].astype(BF16)
                    qs2, dys2 = _stack_heads(q2, masks), _stack_heads(dy2, masks)
                    per_row = lambda st: jnp.concatenate([st[:, 0:1], st[:, _HEAD_DIM:_HEAD_DIM + 1]], axis=0)
                    sc = _scores(qs2, k2, b_ref, j, first)
                    p = jnp.exp(sc - per_row(lses[q]))
                    dp = lax.dot_general(dys2, v2, (((1,), (1,)), ((), ())), preferred_element_type=F32)
                    ds = p * (dp - per_row(dls[q]))
                    ds_sum[0] = ds_sum[0] + ds[:_STEPS]
                    ds_sum[1] = ds_sum[1] + ds[_STEPS:]
                    dsb = ds.astype(BF16)
                    dq_p = _unstack_heads(jnp.dot(dsb, k2, preferred_element_type=F32), masks)
                    tdn = (((0,), (0,)), ((), ()))
                    dk_p = lax.dot_general(dsb, qs2, tdn, preferred_element_type=F32) * scale
                    dv_p = lax.dot_general(p.astype(BF16), dys2, tdn, preferred_element_type=F32)
                    slot = carry_slot(j, q)
                    dq_res.append(dq_p * scale)
                    dk_res.append(dkc_ref[slot] + dk_p[:_STEPS])
                    dv_res.append(dvc_ref[slot] + dv_p[:_STEPS])
                    dkc_ref[slot] = dk_p[_STEPS:]
                    dvc_ref[slot] = dv_p[_STEPS:]
                for hh in range(2):
                    dsa_ref[2 * j + hh] += ds_sum[hh]
                _store_strided(dq_ref, tmp[8], j, r, dil, dq_res)
                _store_strided(dk_ref, tmp[9], j, r, dil, dk_res)
                _store_strided(dv_ref, tmp[10], j, r, dil, dv_res)

        @pl.when(n == nb)
        def _():
            for j in range(pp):
                _store_strided(dk_ref, tmp[9], j, r, dil, [dkc_ref[carry_slot(j, q)] for q in range(sub)])
                _store_strided(dv_ref, tmp[10], j, r, dil, [dvc_ref[carry_slot(j, q)] for q in range(sub)])

    def clamp(n):
        return jnp.minimum(n, nb - 1)

    cur = pl.BlockSpec((pp, rb, _LANES), lambda hb, n, r: (g * npb + hb, clamp(n), 0))
    prev = pl.BlockSpec((pp, rb, _LANES), lambda hb, n, r: (g * npb + hb, jnp.maximum(clamp(n) - 1, 0), 0))
    stat = pl.BlockSpec((pp, rb, _LANES), lambda hb, n, r: (hb, clamp(n), 0))
    bspec = pl.BlockSpec((2 * pp, _STEPS, 2 * _STEPS), lambda hb, n, r: (g * npb + hb, 0, 0))
    dkspec = pl.BlockSpec((pp, rb, _LANES), lambda hb, n, r: (g * npb + hb, jnp.maximum(n - 1, 0), 0))
    dsspec = pl.BlockSpec((2 * pp, _STEPS, 2 * _STEPS), lambda hb, n, r: (hb, 0, 0))
    wide = jax.ShapeDtypeStruct((3 * _PAIRS, s, _LANES), F32)
    np_ = len(prev_grads)
    return _pcall(
        body, name=f"attn_bwd_g{g}", grid=(npb, nb + 1, dil // sub),
        in_specs=[cur, cur, prev, cur, prev, stat, stat, stat, bspec] + [_ANY] * np_,
        out_specs=[cur, dkspec, dkspec, dsspec],
        out_shape=[wide, wide, wide, jax.ShapeDtypeStruct((8, _STEPS, 2 * _STEPS), F32)],
        scratch=[pltpu.VMEM((dil * pp, _STEPS, _LANES), F32), pltpu.VMEM((dil * pp, _STEPS, _LANES), F32)]
        + _tmp_rows(dil, 11),
        aliases={9 + t: t for t in range(np_)}, rider=rider,
    )(uq, uk, uk, uv, uv, dys, lse, delta, bias, *prev_grads)


def _place():
    x, y, c = lax.axis_index("x"), lax.axis_index("y"), lax.axis_index("c")
    chips = [(1 - x, y), (x, 1 - y), (1 - x, 1 - y)]
    return x, y, c, chips


def _slab(ref, axis, chip, width):
    start = pl.multiple_of(chip * width, width)
    if axis == 0:
        return ref.at[pl.ds(start, width), :]
    return ref.at[:, pl.ds(start, width)]


def _run_rider(rider, name):
    nin, nout = len(rider.ins), len(rider.out_shapes)

    def body(*refs):
        ins, outs, scr = refs[:nin], refs[nin:nin + nout], refs[nin + nout:]
        rider.start(ins, outs, scr)
        rider.finish(ins, outs, scr)

    return _pcall(body, name=name, in_specs=[_ANY] * nin, out_specs=[_ANY] * nout, out_shape=rider.out_shapes,
                  scratch=rider.scratch)(*rider.ins)


def _gather_halves_rider(shard, axis):
    shape = list(shard.shape)
    shape[axis] *= 4
    full = jax.ShapeDtypeStruct(tuple(shape), shard.dtype)
    half = shard.shape[0] // 2
    width = shard.shape[axis]

    def region(out, chip, core):
        if axis == 0:
            return out.at[pl.ds(pl.multiple_of(chip * width + core * half, half), half), :]
        return out.at[pl.ds(pl.multiple_of(core * half, half), half), pl.ds(pl.multiple_of(chip * width, width), width)]

    def copies(ins, outs, scr):
        send, recv, loc = scr
        (src,), (out,) = ins, outs
        x, y, c, chips = _place()
        mine = 2 * x + y
        own = pltpu.make_async_copy(src, _slab(out, axis, mine, width), loc.at[0])
        my_half = src.at[pl.ds(pl.multiple_of(c * half, half), half), :]
        over_ici, ici_in, to_sib, sib_in = [], [], [], []
        for j, (px, py) in enumerate(chips):
            theirs = 2 * px + py
            ici = dict(send_sem=send.at[j], recv_sem=recv.at[j], device_id=(px, py, c), device_id_type=MESH)
            d2d = dict(send_sem=send.at[3 + j], recv_sem=recv.at[3 + j], device_id=(x, y, 1 - c),
                       device_id_type=MESH)
            over_ici.append(pltpu.make_async_remote_copy(src_ref=my_half, dst_ref=region(out, mine, c), **ici))
            ici_in.append(pltpu.make_async_remote_copy(src_ref=my_half, dst_ref=region(out, theirs, c), **ici))
            to_sib.append(pltpu.make_async_remote_copy(
                src_ref=region(out, theirs, c), dst_ref=region(out, theirs, c), **d2d))
            sib_in.append(pltpu.make_async_remote_copy(
                src_ref=region(out, theirs, c), dst_ref=region(out, theirs, 1 - c), **d2d))
        return own, over_ici, ici_in, to_sib, sib_in

    def start(ins, outs, scr):
        own, over_ici, _, _, _ = copies(ins, outs, scr)
        own.start()
        for cp in over_ici:
            cp.start()

    def finish(ins, outs, scr):
        own, over_ici, ici_in, to_sib, sib_in = copies(ins, outs, scr)
        for j in range(3):
            ici_in[j].wait_recv()
            to_sib[j].start()
        for cp in sib_in:
            cp.wait_recv()
        own.wait()
        for cp in over_ici + to_sib:
            cp.wait_send()

    return _Rider([shard], [full], [pltpu.SemaphoreType.DMA((6,)), pltpu.SemaphoreType.DMA((6,)),
                                    pltpu.SemaphoreType.DMA((1,))], start, finish)


def _join_riders(riders):
    if len(riders) == 1:
        return riders[0]

    def parts(ins, outs, scr):
        pi = po = ps = 0
        for rd in riders:
            ni, no, ns = len(rd.ins), len(rd.out_shapes), len(rd.scratch)
            yield rd, ins[pi:pi + ni], outs[po:po + no], scr[ps:ps + ns]
            pi, po, ps = pi + ni, po + no, ps + ns

    def start(ins, outs, scr):
        for rd, i, o, sc in parts(ins, outs, scr):
            rd.start(i, o, sc)

    def finish(ins, outs, scr):
        for rd, i, o, sc in parts(ins, outs, scr):
            rd.finish(i, o, sc)

    return _Rider(sum((rd.ins for rd in riders), []), sum((rd.out_shapes for rd in riders), []),
                  sum((rd.scratch for rd in riders), []), start, finish)


def _scatter_rider(grads, axes, rows=None):
    nw = len(grads)
    outs_shape = []
    for gr, ax in zip(grads, axes):
        shape = list(gr.shape)
        shape[ax] //= 4
        if rows is not None:
            assert ax == 1
            shape[0] = rows[1] - rows[0]
        outs_shape.append(jax.ShapeDtypeStruct((3,) + tuple(shape), gr.dtype))

    def copies(ins, outs, scr):
        send, recv = scr
        x, y, c, chips = _place()
        cps = []
        for t in range(nw):
            width = ins[t].shape[axes[t]] // 4
            src = ins[t] if rows is None else ins[t].at[pl.ds(rows[0], rows[1] - rows[0]), :]
            for j, (px, py) in enumerate(chips):
                cps.append(pltpu.make_async_remote_copy(
                    src_ref=_slab(src, axes[t], 2 * px + py, width), dst_ref=outs[t].at[j],
                    send_sem=send.at[3 * t + j], recv_sem=recv.at[3 * t + j],
                    device_id=(px, py, c), device_id_type=MESH))
        return cps

    def start(ins, outs, scr):
        for cp in copies(ins, outs, scr):
            cp.start()

    def finish(ins, outs, scr):
        cps = copies(ins, outs, scr)
        for cp in cps:
            cp.wait_recv()
        for cp in cps:
            cp.wait_send()

    return _Rider(grads, outs_shape, [pltpu.SemaphoreType.DMA((3 * nw,)), pltpu.SemaphoreType.DMA((3 * nw,))],
                  start, finish)


def _swap_rider(parts):
    nw = len(parts)

    def copies(ins, outs, scr):
        send, recv = scr
        x, y, c, _ = _place()
        return [pltpu.make_async_remote_copy(
            src_ref=ins[t], dst_ref=outs[t], send_sem=send.at[t], recv_sem=recv.at[t],
            device_id=(x, y, 1 - c), device_id_type=MESH) for t in range(nw)]

    def start(ins, outs, scr):
        for cp in copies(ins, outs, scr):
            cp.start()

    def finish(ins, outs, scr):
        cps = copies(ins, outs, scr)
        for cp in cps:
            cp.wait_recv()
        for cp in cps:
            cp.wait_send()

    return _Rider(parts, [jax.ShapeDtypeStruct(p.shape, p.dtype) for p in parts],
                  [pltpu.SemaphoreType.DMA((nw,)), pltpu.SemaphoreType.DMA((nw,))], start, finish)


def _sum_all_devices(buf, name):
    rows, cols = buf.shape

    def body(in_ref, o_ref, gat_ref, send, recv):
        x, y, c, _ = _place()
        me = 4 * x + 2 * y + c
        gat_ref[me] = in_ref[...]
        started = []
        for mask in range(1, 8):
            fx, fy, fc = (mask >> 2) & 1, (mask >> 1) & 1, mask & 1
            peer = (x + fx * (1 - 2 * x), y + fy * (1 - 2 * y), c + fc * (1 - 2 * c))
            cp = pltpu.make_async_remote_copy(
                src_ref=in_ref, dst_ref=gat_ref.at[me], send_sem=send.at[mask - 1], recv_sem=recv.at[mask - 1],
                device_id=peer, device_id_type=MESH)
            cp.start()
            started.append(cp)
        for cp in started:
            cp.wait_recv()
        for cp in started:
            cp.wait_send()
        acc = gat_ref[0]
        for t in range(1, 8):
            acc = acc + gat_ref[t]
        o_ref[...] = acc

    vm = pl.BlockSpec(memory_space=pltpu.VMEM)
    return _pcall(
        body, name=name, in_specs=[vm], out_specs=vm, out_shape=jax.ShapeDtypeStruct((rows, cols), F32),
        scratch=[pltpu.VMEM((8, rows, cols), F32), pltpu.SemaphoreType.DMA((7,)), pltpu.SemaphoreType.DMA((7,))],
    )(buf)


_UPD_T = 256


def _sum_partials(own, got, name):
    rows, cols = own.shape
    tr = min(_UPD_T, rows)

    def body(own_ref, got_ref, o_ref):
        acc = own_ref[...].astype(F32)
        for j in range(3):
            acc = acc + got_ref[j].astype(F32)
        o_ref[...] = acc

    blk = pl.BlockSpec((tr, cols), lambda i: (i, 0))
    return _pcall(
        body, name=name, grid=(rows // tr,),
        in_specs=[blk, pl.BlockSpec((3, tr, cols), lambda i: (0, i, 0))], out_specs=blk,
        out_shape=jax.ShapeDtypeStruct((rows, cols), F32))(own, got)


def _adamw_math(w, gr, m, v):
    m = _B1 * m + (1.0 - _B1) * gr
    v = _B2 * v + (1.0 - _B2) * (gr * gr)
    m_hat = m / (1.0 - _B1 ** _STEP)
    v_hat = v / (1.0 - _B2 ** _STEP)
    delta = -_LR * (m_hat / (jnp.sqrt(v_hat) + _EPS) + _WD * w)
    return delta, m, v


def _adamw(w, m, v, parts, name):
    rows, cols = w.shape
    tr = min(_UPD_T, rows)
    npart = len(parts)

    def body(w_ref, m_ref, v_ref, *rest):
        p_refs, (g_ref, d_ref, nm_ref, nv_ref) = rest[:npart], rest[npart:]
        gr = p_refs[0][...]
        for p in p_refs[1:]:
            gr = gr + p[...]
        delta, nm, nv = _adamw_math(w_ref[...], gr, m_ref[...], v_ref[...])
        g_ref[...] = gr
        d_ref[...] = delta
        nm_ref[...] = nm
        nv_ref[...] = nv

    blk = pl.BlockSpec((tr, cols), lambda i: (i, 0))
    sh = jax.ShapeDtypeStruct((rows, cols), F32)
    return _pcall(body, name=name, grid=(rows // tr,), in_specs=[blk] * (3 + npart), out_specs=[blk] * 4,
                  out_shape=[sh] * 4)(w, m, v, *parts)


def _adamw_layers(w, m, v, parts, name):
    _, rows, cols = w.shape
    tr = min(_UPD_T, rows)
    npart = len(parts[0])

    def body(w_ref, m_ref, v_ref, *rest):
        p_refs, (g_ref, d_ref, nm_ref, nv_ref) = rest[:2 * npart], rest[2 * npart:]
        grs = []
        for layer in range(2):
            gr = p_refs[layer * npart][...]
            for p in p_refs[layer * npart + 1:(layer + 1) * npart]:
                gr = gr + p[...]
            grs.append(gr)
        gr = jnp.where(pl.program_id(0) == 0, grs[0], grs[1])
        delta, nm, nv = _adamw_math(w_ref[...], gr, m_ref[...], v_ref[...])
        g_ref[...] = gr
        d_ref[...] = delta
        nm_ref[...] = nm
        nv_ref[...] = nv

    blk = pl.BlockSpec((None, tr, cols), lambda l, i: (l, i, 0))

    def part_spec(layer):
        return pl.BlockSpec((tr, cols), lambda l, i: (jnp.where(l == layer, i, 0), 0))

    sh = jax.ShapeDtypeStruct(w.shape, F32)
    return _pcall(
        body, name=name, grid=(2, rows // tr),
        in_specs=[blk] * 3 + [part_spec(0)] * npart + [part_spec(1)] * npart, out_specs=[blk] * 4,
        out_shape=[sh] * 4)(w, m, v, *parts[0], *parts[1])


_PACK_W = 1024


def _pack(arrs, rows):
    flat = []
    for a in arrs:
        f = a.reshape(-1).astype(F32)
        pad = (-f.shape[0]) % _PACK_W
        flat.append(jnp.pad(f, (0, pad)))
    f = jnp.concatenate(flat)
    f = jnp.pad(f, (0, rows * _PACK_W - f.shape[0]))
    return f.reshape(rows, _PACK_W)


def _unpack(buf, shapes):
    flat = buf.reshape(-1)
    out, pos = [], 0
    for sh in shapes:
        size = math.prod(sh)
        out.append(flat[pos:pos + size].reshape(sh))
        pos += size + ((-size) % _PACK_W)
    return out


def _pack_rows(shapes):
    total = sum(-(-math.prod(sh) // _PACK_W) for sh in shapes)
    return -(-total // 8) * 8


def kernel(x, rel_bias, ab_norm, ab_w_in, ab_conv_w, ab_conv_b, ab_ln_g, ab_ln_b, ab_w_out, sc_norm, sc_w_in, sc_conv_w, sc_w_out, mlp_norm, mlp_w_up, mlp_w_down, final_norm, loss_target, m_rel_bias, m_ab_norm, m_ab_w_in, m_ab_conv_w, m_ab_conv_b, m_ab_ln_g, m_ab_ln_b, m_ab_w_out, m_sc_norm, m_sc_w_in, m_sc_conv_w, m_sc_w_out, m_mlp_norm, m_mlp_w_up, m_mlp_w_down, m_final_norm, v_rel_bias, v_ab_norm, v_ab_w_in, v_ab_conv_w, v_ab_conv_b, v_ab_ln_g, v_ab_ln_b, v_ab_w_out, v_sc_norm, v_sc_w_in, v_sc_conv_w, v_sc_w_out, v_mlp_norm, v_mlp_w_up, v_mlp_w_down, v_final_norm):
    s, d = x.shape[1], x.shape[2]
    dff = 4 * d
    c = _GROUP_COLS
    chip = 2 * lax.axis_index("x") + lax.axis_index("y")
    on_c0 = (lax.axis_index("c") == 0).astype(F32)
    h0 = x[0]
    tgt = loss_target[0]

    cw_sh, scn_sh, scw_sh = ab_conv_w[0], sc_norm, sc_conv_w[0]
    conv_w_full = lax.dynamic_update_slice(jnp.zeros((_CONV_K, c), F32), cw_sh * on_c0, (0, chip * cw_sh.shape[1]))
    scn_full = lax.dynamic_update_slice(jnp.zeros((1, d), F32), scn_sh * on_c0, (0, chip * scn_sh.shape[1]))
    scw_full = lax.dynamic_update_slice(jnp.zeros((3, d), F32), scw_sh * on_c0, (0, chip * scw_sh.shape[1]))
    small_shapes = [(_CONV_K, c), (1, d), (3, d)]
    small = _sum_all_devices(_pack([conv_w_full, scn_full, scw_full], _pack_rows(small_shapes)), "gather_small")
    conv_w, sc_g, sc_cw = _unpack(small, small_shapes)

    w_shards = [ab_w_in[0], ab_w_out[0], sc_w_in[0], sc_w_out[0], mlp_w_up[0], mlp_w_up[1],
                mlp_w_down[0], mlp_w_down[1]]
    w_axes = [1, 0, 1, 0, 1, 1, 0, 0]
    wb = [w.astype(BF16) for w in w_shards]
    full_w = [None] * 8

    def gather(idx):
        return _join_riders([_gather_halves_rider(wb[t], w_axes[t]) for t in idx])

    def put(idx, got_w):
        for t, w in zip(idx, got_w):
            full_w[t] = w

    buckets = _bucket_maps()
    bias = _bias_expand(rel_bias, buckets)
    n0, got_w = _rms_fwd(h0, ab_norm, "rms_fwd_ab", rider=_gather_halves_rider(wb[0], w_axes[0]))
    put([0], got_w)
    w_in = full_w[0]
    tm = min(1024, s)
    tm2 = min(2048, s)
    tmh = min(512, s)
    uc = _mm(n0, w_in, "nn", m=s, n=2 * c, k=d, tm=tm2, tn=2 * c, tk=d, out_dtype=BF16, name="proj_conv")
    uq, uk, uv = [], [], []
    for t, (nm, dst) in enumerate(zip("qkv", (uq, uk, uv))):
        res = _mm(n0, w_in, "nn", m=s, n=3 * c, k=d, tm=tm2, tn=c, tk=d, out_dtype=F32, name=f"proj_{nm}",
                  b_off=(0, 2 + 3 * t), split="o", rider=gather([1]) if t == 0 else None)
        if t == 0:
            res, got_w = res
            put([1], got_w)
        dst.append(res)
    uq, uk, uv = uq[0], uk[0], uv[0]
    (cat, ca), got_w = _conv_a_fwd(uc, conv_w, ab_conv_b, ab_ln_g, ab_ln_b, rider=gather([2]))
    put([2], got_w)
    outs, lses = [], []
    for g, (_, dil) in enumerate(_GROUPS):
        idx = ([4], [6], [3, 5])[g]
        (o, l), got_w = _attn_fwd(uq, uk, uv, bias, g, dil, 4 if dil <= 4 else 2, rider=gather(idx))
        put(idx, got_w)
        outs.append(o)
        lses.append(l)
    cat, lse = _attn_merge(outs, lses, cat)
    h1, n1 = _mm(cat, full_w[1], "nn", m=s, n=d, k=d, tm=tm, tn=d, tk=d, out_dtype=(F32, BF16), name="out_ab",
                 epi=_epi_add_rms, extras=(h0,), vecs=(mlp_norm[0:1],))

    def mlp_fwd(h, nrm, layer, next_gain=None, rider=None):
        zr = _mm(nrm, full_w[4 + layer], "nn", m=s, n=dff, k=d, tm=tmh, tn=dff, tk=d, out_dtype=BF16,
                 name=f"mlp_up{layer}", epi=_epi_relu, rider=rider)
        if rider is not None:
            zr, got_r = zr
            put([7], got_r)
        kw = dict(m=s, n=d, k=dff, tm=tmh, tn=d, tk=dff, name=f"mlp_down{layer}", a_pro=_square, extras=(h,))
        if next_gain is None:
            return zr, _mm(zr, full_w[6 + layer], "nn", out_dtype=F32, epi=_epi_add, **kw), None
        hn, nn = _mm(zr, full_w[6 + layer], "nn", out_dtype=(F32, BF16), epi=_epi_add_rms, vecs=(next_gain,), **kw)
        return zr, hn, nn

    zr0, h2, n2 = mlp_fwd(h1, n1, 0, next_gain=sc_g, rider=gather([7]))
    _, w_out, w_si, w_so, w_up0, w_up1, w_dn0, w_dn1 = full_w
    w_up, w_dn = [w_up0, w_up1], [w_dn0, w_dn1]
    u2 = _mm(n2, w_si, "nn", m=s, n=3 * d, k=d, tm=tmh, tn=3 * d, tk=d, out_dtype=BF16, name="proj_sc")
    scv = _short_conv_fwd(u2, sc_cw)
    h3, n3 = _mm(scv, w_so, "nn", m=s, n=d, k=d, tm=tm, tn=d, tk=d, out_dtype=(F32, BF16), name="out_sc",
                 epi=_epi_add_rms, extras=(h2,), vecs=(mlp_norm[1:2],))
    zr1, h4, _ = mlp_fwd(h3, n3, 1)

    dh4, dh4b, g_final, loss_part = _loss_head(h4, tgt, final_norm.reshape(1, d))
    tkw = min(2048, s)

    big_grads, got, sums = [None] * 8, [None] * 8, [None] * 8

    def scatter(t):
        return _scatter_rider([big_grads[t]], [w_axes[t]])

    def own_slab(t):
        width = big_grads[t].shape[w_axes[t]] // 4
        return lax.dynamic_slice_in_dim(big_grads[t], chip * width, width, axis=w_axes[t])

    def arrived(t, got_t):
        got[t] = got_t[0]
        sums[t] = _sum_partials(own_slab(t), got[t], f"sum_partials{t}")

    def mlp_bwd(dh, dhb, h, nrm, zr, layer):
        dz = _mm(dhb, w_dn[layer], "nt", m=s, n=dff, k=d, tm=tmh, tn=dff, tk=d, out_dtype=BF16,
                 name=f"mlp_down{layer}_dx", epi=_epi_relu_sq_bwd, extras=(zr,))
        big_grads[6 + layer] = _mm(zr, dhb, "tn", m=dff, n=d, k=s, tm=1024, tn=d, tk=tkw, out_dtype=BF16,
                                   name=f"mlp_down{layer}_dw", a_pro=_square)
        big_grads[4 + layer] = _mm(nrm, dz, "tn", m=d, n=dff, k=s, tm=d, tn=1024, tk=tkw, out_dtype=BF16,
                                   name=f"mlp_up{layer}_dw")
        res, got_t = _mm(dz, w_up[layer], "nt", m=s, n=d, k=dff, tm=tmh, tn=d, tk=dff, out_dtype=(F32, BF16),
                         name=f"mlp_up{layer}_dx", rider=scatter(6 + layer), epi=_epi_rms_bwd,
                         extras=(h, dh), vecs=(mlp_norm[layer:layer + 1],), row_sum=True)
        arrived(6 + layer, got_t)
        return res

    dh3, dh3b, g_mn1 = mlp_bwd(dh4, dh4b, h3, n3, zr1, 1)

    dsc = _mm(dh3b, w_so, "nt", m=s, n=d, k=d, tm=tm, tn=d, tk=d, out_dtype=F32, name="out_sc_dx")
    big_grads[3] = _mm(scv, dh3b, "tn", m=d, n=d, k=s, tm=d, tn=d, tk=tkw, out_dtype=BF16, name="out_sc_dw")
    (du2, g_sccw8), got_t = _short_conv_bwd(u2, dsc, sc_cw, rider=scatter(5))
    arrived(5, got_t)
    big_grads[2], got_t = _mm(n2, du2, "tn", m=d, n=3 * d, k=s, tm=d, tn=1024, tk=tkw, out_dtype=BF16,
                              name="proj_sc_dw", rider=scatter(3))
    arrived(3, got_t)
    (dh2, dh2b, g_scn), got_t = _mm(
        du2, w_si, "nt", m=s, n=d, k=3 * d, tm=tmh, tn=d, tk=3 * d, out_dtype=(F32, BF16), name="proj_sc_dx",
        rider=scatter(2), epi=_epi_rms_bwd, extras=(h2, dh3), vecs=(sc_g,), row_sum=True)
    arrived(2, got_t)

    dh1, dh1b, g_mn0 = mlp_bwd(dh2, dh2b, h1, n1, zr0, 0)

    dcat = _mm(dh1b, w_out, "nt", m=s, n=d, k=d, tm=tm, tn=d, tk=d, out_dtype=F32, name="out_ab_dx")
    big_grads[1] = _mm(cat, dh1b, "tn", m=d, n=d, k=s, tm=d, tn=d, tk=tkw, out_dtype=BF16, name="out_ab_dw")
    (dca, conv_stats), got_t = _conv_a_bwd_ln(ca, dcat, ab_ln_g, ab_ln_b, rider=scatter(1))
    arrived(1, got_t)
    (duc, g_cw32), got_t = _conv_a_bwd_conv(uc, dca, conv_w, rider=scatter(4))
    arrived(4, got_t)
    delta, dys = _attn_delta(dcat, cat)

    grads_qkv, ds_list = [], []
    for g, (_, dil) in enumerate(_GROUPS):
        dq, dk, dv, dsa = _attn_bwd(uq, uk, uv, dys, lse, delta, bias, grads_qkv, g, dil, 4 if dil <= 4 else 1)
        grads_qkv = [dq, dk, dv]
        ds_list.append(dsa)
    g_bias = _bias_reduce(jnp.concatenate(ds_list, axis=0), buckets)[:, :, 0].T

    secs = [(duc, 2 * c, 0)] + [(grads_qkv[t], 3 * c, 2 + 3 * t) for t in range(3)]
    g_in_parts = []
    for t, (du, width, off) in enumerate(secs):
        rider = _swap_rider(sums[1:]) if t == 1 else None
        part = _mm(n0, du, "tn", m=d, n=width, k=s, tm=d, tn=width, tk=min(1024, s) if t else tkw, out_dtype=BF16,
                   name=f"proj_ab_dw{t}", rider=rider, split="b" if t else "")
        if rider is not None:
            part, sib_late = part
        g_in_parts.append(part)
    big_grads[0] = jnp.concatenate(g_in_parts, axis=1)
    dn0 = _mm(duc, w_in, "nt", m=s, n=d, k=2 * c, tm=tm, tn=d, tk=2 * c, out_dtype=F32, name="proj_ab_dx_conv")
    dn0, (got[0],) = _dx_qkv(*grads_qkv, w_in, dn0, tm=tm, tk=c, col0=2 * c, name="proj_ab_dx_qkv",
                             rider=scatter(0))
    grad_x, _, g_abn = _rms_bwd(dn0, h0, ab_norm, dh1, "rms_bwd_ab")
    sums[0] = _sum_partials(own_slab(0), got[0], "sum_partials0")
    sib = list(_run_rider(_swap_rider([sums[0]]), "swap_sibling_w_in")) + sib_late

    upd = [_adamw(w_shards[t], mm[0], vv[0], [sums[t], sib[t]], f"adamw{t}")
           for t, (mm, vv) in enumerate(((m_ab_w_in, v_ab_w_in), (m_ab_w_out, v_ab_w_out),
                                         (m_sc_w_in, v_sc_w_in), (m_sc_w_out, v_sc_w_out)))]
    upd_up = _adamw_layers(mlp_w_up, m_mlp_w_up, v_mlp_w_up, [[sums[4], sib[4]], [sums[5], sib[5]]], "adamw_up")
    upd_dn = _adamw_layers(mlp_w_down, m_mlp_w_down, v_mlp_w_down, [[sums[6], sib[6]], [sums[7], sib[7]]],
                           "adamw_down")

    full_shapes = [(_NUM_BUCKETS, rel_bias.shape[1]), (1, d), (_CONV_K, c), (1, c), (1, c), (1, c), (1, d),
                   (3, d), (2, d), (d,), (1, 1)]
    small_grads = [g_bias, g_abn, g_cw32[:_CONV_K], conv_stats[0:1], conv_stats[1:2], conv_stats[2:3], g_scn,
                   g_sccw8[:3], jnp.concatenate([g_mn0, g_mn1], axis=0), g_final.reshape(d), loss_part[0:1, 0:1]]
    tot = _unpack(_sum_all_devices(_pack(small_grads, _pack_rows(full_shapes)), "sum_small"), full_shapes)
    loss = tot.pop()[0, 0]
    for idx, sh in ((2, cw_sh), (6, scn_sh), (7, scw_sh)):
        width = sh.shape[1]
        tot[idx] = lax.dynamic_slice_in_dim(tot[idx], chip * width, width, axis=1)
    sm_w = [rel_bias, ab_norm, cw_sh, ab_conv_b, ab_ln_g, ab_ln_b, scn_sh, scw_sh, mlp_norm, final_norm]
    sm_m = [m_rel_bias, m_ab_norm, m_ab_conv_w[0], m_ab_conv_b, m_ab_ln_g, m_ab_ln_b, m_sc_norm, m_sc_conv_w[0],
            m_mlp_norm, m_final_norm]
    sm_v = [v_rel_bias, v_ab_norm, v_ab_conv_w[0], v_ab_conv_b, v_ab_ln_g, v_ab_ln_b, v_sc_norm, v_sc_conv_w[0],
            v_mlp_norm, v_final_norm]
    sh_shapes = [tuple(t.shape) for t in tot]
    rows = _pack_rows(sh_shapes)
    sm_upd = _adamw(_pack(sm_w, rows), _pack(sm_m, rows), _pack(sm_v, rows), [_pack(tot, rows)], "adamw_small")
    sm_g, sm_d, sm_nm, sm_nv = [_unpack(buf, sh_shapes) for buf in sm_upd]

    def assemble(kind, sm):
        big = [u[kind] for u in upd]
        return [sm[0], sm[1], big[0][None], sm[2][None], sm[3], sm[4], sm[5], big[1][None], sm[6], big[2][None],
                sm[7][None], big[3][None], sm[8], upd_up[kind], upd_dn[kind], sm[9]]

    res = [loss, grad_x[None]]
    for kind, sm in enumerate((sm_g, sm_d, sm_nm, sm_nv)):
        res += assemble(kind, sm)
    return tuple(res)
```

```python
import functools
import math

import jax
import jax.numpy as jnp
from jax import lax
from jax.experimental import pallas as pl
from jax.experimental.pallas import tpu as pltpu

F32 = jnp.float32
BF16 = jnp.bfloat16
MESH = pl.DeviceIdType.MESH

_GROUPS = ((128, 1), (512, 4), (2048, 16))
_STEPS = 128
_HEAD_DIM = 64
_GROUP_COLS = 512
_NUM_BUCKETS = 32
_MAX_DISTANCE = 2048
_CONV_K = 31
_HALO = 32
_SC_HALO = 16
_RMS_EPS = 1e-6
_LN_EPS = 1e-5
_NEG = -1e30
_LANES = 128
_VMEM_LIMIT = 56 * 1024 * 1024

_LR, _B1, _B2, _EPS, _WD, _STEP = 0.001, 0.9, 0.999, 1e-08, 0.01, 10


class _Rider:
    def __init__(self, ins, out_shapes, scratch, start, finish):
        self.ins, self.out_shapes, self.scratch = list(ins), list(out_shapes), list(scratch)
        self.start, self.finish = start, finish


def _pcall(body, *, name, out_shape, in_specs, out_specs, grid=None, scratch=(), aliases=None, rider=None):
    kw = {} if grid is None else {"grid": grid}
    cparams = pltpu.CompilerParams(vmem_limit_bytes=_VMEM_LIMIT)
    if rider is None:
        return pl.pallas_call(
            body, name=name, out_shape=out_shape, in_specs=in_specs, out_specs=out_specs,
            scratch_shapes=list(scratch), input_output_aliases=aliases or {},
            compiler_params=cparams, **kw)
    single = not isinstance(out_specs, (list, tuple))
    ospecs = [out_specs] if single else list(out_specs)
    oshapes = [out_shape] if single else list(out_shape)
    nin, nout, nscr = len(in_specs), len(ospecs), len(scratch)
    rin, rout = len(rider.ins), len(rider.out_shapes)

    def wrapped(*refs):
        h_in, r_in = refs[:nin], refs[nin:nin + rin]
        p = nin + rin
        h_out, r_out = refs[p:p + nout], refs[p + nout:p + nout + rout]
        p += nout + rout
        h_scr, r_scr = refs[p:p + nscr], refs[p + nscr:]
        ids = [pl.program_id(a) for a in range(len(grid))]
        first = functools.reduce(jnp.logical_and, [i == 0 for i in ids])
        last = functools.reduce(jnp.logical_and, [i == g - 1 for i, g in zip(ids, grid)])

        @pl.when(first)
        def _():
            rider.start(r_in, r_out, r_scr)

        body(*h_in, *h_out, *h_scr)

        @pl.when(last)
        def _():
            rider.finish(r_in, r_out, r_scr)

    call = pl.pallas_call(
        wrapped, name=name, out_shape=oshapes + rider.out_shapes,
        in_specs=list(in_specs) + [_ANY] * rin, out_specs=ospecs + [_ANY] * rout,
        scratch_shapes=list(scratch) + rider.scratch, input_output_aliases=aliases or {},
        compiler_params=cparams, **kw)

    def run(*operands):
        res = call(*operands, *rider.ins)
        host = res[0] if single else list(res[:nout])
        return host, list(res[nout:])

    return run


def _sig(x):
    return 1.0 / (1.0 + jnp.exp(-x))


_ANY = pl.BlockSpec(memory_space=pl.ANY)


def _lanes_of(ref):
    parts = [ref[p] for p in range(ref.shape[0])]
    return parts[0] if len(parts) == 1 else jnp.concatenate(parts, axis=1)


def _mm(a, b, mode, *, m, n, k, tm, tn, tk, out_dtype, name, epi=None, extras=(), b_off=(0, 0), rider=None,
        split="", vecs=(), a_pro=None, row_sum=False):
    nk = k // tk
    assert m % tm == 0 and n % tn == 0 and k % tk == 0
    o0, o1 = b_off
    if mode == "nn":
        a_spec = pl.BlockSpec((tm, tk), lambda i, j, kk: (i, kk))
        b_spec = pl.BlockSpec((tk, tn), lambda i, j, kk: (kk + o0, j + o1))
        dn = (((1,), (0,)), ((), ()))
    elif mode == "nt":
        a_spec = pl.BlockSpec((tm, tk), lambda i, j, kk: (i, kk))
        if "a" in split:
            a_spec = pl.BlockSpec((tk // _LANES, tm, _LANES), lambda i, j, kk: (kk, i, 0))
        b_spec = pl.BlockSpec((tn, tk), lambda i, j, kk: (j + o0, kk + o1))
        dn = (((1,), (1,)), ((), ()))
    else:
        a_spec = pl.BlockSpec((tk, tm), lambda i, j, kk: (kk, i))
        b_spec = pl.BlockSpec((tk, tn), lambda i, j, kk: (kk + o0, j + o1))
        if "b" in split:
            b_spec = pl.BlockSpec((tn // _LANES, tk, _LANES), lambda i, j, kk: (j, kk, 0))
        dn = (((0,), (0,)), ((), ()))
    o_spec = pl.BlockSpec((tm, tn), lambda i, j, kk: (i, j))
    e_spec = o_spec
    if "o" in split:
        o_spec = pl.BlockSpec((tn // _LANES, tm, _LANES), lambda i, j, kk: (j, i, 0))
    v_spec = pl.BlockSpec((1, tn), lambda i, j, kk: (0, j))
    ne = len(extras) + len(vecs)
    multi = isinstance(out_dtype, tuple)
    dts = out_dtype if multi else (out_dtype,)
    no = len(dts)
    nr = 1 if row_sum else 0
    assert not row_sum or tn == n

    def body(a_ref, b_ref, *rest):
        ex, o_refs = rest[:ne], rest[ne:ne + no]
        av = _lanes_of(a_ref) if "a" in split else a_ref[...]
        bv = _lanes_of(b_ref) if "b" in split else b_ref[...]
        if av.dtype != BF16:
            av = av.astype(BF16)
        if bv.dtype != BF16:
            bv = bv.astype(BF16)
        if a_pro is not None:
            av = a_pro(av)
        p = lax.dot_general(av, bv, dn, preferred_element_type=F32)

        def fin(x):
            if epi is not None:
                x = epi(x, *[e[...] for e in ex])
            if row_sum:
                row, x = x[-1], (x[:-1] if multi else x[0])
                row_ref = rest[ne + no]

                @pl.when(pl.program_id(0) == 0)
                def _():
                    row_ref[...] = row

                @pl.when(pl.program_id(0) > 0)
                def _():
                    row_ref[...] += row

            for o_ref, val, dt in zip(o_refs, x if multi else (x,), dts):
                if "o" in split:
                    for p in range(tn // _LANES):
                        o_ref[p] = val[:, p * _LANES:(p + 1) * _LANES].astype(dt)
                else:
                    o_ref[...] = val.astype(dt)

        if nk == 1:
            fin(p)
        else:
            acc = rest[ne + no + nr]
            kk = pl.program_id(2)

            @pl.when(kk == 0)
            def _():
                acc[...] = p

            @pl.when(kk > 0)
            def _():
                acc[...] += p

            @pl.when(kk == nk - 1)
            def _():
                fin(acc[...])

    oshape = (n // _LANES, m, _LANES) if "o" in split else (m, n)
    shapes = [jax.ShapeDtypeStruct(oshape, dt) for dt in dts]
    ospecs = [o_spec] * no
    if row_sum:
        shapes.append(jax.ShapeDtypeStruct((1, n), F32))
        ospecs.append(v_spec)
    lone = not multi and not row_sum
    return _pcall(
        body, name=name, grid=(m // tm, n // tn, nk),
        in_specs=[a_spec, b_spec] + [e_spec] * len(extras) + [v_spec] * len(vecs),
        out_specs=ospecs[0] if lone else ospecs, out_shape=shapes[0] if lone else shapes,
        scratch=[pltpu.VMEM((tm, tn), F32)] if nk > 1 else [], rider=rider,
    )(a, b, *extras, *vecs)


def _dx_qkv(dq, dk, dv, w, prev, *, tm, tk, col0, name, rider=None):
    npair, s, _ = dq.shape
    n = w.shape[0]
    per = npair * _LANES // tk
    nk = 3 * per
    off = col0 // tk
    nt_dims = (((1,), (1,)), ((), ()))

    def a_spec(t):
        return pl.BlockSpec((tk // _LANES, tm, _LANES), lambda i, kk: (jnp.clip(kk - per * t, 0, per - 1), i, 0))

    def body(a0, a1, a2, b_ref, e_ref, o_ref, acc):
        kk = pl.program_id(1)
        bv = b_ref[...]

        @pl.when(kk == 0)
        def _():
            acc[...] = e_ref[...]

        for t, a_ref in enumerate((a0, a1, a2)):
            @pl.when(jnp.logical_and(kk >= per * t, kk < per * (t + 1)))
            def _():
                acc[...] += lax.dot_general(_lanes_of(a_ref).astype(BF16), bv, nt_dims, preferred_element_type=F32)

        @pl.when(kk == nk - 1)
        def _():
            o_ref[...] = acc[...]

    row = pl.BlockSpec((tm, n), lambda i, kk: (i, 0))
    return _pcall(
        body, name=name, grid=(s // tm, nk),
        in_specs=[a_spec(0), a_spec(1), a_spec(2), pl.BlockSpec((n, tk), lambda i, kk: (0, kk + off)), row],
        out_specs=row, out_shape=jax.ShapeDtypeStruct((s, n), F32),
        scratch=[pltpu.VMEM((tm, n), F32)], rider=rider)(dq, dk, dv, w, prev)


def _epi_add(x, r):
    return x + r


def _epi_relu(x):
    return jnp.maximum(x, 0.0)


def _square(x):
    return x * x


def _epi_relu_sq_bwd(da, zr):
    return da * (2.0 * zr.astype(F32))


def _epi_add_rms(x, r, g):
    h = x + r
    return h, h * lax.rsqrt(jnp.mean(h * h, axis=-1, keepdims=True) + _RMS_EPS) * g


def _epi_rms_bwd(dn, h, dh_in, g):
    dx, dg = _rms_bwd_math(dn, h, g)
    dh = dh_in + dx
    return dh, dh, dg


_ROW_T = 512


def _rms_fwd(h, g, name, rider=None):
    s, d = h.shape

    def body(h_ref, g_ref, o_ref):
        x = h_ref[...]
        r = lax.rsqrt(jnp.mean(x * x, axis=-1, keepdims=True) + _RMS_EPS)
        o_ref[...] = (x * r * g_ref[...]).astype(BF16)

    row = pl.BlockSpec((_ROW_T, d), lambda i: (i, 0))
    vec = pl.BlockSpec((1, d), lambda i: (0, 0))
    return _pcall(body, name=name, grid=(s // _ROW_T,), in_specs=[row, vec], out_specs=row,
                  out_shape=jax.ShapeDtypeStruct((s, d), BF16), rider=rider)(h, g)


def _rms_bwd_math(dn, x, g):
    r = lax.rsqrt(jnp.mean(x * x, axis=-1, keepdims=True) + _RMS_EPS)
    xhat = x * r
    dg = jnp.sum(dn * xhat, axis=0, keepdims=True)
    t = dn * g
    dx = r * (t - xhat * jnp.mean(t * xhat, axis=-1, keepdims=True))
    return dx, dg


def _rms_bwd(dn, h, g, dh_in, name, rider=None):
    s, d = h.shape

    def body(dn_ref, h_ref, g_ref, dhi_ref, dh_ref, dhb_ref, dg_ref):
        dx, dg = _rms_bwd_math(dn_ref[...], h_ref[...], g_ref[...])
        dh = dhi_ref[...] + dx
        dh_ref[...] = dh
        dhb_ref[...] = dh.astype(BF16)

        @pl.when(pl.program_id(0) == 0)
        def _():
            dg_ref[...] = jnp.zeros_like(dg_ref)

        dg_ref[...] += dg

    row = pl.BlockSpec((_ROW_T, d), lambda i: (i, 0))
    vec = pl.BlockSpec((1, d), lambda i: (0, 0))
    return _pcall(
        body, name=name, grid=(s // _ROW_T,), in_specs=[row, row, vec, row], out_specs=[row, row, vec],
        out_shape=[jax.ShapeDtypeStruct((s, d), F32), jax.ShapeDtypeStruct((s, d), BF16),
                   jax.ShapeDtypeStruct((1, d), F32)], rider=rider)(dn, h, g, dh_in)


def _loss_head(h, tgt, g):
    s, d = h.shape

    def body(h_ref, t_ref, g_ref, dh_ref, dhb_ref, dg_ref, loss_ref):
        x, gv = h_ref[...], g_ref[...]
        r = lax.rsqrt(jnp.mean(x * x, axis=-1, keepdims=True) + _RMS_EPS)
        err = x * r * gv - t_ref[...]
        part = 0.5 * jnp.sum(jnp.mean(err * err, axis=-1, keepdims=True))
        dx, dg = _rms_bwd_math(err * (1.0 / d), x, gv)
        dh_ref[...] = dx
        dhb_ref[...] = dx.astype(BF16)

        @pl.when(pl.program_id(0) == 0)
        def _():
            dg_ref[...] = jnp.zeros_like(dg_ref)
            loss_ref[...] = jnp.zeros_like(loss_ref)

        dg_ref[...] += dg
        loss_ref[...] += jnp.full(loss_ref.shape, part, F32)

    row = pl.BlockSpec((_ROW_T, d), lambda i: (i, 0))
    vec = pl.BlockSpec((1, d), lambda i: (0, 0))
    one = pl.BlockSpec((1, _LANES), lambda i: (0, 0))
    return _pcall(
        body, name="loss_head", grid=(s // _ROW_T,), in_specs=[row, row, vec], out_specs=[row, row, vec, one],
        out_shape=[jax.ShapeDtypeStruct((s, d), F32), jax.ShapeDtypeStruct((s, d), BF16),
                   jax.ShapeDtypeStruct((1, d), F32), jax.ShapeDtypeStruct((1, _LANES), F32)])(h, tgt, g)


_CONV_T = 256
_CONV_RC = 64


def _conv_a_specs(s):
    c = _GROUP_COLS
    hb = _CONV_T // _HALO
    val = pl.BlockSpec((_CONV_T, c), lambda i: (i, 0))
    gate = pl.BlockSpec((_CONV_T, c), lambda i: (i, 1))
    hval = pl.BlockSpec((_HALO, c), lambda i: (jnp.maximum(i * hb - 1, 0), 0))
    hgate = pl.BlockSpec((_HALO, c), lambda i: (jnp.maximum(i * hb - 1, 0), 1))
    return val, gate, hval, hgate


def _fill_glu(val_ref, gate_ref, hval_ref, hgate_ref, hs_ref):
    i = pl.program_id(0)
    hs_ref[pl.ds(_HALO, _CONV_T), :] = val_ref[...].astype(F32) * _sig(gate_ref[...].astype(F32))
    halo = hval_ref[...].astype(F32) * _sig(hgate_ref[...].astype(F32))
    hs_ref[pl.ds(0, _HALO), :] = jnp.where(i > 0, halo, 0.0)


_SHIFT_ROWS = _CONV_T + _HALO - 8


def _fill_shifts(src_ref, sh_ref):
    for b in range(1, 8):
        sh_ref[b - 1] = src_ref[pl.ds(b, _SHIFT_ROWS), :]


def _tap_rows(src_ref, sh_ref, start, rows, lanes=slice(None)):
    b = start % 8
    if b == 0:
        return src_ref[pl.ds(start, rows), lanes]
    return sh_ref[b - 1, pl.ds(start - b, rows), lanes]


def _conv_rows(hs_ref, sh_ref, w_ref, r0, rows):
    off = _HALO - (_CONV_K - 1)
    acc = jnp.zeros((rows, _GROUP_COLS), F32)
    for kk in range(_CONV_K):
        acc = acc + w_ref[kk:kk + 1, :] * _tap_rows(hs_ref, sh_ref, r0 + off + kk, rows)
    return acc


def _ln_fwd(ca, g, b):
    mu = jnp.mean(ca, axis=-1, keepdims=True)
    xc = ca - mu
    rstd = lax.rsqrt(jnp.mean(xc * xc, axis=-1, keepdims=True) + _LN_EPS)
    xhat = xc * rstd
    return xhat, rstd, xhat * g + b


def _conv_a_fwd(uc, w, cb, lg, lb, rider=None):
    s = uc.shape[0]
    c = _GROUP_COLS

    def body(val_ref, gate_ref, hval_ref, hgate_ref, w_ref, cb_ref, lg_ref, lb_ref, o_ref, ca_ref, hs_ref, sh_ref):
        _fill_glu(val_ref, gate_ref, hval_ref, hgate_ref, hs_ref)
        _fill_shifts(hs_ref, sh_ref)
        for rc in range(_CONV_T // _CONV_RC):
            r0 = rc * _CONV_RC
            ca = _conv_rows(hs_ref, sh_ref, w_ref, r0, _CONV_RC) + cb_ref[...]
            ca_ref[pl.ds(r0, _CONV_RC), :] = ca
            _, _, ln = _ln_fwd(ca, lg_ref[...], lb_ref[...])
            o_ref[pl.ds(r0, _CONV_RC), :] = (ln * _sig(ln)).astype(BF16)

    val, gate, hval, hgate = _conv_a_specs(s)
    wspec = pl.BlockSpec((_CONV_K, c), lambda i: (0, 0))
    vec = pl.BlockSpec((1, c), lambda i: (0, 0))
    blk = pl.BlockSpec((_CONV_T, c), lambda i: (i, 0))
    return _pcall(
        body, name="conv_a_fwd", grid=(s // _CONV_T,),
        in_specs=[val, gate, hval, hgate, wspec, vec, vec, vec],
        out_specs=[blk, blk],
        out_shape=[jax.ShapeDtypeStruct((s, 2 * c), BF16), jax.ShapeDtypeStruct((s, c), F32)],
        scratch=[pltpu.VMEM((_CONV_T + _HALO, c), F32), pltpu.VMEM((7, _SHIFT_ROWS, c), F32)],
        rider=rider)(uc, uc, uc, uc, w, cb, lg, lb)


def _conv_a_bwd_ln(ca_all, dcat, lg, lb, rider=None):
    s = ca_all.shape[0]
    c = _GROUP_COLS

    def body(ca_ref, dy_ref, lg_ref, lb_ref, dca_ref, st_ref):
        @pl.when(pl.program_id(0) == 0)
        def _():
            st_ref[...] = jnp.zeros_like(st_ref)

        for rc in range(_CONV_T // _CONV_RC):
            r0 = rc * _CONV_RC
            ca = ca_ref[pl.ds(r0, _CONV_RC), :]
            xhat, rstd, ln = _ln_fwd(ca, lg_ref[...], lb_ref[...])
            sg = _sig(ln)
            dln = dy_ref[pl.ds(r0, _CONV_RC), :] * (sg * (1.0 + ln * (1.0 - sg)))
            dxh = dln * lg_ref[...]
            dca = rstd * (dxh - jnp.mean(dxh, axis=-1, keepdims=True)
                          - xhat * jnp.mean(dxh * xhat, axis=-1, keepdims=True))
            dca_ref[pl.ds(r0, _CONV_RC), :] = dca
            st_ref[0:1, :] += jnp.sum(dca, axis=0, keepdims=True)
            st_ref[1:2, :] += jnp.sum(dln * xhat, axis=0, keepdims=True)
            st_ref[2:3, :] += jnp.sum(dln, axis=0, keepdims=True)

    blk = pl.BlockSpec((_CONV_T, c), lambda i: (i, 0))
    vec = pl.BlockSpec((1, c), lambda i: (0, 0))
    st = pl.BlockSpec((8, c), lambda i: (0, 0))
    return _pcall(
        body, name="conv_a_bwd_ln", grid=(s // _CONV_T,),
        in_specs=[blk, blk, vec, vec], out_specs=[blk, st],
        out_shape=[jax.ShapeDtypeStruct((s, c), F32), jax.ShapeDtypeStruct((8, c), F32)],
        rider=rider)(ca_all, dcat, lg, lb)


def _conv_a_bwd_conv(uc, dca, w, rider=None):
    s = uc.shape[0]
    c = _GROUP_COLS
    nblk = s // _CONV_T
    hb = _CONV_T // _HALO
    off = _HALO - (_CONV_K - 1)

    def body(val_ref, gate_ref, hval_ref, hgate_ref, d_ref, dn_ref, w_ref, du_ref, dw_ref, hs_ref, ds_ref,
             hsh_ref, dsh_ref):
        i = pl.program_id(0)
        _fill_glu(val_ref, gate_ref, hval_ref, hgate_ref, hs_ref)
        ds_ref[pl.ds(0, _CONV_T), :] = d_ref[...]
        ds_ref[pl.ds(_CONV_T, _HALO), :] = jnp.where(i < nblk - 1, dn_ref[...], 0.0)
        _fill_shifts(hs_ref, hsh_ref)
        _fill_shifts(ds_ref, dsh_ref)

        @pl.when(i == 0)
        def _():
            dw_ref[...] = jnp.zeros_like(dw_ref)

        for rc in range(_CONV_T // _CONV_RC):
            r0 = rc * _CONV_RC
            dcur = ds_ref[pl.ds(r0, _CONV_RC), :]
            dh = jnp.zeros((_CONV_RC, c), F32)
            for kk in range(_CONV_K):
                dh = dh + w_ref[kk:kk + 1, :] * _tap_rows(ds_ref, dsh_ref, r0 + _CONV_K - 1 - kk, _CONV_RC)
                dw_ref[kk:kk + 1, :] += jnp.sum(dcur * _tap_rows(hs_ref, hsh_ref, r0 + off + kk, _CONV_RC),
                                                 axis=0, keepdims=True)
            v = val_ref[pl.ds(r0, _CONV_RC), :].astype(F32)
            sg = _sig(gate_ref[pl.ds(r0, _CONV_RC), :].astype(F32))
            du_ref[pl.ds(r0, _CONV_RC), pl.ds(0, c)] = (dh * sg).astype(BF16)
            du_ref[pl.ds(r0, _CONV_RC), pl.ds(c, c)] = (dh * v * sg * (1.0 - sg)).astype(BF16)

    val, gate, hval, hgate = _conv_a_specs(s)
    blk = pl.BlockSpec((_CONV_T, c), lambda i: (i, 0))
    nxt = pl.BlockSpec((_HALO, c), lambda i: (jnp.minimum((i + 1) * hb, s // _HALO - 1), 0))
    wspec = pl.BlockSpec((_CONV_K, c), lambda i: (0, 0))
    return _pcall(
        body, name="conv_a_bwd_conv", grid=(nblk,),
        in_specs=[val, gate, hval, hgate, blk, nxt, wspec],
        out_specs=[pl.BlockSpec((_CONV_T, 2 * c), lambda i: (i, 0)), pl.BlockSpec((_HALO, c), lambda i: (0, 0))],
        out_shape=[jax.ShapeDtypeStruct((s, 2 * c), BF16), jax.ShapeDtypeStruct((_HALO, c), F32)],
        scratch=[pltpu.VMEM((_CONV_T + _HALO, c), F32), pltpu.VMEM((_CONV_T + _HALO, c), F32),
                 pltpu.VMEM((7, _SHIFT_ROWS, c), F32), pltpu.VMEM((7, _SHIFT_ROWS, c), F32)],
        rider=rider,
    )(uc, uc, uc, uc, dca, dca, w)


_SC_T = 256
_SC_RC = 32
_SC_LC = 512


def _sc_chunks(d):
    return [(pl.ds(r0, _SC_RC), pl.ds(l0, _SC_LC)) for r0 in range(0, _SC_T, _SC_RC) for l0 in range(0, d, _SC_LC)]


def _short_conv_fwd(u2, w):
    s, d3 = u2.shape
    d = d3 // 3
    hb = _SC_T // _SC_HALO

    def body(b_ref, c_ref, v_ref, hc_ref, hv_ref, w_ref, o_ref, cs_ref):
        i = pl.program_id(0)
        cs_ref[pl.ds(0, _SC_HALO), :] = jnp.where(i > 0, hc_ref[...].astype(F32) * hv_ref[...].astype(F32), 0.0)
        for rows, lanes in _sc_chunks(d):
            cs_ref[pl.ds(_SC_HALO + rows.start, _SC_RC), lanes] = (
                c_ref[rows, lanes].astype(F32) * v_ref[rows, lanes].astype(F32))
        for rows, lanes in _sc_chunks(d):
            taps = [cs_ref[pl.ds(_SC_HALO - 2 + kk + rows.start, _SC_RC), lanes] for kk in range(3)]
            conv = w_ref[0:1, lanes] * taps[0] + w_ref[1:2, lanes] * taps[1] + w_ref[2:3, lanes] * taps[2]
            o_ref[rows, lanes] = (b_ref[rows, lanes].astype(F32) * conv).astype(BF16)

    def col(j):
        return pl.BlockSpec((_SC_T, d), lambda i: (i, j))

    def halo(j):
        return pl.BlockSpec((_SC_HALO, d), lambda i: (jnp.maximum(i * hb - 1, 0), j))

    return _pcall(
        body, name="short_conv_fwd", grid=(s // _SC_T,),
        in_specs=[col(0), col(1), col(2), halo(1), halo(2), pl.BlockSpec((3, d), lambda i: (0, 0))],
        out_specs=pl.BlockSpec((_SC_T, d), lambda i: (i, 0)),
        out_shape=jax.ShapeDtypeStruct((s, d), BF16),
        scratch=[pltpu.VMEM((_SC_T + _SC_HALO, d), F32)])(u2, u2, u2, u2, u2, w)


def _short_conv_bwd(u2, dsc, w, rider=None):
    s, d3 = u2.shape
    d = d3 // 3
    hb = _SC_T // _SC_HALO
    nblk = s // _SC_T

    def body(b_ref, c_ref, v_ref, hc_ref, hv_ref, nb_ref, d_ref, nd_ref, w_ref, du_ref, dw_ref, cs_ref, ds_ref):
        i = pl.program_id(0)
        cs_ref[pl.ds(0, _SC_HALO), :] = jnp.where(i > 0, hc_ref[...].astype(F32) * hv_ref[...].astype(F32), 0.0)
        ds_ref[pl.ds(_SC_T, _SC_HALO), :] = jnp.where(i < nblk - 1, nd_ref[...] * nb_ref[...].astype(F32), 0.0)
        for rows, lanes in _sc_chunks(d):
            cs_ref[pl.ds(_SC_HALO + rows.start, _SC_RC), lanes] = (
                c_ref[rows, lanes].astype(F32) * v_ref[rows, lanes].astype(F32))
            ds_ref[rows, lanes] = d_ref[rows, lanes] * b_ref[rows, lanes].astype(F32)

        @pl.when(i == 0)
        def _():
            dw_ref[...] = jnp.zeros_like(dw_ref)

        for l0 in range(0, d, _SC_LC):
            lanes = pl.ds(l0, _SC_LC)
            dw_acc = [jnp.zeros((8, _SC_LC), F32)] * 3
            for r0 in range(0, _SC_T, _SC_RC):
                rows = pl.ds(r0, _SC_RC)
                taps = [cs_ref[pl.ds(_SC_HALO - 2 + kk + r0, _SC_RC), lanes] for kk in range(3)]
                conv = w_ref[0:1, lanes] * taps[0] + w_ref[1:2, lanes] * taps[1] + w_ref[2:3, lanes] * taps[2]
                dconv = ds_ref[rows, lanes]
                dcv = (w_ref[2:3, lanes] * dconv + w_ref[1:2, lanes] * ds_ref[pl.ds(r0 + 1, _SC_RC), lanes]
                       + w_ref[0:1, lanes] * ds_ref[pl.ds(r0 + 2, _SC_RC), lanes])
                du_ref[rows, lanes] = (d_ref[rows, lanes] * conv).astype(BF16)
                du_ref[rows, pl.ds(d + l0, _SC_LC)] = (dcv * v_ref[rows, lanes].astype(F32)).astype(BF16)
                du_ref[rows, pl.ds(2 * d + l0, _SC_LC)] = (dcv * c_ref[rows, lanes].astype(F32)).astype(BF16)
                for kk in range(3):
                    prod = dconv * taps[kk]
                    dw_acc[kk] = dw_acc[kk] + sum(prod[t:t + 8] for t in range(0, _SC_RC, 8))
            for kk in range(3):
                dw_ref[kk:kk + 1, lanes] += jnp.sum(dw_acc[kk], axis=0, keepdims=True)

    def col(j):
        return pl.BlockSpec((_SC_T, d), lambda i: (i, j))

    def halo(j):
        return pl.BlockSpec((_SC_HALO, d), lambda i: (jnp.maximum(i * hb - 1, 0), j))

    def nxt(j):
        return pl.BlockSpec((_SC_HALO, d), lambda i: (jnp.minimum((i + 1) * hb, s // _SC_HALO - 1), j))

    return _pcall(
        body, name="short_conv_bwd", grid=(nblk,),
        in_specs=[col(0), col(1), col(2), halo(1), halo(2), nxt(0), col(0), nxt(0),
                  pl.BlockSpec((3, d), lambda i: (0, 0))],
        out_specs=[pl.BlockSpec((_SC_T, d3), lambda i: (i, 0)), pl.BlockSpec((8, d), lambda i: (0, 0))],
        out_shape=[jax.ShapeDtypeStruct((s, d3), BF16), jax.ShapeDtypeStruct((8, d), F32)],
        scratch=[pltpu.VMEM((_SC_T + _SC_HALO, d), F32), pltpu.VMEM((_SC_T + _SC_HALO, d), F32)],
        rider=rider,
    )(u2, u2, u2, u2, u2, u2, dsc, dsc, w)


def _bucket_maps():
    a_idx = jnp.arange(_STEPS)[:, None]
    c_idx = jnp.arange(2 * _STEPS)[None, :]
    mdist = jnp.clip(a_idx + _STEPS - c_idx, 0, _STEPS)
    max_exact = _NUM_BUCKETS // 2
    maps = []
    for _, dil in _GROUPS:
        nn = mdist * dil
        nf = jnp.maximum(nn, 1).astype(F32)
        large = max_exact + (jnp.log(nf / max_exact) / math.log(_MAX_DISTANCE / max_exact)
                             * (_NUM_BUCKETS - max_exact)).astype(jnp.int32)
        maps.append(jnp.where(nn < max_exact, nn, jnp.minimum(large, _NUM_BUCKETS - 1)).astype(jnp.int32))
    return jnp.stack(maps, axis=0)


def _bias_expand(rel_bias, buckets):
    nh = rel_bias.shape[1]

    def body(rb_ref, bk_ref, o_ref):
        h = pl.program_id(0)
        bk = bk_ref[0]
        acc = jnp.zeros(bk.shape, F32)
        for b in range(_NUM_BUCKETS):
            acc = jnp.where(bk == b, rb_ref[b, h], acc)
        a = lax.broadcasted_iota(jnp.int32, bk.shape, 0)
        c = lax.broadcasted_iota(jnp.int32, bk.shape, 1)
        mdist = a + _STEPS - c
        o_ref[0] = jnp.where((mdist >= 0) & (mdist <= _STEPS), acc, _NEG)

    return _pcall(
        body, name="bias_expand", grid=(nh,),
        in_specs=[pl.BlockSpec(memory_space=pltpu.SMEM),
                  pl.BlockSpec((1, _STEPS, 2 * _STEPS), lambda h: (h // 8, 0, 0))],
        out_specs=pl.BlockSpec((1, _STEPS, 2 * _STEPS), lambda h: (h, 0, 0)),
        out_shape=jax.ShapeDtypeStruct((nh, _STEPS, 2 * _STEPS), F32))(rel_bias, buckets)


def _bias_reduce(ds_all, buckets):
    nh = ds_all.shape[0]

    def body(ds_ref, bk_ref, o_ref):
        t, bk = ds_ref[0], bk_ref[0]
        rows = lax.broadcasted_iota(jnp.int32, (_NUM_BUCKETS, _LANES), 0)
        out = jnp.zeros((_NUM_BUCKETS, _LANES), F32)
        for b in range(_NUM_BUCKETS):
            out = jnp.where(rows == b, jnp.sum(jnp.where(bk == b, t, 0.0)), out)
        o_ref[0] = out

    blk = pl.BlockSpec((1, _STEPS, 2 * _STEPS), lambda h: (h, 0, 0))
    return _pcall(
        body, name="bias_reduce", grid=(nh,),
        in_specs=[blk, pl.BlockSpec((1, _STEPS, 2 * _STEPS), lambda h: (h // 8, 0, 0))],
        out_specs=pl.BlockSpec((1, _NUM_BUCKETS, _LANES), lambda h: (h, 0, 0)),
        out_shape=jax.ShapeDtypeStruct((nh, _NUM_BUCKETS, _LANES), F32))(ds_all, buckets)


def _sub_residues(dil):
    return 4 if dil % 16 == 0 else 1


def _strided_rows(ref, tmp_ref, p, r, dil):
    sub = _sub_residues(dil)
    if dil == 1:
        return [ref[p]]
    if sub == 1:
        return [ref[p, pl.ds(r, _STEPS, stride=dil), :]]
    tmp_ref[...] = ref[p, pl.ds(r, _STEPS * sub, stride=dil // sub), :]
    return [tmp_ref[pl.ds(q, _STEPS, stride=sub), :] for q in range(sub)]


def _store_strided(ref, tmp_ref, p, r, dil, vals):
    sub = _sub_residues(dil)
    if dil == 1:
        ref[p] = vals[0]
    elif sub == 1:
        ref[p, pl.ds(r, _STEPS, stride=dil), :] = vals[0]
    else:
        for q, val in enumerate(vals):
            tmp_ref[pl.ds(q, _STEPS, stride=sub), :] = val
        ref[p, pl.ds(r, _STEPS * sub, stride=dil // sub), :] = tmp_ref[...]


def _tmp_rows(dil, count):
    sub = _sub_residues(dil)
    return [pltpu.VMEM((_STEPS * sub, _LANES), F32)] * count if sub > 1 else []


def _head_masks():
    lane = lax.broadcasted_iota(jnp.int32, (1, _LANES), 1)
    return [lane < _HEAD_DIM, lane >= _HEAD_DIM]


def _stack_heads(x2, masks):
    return jnp.concatenate([jnp.where(masks[0], x2, 0), jnp.where(masks[1], x2, 0)], axis=0)


def _unstack_heads(y, masks):
    return jnp.where(masks[0], y[:_STEPS], y[_STEPS:])


def _scores(qs2, k2, b_ref, j, first):
    sc = lax.dot_general(qs2, k2, (((1,), (1,)), ((), ())), preferred_element_type=F32)
    sc = sc * (_HEAD_DIM ** -0.5) + jnp.concatenate([b_ref[2 * j], b_ref[2 * j + 1]], axis=0)
    col = lax.broadcasted_iota(jnp.int32, sc.shape, 1)
    return jnp.where(jnp.logical_and(first, col < _STEPS), _NEG, sc)


_PAIRS = _GROUP_COLS // _LANES


def _attn_fwd(uq, uk, uv, bias, g, dil, pp, rider=None):
    s = uq.shape[1]
    rb = _STEPS * dil
    nb = s // rb
    npb = _PAIRS // pp

    sub = _sub_residues(dil)

    def body(q_ref, kc_ref, kp_ref, vc_ref, vp_ref, b_ref, o_ref, l_ref, *tmp):
        tmp = tmp + (None,) * 7
        n, r = pl.program_id(1), pl.program_id(2)
        first = n == 0
        masks = _head_masks()
        for j in range(pp):
            qs = _strided_rows(q_ref, tmp[0], j, r, dil)
            kps, kcs = _strided_rows(kp_ref, tmp[1], j, r, dil), _strided_rows(kc_ref, tmp[2], j, r, dil)
            vps, vcs = _strided_rows(vp_ref, tmp[3], j, r, dil), _strided_rows(vc_ref, tmp[4], j, r, dil)
            o_res, l_res = [], []
            for q in range(sub):
                q2 = qs[q].astype(BF16)
                k2 = jnp.concatenate([kps[q], kcs[q]], axis=0).astype(BF16)
                v2 = jnp.concatenate([vps[q], vcs[q]], axis=0).astype(BF16)
                sc = _scores(_stack_heads(q2, masks), k2, b_ref, j, first)
                mx = jnp.max(sc, axis=-1, keepdims=True)
                p = jnp.exp(sc - mx)
                den = jnp.sum(p, axis=-1, keepdims=True)
                o2 = jnp.dot(p.astype(BF16), v2, preferred_element_type=F32) / den
                o_res.append(_unstack_heads(o2, masks))
                l_res.append(_unstack_heads(jnp.broadcast_to(mx + jnp.log(den), o2.shape), masks))
            _store_strided(o_ref, tmp[5], j, r, dil, o_res)
            _store_strided(l_ref, tmp[6], j, r, dil, l_res)

    cur = pl.BlockSpec((pp, rb, _LANES), lambda hb, n, r: (g * npb + hb, n, 0))
    prev = pl.BlockSpec((pp, rb, _LANES), lambda hb, n, r: (g * npb + hb, jnp.maximum(n - 1, 0), 0))
    bspec = pl.BlockSpec((2 * pp, _STEPS, 2 * _STEPS), lambda hb, n, r: (g * npb + hb, 0, 0))
    ospec = pl.BlockSpec((pp, rb, _LANES), lambda hb, n, r: (hb, n, 0))
    sh = jax.ShapeDtypeStruct((_PAIRS, s, _LANES), F32)
    return _pcall(
        body, name=f"attn_fwd_g{g}", grid=(npb, nb, dil // sub),
        in_specs=[cur, cur, prev, cur, prev, bspec], out_specs=[ospec, ospec], out_shape=[sh, sh],
        scratch=_tmp_rows(dil, 7), rider=rider,
    )(uq, uk, uk, uv, uv, bias)


def _attn_merge(outs, lses, cat):
    s = outs[0].shape[1]
    c = _GROUP_COLS

    def body(o0, o1, o2, l0, l1, l2, cat_in, cat_ref, lse_ref):
        del cat_in
        a0, a1, a2 = l0[...], l1[...], l2[...]
        mx = jnp.maximum(jnp.maximum(a0, a1), a2)
        w0, w1, w2 = jnp.exp(a0 - mx), jnp.exp(a1 - mx), jnp.exp(a2 - mx)
        den = w0 + w1 + w2
        y = ((w0 * o0[...] + w1 * o1[...] + w2 * o2[...]) / den).astype(BF16)
        for p in range(_PAIRS):
            cat_ref[:, p * _LANES:(p + 1) * _LANES] = y[p]
        lse_ref[...] = mx + jnp.log(den)

    blk = pl.BlockSpec((_PAIRS, _ROW_T, _LANES), lambda i: (0, i, 0))
    return _pcall(
        body, name="attn_merge", grid=(s // _ROW_T,),
        in_specs=[blk] * 6 + [_ANY],
        out_specs=[pl.BlockSpec((_ROW_T, c), lambda i: (i, 1)), blk],
        out_shape=[jax.ShapeDtypeStruct(cat.shape, BF16), jax.ShapeDtypeStruct((_PAIRS, s, _LANES), F32)],
        aliases={6: 0})(*outs, *lses, cat)


def _attn_delta(dcat, cat):
    s = dcat.shape[0]
    c = _GROUP_COLS
    seg = (jnp.arange(c)[:, None] // _HEAD_DIM == jnp.arange(c)[None, :] // _HEAD_DIM).astype(BF16)

    def body(dy_ref, y_ref, seg_ref, dl_ref, dys_ref):
        dy = dy_ref[...]
        prod = dy * y_ref[...].astype(F32)
        hi = prod.astype(BF16)
        lo = (prod - hi.astype(F32)).astype(BF16)
        dl = (jnp.dot(hi, seg_ref[...], preferred_element_type=F32)
              + jnp.dot(lo, seg_ref[...], preferred_element_type=F32))
        for p in range(_PAIRS):
            dl_ref[p] = dl[:, p * _LANES:(p + 1) * _LANES]
            dys_ref[p] = dy[:, p * _LANES:(p + 1) * _LANES]

    right = pl.BlockSpec((_ROW_T, c), lambda i: (i, 1))
    blk = pl.BlockSpec((_PAIRS, _ROW_T, _LANES), lambda i: (0, i, 0))
    sh = jax.ShapeDtypeStruct((_PAIRS, s, _LANES), F32)
    return _pcall(
        body, name="attn_delta", grid=(s // _ROW_T,),
        in_specs=[right, right, pl.BlockSpec((c, c), lambda i: (0, 0))],
        out_specs=[blk, blk], out_shape=[sh, sh])(dcat, cat, seg)


def _attn_bwd(uq, uk, uv, dys, lse, delta, bias, prev_grads, g, dil, pp, rider=None):
    s = uq.shape[1]
    rb = _STEPS * dil
    nb = s // rb
    npb = _PAIRS // pp
    scale = _HEAD_DIM ** -0.5

    sub = _sub_residues(dil)

    def body(q_ref, kc_ref, kp_ref, vc_ref, vp_ref, dy_ref, l_ref, dl_ref, b_ref, *rest):
        rest = rest[len(prev_grads):]
        dq_ref, dk_ref, dv_ref, dsa_ref, dkc_ref, dvc_ref = rest[:6]
        tmp = rest[6:] + (None,) * 11
        n, r = pl.program_id(1), pl.program_id(2)

        def carry_slot(j, q):
            return ((r + (dil // sub) * q) * pp + j) if sub > 1 else r * pp + j

        @pl.when(jnp.logical_and(n == 0, r == 0))
        def _():
            dsa_ref[...] = jnp.zeros_like(dsa_ref)

        @pl.when(n == 0)
        def _():
            for j in range(pp):
                for q in range(sub):
                    dkc_ref[carry_slot(j, q)] = jnp.zeros((_STEPS, _LANES), F32)
                    dvc_ref[carry_slot(j, q)] = jnp.zeros((_STEPS, _LANES), F32)

        @pl.when(n < nb)
        def _():
            first = n == 0
            masks = _head_masks()
            for j in range(pp):
                qs = _strided_rows(q_ref, tmp[0], j, r, dil)
                kps, kcs = _strided_rows(kp_ref, tmp[1], j, r, dil), _strided_rows(kc_ref, tmp[2], j, r, dil)
                vps, vcs = _strided_rows(vp_ref, tmp[3], j, r, dil), _strided_rows(vc_ref, tmp[4], j, r, dil)
                dys_ = _strided_rows(dy_ref, tmp[5], j, r, dil)
                lses = _strided_rows(l_ref, tmp[6], j, r, dil)
                dls = _strided_rows(dl_ref, tmp[7], j, r, dil)
                ds_sum = [jnp.zeros((_STEPS, 2 * _STEPS), F32)] * 2
                dq_res, dk_res, dv_res = [], [], []
                for q in range(sub):
                    q2 = qs[q].astype(BF16)
                    k2 = jnp.concatenate([kps[q], kcs[q]], axis=0).astype(BF16)
                    v2 = jnp.concatenate([vps[q], vcs[q]], axis=0).astype(BF16)
                    dy2 = dys_[q].astype(BF16)
                    qs2, dys2 = _stack_heads(q2, masks), _stack_heads(dy2, masks)
                    per_row = lambda st: jnp.concatenate([st[:, 0:1], st[:, _HEAD_DIM:_HEAD_DIM + 1]], axis=0)
                    sc = _scores(qs2, k2, b_ref, j, first)
                    p = jnp.exp(sc - per_row(lses[q]))
                    dp = lax.dot_general(dys2, v2, (((1,), (1,)), ((), ())), preferred_element_type=F32)
                    ds = p * (dp - per_row(dls[q]))
                    ds_sum[0] = ds_sum[0] + ds[:_STEPS]
                    ds_sum[1] = ds_sum[1] + ds[_STEPS:]
                    dsb = ds.astype(BF16)
                    dq_p = _unstack_heads(jnp.dot(dsb, k2, preferred_element_type=F32), masks)
                    tdn = (((0,), (0,)), ((), ()))
                    dk_p = lax.dot_general(dsb, qs2, tdn, preferred_element_type=F32) * scale
                    dv_p = lax.dot_general(p.astype(BF16), dys2, tdn, preferred_element_type=F32)
                    slot = carry_slot(j, q)
                    dq_res.append(dq_p * scale)
                    dk_res.append(dkc_ref[slot] + dk_p[:_STEPS])
                    dv_res.append(dvc_ref[slot] + dv_p[:_STEPS])
                    dkc_ref[slot] = dk_p[_STEPS:]
                    dvc_ref[slot] = dv_p[_STEPS:]
                for hh in range(2):
                    dsa_ref[2 * j + hh] += ds_sum[hh]
                _store_strided(dq_ref, tmp[8], j, r, dil, dq_res)
                _store_strided(dk_ref, tmp[9], j, r, dil, dk_res)
                _store_strided(dv_ref, tmp[10], j, r, dil, dv_res)

        @pl.when(n == nb)
        def _():
            for j in range(pp):
                _store_strided(dk_ref, tmp[9], j, r, dil, [dkc_ref[carry_slot(j, q)] for q in range(sub)])
                _store_strided(dv_ref, tmp[10], j, r, dil, [dvc_ref[carry_slot(j, q)] for q in range(sub)])

    def clamp(n):
        return jnp.minimum(n, nb - 1)

    cur = pl.BlockSpec((pp, rb, _LANES), lambda hb, n, r: (g * npb + hb, clamp(n), 0))
    prev = pl.BlockSpec((pp, rb, _LANES), lambda hb, n, r: (g * npb + hb, jnp.maximum(clamp(n) - 1, 0), 0))
    stat = pl.BlockSpec((pp, rb, _LANES), lambda hb, n, r: (hb, clamp(n), 0))
    bspec = pl.BlockSpec((2 * pp, _STEPS, 2 * _STEPS), lambda hb, n, r: (g * npb + hb, 0, 0))
    dkspec = pl.BlockSpec((pp, rb, _LANES), lambda hb, n, r: (g * npb + hb, jnp.maximum(n - 1, 0), 0))
    dsspec = pl.BlockSpec((2 * pp, _STEPS, 2 * _STEPS), lambda hb, n, r: (hb, 0, 0))
    wide = jax.ShapeDtypeStruct((3 * _PAIRS, s, _LANES), F32)
    np_ = len(prev_grads)
    return _pcall(
        body, name=f"attn_bwd_g{g}", grid=(npb, nb + 1, dil // sub),
        in_specs=[cur, cur, prev, cur, prev, stat, stat, stat, bspec] + [_ANY] * np_,
        out_specs=[cur, dkspec, dkspec, dsspec],
        out_shape=[wide, wide, wide, jax.ShapeDtypeStruct((8, _STEPS, 2 * _STEPS), F32)],
        scratch=[pltpu.VMEM((dil * pp, _STEPS, _LANES), F32), pltpu.VMEM((dil * pp, _STEPS, _LANES), F32)]
        + _tmp_rows(dil, 11),
        aliases={9 + t: t for t in range(np_)}, rider=rider,
    )(uq, uk, uk, uv, uv, dys, lse, delta, bias, *prev_grads)


def _place():
    x, y, c = lax.axis_index("x"), lax.axis_index("y"), lax.axis_index("c")
    chips = [(1 - x, y), (x, 1 - y), (1 - x, 1 - y)]
    return x, y, c, chips


def _slab(ref, axis, chip, width):
    start = pl.multiple_of(chip * width, width)
    if axis == 0:
        return ref.at[pl.ds(start, width), :]
    return ref.at[:, pl.ds(start, width)]


def _run_rider(rider, name):
    nin, nout = len(rider.ins), len(rider.out_shapes)

    def body(*refs):
        ins, outs, scr = refs[:nin], refs[nin:nin + nout], refs[nin + nout:]
        rider.start(ins, outs, scr)
        rider.finish(ins, outs, scr)

    return _pcall(body, name=name, in_specs=[_ANY] * nin, out_specs=[_ANY] * nout, out_shape=rider.out_shapes,
                  scratch=rider.scratch)(*rider.ins)


def _gather_halves_rider(shard, axis):
    shape = list(shard.shape)
    shape[axis] *= 4
    full = jax.ShapeDtypeStruct(tuple(shape), shard.dtype)
    half = shard.shape[0] // 2
    width = shard.shape[axis]

    def region(out, chip, core):
        if axis == 0:
            return out.at[pl.ds(pl.multiple_of(chip * width + core * half, half), half), :]
        return out.at[pl.ds(pl.multiple_of(core * half, half), half), pl.ds(pl.multiple_of(chip * width, width), width)]

    def copies(ins, outs, scr):
        send, recv, loc = scr
        (src,), (out,) = ins, outs
        x, y, c, chips = _place()
        mine = 2 * x + y
        own = pltpu.make_async_copy(src, _slab(out, axis, mine, width), loc.at[0])
        my_half = src.at[pl.ds(pl.multiple_of(c * half, half), half), :]
        over_ici, ici_in, to_sib, sib_in = [], [], [], []
        for j, (px, py) in enumerate(chips):
            theirs = 2 * px + py
            ici = dict(send_sem=send.at[j], recv_sem=recv.at[j], device_id=(px, py, c), device_id_type=MESH)
            d2d = dict(send_sem=send.at[3 + j], recv_sem=recv.at[3 + j], device_id=(x, y, 1 - c),
                       device_id_type=MESH)
            over_ici.append(pltpu.make_async_remote_copy(src_ref=my_half, dst_ref=region(out, mine, c), **ici))
            ici_in.append(pltpu.make_async_remote_copy(src_ref=my_half, dst_ref=region(out, theirs, c), **ici))
            to_sib.append(pltpu.make_async_remote_copy(
                src_ref=region(out, theirs, c), dst_ref=region(out, theirs, c), **d2d))
            sib_in.append(pltpu.make_async_remote_copy(
                src_ref=region(out, theirs, c), dst_ref=region(out, theirs, 1 - c), **d2d))
        return own, over_ici, ici_in, to_sib, sib_in

    def start(ins, outs, scr):
        own, over_ici, _, _, _ = copies(ins, outs, scr)
        own.start()
        for cp in over_ici:
            cp.start()

    def finish(ins, outs, scr):
        own, over_ici, ici_in, to_sib, sib_in = copies(ins, outs, scr)
        for j in range(3):
            ici_in[j].wait_recv()
            to_sib[j].start()
        for cp in sib_in:
            cp.wait_recv()
        own.wait()
        for cp in over_ici + to_sib:
            cp.wait_send()

    return _Rider([shard], [full], [pltpu.SemaphoreType.DMA((6,)), pltpu.SemaphoreType.DMA((6,)),
                                    pltpu.SemaphoreType.DMA((1,))], start, finish)


def _join_riders(riders):
    if len(riders) == 1:
        return riders[0]

    def parts(ins, outs, scr):
        pi = po = ps = 0
        for rd in riders:
            ni, no, ns = len(rd.ins), len(rd.out_shapes), len(rd.scratch)
            yield rd, ins[pi:pi + ni], outs[po:po + no], scr[ps:ps + ns]
            pi, po, ps = pi + ni, po + no, ps + ns

    def start(ins, outs, scr):
        for rd, i, o, sc in parts(ins, outs, scr):
            rd.start(i, o, sc)

    def finish(ins, outs, scr):
        for rd, i, o, sc in parts(ins, outs, scr):
            rd.finish(i, o, sc)

    return _Rider(sum((rd.ins for rd in riders), []), sum((rd.out_shapes for rd in riders), []),
                  sum((rd.scratch for rd in riders), []), start, finish)


def _scatter_rider(grads, axes, rows=None):
    nw = len(grads)
    outs_shape = []
    for gr, ax in zip(grads, axes):
        shape = list(gr.shape)
        shape[ax] //= 4
        if rows is not None:
            assert ax == 1
            shape[0] = rows[1] - rows[0]
        outs_shape.append(jax.ShapeDtypeStruct((3,) + tuple(shape), gr.dtype))

    def copies(ins, outs, scr):
        send, recv = scr
        x, y, c, chips = _place()
        cps = []
        for t in range(nw):
            width = ins[t].shape[axes[t]] // 4
            src = ins[t] if rows is None else ins[t].at[pl.ds(rows[0], rows[1] - rows[0]), :]
            for j, (px, py) in enumerate(chips):
                cps.append(pltpu.make_async_remote_copy(
                    src_ref=_slab(src, axes[t], 2 * px + py, width), dst_ref=outs[t].at[j],
                    send_sem=send.at[3 * t + j], recv_sem=recv.at[3 * t + j],
                    device_id=(px, py, c), device_id_type=MESH))
        return cps

    def start(ins, outs, scr):
        for cp in copies(ins, outs, scr):
            cp.start()

    def finish(ins, outs, scr):
        cps = copies(ins, outs, scr)
        for cp in cps:
            cp.wait_recv()
        for cp in cps:
            cp.wait_send()

    return _Rider(grads, outs_shape, [pltpu.SemaphoreType.DMA((3 * nw,)), pltpu.SemaphoreType.DMA((3 * nw,))],
                  start, finish)


def _swap_rider(parts):
    nw = len(parts)

    def copies(ins, outs, scr):
        send, recv = scr
        x, y, c, _ = _place()
        return [pltpu.make_async_remote_copy(
            src_ref=ins[t], dst_ref=outs[t], send_sem=send.at[t], recv_sem=recv.at[t],
            device_id=(x, y, 1 - c), device_id_type=MESH) for t in range(nw)]

    def start(ins, outs, scr):
        for cp in copies(ins, outs, scr):
            cp.start()

    def finish(ins, outs, scr):
        cps = copies(ins, outs, scr)
        for cp in cps:
            cp.wait_recv()
        for cp in cps:
            cp.wait_send()

    return _Rider(parts, [jax.ShapeDtypeStruct(p.shape, p.dtype) for p in parts],
                  [pltpu.SemaphoreType.DMA((nw,)), pltpu.SemaphoreType.DMA((nw,))], start, finish)


def _sum_all_devices(buf, name):
    rows, cols = buf.shape

    def body(in_ref, o_ref, gat_ref, send, recv):
        x, y, c, _ = _place()
        me = 4 * x + 2 * y + c
        gat_ref[me] = in_ref[...]
        started = []
        for mask in range(1, 8):
            fx, fy, fc = (mask >> 2) & 1, (mask >> 1) & 1, mask & 1
            peer = (x + fx * (1 - 2 * x), y + fy * (1 - 2 * y), c + fc * (1 - 2 * c))
            cp = pltpu.make_async_remote_copy(
                src_ref=in_ref, dst_ref=gat_ref.at[me], send_sem=send.at[mask - 1], recv_sem=recv.at[mask - 1],
                device_id=peer, device_id_type=MESH)
            cp.start()
            started.append(cp)
        for cp in started:
            cp.wait_recv()
        for cp in started:
            cp.wait_send()
        acc = gat_ref[0]
        for t in range(1, 8):
            acc = acc + gat_ref[t]
        o_ref[...] = acc

    vm = pl.BlockSpec(memory_space=pltpu.VMEM)
    return _pcall(
        body, name=name, in_specs=[vm], out_specs=vm, out_shape=jax.ShapeDtypeStruct((rows, cols), F32),
        scratch=[pltpu.VMEM((8, rows, cols), F32), pltpu.SemaphoreType.DMA((7,)), pltpu.SemaphoreType.DMA((7,))],
    )(buf)


_UPD_T = 256


def _sum_partials(own, got, name):
    rows, cols = own.shape
    tr = min(_UPD_T, rows)

    def body(own_ref, got_ref, o_ref):
        acc = own_ref[...].astype(F32)
        for j in range(3):
            acc = acc + got_ref[j].astype(F32)
        o_ref[...] = acc

    blk = pl.BlockSpec((tr, cols), lambda i: (i, 0))
    return _pcall(
        body, name=name, grid=(rows // tr,),
        in_specs=[blk, pl.BlockSpec((3, tr, cols), lambda i: (0, i, 0))], out_specs=blk,
        out_shape=jax.ShapeDtypeStruct((rows, cols), F32))(own, got)


def _adamw_math(w, gr, m, v):
    m = _B1 * m + (1.0 - _B1) * gr
    v = _B2 * v + (1.0 - _B2) * (gr * gr)
    m_hat = m / (1.0 - _B1 ** _STEP)
    v_hat = v / (1.0 - _B2 ** _STEP)
    delta = -_LR * (m_hat / (jnp.sqrt(v_hat) + _EPS) + _WD * w)
    return delta, m, v


def _adamw(w, m, v, parts, name):
    rows, cols = w.shape
    tr = min(_UPD_T, rows)
    npart = len(parts)

    def body(w_ref, m_ref, v_ref, *rest):
        p_refs, (g_ref, d_ref, nm_ref, nv_ref) = rest[:npart], rest[npart:]
        gr = p_refs[0][...]
        for p in p_refs[1:]:
            gr = gr + p[...]
        delta, nm, nv = _adamw_math(w_ref[...], gr, m_ref[...], v_ref[...])
        g_ref[...] = gr
        d_ref[...] = delta
        nm_ref[...] = nm
        nv_ref[...] = nv

    blk = pl.BlockSpec((tr, cols), lambda i: (i, 0))
    sh = jax.ShapeDtypeStruct((rows, cols), F32)
    return _pcall(body, name=name, grid=(rows // tr,), in_specs=[blk] * (3 + npart), out_specs=[blk] * 4,
                  out_shape=[sh] * 4)(w, m, v, *parts)


def _adamw_layers(w, m, v, parts, name):
    _, rows, cols = w.shape
    tr = min(_UPD_T, rows)
    npart = len(parts[0])

    def body(w_ref, m_ref, v_ref, *rest):
        p_refs, (g_ref, d_ref, nm_ref, nv_ref) = rest[:2 * npart], rest[2 * npart:]
        grs = []
        for layer in range(2):
            gr = p_refs[layer * npart][...]
            for p in p_refs[layer * npart + 1:(layer + 1) * npart]:
                gr = gr + p[...]
            grs.append(gr)
        gr = jnp.where(pl.program_id(0) == 0, grs[0], grs[1])
        delta, nm, nv = _adamw_math(w_ref[...], gr, m_ref[...], v_ref[...])
        g_ref[...] = gr
        d_ref[...] = delta
        nm_ref[...] = nm
        nv_ref[...] = nv

    blk = pl.BlockSpec((None, tr, cols), lambda l, i: (l, i, 0))

    def part_spec(layer):
        return pl.BlockSpec((tr, cols), lambda l, i: (jnp.where(l == layer, i, 0), 0))

    sh = jax.ShapeDtypeStruct(w.shape, F32)
    return _pcall(
        body, name=name, grid=(2, rows // tr),
        in_specs=[blk] * 3 + [part_spec(0)] * npart + [part_spec(1)] * npart, out_specs=[blk] * 4,
        out_shape=[sh] * 4)(w, m, v, *parts[0], *parts[1])


_PACK_W = 1024


def _pack(arrs, rows):
    flat = []
    for a in arrs:
        f = a.reshape(-1).astype(F32)
        pad = (-f.shape[0]) % _PACK_W
        flat.append(jnp.pad(f, (0, pad)))
    f = jnp.concatenate(flat)
    f = jnp.pad(f, (0, rows * _PACK_W - f.shape[0]))
    return f.reshape(rows, _PACK_W)


def _unpack(buf, shapes):
    flat = buf.reshape(-1)
    out, pos = [], 0
    for sh in shapes:
        size = math.prod(sh)
        out.append(flat[pos:pos + size].reshape(sh))
        pos += size + ((-size) % _PACK_W)
    return out


def _pack_rows(shapes):
    total = sum(-(-math.prod(sh) // _PACK_W) for sh in shapes)
    return -(-total // 8) * 8


def kernel(x, rel_bias, ab_norm, ab_w_in, ab_conv_w, ab_conv_b, ab_ln_g, ab_ln_b, ab_w_out, sc_norm, sc_w_in, sc_conv_w, sc_w_out, mlp_norm, mlp_w_up, mlp_w_down, final_norm, loss_target, m_rel_bias, m_ab_norm, m_ab_w_in, m_ab_conv_w, m_ab_conv_b, m_ab_ln_g, m_ab_ln_b, m_ab_w_out, m_sc_norm, m_sc_w_in, m_sc_conv_w, m_sc_w_out, m_mlp_norm, m_mlp_w_up, m_mlp_w_down, m_final_norm, v_rel_bias, v_ab_norm, v_ab_w_in, v_ab_conv_w, v_ab_conv_b, v_ab_ln_g, v_ab_ln_b, v_ab_w_out, v_sc_norm, v_sc_w_in, v_sc_conv_w, v_sc_w_out, v_mlp_norm, v_mlp_w_up, v_mlp_w_down, v_final_norm):
    s, d = x.shape[1], x.shape[2]
    dff = 4 * d
    c = _GROUP_COLS
    chip = 2 * lax.axis_index("x") + lax.axis_index("y")
    on_c0 = (lax.axis_index("c") == 0).astype(F32)
    h0 = x[0]
    tgt = loss_target[0]

    cw_sh, scn_sh, scw_sh = ab_conv_w[0], sc_norm, sc_conv_w[0]
    conv_w_full = lax.dynamic_update_slice(jnp.zeros((_CONV_K, c), F32), cw_sh * on_c0, (0, chip * cw_sh.shape[1]))
    scn_full = lax.dynamic_update_slice(jnp.zeros((1, d), F32), scn_sh * on_c0, (0, chip * scn_sh.shape[1]))
    scw_full = lax.dynamic_update_slice(jnp.zeros((3, d), F32), scw_sh * on_c0, (0, chip * scw_sh.shape[1]))
    small_shapes = [(_CONV_K, c), (1, d), (3, d)]
    small = _sum_all_devices(_pack([conv_w_full, scn_full, scw_full], _pack_rows(small_shapes)), "gather_small")
    conv_w, sc_g, sc_cw = _unpack(small, small_shapes)

    w_shards = [ab_w_in[0], ab_w_out[0], sc_w_in[0], sc_w_out[0], mlp_w_up[0], mlp_w_up[1],
                mlp_w_down[0], mlp_w_down[1]]
    w_axes = [1, 0, 1, 0, 1, 1, 0, 0]
    wb = [w.astype(BF16) for w in w_shards]
    full_w = [None] * 8

    def gather(idx):
        return _join_riders([_gather_halves_rider(wb[t], w_axes[t]) for t in idx])

    def put(idx, got_w):
        for t, w in zip(idx, got_w):
            full_w[t] = w

    buckets = _bucket_maps()
    bias = _bias_expand(rel_bias, buckets)
    n0, got_w = _rms_fwd(h0, ab_norm, "rms_fwd_ab", rider=_gather_halves_rider(wb[0], w_axes[0]))
    put([0], got_w)
    w_in = full_w[0]
    tm = min(1024, s)
    tm2 = min(2048, s)
    tmh = min(512, s)
    uc = _mm(n0, w_in, "nn", m=s, n=2 * c, k=d, tm=tm2, tn=2 * c, tk=d, out_dtype=BF16, name="proj_conv")
    uq, uk, uv = [], [], []
    for t, (nm, dst) in enumerate(zip("qkv", (uq, uk, uv))):
        res = _mm(n0, w_in, "nn", m=s, n=3 * c, k=d, tm=tm2, tn=c, tk=d, out_dtype=F32, name=f"proj_{nm}",
                  b_off=(0, 2 + 3 * t), split="o", rider=gather([1]) if t == 0 else None)
        if t == 0:
            res, got_w = res
            put([1], got_w)
        dst.append(res)
    uq, uk, uv = uq[0], uk[0], uv[0]
    (cat, ca), got_w = _conv_a_fwd(uc, conv_w, ab_conv_b, ab_ln_g, ab_ln_b, rider=gather([2]))
    put([2], got_w)
    outs, lses = [], []
    for g, (_, dil) in enumerate(_GROUPS):
        idx = ([4], [6], [3, 5])[g]
        (o, l), got_w = _attn_fwd(uq, uk, uv, bias, g, dil, 4 if dil <= 4 else 2, rider=gather(idx))
        put(idx, got_w)
        outs.append(o)
        lses.append(l)
    cat, lse = _attn_merge(outs, lses, cat)
    h1, n1 = _mm(cat, full_w[1], "nn", m=s, n=d, k=d, tm=tm, tn=d, tk=d, out_dtype=(F32, BF16), name="out_ab",
                 epi=_epi_add_rms, extras=(h0,), vecs=(mlp_norm[0:1],))

    def mlp_fwd(h, nrm, layer, next_gain=None, rider=None):
        zr = _mm(nrm, full_w[4 + layer], "nn", m=s, n=dff, k=d, tm=tmh, tn=dff, tk=d, out_dtype=BF16,
                 name=f"mlp_up{layer}", epi=_epi_relu, rider=rider)
        if rider is not None:
            zr, got_r = zr
            put([7], got_r)
        kw = dict(m=s, n=d, k=dff, tm=tmh, tn=d, tk=dff, name=f"mlp_down{layer}", a_pro=_square, extras=(h,))
        if next_gain is None:
            return zr, _mm(zr, full_w[6 + layer], "nn", out_dtype=F32, epi=_epi_add, **kw), None
        hn, nn = _mm(zr, full_w[6 + layer], "nn", out_dtype=(F32, BF16), epi=_epi_add_rms, vecs=(next_gain,), **kw)
        return zr, hn, nn

    zr0, h2, n2 = mlp_fwd(h1, n1, 0, next_gain=sc_g, rider=gather([7]))
    _, w_out, w_si, w_so, w_up0, w_up1, w_dn0, w_dn1 = full_w
    w_up, w_dn = [w_up0, w_up1], [w_dn0, w_dn1]
    u2 = _mm(n2, w_si, "nn", m=s, n=3 * d, k=d, tm=tmh, tn=3 * d, tk=d, out_dtype=BF16, name="proj_sc")
    scv = _short_conv_fwd(u2, sc_cw)
    h3, n3 = _mm(scv, w_so, "nn", m=s, n=d, k=d, tm=tm, tn=d, tk=d, out_dtype=(F32, BF16), name="out_sc",
                 epi=_epi_add_rms, extras=(h2,), vecs=(mlp_norm[1:2],))
    zr1, h4, _ = mlp_fwd(h3, n3, 1)

    dh4, dh4b, g_final, loss_part = _loss_head(h4, tgt, final_norm.reshape(1, d))
    tkw = min(2048, s)

    big_grads, got, sums = [None] * 8, [None] * 8, [None] * 8

    def scatter(t):
        return _scatter_rider([big_grads[t]], [w_axes[t]])

    def own_slab(t):
        width = big_grads[t].shape[w_axes[t]] // 4
        return lax.dynamic_slice_in_dim(big_grads[t], chip * width, width, axis=w_axes[t])

    def arrived(t, got_t):
        got[t] = got_t[0]
        sums[t] = _sum_partials(own_slab(t), got[t], f"sum_partials{t}")

    def mlp_bwd(dh, dhb, h, nrm, zr, layer):
        dz = _mm(dhb, w_dn[layer], "nt", m=s, n=dff, k=d, tm=tmh, tn=dff, tk=d, out_dtype=BF16,
                 name=f"mlp_down{layer}_dx", epi=_epi_relu_sq_bwd, extras=(zr,))
        big_grads[6 + layer] = _mm(zr, dhb, "tn", m=dff, n=d, k=s, tm=1024, tn=d, tk=tkw, out_dtype=BF16,
                                   name=f"mlp_down{layer}_dw", a_pro=_square)
        big_grads[4 + layer] = _mm(nrm, dz, "tn", m=d, n=dff, k=s, tm=d, tn=1024, tk=tkw, out_dtype=BF16,
                                   name=f"mlp_up{layer}_dw")
        res, got_t = _mm(dz, w_up[layer], "nt", m=s, n=d, k=dff, tm=tmh, tn=d, tk=dff, out_dtype=(F32, BF16),
                         name=f"mlp_up{layer}_dx", rider=scatter(6 + layer), epi=_epi_rms_bwd,
                         extras=(h, dh), vecs=(mlp_norm[layer:layer + 1],), row_sum=True)
        arrived(6 + layer, got_t)
        return res

    dh3, dh3b, g_mn1 = mlp_bwd(dh4, dh4b, h3, n3, zr1, 1)

    dsc = _mm(dh3b, w_so, "nt", m=s, n=d, k=d, tm=tm, tn=d, tk=d, out_dtype=F32, name="out_sc_dx")
    big_grads[3] = _mm(scv, dh3b, "tn", m=d, n=d, k=s, tm=d, tn=d, tk=tkw, out_dtype=BF16, name="out_sc_dw")
    du2, g_sccw8 = _short_conv_bwd(u2, dsc, sc_cw)
    big_grads[2], got_t = _mm(n2, du2, "tn", m=d, n=3 * d, k=s, tm=d, tn=1024, tk=tkw, out_dtype=BF16,
                              name="proj_sc_dw", rider=scatter(3))
    arrived(3, got_t)
    (dh2, dh2b, g_scn), got_t = _mm(
        du2, w_si, "nt", m=s, n=d, k=3 * d, tm=tmh, tn=d, tk=3 * d, out_dtype=(F32, BF16), name="proj_sc_dx",
        rider=scatter(5), epi=_epi_rms_bwd, extras=(h2, dh3), vecs=(sc_g,), row_sum=True)
    arrived(5, got_t)

    dh1, dh1b, g_mn0 = mlp_bwd(dh2, dh2b, h1, n1, zr0, 0)

    dcat = _mm(dh1b, w_out, "nt", m=s, n=d, k=d, tm=tm, tn=d, tk=d, out_dtype=F32, name="out_ab_dx")
    big_grads[1] = _mm(cat, dh1b, "tn", m=d, n=d, k=s, tm=d, tn=d, tk=tkw, out_dtype=BF16, name="out_ab_dw")
    (dca, conv_stats), got_t = _conv_a_bwd_ln(ca, dcat, ab_ln_g, ab_ln_b, rider=scatter(1))
    arrived(1, got_t)
    (duc, g_cw32), got_t = _conv_a_bwd_conv(uc, dca, conv_w, rider=scatter(4))
    arrived(4, got_t)
    delta, dys = _attn_delta(dcat, cat)

    grads_qkv, ds_list = [], []
    for g, (_, dil) in enumerate(_GROUPS):
        res = _attn_bwd(uq, uk, uv, dys, lse, delta, bias, grads_qkv, g, dil, 4 if dil <= 4 else 1,
                        rider=scatter(2) if g == 0 else None)
        if g == 0:
            res, got_t = res
            arrived(2, got_t)
        dq, dk, dv, dsa = res
        grads_qkv = [dq, dk, dv]
        ds_list.append(dsa)
    g_bias = _bias_reduce(jnp.concatenate(ds_list, axis=0), buckets)[:, :, 0].T

    secs = [(duc, 2 * c, 0)] + [(grads_qkv[t], 3 * c, 2 + 3 * t) for t in range(3)]
    g_in_parts = []
    for t, (du, width, off) in enumerate(secs):
        rider = _swap_rider(sums[1:]) if t == 1 else None
        part = _mm(n0, du, "tn", m=d, n=width, k=s, tm=d, tn=width, tk=min(1024, s) if t else tkw, out_dtype=BF16,
                   name=f"proj_ab_dw{t}", rider=rider, split="b" if t else "")
        if rider is not None:
            part, sib_late = part
        g_in_parts.append(part)
    big_grads[0] = jnp.concatenate(g_in_parts, axis=1)
    cut = d * 3 // 4
    dn0 = _mm(duc, w_in, "nt", m=s, n=d, k=2 * c, tm=tm, tn=d, tk=2 * c, out_dtype=F32, name="proj_ab_dx_conv")
    dn0, (got_top,) = _dx_qkv(*grads_qkv, w_in, dn0, tm=tm, tk=c, col0=2 * c, name="proj_ab_dx_qkv",
                              rider=_scatter_rider([big_grads[0]], [w_axes[0]], rows=(0, cut)))
    (grad_x, _, g_abn), (got_bot,) = _rms_bwd(
        dn0, h0, ab_norm, dh1, "rms_bwd_ab", rider=_scatter_rider([big_grads[0]], [w_axes[0]], rows=(cut, d)))
    own0 = own_slab(0)
    sums[0] = jnp.concatenate([_sum_partials(own0[:cut], got_top, "sum_partials0_top"),
                               _sum_partials(own0[cut:], got_bot, "sum_partials0_bottom")], axis=0)
    sib = list(_run_rider(_swap_rider([sums[0]]), "swap_sibling_w_in")) + sib_late

    upd = [_adamw(w_shards[t], mm[0], vv[0], [sums[t], sib[t]], f"adamw{t}")
           for t, (mm, vv) in enumerate(((m_ab_w_in, v_ab_w_in), (m_ab_w_out, v_ab_w_out),
                                         (m_sc_w_in, v_sc_w_in), (m_sc_w_out, v_sc_w_out)))]
    upd_up = _adamw_layers(mlp_w_up, m_mlp_w_up, v_mlp_w_up, [[sums[4], sib[4]], [sums[5], sib[5]]], "adamw_up")
    upd_dn = _adamw_layers(mlp_w_down, m_mlp_w_down, v_mlp_w_down, [[sums[6], sib[6]], [sums[7], sib[7]]],
                           "adamw_down")

    full_shapes = [(_NUM_BUCKETS, rel_bias.shape[1]), (1, d), (_CONV_K, c), (1, c), (1, c), (1, c), (1, d),
                   (3, d), (2, d), (d,), (1, 1)]
    small_grads = [g_bias, g_abn, g_cw32[:_CONV_K], conv_stats[0:1], conv_stats[1:2], conv_stats[2:3], g_scn,
                   g_sccw8[:3], jnp.concatenate([g_mn0, g_mn1], axis=0), g_final.reshape(d), loss_part[0:1, 0:1]]
    tot = _unpack(_sum_all_devices(_pack(small_grads, _pack_rows(full_shapes)), "sum_small"), full_shapes)
    loss = tot.pop()[0, 0]
    for idx, sh in ((2, cw_sh), (6, scn_sh), (7, scw_sh)):
        width = sh.shape[1]
        tot[idx] = lax.dynamic_slice_in_dim(tot[idx], chip * width, width, axis=1)
    sm_w = [rel_bias, ab_norm, cw_sh, ab_conv_b, ab_ln_g, ab_ln_b, scn_sh, scw_sh, mlp_norm, final_norm]
    sm_m = [m_rel_bias, m_ab_norm, m_ab_conv_w[0], m_ab_conv_b, m_ab_ln_g, m_ab_ln_b, m_sc_norm, m_sc_conv_w[0],
            m_mlp_norm, m_final_norm]
    sm_v = [v_rel_bias, v_ab_norm, v_ab_conv_w[0], v_ab_conv_b, v_ab_ln_g, v_ab_ln_b, v_sc_norm, v_sc_conv_w[0],
            v_mlp_norm, v_final_norm]
    sh_shapes = [tuple(t.shape) for t in tot]
    rows = _pack_rows(sh_shapes)
    sm_upd = _adamw(_pack(sm_w, rows), _pack(sm_m, rows), _pack(sm_v, rows), [_pack(tot, rows)], "adamw_small")
    sm_g, sm_d, sm_nm, sm_nv = [_unpack(buf, sh_shapes) for buf in sm_upd]

    def assemble(kind, sm):
        big = [u[kind] for u in upd]
        return [sm[0], sm[1], big[0][None], sm[2][None], sm[3], sm[4], sm[5], big[1][None], sm[6], big[2][None],
                sm[7][None], big[3][None], sm[8], upd_up[kind], upd_dn[kind], sm[9]]

    res = [loss, grad_x[None]]
    for kind, sm in enumerate((sm_g, sm_d, sm_nm, sm_nv)):
        res += assemble(kind, sm)
    return tuple(res)
```

```python
import functools
import math

import jax
import jax.numpy as jnp
from jax import lax
from jax.experimental import pallas as pl
from jax.experimental.pallas import tpu as pltpu

F32 = jnp.float32
BF16 = jnp.bfloat16
MESH = pl.DeviceIdType.MESH

_GROUPS = ((128, 1), (512, 4), (2048, 16))
_STEPS = 128
_HEAD_DIM = 64
_GROUP_COLS = 512
_NUM_BUCKETS = 32
_MAX_DISTANCE = 2048
_CONV_K = 31
_HALO = 32
_SC_HALO = 16
_RMS_EPS = 1e-6
_LN_EPS = 1e-5
_NEG = -1e30
_LANES = 128
_VMEM_LIMIT = 56 * 1024 * 1024

_LR, _B1, _B2, _EPS, _WD, _STEP = 0.001, 0.9, 0.999, 1e-08, 0.01, 10


class _Rider:
    def __init__(self, ins, out_shapes, scratch, start, finish):
        self.ins, self.out_shapes, self.scratch = list(ins), list(out_shapes), list(scratch)
        self.start, self.finish = start, finish


def _pcall(body, *, name, out_shape, in_specs, out_specs, grid=None, scratch=(), aliases=None, rider=None):
    kw = {} if grid is None else {"grid": grid}
    cparams = pltpu.CompilerParams(vmem_limit_bytes=_VMEM_LIMIT)
    if rider is None:
        return pl.pallas_call(
            body, name=name, out_shape=out_shape, in_specs=in_specs, out_specs=out_specs,
            scratch_shapes=list(scratch), input_output_aliases=aliases or {},
            compiler_params=cparams, **kw)
    single = not isinstance(out_specs, (list, tuple))
    ospecs = [out_specs] if single else list(out_specs)
    oshapes = [out_shape] if single else list(out_shape)
    nin, nout, nscr = len(in_specs), len(ospecs), len(scratch)
    rin, rout = len(rider.ins), len(rider.out_shapes)

    def wrapped(*refs):
        h_in, r_in = refs[:nin], refs[nin:nin + rin]
        p = nin + rin
        h_out, r_out = refs[p:p + nout], refs[p + nout:p + nout + rout]
        p += nout + rout
        h_scr, r_scr = refs[p:p + nscr], refs[p + nscr:]
        ids = [pl.program_id(a) for a in range(len(grid))]
        first = functools.reduce(jnp.logical_and, [i == 0 for i in ids])
        last = functools.reduce(jnp.logical_and, [i == g - 1 for i, g in zip(ids, grid)])

        @pl.when(first)
        def _():
            rider.start(r_in, r_out, r_scr)

        body(*h_in, *h_out, *h_scr)

        @pl.when(last)
        def _():
            rider.finish(r_in, r_out, r_scr)

    call = pl.pallas_call(
        wrapped, name=name, out_shape=oshapes + rider.out_shapes,
        in_specs=list(in_specs) + [_ANY] * rin, out_specs=ospecs + [_ANY] * rout,
        scratch_shapes=list(scratch) + rider.scratch, input_output_aliases=aliases or {},
        compiler_params=cparams, **kw)

    def run(*operands):
        res = call(*operands, *rider.ins)
        host = res[0] if single else list(res[:nout])
        return host, list(res[nout:])

    return run


def _sig(x):
    return 1.0 / (1.0 + jnp.exp(-x))


_ANY = pl.BlockSpec(memory_space=pl.ANY)


def _lanes_of(ref):
    parts = [ref[p] for p in range(ref.shape[0])]
    return parts[0] if len(parts) == 1 else jnp.concatenate(parts, axis=1)


def _mm(a, b, mode, *, m, n, k, tm, tn, tk, out_dtype, name, epi=None, extras=(), b_off=(0, 0), rider=None,
        split="", vecs=(), a_pro=None, row_sum=False):
    nk = k // tk
    assert m % tm == 0 and n % tn == 0 and k % tk == 0
    o0, o1 = b_off
    if mode == "nn":
        a_spec = pl.BlockSpec((tm, tk), lambda i, j, kk: (i, kk))
        b_spec = pl.BlockSpec((tk, tn), lambda i, j, kk: (kk + o0, j + o1))
        dn = (((1,), (0,)), ((), ()))
    elif mode == "nt":
        a_spec = pl.BlockSpec((tm, tk), lambda i, j, kk: (i, kk))
        if "a" in split:
            a_spec = pl.BlockSpec((tk // _LANES, tm, _LANES), lambda i, j, kk: (kk, i, 0))
        b_spec = pl.BlockSpec((tn, tk), lambda i, j, kk: (j + o0, kk + o1))
        dn = (((1,), (1,)), ((), ()))
    else:
        a_spec = pl.BlockSpec((tk, tm), lambda i, j, kk: (kk, i))
        b_spec = pl.BlockSpec((tk, tn), lambda i, j, kk: (kk + o0, j + o1))
        if "b" in split:
            b_spec = pl.BlockSpec((tn // _LANES, tk, _LANES), lambda i, j, kk: (j, kk, 0))
        dn = (((0,), (0,)), ((), ()))
    o_spec = pl.BlockSpec((tm, tn), lambda i, j, kk: (i, j))
    e_spec = o_spec
    if "o" in split:
        o_spec = pl.BlockSpec((tn // _LANES, tm, _LANES), lambda i, j, kk: (j, i, 0))
    v_spec = pl.BlockSpec((1, tn), lambda i, j, kk: (0, j))
    ne = len(extras) + len(vecs)
    multi = isinstance(out_dtype, tuple)
    dts = out_dtype if multi else (out_dtype,)
    no = len(dts)
    nr = 1 if row_sum else 0
    assert not row_sum or tn == n

    def body(a_ref, b_ref, *rest):
        ex, o_refs = rest[:ne], rest[ne:ne + no]
        av = _lanes_of(a_ref) if "a" in split else a_ref[...]
        bv = _lanes_of(b_ref) if "b" in split else b_ref[...]
        if av.dtype != BF16:
            av = av.astype(BF16)
        if bv.dtype != BF16:
            bv = bv.astype(BF16)
        if a_pro is not None:
            av = a_pro(av)
        p = lax.dot_general(av, bv, dn, preferred_element_type=F32)

        def fin(x):
            if epi is not None:
                x = epi(x, *[e[...] for e in ex])
            if row_sum:
                row, x = x[-1], (x[:-1] if multi else x[0])
                row_ref = rest[ne + no]

                @pl.when(pl.program_id(0) == 0)
                def _():
                    row_ref[...] = row

                @pl.when(pl.program_id(0) > 0)
                def _():
                    row_ref[...] += row

            for o_ref, val, dt in zip(o_refs, x if multi else (x,), dts):
                if "o" in split:
                    for p in range(tn // _LANES):
                        o_ref[p] = val[:, p * _LANES:(p + 1) * _LANES].astype(dt)
                else:
                    o_ref[...] = val.astype(dt)

        if nk == 1:
            fin(p)
        else:
            acc = rest[ne + no + nr]
            kk = pl.program_id(2)

            @pl.when(kk == 0)
            def _():
                acc[...] = p

            @pl.when(kk > 0)
            def _():
                acc[...] += p

            @pl.when(kk == nk - 1)
            def _():
                fin(acc[...])

    oshape = (n // _LANES, m, _LANES) if "o" in split else (m, n)
    shapes = [jax.ShapeDtypeStruct(oshape, dt) for dt in dts]
    ospecs = [o_spec] * no
    if row_sum:
        shapes.append(jax.ShapeDtypeStruct((1, n), F32))
        ospecs.append(v_spec)
    lone = not multi and not row_sum
    return _pcall(
        body, name=name, grid=(m // tm, n // tn, nk),
        in_specs=[a_spec, b_spec] + [e_spec] * len(extras) + [v_spec] * len(vecs),
        out_specs=ospecs[0] if lone else ospecs, out_shape=shapes[0] if lone else shapes,
        scratch=[pltpu.VMEM((tm, tn), F32)] if nk > 1 else [], rider=rider,
    )(a, b, *extras, *vecs)


def _dx_qkv(dq, dk, dv, w, prev, *, tm, tk, col0, name, rider=None):
    npair, s, _ = dq.shape
    n = w.shape[0]
    per = npair * _LANES // tk
    nk = 3 * per
    off = col0 // tk
    nt_dims = (((1,), (1,)), ((), ()))

    def a_spec(t):
        return pl.BlockSpec((tk // _LANES, tm, _LANES), lambda i, kk: (jnp.clip(kk - per * t, 0, per - 1), i, 0))

    def body(a0, a1, a2, b_ref, e_ref, o_ref, acc):
        kk = pl.program_id(1)
        bv = b_ref[...]

        @pl.when(kk == 0)
        def _():
            acc[...] = e_ref[...]

        for t, a_ref in enumerate((a0, a1, a2)):
            @pl.when(jnp.logical_and(kk >= per * t, kk < per * (t + 1)))
            def _():
                acc[...] += lax.dot_general(_lanes_of(a_ref).astype(BF16), bv, nt_dims, preferred_element_type=F32)

        @pl.when(kk == nk - 1)
        def _():
            o_ref[...] = acc[...]

    row = pl.BlockSpec((tm, n), lambda i, kk: (i, 0))
    return _pcall(
        body, name=name, grid=(s // tm, nk),
        in_specs=[a_spec(0), a_spec(1), a_spec(2), pl.BlockSpec((n, tk), lambda i, kk: (0, kk + off)), row],
        out_specs=row, out_shape=jax.ShapeDtypeStruct((s, n), F32),
        scratch=[pltpu.VMEM((tm, n), F32)], rider=rider)(dq, dk, dv, w, prev)


def _epi_add(x, r):
    return x + r


def _epi_relu(x):
    return jnp.maximum(x, 0.0)


def _square(x):
    return x * x


def _epi_relu_sq_bwd(da, zr):
    return da * (2.0 * zr.astype(F32))


def _epi_add_rms(x, r, g):
    h = x + r
    return h, h * lax.rsqrt(jnp.mean(h * h, axis=-1, keepdims=True) + _RMS_EPS) * g


def _epi_rms_bwd(dn, h, dh_in, g):
    dx, dg = _rms_bwd_math(dn, h, g)
    dh = dh_in + dx
    return dh, dh, dg


_ROW_T = 512


def _rms_fwd(h, g, name, rider=None):
    s, d = h.shape

    def body(h_ref, g_ref, o_ref):
        x = h_ref[...]
        r = lax.rsqrt(jnp.mean(x * x, axis=-1, keepdims=True) + _RMS_EPS)
        o_ref[...] = (x * r * g_ref[...]).astype(BF16)

    row = pl.BlockSpec((_ROW_T, d), lambda i: (i, 0))
    vec = pl.BlockSpec((1, d), lambda i: (0, 0))
    return _pcall(body, name=name, grid=(s // _ROW_T,), in_specs=[row, vec], out_specs=row,
                  out_shape=jax.ShapeDtypeStruct((s, d), BF16), rider=rider)(h, g)


def _rms_bwd_math(dn, x, g):
    r = lax.rsqrt(jnp.mean(x * x, axis=-1, keepdims=True) + _RMS_EPS)
    xhat = x * r
    dg = jnp.sum(dn * xhat, axis=0, keepdims=True)
    t = dn * g
    dx = r * (t - xhat * jnp.mean(t * xhat, axis=-1, keepdims=True))
    return dx, dg


def _rms_bwd(dn, h, g, dh_in, name, rider=None):
    s, d = h.shape

    def body(dn_ref, h_ref, g_ref, dhi_ref, dh_ref, dhb_ref, dg_ref):
        dx, dg = _rms_bwd_math(dn_ref[...], h_ref[...], g_ref[...])
        dh = dhi_ref[...] + dx
        dh_ref[...] = dh
        dhb_ref[...] = dh.astype(BF16)

        @pl.when(pl.program_id(0) == 0)
        def _():
            dg_ref[...] = jnp.zeros_like(dg_ref)

        dg_ref[...] += dg

    row = pl.BlockSpec((_ROW_T, d), lambda i: (i, 0))
    vec = pl.BlockSpec((1, d), lambda i: (0, 0))
    return _pcall(
        body, name=name, grid=(s // _ROW_T,), in_specs=[row, row, vec, row], out_specs=[row, row, vec],
        out_shape=[jax.ShapeDtypeStruct((s, d), F32), jax.ShapeDtypeStruct((s, d), BF16),
                   jax.ShapeDtypeStruct((1, d), F32)], rider=rider)(dn, h, g, dh_in)


def _loss_head(h, tgt, g):
    s, d = h.shape

    def body(h_ref, t_ref, g_ref, dh_ref, dhb_ref, dg_ref, loss_ref):
        x, gv = h_ref[...], g_ref[...]
        r = lax.rsqrt(jnp.mean(x * x, axis=-1, keepdims=True) + _RMS_EPS)
        err = x * r * gv - t_ref[...]
        part = 0.5 * jnp.sum(jnp.mean(err * err, axis=-1, keepdims=True))
        dx, dg = _rms_bwd_math(err * (1.0 / d), x, gv)
        dh_ref[...] = dx
        dhb_ref[...] = dx.astype(BF16)

        @pl.when(pl.program_id(0) == 0)
        def _():
            dg_ref[...] = jnp.zeros_like(dg_ref)
            loss_ref[...] = jnp.zeros_like(loss_ref)

        dg_ref[...] += dg
        loss_ref[...] += jnp.full(loss_ref.shape, part, F32)

    row = pl.BlockSpec((_ROW_T, d), lambda i: (i, 0))
    vec = pl.BlockSpec((1, d), lambda i: (0, 0))
    one = pl.BlockSpec((1, _LANES), lambda i: (0, 0))
    return _pcall(
        body, name="loss_head", grid=(s // _ROW_T,), in_specs=[row, row, vec], out_specs=[row, row, vec, one],
        out_shape=[jax.ShapeDtypeStruct((s, d), F32), jax.ShapeDtypeStruct((s, d), BF16),
                   jax.ShapeDtypeStruct((1, d), F32), jax.ShapeDtypeStruct((1, _LANES), F32)])(h, tgt, g)


_CONV_T = 256
_CONV_RC = 64


def _conv_a_specs(s):
    c = _GROUP_COLS
    hb = _CONV_T // _HALO
    val = pl.BlockSpec((_CONV_T, c), lambda i: (i, 0))
    gate = pl.BlockSpec((_CONV_T, c), lambda i: (i, 1))
    hval = pl.BlockSpec((_HALO, c), lambda i: (jnp.maximum(i * hb - 1, 0), 0))
    hgate = pl.BlockSpec((_HALO, c), lambda i: (jnp.maximum(i * hb - 1, 0), 1))
    return val, gate, hval, hgate


def _fill_glu(val_ref, gate_ref, hval_ref, hgate_ref, hs_ref):
    i = pl.program_id(0)
    hs_ref[pl.ds(_HALO, _CONV_T), :] = val_ref[...].astype(F32) * _sig(gate_ref[...].astype(F32))
    halo = hval_ref[...].astype(F32) * _sig(hgate_ref[...].astype(F32))
    hs_ref[pl.ds(0, _HALO), :] = jnp.where(i > 0, halo, 0.0)


_SHIFT_ROWS = _CONV_T + _HALO - 8


def _fill_shifts(src_ref, sh_ref):
    for b in range(1, 8):
        sh_ref[b - 1] = src_ref[pl.ds(b, _SHIFT_ROWS), :]


def _tap_rows(src_ref, sh_ref, start, rows, lanes=slice(None)):
    b = start % 8
    if b == 0:
        return src_ref[pl.ds(start, rows), lanes]
    return sh_ref[b - 1, pl.ds(start - b, rows), lanes]


def _conv_rows(hs_ref, sh_ref, w_ref, r0, rows):
    off = _HALO - (_CONV_K - 1)
    acc = jnp.zeros((rows, _GROUP_COLS), F32)
    for kk in range(_CONV_K):
        acc = acc + w_ref[kk:kk + 1, :] * _tap_rows(hs_ref, sh_ref, r0 + off + kk, rows)
    return acc


def _ln_fwd(ca, g, b):
    mu = jnp.mean(ca, axis=-1, keepdims=True)
    xc = ca - mu
    rstd = lax.rsqrt(jnp.mean(xc * xc, axis=-1, keepdims=True) + _LN_EPS)
    xhat = xc * rstd
    return xhat, rstd, xhat * g + b


def _conv_a_fwd(uc, w, cb, lg, lb, rider=None):
    s = uc.shape[0]
    c = _GROUP_COLS

    def body(val_ref, gate_ref, hval_ref, hgate_ref, w_ref, cb_ref, lg_ref, lb_ref, o_ref, ca_ref, hs_ref, sh_ref):
        _fill_glu(val_ref, gate_ref, hval_ref, hgate_ref, hs_ref)
        _fill_shifts(hs_ref, sh_ref)
        for rc in range(_CONV_T // _CONV_RC):
            r0 = rc * _CONV_RC
            ca = _conv_rows(hs_ref, sh_ref, w_ref, r0, _CONV_RC) + cb_ref[...]
            ca_ref[pl.ds(r0, _CONV_RC), :] = ca
            _, _, ln = _ln_fwd(ca, lg_ref[...], lb_ref[...])
            o_ref[pl.ds(r0, _CONV_RC), :] = (ln * _sig(ln)).astype(BF16)

    val, gate, hval, hgate = _conv_a_specs(s)
    wspec = pl.BlockSpec((_CONV_K, c), lambda i: (0, 0))
    vec = pl.BlockSpec((1, c), lambda i: (0, 0))
    blk = pl.BlockSpec((_CONV_T, c), lambda i: (i, 0))
    return _pcall(
        body, name="conv_a_fwd", grid=(s // _CONV_T,),
        in_specs=[val, gate, hval, hgate, wspec, vec, vec, vec],
        out_specs=[blk, blk],
        out_shape=[jax.ShapeDtypeStruct((s, 2 * c), BF16), jax.ShapeDtypeStruct((s, c), F32)],
        scratch=[pltpu.VMEM((_CONV_T + _HALO, c), F32), pltpu.VMEM((7, _SHIFT_ROWS, c), F32)],
        rider=rider)(uc, uc, uc, uc, w, cb, lg, lb)


def _conv_a_bwd_ln(ca_all, dcat, lg, lb, rider=None):
    s = ca_all.shape[0]
    c = _GROUP_COLS

    def body(ca_ref, dy_ref, lg_ref, lb_ref, dca_ref, st_ref):
        @pl.when(pl.program_id(0) == 0)
        def _():
            st_ref[...] = jnp.zeros_like(st_ref)

        for rc in range(_CONV_T // _CONV_RC):
            r0 = rc * _CONV_RC
            ca = ca_ref[pl.ds(r0, _CONV_RC), :]
            xhat, rstd, ln = _ln_fwd(ca, lg_ref[...], lb_ref[...])
            sg = _sig(ln)
            dln = dy_ref[pl.ds(r0, _CONV_RC), :] * (sg * (1.0 + ln * (1.0 - sg)))
            dxh = dln * lg_ref[...]
            dca = rstd * (dxh - jnp.mean(dxh, axis=-1, keepdims=True)
                          - xhat * jnp.mean(dxh * xhat, axis=-1, keepdims=True))
            dca_ref[pl.ds(r0, _CONV_RC), :] = dca
            st_ref[0:1, :] += jnp.sum(dca, axis=0, keepdims=True)
            st_ref[1:2, :] += jnp.sum(dln * xhat, axis=0, keepdims=True)
            st_ref[2:3, :] += jnp.sum(dln, axis=0, keepdims=True)

    blk = pl.BlockSpec((_CONV_T, c), lambda i: (i, 0))
    vec = pl.BlockSpec((1, c), lambda i: (0, 0))
    st = pl.BlockSpec((8, c), lambda i: (0, 0))
    return _pcall(
        body, name="conv_a_bwd_ln", grid=(s // _CONV_T,),
        in_specs=[blk, blk, vec, vec], out_specs=[blk, st],
        out_shape=[jax.ShapeDtypeStruct((s, c), F32), jax.ShapeDtypeStruct((8, c), F32)],
        rider=rider)(ca_all, dcat, lg, lb)


def _conv_a_bwd_conv(uc, dca, w, rider=None):
    s = uc.shape[0]
    c = _GROUP_COLS
    nblk = s // _CONV_T
    hb = _CONV_T // _HALO
    off = _HALO - (_CONV_K - 1)

    def body(val_ref, gate_ref, hval_ref, hgate_ref, d_ref, dn_ref, w_ref, du_ref, dw_ref, hs_ref, ds_ref,
             hsh_ref, dsh_ref):
        i = pl.program_id(0)
        _fill_glu(val_ref, gate_ref, hval_ref, hgate_ref, hs_ref)
        ds_ref[pl.ds(0, _CONV_T), :] = d_ref[...]
        ds_ref[pl.ds(_CONV_T, _HALO), :] = jnp.where(i < nblk - 1, dn_ref[...], 0.0)
        _fill_shifts(hs_ref, hsh_ref)
        _fill_shifts(ds_ref, dsh_ref)

        @pl.when(i == 0)
        def _():
            dw_ref[...] = jnp.zeros_like(dw_ref)

        for rc in range(_CONV_T // _CONV_RC):
            r0 = rc * _CONV_RC
            dcur = ds_ref[pl.ds(r0, _CONV_RC), :]
            dh = jnp.zeros((_CONV_RC, c), F32)
            for kk in range(_CONV_K):
                dh = dh + w_ref[kk:kk + 1, :] * _tap_rows(ds_ref, dsh_ref, r0 + _CONV_K - 1 - kk, _CONV_RC)
                dw_ref[kk:kk + 1, :] += jnp.sum(dcur * _tap_rows(hs_ref, hsh_ref, r0 + off + kk, _CONV_RC),
                                                 axis=0, keepdims=True)
            v = val_ref[pl.ds(r0, _CONV_RC), :].astype(F32)
            sg = _sig(gate_ref[pl.ds(r0, _CONV_RC), :].astype(F32))
            du_ref[pl.ds(r0, _CONV_RC), pl.ds(0, c)] = (dh * sg).astype(BF16)
            du_ref[pl.ds(r0, _CONV_RC), pl.ds(c, c)] = (dh * v * sg * (1.0 - sg)).astype(BF16)

    val, gate, hval, hgate = _conv_a_specs(s)
    blk = pl.BlockSpec((_CONV_T, c), lambda i: (i, 0))
    nxt = pl.BlockSpec((_HALO, c), lambda i: (jnp.minimum((i + 1) * hb, s // _HALO - 1), 0))
    wspec = pl.BlockSpec((_CONV_K, c), lambda i: (0, 0))
    return _pcall(
        body, name="conv_a_bwd_conv", grid=(nblk,),
        in_specs=[val, gate, hval, hgate, blk, nxt, wspec],
        out_specs=[pl.BlockSpec((_CONV_T, 2 * c), lambda i: (i, 0)), pl.BlockSpec((_HALO, c), lambda i: (0, 0))],
        out_shape=[jax.ShapeDtypeStruct((s, 2 * c), BF16), jax.ShapeDtypeStruct((_HALO, c), F32)],
        scratch=[pltpu.VMEM((_CONV_T + _HALO, c), F32), pltpu.VMEM((_CONV_T + _HALO, c), F32),
                 pltpu.VMEM((7, _SHIFT_ROWS, c), F32), pltpu.VMEM((7, _SHIFT_ROWS, c), F32)],
        rider=rider,
    )(uc, uc, uc, uc, dca, dca, w)


_SC_T = 256
_SC_RC = 32
_SC_LC = 512


def _sc_chunks(d):
    return [(pl.ds(r0, _SC_RC), pl.ds(l0, _SC_LC)) for r0 in range(0, _SC_T, _SC_RC) for l0 in range(0, d, _SC_LC)]


def _short_conv_fwd(u2, w):
    s, d3 = u2.shape
    d = d3 // 3
    hb = _SC_T // _SC_HALO

    def body(b_ref, c_ref, v_ref, hc_ref, hv_ref, w_ref, o_ref, cs_ref):
        i = pl.program_id(0)
        cs_ref[pl.ds(0, _SC_HALO), :] = jnp.where(i > 0, hc_ref[...].astype(F32) * hv_ref[...].astype(F32), 0.0)
        for rows, lanes in _sc_chunks(d):
            cs_ref[pl.ds(_SC_HALO + rows.start, _SC_RC), lanes] = (
                c_ref[rows, lanes].astype(F32) * v_ref[rows, lanes].astype(F32))
        for rows, lanes in _sc_chunks(d):
            taps = [cs_ref[pl.ds(_SC_HALO - 2 + kk + rows.start, _SC_RC), lanes] for kk in range(3)]
            conv = w_ref[0:1, lanes] * taps[0] + w_ref[1:2, lanes] * taps[1] + w_ref[2:3, lanes] * taps[2]
            o_ref[rows, lanes] = (b_ref[rows, lanes].astype(F32) * conv).astype(BF16)

    def col(j):
        return pl.BlockSpec((_SC_T, d), lambda i: (i, j))

    def halo(j):
        return pl.BlockSpec((_SC_HALO, d), lambda i: (jnp.maximum(i * hb - 1, 0), j))

    return _pcall(
        body, name="short_conv_fwd", grid=(s // _SC_T,),
        in_specs=[col(0), col(1), col(2), halo(1), halo(2), pl.BlockSpec((3, d), lambda i: (0, 0))],
        out_specs=pl.BlockSpec((_SC_T, d), lambda i: (i, 0)),
        out_shape=jax.ShapeDtypeStruct((s, d), BF16),
        scratch=[pltpu.VMEM((_SC_T + _SC_HALO, d), F32)])(u2, u2, u2, u2, u2, w)


def _short_conv_bwd(u2, dsc, w, rider=None):
    s, d3 = u2.shape
    d = d3 // 3
    hb = _SC_T // _SC_HALO
    nblk = s // _SC_T

    def body(b_ref, c_ref, v_ref, hc_ref, hv_ref, nb_ref, d_ref, nd_ref, w_ref, du_ref, dw_ref, cs_ref, ds_ref):
        i = pl.program_id(0)
        cs_ref[pl.ds(0, _SC_HALO), :] = jnp.where(i > 0, hc_ref[...].astype(F32) * hv_ref[...].astype(F32), 0.0)
        ds_ref[pl.ds(_SC_T, _SC_HALO), :] = jnp.where(i < nblk - 1, nd_ref[...] * nb_ref[...].astype(F32), 0.0)
        for rows, lanes in _sc_chunks(d):
            cs_ref[pl.ds(_SC_HALO + rows.start, _SC_RC), lanes] = (
                c_ref[rows, lanes].astype(F32) * v_ref[rows, lanes].astype(F32))
            ds_ref[rows, lanes] = d_ref[rows, lanes] * b_ref[rows, lanes].astype(F32)

        @pl.when(i == 0)
        def _():
            dw_ref[...] = jnp.zeros_like(dw_ref)

        for l0 in range(0, d, _SC_LC):
            lanes = pl.ds(l0, _SC_LC)
            dw_acc = [jnp.zeros((8, _SC_LC), F32)] * 3
            for r0 in range(0, _SC_T, _SC_RC):
                rows = pl.ds(r0, _SC_RC)
                taps = [cs_ref[pl.ds(_SC_HALO - 2 + kk + r0, _SC_RC), lanes] for kk in range(3)]
                conv = w_ref[0:1, lanes] * taps[0] + w_ref[1:2, lanes] * taps[1] + w_ref[2:3, lanes] * taps[2]
                dconv = ds_ref[rows, lanes]
                dcv = (w_ref[2:3, lanes] * dconv + w_ref[1:2, lanes] * ds_ref[pl.ds(r0 + 1, _SC_RC), lanes]
                       + w_ref[0:1, lanes] * ds_ref[pl.ds(r0 + 2, _SC_RC), lanes])
                du_ref[rows, lanes] = (d_ref[rows, lanes] * conv).astype(BF16)
                du_ref[rows, pl.ds(d + l0, _SC_LC)] = (dcv * v_ref[rows, lanes].astype(F32)).astype(BF16)
                du_ref[rows, pl.ds(2 * d + l0, _SC_LC)] = (dcv * c_ref[rows, lanes].astype(F32)).astype(BF16)
                for kk in range(3):
                    prod = dconv * taps[kk]
                    dw_acc[kk] = dw_acc[kk] + sum(prod[t:t + 8] for t in range(0, _SC_RC, 8))
            for kk in range(3):
                dw_ref[kk:kk + 1, lanes] += jnp.sum(dw_acc[kk], axis=0, keepdims=True)

    def col(j):
        return pl.BlockSpec((_SC_T, d), lambda i: (i, j))

    def halo(j):
        return pl.BlockSpec((_SC_HALO, d), lambda i: (jnp.maximum(i * hb - 1, 0), j))

    def nxt(j):
        return pl.BlockSpec((_SC_HALO, d), lambda i: (jnp.minimum((i + 1) * hb, s // _SC_HALO - 1), j))

    return _pcall(
        body, name="short_conv_bwd", grid=(nblk,),
        in_specs=[col(0), col(1), col(2), halo(1), halo(2), nxt(0), col(0), nxt(0),
                  pl.BlockSpec((3, d), lambda i: (0, 0))],
        out_specs=[pl.BlockSpec((_SC_T, d3), lambda i: (i, 0)), pl.BlockSpec((8, d), lambda i: (0, 0))],
        out_shape=[jax.ShapeDtypeStruct((s, d3), BF16), jax.ShapeDtypeStruct((8, d), F32)],
        scratch=[pltpu.VMEM((_SC_T + _SC_HALO, d), F32), pltpu.VMEM((_SC_T + _SC_HALO, d), F32)],
        rider=rider,
    )(u2, u2, u2, u2, u2, u2, dsc, dsc, w)


def _bucket_maps():
    a_idx = jnp.arange(_STEPS)[:, None]
    c_idx = jnp.arange(2 * _STEPS)[None, :]
    mdist = jnp.clip(a_idx + _STEPS - c_idx, 0, _STEPS)
    max_exact = _NUM_BUCKETS // 2
    maps = []
    for _, dil in _GROUPS:
        nn = mdist * dil
        nf = jnp.maximum(nn, 1).astype(F32)
        large = max_exact + (jnp.log(nf / max_exact) / math.log(_MAX_DISTANCE / max_exact)
                             * (_NUM_BUCKETS - max_exact)).astype(jnp.int32)
        maps.append(jnp.where(nn < max_exact, nn, jnp.minimum(large, _NUM_BUCKETS - 1)).astype(jnp.int32))
    return jnp.stack(maps, axis=0)


def _bias_expand(rel_bias, buckets):
    nh = rel_bias.shape[1]

    def body(rb_ref, bk_ref, o_ref):
        h = pl.program_id(0)
        bk = bk_ref[0]
        acc = jnp.zeros(bk.shape, F32)
        for b in range(_NUM_BUCKETS):
            acc = jnp.where(bk == b, rb_ref[b, h], acc)
        a = lax.broadcasted_iota(jnp.int32, bk.shape, 0)
        c = lax.broadcasted_iota(jnp.int32, bk.shape, 1)
        mdist = a + _STEPS - c
        o_ref[0] = jnp.where((mdist >= 0) & (mdist <= _STEPS), acc, _NEG)

    return _pcall(
        body, name="bias_expand", grid=(nh,),
        in_specs=[pl.BlockSpec(memory_space=pltpu.SMEM),
                  pl.BlockSpec((1, _STEPS, 2 * _STEPS), lambda h: (h // 8, 0, 0))],
        out_specs=pl.BlockSpec((1, _STEPS, 2 * _STEPS), lambda h: (h, 0, 0)),
        out_shape=jax.ShapeDtypeStruct((nh, _STEPS, 2 * _STEPS), F32))(rel_bias, buckets)


def _bias_reduce(ds_all, buckets):
    nh = ds_all.shape[0]

    def body(ds_ref, bk_ref, o_ref):
        t, bk = ds_ref[0], bk_ref[0]
        rows = lax.broadcasted_iota(jnp.int32, (_NUM_BUCKETS, _LANES), 0)
        out = jnp.zeros((_NUM_BUCKETS, _LANES), F32)
        for b in range(_NUM_BUCKETS):
            out = jnp.where(rows == b, jnp.sum(jnp.where(bk == b, t, 0.0)), out)
        o_ref[0] = out

    blk = pl.BlockSpec((1, _STEPS, 2 * _STEPS), lambda h: (h, 0, 0))
    return _pcall(
        body, name="bias_reduce", grid=(nh,),
        in_specs=[blk, pl.BlockSpec((1, _STEPS, 2 * _STEPS), lambda h: (h // 8, 0, 0))],
        out_specs=pl.BlockSpec((1, _NUM_BUCKETS, _LANES), lambda h: (h, 0, 0)),
        out_shape=jax.ShapeDtypeStruct((nh, _NUM_BUCKETS, _LANES), F32))(ds_all, buckets)


def _sub_residues(dil):
    return 4 if dil % 16 == 0 else 1


def _strided_rows(ref, tmp_ref, p, r, dil):
    sub = _sub_residues(dil)
    if dil == 1:
        return [ref[p]]
    if sub == 1:
        return [ref[p, pl.ds(r, _STEPS, stride=dil), :]]
    tmp_ref[...] = ref[p, pl.ds(r, _STEPS * sub, stride=dil // sub), :]
    return [tmp_ref[pl.ds(q, _STEPS, stride=sub), :] for q in range(sub)]


def _store_strided(ref, tmp_ref, p, r, dil, vals):
    sub = _sub_residues(dil)
    if dil == 1:
        ref[p] = vals[0]
    elif sub == 1:
        ref[p, pl.ds(r, _STEPS, stride=dil), :] = vals[0]
    else:
        for q, val in enumerate(vals):
            tmp_ref[pl.ds(q, _STEPS, stride=sub), :] = val
        ref[p, pl.ds(r, _STEPS * sub, stride=dil // sub), :] = tmp_ref[...]


def _tmp_rows(dil, count):
    sub = _sub_residues(dil)
    return [pltpu.VMEM((_STEPS * sub, _LANES), F32)] * count if sub > 1 else []


def _head_masks():
    lane = lax.broadcasted_iota(jnp.int32, (1, _LANES), 1)
    return [lane < _HEAD_DIM, lane >= _HEAD_DIM]


def _stack_heads(x2, masks):
    return jnp.concatenate([jnp.where(masks[0], x2, 0), jnp.where(masks[1], x2, 0)], axis=0)


def _unstack_heads(y, masks):
    return jnp.where(masks[0], y[:_STEPS], y[_STEPS:])


def _scores(qs2, k2, b_ref, j, first):
    sc = lax.dot_general(qs2, k2, (((1,), (1,)), ((), ())), preferred_element_type=F32)
    sc = sc * (_HEAD_DIM ** -0.5) + jnp.concatenate([b_ref[2 * j], b_ref[2 * j + 1]], axis=0)
    col = lax.broadcasted_iota(jnp.int32, sc.shape, 1)
    return jnp.where(jnp.logical_and(first, col < _STEPS), _NEG, sc)


_PAIRS = _GROUP_COLS // _LANES


def _attn_fwd(uq, uk, uv, bias, g, dil, pp, rider=None):
    s = uq.shape[1]
    rb = _STEPS * dil
    nb = s // rb
    npb = _PAIRS // pp

    sub = _sub_residues(dil)

    def body(q_ref, kc_ref, kp_ref, vc_ref, vp_ref, b_ref, o_ref, l_ref, *tmp):
        tmp = tmp + (None,) * 7
        n, r = pl.program_id(1), pl.program_id(2)
        first = n == 0
        masks = _head_masks()
        for j in range(pp):
            qs = _strided_rows(q_ref, tmp[0], j, r, dil)
            kps, kcs = _strided_rows(kp_ref, tmp[1], j, r, dil), _strided_rows(kc_ref, tmp[2], j, r, dil)
            vps, vcs = _strided_rows(vp_ref, tmp[3], j, r, dil), _strided_rows(vc_ref, tmp[4], j, r, dil)
            o_res, l_res = [], []
            for q in range(sub):
                q2 = qs[q].astype(BF16)
                k2 = jnp.concatenate([kps[q], kcs[q]], axis=0).astype(BF16)
                v2 = jnp.concatenate([vps[q], vcs[q]], axis=0).astype(BF16)
                sc = _scores(_stack_heads(q2, masks), k2, b_ref, j, first)
                mx = jnp.max(sc, axis=-1, keepdims=True)
                p = jnp.exp(sc - mx)
                den = jnp.sum(p, axis=-1, keepdims=True)
                o2 = jnp.dot(p.astype(BF16), v2, preferred_element_type=F32) / den
                o_res.append(_unstack_heads(o2, masks))
                l_res.append(_unstack_heads(jnp.broadcast_to(mx + jnp.log(den), o2.shape), masks))
            _store_strided(o_ref, tmp[5], j, r, dil, o_res)
            _store_strided(l_ref, tmp[6], j, r, dil, l_res)

    cur = pl.BlockSpec((pp, rb, _LANES), lambda hb, n, r: (g * npb + hb, n, 0))
    prev = pl.BlockSpec((pp, rb, _LANES), lambda hb, n, r: (g * npb + hb, jnp.maximum(n - 1, 0), 0))
    bspec = pl.BlockSpec((2 * pp, _STEPS, 2 * _STEPS), lambda hb, n, r: (g * npb + hb, 0, 0))
    ospec = pl.BlockSpec((pp, rb, _LANES), lambda hb, n, r: (hb, n, 0))
    sh = jax.ShapeDtypeStruct((_PAIRS, s, _LANES), F32)
    return _pcall(
        body, name=f"attn_fwd_g{g}", grid=(npb, nb, dil // sub),
        in_specs=[cur, cur, prev, cur, prev, bspec], out_specs=[ospec, ospec], out_shape=[sh, sh],
        scratch=_tmp_rows(dil, 7), rider=rider,
    )(uq, uk, uk, uv, uv, bias)


def _attn_merge(outs, lses, cat):
    s = outs[0].shape[1]
    c = _GROUP_COLS

    def body(o0, o1, o2, l0, l1, l2, cat_in, cat_ref, lse_ref):
        del cat_in
        a0, a1, a2 = l0[...], l1[...], l2[...]
        mx = jnp.maximum(jnp.maximum(a0, a1), a2)
        w0, w1, w2 = jnp.exp(a0 - mx), jnp.exp(a1 - mx), jnp.exp(a2 - mx)
        den = w0 + w1 + w2
        y = ((w0 * o0[...] + w1 * o1[...] + w2 * o2[...]) / den).astype(BF16)
        for p in range(_PAIRS):
            cat_ref[:, p * _LANES:(p + 1) * _LANES] = y[p]
        lse_ref[...] = mx + jnp.log(den)

    blk = pl.BlockSpec((_PAIRS, _ROW_T, _LANES), lambda i: (0, i, 0))
    return _pcall(
        body, name="attn_merge", grid=(s // _ROW_T,),
        in_specs=[blk] * 6 + [_ANY],
        out_specs=[pl.BlockSpec((_ROW_T, c), lambda i: (i, 1)), blk],
        out_shape=[jax.ShapeDtypeStruct(cat.shape, BF16), jax.ShapeDtypeStruct((_PAIRS, s, _LANES), F32)],
        aliases={6: 0})(*outs, *lses, cat)


def _attn_delta(dcat, cat):
    s = dcat.shape[0]
    c = _GROUP_COLS
    seg = (jnp.arange(c)[:, None] // _HEAD_DIM == jnp.arange(c)[None, :] // _HEAD_DIM).astype(BF16)

    def body(dy_ref, y_ref, seg_ref, dl_ref, dys_ref):
        dy = dy_ref[...]
        prod = dy * y_ref[...].astype(F32)
        hi = prod.astype(BF16)
        lo = (prod - hi.astype(F32)).astype(BF16)
        dl = (jnp.dot(hi, seg_ref[...], preferred_element_type=F32)
              + jnp.dot(lo, seg_ref[...], preferred_element_type=F32))
        for p in range(_PAIRS):
            dl_ref[p] = dl[:, p * _LANES:(p + 1) * _LANES]
            dys_ref[p] = dy[:, p * _LANES:(p + 1) * _LANES]

    right = pl.BlockSpec((_ROW_T, c), lambda i: (i, 1))
    blk = pl.BlockSpec((_PAIRS, _ROW_T, _LANES), lambda i: (0, i, 0))
    sh = jax.ShapeDtypeStruct((_PAIRS, s, _LANES), F32)
    return _pcall(
        body, name="attn_delta", grid=(s // _ROW_T,),
        in_specs=[right, right, pl.BlockSpec((c, c), lambda i: (0, 0))],
        out_specs=[blk, blk], out_shape=[sh, sh])(dcat, cat, seg)


def _attn_bwd(uq, uk, uv, dys, lse, delta, bias, prev_grads, g, dil, pp, rider=None):
    s = uq.shape[1]
    rb = _STEPS * dil
    nb = s // rb
    npb = _PAIRS // pp
    scale = _HEAD_DIM ** -0.5

    sub = _sub_residues(dil)

    def body(q_ref, kc_ref, kp_ref, vc_ref, vp_ref, dy_ref, l_ref, dl_ref, b_ref, *rest):
        rest = rest[len(prev_grads):]
        dq_ref, dk_ref, dv_ref, dsa_ref, dkc_ref, dvc_ref = rest[:6]
        tmp = rest[6:] + (None,) * 11
        n, r = pl.program_id(1), pl.program_id(2)

        def carry_slot(j, q):
            return ((r + (dil // sub) * q) * pp + j) if sub > 1 else r * pp + j

        @pl.when(jnp.logical_and(n == 0, r == 0))
        def _():
            dsa_ref[...] = jnp.zeros_like(dsa_ref)

        @pl.when(n == 0)
        def _():
            for j in range(pp):
                for q in range(sub):
                    dkc_ref[carry_slot(j, q)] = jnp.zeros((_STEPS, _LANES), F32)
                    dvc_ref[carry_slot(j, q)] = jnp.zeros((_STEPS, _LANES), F32)

        @pl.when(n < nb)
        def _():
            first = n == 0
            masks = _head_masks()
            for j in range(pp):
                qs = _strided_rows(q_ref, tmp[0], j, r, dil)
                kps, kcs = _strided_rows(kp_ref, tmp[1], j, r, dil), _strided_rows(kc_ref, tmp[2], j, r, dil)
                vps, vcs = _strided_rows(vp_ref, tmp[3], j, r, dil), _strided_rows(vc_ref, tmp[4], j, r, dil)
                dys_ = _strided_rows(dy_ref, tmp[5], j, r, dil)
                lses = _strided_rows(l_ref, tmp[6], j, r, dil)
                dls = _strided_rows(dl_ref, tmp[7], j, r, dil)
                ds_sum = [jnp.zeros((_STEPS, 2 * _STEPS), F32)] * 2
                dq_res, dk_res, dv_res = [], [], []
                for q in range(sub):
                    q2 = qs[q].astype(BF16)
                    k2 = jnp.concatenate([kps[q], kcs[q]], axis=0).astype(BF16)
                    v2 = jnp.concatenate([vps[q], vcs[q]], axis=0).astype(BF16)
                    dy2 = dys_[q].astype(BF16)
                    qs2, dys2 = _stack_heads(q2, masks), _stack_heads(dy2, masks)
                    per_row = lambda st: jnp.concatenate([st[:, 0:1], st[:, _HEAD_DIM:_HEAD_DIM + 1]], axis=0)
                    sc = _scores(qs2, k2, b_ref, j, first)
                    p = jnp.exp(sc - per_row(lses[q]))
                    dp = lax.dot_general(dys2, v2, (((1,), (1,)), ((), ())), preferred_element_type=F32)
                    ds = p * (dp - per_row(dls[q]))
                    ds_sum[0] = ds_sum[0] + ds[:_STEPS]
                    ds_sum[1] = ds_sum[1] + ds[_STEPS:]
                    dsb = ds.astype(BF16)
                    dq_p = _unstack_heads(jnp.dot(dsb, k2, preferred_element_type=F32), masks)
                    tdn = (((0,), (0,)), ((), ()))
                    dk_p = lax.dot_general(dsb, qs2, tdn, preferred_element_type=F32) * scale
                    dv_p = lax.dot_general(p.astype(BF16), dys2, tdn, preferred_element_type=F32)
                    slot = carry_slot(j, q)
                    dq_res.append(dq_p * scale)
                    dk_res.append(dkc_ref[slot] + dk_p[:_STEPS])
                    dv_res.append(dvc_ref[slot] + dv_p[:_STEPS])
                    dkc_ref[slot] = dk_p[_STEPS:]
                    dvc_ref[slot] = dv_p[_STEPS:]
                for hh in range(2):
                    dsa_ref[2 * j + hh] += ds_sum[hh]
                _store_strided(dq_ref, tmp[8], j, r, dil, dq_res)
                _store_strided(dk_ref, tmp[9], j, r, dil, dk_res)
                _store_strided(dv_ref, tmp[10], j, r, dil, dv_res)

        @pl.when(n == nb)
        def _():
            for j in range(pp):
                _store_strided(dk_ref, tmp[9], j, r, dil, [dkc_ref[carry_slot(j, q)] for q in range(sub)])
                _store_strided(dv_ref, tmp[10], j, r, dil, [dvc_ref[carry_slot(j, q)] for q in range(sub)])

    def clamp(n):
        return jnp.minimum(n, nb - 1)

    cur = pl.BlockSpec((pp, rb, _LANES), lambda hb, n, r: (g * npb + hb, clamp(n), 0))
    prev = pl.BlockSpec((pp, rb, _LANES), lambda hb, n, r: (g * npb + hb, jnp.maximum(clamp(n) - 1, 0), 0))
    stat = pl.BlockSpec((pp, rb, _LANES), lambda hb, n, r: (hb, clamp(n), 0))
    bspec = pl.BlockSpec((2 * pp, _STEPS, 2 * _STEPS), lambda hb, n, r: (g * npb + hb, 0, 0))
    dkspec = pl.BlockSpec((pp, rb, _LANES), lambda hb, n, r: (g * npb + hb, jnp.maximum(n - 1, 0), 0))
    dsspec = pl.BlockSpec((2 * pp, _STEPS, 2 * _STEPS), lambda hb, n, r: (hb, 0, 0))
    wide = jax.ShapeDtypeStruct((3 * _PAIRS, s, _LANES), F32)
    np_ = len(prev_grads)
    return _pcall(
        body, name=f"attn_bwd_g{g}", grid=(npb, nb + 1, dil // sub),
        in_specs=[cur, cur, prev, cur, prev, stat, stat, stat, bspec] + [_ANY] * np_,
        out_specs=[cur, dkspec, dkspec, dsspec],
        out_shape=[wide, wide, wide, jax.ShapeDtypeStruct((8, _STEPS, 2 * _STEPS), F32)],
        scratch=[pltpu.VMEM((dil * pp, _STEPS, _LANES), F32), pltpu.VMEM((dil * pp, _STEPS, _LANES), F32)]
        + _tmp_rows(dil, 11),
        aliases={9 + t: t for t in range(np_)}, rider=rider,
    )(uq, uk, uk, uv, uv, dys, lse, delta, bias, *prev_grads)


def _place():
    x, y, c = lax.axis_index("x"), lax.axis_index("y"), lax.axis_index("c")
    chips = [(1 - x, y), (x, 1 - y), (1 - x, 1 - y)]
    return x, y, c, chips


def _slab(ref, axis, chip, width):
    start = pl.multiple_of(chip * width, width)
    if axis == 0:
        return ref.at[pl.ds(start, width), :]
    return ref.at[:, pl.ds(start, width)]


def _run_rider(rider, name):
    nin, nout = len(rider.ins), len(rider.out_shapes)

    def body(*refs):
        ins, outs, scr = refs[:nin], refs[nin:nin + nout], refs[nin + nout:]
        rider.start(ins, outs, scr)
        rider.finish(ins, outs, scr)

    return _pcall(body, name=name, in_specs=[_ANY] * nin, out_specs=[_ANY] * nout, out_shape=rider.out_shapes,
                  scratch=rider.scratch)(*rider.ins)


def _gather_halves_rider(shard, axis):
    shape = list(shard.shape)
    shape[axis] *= 4
    full = jax.ShapeDtypeStruct(tuple(shape), shard.dtype)
    half = shard.shape[0] // 2
    width = shard.shape[axis]

    def region(out, chip, core):
        if axis == 0:
            return out.at[pl.ds(pl.multiple_of(chip * width + core * half, half), half), :]
        return out.at[pl.ds(pl.multiple_of(core * half, half), half), pl.ds(pl.multiple_of(chip * width, width), width)]

    def copies(ins, outs, scr):
        send, recv, loc = scr
        (src,), (out,) = ins, outs
        x, y, c, chips = _place()
        mine = 2 * x + y
        own = pltpu.make_async_copy(src, _slab(out, axis, mine, width), loc.at[0])
        my_half = src.at[pl.ds(pl.multiple_of(c * half, half), half), :]
        over_ici, ici_in, to_sib, sib_in = [], [], [], []
        for j, (px, py) in enumerate(chips):
            theirs = 2 * px + py
            ici = dict(send_sem=send.at[j], recv_sem=recv.at[j], device_id=(px, py, c), device_id_type=MESH)
            d2d = dict(send_sem=send.at[3 + j], recv_sem=recv.at[3 + j], device_id=(x, y, 1 - c),
                       device_id_type=MESH)
            over_ici.append(pltpu.make_async_remote_copy(src_ref=my_half, dst_ref=region(out, mine, c), **ici))
            ici_in.append(pltpu.make_async_remote_copy(src_ref=my_half, dst_ref=region(out, theirs, c), **ici))
            to_sib.append(pltpu.make_async_remote_copy(
                src_ref=region(out, theirs, c), dst_ref=region(out, theirs, c), **d2d))
            sib_in.append(pltpu.make_async_remote_copy(
                src_ref=region(out, theirs, c), dst_ref=region(out, theirs, 1 - c), **d2d))
        return own, over_ici, ici_in, to_sib, sib_in

    def start(ins, outs, scr):
        own, over_ici, _, _, _ = copies(ins, outs, scr)
        own.start()
        for cp in over_ici:
            cp.start()

    def finish(ins, outs, scr):
        own, over_ici, ici_in, to_sib, sib_in = copies(ins, outs, scr)
        for j in range(3):
            ici_in[j].wait_recv()
            to_sib[j].start()
        for cp in sib_in:
            cp.wait_recv()
        own.wait()
        for cp in over_ici + to_sib:
            cp.wait_send()

    return _Rider([shard], [full], [pltpu.SemaphoreType.DMA((6,)), pltpu.SemaphoreType.DMA((6,)),
                                    pltpu.SemaphoreType.DMA((1,))], start, finish)


def _join_riders(riders):
    if len(riders) == 1:
        return riders[0]

    def parts(ins, outs, scr):
        pi = po = ps = 0
        for rd in riders:
            ni, no, ns = len(rd.ins), len(rd.out_shapes), len(rd.scratch)
            yield rd, ins[pi:pi + ni], outs[po:po + no], scr[ps:ps + ns]
            pi, po, ps = pi + ni, po + no, ps + ns

    def start(ins, outs, scr):
        for rd, i, o, sc in parts(ins, outs, scr):
            rd.start(i, o, sc)

    def finish(ins, outs, scr):
        for rd, i, o, sc in parts(ins, outs, scr):
            rd.finish(i, o, sc)

    return _Rider(sum((rd.ins for rd in riders), []), sum((rd.out_shapes for rd in riders), []),
                  sum((rd.scratch for rd in riders), []), start, finish)


def _scatter_rider(grads, axes):
    nw = len(grads)
    outs_shape = []
    for gr, ax in zip(grads, axes):
        shape = list(gr.shape)
        shape[ax] //= 4
        outs_shape.append(jax.ShapeDtypeStruct((3,) + tuple(shape), gr.dtype))

    def copies(ins, outs, scr):
        send, recv = scr
        x, y, c, chips = _place()
        cps = []
        for t in range(nw):
            width = ins[t].shape[axes[t]] // 4
            for j, (px, py) in enumerate(chips):
                cps.append(pltpu.make_async_remote_copy(
                    src_ref=_slab(ins[t], axes[t], 2 * px + py, width), dst_ref=outs[t].at[j],
                    send_sem=send.at[3 * t + j], recv_sem=recv.at[3 * t + j],
                    device_id=(px, py, c), device_id_type=MESH))
        return cps

    def start(ins, outs, scr):
        for cp in copies(ins, outs, scr):
            cp.start()

    def finish(ins, outs, scr):
        cps = copies(ins, outs, scr)
        for cp in cps:
            cp.wait_recv()
        for cp in cps:
            cp.wait_send()

    return _Rider(grads, outs_shape, [pltpu.SemaphoreType.DMA((3 * nw,)), pltpu.SemaphoreType.DMA((3 * nw,))],
                  start, finish)


def _swap_rider(parts):
    nw = len(parts)

    def copies(ins, outs, scr):
        send, recv = scr
        x, y, c, _ = _place()
        return [pltpu.make_async_remote_copy(
            src_ref=ins[t], dst_ref=outs[t], send_sem=send.at[t], recv_sem=recv.at[t],
            device_id=(x, y, 1 - c), device_id_type=MESH) for t in range(nw)]

    def start(ins, outs, scr):
        for cp in copies(ins, outs, scr):
            cp.start()

    def finish(ins, outs, scr):
        cps = copies(ins, outs, scr)
        for cp in cps:
            cp.wait_recv()
        for cp in cps:
            cp.wait_send()

    return _Rider(parts, [jax.ShapeDtypeStruct(p.shape, p.dtype) for p in parts],
                  [pltpu.SemaphoreType.DMA((nw,)), pltpu.SemaphoreType.DMA((nw,))], start, finish)


def _sum_all_devices(buf, name):
    rows, cols = buf.shape

    def body(in_ref, o_ref, gat_ref, send, recv):
        x, y, c, _ = _place()
        me = 4 * x + 2 * y + c
        gat_ref[me] = in_ref[...]
        started = []
        for mask in range(1, 8):
            fx, fy, fc = (mask >> 2) & 1, (mask >> 1) & 1, mask & 1
            peer = (x + fx * (1 - 2 * x), y + fy * (1 - 2 * y), c + fc * (1 - 2 * c))
            cp = pltpu.make_async_remote_copy(
                src_ref=in_ref, dst_ref=gat_ref.at[me], send_sem=send.at[mask - 1], recv_sem=recv.at[mask - 1],
                device_id=peer, device_id_type=MESH)
            cp.start()
            started.append(cp)
        for cp in started:
            cp.wait_recv()
        for cp in started:
            cp.wait_send()
        acc = gat_ref[0]
        for t in range(1, 8):
            acc = acc + gat_ref[t]
        o_ref[...] = acc

    vm = pl.BlockSpec(memory_space=pltpu.VMEM)
    return _pcall(
        body, name=name, in_specs=[vm], out_specs=vm, out_shape=jax.ShapeDtypeStruct((rows, cols), F32),
        scratch=[pltpu.VMEM((8, rows, cols), F32), pltpu.SemaphoreType.DMA((7,)), pltpu.SemaphoreType.DMA((7,))],
    )(buf)


_UPD_T = 256


def _sum_partials(own, got, name):
    rows, cols = own.shape
    tr = min(_UPD_T, rows)

    def body(own_ref, got_ref, o_ref):
        acc = own_ref[...].astype(F32)
        for j in range(3):
            acc = acc + got_ref[j].astype(F32)
        o_ref[...] = acc

    blk = pl.BlockSpec((tr, cols), lambda i: (i, 0))
    return _pcall(
        body, name=name, grid=(rows // tr,),
        in_specs=[blk, pl.BlockSpec((3, tr, cols), lambda i: (0, i, 0))], out_specs=blk,
        out_shape=jax.ShapeDtypeStruct((rows, cols), F32))(own, got)


def _adamw_math(w, gr, m, v):
    m = _B1 * m + (1.0 - _B1) * gr
    v = _B2 * v + (1.0 - _B2) * (gr * gr)
    m_hat = m / (1.0 - _B1 ** _STEP)
    v_hat = v / (1.0 - _B2 ** _STEP)
    delta = -_LR * (m_hat / (jnp.sqrt(v_hat) + _EPS) + _WD * w)
    return delta, m, v


def _adamw(w, m, v, parts, name):
    rows, cols = w.shape
    tr = min(_UPD_T, rows)
    npart = len(parts)

    def body(w_ref, m_ref, v_ref, *rest):
        p_refs, (g_ref, d_ref, nm_ref, nv_ref) = rest[:npart], rest[npart:]
        gr = p_refs[0][...]
        for p in p_refs[1:]:
            gr = gr + p[...]
        delta, nm, nv = _adamw_math(w_ref[...], gr, m_ref[...], v_ref[...])
        g_ref[...] = gr
        d_ref[...] = delta
        nm_ref[...] = nm
        nv_ref[...] = nv

    blk = pl.BlockSpec((tr, cols), lambda i: (i, 0))
    sh = jax.ShapeDtypeStruct((rows, cols), F32)
    return _pcall(body, name=name, grid=(rows // tr,), in_specs=[blk] * (3 + npart), out_specs=[blk] * 4,
                  out_shape=[sh] * 4)(w, m, v, *parts)


def _adamw_layers(w, m, v, parts, name):
    _, rows, cols = w.shape
    tr = min(_UPD_T, rows)
    npart = len(parts[0])

    def body(w_ref, m_ref, v_ref, *rest):
        p_refs, (g_ref, d_ref, nm_ref, nv_ref) = rest[:2 * npart], rest[2 * npart:]
        grs = []
        for layer in range(2):
            gr = p_refs[layer * npart][...]
            for p in p_refs[layer * npart + 1:(layer + 1) * npart]:
                gr = gr + p[...]
            grs.append(gr)
        gr = jnp.where(pl.program_id(0) == 0, grs[0], grs[1])
        delta, nm, nv = _adamw_math(w_ref[...], gr, m_ref[...], v_ref[...])
        g_ref[...] = gr
        d_ref[...] = delta
        nm_ref[...] = nm
        nv_ref[...] = nv

    blk = pl.BlockSpec((None, tr, cols), lambda l, i: (l, i, 0))

    def part_spec(layer):
        return pl.BlockSpec((tr, cols), lambda l, i: (jnp.where(l == layer, i, 0), 0))

    sh = jax.ShapeDtypeStruct(w.shape, F32)
    return _pcall(
        body, name=name, grid=(2, rows // tr),
        in_specs=[blk] * 3 + [part_spec(0)] * npart + [part_spec(1)] * npart, out_specs=[blk] * 4,
        out_shape=[sh] * 4)(w, m, v, *parts[0], *parts[1])


_PACK_W = 1024


def _pack(arrs, rows):
    flat = []
    for a in arrs:
        f = a.reshape(-1).astype(F32)
        pad = (-f.shape[0]) % _PACK_W
        flat.append(jnp.pad(f, (0, pad)))
    f = jnp.concatenate(flat)
    f = jnp.pad(f, (0, rows * _PACK_W - f.shape[0]))
    return f.reshape(rows, _PACK_W)


def _unpack(buf, shapes):
    flat = buf.reshape(-1)
    out, pos = [], 0
    for sh in shapes:
        size = math.prod(sh)
        out.append(flat[pos:pos + size].reshape(sh))
        pos += size + ((-size) % _PACK_W)
    return out


def _pack_rows(shapes):
    total = sum(-(-math.prod(sh) // _PACK_W) for sh in shapes)
    return -(-total // 8) * 8


def kernel(x, rel_bias, ab_norm, ab_w_in, ab_conv_w, ab_conv_b, ab_ln_g, ab_ln_b, ab_w_out, sc_norm, sc_w_in, sc_conv_w, sc_w_out, mlp_norm, mlp_w_up, mlp_w_down, final_norm, loss_target, m_rel_bias, m_ab_norm, m_ab_w_in, m_ab_conv_w, m_ab_conv_b, m_ab_ln_g, m_ab_ln_b, m_ab_w_out, m_sc_norm, m_sc_w_in, m_sc_conv_w, m_sc_w_out, m_mlp_norm, m_mlp_w_up, m_mlp_w_down, m_final_norm, v_rel_bias, v_ab_norm, v_ab_w_in, v_ab_conv_w, v_ab_conv_b, v_ab_ln_g, v_ab_ln_b, v_ab_w_out, v_sc_norm, v_sc_w_in, v_sc_conv_w, v_sc_w_out, v_mlp_norm, v_mlp_w_up, v_mlp_w_down, v_final_norm):
    s, d = x.shape[1], x.shape[2]
    dff = 4 * d
    c = _GROUP_COLS
    chip = 2 * lax.axis_index("x") + lax.axis_index("y")
    on_c0 = (lax.axis_index("c") == 0).astype(F32)
    h0 = x[0]
    tgt = loss_target[0]

    cw_sh, scn_sh, scw_sh = ab_conv_w[0], sc_norm, sc_conv_w[0]
    conv_w_full = lax.dynamic_update_slice(jnp.zeros((_CONV_K, c), F32), cw_sh * on_c0, (0, chip * cw_sh.shape[1]))
    scn_full = lax.dynamic_update_slice(jnp.zeros((1, d), F32), scn_sh * on_c0, (0, chip * scn_sh.shape[1]))
    scw_full = lax.dynamic_update_slice(jnp.zeros((3, d), F32), scw_sh * on_c0, (0, chip * scw_sh.shape[1]))
    small_shapes = [(_CONV_K, c), (1, d), (3, d)]
    small = _sum_all_devices(_pack([conv_w_full, scn_full, scw_full], _pack_rows(small_shapes)), "gather_small")
    conv_w, sc_g, sc_cw = _unpack(small, small_shapes)

    w_shards = [ab_w_in[0], ab_w_out[0], sc_w_in[0], sc_w_out[0], mlp_w_up[0], mlp_w_up[1],
                mlp_w_down[0], mlp_w_down[1]]
    w_axes = [1, 0, 1, 0, 1, 1, 0, 0]
    wb = [w.astype(BF16) for w in w_shards]
    full_w = [None] * 8

    def gather(idx):
        return _join_riders([_gather_halves_rider(wb[t], w_axes[t]) for t in idx])

    def put(idx, got_w):
        for t, w in zip(idx, got_w):
            full_w[t] = w

    buckets = _bucket_maps()
    bias = _bias_expand(rel_bias, buckets)
    n0, got_w = _rms_fwd(h0, ab_norm, "rms_fwd_ab", rider=_gather_halves_rider(wb[0], w_axes[0]))
    put([0], got_w)
    w_in = full_w[0]
    tm = min(1024, s)
    tm2 = min(2048, s)
    tmh = min(512, s)
    uc = _mm(n0, w_in, "nn", m=s, n=2 * c, k=d, tm=tm2, tn=2 * c, tk=d, out_dtype=BF16, name="proj_conv")
    uq, uk, uv = [], [], []
    for t, (nm, dst) in enumerate(zip("qkv", (uq, uk, uv))):
        res = _mm(n0, w_in, "nn", m=s, n=3 * c, k=d, tm=tm2, tn=c, tk=d, out_dtype=F32, name=f"proj_{nm}",
                  b_off=(0, 2 + 3 * t), split="o", rider=gather([1]) if t == 0 else None)
        if t == 0:
            res, got_w = res
            put([1], got_w)
        dst.append(res)
    uq, uk, uv = uq[0], uk[0], uv[0]
    (cat, ca), got_w = _conv_a_fwd(uc, conv_w, ab_conv_b, ab_ln_g, ab_ln_b, rider=gather([2]))
    put([2], got_w)
    outs, lses = [], []
    for g, (_, dil) in enumerate(_GROUPS):
        idx = ([4], [6], [3, 5])[g]
        (o, l), got_w = _attn_fwd(uq, uk, uv, bias, g, dil, 4 if dil <= 4 else 2, rider=gather(idx))
        put(idx, got_w)
        outs.append(o)
        lses.append(l)
    cat, lse = _attn_merge(outs, lses, cat)
    h1, n1 = _mm(cat, full_w[1], "nn", m=s, n=d, k=d, tm=tm, tn=d, tk=d, out_dtype=(F32, BF16), name="out_ab",
                 epi=_epi_add_rms, extras=(h0,), vecs=(mlp_norm[0:1],))

    def mlp_fwd(h, nrm, layer, next_gain=None, rider=None):
        zr = _mm(nrm, full_w[4 + layer], "nn", m=s, n=dff, k=d, tm=tmh, tn=dff, tk=d, out_dtype=BF16,
                 name=f"mlp_up{layer}", epi=_epi_relu, rider=rider)
        if rider is not None:
            zr, got_r = zr
            put([7], got_r)
        kw = dict(m=s, n=d, k=dff, tm=tmh, tn=d, tk=dff, name=f"mlp_down{layer}", a_pro=_square, extras=(h,))
        if next_gain is None:
            return zr, _mm(zr, full_w[6 + layer], "nn", out_dtype=F32, epi=_epi_add, **kw), None
        hn, nn = _mm(zr, full_w[6 + layer], "nn", out_dtype=(F32, BF16), epi=_epi_add_rms, vecs=(next_gain,), **kw)
        return zr, hn, nn

    zr0, h2, n2 = mlp_fwd(h1, n1, 0, next_gain=sc_g, rider=gather([7]))
    _, w_out, w_si, w_so, w_up0, w_up1, w_dn0, w_dn1 = full_w
    w_up, w_dn = [w_up0, w_up1], [w_dn0, w_dn1]
    u2 = _mm(n2, w_si, "nn", m=s, n=3 * d, k=d, tm=tmh, tn=3 * d, tk=d, out_dtype=BF16, name="proj_sc")
    scv = _short_conv_fwd(u2, sc_cw)
    h3, n3 = _mm(scv, w_so, "nn", m=s, n=d, k=d, tm=tm, tn=d, tk=d, out_dtype=(F32, BF16), name="out_sc",
                 epi=_epi_add_rms, extras=(h2,), vecs=(mlp_norm[1:2],))
    zr1, h4, _ = mlp_fwd(h3, n3, 1)

    dh4, dh4b, g_final, loss_part = _loss_head(h4, tgt, final_norm.reshape(1, d))
    tkw = min(2048, s)

    big_grads, got, sums = [None] * 8, [None] * 8, [None] * 8

    def scatter(t):
        return _scatter_rider([big_grads[t]], [w_axes[t]])

    def own_slab(t):
        width = big_grads[t].shape[w_axes[t]] // 4
        return lax.dynamic_slice_in_dim(big_grads[t], chip * width, width, axis=w_axes[t])

    def arrived(t, got_t):
        got[t] = got_t[0]
        sums[t] = _sum_partials(own_slab(t), got[t], f"sum_partials{t}")

    def mlp_bwd(dh, dhb, h, nrm, zr, layer):
        dz = _mm(dhb, w_dn[layer], "nt", m=s, n=dff, k=d, tm=tmh, tn=dff, tk=d, out_dtype=BF16,
                 name=f"mlp_down{layer}_dx", epi=_epi_relu_sq_bwd, extras=(zr,))
        big_grads[6 + layer] = _mm(zr, dhb, "tn", m=dff, n=d, k=s, tm=1024, tn=d, tk=tkw, out_dtype=BF16,
                                   name=f"mlp_down{layer}_dw", a_pro=_square)
        big_grads[4 + layer] = _mm(nrm, dz, "tn", m=d, n=dff, k=s, tm=d, tn=1024, tk=tkw, out_dtype=BF16,
                                   name=f"mlp_up{layer}_dw")
        res = _mm(dz, w_up[layer], "nt", m=s, n=d, k=dff, tm=tmh, tn=d, tk=dff, out_dtype=(F32, BF16),
                  name=f"mlp_up{layer}_dx", rider=scatter(6) if layer == 0 else None, epi=_epi_rms_bwd,
                  extras=(h, dh), vecs=(mlp_norm[layer:layer + 1],), row_sum=True)
        if layer == 0:
            res, got_t = res
            arrived(6, got_t)
        return res

    dh3, dh3b, g_mn1 = mlp_bwd(dh4, dh4b, h3, n3, zr1, 1)

    dsc = _mm(dh3b, w_so, "nt", m=s, n=d, k=d, tm=tm, tn=d, tk=d, out_dtype=F32, name="out_sc_dx")
    big_grads[3] = _mm(scv, dh3b, "tn", m=d, n=d, k=s, tm=d, tn=d, tk=tkw, out_dtype=BF16, name="out_sc_dw")
    du2, g_sccw8 = _short_conv_bwd(u2, dsc, sc_cw)
    big_grads[2], got_t = _mm(n2, du2, "tn", m=d, n=3 * d, k=s, tm=d, tn=1024, tk=tkw, out_dtype=BF16,
                              name="proj_sc_dw", rider=scatter(3))
    arrived(3, got_t)
    dh2, dh2b, g_scn = _mm(
        du2, w_si, "nt", m=s, n=d, k=3 * d, tm=tmh, tn=d, tk=3 * d, out_dtype=(F32, BF16), name="proj_sc_dx",
        epi=_epi_rms_bwd, extras=(h2, dh3), vecs=(sc_g,), row_sum=True)

    dh1, dh1b, g_mn0 = mlp_bwd(dh2, dh2b, h1, n1, zr0, 0)

    dcat = _mm(dh1b, w_out, "nt", m=s, n=d, k=d, tm=tm, tn=d, tk=d, out_dtype=F32, name="out_ab_dx")
    big_grads[1] = _mm(cat, dh1b, "tn", m=d, n=d, k=s, tm=d, tn=d, tk=tkw, out_dtype=BF16, name="out_ab_dw")
    (dca, conv_stats), got_t = _conv_a_bwd_ln(ca, dcat, ab_ln_g, ab_ln_b, rider=scatter(1))
    arrived(1, got_t)
    (duc, g_cw32), got_t = _conv_a_bwd_conv(uc, dca, conv_w, rider=scatter(4))
    arrived(4, got_t)
    delta, dys = _attn_delta(dcat, cat)

    grads_qkv, ds_list = [], []
    for g, (_, dil) in enumerate(_GROUPS):
        late = (2, 5, 7)[g]
        (dq, dk, dv, dsa), got_t = _attn_bwd(uq, uk, uv, dys, lse, delta, bias, grads_qkv, g, dil,
                                             4 if dil <= 4 else 1, rider=scatter(late))
        arrived(late, got_t)
        grads_qkv = [dq, dk, dv]
        ds_list.append(dsa)
    g_bias = _bias_reduce(jnp.concatenate(ds_list, axis=0), buckets)[:, :, 0].T

    secs = [(duc, 2 * c, 0)] + [(grads_qkv[t], 3 * c, 2 + 3 * t) for t in range(3)]
    g_in_parts = []
    for t, (du, width, off) in enumerate(secs):
        rider = _swap_rider(sums[1:]) if t == 1 else None
        part = _mm(n0, du, "tn", m=d, n=width, k=s, tm=d, tn=width, tk=min(1024, s) if t else tkw, out_dtype=BF16,
                   name=f"proj_ab_dw{t}", rider=rider, split="b" if t else "")
        if rider is not None:
            part, sib_late = part
        g_in_parts.append(part)
    big_grads[0] = jnp.concatenate(g_in_parts, axis=1)
    dn0 = _mm(duc, w_in, "nt", m=s, n=d, k=2 * c, tm=tm, tn=d, tk=2 * c, out_dtype=F32, name="proj_ab_dx_conv")
    dn0, got_t = _dx_qkv(*grads_qkv, w_in, dn0, tm=tm, tk=c, col0=2 * c, name="proj_ab_dx_qkv", rider=scatter(0))
    arrived(0, got_t)
    grad_x, _, g_abn = _rms_bwd(dn0, h0, ab_norm, dh1, "rms_bwd_ab")
    sib = list(_run_rider(_swap_rider([sums[0]]), "swap_sibling_w_in")) + sib_late

    upd = [_adamw(w_shards[t], mm[0], vv[0], [sums[t], sib[t]], f"adamw{t}")
           for t, (mm, vv) in enumerate(((m_ab_w_in, v_ab_w_in), (m_ab_w_out, v_ab_w_out),
                                         (m_sc_w_in, v_sc_w_in), (m_sc_w_out, v_sc_w_out)))]
    upd_up = _adamw_layers(mlp_w_up, m_mlp_w_up, v_mlp_w_up, [[sums[4], sib[4]], [sums[5], sib[5]]], "adamw_up")
    upd_dn = _adamw_layers(mlp_w_down, m_mlp_w_down, v_mlp_w_down, [[sums[6], sib[6]], [sums[7], sib[7]]],
                           "adamw_down")

    full_shapes = [(_NUM_BUCKETS, rel_bias.shape[1]), (1, d), (_CONV_K, c), (1, c), (1, c), (1, c), (1, d),
                   (3, d), (2, d), (d,), (1, 1)]
    small_grads = [g_bias, g_abn, g_cw32[:_CONV_K], conv_stats[0:1], conv_stats[1:2], conv_stats[2:3], g_scn,
                   g_sccw8[:3], jnp.concatenate([g_mn0, g_mn1], axis=0), g_final.reshape(d), loss_part[0:1, 0:1]]
    tot = _unpack(_sum_all_devices(_pack(small_grads, _pack_rows(full_shapes)), "sum_small"), full_shapes)
    loss = tot.pop()[0, 0]
    for idx, sh in ((2, cw_sh), (6, scn_sh), (7, scw_sh)):
        width = sh.shape[1]
        tot[idx] = lax.dynamic_slice_in_dim(tot[idx], chip * width, width, axis=1)
    sm_w = [rel_bias, ab_norm, cw_sh, ab_conv_b, ab_ln_g, ab_ln_b, scn_sh, scw_sh, mlp_norm, final_norm]
    sm_m = [m_rel_bias, m_ab_norm, m_ab_conv_w[0], m_ab_conv_b, m_ab_ln_g, m_ab_ln_b, m_sc_norm, m_sc_conv_w[0],
            m_mlp_norm, m_final_norm]
    sm_v = [v_rel_bias, v_ab_norm, v_ab_conv_w[0], v_ab_conv_b, v_ab_ln_g, v_ab_ln_b, v_sc_norm, v_sc_conv_w[0],
            v_mlp_norm, v_final_norm]
    sh_shapes = [tuple(t.shape) for t in tot]
    rows = _pack_rows(sh_shapes)
    sm_upd = _adamw(_pack(sm_w, rows), _pack(sm_m, rows), _pack(sm_v, rows), [_pack(tot, rows)], "adamw_small")
    sm_g, sm_d, sm_nm, sm_nv = [_unpack(buf, sh_shapes) for buf in sm_upd]

    def assemble(kind, sm):
        big = [u[kind] for u in upd]
        return [sm[0], sm[1], big[0][None], sm[2][None], sm[3], sm[4], sm[5], big[1][None], sm[6], big[2][None],
                sm[7][None], big[3][None], sm[8], upd_up[kind], upd_dn[kind], sm[9]]

    res = [loss, grad_x[None]]
    for kind, sm in enumerate((sm_g, sm_d, sm_nm, sm_nv)):
        res += assemble(kind, sm)
    return tuple(res)
```

```python
import functools
import math

import jax
import jax.numpy as jnp
from jax import lax
from jax.experimental import pallas as pl
from jax.experimental.pallas import tpu as pltpu

F32 = jnp.float32
BF16 = jnp.bfloat16
MESH = pl.DeviceIdType.MESH

_GROUPS = ((128, 1), (512, 4), (2048, 16))
_STEPS = 128
_HEAD_DIM = 64
_GROUP_COLS = 512
_NUM_BUCKETS = 32
_MAX_DISTANCE = 2048
_CONV_K = 31
_HALO = 32
_SC_HALO = 16
_RMS_EPS = 1e-6
_LN_EPS = 1e-5
_NEG = -1e30
_LANES = 128
_VMEM_LIMIT = 56 * 1024 * 1024

_LR, _B1, _B2, _EPS, _WD, _STEP = 0.001, 0.9, 0.999, 1e-08, 0.01, 10


class _Rider:
    def __init__(self, ins, out_shapes, scratch, start, finish):
        self.ins, self.out_shapes, self.scratch = list(ins), list(out_shapes), list(scratch)
        self.start, self.finish = start, finish


def _pcall(body, *, name, out_shape, in_specs, out_specs, grid=None, scratch=(), aliases=None, rider=None):
    kw = {} if grid is None else {"grid": grid}
    cparams = pltpu.CompilerParams(vmem_limit_bytes=_VMEM_LIMIT)
    if rider is None:
        return pl.pallas_call(
            body, name=name, out_shape=out_shape, in_specs=in_specs, out_specs=out_specs,
            scratch_shapes=list(scratch), input_output_aliases=aliases or {},
            compiler_params=cparams, **kw)
    single = not isinstance(out_specs, (list, tuple))
    ospecs = [out_specs] if single else list(out_specs)
    oshapes = [out_shape] if single else list(out_shape)
    nin, nout, nscr = len(in_specs), len(ospecs), len(scratch)
    rin, rout = len(rider.ins), len(rider.out_shapes)

    def wrapped(*refs):
        h_in, r_in = refs[:nin], refs[nin:nin + rin]
        p = nin + rin
        h_out, r_out = refs[p:p + nout], refs[p + nout:p + nout + rout]
        p += nout + rout
        h_scr, r_scr = refs[p:p + nscr], refs[p + nscr:]
        ids = [pl.program_id(a) for a in range(len(grid))]
        first = functools.reduce(jnp.logical_and, [i == 0 for i in ids])
        last = functools.reduce(jnp.logical_and, [i == g - 1 for i, g in zip(ids, grid)])

        @pl.when(first)
        def _():
            rider.start(r_in, r_out, r_scr)

        body(*h_in, *h_out, *h_scr)

        @pl.when(last)
        def _():
            rider.finish(r_in, r_out, r_scr)

    call = pl.pallas_call(
        wrapped, name=name, out_shape=oshapes + rider.out_shapes,
        in_specs=list(in_specs) + [_ANY] * rin, out_specs=ospecs + [_ANY] * rout,
        scratch_shapes=list(scratch) + rider.scratch, input_output_aliases=aliases or {},
        compiler_params=cparams, **kw)

    def run(*operands):
        res = call(*operands, *rider.ins)
        host = res[0] if single else list(res[:nout])
        return host, list(res[nout:])

    return run


def _sig(x):
    return 1.0 / (1.0 + jnp.exp(-x))


_ANY = pl.BlockSpec(memory_space=pl.ANY)


def _lanes_of(ref):
    parts = [ref[p] for p in range(ref.shape[0])]
    return parts[0] if len(parts) == 1 else jnp.concatenate(parts, axis=1)


def _mm(a, b, mode, *, m, n, k, tm, tn, tk, out_dtype, name, epi=None, extras=(), b_off=(0, 0), rider=None,
        split="", vecs=(), a_pro=None, row_sum=False):
    nk = k // tk
    assert m % tm == 0 and n % tn == 0 and k % tk == 0
    o0, o1 = b_off
    if mode == "nn":
        a_spec = pl.BlockSpec((tm, tk), lambda i, j, kk: (i, kk))
        b_spec = pl.BlockSpec((tk, tn), lambda i, j, kk: (kk + o0, j + o1))
        dn = (((1,), (0,)), ((), ()))
    elif mode == "nt":
        a_spec = pl.BlockSpec((tm, tk), lambda i, j, kk: (i, kk))
        if "a" in split:
            a_spec = pl.BlockSpec((tk // _LANES, tm, _LANES), lambda i, j, kk: (kk, i, 0))
        b_spec = pl.BlockSpec((tn, tk), lambda i, j, kk: (j + o0, kk + o1))
        dn = (((1,), (1,)), ((), ()))
    else:
        a_spec = pl.BlockSpec((tk, tm), lambda i, j, kk: (kk, i))
        b_spec = pl.BlockSpec((tk, tn), lambda i, j, kk: (kk + o0, j + o1))
        if "b" in split:
            b_spec = pl.BlockSpec((tn // _LANES, tk, _LANES), lambda i, j, kk: (j, kk, 0))
        dn = (((0,), (0,)), ((), ()))
    o_spec = pl.BlockSpec((tm, tn), lambda i, j, kk: (i, j))
    e_spec = o_spec
    if "o" in split:
        o_spec = pl.BlockSpec((tn // _LANES, tm, _LANES), lambda i, j, kk: (j, i, 0))
    v_spec = pl.BlockSpec((1, tn), lambda i, j, kk: (0, j))
    ne = len(extras) + len(vecs)
    multi = isinstance(out_dtype, tuple)
    dts = out_dtype if multi else (out_dtype,)
    no = len(dts)
    nr = 1 if row_sum else 0
    assert not row_sum or tn == n

    def body(a_ref, b_ref, *rest):
        ex, o_refs = rest[:ne], rest[ne:ne + no]
        av = _lanes_of(a_ref) if "a" in split else a_ref[...]
        bv = _lanes_of(b_ref) if "b" in split else b_ref[...]
        if av.dtype != BF16:
            av = av.astype(BF16)
        if bv.dtype != BF16:
            bv = bv.astype(BF16)
        if a_pro is not None:
            av = a_pro(av)
        p = lax.dot_general(av, bv, dn, preferred_element_type=F32)

        def fin(x):
            if epi is not None:
                x = epi(x, *[e[...] for e in ex])
            if row_sum:
                row, x = x[-1], (x[:-1] if multi else x[0])
                row_ref = rest[ne + no]

                @pl.when(pl.program_id(0) == 0)
                def _():
                    row_ref[...] = row

                @pl.when(pl.program_id(0) > 0)
                def _():
                    row_ref[...] += row

            for o_ref, val, dt in zip(o_refs, x if multi else (x,), dts):
                if "o" in split:
                    for p in range(tn // _LANES):
                        o_ref[p] = val[:, p * _LANES:(p + 1) * _LANES].astype(dt)
                else:
                    o_ref[...] = val.astype(dt)

        if nk == 1:
            fin(p)
        else:
            acc = rest[ne + no + nr]
            kk = pl.program_id(2)

            @pl.when(kk == 0)
            def _():
                acc[...] = p

            @pl.when(kk > 0)
            def _():
                acc[...] += p

            @pl.when(kk == nk - 1)
            def _():
                fin(acc[...])

    oshape = (n // _LANES, m, _LANES) if "o" in split else (m, n)
    shapes = [jax.ShapeDtypeStruct(oshape, dt) for dt in dts]
    ospecs = [o_spec] * no
    if row_sum:
        shapes.append(jax.ShapeDtypeStruct((1, n), F32))
        ospecs.append(v_spec)
    lone = not multi and not row_sum
    return _pcall(
        body, name=name, grid=(m // tm, n // tn, nk),
        in_specs=[a_spec, b_spec] + [e_spec] * len(extras) + [v_spec] * len(vecs),
        out_specs=ospecs[0] if lone else ospecs, out_shape=shapes[0] if lone else shapes,
        scratch=[pltpu.VMEM((tm, tn), F32)] if nk > 1 else [], rider=rider,
    )(a, b, *extras, *vecs)


def _epi_add(x, r):
    return x + r


def _epi_relu(x):
    return jnp.maximum(x, 0.0)


def _square(x):
    return x * x


def _epi_relu_sq_bwd(da, zr):
    return da * (2.0 * zr.astype(F32))


def _epi_add_rms(x, r, g):
    h = x + r
    return h, h * lax.rsqrt(jnp.mean(h * h, axis=-1, keepdims=True) + _RMS_EPS) * g


def _epi_rms_bwd(dn, h, dh_in, g):
    dx, dg = _rms_bwd_math(dn, h, g)
    dh = dh_in + dx
    return dh, dh, dg


_ROW_T = 512


def _rms_fwd(h, g, name, rider=None):
    s, d = h.shape

    def body(h_ref, g_ref, o_ref):
        x = h_ref[...]
        r = lax.rsqrt(jnp.mean(x * x, axis=-1, keepdims=True) + _RMS_EPS)
        o_ref[...] = (x * r * g_ref[...]).astype(BF16)

    row = pl.BlockSpec((_ROW_T, d), lambda i: (i, 0))
    vec = pl.BlockSpec((1, d), lambda i: (0, 0))
    return _pcall(body, name=name, grid=(s // _ROW_T,), in_specs=[row, vec], out_specs=row,
                  out_shape=jax.ShapeDtypeStruct((s, d), BF16), rider=rider)(h, g)


def _rms_bwd_math(dn, x, g):
    r = lax.rsqrt(jnp.mean(x * x, axis=-1, keepdims=True) + _RMS_EPS)
    xhat = x * r
    dg = jnp.sum(dn * xhat, axis=0, keepdims=True)
    t = dn * g
    dx = r * (t - xhat * jnp.mean(t * xhat, axis=-1, keepdims=True))
    return dx, dg


def _rms_bwd(dn, h, g, dh_in, name, rider=None):
    s, d = h.shape

    def body(dn_ref, h_ref, g_ref, dhi_ref, dh_ref, dhb_ref, dg_ref):
        dx, dg = _rms_bwd_math(dn_ref[...], h_ref[...], g_ref[...])
        dh = dhi_ref[...] + dx
        dh_ref[...] = dh
        dhb_ref[...] = dh.astype(BF16)

        @pl.when(pl.program_id(0) == 0)
        def _():
            dg_ref[...] = jnp.zeros_like(dg_ref)

        dg_ref[...] += dg

    row = pl.BlockSpec((_ROW_T, d), lambda i: (i, 0))
    vec = pl.BlockSpec((1, d), lambda i: (0, 0))
    return _pcall(
        body, name=name, grid=(s // _ROW_T,), in_specs=[row, row, vec, row], out_specs=[row, row, vec],
        out_shape=[jax.ShapeDtypeStruct((s, d), F32), jax.ShapeDtypeStruct((s, d), BF16),
                   jax.ShapeDtypeStruct((1, d), F32)], rider=rider)(dn, h, g, dh_in)


def _loss_head(h, tgt, g):
    s, d = h.shape

    def body(h_ref, t_ref, g_ref, dh_ref, dhb_ref, dg_ref, loss_ref):
        x, gv = h_ref[...], g_ref[...]
        r = lax.rsqrt(jnp.mean(x * x, axis=-1, keepdims=True) + _RMS_EPS)
        err = x * r * gv - t_ref[...]
        part = 0.5 * jnp.sum(jnp.mean(err * err, axis=-1, keepdims=True))
        dx, dg = _rms_bwd_math(err * (1.0 / d), x, gv)
        dh_ref[...] = dx
        dhb_ref[...] = dx.astype(BF16)

        @pl.when(pl.program_id(0) == 0)
        def _():
            dg_ref[...] = jnp.zeros_like(dg_ref)
            loss_ref[...] = jnp.zeros_like(loss_ref)

        dg_ref[...] += dg
        loss_ref[...] += jnp.full(loss_ref.shape, part, F32)

    row = pl.BlockSpec((_ROW_T, d), lambda i: (i, 0))
    vec = pl.BlockSpec((1, d), lambda i: (0, 0))
    one = pl.BlockSpec((1, _LANES), lambda i: (0, 0))
    return _pcall(
        body, name="loss_head", grid=(s // _ROW_T,), in_specs=[row, row, vec], out_specs=[row, row, vec, one],
        out_shape=[jax.ShapeDtypeStruct((s, d), F32), jax.ShapeDtypeStruct((s, d), BF16),
                   jax.ShapeDtypeStruct((1, d), F32), jax.ShapeDtypeStruct((1, _LANES), F32)])(h, tgt, g)


_CONV_T = 256
_CONV_RC = 32


def _conv_a_specs(s):
    c = _GROUP_COLS
    hb = _CONV_T // _HALO
    val = pl.BlockSpec((_CONV_T, c), lambda i: (i, 0))
    gate = pl.BlockSpec((_CONV_T, c), lambda i: (i, 1))
    hval = pl.BlockSpec((_HALO, c), lambda i: (jnp.maximum(i * hb - 1, 0), 0))
    hgate = pl.BlockSpec((_HALO, c), lambda i: (jnp.maximum(i * hb - 1, 0), 1))
    return val, gate, hval, hgate


def _fill_glu(val_ref, gate_ref, hval_ref, hgate_ref, hs_ref):
    i = pl.program_id(0)
    hs_ref[pl.ds(_HALO, _CONV_T), :] = val_ref[...].astype(F32) * _sig(gate_ref[...].astype(F32))
    halo = hval_ref[...].astype(F32) * _sig(hgate_ref[...].astype(F32))
    hs_ref[pl.ds(0, _HALO), :] = jnp.where(i > 0, halo, 0.0)


_SHIFT_ROWS = _CONV_T + _HALO - 8


def _fill_shifts(src_ref, sh_ref):
    for b in range(1, 8):
        sh_ref[b - 1] = src_ref[pl.ds(b, _SHIFT_ROWS), :]


def _tap_rows(src_ref, sh_ref, start, rows, lanes=slice(None)):
    b = start % 8
    if b == 0:
        return src_ref[pl.ds(start, rows), lanes]
    return sh_ref[b - 1, pl.ds(start - b, rows), lanes]


def _conv_rows(hs_ref, sh_ref, w_ref, r0, rows):
    off = _HALO - (_CONV_K - 1)
    acc = jnp.zeros((rows, _GROUP_COLS), F32)
    for kk in range(_CONV_K):
        acc = acc + w_ref[kk:kk + 1, :] * _tap_rows(hs_ref, sh_ref, r0 + off + kk, rows)
    return acc


def _ln_fwd(ca, g, b):
    mu = jnp.mean(ca, axis=-1, keepdims=True)
    xc = ca - mu
    rstd = lax.rsqrt(jnp.mean(xc * xc, axis=-1, keepdims=True) + _LN_EPS)
    xhat = xc * rstd
    return xhat, rstd, xhat * g + b


def _conv_a_fwd(uc, w, cb, lg, lb, rider=None):
    s = uc.shape[0]
    c = _GROUP_COLS

    def body(val_ref, gate_ref, hval_ref, hgate_ref, w_ref, cb_ref, lg_ref, lb_ref, o_ref, ca_ref, hs_ref, sh_ref):
        _fill_glu(val_ref, gate_ref, hval_ref, hgate_ref, hs_ref)
        _fill_shifts(hs_ref, sh_ref)
        for rc in range(_CONV_T // _CONV_RC):
            r0 = rc * _CONV_RC
            ca = _conv_rows(hs_ref, sh_ref, w_ref, r0, _CONV_RC) + cb_ref[...]
            ca_ref[pl.ds(r0, _CONV_RC), :] = ca
            _, _, ln = _ln_fwd(ca, lg_ref[...], lb_ref[...])
            o_ref[pl.ds(r0, _CONV_RC), :] = (ln * _sig(ln)).astype(BF16)

    val, gate, hval, hgate = _conv_a_specs(s)
    wspec = pl.BlockSpec((_CONV_K, c), lambda i: (0, 0))
    vec = pl.BlockSpec((1, c), lambda i: (0, 0))
    blk = pl.BlockSpec((_CONV_T, c), lambda i: (i, 0))
    return _pcall(
        body, name="conv_a_fwd", grid=(s // _CONV_T,),
        in_specs=[val, gate, hval, hgate, wspec, vec, vec, vec],
        out_specs=[blk, blk],
        out_shape=[jax.ShapeDtypeStruct((s, 2 * c), BF16), jax.ShapeDtypeStruct((s, c), F32)],
        scratch=[pltpu.VMEM((_CONV_T + _HALO, c), F32), pltpu.VMEM((7, _SHIFT_ROWS, c), F32)],
        rider=rider)(uc, uc, uc, uc, w, cb, lg, lb)


def _conv_a_bwd_ln(ca_all, dcat, lg, lb, rider=None):
    s = ca_all.shape[0]
    c = _GROUP_COLS

    def body(ca_ref, dy_ref, lg_ref, lb_ref, dca_ref, st_ref):
        @pl.when(pl.program_id(0) == 0)
        def _():
            st_ref[...] = jnp.zeros_like(st_ref)

        for rc in range(_CONV_T // _CONV_RC):
            r0 = rc * _CONV_RC
            ca = ca_ref[pl.ds(r0, _CONV_RC), :]
            xhat, rstd, ln = _ln_fwd(ca, lg_ref[...], lb_ref[...])
            sg = _sig(ln)
            dln = dy_ref[pl.ds(r0, _CONV_RC), :] * (sg * (1.0 + ln * (1.0 - sg)))
            dxh = dln * lg_ref[...]
            dca = rstd * (dxh - jnp.mean(dxh, axis=-1, keepdims=True)
                          - xhat * jnp.mean(dxh * xhat, axis=-1, keepdims=True))
            dca_ref[pl.ds(r0, _CONV_RC), :] = dca
            st_ref[0:1, :] += jnp.sum(dca, axis=0, keepdims=True)
            st_ref[1:2, :] += jnp.sum(dln * xhat, axis=0, keepdims=True)
            st_ref[2:3, :] += jnp.sum(dln, axis=0, keepdims=True)

    blk = pl.BlockSpec((_CONV_T, c), lambda i: (i, 0))
    vec = pl.BlockSpec((1, c), lambda i: (0, 0))
    st = pl.BlockSpec((8, c), lambda i: (0, 0))
    return _pcall(
        body, name="conv_a_bwd_ln", grid=(s // _CONV_T,),
        in_specs=[blk, blk, vec, vec], out_specs=[blk, st],
        out_shape=[jax.ShapeDtypeStruct((s, c), F32), jax.ShapeDtypeStruct((8, c), F32)],
        rider=rider)(ca_all, dcat, lg, lb)


def _conv_a_bwd_conv(uc, dca, w, rider=None):
    s = uc.shape[0]
    c = _GROUP_COLS
    nblk = s // _CONV_T
    hb = _CONV_T // _HALO
    off = _HALO - (_CONV_K - 1)

    def body(val_ref, gate_ref, hval_ref, hgate_ref, d_ref, dn_ref, w_ref, du_ref, dw_ref, hs_ref, ds_ref,
             hsh_ref, dsh_ref, dwa_ref):
        i = pl.program_id(0)
        _fill_glu(val_ref, gate_ref, hval_ref, hgate_ref, hs_ref)
        ds_ref[pl.ds(0, _CONV_T), :] = d_ref[...]
        ds_ref[pl.ds(_CONV_T, _HALO), :] = jnp.where(i < nblk - 1, dn_ref[...], 0.0)
        _fill_shifts(hs_ref, hsh_ref)
        _fill_shifts(ds_ref, dsh_ref)

        @pl.when(i == 0)
        def _():
            dwa_ref[...] = jnp.zeros_like(dwa_ref)

        rows = 32
        for r0 in range(0, _CONV_T, rows):
            dcur = ds_ref[pl.ds(r0, rows), :]
            dh = jnp.zeros((rows, c), F32)
            for kk in range(_CONV_K):
                dh = dh + w_ref[kk:kk + 1, :] * _tap_rows(ds_ref, dsh_ref, r0 + _CONV_K - 1 - kk, rows)
                prod = dcur * _tap_rows(hs_ref, hsh_ref, r0 + off + kk, rows)
                dwa_ref[pl.ds(8 * kk, 8), :] += sum(prod[t:t + 8] for t in range(0, rows, 8))
            v = val_ref[pl.ds(r0, rows), :].astype(F32)
            sg = _sig(gate_ref[pl.ds(r0, rows), :].astype(F32))
            du_ref[pl.ds(r0, rows), pl.ds(0, c)] = (dh * sg).astype(BF16)
            du_ref[pl.ds(r0, rows), pl.ds(c, c)] = (dh * v * sg * (1.0 - sg)).astype(BF16)

        @pl.when(i == nblk - 1)
        def _():
            dw_ref[...] = jnp.zeros_like(dw_ref)
            for kk in range(_CONV_K):
                dw_ref[kk:kk + 1, :] = jnp.sum(dwa_ref[pl.ds(8 * kk, 8), :], axis=0, keepdims=True)

    val, gate, hval, hgate = _conv_a_specs(s)
    blk = pl.BlockSpec((_CONV_T, c), lambda i: (i, 0))
    nxt = pl.BlockSpec((_HALO, c), lambda i: (jnp.minimum((i + 1) * hb, s // _HALO - 1), 0))
    wspec = pl.BlockSpec((_CONV_K, c), lambda i: (0, 0))
    return _pcall(
        body, name="conv_a_bwd_conv", grid=(nblk,),
        in_specs=[val, gate, hval, hgate, blk, nxt, wspec],
        out_specs=[pl.BlockSpec((_CONV_T, 2 * c), lambda i: (i, 0)), pl.BlockSpec((_HALO, c), lambda i: (0, 0))],
        out_shape=[jax.ShapeDtypeStruct((s, 2 * c), BF16), jax.ShapeDtypeStruct((_HALO, c), F32)],
        scratch=[pltpu.VMEM((_CONV_T + _HALO, c), F32), pltpu.VMEM((_CONV_T + _HALO, c), F32),
                 pltpu.VMEM((7, _SHIFT_ROWS, c), F32), pltpu.VMEM((7, _SHIFT_ROWS, c), F32),
                 pltpu.VMEM((8 * _HALO, c), F32)],
        rider=rider,
    )(uc, uc, uc, uc, dca, dca, w)


_SC_T = 256
_SC_RC = 32
_SC_LC = 512


def _sc_chunks(d):
    return [(pl.ds(r0, _SC_RC), pl.ds(l0, _SC_LC)) for r0 in range(0, _SC_T, _SC_RC) for l0 in range(0, d, _SC_LC)]


def _short_conv_fwd(u2, w):
    s, d3 = u2.shape
    d = d3 // 3
    hb = _SC_T // _SC_HALO

    def body(b_ref, c_ref, v_ref, hc_ref, hv_ref, w_ref, o_ref, cs_ref):
        i = pl.program_id(0)
        cs_ref[pl.ds(0, _SC_HALO), :] = jnp.where(i > 0, hc_ref[...].astype(F32) * hv_ref[...].astype(F32), 0.0)
        for rows, lanes in _sc_chunks(d):
            cs_ref[pl.ds(_SC_HALO + rows.start, _SC_RC), lanes] = (
                c_ref[rows, lanes].astype(F32) * v_ref[rows, lanes].astype(F32))
        for rows, lanes in _sc_chunks(d):
            taps = [cs_ref[pl.ds(_SC_HALO - 2 + kk + rows.start, _SC_RC), lanes] for kk in range(3)]
            conv = w_ref[0:1, lanes] * taps[0] + w_ref[1:2, lanes] * taps[1] + w_ref[2:3, lanes] * taps[2]
            o_ref[rows, lanes] = (b_ref[rows, lanes].astype(F32) * conv).astype(BF16)

    def col(j):
        return pl.BlockSpec((_SC_T, d), lambda i: (i, j))

    def halo(j):
        return pl.BlockSpec((_SC_HALO, d), lambda i: (jnp.maximum(i * hb - 1, 0), j))

    return _pcall(
        body, name="short_conv_fwd", grid=(s // _SC_T,),
        in_specs=[col(0), col(1), col(2), halo(1), halo(2), pl.BlockSpec((3, d), lambda i: (0, 0))],
        out_specs=pl.BlockSpec((_SC_T, d), lambda i: (i, 0)),
        out_shape=jax.ShapeDtypeStruct((s, d), BF16),
        scratch=[pltpu.VMEM((_SC_T + _SC_HALO, d), F32)])(u2, u2, u2, u2, u2, w)


def _short_conv_bwd(u2, dsc, w, rider=None):
    s, d3 = u2.shape
    d = d3 // 3
    hb = _SC_T // _SC_HALO
    nblk = s // _SC_T

    def body(b_ref, c_ref, v_ref, hc_ref, hv_ref, nb_ref, d_ref, nd_ref, w_ref, du_ref, dw_ref, cs_ref, ds_ref):
        i = pl.program_id(0)
        cs_ref[pl.ds(0, _SC_HALO), :] = jnp.where(i > 0, hc_ref[...].astype(F32) * hv_ref[...].astype(F32), 0.0)
        ds_ref[pl.ds(_SC_T, _SC_HALO), :] = jnp.where(i < nblk - 1, nd_ref[...] * nb_ref[...].astype(F32), 0.0)
        for rows, lanes in _sc_chunks(d):
            cs_ref[pl.ds(_SC_HALO + rows.start, _SC_RC), lanes] = (
                c_ref[rows, lanes].astype(F32) * v_ref[rows, lanes].astype(F32))
            ds_ref[rows, lanes] = d_ref[rows, lanes] * b_ref[rows, lanes].astype(F32)

        @pl.when(i == 0)
        def _():
            dw_ref[...] = jnp.zeros_like(dw_ref)

        for l0 in range(0, d, _SC_LC):
            lanes = pl.ds(l0, _SC_LC)
            dw_acc = [jnp.zeros((8, _SC_LC), F32)] * 3
            for r0 in range(0, _SC_T, _SC_RC):
                rows = pl.ds(r0, _SC_RC)
                taps = [cs_ref[pl.ds(_SC_HALO - 2 + kk + r0, _SC_RC), lanes] for kk in range(3)]
                conv = w_ref[0:1, lanes] * taps[0] + w_ref[1:2, lanes] * taps[1] + w_ref[2:3, lanes] * taps[2]
                dconv = ds_ref[rows, lanes]
                dcv = (w_ref[2:3, lanes] * dconv + w_ref[1:2, lanes] * ds_ref[pl.ds(r0 + 1, _SC_RC), lanes]
                       + w_ref[0:1, lanes] * ds_ref[pl.ds(r0 + 2, _SC_RC), lanes])
                du_ref[rows, lanes] = (d_ref[rows, lanes] * conv).astype(BF16)
                du_ref[rows, pl.ds(d + l0, _SC_LC)] = (dcv * v_ref[rows, lanes].astype(F32)).astype(BF16)
                du_ref[rows, pl.ds(2 * d + l0, _SC_LC)] = (dcv * c_ref[rows, lanes].astype(F32)).astype(BF16)
                for kk in range(3):
                    prod = dconv * taps[kk]
                    dw_acc[kk] = dw_acc[kk] + sum(prod[t:t + 8] for t in range(0, _SC_RC, 8))
            for kk in range(3):
                dw_ref[kk:kk + 1, lanes] += jnp.sum(dw_acc[kk], axis=0, keepdims=True)

    def col(j):
        return pl.BlockSpec((_SC_T, d), lambda i: (i, j))

    def halo(j):
        return pl.BlockSpec((_SC_HALO, d), lambda i: (jnp.maximum(i * hb - 1, 0), j))

    def nxt(j):
        return pl.BlockSpec((_SC_HALO, d), lambda i: (jnp.minimum((i + 1) * hb, s // _SC_HALO - 1), j))

    return _pcall(
        body, name="short_conv_bwd", grid=(nblk,),
        in_specs=[col(0), col(1), col(2), halo(1), halo(2), nxt(0), col(0), nxt(0),
                  pl.BlockSpec((3, d), lambda i: (0, 0))],
        out_specs=[pl.BlockSpec((_SC_T, d3), lambda i: (i, 0)), pl.BlockSpec((8, d), lambda i: (0, 0))],
        out_shape=[jax.ShapeDtypeStruct((s, d3), BF16), jax.ShapeDtypeStruct((8, d), F32)],
        scratch=[pltpu.VMEM((_SC_T + _SC_HALO, d), F32), pltpu.VMEM((_SC_T + _SC_HALO, d), F32)],
        rider=rider,
    )(u2, u2, u2, u2, u2, u2, dsc, dsc, w)


def _bucket_maps():
    a_idx = jnp.arange(_STEPS)[:, None]
    c_idx = jnp.arange(2 * _STEPS)[None, :]
    mdist = jnp.clip(a_idx + _STEPS - c_idx, 0, _STEPS)
    max_exact = _NUM_BUCKETS // 2
    maps = []
    for _, dil in _GROUPS:
        nn = mdist * dil
        nf = jnp.maximum(nn, 1).astype(F32)
        large = max_exact + (jnp.log(nf / max_exact) / math.log(_MAX_DISTANCE / max_exact)
                             * (_NUM_BUCKETS - max_exact)).astype(jnp.int32)
        maps.append(jnp.where(nn < max_exact, nn, jnp.minimum(large, _NUM_BUCKETS - 1)).astype(jnp.int32))
    return jnp.stack(maps, axis=0)


def _bias_expand(rel_bias, buckets):
    nh = rel_bias.shape[1]

    def body(rb_ref, bk_ref, o_ref):
        h = pl.program_id(0)
        bk = bk_ref[0]
        acc = jnp.zeros(bk.shape, F32)
        for b in range(_NUM_BUCKETS):
            acc = jnp.where(bk == b, rb_ref[b, h], acc)
        a = lax.broadcasted_iota(jnp.int32, bk.shape, 0)
        c = lax.broadcasted_iota(jnp.int32, bk.shape, 1)
        mdist = a + _STEPS - c
        o_ref[0] = jnp.where((mdist >= 0) & (mdist <= _STEPS), acc, _NEG)

    return _pcall(
        body, name="bias_expand", grid=(nh,),
        in_specs=[pl.BlockSpec(memory_space=pltpu.SMEM),
                  pl.BlockSpec((1, _STEPS, 2 * _STEPS), lambda h: (h // 8, 0, 0))],
        out_specs=pl.BlockSpec((1, _STEPS, 2 * _STEPS), lambda h: (h, 0, 0)),
        out_shape=jax.ShapeDtypeStruct((nh, _STEPS, 2 * _STEPS), F32))(rel_bias, buckets)


def _bias_reduce(ds_all, buckets):
    nh = ds_all.shape[0]

    def body(ds_ref, bk_ref, o_ref):
        t, bk = ds_ref[0], bk_ref[0]
        rows = lax.broadcasted_iota(jnp.int32, (_NUM_BUCKETS, _LANES), 0)
        out = jnp.zeros((_NUM_BUCKETS, _LANES), F32)
        for b in range(_NUM_BUCKETS):
            out = jnp.where(rows == b, jnp.sum(jnp.where(bk == b, t, 0.0)), out)
        o_ref[0] = out

    blk = pl.BlockSpec((1, _STEPS, 2 * _STEPS), lambda h: (h, 0, 0))
    return _pcall(
        body, name="bias_reduce", grid=(nh,),
        in_specs=[blk, pl.BlockSpec((1, _STEPS, 2 * _STEPS), lambda h: (h // 8, 0, 0))],
        out_specs=pl.BlockSpec((1, _NUM_BUCKETS, _LANES), lambda h: (h, 0, 0)),
        out_shape=jax.ShapeDtypeStruct((nh, _NUM_BUCKETS, _LANES), F32))(ds_all, buckets)


def _sub_residues(dil):
    return 4 if dil % 16 == 0 else 1


def _strided_rows(ref, tmp_ref, p, r, dil):
    sub = _sub_residues(dil)
    if dil == 1:
        return [ref[p]]
    if sub == 1:
        return [ref[p, pl.ds(r, _STEPS, stride=dil), :]]
    tmp_ref[...] = ref[p, pl.ds(r, _STEPS * sub, stride=dil // sub), :]
    return [tmp_ref[pl.ds(q, _STEPS, stride=sub), :] for q in range(sub)]


def _store_strided(ref, tmp_ref, p, r, dil, vals):
    sub = _sub_residues(dil)
    if dil == 1:
        ref[p] = vals[0]
    elif sub == 1:
        ref[p, pl.ds(r, _STEPS, stride=dil), :] = vals[0]
    else:
        for q, val in enumerate(vals):
            tmp_ref[pl.ds(q, _STEPS, stride=sub), :] = val
        ref[p, pl.ds(r, _STEPS * sub, stride=dil // sub), :] = tmp_ref[...]


def _tmp_rows(dil, count):
    sub = _sub_residues(dil)
    return [pltpu.VMEM((_STEPS * sub, _LANES), F32)] * count if sub > 1 else []


def _head_masks():
    lane = lax.broadcasted_iota(jnp.int32, (1, _LANES), 1)
    return [lane < _HEAD_DIM, lane >= _HEAD_DIM]


def _stack_heads(x2, masks):
    return jnp.concatenate([jnp.where(masks[0], x2, 0), jnp.where(masks[1], x2, 0)], axis=0)


def _unstack_heads(y, masks):
    return jnp.where(masks[0], y[:_STEPS], y[_STEPS:])


def _scores(qs2, k2, b_ref, j, first):
    sc = lax.dot_general(qs2, k2, (((1,), (1,)), ((), ())), preferred_element_type=F32)
    sc = sc * (_HEAD_DIM ** -0.5) + jnp.concatenate([b_ref[2 * j], b_ref[2 * j + 1]], axis=0)
    col = lax.broadcasted_iota(jnp.int32, sc.shape, 1)
    return jnp.where(jnp.logical_and(first, col < _STEPS), _NEG, sc)


_PAIRS = _GROUP_COLS // _LANES


def _attn_fwd(uq, uk, uv, bias, g, dil, pp, rider=None):
    s = uq.shape[1]
    rb = _STEPS * dil
    nb = s // rb
    npb = _PAIRS // pp

    sub = _sub_residues(dil)

    def body(q_ref, kc_ref, kp_ref, vc_ref, vp_ref, b_ref, o_ref, l_ref, *tmp):
        tmp = tmp + (None,) * 7
        n, r = pl.program_id(1), pl.program_id(2)
        first = n == 0
        masks = _head_masks()
        for j in range(pp):
            qs = _strided_rows(q_ref, tmp[0], j, r, dil)
            kps, kcs = _strided_rows(kp_ref, tmp[1], j, r, dil), _strided_rows(kc_ref, tmp[2], j, r, dil)
            vps, vcs = _strided_rows(vp_ref, tmp[3], j, r, dil), _strided_rows(vc_ref, tmp[4], j, r, dil)
            o_res, l_res = [], []
            for q in range(sub):
                q2 = qs[q].astype(BF16)
                k2 = jnp.concatenate([kps[q], kcs[q]], axis=0).astype(BF16)
                v2 = jnp.concatenate([vps[q], vcs[q]], axis=0).astype(BF16)
                sc = _scores(_stack_heads(q2, masks), k2, b_ref, j, first)
                mx = jnp.max(sc, axis=-1, keepdims=True)
                p = jnp.exp(sc - mx)
                den = jnp.sum(p, axis=-1, keepdims=True)
                o2 = jnp.dot(p.astype(BF16), v2, preferred_element_type=F32) / den
                o_res.append(_unstack_heads(o2, masks))
                l_res.append(_unstack_heads(jnp.broadcast_to(mx + jnp.log(den), o2.shape), masks))
            _store_strided(o_ref, tmp[5], j, r, dil, o_res)
            _store_strided(l_ref, tmp[6], j, r, dil, l_res)

    cur = pl.BlockSpec((pp, rb, _LANES), lambda hb, n, r: (g * npb + hb, n, 0))
    prev = pl.BlockSpec((pp, rb, _LANES), lambda hb, n, r: (g * npb + hb, jnp.maximum(n - 1, 0), 0))
    bspec = pl.BlockSpec((2 * pp, _STEPS, 2 * _STEPS), lambda hb, n, r: (g * npb + hb, 0, 0))
    ospec = pl.BlockSpec((pp, rb, _LANES), lambda hb, n, r: (hb, n, 0))
    sh = jax.ShapeDtypeStruct((_PAIRS, s, _LANES), F32)
    return _pcall(
        body, name=f"attn_fwd_g{g}", grid=(npb, nb, dil // sub),
        in_specs=[cur, cur, prev, cur, prev, bspec], out_specs=[ospec, ospec], out_shape=[sh, sh],
        scratch=_tmp_rows(dil, 7), rider=rider,
    )(uq, uk, uk, uv, uv, bias)


def _attn_merge(outs, lses, cat):
    s = outs[0].shape[1]
    c = _GROUP_COLS

    def body(o0, o1, o2, l0, l1, l2, cat_in, cat_ref, lse_ref):
        del cat_in
        a0, a1, a2 = l0[...], l1[...], l2[...]
        mx = jnp.maximum(jnp.maximum(a0, a1), a2)
        w0, w1, w2 = jnp.exp(a0 - mx), jnp.exp(a1 - mx), jnp.exp(a2 - mx)
        den = w0 + w1 + w2
        y = ((w0 * o0[...] + w1 * o1[...] + w2 * o2[...]) / den).astype(BF16)
        for p in range(_PAIRS):
            cat_ref[:, p * _LANES:(p + 1) * _LANES] = y[p]
        lse_ref[...] = mx + jnp.log(den)

    blk = pl.BlockSpec((_PAIRS, _ROW_T, _LANES), lambda i: (0, i, 0))
    return _pcall(
        body, name="attn_merge", grid=(s // _ROW_T,),
        in_specs=[blk] * 6 + [_ANY],
        out_specs=[pl.BlockSpec((_ROW_T, c), lambda i: (i, 1)), blk],
        out_shape=[jax.ShapeDtypeStruct(cat.shape, BF16), jax.ShapeDtypeStruct((_PAIRS, s, _LANES), F32)],
        aliases={6: 0})(*outs, *lses, cat)


def _attn_delta(dcat, cat):
    s = dcat.shape[0]
    c = _GROUP_COLS
    seg = (jnp.arange(c)[:, None] // _HEAD_DIM == jnp.arange(c)[None, :] // _HEAD_DIM).astype(BF16)

    def body(dy_ref, y_ref, seg_ref, dl_ref, dys_ref):
        dy = dy_ref[...]
        prod = dy * y_ref[...].astype(F32)
        hi = prod.astype(BF16)
        lo = (prod - hi.astype(F32)).astype(BF16)
        dl = (jnp.dot(hi, seg_ref[...], preferred_element_type=F32)
              + jnp.dot(lo, seg_ref[...], preferred_element_type=F32))
        for p in range(_PAIRS):
            dl_ref[p] = dl[:, p * _LANES:(p + 1) * _LANES]
            dys_ref[p] = dy[:, p * _LANES:(p + 1) * _LANES]

    right = pl.BlockSpec((_ROW_T, c), lambda i: (i, 1))
    blk = pl.BlockSpec((_PAIRS, _ROW_T, _LANES), lambda i: (0, i, 0))
    sh = jax.ShapeDtypeStruct((_PAIRS, s, _LANES), F32)
    return _pcall(
        body, name="attn_delta", grid=(s // _ROW_T,),
        in_specs=[right, right, pl.BlockSpec((c, c), lambda i: (0, 0))],
        out_specs=[blk, blk], out_shape=[sh, sh])(dcat, cat, seg)


def _attn_bwd(uq, uk, uv, dys, lse, delta, bias, prev_grads, g, dil, pp, rider=None):
    s = uq.shape[1]
    rb = _STEPS * dil
    nb = s // rb
    npb = _PAIRS // pp
    scale = _HEAD_DIM ** -0.5

    sub = _sub_residues(dil)

    def body(q_ref, kc_ref, kp_ref, vc_ref, vp_ref, dy_ref, l_ref, dl_ref, b_ref, *rest):
        rest = rest[len(prev_grads):]
        dqkv_ref, dsa_ref, dqc_ref, dkc_ref, dvc_ref = rest[:5]
        dq_ref, dk_ref, dv_ref = dqkv_ref.at[0], dqkv_ref.at[1], dqkv_ref.at[2]
        tmp = rest[5:] + (None,) * 11
        n, r = pl.program_id(1), pl.program_id(2)

        def carry_slot(j, q):
            return ((r + (dil // sub) * q) * pp + j) if sub > 1 else r * pp + j

        @pl.when(jnp.logical_and(n == 0, r == 0))
        def _():
            dsa_ref[...] = jnp.zeros_like(dsa_ref)

        @pl.when(n == 0)
        def _():
            for j in range(pp):
                for q in range(sub):
                    for carry in (dqc_ref, dkc_ref, dvc_ref):
                        carry[carry_slot(j, q)] = jnp.zeros((_STEPS, _LANES), F32)

        @pl.when(n < nb)
        def _():
            first = n == 0
            masks = _head_masks()
            for j in range(pp):
                qs = _strided_rows(q_ref, tmp[0], j, r, dil)
                kps, kcs = _strided_rows(kp_ref, tmp[1], j, r, dil), _strided_rows(kc_ref, tmp[2], j, r, dil)
                vps, vcs = _strided_rows(vp_ref, tmp[3], j, r, dil), _strided_rows(vc_ref, tmp[4], j, r, dil)
                dys_ = _strided_rows(dy_ref, tmp[5], j, r, dil)
                lses = _strided_rows(l_ref, tmp[6], j, r, dil)
                dls = _strided_rows(dl_ref, tmp[7], j, r, dil)
                ds_sum = [jnp.zeros((_STEPS, 2 * _STEPS), F32)] * 2
                dq_res, dk_res, dv_res = [], [], []
                for q in range(sub):
                    q2 = qs[q].astype(BF16)
                    k2 = jnp.concatenate([kps[q], kcs[q]], axis=0).astype(BF16)
                    v2 = jnp.concatenate([vps[q], vcs[q]], axis=0).astype(BF16)
                    dy2 = dys_[q].astype(BF16)
                    qs2, dys2 = _stack_heads(q2, masks), _stack_heads(dy2, masks)
                    per_row = lambda st: jnp.concatenate([st[:, 0:1], st[:, _HEAD_DIM:_HEAD_DIM + 1]], axis=0)
                    sc = _scores(qs2, k2, b_ref, j, first)
                    p = jnp.exp(sc - per_row(lses[q]))
                    dp = lax.dot_general(dys2, v2, (((1,), (1,)), ((), ())), preferred_element_type=F32)
                    ds = p * (dp - per_row(dls[q]))
                    ds_sum[0] = ds_sum[0] + ds[:_STEPS]
                    ds_sum[1] = ds_sum[1] + ds[_STEPS:]
                    dsb = ds.astype(BF16)
                    dq_p = _unstack_heads(jnp.dot(dsb, k2, preferred_element_type=F32), masks)
                    tdn = (((0,), (0,)), ((), ()))
                    dk_p = lax.dot_general(dsb, qs2, tdn, preferred_element_type=F32) * scale
                    dv_p = lax.dot_general(p.astype(BF16), dys2, tdn, preferred_element_type=F32)
                    slot = carry_slot(j, q)
                    dq_res.append(dqc_ref[slot])
                    dk_res.append(dkc_ref[slot] + dk_p[:_STEPS])
                    dv_res.append(dvc_ref[slot] + dv_p[:_STEPS])
                    dqc_ref[slot] = dq_p * scale
                    dkc_ref[slot] = dk_p[_STEPS:]
                    dvc_ref[slot] = dv_p[_STEPS:]
                for hh in range(2):
                    dsa_ref[2 * j + hh] += ds_sum[hh]
                _store_strided(dq_ref, tmp[8], j, r, dil, dq_res)
                _store_strided(dk_ref, tmp[9], j, r, dil, dk_res)
                _store_strided(dv_ref, tmp[10], j, r, dil, dv_res)

        @pl.when(n == nb)
        def _():
            for j in range(pp):
                for ref, carry, t in ((dq_ref, dqc_ref, 8), (dk_ref, dkc_ref, 9), (dv_ref, dvc_ref, 10)):
                    _store_strided(ref, tmp[t], j, r, dil, [carry[carry_slot(j, q)] for q in range(sub)])

    def clamp(n):
        return jnp.minimum(n, nb - 1)

    cur = pl.BlockSpec((pp, rb, _LANES), lambda hb, n, r: (g * npb + hb, clamp(n), 0))
    prev = pl.BlockSpec((pp, rb, _LANES), lambda hb, n, r: (g * npb + hb, jnp.maximum(clamp(n) - 1, 0), 0))
    stat = pl.BlockSpec((pp, rb, _LANES), lambda hb, n, r: (hb, clamp(n), 0))
    bspec = pl.BlockSpec((2 * pp, _STEPS, 2 * _STEPS), lambda hb, n, r: (g * npb + hb, 0, 0))
    late = pl.BlockSpec((3, pp, rb, _LANES), lambda hb, n, r: (0, g * npb + hb, jnp.maximum(n - 1, 0), 0))
    dsspec = pl.BlockSpec((2 * pp, _STEPS, 2 * _STEPS), lambda hb, n, r: (hb, 0, 0))
    np_ = len(prev_grads)
    carry = pltpu.VMEM((dil * pp, _STEPS, _LANES), F32)
    return _pcall(
        body, name=f"attn_bwd_g{g}", grid=(npb, nb + 1, dil // sub),
        in_specs=[cur, cur, prev, cur, prev, stat, stat, stat, bspec] + [_ANY] * np_,
        out_specs=[late, dsspec],
        out_shape=[jax.ShapeDtypeStruct((3, 3 * _PAIRS, s, _LANES), F32),
                   jax.ShapeDtypeStruct((8, _STEPS, 2 * _STEPS), F32)],
        scratch=[carry, carry, carry] + _tmp_rows(dil, 11),
        aliases={9 + t: t for t in range(np_)}, rider=rider,
    )(uq, uk, uk, uv, uv, dys, lse, delta, bias, *prev_grads)


def _place():
    x, y, c = lax.axis_index("x"), lax.axis_index("y"), lax.axis_index("c")
    chips = [(1 - x, y), (x, 1 - y), (1 - x, 1 - y)]
    return x, y, c, chips


def _slab(ref, axis, chip, width):
    start = pl.multiple_of(chip * width, width)
    if axis == 0:
        return ref.at[pl.ds(start, width), :]
    return ref.at[:, pl.ds(start, width)]


def _run_rider(rider, name):
    nin, nout = len(rider.ins), len(rider.out_shapes)

    def body(*refs):
        ins, outs, scr = refs[:nin], refs[nin:nin + nout], refs[nin + nout:]
        rider.start(ins, outs, scr)
        rider.finish(ins, outs, scr)

    return _pcall(body, name=name, in_specs=[_ANY] * nin, out_specs=[_ANY] * nout, out_shape=rider.out_shapes,
                  scratch=rider.scratch)(*rider.ins)


def _gather_halves_rider(shard, axis):
    shape = list(shard.shape)
    shape[axis] *= 4
    full = jax.ShapeDtypeStruct(tuple(shape), shard.dtype)
    half = shard.shape[0] // 2
    width = shard.shape[axis]

    def region(out, chip, core):
        if axis == 0:
            return out.at[pl.ds(pl.multiple_of(chip * width + core * half, half), half), :]
        return out.at[pl.ds(pl.multiple_of(core * half, half), half), pl.ds(pl.multiple_of(chip * width, width), width)]

    def copies(ins, outs, scr):
        send, recv, loc = scr
        (src,), (out,) = ins, outs
        x, y, c, chips = _place()
        mine = 2 * x + y
        own = pltpu.make_async_copy(src, _slab(out, axis, mine, width), loc.at[0])
        my_half = src.at[pl.ds(pl.multiple_of(c * half, half), half), :]
        over_ici, ici_in, to_sib, sib_in = [], [], [], []
        for j, (px, py) in enumerate(chips):
            theirs = 2 * px + py
            ici = dict(send_sem=send.at[j], recv_sem=recv.at[j], device_id=(px, py, c), device_id_type=MESH)
            d2d = dict(send_sem=send.at[3 + j], recv_sem=recv.at[3 + j], device_id=(x, y, 1 - c),
                       device_id_type=MESH)
            over_ici.append(pltpu.make_async_remote_copy(src_ref=my_half, dst_ref=region(out, mine, c), **ici))
            ici_in.append(pltpu.make_async_remote_copy(src_ref=my_half, dst_ref=region(out, theirs, c), **ici))
            to_sib.append(pltpu.make_async_remote_copy(
                src_ref=region(out, theirs, c), dst_ref=region(out, theirs, c), **d2d))
            sib_in.append(pltpu.make_async_remote_copy(
                src_ref=region(out, theirs, c), dst_ref=region(out, theirs, 1 - c), **d2d))
        return own, over_ici, ici_in, to_sib, sib_in

    def start(ins, outs, scr):
        own, over_ici, _, _, _ = copies(ins, outs, scr)
        own.start()
        for cp in over_ici:
            cp.start()

    def finish(ins, outs, scr):
        own, over_ici, ici_in, to_sib, sib_in = copies(ins, outs, scr)
        for j in range(3):
            ici_in[j].wait_recv()
            to_sib[j].start()
        for cp in sib_in:
            cp.wait_recv()
        own.wait()
        for cp in over_ici + to_sib:
            cp.wait_send()

    return _Rider([shard], [full], [pltpu.SemaphoreType.DMA((6,)), pltpu.SemaphoreType.DMA((6,)),
                                    pltpu.SemaphoreType.DMA((1,))], start, finish)


def _join_riders(riders):
    if len(riders) == 1:
        return riders[0]

    def parts(ins, outs, scr):
        pi = po = ps = 0
        for rd in riders:
            ni, no, ns = len(rd.ins), len(rd.out_shapes), len(rd.scratch)
            yield rd, ins[pi:pi + ni], outs[po:po + no], scr[ps:ps + ns]
            pi, po, ps = pi + ni, po + no, ps + ns

    def start(ins, outs, scr):
        for rd, i, o, sc in parts(ins, outs, scr):
            rd.start(i, o, sc)

    def finish(ins, outs, scr):
        for rd, i, o, sc in parts(ins, outs, scr):
            rd.finish(i, o, sc)

    return _Rider(sum((rd.ins for rd in riders), []), sum((rd.out_shapes for rd in riders), []),
                  sum((rd.scratch for rd in riders), []), start, finish)


def _scatter_rider(grads, axes):
    nw = len(grads)
    outs_shape = []
    for gr, ax in zip(grads, axes):
        shape = list(gr.shape)
        shape[ax] //= 4
        outs_shape.append(jax.ShapeDtypeStruct((3,) + tuple(shape), gr.dtype))

    def copies(ins, outs, scr):
        send, recv = scr
        x, y, c, chips = _place()
        cps = []
        for t in range(nw):
            width = ins[t].shape[axes[t]] // 4
            for j, (px, py) in enumerate(chips):
                cps.append(pltpu.make_async_remote_copy(
                    src_ref=_slab(ins[t], axes[t], 2 * px + py, width), dst_ref=outs[t].at[j],
                    send_sem=send.at[3 * t + j], recv_sem=recv.at[3 * t + j],
                    device_id=(px, py, c), device_id_type=MESH))
        return cps

    def start(ins, outs, scr):
        for cp in copies(ins, outs, scr):
            cp.start()

    def finish(ins, outs, scr):
        cps = copies(ins, outs, scr)
        for cp in cps:
            cp.wait_recv()
        for cp in cps:
            cp.wait_send()

    return _Rider(grads, outs_shape, [pltpu.SemaphoreType.DMA((3 * nw,)), pltpu.SemaphoreType.DMA((3 * nw,))],
                  start, finish)


def _swap_rider(parts):
    nw = len(parts)

    def copies(ins, outs, scr):
        send, recv = scr
        x, y, c, _ = _place()
        return [pltpu.make_async_remote_copy(
            src_ref=ins[t], dst_ref=outs[t], send_sem=send.at[t], recv_sem=recv.at[t],
            device_id=(x, y, 1 - c), device_id_type=MESH) for t in range(nw)]

    def start(ins, outs, scr):
        for cp in copies(ins, outs, scr):
            cp.start()

    def finish(ins, outs, scr):
        cps = copies(ins, outs, scr)
        for cp in cps:
            cp.wait_recv()
        for cp in cps:
            cp.wait_send()

    return _Rider(parts, [jax.ShapeDtypeStruct(p.shape, p.dtype) for p in parts],
                  [pltpu.SemaphoreType.DMA((nw,)), pltpu.SemaphoreType.DMA((nw,))], start, finish)


def _sum_all_devices(buf, name):
    rows, cols = buf.shape

    def body(in_ref, o_ref, gat_ref, send, recv):
        x, y, c, _ = _place()
        me = 4 * x + 2 * y + c
        gat_ref[me] = in_ref[...]
        started = []
        for mask in range(1, 8):
            fx, fy, fc = (mask >> 2) & 1, (mask >> 1) & 1, mask & 1
            peer = (x + fx * (1 - 2 * x), y + fy * (1 - 2 * y), c + fc * (1 - 2 * c))
            cp = pltpu.make_async_remote_copy(
                src_ref=in_ref, dst_ref=gat_ref.at[me], send_sem=send.at[mask - 1], recv_sem=recv.at[mask - 1],
                device_id=peer, device_id_type=MESH)
            cp.start()
            started.append(cp)
        for cp in started:
            cp.wait_recv()
        for cp in started:
            cp.wait_send()
        acc = gat_ref[0]
        for t in range(1, 8):
            acc = acc + gat_ref[t]
        o_ref[...] = acc

    vm = pl.BlockSpec(memory_space=pltpu.VMEM)
    return _pcall(
        body, name=name, in_specs=[vm], out_specs=vm, out_shape=jax.ShapeDtypeStruct((rows, cols), F32),
        scratch=[pltpu.VMEM((8, rows, cols), F32), pltpu.SemaphoreType.DMA((7,)), pltpu.SemaphoreType.DMA((7,))],
    )(buf)


_UPD_T = 256


def _sum_partials(own, got, name):
    rows, cols = own.shape
    tr = min(_UPD_T, rows)

    def body(own_ref, got_ref, o_ref):
        acc = own_ref[...].astype(F32)
        for j in range(3):
            acc = acc + got_ref[j].astype(F32)
        o_ref[...] = acc

    blk = pl.BlockSpec((tr, cols), lambda i: (i, 0))
    return _pcall(
        body, name=name, grid=(rows // tr,),
        in_specs=[blk, pl.BlockSpec((3, tr, cols), lambda i: (0, i, 0))], out_specs=blk,
        out_shape=jax.ShapeDtypeStruct((rows, cols), F32))(own, got)


def _adamw_math(w, gr, m, v):
    m = _B1 * m + (1.0 - _B1) * gr
    v = _B2 * v + (1.0 - _B2) * (gr * gr)
    m_hat = m / (1.0 - _B1 ** _STEP)
    v_hat = v / (1.0 - _B2 ** _STEP)
    delta = -_LR * (m_hat / (jnp.sqrt(v_hat) + _EPS) + _WD * w)
    return delta, m, v


def _adamw(w, m, v, parts, name):
    rows, cols = w.shape
    tr = min(_UPD_T, rows)
    npart = len(parts)

    def body(w_ref, m_ref, v_ref, *rest):
        p_refs, (g_ref, d_ref, nm_ref, nv_ref) = rest[:npart], rest[npart:]
        gr = p_refs[0][...]
        for p in p_refs[1:]:
            gr = gr + p[...]
        delta, nm, nv = _adamw_math(w_ref[...], gr, m_ref[...], v_ref[...])
        g_ref[...] = gr
        d_ref[...] = delta
        nm_ref[...] = nm
        nv_ref[...] = nv

    blk = pl.BlockSpec((tr, cols), lambda i: (i, 0))
    sh = jax.ShapeDtypeStruct((rows, cols), F32)
    return _pcall(body, name=name, grid=(rows // tr,), in_specs=[blk] * (3 + npart), out_specs=[blk] * 4,
                  out_shape=[sh] * 4)(w, m, v, *parts)


def _adamw_layers(w, m, v, parts, name):
    _, rows, cols = w.shape
    tr = min(_UPD_T, rows)
    npart = len(parts[0])

    def body(w_ref, m_ref, v_ref, *rest):
        p_refs, (g_ref, d_ref, nm_ref, nv_ref) = rest[:2 * npart], rest[2 * npart:]
        grs = []
        for layer in range(2):
            gr = p_refs[layer * npart][...]
            for p in p_refs[layer * npart + 1:(layer + 1) * npart]:
                gr = gr + p[...]
            grs.append(gr)
        gr = jnp.where(pl.program_id(0) == 0, grs[0], grs[1])
        delta, nm, nv = _adamw_math(w_ref[...], gr, m_ref[...], v_ref[...])
        g_ref[...] = gr
        d_ref[...] = delta
        nm_ref[...] = nm
        nv_ref[...] = nv

    blk = pl.BlockSpec((None, tr, cols), lambda l, i: (l, i, 0))

    def part_spec(layer):
        return pl.BlockSpec((tr, cols), lambda l, i: (jnp.where(l == layer, i, 0), 0))

    sh = jax.ShapeDtypeStruct(w.shape, F32)
    return _pcall(
        body, name=name, grid=(2, rows // tr),
        in_specs=[blk] * 3 + [part_spec(0)] * npart + [part_spec(1)] * npart, out_specs=[blk] * 4,
        out_shape=[sh] * 4)(w, m, v, *parts[0], *parts[1])


_PACK_W = 1024


def _pack(arrs, rows):
    flat = []
    for a in arrs:
        f = a.reshape(-1).astype(F32)
        pad = (-f.shape[0]) % _PACK_W
        flat.append(jnp.pad(f, (0, pad)))
    f = jnp.concatenate(flat)
    f = jnp.pad(f, (0, rows * _PACK_W - f.shape[0]))
    return f.reshape(rows, _PACK_W)


def _unpack(buf, shapes):
    flat = buf.reshape(-1)
    out, pos = [], 0
    for sh in shapes:
        size = math.prod(sh)
        out.append(flat[pos:pos + size].reshape(sh))
        pos += size + ((-size) % _PACK_W)
    return out


def _pack_rows(shapes):
    total = sum(-(-math.prod(sh) // _PACK_W) for sh in shapes)
    return -(-total // 8) * 8


def kernel(x, rel_bias, ab_norm, ab_w_in, ab_conv_w, ab_conv_b, ab_ln_g, ab_ln_b, ab_w_out, sc_norm, sc_w_in, sc_conv_w, sc_w_out, mlp_norm, mlp_w_up, mlp_w_down, final_norm, loss_target, m_rel_bias, m_ab_norm, m_ab_w_in, m_ab_conv_w, m_ab_conv_b, m_ab_ln_g, m_ab_ln_b, m_ab_w_out, m_sc_norm, m_sc_w_in, m_sc_conv_w, m_sc_w_out, m_mlp_norm, m_mlp_w_up, m_mlp_w_down, m_final_norm, v_rel_bias, v_ab_norm, v_ab_w_in, v_ab_conv_w, v_ab_conv_b, v_ab_ln_g, v_ab_ln_b, v_ab_w_out, v_sc_norm, v_sc_w_in, v_sc_conv_w, v_sc_w_out, v_mlp_norm, v_mlp_w_up, v_mlp_w_down, v_final_norm):
    s, d = x.shape[1], x.shape[2]
    dff = 4 * d
    c = _GROUP_COLS
    chip = 2 * lax.axis_index("x") + lax.axis_index("y")
    on_c0 = (lax.axis_index("c") == 0).astype(F32)
    h0 = x[0]
    tgt = loss_target[0]

    cw_sh, scn_sh, scw_sh = ab_conv_w[0], sc_norm, sc_conv_w[0]
    conv_w_full = lax.dynamic_update_slice(jnp.zeros((_CONV_K, c), F32), cw_sh * on_c0, (0, chip * cw_sh.shape[1]))
    scn_full = lax.dynamic_update_slice(jnp.zeros((1, d), F32), scn_sh * on_c0, (0, chip * scn_sh.shape[1]))
    scw_full = lax.dynamic_update_slice(jnp.zeros((3, d), F32), scw_sh * on_c0, (0, chip * scw_sh.shape[1]))
    small_shapes = [(_CONV_K, c), (1, d), (3, d)]
    small = _sum_all_devices(_pack([conv_w_full, scn_full, scw_full], _pack_rows(small_shapes)), "gather_small")
    conv_w, sc_g, sc_cw = _unpack(small, small_shapes)

    w_shards = [ab_w_in[0], ab_w_out[0], sc_w_in[0], sc_w_out[0], mlp_w_up[0], mlp_w_up[1],
                mlp_w_down[0], mlp_w_down[1]]
    w_axes = [1, 0, 1, 0, 1, 1, 0, 0]
    wb = [w.astype(BF16) for w in w_shards]
    full_w = [None] * 8

    def gather(idx):
        return _join_riders([_gather_halves_rider(wb[t], w_axes[t]) for t in idx])

    def put(idx, got_w):
        for t, w in zip(idx, got_w):
            full_w[t] = w

    buckets = _bucket_maps()
    bias = _bias_expand(rel_bias, buckets)
    n0, got_w = _rms_fwd(h0, ab_norm, "rms_fwd_ab", rider=_gather_halves_rider(wb[0], w_axes[0]))
    put([0], got_w)
    w_in = full_w[0]
    tm = min(1024, s)
    tm2 = min(2048, s)
    tmh = min(512, s)
    uc = _mm(n0, w_in, "nn", m=s, n=2 * c, k=d, tm=tm2, tn=2 * c, tk=d, out_dtype=BF16, name="proj_conv")
    uq, uk, uv = [], [], []
    for t, (nm, dst) in enumerate(zip("qkv", (uq, uk, uv))):
        res = _mm(n0, w_in, "nn", m=s, n=3 * c, k=d, tm=tm2, tn=c, tk=d, out_dtype=F32, name=f"proj_{nm}",
                  b_off=(0, 2 + 3 * t), split="o", rider=gather([1]) if t == 0 else None)
        if t == 0:
            res, got_w = res
            put([1], got_w)
        dst.append(res)
    uq, uk, uv = uq[0], uk[0], uv[0]
    (cat, ca), got_w = _conv_a_fwd(uc, conv_w, ab_conv_b, ab_ln_g, ab_ln_b, rider=gather([2]))
    put([2], got_w)
    outs, lses = [], []
    for g, (_, dil) in enumerate(_GROUPS):
        idx = ([4], [6], [3, 5])[g]
        (o, l), got_w = _attn_fwd(uq, uk, uv, bias, g, dil, 4 if dil <= 4 else 2, rider=gather(idx))
        put(idx, got_w)
        outs.append(o)
        lses.append(l)
    cat, lse = _attn_merge(outs, lses, cat)
    h1, n1 = _mm(cat, full_w[1], "nn", m=s, n=d, k=d, tm=tm, tn=d, tk=d, out_dtype=(F32, BF16), name="out_ab",
                 epi=_epi_add_rms, extras=(h0,), vecs=(mlp_norm[0:1],))

    def mlp_fwd(h, nrm, layer, next_gain=None, rider=None):
        zr = _mm(nrm, full_w[4 + layer], "nn", m=s, n=dff, k=d, tm=tmh, tn=dff, tk=d, out_dtype=BF16,
                 name=f"mlp_up{layer}", epi=_epi_relu, rider=rider)
        if rider is not None:
            zr, got_r = zr
            put([7], got_r)
        kw = dict(m=s, n=d, k=dff, tm=tmh, tn=d, tk=dff, name=f"mlp_down{layer}", a_pro=_square, extras=(h,))
        if next_gain is None:
            return zr, _mm(zr, full_w[6 + layer], "nn", out_dtype=F32, epi=_epi_add, **kw), None
        hn, nn = _mm(zr, full_w[6 + layer], "nn", out_dtype=(F32, BF16), epi=_epi_add_rms, vecs=(next_gain,), **kw)
        return zr, hn, nn

    zr0, h2, n2 = mlp_fwd(h1, n1, 0, next_gain=sc_g, rider=gather([7]))
    _, w_out, w_si, w_so, w_up0, w_up1, w_dn0, w_dn1 = full_w
    w_up, w_dn = [w_up0, w_up1], [w_dn0, w_dn1]
    u2 = _mm(n2, w_si, "nn", m=s, n=3 * d, k=d, tm=tmh, tn=3 * d, tk=d, out_dtype=BF16, name="proj_sc")
    scv = _short_conv_fwd(u2, sc_cw)
    h3, n3 = _mm(scv, w_so, "nn", m=s, n=d, k=d, tm=tm, tn=d, tk=d, out_dtype=(F32, BF16), name="out_sc",
                 epi=_epi_add_rms, extras=(h2,), vecs=(mlp_norm[1:2],))
    zr1, h4, _ = mlp_fwd(h3, n3, 1)

    dh4, dh4b, g_final, loss_part = _loss_head(h4, tgt, final_norm.reshape(1, d))
    tkw = min(2048, s)

    big_grads, got, sums = [None] * 8, [None] * 8, [None] * 8

    def scatter(t):
        return _scatter_rider([big_grads[t]], [w_axes[t]])

    def own_slab(t):
        width = big_grads[t].shape[w_axes[t]] // 4
        return lax.dynamic_slice_in_dim(big_grads[t], chip * width, width, axis=w_axes[t])

    def arrived(t, got_t):
        got[t] = got_t[0]
        sums[t] = _sum_partials(own_slab(t), got[t], f"sum_partials{t}")

    def mlp_bwd(dh, dhb, h, nrm, zr, layer):
        dz = _mm(dhb, w_dn[layer], "nt", m=s, n=dff, k=d, tm=tmh, tn=dff, tk=d, out_dtype=BF16,
                 name=f"mlp_down{layer}_dx", epi=_epi_relu_sq_bwd, extras=(zr,))
        big_grads[6 + layer] = _mm(zr, dhb, "tn", m=dff, n=d, k=s, tm=1024, tn=d, tk=tkw, out_dtype=BF16,
                                   name=f"mlp_down{layer}_dw", a_pro=_square)
        big_grads[4 + layer] = _mm(nrm, dz, "tn", m=d, n=dff, k=s, tm=d, tn=1024, tk=tkw, out_dtype=BF16,
                                   name=f"mlp_up{layer}_dw")
        res = _mm(dz, w_up[layer], "nt", m=s, n=d, k=dff, tm=tmh, tn=d, tk=dff, out_dtype=(F32, BF16),
                  name=f"mlp_up{layer}_dx", rider=scatter(6) if layer == 0 else None, epi=_epi_rms_bwd,
                  extras=(h, dh), vecs=(mlp_norm[layer:layer + 1],), row_sum=True)
        if layer == 0:
            res, got_t = res
            arrived(6, got_t)
        return res

    dh3, dh3b, g_mn1 = mlp_bwd(dh4, dh4b, h3, n3, zr1, 1)

    dsc = _mm(dh3b, w_so, "nt", m=s, n=d, k=d, tm=tm, tn=d, tk=d, out_dtype=F32, name="out_sc_dx")
    big_grads[3] = _mm(scv, dh3b, "tn", m=d, n=d, k=s, tm=d, tn=d, tk=tkw, out_dtype=BF16, name="out_sc_dw")
    du2, g_sccw8 = _short_conv_bwd(u2, dsc, sc_cw)
    big_grads[2], got_t = _mm(n2, du2, "tn", m=d, n=3 * d, k=s, tm=d, tn=1024, tk=tkw, out_dtype=BF16,
                              name="proj_sc_dw", rider=scatter(3))
    arrived(3, got_t)
    dh2, dh2b, g_scn = _mm(
        du2, w_si, "nt", m=s, n=d, k=3 * d, tm=tmh, tn=d, tk=3 * d, out_dtype=(F32, BF16), name="proj_sc_dx",
        epi=_epi_rms_bwd, extras=(h2, dh3), vecs=(sc_g,), row_sum=True)

    dh1, dh1b, g_mn0 = mlp_bwd(dh2, dh2b, h1, n1, zr0, 0)

    dcat = _mm(dh1b, w_out, "nt", m=s, n=d, k=d, tm=tm, tn=d, tk=d, out_dtype=F32, name="out_ab_dx")
    big_grads[1] = _mm(cat, dh1b, "tn", m=d, n=d, k=s, tm=d, tn=d, tk=tkw, out_dtype=BF16, name="out_ab_dw")
    (dca, conv_stats), got_t = _conv_a_bwd_ln(ca, dcat, ab_ln_g, ab_ln_b, rider=scatter(1))
    arrived(1, got_t)
    (duc, g_cw32), got_t = _conv_a_bwd_conv(uc, dca, conv_w, rider=scatter(4))
    arrived(4, got_t)
    delta, dys = _attn_delta(dcat, cat)

    dqkv, ds_list = [], []
    for g, (_, dil) in enumerate(_GROUPS):
        late = (2, 5, 7)[g]
        (grads, dsa), got_t = _attn_bwd(uq, uk, uv, dys, lse, delta, bias, dqkv, g, dil, 4 if dil <= 4 else 1,
                                        rider=scatter(late))
        arrived(late, got_t)
        dqkv = [grads]
        ds_list.append(dsa)
    g_bias = _bias_reduce(jnp.concatenate(ds_list, axis=0), buckets)[:, :, 0].T
    dqkv = dqkv[0].reshape(9 * _PAIRS, s, _LANES)

    g_in_conv = _mm(n0, duc, "tn", m=d, n=2 * c, k=s, tm=d, tn=2 * c, tk=tkw, out_dtype=BF16,
                    name="proj_ab_dw_conv")
    g_in_qkv, sib_late = _mm(n0, dqkv, "tn", m=d, n=9 * c, k=s, tm=d, tn=3 * c, tk=min(1024, s), out_dtype=BF16,
                             name="proj_ab_dw_qkv", rider=_swap_rider(sums[1:]), split="b")
    big_grads[0] = jnp.concatenate([g_in_conv, g_in_qkv], axis=1)
    dn0 = _mm(duc, w_in, "nt", m=s, n=d, k=2 * c, tm=tm, tn=d, tk=2 * c, out_dtype=F32, name="proj_ab_dx_conv")
    dn0, got_t = _mm(dqkv, w_in[:, 2 * c:], "nt", m=s, n=d, k=9 * c, tm=tm, tn=d, tk=3 * c, out_dtype=F32,
                     name="proj_ab_dx_qkv", epi=_epi_add, extras=(dn0,), rider=scatter(0), split="a")
    arrived(0, got_t)
    grad_x, _, g_abn = _rms_bwd(dn0, h0, ab_norm, dh1, "rms_bwd_ab")
    sib = list(_run_rider(_swap_rider([sums[0]]), "swap_sibling_w_in")) + sib_late

    upd = [_adamw(w_shards[t], mm[0], vv[0], [sums[t], sib[t]], f"adamw{t}")
           for t, (mm, vv) in enumerate(((m_ab_w_in, v_ab_w_in), (m_ab_w_out, v_ab_w_out),
                                         (m_sc_w_in, v_sc_w_in), (m_sc_w_out, v_sc_w_out)))]
    upd_up = _adamw_layers(mlp_w_up, m_mlp_w_up, v_mlp_w_up, [[sums[4], sib[4]], [sums[5], sib[5]]], "adamw_up")
    upd_dn = _adamw_layers(mlp_w_down, m_mlp_w_down, v_mlp_w_down, [[sums[6], sib[6]], [sums[7], sib[7]]],
                           "adamw_down")

    full_shapes = [(_NUM_BUCKETS, rel_bias.shape[1]), (1, d), (_CONV_K, c), (1, c), (1, c), (1, c), (1, d),
                   (3, d), (2, d), (d,), (1, 1)]
    small_grads = [g_bias, g_abn, g_cw32[:_CONV_K], conv_stats[0:1], conv_stats[1:2], conv_stats[2:3], g_scn,
                   g_sccw8[:3], jnp.concatenate([g_mn0, g_mn1], axis=0), g_final.reshape(d), loss_part[0:1, 0:1]]
    tot = _unpack(_sum_all_devices(_pack(small_grads, _pack_rows(full_shapes)), "sum_small"), full_shapes)
    loss = tot.pop()[0, 0]
    for idx, sh in ((2, cw_sh), (6, scn_sh), (7, scw_sh)):
        width = sh.shape[1]
        tot[idx] = lax.dynamic_slice_in_dim(tot[idx], chip * width, width, axis=1)
    sm_w = [rel_bias, ab_norm, cw_sh, ab_conv_b, ab_ln_g, ab_ln_b, scn_sh, scw_sh, mlp_norm, final_norm]
    sm_m = [m_rel_bias, m_ab_norm, m_ab_conv_w[0], m_ab_conv_b, m_ab_ln_g, m_ab_ln_b, m_sc_norm, m_sc_conv_w[0],
            m_mlp_norm, m_final_norm]
    sm_v = [v_rel_bias, v_ab_norm, v_ab_conv_w[0], v_ab_conv_b, v_ab_ln_g, v_ab_ln_b, v_sc_norm, v_sc_conv_w[0],
            v_mlp_norm, v_final_norm]
    sh_shapes = [tuple(t.shape) for t in tot]
    rows = _pack_rows(sh_shapes)
    sm_upd = _adamw(_pack(sm_w, rows), _pack(sm_m, rows), _pack(sm_v, rows), [_pack(tot, rows)], "adamw_small")
    sm_g, sm_d, sm_nm, sm_nv = [_unpack(buf, sh_shapes) for buf in sm_upd]

    def assemble(kind, sm):
        big = [u[kind] for u in upd]
        return [sm[0], sm[1], big[0][None], sm[2][None], sm[3], sm[4], sm[5], big[1][None], sm[6], big[2][None],
                sm[7][None], big[3][None], sm[8], upd_up[kind], upd_dn[kind], sm[9]]

    res = [loss, grad_x[None]]
    for kind, sm in enumerate((sm_g, sm_d, sm_nm, sm_nv)):
        res += assemble(kind, sm)
    return tuple(res)
```

```python
import functools
import math

import jax
import jax.numpy as jnp
from jax import lax
from jax.experimental import pallas as pl
from jax.experimental.pallas import tpu as pltpu

F32 = jnp.float32
BF16 = jnp.bfloat16
MESH = pl.DeviceIdType.MESH

_GROUPS = ((128, 1), (512, 4), (2048, 16))
_STEPS = 128
_HEAD_DIM = 64
_GROUP_COLS = 512
_NUM_BUCKETS = 32
_MAX_DISTANCE = 2048
_CONV_K = 31
_HALO = 32
_SC_HALO = 16
_RMS_EPS = 1e-6
_LN_EPS = 1e-5
_NEG = -1e30
_LANES = 128
_VMEM_LIMIT = 56 * 1024 * 1024

_LR, _B1, _B2, _EPS, _WD, _STEP = 0.001, 0.9, 0.999, 1e-08, 0.01, 10


class _Rider:
    def __init__(self, ins, out_shapes, scratch, start, finish):
        self.ins, self.out_shapes, self.scratch = list(ins), list(out_shapes), list(scratch)
        self.start, self.finish = start, finish


def _pcall(body, *, name, out_shape, in_specs, out_specs, grid=None, scratch=(), aliases=None, rider=None):
    kw = {} if grid is None else {"grid": grid}
    cparams = pltpu.CompilerParams(vmem_limit_bytes=_VMEM_LIMIT)
    if rider is None:
        return pl.pallas_call(
            body, name=name, out_shape=out_shape, in_specs=in_specs, out_specs=out_specs,
            scratch_shapes=list(scratch), input_output_aliases=aliases or {},
            compiler_params=cparams, **kw)
    single = not isinstance(out_specs, (list, tuple))
    ospecs = [out_specs] if single else list(out_specs)
    oshapes = [out_shape] if single else list(out_shape)
    nin, nout, nscr = len(in_specs), len(ospecs), len(scratch)
    rin, rout = len(rider.ins), len(rider.out_shapes)

    def wrapped(*refs):
        h_in, r_in = refs[:nin], refs[nin:nin + rin]
        p = nin + rin
        h_out, r_out = refs[p:p + nout], refs[p + nout:p + nout + rout]
        p += nout + rout
        h_scr, r_scr = refs[p:p + nscr], refs[p + nscr:]
        ids = [pl.program_id(a) for a in range(len(grid))]
        first = functools.reduce(jnp.logical_and, [i == 0 for i in ids])
        last = functools.reduce(jnp.logical_and, [i == g - 1 for i, g in zip(ids, grid)])

        @pl.when(first)
        def _():
            rider.start(r_in, r_out, r_scr)

        body(*h_in, *h_out, *h_scr)

        @pl.when(last)
        def _():
            rider.finish(r_in, r_out, r_scr)

    call = pl.pallas_call(
        wrapped, name=name, out_shape=oshapes + rider.out_shapes,
        in_specs=list(in_specs) + [_ANY] * rin, out_specs=ospecs + [_ANY] * rout,
        scratch_shapes=list(scratch) + rider.scratch, input_output_aliases=aliases or {},
        compiler_params=cparams, **kw)

    def run(*operands):
        res = call(*operands, *rider.ins)
        host = res[0] if single else list(res[:nout])
        return host, list(res[nout:])

    return run


def _sig(x):
    return 1.0 / (1.0 + jnp.exp(-x))


_ANY = pl.BlockSpec(memory_space=pl.ANY)


def _lanes_of(ref):
    parts = [ref[p] for p in range(ref.shape[0])]
    return parts[0] if len(parts) == 1 else jnp.concatenate(parts, axis=1)


def _mm(a, b, mode, *, m, n, k, tm, tn, tk, out_dtype, name, epi=None, extras=(), b_off=(0, 0), rider=None,
        split="", vecs=(), a_pro=None, row_sum=False):
    nk = k // tk
    assert m % tm == 0 and n % tn == 0 and k % tk == 0
    o0, o1 = b_off
    if mode == "nn":
        a_spec = pl.BlockSpec((tm, tk), lambda i, j, kk: (i, kk))
        b_spec = pl.BlockSpec((tk, tn), lambda i, j, kk: (kk + o0, j + o1))
        dn = (((1,), (0,)), ((), ()))
    elif mode == "nt":
        a_spec = pl.BlockSpec((tm, tk), lambda i, j, kk: (i, kk))
        if "a" in split:
            a_spec = pl.BlockSpec((tk // _LANES, tm, _LANES), lambda i, j, kk: (kk, i, 0))
        b_spec = pl.BlockSpec((tn, tk), lambda i, j, kk: (j + o0, kk + o1))
        dn = (((1,), (1,)), ((), ()))
    else:
        a_spec = pl.BlockSpec((tk, tm), lambda i, j, kk: (kk, i))
        b_spec = pl.BlockSpec((tk, tn), lambda i, j, kk: (kk + o0, j + o1))
        if "b" in split:
            b_spec = pl.BlockSpec((tn // _LANES, tk, _LANES), lambda i, j, kk: (j, kk, 0))
        dn = (((0,), (0,)), ((), ()))
    o_spec = pl.BlockSpec((tm, tn), lambda i, j, kk: (i, j))
    e_spec = o_spec
    if "o" in split:
        o_spec = pl.BlockSpec((tn // _LANES, tm, _LANES), lambda i, j, kk: (j, i, 0))
    v_spec = pl.BlockSpec((1, tn), lambda i, j, kk: (0, j))
    ne = len(extras) + len(vecs)
    multi = isinstance(out_dtype, tuple)
    dts = out_dtype if multi else (out_dtype,)
    no = len(dts)
    nr = 1 if row_sum else 0
    assert not row_sum or tn == n

    def body(a_ref, b_ref, *rest):
        ex, o_refs = rest[:ne], rest[ne:ne + no]
        av = _lanes_of(a_ref) if "a" in split else a_ref[...]
        bv = _lanes_of(b_ref) if "b" in split else b_ref[...]
        if av.dtype != BF16:
            av = av.astype(BF16)
        if bv.dtype != BF16:
            bv = bv.astype(BF16)
        if a_pro is not None:
            av = a_pro(av)
        p = lax.dot_general(av, bv, dn, preferred_element_type=F32)

        def fin(x):
            if epi is not None:
                x = epi(x, *[e[...] for e in ex])
            if row_sum:
                row, x = x[-1], (x[:-1] if multi else x[0])
                row_ref = rest[ne + no]

                @pl.when(pl.program_id(0) == 0)
                def _():
                    row_ref[...] = row

                @pl.when(pl.program_id(0) > 0)
                def _():
                    row_ref[...] += row

            for o_ref, val, dt in zip(o_refs, x if multi else (x,), dts):
                if "o" in split:
                    for p in range(tn // _LANES):
                        o_ref[p] = val[:, p * _LANES:(p + 1) * _LANES].astype(dt)
                else:
                    o_ref[...] = val.astype(dt)

        if nk == 1:
            fin(p)
        else:
            acc = rest[ne + no + nr]
            kk = pl.program_id(2)

            @pl.when(kk == 0)
            def _():
                acc[...] = p

            @pl.when(kk > 0)
            def _():
                acc[...] += p

            @pl.when(kk == nk - 1)
            def _():
                fin(acc[...])

    oshape = (n // _LANES, m, _LANES) if "o" in split else (m, n)
    shapes = [jax.ShapeDtypeStruct(oshape, dt) for dt in dts]
    ospecs = [o_spec] * no
    if row_sum:
        shapes.append(jax.ShapeDtypeStruct((1, n), F32))
        ospecs.append(v_spec)
    lone = not multi and not row_sum
    return _pcall(
        body, name=name, grid=(m // tm, n // tn, nk),
        in_specs=[a_spec, b_spec] + [e_spec] * len(extras) + [v_spec] * len(vecs),
        out_specs=ospecs[0] if lone else ospecs, out_shape=shapes[0] if lone else shapes,
        scratch=[pltpu.VMEM((tm, tn), F32)] if nk > 1 else [], rider=rider,
    )(a, b, *extras, *vecs)


def _epi_add(x, r):
    return x + r


def _epi_relu(x):
    return jnp.maximum(x, 0.0)


def _square(x):
    return x * x


def _epi_relu_sq_bwd(da, zr):
    return da * (2.0 * zr.astype(F32))


def _epi_add_rms(x, r, g):
    h = x + r
    return h, h * lax.rsqrt(jnp.mean(h * h, axis=-1, keepdims=True) + _RMS_EPS) * g


def _epi_add_rms_bwd(dn, dn_more, h, dh_in, g):
    dx, dg = _rms_bwd_math(dn + dn_more, h, g)
    return dh_in + dx, dg


def _epi_rms_bwd(dn, h, dh_in, g):
    dx, dg = _rms_bwd_math(dn, h, g)
    dh = dh_in + dx
    return dh, dh, dg


_ROW_T = 512


def _rms_fwd(h, g, name, rider=None):
    s, d = h.shape

    def body(h_ref, g_ref, o_ref):
        x = h_ref[...]
        r = lax.rsqrt(jnp.mean(x * x, axis=-1, keepdims=True) + _RMS_EPS)
        o_ref[...] = (x * r * g_ref[...]).astype(BF16)

    row = pl.BlockSpec((_ROW_T, d), lambda i: (i, 0))
    vec = pl.BlockSpec((1, d), lambda i: (0, 0))
    return _pcall(body, name=name, grid=(s // _ROW_T,), in_specs=[row, vec], out_specs=row,
                  out_shape=jax.ShapeDtypeStruct((s, d), BF16), rider=rider)(h, g)


def _rms_bwd_math(dn, x, g):
    r = lax.rsqrt(jnp.mean(x * x, axis=-1, keepdims=True) + _RMS_EPS)
    xhat = x * r
    dg = jnp.sum(dn * xhat, axis=0, keepdims=True)
    t = dn * g
    dx = r * (t - xhat * jnp.mean(t * xhat, axis=-1, keepdims=True))
    return dx, dg


def _loss_head(h, tgt, g):
    s, d = h.shape

    def body(h_ref, t_ref, g_ref, dh_ref, dhb_ref, dg_ref, loss_ref):
        x, gv = h_ref[...], g_ref[...]
        r = lax.rsqrt(jnp.mean(x * x, axis=-1, keepdims=True) + _RMS_EPS)
        err = x * r * gv - t_ref[...]
        part = 0.5 * jnp.sum(jnp.mean(err * err, axis=-1, keepdims=True))
        dx, dg = _rms_bwd_math(err * (1.0 / d), x, gv)
        dh_ref[...] = dx
        dhb_ref[...] = dx.astype(BF16)

        @pl.when(pl.program_id(0) == 0)
        def _():
            dg_ref[...] = jnp.zeros_like(dg_ref)
            loss_ref[...] = jnp.zeros_like(loss_ref)

        dg_ref[...] += dg
        loss_ref[...] += jnp.full(loss_ref.shape, part, F32)

    row = pl.BlockSpec((_ROW_T, d), lambda i: (i, 0))
    vec = pl.BlockSpec((1, d), lambda i: (0, 0))
    one = pl.BlockSpec((1, _LANES), lambda i: (0, 0))
    return _pcall(
        body, name="loss_head", grid=(s // _ROW_T,), in_specs=[row, row, vec], out_specs=[row, row, vec, one],
        out_shape=[jax.ShapeDtypeStruct((s, d), F32), jax.ShapeDtypeStruct((s, d), BF16),
                   jax.ShapeDtypeStruct((1, d), F32), jax.ShapeDtypeStruct((1, _LANES), F32)])(h, tgt, g)


_CONV_T = 256
_CONV_RC = 64


def _conv_a_specs(s):
    c = _GROUP_COLS
    hb = _CONV_T // _HALO
    val = pl.BlockSpec((_CONV_T, c), lambda i: (i, 0))
    gate = pl.BlockSpec((_CONV_T, c), lambda i: (i, 1))
    hval = pl.BlockSpec((_HALO, c), lambda i: (jnp.maximum(i * hb - 1, 0), 0))
    hgate = pl.BlockSpec((_HALO, c), lambda i: (jnp.maximum(i * hb - 1, 0), 1))
    return val, gate, hval, hgate


def _fill_glu(val_ref, gate_ref, hval_ref, hgate_ref, hs_ref):
    i = pl.program_id(0)
    hs_ref[pl.ds(_HALO, _CONV_T), :] = val_ref[...].astype(F32) * _sig(gate_ref[...].astype(F32))
    halo = hval_ref[...].astype(F32) * _sig(hgate_ref[...].astype(F32))
    hs_ref[pl.ds(0, _HALO), :] = jnp.where(i > 0, halo, 0.0)


_SHIFT_ROWS = _CONV_T + _HALO - 8


def _fill_shifts(src_ref, sh_ref):
    for b in range(1, 8):
        sh_ref[b - 1] = src_ref[pl.ds(b, _SHIFT_ROWS), :]


def _tap_rows(src_ref, sh_ref, start, rows, lanes=slice(None)):
    b = start % 8
    if b == 0:
        return src_ref[pl.ds(start, rows), lanes]
    return sh_ref[b - 1, pl.ds(start - b, rows), lanes]


def _conv_rows(hs_ref, sh_ref, w_ref, r0, rows):
    off = _HALO - (_CONV_K - 1)
    acc = jnp.zeros((rows, _GROUP_COLS), F32)
    for kk in range(_CONV_K):
        acc = acc + w_ref[kk:kk + 1, :] * _tap_rows(hs_ref, sh_ref, r0 + off + kk, rows)
    return acc


def _ln_fwd(ca, g, b):
    mu = jnp.mean(ca, axis=-1, keepdims=True)
    xc = ca - mu
    rstd = lax.rsqrt(jnp.mean(xc * xc, axis=-1, keepdims=True) + _LN_EPS)
    xhat = xc * rstd
    return xhat, rstd, xhat * g + b


def _conv_a_fwd(uc, w, cb, lg, lb, rider=None):
    s = uc.shape[0]
    c = _GROUP_COLS

    def body(val_ref, gate_ref, hval_ref, hgate_ref, w_ref, cb_ref, lg_ref, lb_ref, o_ref, ca_ref, hs_ref, sh_ref):
        _fill_glu(val_ref, gate_ref, hval_ref, hgate_ref, hs_ref)
        _fill_shifts(hs_ref, sh_ref)
        for rc in range(_CONV_T // _CONV_RC):
            r0 = rc * _CONV_RC
            ca = _conv_rows(hs_ref, sh_ref, w_ref, r0, _CONV_RC) + cb_ref[...]
            ca_ref[pl.ds(r0, _CONV_RC), :] = ca
            _, _, ln = _ln_fwd(ca, lg_ref[...], lb_ref[...])
            o_ref[pl.ds(r0, _CONV_RC), :] = (ln * _sig(ln)).astype(BF16)

    val, gate, hval, hgate = _conv_a_specs(s)
    wspec = pl.BlockSpec((_CONV_K, c), lambda i: (0, 0))
    vec = pl.BlockSpec((1, c), lambda i: (0, 0))
    blk = pl.BlockSpec((_CONV_T, c), lambda i: (i, 0))
    return _pcall(
        body, name="conv_a_fwd", grid=(s // _CONV_T,),
        in_specs=[val, gate, hval, hgate, wspec, vec, vec, vec],
        out_specs=[blk, blk],
        out_shape=[jax.ShapeDtypeStruct((s, 2 * c), BF16), jax.ShapeDtypeStruct((s, c), F32)],
        scratch=[pltpu.VMEM((_CONV_T + _HALO, c), F32), pltpu.VMEM((7, _SHIFT_ROWS, c), F32)],
        rider=rider)(uc, uc, uc, uc, w, cb, lg, lb)


def _conv_a_bwd_ln(ca_all, dcat, lg, lb, rider=None):
    s = ca_all.shape[0]
    c = _GROUP_COLS

    def body(ca_ref, dy_ref, lg_ref, lb_ref, dca_ref, st_ref):
        @pl.when(pl.program_id(0) == 0)
        def _():
            st_ref[...] = jnp.zeros_like(st_ref)

        for rc in range(_CONV_T // _CONV_RC):
            r0 = rc * _CONV_RC
            ca = ca_ref[pl.ds(r0, _CONV_RC), :]
            xhat, rstd, ln = _ln_fwd(ca, lg_ref[...], lb_ref[...])
            sg = _sig(ln)
            dln = dy_ref[pl.ds(r0, _CONV_RC), :] * (sg * (1.0 + ln * (1.0 - sg)))
            dxh = dln * lg_ref[...]
            dca = rstd * (dxh - jnp.mean(dxh, axis=-1, keepdims=True)
                          - xhat * jnp.mean(dxh * xhat, axis=-1, keepdims=True))
            dca_ref[pl.ds(r0, _CONV_RC), :] = dca
            st_ref[0:1, :] += jnp.sum(dca, axis=0, keepdims=True)
            st_ref[1:2, :] += jnp.sum(dln * xhat, axis=0, keepdims=True)
            st_ref[2:3, :] += jnp.sum(dln, axis=0, keepdims=True)

    blk = pl.BlockSpec((_CONV_T, c), lambda i: (i, 0))
    vec = pl.BlockSpec((1, c), lambda i: (0, 0))
    st = pl.BlockSpec((8, c), lambda i: (0, 0))
    return _pcall(
        body, name="conv_a_bwd_ln", grid=(s // _CONV_T,),
        in_specs=[blk, blk, vec, vec], out_specs=[blk, st],
        out_shape=[jax.ShapeDtypeStruct((s, c), F32), jax.ShapeDtypeStruct((8, c), F32)],
        rider=rider)(ca_all, dcat, lg, lb)


def _conv_a_bwd_conv(uc, dca, w, rider=None):
    s = uc.shape[0]
    c = _GROUP_COLS
    nblk = s // _CONV_T
    hb = _CONV_T // _HALO
    off = _HALO - (_CONV_K - 1)

    def body(val_ref, gate_ref, hval_ref, hgate_ref, d_ref, dn_ref, w_ref, du_ref, dw_ref, hs_ref, ds_ref,
             hsh_ref, dsh_ref, dwa_ref):
        i = pl.program_id(0)
        _fill_glu(val_ref, gate_ref, hval_ref, hgate_ref, hs_ref)
        ds_ref[pl.ds(0, _CONV_T), :] = d_ref[...]
        ds_ref[pl.ds(_CONV_T, _HALO), :] = jnp.where(i < nblk - 1, dn_ref[...], 0.0)
        _fill_shifts(hs_ref, hsh_ref)
        _fill_shifts(ds_ref, dsh_ref)

        @pl.when(i == 0)
        def _():
            dwa_ref[...] = jnp.zeros_like(dwa_ref)

        rows = 32
        for r0 in range(0, _CONV_T, rows):
            dcur = ds_ref[pl.ds(r0, rows), :]
            dh = jnp.zeros((rows, c), F32)
            for kk in range(_CONV_K):
                dh = dh + w_ref[kk:kk + 1, :] * _tap_rows(ds_ref, dsh_ref, r0 + _CONV_K - 1 - kk, rows)
                prod = dcur * _tap_rows(hs_ref, hsh_ref, r0 + off + kk, rows)
                dwa_ref[pl.ds(8 * kk, 8), :] += sum(prod[t:t + 8] for t in range(0, rows, 8))
            v = val_ref[pl.ds(r0, rows), :].astype(F32)
            sg = _sig(gate_ref[pl.ds(r0, rows), :].astype(F32))
            du_ref[pl.ds(r0, rows), pl.ds(0, c)] = (dh * sg).astype(BF16)
            du_ref[pl.ds(r0, rows), pl.ds(c, c)] = (dh * v * sg * (1.0 - sg)).astype(BF16)

        @pl.when(i == nblk - 1)
        def _():
            dw_ref[...] = jnp.zeros_like(dw_ref)
            for kk in range(_CONV_K):
                dw_ref[kk:kk + 1, :] = jnp.sum(dwa_ref[pl.ds(8 * kk, 8), :], axis=0, keepdims=True)

    val, gate, hval, hgate = _conv_a_specs(s)
    blk = pl.BlockSpec((_CONV_T, c), lambda i: (i, 0))
    nxt = pl.BlockSpec((_HALO, c), lambda i: (jnp.minimum((i + 1) * hb, s // _HALO - 1), 0))
    wspec = pl.BlockSpec((_CONV_K, c), lambda i: (0, 0))
    return _pcall(
        body, name="conv_a_bwd_conv", grid=(nblk,),
        in_specs=[val, gate, hval, hgate, blk, nxt, wspec],
        out_specs=[pl.BlockSpec((_CONV_T, 2 * c), lambda i: (i, 0)), pl.BlockSpec((_HALO, c), lambda i: (0, 0))],
        out_shape=[jax.ShapeDtypeStruct((s, 2 * c), BF16), jax.ShapeDtypeStruct((_HALO, c), F32)],
        scratch=[pltpu.VMEM((_CONV_T + _HALO, c), F32), pltpu.VMEM((_CONV_T + _HALO, c), F32),
                 pltpu.VMEM((7, _SHIFT_ROWS, c), F32), pltpu.VMEM((7, _SHIFT_ROWS, c), F32),
                 pltpu.VMEM((8 * _HALO, c), F32)],
        rider=rider,
    )(uc, uc, uc, uc, dca, dca, w)


_SC_T = 256
_SC_RC = 32
_SC_LC = 512


def _sc_chunks(d):
    return [(pl.ds(r0, _SC_RC), pl.ds(l0, _SC_LC)) for r0 in range(0, _SC_T, _SC_RC) for l0 in range(0, d, _SC_LC)]


def _short_conv_fwd(u2, w):
    s, d3 = u2.shape
    d = d3 // 3
    hb = _SC_T // _SC_HALO

    def body(b_ref, c_ref, v_ref, hc_ref, hv_ref, w_ref, o_ref, cs_ref):
        i = pl.program_id(0)
        cs_ref[pl.ds(0, _SC_HALO), :] = jnp.where(i > 0, hc_ref[...].astype(F32) * hv_ref[...].astype(F32), 0.0)
        for rows, lanes in _sc_chunks(d):
            cs_ref[pl.ds(_SC_HALO + rows.start, _SC_RC), lanes] = (
                c_ref[rows, lanes].astype(F32) * v_ref[rows, lanes].astype(F32))
        for rows, lanes in _sc_chunks(d):
            taps = [cs_ref[pl.ds(_SC_HALO - 2 + kk + rows.start, _SC_RC), lanes] for kk in range(3)]
            conv = w_ref[0:1, lanes] * taps[0] + w_ref[1:2, lanes] * taps[1] + w_ref[2:3, lanes] * taps[2]
            o_ref[rows, lanes] = (b_ref[rows, lanes].astype(F32) * conv).astype(BF16)

    def col(j):
        return pl.BlockSpec((_SC_T, d), lambda i: (i, j))

    def halo(j):
        return pl.BlockSpec((_SC_HALO, d), lambda i: (jnp.maximum(i * hb - 1, 0), j))

    return _pcall(
        body, name="short_conv_fwd", grid=(s // _SC_T,),
        in_specs=[col(0), col(1), col(2), halo(1), halo(2), pl.BlockSpec((3, d), lambda i: (0, 0))],
        out_specs=pl.BlockSpec((_SC_T, d), lambda i: (i, 0)),
        out_shape=jax.ShapeDtypeStruct((s, d), BF16),
        scratch=[pltpu.VMEM((_SC_T + _SC_HALO, d), F32)])(u2, u2, u2, u2, u2, w)


def _short_conv_bwd(u2, dsc, w, rider=None):
    s, d3 = u2.shape
    d = d3 // 3
    hb = _SC_T // _SC_HALO
    nblk = s // _SC_T

    def body(b_ref, c_ref, v_ref, hc_ref, hv_ref, nb_ref, d_ref, nd_ref, w_ref, du_ref, dw_ref, cs_ref, ds_ref):
        i = pl.program_id(0)
        cs_ref[pl.ds(0, _SC_HALO), :] = jnp.where(i > 0, hc_ref[...].astype(F32) * hv_ref[...].astype(F32), 0.0)
        ds_ref[pl.ds(_SC_T, _SC_HALO), :] = jnp.where(i < nblk - 1, nd_ref[...] * nb_ref[...].astype(F32), 0.0)
        for rows, lanes in _sc_chunks(d):
            cs_ref[pl.ds(_SC_HALO + rows.start, _SC_RC), lanes] = (
                c_ref[rows, lanes].astype(F32) * v_ref[rows, lanes].astype(F32))
            ds_ref[rows, lanes] = d_ref[rows, lanes] * b_ref[rows, lanes].astype(F32)

        @pl.when(i == 0)
        def _():
            dw_ref[...] = jnp.zeros_like(dw_ref)

        for l0 in range(0, d, _SC_LC):
            lanes = pl.ds(l0, _SC_LC)
            dw_acc = [jnp.zeros((8, _SC_LC), F32)] * 3
            for r0 in range(0, _SC_T, _SC_RC):
                rows = pl.ds(r0, _SC_RC)
                taps = [cs_ref[pl.ds(_SC_HALO - 2 + kk + r0, _SC_RC), lanes] for kk in range(3)]
                conv = w_ref[0:1, lanes] * taps[0] + w_ref[1:2, lanes] * taps[1] + w_ref[2:3, lanes] * taps[2]
                dconv = ds_ref[rows, lanes]
                dcv = (w_ref[2:3, lanes] * dconv + w_ref[1:2, lanes] * ds_ref[pl.ds(r0 + 1, _SC_RC), lanes]
                       + w_ref[0:1, lanes] * ds_ref[pl.ds(r0 + 2, _SC_RC), lanes])
                du_ref[rows, lanes] = (d_ref[rows, lanes] * conv).astype(BF16)
                du_ref[rows, pl.ds(d + l0, _SC_LC)] = (dcv * v_ref[rows, lanes].astype(F32)).astype(BF16)
                du_ref[rows, pl.ds(2 * d + l0, _SC_LC)] = (dcv * c_ref[rows, lanes].astype(F32)).astype(BF16)
                for kk in range(3):
                    prod = dconv * taps[kk]
                    dw_acc[kk] = dw_acc[kk] + sum(prod[t:t + 8] for t in range(0, _SC_RC, 8))
            for kk in range(3):
                dw_ref[kk:kk + 1, lanes] += jnp.sum(dw_acc[kk], axis=0, keepdims=True)

    def col(j):
        return pl.BlockSpec((_SC_T, d), lambda i: (i, j))

    def halo(j):
        return pl.BlockSpec((_SC_HALO, d), lambda i: (jnp.maximum(i * hb - 1, 0), j))

    def nxt(j):
        return pl.BlockSpec((_SC_HALO, d), lambda i: (jnp.minimum((i + 1) * hb, s // _SC_HALO - 1), j))

    return _pcall(
        body, name="short_conv_bwd", grid=(nblk,),
        in_specs=[col(0), col(1), col(2), halo(1), halo(2), nxt(0), col(0), nxt(0),
                  pl.BlockSpec((3, d), lambda i: (0, 0))],
        out_specs=[pl.BlockSpec((_SC_T, d3), lambda i: (i, 0)), pl.BlockSpec((8, d), lambda i: (0, 0))],
        out_shape=[jax.ShapeDtypeStruct((s, d3), BF16), jax.ShapeDtypeStruct((8, d), F32)],
        scratch=[pltpu.VMEM((_SC_T + _SC_HALO, d), F32), pltpu.VMEM((_SC_T + _SC_HALO, d), F32)],
        rider=rider,
    )(u2, u2, u2, u2, u2, u2, dsc, dsc, w)


def _bucket_maps():
    a_idx = jnp.arange(_STEPS)[:, None]
    c_idx = jnp.arange(2 * _STEPS)[None, :]
    mdist = jnp.clip(a_idx + _STEPS - c_idx, 0, _STEPS)
    max_exact = _NUM_BUCKETS // 2
    maps = []
    for _, dil in _GROUPS:
        nn = mdist * dil
        nf = jnp.maximum(nn, 1).astype(F32)
        large = max_exact + (jnp.log(nf / max_exact) / math.log(_MAX_DISTANCE / max_exact)
                             * (_NUM_BUCKETS - max_exact)).astype(jnp.int32)
        maps.append(jnp.where(nn < max_exact, nn, jnp.minimum(large, _NUM_BUCKETS - 1)).astype(jnp.int32))
    return jnp.stack(maps, axis=0)


def _bias_expand(rel_bias, buckets):
    nh = rel_bias.shape[1]

    def body(rb_ref, bk_ref, o_ref):
        h = pl.program_id(0)
        bk = bk_ref[0]
        acc = jnp.zeros(bk.shape, F32)
        for b in range(_NUM_BUCKETS):
            acc = jnp.where(bk == b, rb_ref[b, h], acc)
        a = lax.broadcasted_iota(jnp.int32, bk.shape, 0)
        c = lax.broadcasted_iota(jnp.int32, bk.shape, 1)
        mdist = a + _STEPS - c
        o_ref[0] = jnp.where((mdist >= 0) & (mdist <= _STEPS), acc, _NEG)

    return _pcall(
        body, name="bias_expand", grid=(nh,),
        in_specs=[pl.BlockSpec(memory_space=pltpu.SMEM),
                  pl.BlockSpec((1, _STEPS, 2 * _STEPS), lambda h: (h // 8, 0, 0))],
        out_specs=pl.BlockSpec((1, _STEPS, 2 * _STEPS), lambda h: (h, 0, 0)),
        out_shape=jax.ShapeDtypeStruct((nh, _STEPS, 2 * _STEPS), F32))(rel_bias, buckets)


def _bias_reduce(ds_all, buckets):
    nh = ds_all.shape[0]

    def body(ds_ref, bk_ref, o_ref):
        t, bk = ds_ref[0], bk_ref[0]
        rows = lax.broadcasted_iota(jnp.int32, (_NUM_BUCKETS, _LANES), 0)
        out = jnp.zeros((_NUM_BUCKETS, _LANES), F32)
        for b in range(_NUM_BUCKETS):
            out = jnp.where(rows == b, jnp.sum(jnp.where(bk == b, t, 0.0)), out)
        o_ref[0] = out

    blk = pl.BlockSpec((1, _STEPS, 2 * _STEPS), lambda h: (h, 0, 0))
    return _pcall(
        body, name="bias_reduce", grid=(nh,),
        in_specs=[blk, pl.BlockSpec((1, _STEPS, 2 * _STEPS), lambda h: (h // 8, 0, 0))],
        out_specs=pl.BlockSpec((1, _NUM_BUCKETS, _LANES), lambda h: (h, 0, 0)),
        out_shape=jax.ShapeDtypeStruct((nh, _NUM_BUCKETS, _LANES), F32))(ds_all, buckets)


def _sub_residues(dil):
    return 4 if dil % 16 == 0 else 1


def _strided_rows(ref, tmp_ref, p, r, dil):
    sub = _sub_residues(dil)
    if dil == 1:
        return [ref[p]]
    if sub == 1:
        return [ref[p, pl.ds(r, _STEPS, stride=dil), :]]
    tmp_ref[...] = ref[p, pl.ds(r, _STEPS * sub, stride=dil // sub), :]
    return [tmp_ref[pl.ds(q, _STEPS, stride=sub), :] for q in range(sub)]


def _store_strided(ref, tmp_ref, p, r, dil, vals):
    sub = _sub_residues(dil)
    if dil == 1:
        ref[p] = vals[0]
    elif sub == 1:
        ref[p, pl.ds(r, _STEPS, stride=dil), :] = vals[0]
    else:
        for q, val in enumerate(vals):
            tmp_ref[pl.ds(q, _STEPS, stride=sub), :] = val
        ref[p, pl.ds(r, _STEPS * sub, stride=dil // sub), :] = tmp_ref[...]


def _tmp_rows(dil, count):
    sub = _sub_residues(dil)
    return [pltpu.VMEM((_STEPS * sub, _LANES), F32)] * count if sub > 1 else []


def _head_masks():
    lane = lax.broadcasted_iota(jnp.int32, (1, _LANES), 1)
    return [lane < _HEAD_DIM, lane >= _HEAD_DIM]


def _stack_heads(x2, masks):
    return jnp.concatenate([jnp.where(masks[0], x2, 0), jnp.where(masks[1], x2, 0)], axis=0)


def _unstack_heads(y, masks):
    return jnp.where(masks[0], y[:_STEPS], y[_STEPS:])


def _scores(qs2, k2, b_ref, j, first):
    sc = lax.dot_general(qs2, k2, (((1,), (1,)), ((), ())), preferred_element_type=F32)
    sc = sc * (_HEAD_DIM ** -0.5) + jnp.concatenate([b_ref[2 * j], b_ref[2 * j + 1]], axis=0)
    col = lax.broadcasted_iota(jnp.int32, sc.shape, 1)
    return jnp.where(jnp.logical_and(first, col < _STEPS), _NEG, sc)


_PAIRS = _GROUP_COLS // _LANES


def _attn_fwd(uq, uk, uv, bias, g, dil, pp, rider=None):
    s = uq.shape[1]
    rb = _STEPS * dil
    nb = s // rb
    npb = _PAIRS // pp

    sub = _sub_residues(dil)

    def body(q_ref, kc_ref, kp_ref, vc_ref, vp_ref, b_ref, o_ref, l_ref, *tmp):
        tmp = tmp + (None,) * 7
        n, r = pl.program_id(1), pl.program_id(2)
        first = n == 0
        masks = _head_masks()
        for j in range(pp):
            qs = _strided_rows(q_ref, tmp[0], j, r, dil)
            kps, kcs = _strided_rows(kp_ref, tmp[1], j, r, dil), _strided_rows(kc_ref, tmp[2], j, r, dil)
            vps, vcs = _strided_rows(vp_ref, tmp[3], j, r, dil), _strided_rows(vc_ref, tmp[4], j, r, dil)
            o_res, l_res = [], []
            for q in range(sub):
                q2 = qs[q].astype(BF16)
                k2 = jnp.concatenate([kps[q], kcs[q]], axis=0).astype(BF16)
                v2 = jnp.concatenate([vps[q], vcs[q]], axis=0).astype(BF16)
                sc = _scores(_stack_heads(q2, masks), k2, b_ref, j, first)
                mx = jnp.max(sc, axis=-1, keepdims=True)
                p = jnp.exp(sc - mx)
                den = jnp.sum(p, axis=-1, keepdims=True)
                o2 = jnp.dot(p.astype(BF16), v2, preferred_element_type=F32) / den
                o_res.append(_unstack_heads(o2, masks))
                l_res.append(_unstack_heads(jnp.broadcast_to(mx + jnp.log(den), o2.shape), masks))
            _store_strided(o_ref, tmp[5], j, r, dil, o_res)
            _store_strided(l_ref, tmp[6], j, r, dil, l_res)

    cur = pl.BlockSpec((pp, rb, _LANES), lambda hb, n, r: (g * npb + hb, n, 0))
    prev = pl.BlockSpec((pp, rb, _LANES), lambda hb, n, r: (g * npb + hb, jnp.maximum(n - 1, 0), 0))
    bspec = pl.BlockSpec((2 * pp, _STEPS, 2 * _STEPS), lambda hb, n, r: (g * npb + hb, 0, 0))
    ospec = pl.BlockSpec((pp, rb, _LANES), lambda hb, n, r: (hb, n, 0))
    sh = jax.ShapeDtypeStruct((_PAIRS, s, _LANES), F32)
    return _pcall(
        body, name=f"attn_fwd_g{g}", grid=(npb, nb, dil // sub),
        in_specs=[cur, cur, prev, cur, prev, bspec], out_specs=[ospec, ospec], out_shape=[sh, sh],
        scratch=_tmp_rows(dil, 7), rider=rider,
    )(uq, uk, uk, uv, uv, bias)


def _attn_merge(outs, lses, cat):
    s = outs[0].shape[1]
    c = _GROUP_COLS

    def body(o0, o1, o2, l0, l1, l2, cat_in, cat_ref, lse_ref):
        del cat_in
        a0, a1, a2 = l0[...], l1[...], l2[...]
        mx = jnp.maximum(jnp.maximum(a0, a1), a2)
        w0, w1, w2 = jnp.exp(a0 - mx), jnp.exp(a1 - mx), jnp.exp(a2 - mx)
        den = w0 + w1 + w2
        y = ((w0 * o0[...] + w1 * o1[...] + w2 * o2[...]) / den).astype(BF16)
        for p in range(_PAIRS):
            cat_ref[:, p * _LANES:(p + 1) * _LANES] = y[p]
        lse_ref[...] = mx + jnp.log(den)

    blk = pl.BlockSpec((_PAIRS, _ROW_T, _LANES), lambda i: (0, i, 0))
    return _pcall(
        body, name="attn_merge", grid=(s // _ROW_T,),
        in_specs=[blk] * 6 + [_ANY],
        out_specs=[pl.BlockSpec((_ROW_T, c), lambda i: (i, 1)), blk],
        out_shape=[jax.ShapeDtypeStruct(cat.shape, BF16), jax.ShapeDtypeStruct((_PAIRS, s, _LANES), F32)],
        aliases={6: 0})(*outs, *lses, cat)


def _attn_delta(dcat, cat):
    s = dcat.shape[0]
    c = _GROUP_COLS
    seg = (jnp.arange(c)[:, None] // _HEAD_DIM == jnp.arange(c)[None, :] // _HEAD_DIM).astype(BF16)

    def body(dy_ref, y_ref, seg_ref, dl_ref, dys_ref):
        dy = dy_ref[...]
        prod = dy * y_ref[...].astype(F32)
        hi = prod.astype(BF16)
        lo = (prod - hi.astype(F32)).astype(BF16)
        dl = (jnp.dot(hi, seg_ref[...], preferred_element_type=F32)
              + jnp.dot(lo, seg_ref[...], preferred_element_type=F32))
        for p in range(_PAIRS):
            dl_ref[p] = dl[:, p * _LANES:(p + 1) * _LANES]
            dys_ref[p] = dy[:, p * _LANES:(p + 1) * _LANES]

    right = pl.BlockSpec((_ROW_T, c), lambda i: (i, 1))
    blk = pl.BlockSpec((_PAIRS, _ROW_T, _LANES), lambda i: (0, i, 0))
    sh = jax.ShapeDtypeStruct((_PAIRS, s, _LANES), F32)
    return _pcall(
        body, name="attn_delta", grid=(s // _ROW_T,),
        in_specs=[right, right, pl.BlockSpec((c, c), lambda i: (0, 0))],
        out_specs=[blk, blk], out_shape=[sh, sh])(dcat, cat, seg)


def _attn_bwd(uq, uk, uv, dys, lse, delta, bias, prev_grads, g, dil, pp, rider=None):
    s = uq.shape[1]
    rb = _STEPS * dil
    nb = s // rb
    npb = _PAIRS // pp
    scale = _HEAD_DIM ** -0.5

    sub = _sub_residues(dil)

    def body(q_ref, kc_ref, kp_ref, vc_ref, vp_ref, dy_ref, l_ref, dl_ref, b_ref, *rest):
        rest = rest[len(prev_grads):]
        dqkv_ref, dsa_ref, dqc_ref, dkc_ref, dvc_ref = rest[:5]
        dq_ref, dk_ref, dv_ref = dqkv_ref.at[0], dqkv_ref.at[1], dqkv_ref.at[2]
        tmp = rest[5:] + (None,) * 11
        n, r = pl.program_id(1), pl.program_id(2)

        def carry_slot(j, q):
            return ((r + (dil // sub) * q) * pp + j) if sub > 1 else r * pp + j

        @pl.when(jnp.logical_and(n == 0, r == 0))
        def _():
            dsa_ref[...] = jnp.zeros_like(dsa_ref)

        @pl.when(n == 0)
        def _():
            for j in range(pp):
                for q in range(sub):
                    for carry in (dqc_ref, dkc_ref, dvc_ref):
                        carry[carry_slot(j, q)] = jnp.zeros((_STEPS, _LANES), F32)

        @pl.when(n < nb)
        def _():
            first = n == 0
            masks = _head_masks()
            for j in range(pp):
                qs = _strided_rows(q_ref, tmp[0], j, r, dil)
                kps, kcs = _strided_rows(kp_ref, tmp[1], j, r, dil), _strided_rows(kc_ref, tmp[2], j, r, dil)
                vps, vcs = _strided_rows(vp_ref, tmp[3], j, r, dil), _strided_rows(vc_ref, tmp[4], j, r, dil)
                dys_ = _strided_rows(dy_ref, tmp[5], j, r, dil)
                lses = _strided_rows(l_ref, tmp[6], j, r, dil)
                dls = _strided_rows(dl_ref, tmp[7], j, r, dil)
                ds_sum = [jnp.zeros((_STEPS, 2 * _STEPS), F32)] * 2
                dq_res, dk_res, dv_res = [], [], []
                for q in range(sub):
                    q2 = qs[q].astype(BF16)
                    k2 = jnp.concatenate([kps[q], kcs[q]], axis=0).astype(BF16)
                    v2 = jnp.concatenate([vps[q], vcs[q]], axis=0).astype(BF16)
                    dy2 = dys_[q].astype(BF16)
                    qs2, dys2 = _stack_heads(q2, masks), _stack_heads(dy2, masks)
                    per_row = lambda st: jnp.concatenate([st[:, 0:1], st[:, _HEAD_DIM:_HEAD_DIM + 1]], axis=0)
                    sc = _scores(qs2, k2, b_ref, j, first)
                    p = jnp.exp(sc - per_row(lses[q]))
                    dp = lax.dot_general(dys2, v2, (((1,), (1,)), ((), ())), preferred_element_type=F32)
                    ds = p * (dp - per_row(dls[q]))
                    ds_sum[0] = ds_sum[0] + ds[:_STEPS]
                    ds_sum[1] = ds_sum[1] + ds[_STEPS:]
                    dsb = ds.astype(BF16)
                    dq_p = _unstack_heads(jnp.dot(dsb, k2, preferred_element_type=F32), masks)
                    tdn = (((0,), (0,)), ((), ()))
                    dk_p = lax.dot_general(dsb, qs2, tdn, preferred_element_type=F32) * scale
                    dv_p = lax.dot_general(p.astype(BF16), dys2, tdn, preferred_element_type=F32)
                    slot = carry_slot(j, q)
                    dq_res.append(dqc_ref[slot])
                    dk_res.append(dkc_ref[slot] + dk_p[:_STEPS])
                    dv_res.append(dvc_ref[slot] + dv_p[:_STEPS])
                    dqc_ref[slot] = dq_p * scale
                    dkc_ref[slot] = dk_p[_STEPS:]
                    dvc_ref[slot] = dv_p[_STEPS:]
                for hh in range(2):
                    dsa_ref[2 * j + hh] += ds_sum[hh]
                _store_strided(dq_ref, tmp[8], j, r, dil, dq_res)
                _store_strided(dk_ref, tmp[9], j, r, dil, dk_res)
                _store_strided(dv_ref, tmp[10], j, r, dil, dv_res)

        @pl.when(n == nb)
        def _():
            for j in range(pp):
                for ref, carry, t in ((dq_ref, dqc_ref, 8), (dk_ref, dkc_ref, 9), (dv_ref, dvc_ref, 10)):
                    _store_strided(ref, tmp[t], j, r, dil, [carry[carry_slot(j, q)] for q in range(sub)])

    def clamp(n):
        return jnp.minimum(n, nb - 1)

    cur = pl.BlockSpec((pp, rb, _LANES), lambda hb, n, r: (g * npb + hb, clamp(n), 0))
    prev = pl.BlockSpec((pp, rb, _LANES), lambda hb, n, r: (g * npb + hb, jnp.maximum(clamp(n) - 1, 0), 0))
    stat = pl.BlockSpec((pp, rb, _LANES), lambda hb, n, r: (hb, clamp(n), 0))
    bspec = pl.BlockSpec((2 * pp, _STEPS, 2 * _STEPS), lambda hb, n, r: (g * npb + hb, 0, 0))
    late = pl.BlockSpec((3, pp, rb, _LANES), lambda hb, n, r: (0, g * npb + hb, jnp.maximum(n - 1, 0), 0))
    dsspec = pl.BlockSpec((2 * pp, _STEPS, 2 * _STEPS), lambda hb, n, r: (hb, 0, 0))
    np_ = len(prev_grads)
    carry = pltpu.VMEM((dil * pp, _STEPS, _LANES), F32)
    return _pcall(
        body, name=f"attn_bwd_g{g}", grid=(npb, nb + 1, dil // sub),
        in_specs=[cur, cur, prev, cur, prev, stat, stat, stat, bspec] + [_ANY] * np_,
        out_specs=[late, dsspec],
        out_shape=[jax.ShapeDtypeStruct((3, 3 * _PAIRS, s, _LANES), F32),
                   jax.ShapeDtypeStruct((8, _STEPS, 2 * _STEPS), F32)],
        scratch=[carry, carry, carry] + _tmp_rows(dil, 11),
        aliases={9 + t: t for t in range(np_)}, rider=rider,
    )(uq, uk, uk, uv, uv, dys, lse, delta, bias, *prev_grads)


def _place():
    x, y, c = lax.axis_index("x"), lax.axis_index("y"), lax.axis_index("c")
    chips = [(1 - x, y), (x, 1 - y), (1 - x, 1 - y)]
    return x, y, c, chips


def _slab(ref, axis, chip, width):
    start = pl.multiple_of(chip * width, width)
    if axis == 0:
        return ref.at[pl.ds(start, width), :]
    return ref.at[:, pl.ds(start, width)]


def _run_rider(rider, name):
    nin, nout = len(rider.ins), len(rider.out_shapes)

    def body(*refs):
        ins, outs, scr = refs[:nin], refs[nin:nin + nout], refs[nin + nout:]
        rider.start(ins, outs, scr)
        rider.finish(ins, outs, scr)

    return _pcall(body, name=name, in_specs=[_ANY] * nin, out_specs=[_ANY] * nout, out_shape=rider.out_shapes,
                  scratch=rider.scratch)(*rider.ins)


def _gather_halves_rider(shard, axis):
    shape = list(shard.shape)
    shape[axis] *= 4
    full = jax.ShapeDtypeStruct(tuple(shape), shard.dtype)
    half = shard.shape[0] // 2
    width = shard.shape[axis]

    def region(out, chip, core):
        if axis == 0:
            return out.at[pl.ds(pl.multiple_of(chip * width + core * half, half), half), :]
        return out.at[pl.ds(pl.multiple_of(core * half, half), half), pl.ds(pl.multiple_of(chip * width, width), width)]

    def copies(ins, outs, scr):
        send, recv, loc = scr
        (src,), (out,) = ins, outs
        x, y, c, chips = _place()
        mine = 2 * x + y
        own = pltpu.make_async_copy(src, _slab(out, axis, mine, width), loc.at[0])
        my_half = src.at[pl.ds(pl.multiple_of(c * half, half), half), :]
        over_ici, ici_in, to_sib, sib_in = [], [], [], []
        for j, (px, py) in enumerate(chips):
            theirs = 2 * px + py
            ici = dict(send_sem=send.at[j], recv_sem=recv.at[j], device_id=(px, py, c), device_id_type=MESH)
            d2d = dict(send_sem=send.at[3 + j], recv_sem=recv.at[3 + j], device_id=(x, y, 1 - c),
                       device_id_type=MESH)
            over_ici.append(pltpu.make_async_remote_copy(src_ref=my_half, dst_ref=region(out, mine, c), **ici))
            ici_in.append(pltpu.make_async_remote_copy(src_ref=my_half, dst_ref=region(out, theirs, c), **ici))
            to_sib.append(pltpu.make_async_remote_copy(
                src_ref=region(out, theirs, c), dst_ref=region(out, theirs, c), **d2d))
            sib_in.append(pltpu.make_async_remote_copy(
                src_ref=region(out, theirs, c), dst_ref=region(out, theirs, 1 - c), **d2d))
        return own, over_ici, ici_in, to_sib, sib_in

    def start(ins, outs, scr):
        own, over_ici, _, _, _ = copies(ins, outs, scr)
        own.start()
        for cp in over_ici:
            cp.start()

    def finish(ins, outs, scr):
        own, over_ici, ici_in, to_sib, sib_in = copies(ins, outs, scr)
        for j in range(3):
            ici_in[j].wait_recv()
            to_sib[j].start()
        for cp in sib_in:
            cp.wait_recv()
        own.wait()
        for cp in over_ici + to_sib:
            cp.wait_send()

    return _Rider([shard], [full], [pltpu.SemaphoreType.DMA((6,)), pltpu.SemaphoreType.DMA((6,)),
                                    pltpu.SemaphoreType.DMA((1,))], start, finish)


def _join_riders(riders):
    if len(riders) == 1:
        return riders[0]

    def parts(ins, outs, scr):
        pi = po = ps = 0
        for rd in riders:
            ni, no, ns = len(rd.ins), len(rd.out_shapes), len(rd.scratch)
            yield rd, ins[pi:pi + ni], outs[po:po + no], scr[ps:ps + ns]
            pi, po, ps = pi + ni, po + no, ps + ns

    def start(ins, outs, scr):
        for rd, i, o, sc in parts(ins, outs, scr):
            rd.start(i, o, sc)

    def finish(ins, outs, scr):
        for rd, i, o, sc in parts(ins, outs, scr):
            rd.finish(i, o, sc)

    return _Rider(sum((rd.ins for rd in riders), []), sum((rd.out_shapes for rd in riders), []),
                  sum((rd.scratch for rd in riders), []), start, finish)


def _scatter_rider(grads, axes):
    nw = len(grads)
    outs_shape = []
    for gr, ax in zip(grads, axes):
        shape = list(gr.shape)
        shape[ax] //= 4
        outs_shape.append(jax.ShapeDtypeStruct((3,) + tuple(shape), gr.dtype))

    def copies(ins, outs, scr):
        send, recv = scr
        x, y, c, chips = _place()
        cps = []
        for t in range(nw):
            width = ins[t].shape[axes[t]] // 4
            for j, (px, py) in enumerate(chips):
                cps.append(pltpu.make_async_remote_copy(
                    src_ref=_slab(ins[t], axes[t], 2 * px + py, width), dst_ref=outs[t].at[j],
                    send_sem=send.at[3 * t + j], recv_sem=recv.at[3 * t + j],
                    device_id=(px, py, c), device_id_type=MESH))
        return cps

    def start(ins, outs, scr):
        for cp in copies(ins, outs, scr):
            cp.start()

    def finish(ins, outs, scr):
        cps = copies(ins, outs, scr)
        for cp in cps:
            cp.wait_recv()
        for cp in cps:
            cp.wait_send()

    return _Rider(grads, outs_shape, [pltpu.SemaphoreType.DMA((3 * nw,)), pltpu.SemaphoreType.DMA((3 * nw,))],
                  start, finish)


def _swap_rider(parts):
    nw = len(parts)

    def copies(ins, outs, scr):
        send, recv = scr
        x, y, c, _ = _place()
        return [pltpu.make_async_remote_copy(
            src_ref=ins[t], dst_ref=outs[t], send_sem=send.at[t], recv_sem=recv.at[t],
            device_id=(x, y, 1 - c), device_id_type=MESH) for t in range(nw)]

    def start(ins, outs, scr):
        for cp in copies(ins, outs, scr):
            cp.start()

    def finish(ins, outs, scr):
        cps = copies(ins, outs, scr)
        for cp in cps:
            cp.wait_recv()
        for cp in cps:
            cp.wait_send()

    return _Rider(parts, [jax.ShapeDtypeStruct(p.shape, p.dtype) for p in parts],
                  [pltpu.SemaphoreType.DMA((nw,)), pltpu.SemaphoreType.DMA((nw,))], start, finish)


def _sum_all_devices(buf, name):
    rows, cols = buf.shape

    def body(in_ref, o_ref, gat_ref, send, recv):
        x, y, c, _ = _place()
        me = 4 * x + 2 * y + c
        gat_ref[me] = in_ref[...]
        started = []
        for mask in range(1, 8):
            fx, fy, fc = (mask >> 2) & 1, (mask >> 1) & 1, mask & 1
            peer = (x + fx * (1 - 2 * x), y + fy * (1 - 2 * y), c + fc * (1 - 2 * c))
            cp = pltpu.make_async_remote_copy(
                src_ref=in_ref, dst_ref=gat_ref.at[me], send_sem=send.at[mask - 1], recv_sem=recv.at[mask - 1],
                device_id=peer, device_id_type=MESH)
            cp.start()
            started.append(cp)
        for cp in started:
            cp.wait_recv()
        for cp in started:
            cp.wait_send()
        acc = gat_ref[0]
        for t in range(1, 8):
            acc = acc + gat_ref[t]
        o_ref[...] = acc

    vm = pl.BlockSpec(memory_space=pltpu.VMEM)
    return _pcall(
        body, name=name, in_specs=[vm], out_specs=vm, out_shape=jax.ShapeDtypeStruct((rows, cols), F32),
        scratch=[pltpu.VMEM((8, rows, cols), F32), pltpu.SemaphoreType.DMA((7,)), pltpu.SemaphoreType.DMA((7,))],
    )(buf)


_UPD_T = 256


def _sum_partials(own, got, name):
    rows, cols = own.shape
    tr = min(_UPD_T, rows)

    def body(own_ref, got_ref, o_ref):
        acc = own_ref[...].astype(F32)
        for j in range(3):
            acc = acc + got_ref[j].astype(F32)
        o_ref[...] = acc

    blk = pl.BlockSpec((tr, cols), lambda i: (i, 0))
    return _pcall(
        body, name=name, grid=(rows // tr,),
        in_specs=[blk, pl.BlockSpec((3, tr, cols), lambda i: (0, i, 0))], out_specs=blk,
        out_shape=jax.ShapeDtypeStruct((rows, cols), F32))(own, got)


def _adamw_math(w, gr, m, v):
    m = _B1 * m + (1.0 - _B1) * gr
    v = _B2 * v + (1.0 - _B2) * (gr * gr)
    m_hat = m / (1.0 - _B1 ** _STEP)
    v_hat = v / (1.0 - _B2 ** _STEP)
    delta = -_LR * (m_hat / (jnp.sqrt(v_hat) + _EPS) + _WD * w)
    return delta, m, v


def _adamw(w, m, v, parts, name):
    rows, cols = w.shape
    tr = min(_UPD_T, rows)
    npart = len(parts)

    def body(w_ref, m_ref, v_ref, *rest):
        p_refs, (g_ref, d_ref, nm_ref, nv_ref) = rest[:npart], rest[npart:]
        gr = p_refs[0][...]
        for p in p_refs[1:]:
            gr = gr + p[...]
        delta, nm, nv = _adamw_math(w_ref[...], gr, m_ref[...], v_ref[...])
        g_ref[...] = gr
        d_ref[...] = delta
        nm_ref[...] = nm
        nv_ref[...] = nv

    blk = pl.BlockSpec((tr, cols), lambda i: (i, 0))
    sh = jax.ShapeDtypeStruct((rows, cols), F32)
    return _pcall(body, name=name, grid=(rows // tr,), in_specs=[blk] * (3 + npart), out_specs=[blk] * 4,
                  out_shape=[sh] * 4)(w, m, v, *parts)


def _adamw_layers(w, m, v, parts, name):
    _, rows, cols = w.shape
    tr = min(_UPD_T, rows)
    npart = len(parts[0])

    def body(w_ref, m_ref, v_ref, *rest):
        p_refs, (g_ref, d_ref, nm_ref, nv_ref) = rest[:2 * npart], rest[2 * npart:]
        grs = []
        for layer in range(2):
            gr = p_refs[layer * npart][...]
            for p in p_refs[layer * npart + 1:(layer + 1) * npart]:
                gr = gr + p[...]
            grs.append(gr)
        gr = jnp.where(pl.program_id(0) == 0, grs[0], grs[1])
        delta, nm, nv = _adamw_math(w_ref[...], gr, m_ref[...], v_ref[...])
        g_ref[...] = gr
        d_ref[...] = delta
        nm_ref[...] = nm
        nv_ref[...] = nv

    blk = pl.BlockSpec((None, tr, cols), lambda l, i: (l, i, 0))

    def part_spec(layer):
        return pl.BlockSpec((tr, cols), lambda l, i: (jnp.where(l == layer, i, 0), 0))

    sh = jax.ShapeDtypeStruct(w.shape, F32)
    return _pcall(
        body, name=name, grid=(2, rows // tr),
        in_specs=[blk] * 3 + [part_spec(0)] * npart + [part_spec(1)] * npart, out_specs=[blk] * 4,
        out_shape=[sh] * 4)(w, m, v, *parts[0], *parts[1])


_PACK_W = 1024


def _pack(arrs, rows):
    flat = []
    for a in arrs:
        f = a.reshape(-1).astype(F32)
        pad = (-f.shape[0]) % _PACK_W
        flat.append(jnp.pad(f, (0, pad)))
    f = jnp.concatenate(flat)
    f = jnp.pad(f, (0, rows * _PACK_W - f.shape[0]))
    return f.reshape(rows, _PACK_W)


def _unpack(buf, shapes):
    flat = buf.reshape(-1)
    out, pos = [], 0
    for sh in shapes:
        size = math.prod(sh)
        out.append(flat[pos:pos + size].reshape(sh))
        pos += size + ((-size) % _PACK_W)
    return out


def _pack_rows(shapes):
    total = sum(-(-math.prod(sh) // _PACK_W) for sh in shapes)
    return -(-total // 8) * 8


def kernel(x, rel_bias, ab_norm, ab_w_in, ab_conv_w, ab_conv_b, ab_ln_g, ab_ln_b, ab_w_out, sc_norm, sc_w_in, sc_conv_w, sc_w_out, mlp_norm, mlp_w_up, mlp_w_down, final_norm, loss_target, m_rel_bias, m_ab_norm, m_ab_w_in, m_ab_conv_w, m_ab_conv_b, m_ab_ln_g, m_ab_ln_b, m_ab_w_out, m_sc_norm, m_sc_w_in, m_sc_conv_w, m_sc_w_out, m_mlp_norm, m_mlp_w_up, m_mlp_w_down, m_final_norm, v_rel_bias, v_ab_norm, v_ab_w_in, v_ab_conv_w, v_ab_conv_b, v_ab_ln_g, v_ab_ln_b, v_ab_w_out, v_sc_norm, v_sc_w_in, v_sc_conv_w, v_sc_w_out, v_mlp_norm, v_mlp_w_up, v_mlp_w_down, v_final_norm):
    s, d = x.shape[1], x.shape[2]
    dff = 4 * d
    c = _GROUP_COLS
    chip = 2 * lax.axis_index("x") + lax.axis_index("y")
    on_c0 = (lax.axis_index("c") == 0).astype(F32)
    h0 = x[0]
    tgt = loss_target[0]

    cw_sh, scn_sh, scw_sh = ab_conv_w[0], sc_norm, sc_conv_w[0]
    conv_w_full = lax.dynamic_update_slice(jnp.zeros((_CONV_K, c), F32), cw_sh * on_c0, (0, chip * cw_sh.shape[1]))
    scn_full = lax.dynamic_update_slice(jnp.zeros((1, d), F32), scn_sh * on_c0, (0, chip * scn_sh.shape[1]))
    scw_full = lax.dynamic_update_slice(jnp.zeros((3, d), F32), scw_sh * on_c0, (0, chip * scw_sh.shape[1]))
    small_shapes = [(_CONV_K, c), (1, d), (3, d)]
    small = _sum_all_devices(_pack([conv_w_full, scn_full, scw_full], _pack_rows(small_shapes)), "gather_small")
    conv_w, sc_g, sc_cw = _unpack(small, small_shapes)

    w_shards = [ab_w_in[0], ab_w_out[0], sc_w_in[0], sc_w_out[0], mlp_w_up[0], mlp_w_up[1],
                mlp_w_down[0], mlp_w_down[1]]
    w_axes = [1, 0, 1, 0, 1, 1, 0, 0]
    wb = [w.astype(BF16) for w in w_shards]
    full_w = [None] * 8

    def gather(idx):
        return _join_riders([_gather_halves_rider(wb[t], w_axes[t]) for t in idx])

    def put(idx, got_w):
        for t, w in zip(idx, got_w):
            full_w[t] = w

    buckets = _bucket_maps()
    bias = _bias_expand(rel_bias, buckets)
    n0, got_w = _rms_fwd(h0, ab_norm, "rms_fwd_ab", rider=_gather_halves_rider(wb[0], w_axes[0]))
    put([0], got_w)
    w_in = full_w[0]
    tm = min(1024, s)
    tm2 = min(2048, s)
    tmh = min(512, s)
    uc = _mm(n0, w_in, "nn", m=s, n=2 * c, k=d, tm=tm2, tn=2 * c, tk=d, out_dtype=BF16, name="proj_conv")
    uq, uk, uv = [], [], []
    for t, (nm, dst) in enumerate(zip("qkv", (uq, uk, uv))):
        res = _mm(n0, w_in, "nn", m=s, n=3 * c, k=d, tm=tm2, tn=c, tk=d, out_dtype=F32, name=f"proj_{nm}",
                  b_off=(0, 2 + 3 * t), split="o", rider=gather([1]) if t == 0 else None)
        if t == 0:
            res, got_w = res
            put([1], got_w)
        dst.append(res)
    uq, uk, uv = uq[0], uk[0], uv[0]
    (cat, ca), got_w = _conv_a_fwd(uc, conv_w, ab_conv_b, ab_ln_g, ab_ln_b, rider=gather([2]))
    put([2], got_w)
    outs, lses = [], []
    for g, (_, dil) in enumerate(_GROUPS):
        idx = ([4], [6], [3, 5])[g]
        (o, l), got_w = _attn_fwd(uq, uk, uv, bias, g, dil, 4 if dil <= 4 else 2, rider=gather(idx))
        put(idx, got_w)
        outs.append(o)
        lses.append(l)
    cat, lse = _attn_merge(outs, lses, cat)
    h1, n1 = _mm(cat, full_w[1], "nn", m=s, n=d, k=d, tm=tm, tn=d, tk=d, out_dtype=(F32, BF16), name="out_ab",
                 epi=_epi_add_rms, extras=(h0,), vecs=(mlp_norm[0:1],))

    def mlp_fwd(h, nrm, layer, next_gain=None, rider=None):
        zr = _mm(nrm, full_w[4 + layer], "nn", m=s, n=dff, k=d, tm=tmh, tn=dff, tk=d, out_dtype=BF16,
                 name=f"mlp_up{layer}", epi=_epi_relu, rider=rider)
        if rider is not None:
            zr, got_r = zr
            put([7], got_r)
        kw = dict(m=s, n=d, k=dff, tm=tmh, tn=d, tk=dff, name=f"mlp_down{layer}", a_pro=_square, extras=(h,))
        if next_gain is None:
            return zr, _mm(zr, full_w[6 + layer], "nn", out_dtype=F32, epi=_epi_add, **kw), None
        hn, nn = _mm(zr, full_w[6 + layer], "nn", out_dtype=(F32, BF16), epi=_epi_add_rms, vecs=(next_gain,), **kw)
        return zr, hn, nn

    zr0, h2, n2 = mlp_fwd(h1, n1, 0, next_gain=sc_g, rider=gather([7]))
    _, w_out, w_si, w_so, w_up0, w_up1, w_dn0, w_dn1 = full_w
    w_up, w_dn = [w_up0, w_up1], [w_dn0, w_dn1]
    u2 = _mm(n2, w_si, "nn", m=s, n=3 * d, k=d, tm=tmh, tn=3 * d, tk=d, out_dtype=BF16, name="proj_sc")
    scv = _short_conv_fwd(u2, sc_cw)
    h3, n3 = _mm(scv, w_so, "nn", m=s, n=d, k=d, tm=tm, tn=d, tk=d, out_dtype=(F32, BF16), name="out_sc",
                 epi=_epi_add_rms, extras=(h2,), vecs=(mlp_norm[1:2],))
    zr1, h4, _ = mlp_fwd(h3, n3, 1)

    dh4, dh4b, g_final, loss_part = _loss_head(h4, tgt, final_norm.reshape(1, d))
    tkw = min(2048, s)

    big_grads, got, sums = [None] * 8, [None] * 8, [None] * 8

    def scatter(t):
        return _scatter_rider([big_grads[t]], [w_axes[t]])

    def own_slab(t):
        width = big_grads[t].shape[w_axes[t]] // 4
        return lax.dynamic_slice_in_dim(big_grads[t], chip * width, width, axis=w_axes[t])

    def arrived(t, got_t):
        got[t] = got_t[0]
        sums[t] = _sum_partials(own_slab(t), got[t], f"sum_partials{t}")

    def mlp_bwd(dh, dhb, h, nrm, zr, layer):
        dz = _mm(dhb, w_dn[layer], "nt", m=s, n=dff, k=d, tm=tmh, tn=dff, tk=d, out_dtype=BF16,
                 name=f"mlp_down{layer}_dx", epi=_epi_relu_sq_bwd, extras=(zr,))
        big_grads[6 + layer] = _mm(zr, dhb, "tn", m=dff, n=d, k=s, tm=1024, tn=d, tk=tkw, out_dtype=BF16,
                                   name=f"mlp_down{layer}_dw", a_pro=_square)
        big_grads[4 + layer] = _mm(nrm, dz, "tn", m=d, n=dff, k=s, tm=d, tn=1024, tk=tkw, out_dtype=BF16,
                                   name=f"mlp_up{layer}_dw")
        res = _mm(dz, w_up[layer], "nt", m=s, n=d, k=dff, tm=tmh, tn=d, tk=dff, out_dtype=(F32, BF16),
                  name=f"mlp_up{layer}_dx", rider=scatter(6) if layer == 0 else None, epi=_epi_rms_bwd,
                  extras=(h, dh), vecs=(mlp_norm[layer:layer + 1],), row_sum=True)
        if layer == 0:
            res, got_t = res
            arrived(6, got_t)
        return res

    dh3, dh3b, g_mn1 = mlp_bwd(dh4, dh4b, h3, n3, zr1, 1)

    dsc = _mm(dh3b, w_so, "nt", m=s, n=d, k=d, tm=tm, tn=d, tk=d, out_dtype=F32, name="out_sc_dx")
    big_grads[3] = _mm(scv, dh3b, "tn", m=d, n=d, k=s, tm=d, tn=d, tk=tkw, out_dtype=BF16, name="out_sc_dw")
    du2, g_sccw8 = _short_conv_bwd(u2, dsc, sc_cw)
    big_grads[2], got_t = _mm(n2, du2, "tn", m=d, n=3 * d, k=s, tm=d, tn=1024, tk=tkw, out_dtype=BF16,
                              name="proj_sc_dw", rider=scatter(3))
    arrived(3, got_t)
    dh2, dh2b, g_scn = _mm(
        du2, w_si, "nt", m=s, n=d, k=3 * d, tm=tmh, tn=d, tk=3 * d, out_dtype=(F32, BF16), name="proj_sc_dx",
        epi=_epi_rms_bwd, extras=(h2, dh3), vecs=(sc_g,), row_sum=True)

    dh1, dh1b, g_mn0 = mlp_bwd(dh2, dh2b, h1, n1, zr0, 0)

    dcat = _mm(dh1b, w_out, "nt", m=s, n=d, k=d, tm=tm, tn=d, tk=d, out_dtype=F32, name="out_ab_dx")
    big_grads[1] = _mm(cat, dh1b, "tn", m=d, n=d, k=s, tm=d, tn=d, tk=tkw, out_dtype=BF16, name="out_ab_dw")
    (dca, conv_stats), got_t = _conv_a_bwd_ln(ca, dcat, ab_ln_g, ab_ln_b, rider=scatter(1))
    arrived(1, got_t)
    (duc, g_cw32), got_t = _conv_a_bwd_conv(uc, dca, conv_w, rider=scatter(4))
    arrived(4, got_t)
    delta, dys = _attn_delta(dcat, cat)

    dqkv, ds_list = [], []
    for g, (_, dil) in enumerate(_GROUPS):
        late = (2, 5, 7)[g]
        (grads, dsa), got_t = _attn_bwd(uq, uk, uv, dys, lse, delta, bias, dqkv, g, dil, 4 if dil <= 4 else 1,
                                        rider=scatter(late))
        arrived(late, got_t)
        dqkv = [grads]
        ds_list.append(dsa)
    g_bias = _bias_reduce(jnp.concatenate(ds_list, axis=0), buckets)[:, :, 0].T
    dqkv = dqkv[0].reshape(9 * _PAIRS, s, _LANES)

    g_in_conv = _mm(n0, duc, "tn", m=d, n=2 * c, k=s, tm=d, tn=2 * c, tk=tkw, out_dtype=BF16,
                    name="proj_ab_dw_conv")
    g_in_qkv, sib_late = _mm(n0, dqkv, "tn", m=d, n=9 * c, k=s, tm=d, tn=3 * c, tk=min(1024, s), out_dtype=BF16,
                             name="proj_ab_dw_qkv", rider=_swap_rider(sums[1:]), split="b")
    big_grads[0] = jnp.concatenate([g_in_conv, g_in_qkv], axis=1)
    dn0 = _mm(duc, w_in, "nt", m=s, n=d, k=2 * c, tm=tm, tn=d, tk=2 * c, out_dtype=F32, name="proj_ab_dx_conv")
    (grad_x, g_abn), got_t = _mm(
        dqkv, w_in[:, 2 * c:], "nt", m=s, n=d, k=9 * c, tm=tmh, tn=d, tk=3 * c, out_dtype=F32,
        name="proj_ab_dx_qkv", epi=_epi_add_rms_bwd, extras=(dn0, h0, dh1), vecs=(ab_norm,), row_sum=True,
        rider=scatter(0), split="a")
    arrived(0, got_t)
    sib = list(_run_rider(_swap_rider([sums[0]]), "swap_sibling_w_in")) + sib_late

    upd = [_adamw(w_shards[t], mm[0], vv[0], [sums[t], sib[t]], f"adamw{t}")
           for t, (mm, vv) in enumerate(((m_ab_w_in, v_ab_w_in), (m_ab_w_out, v_ab_w_out),
                                         (m_sc_w_in, v_sc_w_in), (m_sc_w_out, v_sc_w_out)))]
    upd_up = _adamw_layers(mlp_w_up, m_mlp_w_up, v_mlp_w_up, [[sums[4], sib[4]], [sums[5], sib[5]]], "adamw_up")
    upd_dn = _adamw_layers(mlp_w_down, m_mlp_w_down, v_mlp_w_down, [[sums[6], sib[6]], [sums[7], sib[7]]],
                           "adamw_down")

    full_shapes = [(_NUM_BUCKETS, rel_bias.shape[1]), (1, d), (_CONV_K, c), (1, c), (1, c), (1, c), (1, d),
                   (3, d), (2, d), (d,), (1, 1)]
    small_grads = [g_bias, g_abn, g_cw32[:_CONV_K], conv_stats[0:1], conv_stats[1:2], conv_stats[2:3], g_scn,
                   g_sccw8[:3], jnp.concatenate([g_mn0, g_mn1], axis=0), g_final.reshape(d), loss_part[0:1, 0:1]]
    tot = _unpack(_sum_all_devices(_pack(small_grads, _pack_rows(full_shapes)), "sum_small"), full_shapes)
    loss = tot.pop()[0, 0]
    for idx, sh in ((2, cw_sh), (6, scn_sh), (7, scw_sh)):
        width = sh.shape[1]
        tot[idx] = lax.dynamic_slice_in_dim(tot[idx], chip * width, width, axis=1)
    sm_w = [rel_bias, ab_norm, cw_sh, ab_conv_b, ab_ln_g, ab_ln_b, scn_sh, scw_sh, mlp_norm, final_norm]
    sm_m = [m_rel_bias, m_ab_norm, m_ab_conv_w[0], m_ab_conv_b, m_ab_ln_g, m_ab_ln_b, m_sc_norm, m_sc_conv_w[0],
            m_mlp_norm, m_final_norm]
    sm_v = [v_rel_bias, v_ab_norm, v_ab_conv_w[0], v_ab_conv_b, v_ab_ln_g, v_ab_ln_b, v_sc_norm, v_sc_conv_w[0],
            v_mlp_norm, v_final_norm]
    sh_shapes = [tuple(t.shape) for t in tot]
    rows = _pack_rows(sh_shapes)
    sm_upd = _adamw(_pack(sm_w, rows), _pack(sm_m, rows), _pack(sm_v, rows), [_pack(tot, rows)], "adamw_small")
    sm_g, sm_d, sm_nm, sm_nv = [_unpack(buf, sh_shapes) for buf in sm_upd]

    def assemble(kind, sm):
        big = [u[kind] for u in upd]
        return [sm[0], sm[1], big[0][None], sm[2][None], sm[3], sm[4], sm[5], big[1][None], sm[6], big[2][None],
                sm[7][None], big[3][None], sm[8], upd_up[kind], upd_dn[kind], sm[9]]

    res = [loss, grad_x[None]]
    for kind, sm in enumerate((sm_g, sm_d, sm_nm, sm_nv)):
        res += assemble(kind, sm)
    return tuple(res)
```

```python
import functools
import math

import jax
import jax.numpy as jnp
from jax import lax
from jax.experimental import pallas as pl
from jax.experimental.pallas import tpu as pltpu

F32 = jnp.float32
BF16 = jnp.bfloat16
MESH = pl.DeviceIdType.MESH

_GROUPS = ((128, 1), (512, 4), (2048, 16))
_STEPS = 128
_HEAD_DIM = 64
_GROUP_COLS = 512
_NUM_BUCKETS = 32
_MAX_DISTANCE = 2048
_CONV_K = 31
_HALO = 32
_SC_HALO = 16
_RMS_EPS = 1e-6
_LN_EPS = 1e-5
_NEG = -1e30
_LANES = 128
_VMEM_LIMIT = 56 * 1024 * 1024

_LR, _B1, _B2, _EPS, _WD, _STEP = 0.001, 0.9, 0.999, 1e-08, 0.01, 10


class _Rider:
    def __init__(self, ins, out_shapes, scratch, start, finish):
        self.ins, self.out_shapes, self.scratch = list(ins), list(out_shapes), list(scratch)
        self.start, self.finish = start, finish


def _pcall(body, *, name, out_shape, in_specs, out_specs, grid=None, scratch=(), aliases=None, rider=None):
    kw = {} if grid is None else {"grid": grid}
    cparams = pltpu.CompilerParams(vmem_limit_bytes=_VMEM_LIMIT)
    if rider is None:
        return pl.pallas_call(
            body, name=name, out_shape=out_shape, in_specs=in_specs, out_specs=out_specs,
            scratch_shapes=list(scratch), input_output_aliases=aliases or {},
            compiler_params=cparams, **kw)
    single = not isinstance(out_specs, (list, tuple))
    ospecs = [out_specs] if single else list(out_specs)
    oshapes = [out_shape] if single else list(out_shape)
    nin, nout, nscr = len(in_specs), len(ospecs), len(scratch)
    rin, rout = len(rider.ins), len(rider.out_shapes)

    def wrapped(*refs):
        h_in, r_in = refs[:nin], refs[nin:nin + rin]
        p = nin + rin
        h_out, r_out = refs[p:p + nout], refs[p + nout:p + nout + rout]
        p += nout + rout
        h_scr, r_scr = refs[p:p + nscr], refs[p + nscr:]
        ids = [pl.program_id(a) for a in range(len(grid))]
        first = functools.reduce(jnp.logical_and, [i == 0 for i in ids])
        last = functools.reduce(jnp.logical_and, [i == g - 1 for i, g in zip(ids, grid)])

        @pl.when(first)
        def _():
            rider.start(r_in, r_out, r_scr)

        body(*h_in, *h_out, *h_scr)

        @pl.when(last)
        def _():
            rider.finish(r_in, r_out, r_scr)

    call = pl.pallas_call(
        wrapped, name=name, out_shape=oshapes + rider.out_shapes,
        in_specs=list(in_specs) + [_ANY] * rin, out_specs=ospecs + [_ANY] * rout,
        scratch_shapes=list(scratch) + rider.scratch, input_output_aliases=aliases or {},
        compiler_params=cparams, **kw)

    def run(*operands):
        res = call(*operands, *rider.ins)
        host = res[0] if single else list(res[:nout])
        return host, list(res[nout:])

    return run


def _sig(x):
    return 1.0 / (1.0 + jnp.exp(-x))


_ANY = pl.BlockSpec(memory_space=pl.ANY)


def _lanes_of(ref):
    parts = [ref[p] for p in range(ref.shape[0])]
    return parts[0] if len(parts) == 1 else jnp.concatenate(parts, axis=1)


def _mm(a, b, mode, *, m, n, k, tm, tn, tk, out_dtype, name, epi=None, extras=(), b_off=(0, 0), rider=None,
        split="", vecs=(), a_pro=None, row_sum=False):
    nk = k // tk
    assert m % tm == 0 and n % tn == 0 and k % tk == 0
    o0, o1 = b_off
    if mode == "nn":
        a_spec = pl.BlockSpec((tm, tk), lambda i, j, kk: (i, kk))
        b_spec = pl.BlockSpec((tk, tn), lambda i, j, kk: (kk + o0, j + o1))
        dn = (((1,), (0,)), ((), ()))
    elif mode == "nt":
        a_spec = pl.BlockSpec((tm, tk), lambda i, j, kk: (i, kk))
        if "a" in split:
            a_spec = pl.BlockSpec((tk // _LANES, tm, _LANES), lambda i, j, kk: (kk, i, 0))
        b_spec = pl.BlockSpec((tn, tk), lambda i, j, kk: (j + o0, kk + o1))
        dn = (((1,), (1,)), ((), ()))
    else:
        a_spec = pl.BlockSpec((tk, tm), lambda i, j, kk: (kk, i))
        b_spec = pl.BlockSpec((tk, tn), lambda i, j, kk: (kk + o0, j + o1))
        if "b" in split:
            b_spec = pl.BlockSpec((tn // _LANES, tk, _LANES), lambda i, j, kk: (j, kk, 0))
        dn = (((0,), (0,)), ((), ()))
    o_spec = pl.BlockSpec((tm, tn), lambda i, j, kk: (i, j))
    e_spec = o_spec
    if "o" in split:
        o_spec = pl.BlockSpec((tn // _LANES, tm, _LANES), lambda i, j, kk: (j, i, 0))
    v_spec = pl.BlockSpec((1, tn), lambda i, j, kk: (0, j))
    ne = len(extras) + len(vecs)
    multi = isinstance(out_dtype, tuple)
    dts = out_dtype if multi else (out_dtype,)
    no = len(dts)
    nr = 1 if row_sum else 0
    assert not row_sum or tn == n

    def body(a_ref, b_ref, *rest):
        ex, o_refs = rest[:ne], rest[ne:ne + no]
        av = _lanes_of(a_ref) if "a" in split else a_ref[...]
        bv = _lanes_of(b_ref) if "b" in split else b_ref[...]
        if av.dtype != BF16:
            av = av.astype(BF16)
        if bv.dtype != BF16:
            bv = bv.astype(BF16)
        if a_pro is not None:
            av = a_pro(av)
        p = lax.dot_general(av, bv, dn, preferred_element_type=F32)

        def fin(x):
            if epi is not None:
                x = epi(x, *[e[...] for e in ex])
            if row_sum:
                row, x = x[-1], (x[:-1] if multi else x[0])
                row_ref = rest[ne + no]

                @pl.when(pl.program_id(0) == 0)
                def _():
                    row_ref[...] = row

                @pl.when(pl.program_id(0) > 0)
                def _():
                    row_ref[...] += row

            for o_ref, val, dt in zip(o_refs, x if multi else (x,), dts):
                if "o" in split:
                    for p in range(tn // _LANES):
                        o_ref[p] = val[:, p * _LANES:(p + 1) * _LANES].astype(dt)
                else:
                    o_ref[...] = val.astype(dt)

        if nk == 1:
            fin(p)
        else:
            acc = rest[ne + no + nr]
            kk = pl.program_id(2)

            @pl.when(kk == 0)
            def _():
                acc[...] = p

            @pl.when(kk > 0)
            def _():
                acc[...] += p

            @pl.when(kk == nk - 1)
            def _():
                fin(acc[...])

    oshape = (n // _LANES, m, _LANES) if "o" in split else (m, n)
    shapes = [jax.ShapeDtypeStruct(oshape, dt) for dt in dts]
    ospecs = [o_spec] * no
    if row_sum:
        shapes.append(jax.ShapeDtypeStruct((1, n), F32))
        ospecs.append(v_spec)
    lone = not multi and not row_sum
    return _pcall(
        body, name=name, grid=(m // tm, n // tn, nk),
        in_specs=[a_spec, b_spec] + [e_spec] * len(extras) + [v_spec] * len(vecs),
        out_specs=ospecs[0] if lone else ospecs, out_shape=shapes[0] if lone else shapes,
        scratch=[pltpu.VMEM((tm, tn), F32)] if nk > 1 else [], rider=rider,
    )(a, b, *extras, *vecs)


def _epi_add(x, r):
    return x + r


def _epi_relu(x):
    return jnp.maximum(x, 0.0)


def _square(x):
    return x * x


def _epi_relu_sq_bwd(da, zr):
    return da * (2.0 * zr.astype(F32))


def _epi_add_rms(x, r, g):
    h = x + r
    return h, h * lax.rsqrt(jnp.mean(h * h, axis=-1, keepdims=True) + _RMS_EPS) * g


def _epi_rms_bwd(dn, h, dh_in, g):
    dx, dg = _rms_bwd_math(dn, h, g)
    dh = dh_in + dx
    return dh, dh, dg


_ROW_T = 512


def _rms_fwd(h, g, name, rider=None):
    s, d = h.shape

    def body(h_ref, g_ref, o_ref):
        x = h_ref[...]
        r = lax.rsqrt(jnp.mean(x * x, axis=-1, keepdims=True) + _RMS_EPS)
        o_ref[...] = (x * r * g_ref[...]).astype(BF16)

    row = pl.BlockSpec((_ROW_T, d), lambda i: (i, 0))
    vec = pl.BlockSpec((1, d), lambda i: (0, 0))
    return _pcall(body, name=name, grid=(s // _ROW_T,), in_specs=[row, vec], out_specs=row,
                  out_shape=jax.ShapeDtypeStruct((s, d), BF16), rider=rider)(h, g)


def _rms_bwd_math(dn, x, g):
    r = lax.rsqrt(jnp.mean(x * x, axis=-1, keepdims=True) + _RMS_EPS)
    xhat = x * r
    dg = jnp.sum(dn * xhat, axis=0, keepdims=True)
    t = dn * g
    dx = r * (t - xhat * jnp.mean(t * xhat, axis=-1, keepdims=True))
    return dx, dg


def _rms_bwd(dn, h, g, dh_in, name, rider=None):
    s, d = h.shape

    def body(dn_ref, h_ref, g_ref, dhi_ref, dh_ref, dg_ref):
        dx, dg = _rms_bwd_math(dn_ref[...], h_ref[...], g_ref[...])
        dh_ref[...] = dhi_ref[...] + dx

        @pl.when(pl.program_id(0) == 0)
        def _():
            dg_ref[...] = jnp.zeros_like(dg_ref)

        dg_ref[...] += dg

    row = pl.BlockSpec((_ROW_T, d), lambda i: (i, 0))
    vec = pl.BlockSpec((1, d), lambda i: (0, 0))
    return _pcall(
        body, name=name, grid=(s // _ROW_T,), in_specs=[row, row, vec, row], out_specs=[row, vec],
        out_shape=[jax.ShapeDtypeStruct((s, d), F32), jax.ShapeDtypeStruct((1, d), F32)],
        rider=rider)(dn, h, g, dh_in)


def _loss_head(h, tgt, g):
    s, d = h.shape

    def body(h_ref, t_ref, g_ref, dh_ref, dhb_ref, dg_ref, loss_ref):
        x, gv = h_ref[...], g_ref[...]
        r = lax.rsqrt(jnp.mean(x * x, axis=-1, keepdims=True) + _RMS_EPS)
        err = x * r * gv - t_ref[...]
        part = 0.5 * jnp.sum(jnp.mean(err * err, axis=-1, keepdims=True))
        dx, dg = _rms_bwd_math(err * (1.0 / d), x, gv)
        dh_ref[...] = dx
        dhb_ref[...] = dx.astype(BF16)

        @pl.when(pl.program_id(0) == 0)
        def _():
            dg_ref[...] = jnp.zeros_like(dg_ref)
            loss_ref[...] = jnp.zeros_like(loss_ref)

        dg_ref[...] += dg
        loss_ref[...] += jnp.full(loss_ref.shape, part, F32)

    row = pl.BlockSpec((_ROW_T, d), lambda i: (i, 0))
    vec = pl.BlockSpec((1, d), lambda i: (0, 0))
    one = pl.BlockSpec((1, _LANES), lambda i: (0, 0))
    return _pcall(
        body, name="loss_head", grid=(s // _ROW_T,), in_specs=[row, row, vec], out_specs=[row, row, vec, one],
        out_shape=[jax.ShapeDtypeStruct((s, d), F32), jax.ShapeDtypeStruct((s, d), BF16),
                   jax.ShapeDtypeStruct((1, d), F32), jax.ShapeDtypeStruct((1, _LANES), F32)])(h, tgt, g)


_CONV_T = 256
_CONV_RC = 64


def _conv_a_specs(s):
    c = _GROUP_COLS
    hb = _CONV_T // _HALO
    val = pl.BlockSpec((_CONV_T, c), lambda i: (i, 0))
    gate = pl.BlockSpec((_CONV_T, c), lambda i: (i, 1))
    hval = pl.BlockSpec((_HALO, c), lambda i: (jnp.maximum(i * hb - 1, 0), 0))
    hgate = pl.BlockSpec((_HALO, c), lambda i: (jnp.maximum(i * hb - 1, 0), 1))
    return val, gate, hval, hgate


def _fill_glu(val_ref, gate_ref, hval_ref, hgate_ref, hs_ref):
    i = pl.program_id(0)
    hs_ref[pl.ds(_HALO, _CONV_T), :] = val_ref[...].astype(F32) * _sig(gate_ref[...].astype(F32))
    halo = hval_ref[...].astype(F32) * _sig(hgate_ref[...].astype(F32))
    hs_ref[pl.ds(0, _HALO), :] = jnp.where(i > 0, halo, 0.0)


_SHIFT_ROWS = _CONV_T + _HALO - 8


def _fill_shifts(src_ref, sh_ref):
    for b in range(1, 8):
        sh_ref[b - 1] = src_ref[pl.ds(b, _SHIFT_ROWS), :]


def _tap_rows(src_ref, sh_ref, start, rows, lanes=slice(None)):
    b = start % 8
    if b == 0:
        return src_ref[pl.ds(start, rows), lanes]
    return sh_ref[b - 1, pl.ds(start - b, rows), lanes]


def _conv_rows(hs_ref, sh_ref, w_ref, r0, rows):
    off = _HALO - (_CONV_K - 1)
    acc = jnp.zeros((rows, _GROUP_COLS), F32)
    for kk in range(_CONV_K):
        acc = acc + w_ref[kk:kk + 1, :] * _tap_rows(hs_ref, sh_ref, r0 + off + kk, rows)
    return acc


def _ln_fwd(ca, g, b):
    mu = jnp.mean(ca, axis=-1, keepdims=True)
    xc = ca - mu
    rstd = lax.rsqrt(jnp.mean(xc * xc, axis=-1, keepdims=True) + _LN_EPS)
    xhat = xc * rstd
    return xhat, rstd, xhat * g + b


def _conv_a_fwd(uc, w, cb, lg, lb, rider=None):
    s = uc.shape[0]
    c = _GROUP_COLS

    def body(val_ref, gate_ref, hval_ref, hgate_ref, w_ref, cb_ref, lg_ref, lb_ref, o_ref, ca_ref, hs_ref, sh_ref):
        _fill_glu(val_ref, gate_ref, hval_ref, hgate_ref, hs_ref)
        _fill_shifts(hs_ref, sh_ref)
        for rc in range(_CONV_T // _CONV_RC):
            r0 = rc * _CONV_RC
            ca = _conv_rows(hs_ref, sh_ref, w_ref, r0, _CONV_RC) + cb_ref[...]
            ca_ref[pl.ds(r0, _CONV_RC), :] = ca
            _, _, ln = _ln_fwd(ca, lg_ref[...], lb_ref[...])
            o_ref[pl.ds(r0, _CONV_RC), :] = (ln * _sig(ln)).astype(BF16)

    val, gate, hval, hgate = _conv_a_specs(s)
    wspec = pl.BlockSpec((_CONV_K, c), lambda i: (0, 0))
    vec = pl.BlockSpec((1, c), lambda i: (0, 0))
    blk = pl.BlockSpec((_CONV_T, c), lambda i: (i, 0))
    return _pcall(
        body, name="conv_a_fwd", grid=(s // _CONV_T,),
        in_specs=[val, gate, hval, hgate, wspec, vec, vec, vec],
        out_specs=[blk, blk],
        out_shape=[jax.ShapeDtypeStruct((s, 2 * c), BF16), jax.ShapeDtypeStruct((s, c), F32)],
        scratch=[pltpu.VMEM((_CONV_T + _HALO, c), F32), pltpu.VMEM((7, _SHIFT_ROWS, c), F32)],
        rider=rider)(uc, uc, uc, uc, w, cb, lg, lb)


def _conv_a_bwd_ln(ca_all, dcat, lg, lb, rider=None):
    s = ca_all.shape[0]
    c = _GROUP_COLS

    def body(ca_ref, dy_ref, lg_ref, lb_ref, dca_ref, st_ref):
        @pl.when(pl.program_id(0) == 0)
        def _():
            st_ref[...] = jnp.zeros_like(st_ref)

        for rc in range(_CONV_T // _CONV_RC):
            r0 = rc * _CONV_RC
            ca = ca_ref[pl.ds(r0, _CONV_RC), :]
            xhat, rstd, ln = _ln_fwd(ca, lg_ref[...], lb_ref[...])
            sg = _sig(ln)
            dln = dy_ref[pl.ds(r0, _CONV_RC), :] * (sg * (1.0 + ln * (1.0 - sg)))
            dxh = dln * lg_ref[...]
            dca = rstd * (dxh - jnp.mean(dxh, axis=-1, keepdims=True)
                          - xhat * jnp.mean(dxh * xhat, axis=-1, keepdims=True))
            dca_ref[pl.ds(r0, _CONV_RC), :] = dca
            st_ref[0:1, :] += jnp.sum(dca, axis=0, keepdims=True)
            st_ref[1:2, :] += jnp.sum(dln * xhat, axis=0, keepdims=True)
            st_ref[2:3, :] += jnp.sum(dln, axis=0, keepdims=True)

    blk = pl.BlockSpec((_CONV_T, c), lambda i: (i, 0))
    vec = pl.BlockSpec((1, c), lambda i: (0, 0))
    st = pl.BlockSpec((8, c), lambda i: (0, 0))
    return _pcall(
        body, name="conv_a_bwd_ln", grid=(s // _CONV_T,),
        in_specs=[blk, blk, vec, vec], out_specs=[blk, st],
        out_shape=[jax.ShapeDtypeStruct((s, c), F32), jax.ShapeDtypeStruct((8, c), F32)],
        rider=rider)(ca_all, dcat, lg, lb)


def _conv_a_bwd_conv(uc, dca, w, rider=None):
    s = uc.shape[0]
    c = _GROUP_COLS
    nblk = s // _CONV_T
    hb = _CONV_T // _HALO
    off = _HALO - (_CONV_K - 1)

    def body(val_ref, gate_ref, hval_ref, hgate_ref, d_ref, dn_ref, w_ref, du_ref, dw_ref, hs_ref, ds_ref,
             hsh_ref, dsh_ref, dwa_ref):
        i = pl.program_id(0)
        _fill_glu(val_ref, gate_ref, hval_ref, hgate_ref, hs_ref)
        ds_ref[pl.ds(0, _CONV_T), :] = d_ref[...]
        ds_ref[pl.ds(_CONV_T, _HALO), :] = jnp.where(i < nblk - 1, dn_ref[...], 0.0)
        _fill_shifts(hs_ref, hsh_ref)
        _fill_shifts(ds_ref, dsh_ref)

        @pl.when(i == 0)
        def _():
            dwa_ref[...] = jnp.zeros_like(dwa_ref)

        rows = 32
        for r0 in range(0, _CONV_T, rows):
            dcur = ds_ref[pl.ds(r0, rows), :]
            dh = jnp.zeros((rows, c), F32)
            for kk in range(_CONV_K):
                dh = dh + w_ref[kk:kk + 1, :] * _tap_rows(ds_ref, dsh_ref, r0 + _CONV_K - 1 - kk, rows)
                prod = dcur * _tap_rows(hs_ref, hsh_ref, r0 + off + kk, rows)
                dwa_ref[pl.ds(8 * kk, 8), :] += sum(prod[t:t + 8] for t in range(0, rows, 8))
            v = val_ref[pl.ds(r0, rows), :].astype(F32)
            sg = _sig(gate_ref[pl.ds(r0, rows), :].astype(F32))
            du_ref[pl.ds(r0, rows), pl.ds(0, c)] = (dh * sg).astype(BF16)
            du_ref[pl.ds(r0, rows), pl.ds(c, c)] = (dh * v * sg * (1.0 - sg)).astype(BF16)

        @pl.when(i == nblk - 1)
        def _():
            dw_ref[...] = jnp.zeros_like(dw_ref)
            for kk in range(_CONV_K):
                dw_ref[kk:kk + 1, :] = jnp.sum(dwa_ref[pl.ds(8 * kk, 8), :], axis=0, keepdims=True)

    val, gate, hval, hgate = _conv_a_specs(s)
    blk = pl.BlockSpec((_CONV_T, c), lambda i: (i, 0))
    nxt = pl.BlockSpec((_HALO, c), lambda i: (jnp.minimum((i + 1) * hb, s // _HALO - 1), 0))
    wspec = pl.BlockSpec((_CONV_K, c), lambda i: (0, 0))
    return _pcall(
        body, name="conv_a_bwd_conv", grid=(nblk,),
        in_specs=[val, gate, hval, hgate, blk, nxt, wspec],
        out_specs=[pl.BlockSpec((_CONV_T, 2 * c), lambda i: (i, 0)), pl.BlockSpec((_HALO, c), lambda i: (0, 0))],
        out_shape=[jax.ShapeDtypeStruct((s, 2 * c), BF16), jax.ShapeDtypeStruct((_HALO, c), F32)],
        scratch=[pltpu.VMEM((_CONV_T + _HALO, c), F32), pltpu.VMEM((_CONV_T + _HALO, c), F32),
                 pltpu.VMEM((7, _SHIFT_ROWS, c), F32), pltpu.VMEM((7, _SHIFT_ROWS, c), F32),
                 pltpu.VMEM((8 * _HALO, c), F32)],
        rider=rider,
    )(uc, uc, uc, uc, dca, dca, w)


_SC_T = 256
_SC_RC = 32
_SC_LC = 512


def _sc_chunks(d):
    return [(pl.ds(r0, _SC_RC), pl.ds(l0, _SC_LC)) for r0 in range(0, _SC_T, _SC_RC) for l0 in range(0, d, _SC_LC)]


def _short_conv_fwd(u2, w):
    s, d3 = u2.shape
    d = d3 // 3
    hb = _SC_T // _SC_HALO

    def body(b_ref, c_ref, v_ref, hc_ref, hv_ref, w_ref, o_ref, cs_ref):
        i = pl.program_id(0)
        cs_ref[pl.ds(0, _SC_HALO), :] = jnp.where(i > 0, hc_ref[...].astype(F32) * hv_ref[...].astype(F32), 0.0)
        for rows, lanes in _sc_chunks(d):
            cs_ref[pl.ds(_SC_HALO + rows.start, _SC_RC), lanes] = (
                c_ref[rows, lanes].astype(F32) * v_ref[rows, lanes].astype(F32))
        for rows, lanes in _sc_chunks(d):
            taps = [cs_ref[pl.ds(_SC_HALO - 2 + kk + rows.start, _SC_RC), lanes] for kk in range(3)]
            conv = w_ref[0:1, lanes] * taps[0] + w_ref[1:2, lanes] * taps[1] + w_ref[2:3, lanes] * taps[2]
            o_ref[rows, lanes] = (b_ref[rows, lanes].astype(F32) * conv).astype(BF16)

    def col(j):
        return pl.BlockSpec((_SC_T, d), lambda i: (i, j))

    def halo(j):
        return pl.BlockSpec((_SC_HALO, d), lambda i: (jnp.maximum(i * hb - 1, 0), j))

    return _pcall(
        body, name="short_conv_fwd", grid=(s // _SC_T,),
        in_specs=[col(0), col(1), col(2), halo(1), halo(2), pl.BlockSpec((3, d), lambda i: (0, 0))],
        out_specs=pl.BlockSpec((_SC_T, d), lambda i: (i, 0)),
        out_shape=jax.ShapeDtypeStruct((s, d), BF16),
        scratch=[pltpu.VMEM((_SC_T + _SC_HALO, d), F32)])(u2, u2, u2, u2, u2, w)


def _short_conv_bwd(u2, dsc, w, rider=None):
    s, d3 = u2.shape
    d = d3 // 3
    hb = _SC_T // _SC_HALO
    nblk = s // _SC_T

    def body(b_ref, c_ref, v_ref, hc_ref, hv_ref, nb_ref, d_ref, nd_ref, w_ref, du_ref, dw_ref, cs_ref, ds_ref):
        i = pl.program_id(0)
        cs_ref[pl.ds(0, _SC_HALO), :] = jnp.where(i > 0, hc_ref[...].astype(F32) * hv_ref[...].astype(F32), 0.0)
        ds_ref[pl.ds(_SC_T, _SC_HALO), :] = jnp.where(i < nblk - 1, nd_ref[...] * nb_ref[...].astype(F32), 0.0)
        for rows, lanes in _sc_chunks(d):
            cs_ref[pl.ds(_SC_HALO + rows.start, _SC_RC), lanes] = (
                c_ref[rows, lanes].astype(F32) * v_ref[rows, lanes].astype(F32))
            ds_ref[rows, lanes] = d_ref[rows, lanes] * b_ref[rows, lanes].astype(F32)

        @pl.when(i == 0)
        def _():
            dw_ref[...] = jnp.zeros_like(dw_ref)

        for l0 in range(0, d, _SC_LC):
            lanes = pl.ds(l0, _SC_LC)
            dw_acc = [jnp.zeros((8, _SC_LC), F32)] * 3
            for r0 in range(0, _SC_T, _SC_RC):
                rows = pl.ds(r0, _SC_RC)
                taps = [cs_ref[pl.ds(_SC_HALO - 2 + kk + r0, _SC_RC), lanes] for kk in range(3)]
                conv = w_ref[0:1, lanes] * taps[0] + w_ref[1:2, lanes] * taps[1] + w_ref[2:3, lanes] * taps[2]
                dconv = ds_ref[rows, lanes]
                dcv = (w_ref[2:3, lanes] * dconv + w_ref[1:2, lanes] * ds_ref[pl.ds(r0 + 1, _SC_RC), lanes]
                       + w_ref[0:1, lanes] * ds_ref[pl.ds(r0 + 2, _SC_RC), lanes])
                du_ref[rows, lanes] = (d_ref[rows, lanes] * conv).astype(BF16)
                du_ref[rows, pl.ds(d + l0, _SC_LC)] = (dcv * v_ref[rows, lanes].astype(F32)).astype(BF16)
                du_ref[rows, pl.ds(2 * d + l0, _SC_LC)] = (dcv * c_ref[rows, lanes].astype(F32)).astype(BF16)
                for kk in range(3):
                    prod = dconv * taps[kk]
                    dw_acc[kk] = dw_acc[kk] + sum(prod[t:t + 8] for t in range(0, _SC_RC, 8))
            for kk in range(3):
                dw_ref[kk:kk + 1, lanes] += jnp.sum(dw_acc[kk], axis=0, keepdims=True)

    def col(j):
        return pl.BlockSpec((_SC_T, d), lambda i: (i, j))

    def halo(j):
        return pl.BlockSpec((_SC_HALO, d), lambda i: (jnp.maximum(i * hb - 1, 0), j))

    def nxt(j):
        return pl.BlockSpec((_SC_HALO, d), lambda i: (jnp.minimum((i + 1) * hb, s // _SC_HALO - 1), j))

    return _pcall(
        body, name="short_conv_bwd", grid=(nblk,),
        in_specs=[col(0), col(1), col(2), halo(1), halo(2), nxt(0), col(0), nxt(0),
                  pl.BlockSpec((3, d), lambda i: (0, 0))],
        out_specs=[pl.BlockSpec((_SC_T, d3), lambda i: (i, 0)), pl.BlockSpec((8, d), lambda i: (0, 0))],
        out_shape=[jax.ShapeDtypeStruct((s, d3), BF16), jax.ShapeDtypeStruct((8, d), F32)],
        scratch=[pltpu.VMEM((_SC_T + _SC_HALO, d), F32), pltpu.VMEM((_SC_T + _SC_HALO, d), F32)],
        rider=rider,
    )(u2, u2, u2, u2, u2, u2, dsc, dsc, w)


def _bucket_maps():
    a_idx = jnp.arange(_STEPS)[:, None]
    c_idx = jnp.arange(2 * _STEPS)[None, :]
    mdist = jnp.clip(a_idx + _STEPS - c_idx, 0, _STEPS)
    max_exact = _NUM_BUCKETS // 2
    maps = []
    for _, dil in _GROUPS:
        nn = mdist * dil
        nf = jnp.maximum(nn, 1).astype(F32)
        large = max_exact + (jnp.log(nf / max_exact) / math.log(_MAX_DISTANCE / max_exact)
                             * (_NUM_BUCKETS - max_exact)).astype(jnp.int32)
        maps.append(jnp.where(nn < max_exact, nn, jnp.minimum(large, _NUM_BUCKETS - 1)).astype(jnp.int32))
    return jnp.stack(maps, axis=0)


def _bias_expand(rel_bias, buckets):
    nh = rel_bias.shape[1]

    def body(rb_ref, bk_ref, o_ref):
        h = pl.program_id(0)
        bk = bk_ref[0]
        acc = jnp.zeros(bk.shape, F32)
        for b in range(_NUM_BUCKETS):
            acc = jnp.where(bk == b, rb_ref[b, h], acc)
        a = lax.broadcasted_iota(jnp.int32, bk.shape, 0)
        c = lax.broadcasted_iota(jnp.int32, bk.shape, 1)
        mdist = a + _STEPS - c
        o_ref[0] = jnp.where((mdist >= 0) & (mdist <= _STEPS), acc, _NEG)

    return _pcall(
        body, name="bias_expand", grid=(nh,),
        in_specs=[pl.BlockSpec(memory_space=pltpu.SMEM),
                  pl.BlockSpec((1, _STEPS, 2 * _STEPS), lambda h: (h // 8, 0, 0))],
        out_specs=pl.BlockSpec((1, _STEPS, 2 * _STEPS), lambda h: (h, 0, 0)),
        out_shape=jax.ShapeDtypeStruct((nh, _STEPS, 2 * _STEPS), F32))(rel_bias, buckets)


def _bias_reduce(ds_all, buckets):
    nh = ds_all.shape[0]

    def body(ds_ref, bk_ref, o_ref):
        t, bk = ds_ref[0], bk_ref[0]
        rows = lax.broadcasted_iota(jnp.int32, (_NUM_BUCKETS, _LANES), 0)
        out = jnp.zeros((_NUM_BUCKETS, _LANES), F32)
        for b in range(_NUM_BUCKETS):
            out = jnp.where(rows == b, jnp.sum(jnp.where(bk == b, t, 0.0)), out)
        o_ref[0] = out

    blk = pl.BlockSpec((1, _STEPS, 2 * _STEPS), lambda h: (h, 0, 0))
    return _pcall(
        body, name="bias_reduce", grid=(nh,),
        in_specs=[blk, pl.BlockSpec((1, _STEPS, 2 * _STEPS), lambda h: (h // 8, 0, 0))],
        out_specs=pl.BlockSpec((1, _NUM_BUCKETS, _LANES), lambda h: (h, 0, 0)),
        out_shape=jax.ShapeDtypeStruct((nh, _NUM_BUCKETS, _LANES), F32))(ds_all, buckets)


def _sub_residues(dil):
    return 4 if dil % 16 == 0 else 1


def _strided_rows(ref, tmp_ref, p, r, dil):
    sub = _sub_residues(dil)
    if dil == 1:
        return [ref[p]]
    if sub == 1:
        return [ref[p, pl.ds(r, _STEPS, stride=dil), :]]
    tmp_ref[...] = ref[p, pl.ds(r, _STEPS * sub, stride=dil // sub), :]
    return [tmp_ref[pl.ds(q, _STEPS, stride=sub), :] for q in range(sub)]


def _store_strided(ref, tmp_ref, p, r, dil, vals):
    sub = _sub_residues(dil)
    if dil == 1:
        ref[p] = vals[0]
    elif sub == 1:
        ref[p, pl.ds(r, _STEPS, stride=dil), :] = vals[0]
    else:
        for q, val in enumerate(vals):
            tmp_ref[pl.ds(q, _STEPS, stride=sub), :] = val
        ref[p, pl.ds(r, _STEPS * sub, stride=dil // sub), :] = tmp_ref[...]


def _tmp_rows(dil, count):
    sub = _sub_residues(dil)
    return [pltpu.VMEM((_STEPS * sub, _LANES), F32)] * count if sub > 1 else []


def _head_masks():
    lane = lax.broadcasted_iota(jnp.int32, (1, _LANES), 1)
    return [lane < _HEAD_DIM, lane >= _HEAD_DIM]


def _stack_heads(x2, masks):
    return jnp.concatenate([jnp.where(masks[0], x2, 0), jnp.where(masks[1], x2, 0)], axis=0)


def _unstack_heads(y, masks):
    return jnp.where(masks[0], y[:_STEPS], y[_STEPS:])


def _scores(qs2, k2, b_ref, j, first):
    sc = lax.dot_general(qs2, k2, (((1,), (1,)), ((), ())), preferred_element_type=F32)
    sc = sc * (_HEAD_DIM ** -0.5) + jnp.concatenate([b_ref[2 * j], b_ref[2 * j + 1]], axis=0)
    col = lax.broadcasted_iota(jnp.int32, sc.shape, 1)
    return jnp.where(jnp.logical_and(first, col < _STEPS), _NEG, sc)


_PAIRS = _GROUP_COLS // _LANES


def _attn_fwd(uq, uk, uv, bias, g, dil, pp, rider=None):
    s = uq.shape[1]
    rb = _STEPS * dil
    nb = s // rb
    npb = _PAIRS // pp

    sub = _sub_residues(dil)

    def body(q_ref, kc_ref, kp_ref, vc_ref, vp_ref, b_ref, o_ref, l_ref, *tmp):
        tmp = tmp + (None,) * 7
        n, r = pl.program_id(1), pl.program_id(2)
        first = n == 0
        masks = _head_masks()
        for j in range(pp):
            qs = _strided_rows(q_ref, tmp[0], j, r, dil)
            kps, kcs = _strided_rows(kp_ref, tmp[1], j, r, dil), _strided_rows(kc_ref, tmp[2], j, r, dil)
            vps, vcs = _strided_rows(vp_ref, tmp[3], j, r, dil), _strided_rows(vc_ref, tmp[4], j, r, dil)
            o_res, l_res = [], []
            for q in range(sub):
                q2 = qs[q].astype(BF16)
                k2 = jnp.concatenate([kps[q], kcs[q]], axis=0).astype(BF16)
                v2 = jnp.concatenate([vps[q], vcs[q]], axis=0).astype(BF16)
                sc = _scores(_stack_heads(q2, masks), k2, b_ref, j, first)
                mx = jnp.max(sc, axis=-1, keepdims=True)
                p = jnp.exp(sc - mx)
                den = jnp.sum(p, axis=-1, keepdims=True)
                o2 = jnp.dot(p.astype(BF16), v2, preferred_element_type=F32) / den
                o_res.append(_unstack_heads(o2, masks))
                l_res.append(_unstack_heads(jnp.broadcast_to(mx + jnp.log(den), o2.shape), masks))
            _store_strided(o_ref, tmp[5], j, r, dil, o_res)
            _store_strided(l_ref, tmp[6], j, r, dil, l_res)

    cur = pl.BlockSpec((pp, rb, _LANES), lambda hb, n, r: (g * npb + hb, n, 0))
    prev = pl.BlockSpec((pp, rb, _LANES), lambda hb, n, r: (g * npb + hb, jnp.maximum(n - 1, 0), 0))
    bspec = pl.BlockSpec((2 * pp, _STEPS, 2 * _STEPS), lambda hb, n, r: (g * npb + hb, 0, 0))
    ospec = pl.BlockSpec((pp, rb, _LANES), lambda hb, n, r: (hb, n, 0))
    sh = jax.ShapeDtypeStruct((_PAIRS, s, _LANES), F32)
    return _pcall(
        body, name=f"attn_fwd_g{g}", grid=(npb, nb, dil // sub),
        in_specs=[cur, cur, prev, cur, prev, bspec], out_specs=[ospec, ospec], out_shape=[sh, sh],
        scratch=_tmp_rows(dil, 7), rider=rider,
    )(uq, uk, uk, uv, uv, bias)


def _attn_merge(outs, lses, cat):
    s = outs[0].shape[1]
    c = _GROUP_COLS

    def body(o0, o1, o2, l0, l1, l2, cat_in, cat_ref, lse_ref):
        del cat_in
        a0, a1, a2 = l0[...], l1[...], l2[...]
        mx = jnp.maximum(jnp.maximum(a0, a1), a2)
        w0, w1, w2 = jnp.exp(a0 - mx), jnp.exp(a1 - mx), jnp.exp(a2 - mx)
        den = w0 + w1 + w2
        y = ((w0 * o0[...] + w1 * o1[...] + w2 * o2[...]) / den).astype(BF16)
        for p in range(_PAIRS):
            cat_ref[:, p * _LANES:(p + 1) * _LANES] = y[p]
        lse_ref[...] = mx + jnp.log(den)

    blk = pl.BlockSpec((_PAIRS, _ROW_T, _LANES), lambda i: (0, i, 0))
    return _pcall(
        body, name="attn_merge", grid=(s // _ROW_T,),
        in_specs=[blk] * 6 + [_ANY],
        out_specs=[pl.BlockSpec((_ROW_T, c), lambda i: (i, 1)), blk],
        out_shape=[jax.ShapeDtypeStruct(cat.shape, BF16), jax.ShapeDtypeStruct((_PAIRS, s, _LANES), F32)],
        aliases={6: 0})(*outs, *lses, cat)


def _attn_delta(dcat, cat):
    s = dcat.shape[0]
    c = _GROUP_COLS
    seg = (jnp.arange(c)[:, None] // _HEAD_DIM == jnp.arange(c)[None, :] // _HEAD_DIM).astype(BF16)

    def body(dy_ref, y_ref, seg_ref, dl_ref, dys_ref):
        dy = dy_ref[...]
        prod = dy * y_ref[...].astype(F32)
        hi = prod.astype(BF16)
        lo = (prod - hi.astype(F32)).astype(BF16)
        dl = (jnp.dot(hi, seg_ref[...], preferred_element_type=F32)
              + jnp.dot(lo, seg_ref[...], preferred_element_type=F32))
        for p in range(_PAIRS):
            dl_ref[p] = dl[:, p * _LANES:(p + 1) * _LANES]
            dys_ref[p] = dy[:, p * _LANES:(p + 1) * _LANES]

    right = pl.BlockSpec((_ROW_T, c), lambda i: (i, 1))
    blk = pl.BlockSpec((_PAIRS, _ROW_T, _LANES), lambda i: (0, i, 0))
    sh = jax.ShapeDtypeStruct((_PAIRS, s, _LANES), F32)
    return _pcall(
        body, name="attn_delta", grid=(s // _ROW_T,),
        in_specs=[right, right, pl.BlockSpec((c, c), lambda i: (0, 0))],
        out_specs=[blk, blk], out_shape=[sh, sh])(dcat, cat, seg)


def _attn_bwd(uq, uk, uv, dys, lse, delta, bias, prev_grads, g, dil, pp, rider=None):
    s = uq.shape[1]
    rb = _STEPS * dil
    nb = s // rb
    npb = _PAIRS // pp
    scale = _HEAD_DIM ** -0.5

    sub = _sub_residues(dil)

    def body(q_ref, kc_ref, kp_ref, vc_ref, vp_ref, dy_ref, l_ref, dl_ref, b_ref, *rest):
        rest = rest[len(prev_grads):]
        dqkv_ref, dsa_ref, dqc_ref, dkc_ref, dvc_ref = rest[:5]
        dq_ref, dk_ref, dv_ref = dqkv_ref.at[0], dqkv_ref.at[1], dqkv_ref.at[2]
        tmp = rest[5:] + (None,) * 11
        n, r = pl.program_id(1), pl.program_id(2)

        def carry_slot(j, q):
            return ((r + (dil // sub) * q) * pp + j) if sub > 1 else r * pp + j

        @pl.when(jnp.logical_and(n == 0, r == 0))
        def _():
            dsa_ref[...] = jnp.zeros_like(dsa_ref)

        @pl.when(n == 0)
        def _():
            for j in range(pp):
                for q in range(sub):
                    for carry in (dqc_ref, dkc_ref, dvc_ref):
                        carry[carry_slot(j, q)] = jnp.zeros((_STEPS, _LANES), F32)

        @pl.when(n < nb)
        def _():
            first = n == 0
            masks = _head_masks()
            for j in range(pp):
                qs = _strided_rows(q_ref, tmp[0], j, r, dil)
                kps, kcs = _strided_rows(kp_ref, tmp[1], j, r, dil), _strided_rows(kc_ref, tmp[2], j, r, dil)
                vps, vcs = _strided_rows(vp_ref, tmp[3], j, r, dil), _strided_rows(vc_ref, tmp[4], j, r, dil)
                dys_ = _strided_rows(dy_ref, tmp[5], j, r, dil)
                lses = _strided_rows(l_ref, tmp[6], j, r, dil)
                dls = _strided_rows(dl_ref, tmp[7], j, r, dil)
                ds_sum = [jnp.zeros((_STEPS, 2 * _STEPS), F32)] * 2
                dq_res, dk_res, dv_res = [], [], []
                for q in range(sub):
                    q2 = qs[q].astype(BF16)
                    k2 = jnp.concatenate([kps[q], kcs[q]], axis=0).astype(BF16)
                    v2 = jnp.concatenate([vps[q], vcs[q]], axis=0).astype(BF16)
                    dy2 = dys_[q].astype(BF16)
                    qs2, dys2 = _stack_heads(q2, masks), _stack_heads(dy2, masks)
                    per_row = lambda st: jnp.concatenate([st[:, 0:1], st[:, _HEAD_DIM:_HEAD_DIM + 1]], axis=0)
                    sc = _scores(qs2, k2, b_ref, j, first)
                    p = jnp.exp(sc - per_row(lses[q]))
                    dp = lax.dot_general(dys2, v2, (((1,), (1,)), ((), ())), preferred_element_type=F32)
                    ds = p * (dp - per_row(dls[q]))
                    ds_sum[0] = ds_sum[0] + ds[:_STEPS]
                    ds_sum[1] = ds_sum[1] + ds[_STEPS:]
                    dsb = ds.astype(BF16)
                    dq_p = _unstack_heads(jnp.dot(dsb, k2, preferred_element_type=F32), masks)
                    tdn = (((0,), (0,)), ((), ()))
                    dk_p = lax.dot_general(dsb, qs2, tdn, preferred_element_type=F32) * scale
                    dv_p = lax.dot_general(p.astype(BF16), dys2, tdn, preferred_element_type=F32)
                    slot = carry_slot(j, q)
                    dq_res.append(dqc_ref[slot])
                    dk_res.append(dkc_ref[slot] + dk_p[:_STEPS])
                    dv_res.append(dvc_ref[slot] + dv_p[:_STEPS])
                    dqc_ref[slot] = dq_p * scale
                    dkc_ref[slot] = dk_p[_STEPS:]
                    dvc_ref[slot] = dv_p[_STEPS:]
                for hh in range(2):
                    dsa_ref[2 * j + hh] += ds_sum[hh]
                _store_strided(dq_ref, tmp[8], j, r, dil, dq_res)
                _store_strided(dk_ref, tmp[9], j, r, dil, dk_res)
                _store_strided(dv_ref, tmp[10], j, r, dil, dv_res)

        @pl.when(n == nb)
        def _():
            for j in range(pp):
                for ref, carry, t in ((dq_ref, dqc_ref, 8), (dk_ref, dkc_ref, 9), (dv_ref, dvc_ref, 10)):
                    _store_strided(ref, tmp[t], j, r, dil, [carry[carry_slot(j, q)] for q in range(sub)])

    def clamp(n):
        return jnp.minimum(n, nb - 1)

    cur = pl.BlockSpec((pp, rb, _LANES), lambda hb, n, r: (g * npb + hb, clamp(n), 0))
    prev = pl.BlockSpec((pp, rb, _LANES), lambda hb, n, r: (g * npb + hb, jnp.maximum(clamp(n) - 1, 0), 0))
    stat = pl.BlockSpec((pp, rb, _LANES), lambda hb, n, r: (hb, clamp(n), 0))
    bspec = pl.BlockSpec((2 * pp, _STEPS, 2 * _STEPS), lambda hb, n, r: (g * npb + hb, 0, 0))
    late = pl.BlockSpec((3, pp, rb, _LANES), lambda hb, n, r: (0, g * npb + hb, jnp.maximum(n - 1, 0), 0))
    dsspec = pl.BlockSpec((2 * pp, _STEPS, 2 * _STEPS), lambda hb, n, r: (hb, 0, 0))
    np_ = len(prev_grads)
    carry = pltpu.VMEM((dil * pp, _STEPS, _LANES), F32)
    return _pcall(
        body, name=f"attn_bwd_g{g}", grid=(npb, nb + 1, dil // sub),
        in_specs=[cur, cur, prev, cur, prev, stat, stat, stat, bspec] + [_ANY] * np_,
        out_specs=[late, dsspec],
        out_shape=[jax.ShapeDtypeStruct((3, 3 * _PAIRS, s, _LANES), F32),
                   jax.ShapeDtypeStruct((8, _STEPS, 2 * _STEPS), F32)],
        scratch=[carry, carry, carry] + _tmp_rows(dil, 11),
        aliases={9 + t: t for t in range(np_)}, rider=rider,
    )(uq, uk, uk, uv, uv, dys, lse, delta, bias, *prev_grads)


def _place():
    x, y, c = lax.axis_index("x"), lax.axis_index("y"), lax.axis_index("c")
    chips = [(1 - x, y), (x, 1 - y), (1 - x, 1 - y)]
    return x, y, c, chips


def _slab(ref, axis, chip, width):
    start = pl.multiple_of(chip * width, width)
    if axis == 0:
        return ref.at[pl.ds(start, width), :]
    return ref.at[:, pl.ds(start, width)]


def _run_rider(rider, name):
    nin, nout = len(rider.ins), len(rider.out_shapes)

    def body(*refs):
        ins, outs, scr = refs[:nin], refs[nin:nin + nout], refs[nin + nout:]
        rider.start(ins, outs, scr)
        rider.finish(ins, outs, scr)

    return _pcall(body, name=name, in_specs=[_ANY] * nin, out_specs=[_ANY] * nout, out_shape=rider.out_shapes,
                  scratch=rider.scratch)(*rider.ins)


def _gather_halves_rider(shard, axis):
    shape = list(shard.shape)
    shape[axis] *= 4
    full = jax.ShapeDtypeStruct(tuple(shape), shard.dtype)
    half = shard.shape[0] // 2
    width = shard.shape[axis]

    def region(out, chip, core):
        if axis == 0:
            return out.at[pl.ds(pl.multiple_of(chip * width + core * half, half), half), :]
        return out.at[pl.ds(pl.multiple_of(core * half, half), half), pl.ds(pl.multiple_of(chip * width, width), width)]

    def copies(ins, outs, scr):
        send, recv, loc = scr
        (src,), (out,) = ins, outs
        x, y, c, chips = _place()
        mine = 2 * x + y
        own = pltpu.make_async_copy(src, _slab(out, axis, mine, width), loc.at[0])
        my_half = src.at[pl.ds(pl.multiple_of(c * half, half), half), :]
        over_ici, ici_in, to_sib, sib_in = [], [], [], []
        for j, (px, py) in enumerate(chips):
            theirs = 2 * px + py
            ici = dict(send_sem=send.at[j], recv_sem=recv.at[j], device_id=(px, py, c), device_id_type=MESH)
            d2d = dict(send_sem=send.at[3 + j], recv_sem=recv.at[3 + j], device_id=(x, y, 1 - c),
                       device_id_type=MESH)
            over_ici.append(pltpu.make_async_remote_copy(src_ref=my_half, dst_ref=region(out, mine, c), **ici))
            ici_in.append(pltpu.make_async_remote_copy(src_ref=my_half, dst_ref=region(out, theirs, c), **ici))
            to_sib.append(pltpu.make_async_remote_copy(
                src_ref=region(out, theirs, c), dst_ref=region(out, theirs, c), **d2d))
            sib_in.append(pltpu.make_async_remote_copy(
                src_ref=region(out, theirs, c), dst_ref=region(out, theirs, 1 - c), **d2d))
        return own, over_ici, ici_in, to_sib, sib_in

    def start(ins, outs, scr):
        own, over_ici, _, _, _ = copies(ins, outs, scr)
        own.start()
        for cp in over_ici:
            cp.start()

    def finish(ins, outs, scr):
        own, over_ici, ici_in, to_sib, sib_in = copies(ins, outs, scr)
        for j in range(3):
            ici_in[j].wait_recv()
            to_sib[j].start()
        for cp in sib_in:
            cp.wait_recv()
        own.wait()
        for cp in over_ici + to_sib:
            cp.wait_send()

    return _Rider([shard], [full], [pltpu.SemaphoreType.DMA((6,)), pltpu.SemaphoreType.DMA((6,)),
                                    pltpu.SemaphoreType.DMA((1,))], start, finish)


def _join_riders(riders):
    if len(riders) == 1:
        return riders[0]

    def parts(ins, outs, scr):
        pi = po = ps = 0
        for rd in riders:
            ni, no, ns = len(rd.ins), len(rd.out_shapes), len(rd.scratch)
            yield rd, ins[pi:pi + ni], outs[po:po + no], scr[ps:ps + ns]
            pi, po, ps = pi + ni, po + no, ps + ns

    def start(ins, outs, scr):
        for rd, i, o, sc in parts(ins, outs, scr):
            rd.start(i, o, sc)

    def finish(ins, outs, scr):
        for rd, i, o, sc in parts(ins, outs, scr):
            rd.finish(i, o, sc)

    return _Rider(sum((rd.ins for rd in riders), []), sum((rd.out_shapes for rd in riders), []),
                  sum((rd.scratch for rd in riders), []), start, finish)


def _scatter_rider(grads, axes, rows=None):
    nw = len(grads)
    outs_shape = []
    for gr, ax in zip(grads, axes):
        shape = list(gr.shape)
        shape[ax] //= 4
        if rows is not None:
            assert ax == 1
            shape[0] = rows[1] - rows[0]
        outs_shape.append(jax.ShapeDtypeStruct((3,) + tuple(shape), gr.dtype))

    def copies(ins, outs, scr):
        send, recv = scr
        x, y, c, chips = _place()
        cps = []
        for t in range(nw):
            width = ins[t].shape[axes[t]] // 4
            src = ins[t] if rows is None else ins[t].at[pl.ds(rows[0], rows[1] - rows[0]), :]
            for j, (px, py) in enumerate(chips):
                cps.append(pltpu.make_async_remote_copy(
                    src_ref=_slab(src, axes[t], 2 * px + py, width), dst_ref=outs[t].at[j],
                    send_sem=send.at[3 * t + j], recv_sem=recv.at[3 * t + j],
                    device_id=(px, py, c), device_id_type=MESH))
        return cps

    def start(ins, outs, scr):
        for cp in copies(ins, outs, scr):
            cp.start()

    def finish(ins, outs, scr):
        cps = copies(ins, outs, scr)
        for cp in cps:
            cp.wait_recv()
        for cp in cps:
            cp.wait_send()

    return _Rider(grads, outs_shape, [pltpu.SemaphoreType.DMA((3 * nw,)), pltpu.SemaphoreType.DMA((3 * nw,))],
                  start, finish)


def _swap_rider(parts):
    nw = len(parts)

    def copies(ins, outs, scr):
        send, recv = scr
        x, y, c, _ = _place()
        return [pltpu.make_async_remote_copy(
            src_ref=ins[t], dst_ref=outs[t], send_sem=send.at[t], recv_sem=recv.at[t],
            device_id=(x, y, 1 - c), device_id_type=MESH) for t in range(nw)]

    def start(ins, outs, scr):
        for cp in copies(ins, outs, scr):
            cp.start()

    def finish(ins, outs, scr):
        cps = copies(ins, outs, scr)
        for cp in cps:
            cp.wait_recv()
        for cp in cps:
            cp.wait_send()

    return _Rider(parts, [jax.ShapeDtypeStruct(p.shape, p.dtype) for p in parts],
                  [pltpu.SemaphoreType.DMA((nw,)), pltpu.SemaphoreType.DMA((nw,))], start, finish)


def _sum_all_devices(buf, name):
    rows, cols = buf.shape

    def body(in_ref, o_ref, gat_ref, send, recv):
        x, y, c, _ = _place()
        me = 4 * x + 2 * y + c
        gat_ref[me] = in_ref[...]
        started = []
        for mask in range(1, 8):
            fx, fy, fc = (mask >> 2) & 1, (mask >> 1) & 1, mask & 1
            peer = (x + fx * (1 - 2 * x), y + fy * (1 - 2 * y), c + fc * (1 - 2 * c))
            cp = pltpu.make_async_remote_copy(
                src_ref=in_ref, dst_ref=gat_ref.at[me], send_sem=send.at[mask - 1], recv_sem=recv.at[mask - 1],
                device_id=peer, device_id_type=MESH)
            cp.start()
            started.append(cp)
        for cp in started:
            cp.wait_recv()
        for cp in started:
            cp.wait_send()
        acc = gat_ref[0]
        for t in range(1, 8):
            acc = acc + gat_ref[t]
        o_ref[...] = acc

    vm = pl.BlockSpec(memory_space=pltpu.VMEM)
    return _pcall(
        body, name=name, in_specs=[vm], out_specs=vm, out_shape=jax.ShapeDtypeStruct((rows, cols), F32),
        scratch=[pltpu.VMEM((8, rows, cols), F32), pltpu.SemaphoreType.DMA((7,)), pltpu.SemaphoreType.DMA((7,))],
    )(buf)


_UPD_T = 256


def _sum_partials(own, got, name):
    rows, cols = own.shape
    tr = min(_UPD_T, rows)

    def body(own_ref, got_ref, o_ref):
        acc = own_ref[...].astype(F32)
        for j in range(3):
            acc = acc + got_ref[j].astype(F32)
        o_ref[...] = acc

    blk = pl.BlockSpec((tr, cols), lambda i: (i, 0))
    return _pcall(
        body, name=name, grid=(rows // tr,),
        in_specs=[blk, pl.BlockSpec((3, tr, cols), lambda i: (0, i, 0))], out_specs=blk,
        out_shape=jax.ShapeDtypeStruct((rows, cols), F32))(own, got)


def _adamw_math(w, gr, m, v):
    m = _B1 * m + (1.0 - _B1) * gr
    v = _B2 * v + (1.0 - _B2) * (gr * gr)
    m_hat = m / (1.0 - _B1 ** _STEP)
    v_hat = v / (1.0 - _B2 ** _STEP)
    delta = -_LR * (m_hat / (jnp.sqrt(v_hat) + _EPS) + _WD * w)
    return delta, m, v


def _adamw(w, m, v, parts, name):
    rows, cols = w.shape
    tr = min(_UPD_T, rows)
    npart = len(parts)

    def body(w_ref, m_ref, v_ref, *rest):
        p_refs, (g_ref, d_ref, nm_ref, nv_ref) = rest[:npart], rest[npart:]
        gr = p_refs[0][...]
        for p in p_refs[1:]:
            gr = gr + p[...]
        delta, nm, nv = _adamw_math(w_ref[...], gr, m_ref[...], v_ref[...])
        g_ref[...] = gr
        d_ref[...] = delta
        nm_ref[...] = nm
        nv_ref[...] = nv

    blk = pl.BlockSpec((tr, cols), lambda i: (i, 0))
    sh = jax.ShapeDtypeStruct((rows, cols), F32)
    return _pcall(body, name=name, grid=(rows // tr,), in_specs=[blk] * (3 + npart), out_specs=[blk] * 4,
                  out_shape=[sh] * 4)(w, m, v, *parts)


def _adamw_layers(w, m, v, parts, name):
    _, rows, cols = w.shape
    tr = min(_UPD_T, rows)
    npart = len(parts[0])

    def body(w_ref, m_ref, v_ref, *rest):
        p_refs, (g_ref, d_ref, nm_ref, nv_ref) = rest[:2 * npart], rest[2 * npart:]
        grs = []
        for layer in range(2):
            gr = p_refs[layer * npart][...]
            for p in p_refs[layer * npart + 1:(layer + 1) * npart]:
                gr = gr + p[...]
            grs.append(gr)
        gr = jnp.where(pl.program_id(0) == 0, grs[0], grs[1])
        delta, nm, nv = _adamw_math(w_ref[...], gr, m_ref[...], v_ref[...])
        g_ref[...] = gr
        d_ref[...] = delta
        nm_ref[...] = nm
        nv_ref[...] = nv

    blk = pl.BlockSpec((None, tr, cols), lambda l, i: (l, i, 0))

    def part_spec(layer):
        return pl.BlockSpec((tr, cols), lambda l, i: (jnp.where(l == layer, i, 0), 0))

    sh = jax.ShapeDtypeStruct(w.shape, F32)
    return _pcall(
        body, name=name, grid=(2, rows // tr),
        in_specs=[blk] * 3 + [part_spec(0)] * npart + [part_spec(1)] * npart, out_specs=[blk] * 4,
        out_shape=[sh] * 4)(w, m, v, *parts[0], *parts[1])


_PACK_W = 1024


def _pack(arrs, rows):
    flat = []
    for a in arrs:
        f = a.reshape(-1).astype(F32)
        pad = (-f.shape[0]) % _PACK_W
        flat.append(jnp.pad(f, (0, pad)))
    f = jnp.concatenate(flat)
    f = jnp.pad(f, (0, rows * _PACK_W - f.shape[0]))
    return f.reshape(rows, _PACK_W)


def _unpack(buf, shapes):
    flat = buf.reshape(-1)
    out, pos = [], 0
    for sh in shapes:
        size = math.prod(sh)
        out.append(flat[pos:pos + size].reshape(sh))
        pos += size + ((-size) % _PACK_W)
    return out


def _pack_rows(shapes):
    total = sum(-(-math.prod(sh) // _PACK_W) for sh in shapes)
    return -(-total // 8) * 8


def kernel(x, rel_bias, ab_norm, ab_w_in, ab_conv_w, ab_conv_b, ab_ln_g, ab_ln_b, ab_w_out, sc_norm, sc_w_in, sc_conv_w, sc_w_out, mlp_norm, mlp_w_up, mlp_w_down, final_norm, loss_target, m_rel_bias, m_ab_norm, m_ab_w_in, m_ab_conv_w, m_ab_conv_b, m_ab_ln_g, m_ab_ln_b, m_ab_w_out, m_sc_norm, m_sc_w_in, m_sc_conv_w, m_sc_w_out, m_mlp_norm, m_mlp_w_up, m_mlp_w_down, m_final_norm, v_rel_bias, v_ab_norm, v_ab_w_in, v_ab_conv_w, v_ab_conv_b, v_ab_ln_g, v_ab_ln_b, v_ab_w_out, v_sc_norm, v_sc_w_in, v_sc_conv_w, v_sc_w_out, v_mlp_norm, v_mlp_w_up, v_mlp_w_down, v_final_norm):
    s, d = x.shape[1], x.shape[2]
    dff = 4 * d
    c = _GROUP_COLS
    chip = 2 * lax.axis_index("x") + lax.axis_index("y")
    on_c0 = (lax.axis_index("c") == 0).astype(F32)
    h0 = x[0]
    tgt = loss_target[0]

    cw_sh, scn_sh, scw_sh = ab_conv_w[0], sc_norm, sc_conv_w[0]
    conv_w_full = lax.dynamic_update_slice(jnp.zeros((_CONV_K, c), F32), cw_sh * on_c0, (0, chip * cw_sh.shape[1]))
    scn_full = lax.dynamic_update_slice(jnp.zeros((1, d), F32), scn_sh * on_c0, (0, chip * scn_sh.shape[1]))
    scw_full = lax.dynamic_update_slice(jnp.zeros((3, d), F32), scw_sh * on_c0, (0, chip * scw_sh.shape[1]))
    small_shapes = [(_CONV_K, c), (1, d), (3, d)]
    small = _sum_all_devices(_pack([conv_w_full, scn_full, scw_full], _pack_rows(small_shapes)), "gather_small")
    conv_w, sc_g, sc_cw = _unpack(small, small_shapes)

    w_shards = [ab_w_in[0], ab_w_out[0], sc_w_in[0], sc_w_out[0], mlp_w_up[0], mlp_w_up[1],
                mlp_w_down[0], mlp_w_down[1]]
    w_axes = [1, 0, 1, 0, 1, 1, 0, 0]
    wb = [w.astype(BF16) for w in w_shards]
    full_w = [None] * 8

    def gather(idx):
        return _join_riders([_gather_halves_rider(wb[t], w_axes[t]) for t in idx])

    def put(idx, got_w):
        for t, w in zip(idx, got_w):
            full_w[t] = w

    buckets = _bucket_maps()
    bias = _bias_expand(rel_bias, buckets)
    n0, got_w = _rms_fwd(h0, ab_norm, "rms_fwd_ab", rider=_gather_halves_rider(wb[0], w_axes[0]))
    put([0], got_w)
    w_in = full_w[0]
    tm = min(1024, s)
    tm2 = min(2048, s)
    tmh = min(512, s)
    uc = _mm(n0, w_in, "nn", m=s, n=2 * c, k=d, tm=tm2, tn=2 * c, tk=d, out_dtype=BF16, name="proj_conv")
    uq, uk, uv = [], [], []
    for t, (nm, dst) in enumerate(zip("qkv", (uq, uk, uv))):
        res = _mm(n0, w_in, "nn", m=s, n=3 * c, k=d, tm=tm2, tn=c, tk=d, out_dtype=F32, name=f"proj_{nm}",
                  b_off=(0, 2 + 3 * t), split="o", rider=gather([1]) if t == 0 else None)
        if t == 0:
            res, got_w = res
            put([1], got_w)
        dst.append(res)
    uq, uk, uv = uq[0], uk[0], uv[0]
    (cat, ca), got_w = _conv_a_fwd(uc, conv_w, ab_conv_b, ab_ln_g, ab_ln_b, rider=gather([2]))
    put([2], got_w)
    outs, lses = [], []
    for g, (_, dil) in enumerate(_GROUPS):
        idx = ([4], [6], [3, 5])[g]
        (o, l), got_w = _attn_fwd(uq, uk, uv, bias, g, dil, 4 if dil <= 4 else 2, rider=gather(idx))
        put(idx, got_w)
        outs.append(o)
        lses.append(l)
    cat, lse = _attn_merge(outs, lses, cat)
    h1, n1 = _mm(cat, full_w[1], "nn", m=s, n=d, k=d, tm=tm, tn=d, tk=d, out_dtype=(F32, BF16), name="out_ab",
                 epi=_epi_add_rms, extras=(h0,), vecs=(mlp_norm[0:1],))

    def mlp_fwd(h, nrm, layer, next_gain=None, rider=None):
        zr = _mm(nrm, full_w[4 + layer], "nn", m=s, n=dff, k=d, tm=tmh, tn=dff, tk=d, out_dtype=BF16,
                 name=f"mlp_up{layer}", epi=_epi_relu, rider=rider)
        if rider is not None:
            zr, got_r = zr
            put([7], got_r)
        kw = dict(m=s, n=d, k=dff, tm=tmh, tn=d, tk=dff, name=f"mlp_down{layer}", a_pro=_square, extras=(h,))
        if next_gain is None:
            return zr, _mm(zr, full_w[6 + layer], "nn", out_dtype=F32, epi=_epi_add, **kw), None
        hn, nn = _mm(zr, full_w[6 + layer], "nn", out_dtype=(F32, BF16), epi=_epi_add_rms, vecs=(next_gain,), **kw)
        return zr, hn, nn

    zr0, h2, n2 = mlp_fwd(h1, n1, 0, next_gain=sc_g, rider=gather([7]))
    _, w_out, w_si, w_so, w_up0, w_up1, w_dn0, w_dn1 = full_w
    w_up, w_dn = [w_up0, w_up1], [w_dn0, w_dn1]
    u2 = _mm(n2, w_si, "nn", m=s, n=3 * d, k=d, tm=tmh, tn=3 * d, tk=d, out_dtype=BF16, name="proj_sc")
    scv = _short_conv_fwd(u2, sc_cw)
    h3, n3 = _mm(scv, w_so, "nn", m=s, n=d, k=d, tm=tm, tn=d, tk=d, out_dtype=(F32, BF16), name="out_sc",
                 epi=_epi_add_rms, extras=(h2,), vecs=(mlp_norm[1:2],))
    zr1, h4, _ = mlp_fwd(h3, n3, 1)

    dh4, dh4b, g_final, loss_part = _loss_head(h4, tgt, final_norm.reshape(1, d))
    tkw = min(2048, s)

    big_grads, got, sums = [None] * 8, [None] * 8, [None] * 8

    def scatter(t):
        return _scatter_rider([big_grads[t]], [w_axes[t]])

    def own_slab(t):
        width = big_grads[t].shape[w_axes[t]] // 4
        return lax.dynamic_slice_in_dim(big_grads[t], chip * width, width, axis=w_axes[t])

    def arrived(t, got_t):
        got[t] = got_t[0]
        sums[t] = _sum_partials(own_slab(t), got[t], f"sum_partials{t}")

    def mlp_bwd(dh, dhb, h, nrm, zr, layer):
        dz = _mm(dhb, w_dn[layer], "nt", m=s, n=dff, k=d, tm=tmh, tn=dff, tk=d, out_dtype=BF16,
                 name=f"mlp_down{layer}_dx", epi=_epi_relu_sq_bwd, extras=(zr,))
        big_grads[6 + layer] = _mm(zr, dhb, "tn", m=dff, n=d, k=s, tm=1024, tn=d, tk=tkw, out_dtype=BF16,
                                   name=f"mlp_down{layer}_dw", a_pro=_square)
        big_grads[4 + layer] = _mm(nrm, dz, "tn", m=d, n=dff, k=s, tm=d, tn=1024, tk=tkw, out_dtype=BF16,
                                   name=f"mlp_up{layer}_dw")
        res = _mm(dz, w_up[layer], "nt", m=s, n=d, k=dff, tm=tmh, tn=d, tk=dff, out_dtype=(F32, BF16),
                  name=f"mlp_up{layer}_dx", rider=scatter(6) if layer == 0 else None, epi=_epi_rms_bwd,
                  extras=(h, dh), vecs=(mlp_norm[layer:layer + 1],), row_sum=True)
        if layer == 0:
            res, got_t = res
            arrived(6, got_t)
        return res

    dh3, dh3b, g_mn1 = mlp_bwd(dh4, dh4b, h3, n3, zr1, 1)

    dsc = _mm(dh3b, w_so, "nt", m=s, n=d, k=d, tm=tm, tn=d, tk=d, out_dtype=F32, name="out_sc_dx")
    big_grads[3] = _mm(scv, dh3b, "tn", m=d, n=d, k=s, tm=d, tn=d, tk=tkw, out_dtype=BF16, name="out_sc_dw")
    du2, g_sccw8 = _short_conv_bwd(u2, dsc, sc_cw)
    big_grads[2], got_t = _mm(n2, du2, "tn", m=d, n=3 * d, k=s, tm=d, tn=1024, tk=tkw, out_dtype=BF16,
                              name="proj_sc_dw", rider=scatter(3))
    arrived(3, got_t)
    dh2, dh2b, g_scn = _mm(
        du2, w_si, "nt", m=s, n=d, k=3 * d, tm=tmh, tn=d, tk=3 * d, out_dtype=(F32, BF16), name="proj_sc_dx",
        epi=_epi_rms_bwd, extras=(h2, dh3), vecs=(sc_g,), row_sum=True)

    dh1, dh1b, g_mn0 = mlp_bwd(dh2, dh2b, h1, n1, zr0, 0)

    dcat = _mm(dh1b, w_out, "nt", m=s, n=d, k=d, tm=tm, tn=d, tk=d, out_dtype=F32, name="out_ab_dx")
    big_grads[1] = _mm(cat, dh1b, "tn", m=d, n=d, k=s, tm=d, tn=d, tk=tkw, out_dtype=BF16, name="out_ab_dw")
    (dca, conv_stats), got_t = _conv_a_bwd_ln(ca, dcat, ab_ln_g, ab_ln_b, rider=scatter(1))
    arrived(1, got_t)
    (duc, g_cw32), got_t = _conv_a_bwd_conv(uc, dca, conv_w, rider=scatter(4))
    arrived(4, got_t)
    delta, dys = _attn_delta(dcat, cat)

    dqkv, ds_list = [], []
    for g, (_, dil) in enumerate(_GROUPS):
        late = (2, 5, 7)[g]
        (grads, dsa), got_t = _attn_bwd(uq, uk, uv, dys, lse, delta, bias, dqkv, g, dil, 4 if dil <= 4 else 1,
                                        rider=scatter(late))
        arrived(late, got_t)
        dqkv = [grads]
        ds_list.append(dsa)
    g_bias = _bias_reduce(jnp.concatenate(ds_list, axis=0), buckets)[:, :, 0].T
    dqkv = dqkv[0].reshape(9 * _PAIRS, s, _LANES)

    g_in_conv = _mm(n0, duc, "tn", m=d, n=2 * c, k=s, tm=d, tn=2 * c, tk=tkw, out_dtype=BF16,
                    name="proj_ab_dw_conv")
    g_in_qkv, sib_late = _mm(n0, dqkv, "tn", m=d, n=9 * c, k=s, tm=d, tn=3 * c, tk=min(1024, s), out_dtype=BF16,
                             name="proj_ab_dw_qkv", rider=_swap_rider(sums[1:]), split="b")
    big_grads[0] = jnp.concatenate([g_in_conv, g_in_qkv], axis=1)
    dn0 = _mm(duc, w_in, "nt", m=s, n=d, k=2 * c, tm=tm, tn=d, tk=2 * c, out_dtype=F32, name="proj_ab_dx_conv")
    cut = d * 3 // 4
    dn0, (got_top,) = _mm(dqkv, w_in[:, 2 * c:], "nt", m=s, n=d, k=9 * c, tm=tm, tn=d, tk=3 * c, out_dtype=F32,
                          name="proj_ab_dx_qkv", epi=_epi_add, extras=(dn0,), split="a",
                          rider=_scatter_rider([big_grads[0]], [w_axes[0]], rows=(0, cut)))
    (grad_x, g_abn), (got_bot,) = _rms_bwd(
        dn0, h0, ab_norm, dh1, "rms_bwd_ab", rider=_scatter_rider([big_grads[0]], [w_axes[0]], rows=(cut, d)))
    own0 = own_slab(0)
    sums[0] = jnp.concatenate([_sum_partials(own0[:cut], got_top, "sum_partials0_top"),
                               _sum_partials(own0[cut:], got_bot, "sum_partials0_bottom")], axis=0)
    sib = list(_run_rider(_swap_rider([sums[0]]), "swap_sibling_w_in")) + sib_late

    upd = [_adamw(w_shards[t], mm[0], vv[0], [sums[t], sib[t]], f"adamw{t}")
           for t, (mm, vv) in enumerate(((m_ab_w_in, v_ab_w_in), (m_ab_w_out, v_ab_w_out),
                                         (m_sc_w_in, v_sc_w_in), (m_sc_w_out, v_sc_w_out)))]
    upd_up = _adamw_layers(mlp_w_up, m_mlp_w_up, v_mlp_w_up, [[sums[4], sib[4]], [sums[5], sib[5]]], "adamw_up")
    upd_dn = _adamw_layers(mlp_w_down, m_mlp_w_down, v_mlp_w_down, [[sums[6], sib[6]], [sums[7], sib[7]]],
                           "adamw_down")

    full_shapes = [(_NUM_BUCKETS, rel_bias.shape[1]), (1, d), (_CONV_K, c), (1, c), (1, c), (1, c), (1, d),
                   (3, d), (2, d), (d,), (1, 1)]
    small_grads = [g_bias, g_abn, g_cw32[:_CONV_K], conv_stats[0:1], conv_stats[1:2], conv_stats[2:3], g_scn,
                   g_sccw8[:3], jnp.concatenate([g_mn0, g_mn1], axis=0), g_final.reshape(d), loss_part[0:1, 0:1]]
    tot = _unpack(_sum_all_devices(_pack(small_grads, _pack_rows(full_shapes)), "sum_small"), full_shapes)
    loss = tot.pop()[0, 0]
    for idx, sh in ((2, cw_sh), (6, scn_sh), (7, scw_sh)):
        width = sh.shape[1]
        tot[idx] = lax.dynamic_slice_in_dim(tot[idx], chip * width, width, axis=1)
    sm_w = [rel_bias, ab_norm, cw_sh, ab_conv_b, ab_ln_g, ab_ln_b, scn_sh, scw_sh, mlp_norm, final_norm]
    sm_m = [m_rel_bias, m_ab_norm, m_ab_conv_w[0], m_ab_conv_b, m_ab_ln_g, m_ab_ln_b, m_sc_norm, m_sc_conv_w[0],
            m_mlp_norm, m_final_norm]
    sm_v = [v_rel_bias, v_ab_norm, v_ab_conv_w[0], v_ab_conv_b, v_ab_ln_g, v_ab_ln_b, v_sc_norm, v_sc_conv_w[0],
            v_mlp_norm, v_final_norm]
    sh_shapes = [tuple(t.shape) for t in tot]
    rows = _pack_rows(sh_shapes)
    sm_upd = _adamw(_pack(sm_w, rows), _pack(sm_m, rows), _pack(sm_v, rows), [_pack(tot, rows)], "adamw_small")
    sm_g, sm_d, sm_nm, sm_nv = [_unpack(buf, sh_shapes) for buf in sm_upd]

    def assemble(kind, sm):
        big = [u[kind] for u in upd]
        return [sm[0], sm[1], big[0][None], sm[2][None], sm[3], sm[4], sm[5], big[1][None], sm[6], big[2][None],
                sm[7][None], big[3][None], sm[8], upd_up[kind], upd_dn[kind], sm[9]]

    res = [loss, grad_x[None]]
    for kind, sm in enumerate((sm_g, sm_d, sm_nm, sm_nv)):
        res += assemble(kind, sm)
    return tuple(res)
```

```python
import functools
import math

import jax
import jax.numpy as jnp
from jax import lax
from jax.experimental import pallas as pl
from jax.experimental.pallas import tpu as pltpu

F32 = jnp.float32
BF16 = jnp.bfloat16
MESH = pl.DeviceIdType.MESH

_GROUPS = ((128, 1), (512, 4), (2048, 16))
_STEPS = 128
_HEAD_DIM = 64
_GROUP_COLS = 512
_NUM_BUCKETS = 32
_MAX_DISTANCE = 2048
_CONV_K = 31
_HALO = 32
_SC_HALO = 16
_RMS_EPS = 1e-6
_LN_EPS = 1e-5
_NEG = -1e30
_LANES = 128
_VMEM_LIMIT = 56 * 1024 * 1024

_LR, _B1, _B2, _EPS, _WD, _STEP = 0.001, 0.9, 0.999, 1e-08, 0.01, 10


class _Rider:
    def __init__(self, ins, out_shapes, scratch, start, finish):
        self.ins, self.out_shapes, self.scratch = list(ins), list(out_shapes), list(scratch)
        self.start, self.finish = start, finish


def _pcall(body, *, name, out_shape, in_specs, out_specs, grid=None, scratch=(), aliases=None, rider=None):
    kw = {} if grid is None else {"grid": grid}
    cparams = pltpu.CompilerParams(vmem_limit_bytes=_VMEM_LIMIT)
    if rider is None:
        return pl.pallas_call(
            body, name=name, out_shape=out_shape, in_specs=in_specs, out_specs=out_specs,
            scratch_shapes=list(scratch), input_output_aliases=aliases or {},
            compiler_params=cparams, **kw)
    single = not isinstance(out_specs, (list, tuple))
    ospecs = [out_specs] if single else list(out_specs)
    oshapes = [out_shape] if single else list(out_shape)
    nin, nout, nscr = len(in_specs), len(ospecs), len(scratch)
    rin, rout = len(rider.ins), len(rider.out_shapes)

    def wrapped(*refs):
        h_in, r_in = refs[:nin], refs[nin:nin + rin]
        p = nin + rin
        h_out, r_out = refs[p:p + nout], refs[p + nout:p + nout + rout]
        p += nout + rout
        h_scr, r_scr = refs[p:p + nscr], refs[p + nscr:]
        ids = [pl.program_id(a) for a in range(len(grid))]
        first = functools.reduce(jnp.logical_and, [i == 0 for i in ids])
        last = functools.reduce(jnp.logical_and, [i == g - 1 for i, g in zip(ids, grid)])

        @pl.when(first)
        def _():
            rider.start(r_in, r_out, r_scr)

        body(*h_in, *h_out, *h_scr)

        @pl.when(last)
        def _():
            rider.finish(r_in, r_out, r_scr)

    call = pl.pallas_call(
        wrapped, name=name, out_shape=oshapes + rider.out_shapes,
        in_specs=list(in_specs) + [_ANY] * rin, out_specs=ospecs + [_ANY] * rout,
        scratch_shapes=list(scratch) + rider.scratch, input_output_aliases=aliases or {},
        compiler_params=cparams, **kw)

    def run(*operands):
        res = call(*operands, *rider.ins)
        host = res[0] if single else list(res[:nout])
        return host, list(res[nout:])

    return run


def _sig(x):
    return 1.0 / (1.0 + jnp.exp(-x))


_ANY = pl.BlockSpec(memory_space=pl.ANY)


def _lanes_of(ref):
    parts = [ref[p] for p in range(ref.shape[0])]
    return parts[0] if len(parts) == 1 else jnp.concatenate(parts, axis=1)


def _mm(a, b, mode, *, m, n, k, tm, tn, tk, out_dtype, name, epi=None, extras=(), b_off=(0, 0), rider=None,
        split="", vecs=(), a_pro=None, row_sum=False):
    nk = k // tk
    assert m % tm == 0 and n % tn == 0 and k % tk == 0
    o0, o1 = b_off
    if mode == "nn":
        a_spec = pl.BlockSpec((tm, tk), lambda i, j, kk: (i, kk))
        b_spec = pl.BlockSpec((tk, tn), lambda i, j, kk: (kk + o0, j + o1))
        dn = (((1,), (0,)), ((), ()))
    elif mode == "nt":
        a_spec = pl.BlockSpec((tm, tk), lambda i, j, kk: (i, kk))
        if "a" in split:
            a_spec = pl.BlockSpec((tk // _LANES, tm, _LANES), lambda i, j, kk: (kk, i, 0))
        b_spec = pl.BlockSpec((tn, tk), lambda i, j, kk: (j + o0, kk + o1))
        dn = (((1,), (1,)), ((), ()))
    else:
        a_spec = pl.BlockSpec((tk, tm), lambda i, j, kk: (kk, i))
        b_spec = pl.BlockSpec((tk, tn), lambda i, j, kk: (kk + o0, j + o1))
        if "b" in split:
            b_spec = pl.BlockSpec((tn // _LANES, tk, _LANES), lambda i, j, kk: (j, kk, 0))
        dn = (((0,), (0,)), ((), ()))
    o_spec = pl.BlockSpec((tm, tn), lambda i, j, kk: (i, j))
    e_spec = o_spec
    if "o" in split:
        o_spec = pl.BlockSpec((tn // _LANES, tm, _LANES), lambda i, j, kk: (j, i, 0))
    v_spec = pl.BlockSpec((1, tn), lambda i, j, kk: (0, j))
    ne = len(extras) + len(vecs)
    multi = isinstance(out_dtype, tuple)
    dts = out_dtype if multi else (out_dtype,)
    no = len(dts)
    nr = 1 if row_sum else 0
    assert not row_sum or tn == n

    def body(a_ref, b_ref, *rest):
        ex, o_refs = rest[:ne], rest[ne:ne + no]
        av = _lanes_of(a_ref) if "a" in split else a_ref[...]
        bv = _lanes_of(b_ref) if "b" in split else b_ref[...]
        if av.dtype != BF16:
            av = av.astype(BF16)
        if bv.dtype != BF16:
            bv = bv.astype(BF16)
        if a_pro is not None:
            av = a_pro(av)
        p = lax.dot_general(av, bv, dn, preferred_element_type=F32)

        def fin(x):
            if epi is not None:
                x = epi(x, *[e[...] for e in ex])
            if row_sum:
                row, x = x[-1], (x[:-1] if multi else x[0])
                row_ref = rest[ne + no]

                @pl.when(pl.program_id(0) == 0)
                def _():
                    row_ref[...] = row

                @pl.when(pl.program_id(0) > 0)
                def _():
                    row_ref[...] += row

            for o_ref, val, dt in zip(o_refs, x if multi else (x,), dts):
                if "o" in split:
                    for p in range(tn // _LANES):
                        o_ref[p] = val[:, p * _LANES:(p + 1) * _LANES].astype(dt)
                else:
                    o_ref[...] = val.astype(dt)

        if nk == 1:
            fin(p)
        else:
            acc = rest[ne + no + nr]
            kk = pl.program_id(2)

            @pl.when(kk == 0)
            def _():
                acc[...] = p

            @pl.when(kk > 0)
            def _():
                acc[...] += p

            @pl.when(kk == nk - 1)
            def _():
                fin(acc[...])

    oshape = (n // _LANES, m, _LANES) if "o" in split else (m, n)
    shapes = [jax.ShapeDtypeStruct(oshape, dt) for dt in dts]
    ospecs = [o_spec] * no
    if row_sum:
        shapes.append(jax.ShapeDtypeStruct((1, n), F32))
        ospecs.append(v_spec)
    lone = not multi and not row_sum
    return _pcall(
        body, name=name, grid=(m // tm, n // tn, nk),
        in_specs=[a_spec, b_spec] + [e_spec] * len(extras) + [v_spec] * len(vecs),
        out_specs=ospecs[0] if lone else ospecs, out_shape=shapes[0] if lone else shapes,
        scratch=[pltpu.VMEM((tm, tn), F32)] if nk > 1 else [], rider=rider,
    )(a, b, *extras, *vecs)


def _epi_add(x, r):
    return x + r


def _epi_relu(x):
    return jnp.maximum(x, 0.0)


def _square(x):
    return x * x


def _epi_relu_sq_bwd(da, zr):
    return da * (2.0 * zr.astype(F32))


def _epi_add_rms(x, r, g):
    h = x + r
    return h, h * lax.rsqrt(jnp.mean(h * h, axis=-1, keepdims=True) + _RMS_EPS) * g


def _epi_rms_bwd(dn, h, dh_in, g):
    dx, dg = _rms_bwd_math(dn, h, g)
    dh = dh_in + dx
    return dh, dh, dg


_ROW_T = 512


def _rms_fwd(h, g, name, rider=None):
    s, d = h.shape

    def body(h_ref, g_ref, o_ref):
        x = h_ref[...]
        r = lax.rsqrt(jnp.mean(x * x, axis=-1, keepdims=True) + _RMS_EPS)
        o_ref[...] = (x * r * g_ref[...]).astype(BF16)

    row = pl.BlockSpec((_ROW_T, d), lambda i: (i, 0))
    vec = pl.BlockSpec((1, d), lambda i: (0, 0))
    return _pcall(body, name=name, grid=(s // _ROW_T,), in_specs=[row, vec], out_specs=row,
                  out_shape=jax.ShapeDtypeStruct((s, d), BF16), rider=rider)(h, g)


def _rms_bwd_math(dn, x, g):
    r = lax.rsqrt(jnp.mean(x * x, axis=-1, keepdims=True) + _RMS_EPS)
    xhat = x * r
    dg = jnp.sum(dn * xhat, axis=0, keepdims=True)
    t = dn * g
    dx = r * (t - xhat * jnp.mean(t * xhat, axis=-1, keepdims=True))
    return dx, dg


def _rms_bwd(dn, h, g, dh_in, name):
    s, d = h.shape

    def body(dn_ref, h_ref, g_ref, dhi_ref, dh_ref, dg_ref):
        dx, dg = _rms_bwd_math(dn_ref[...], h_ref[...], g_ref[...])
        dh_ref[...] = dhi_ref[...] + dx

        @pl.when(pl.program_id(0) == 0)
        def _():
            dg_ref[...] = jnp.zeros_like(dg_ref)

        dg_ref[...] += dg

    row = pl.BlockSpec((_ROW_T, d), lambda i: (i, 0))
    vec = pl.BlockSpec((1, d), lambda i: (0, 0))
    return _pcall(
        body, name=name, grid=(s // _ROW_T,), in_specs=[row, row, vec, row], out_specs=[row, vec],
        out_shape=[jax.ShapeDtypeStruct((s, d), F32), jax.ShapeDtypeStruct((1, d), F32)])(dn, h, g, dh_in)


def _loss_head(h, tgt, g):
    s, d = h.shape

    def body(h_ref, t_ref, g_ref, dh_ref, dhb_ref, dg_ref, loss_ref):
        x, gv = h_ref[...], g_ref[...]
        r = lax.rsqrt(jnp.mean(x * x, axis=-1, keepdims=True) + _RMS_EPS)
        err = x * r * gv - t_ref[...]
        part = 0.5 * jnp.sum(jnp.mean(err * err, axis=-1, keepdims=True))
        dx, dg = _rms_bwd_math(err * (1.0 / d), x, gv)
        dh_ref[...] = dx
        dhb_ref[...] = dx.astype(BF16)

        @pl.when(pl.program_id(0) == 0)
        def _():
            dg_ref[...] = jnp.zeros_like(dg_ref)
            loss_ref[...] = jnp.zeros_like(loss_ref)

        dg_ref[...] += dg
        loss_ref[...] += jnp.full(loss_ref.shape, part, F32)

    row = pl.BlockSpec((_ROW_T, d), lambda i: (i, 0))
    vec = pl.BlockSpec((1, d), lambda i: (0, 0))
    one = pl.BlockSpec((1, _LANES), lambda i: (0, 0))
    return _pcall(
        body, name="loss_head", grid=(s // _ROW_T,), in_specs=[row, row, vec], out_specs=[row, row, vec, one],
        out_shape=[jax.ShapeDtypeStruct((s, d), F32), jax.ShapeDtypeStruct((s, d), BF16),
                   jax.ShapeDtypeStruct((1, d), F32), jax.ShapeDtypeStruct((1, _LANES), F32)])(h, tgt, g)


_CONV_T = 256
_CONV_RC = 64


def _conv_a_specs(s):
    c = _GROUP_COLS
    hb = _CONV_T // _HALO
    val = pl.BlockSpec((_CONV_T, c), lambda i: (i, 0))
    gate = pl.BlockSpec((_CONV_T, c), lambda i: (i, 1))
    hval = pl.BlockSpec((_HALO, c), lambda i: (jnp.maximum(i * hb - 1, 0), 0))
    hgate = pl.BlockSpec((_HALO, c), lambda i: (jnp.maximum(i * hb - 1, 0), 1))
    return val, gate, hval, hgate


def _fill_glu(val_ref, gate_ref, hval_ref, hgate_ref, hs_ref):
    i = pl.program_id(0)
    hs_ref[pl.ds(_HALO, _CONV_T), :] = val_ref[...].astype(F32) * _sig(gate_ref[...].astype(F32))
    halo = hval_ref[...].astype(F32) * _sig(hgate_ref[...].astype(F32))
    hs_ref[pl.ds(0, _HALO), :] = jnp.where(i > 0, halo, 0.0)


_SHIFT_ROWS = _CONV_T + _HALO - 8


def _fill_shifts(src_ref, sh_ref):
    for b in range(1, 8):
        sh_ref[b - 1] = src_ref[pl.ds(b, _SHIFT_ROWS), :]


def _tap_rows(src_ref, sh_ref, start, rows, lanes=slice(None)):
    b = start % 8
    if b == 0:
        return src_ref[pl.ds(start, rows), lanes]
    return sh_ref[b - 1, pl.ds(start - b, rows), lanes]


def _conv_rows(hs_ref, sh_ref, w_ref, r0, rows):
    off = _HALO - (_CONV_K - 1)
    acc = jnp.zeros((rows, _GROUP_COLS), F32)
    for kk in range(_CONV_K):
        acc = acc + w_ref[kk:kk + 1, :] * _tap_rows(hs_ref, sh_ref, r0 + off + kk, rows)
    return acc


def _ln_fwd(ca, g, b):
    mu = jnp.mean(ca, axis=-1, keepdims=True)
    xc = ca - mu
    rstd = lax.rsqrt(jnp.mean(xc * xc, axis=-1, keepdims=True) + _LN_EPS)
    xhat = xc * rstd
    return xhat, rstd, xhat * g + b


def _conv_a_fwd(uc, w, cb, lg, lb, rider=None):
    s = uc.shape[0]
    c = _GROUP_COLS

    def body(val_ref, gate_ref, hval_ref, hgate_ref, w_ref, cb_ref, lg_ref, lb_ref, o_ref, ca_ref, hs_ref, sh_ref):
        _fill_glu(val_ref, gate_ref, hval_ref, hgate_ref, hs_ref)
        _fill_shifts(hs_ref, sh_ref)
        for rc in range(_CONV_T // _CONV_RC):
            r0 = rc * _CONV_RC
            ca = _conv_rows(hs_ref, sh_ref, w_ref, r0, _CONV_RC) + cb_ref[...]
            ca_ref[pl.ds(r0, _CONV_RC), :] = ca
            _, _, ln = _ln_fwd(ca, lg_ref[...], lb_ref[...])
            o_ref[pl.ds(r0, _CONV_RC), :] = (ln * _sig(ln)).astype(BF16)

    val, gate, hval, hgate = _conv_a_specs(s)
    wspec = pl.BlockSpec((_CONV_K, c), lambda i: (0, 0))
    vec = pl.BlockSpec((1, c), lambda i: (0, 0))
    blk = pl.BlockSpec((_CONV_T, c), lambda i: (i, 0))
    return _pcall(
        body, name="conv_a_fwd", grid=(s // _CONV_T,),
        in_specs=[val, gate, hval, hgate, wspec, vec, vec, vec],
        out_specs=[blk, blk],
        out_shape=[jax.ShapeDtypeStruct((s, 2 * c), BF16), jax.ShapeDtypeStruct((s, c), F32)],
        scratch=[pltpu.VMEM((_CONV_T + _HALO, c), F32), pltpu.VMEM((7, _SHIFT_ROWS, c), F32)],
        rider=rider)(uc, uc, uc, uc, w, cb, lg, lb)


def _conv_a_bwd_ln(ca_all, dcat, lg, lb, rider=None):
    s = ca_all.shape[0]
    c = _GROUP_COLS

    def body(ca_ref, dy_ref, lg_ref, lb_ref, dca_ref, st_ref):
        @pl.when(pl.program_id(0) == 0)
        def _():
            st_ref[...] = jnp.zeros_like(st_ref)

        for rc in range(_CONV_T // _CONV_RC):
            r0 = rc * _CONV_RC
            ca = ca_ref[pl.ds(r0, _CONV_RC), :]
            xhat, rstd, ln = _ln_fwd(ca, lg_ref[...], lb_ref[...])
            sg = _sig(ln)
            dln = dy_ref[pl.ds(r0, _CONV_RC), :] * (sg * (1.0 + ln * (1.0 - sg)))
            dxh = dln * lg_ref[...]
            dca = rstd * (dxh - jnp.mean(dxh, axis=-1, keepdims=True)
                          - xhat * jnp.mean(dxh * xhat, axis=-1, keepdims=True))
            dca_ref[pl.ds(r0, _CONV_RC), :] = dca
            st_ref[0:1, :] += jnp.sum(dca, axis=0, keepdims=True)
            st_ref[1:2, :] += jnp.sum(dln * xhat, axis=0, keepdims=True)
            st_ref[2:3, :] += jnp.sum(dln, axis=0, keepdims=True)

    blk = pl.BlockSpec((_CONV_T, c), lambda i: (i, 0))
    vec = pl.BlockSpec((1, c), lambda i: (0, 0))
    st = pl.BlockSpec((8, c), lambda i: (0, 0))
    return _pcall(
        body, name="conv_a_bwd_ln", grid=(s // _CONV_T,),
        in_specs=[blk, blk, vec, vec], out_specs=[blk, st],
        out_shape=[jax.ShapeDtypeStruct((s, c), F32), jax.ShapeDtypeStruct((8, c), F32)],
        rider=rider)(ca_all, dcat, lg, lb)


def _conv_a_bwd_conv(uc, dca, w, rider=None):
    s = uc.shape[0]
    c = _GROUP_COLS
    nblk = s // _CONV_T
    hb = _CONV_T // _HALO
    off = _HALO - (_CONV_K - 1)

    def body(val_ref, gate_ref, hval_ref, hgate_ref, d_ref, dn_ref, w_ref, du_ref, dw_ref, hs_ref, ds_ref,
             hsh_ref, dsh_ref, dwa_ref):
        i = pl.program_id(0)
        _fill_glu(val_ref, gate_ref, hval_ref, hgate_ref, hs_ref)
        ds_ref[pl.ds(0, _CONV_T), :] = d_ref[...]
        ds_ref[pl.ds(_CONV_T, _HALO), :] = jnp.where(i < nblk - 1, dn_ref[...], 0.0)
        _fill_shifts(hs_ref, hsh_ref)
        _fill_shifts(ds_ref, dsh_ref)

        @pl.when(i == 0)
        def _():
            dwa_ref[...] = jnp.zeros_like(dwa_ref)

        rows = 32
        for r0 in range(0, _CONV_T, rows):
            dcur = ds_ref[pl.ds(r0, rows), :]
            dh = jnp.zeros((rows, c), F32)
            for kk in range(_CONV_K):
                dh = dh + w_ref[kk:kk + 1, :] * _tap_rows(ds_ref, dsh_ref, r0 + _CONV_K - 1 - kk, rows)
                prod = dcur * _tap_rows(hs_ref, hsh_ref, r0 + off + kk, rows)
                dwa_ref[pl.ds(8 * kk, 8), :] += sum(prod[t:t + 8] for t in range(0, rows, 8))
            v = val_ref[pl.ds(r0, rows), :].astype(F32)
            sg = _sig(gate_ref[pl.ds(r0, rows), :].astype(F32))
            du_ref[pl.ds(r0, rows), pl.ds(0, c)] = (dh * sg).astype(BF16)
            du_ref[pl.ds(r0, rows), pl.ds(c, c)] = (dh * v * sg * (1.0 - sg)).astype(BF16)

        @pl.when(i == nblk - 1)
        def _():
            dw_ref[...] = jnp.zeros_like(dw_ref)
            for kk in range(_CONV_K):
                dw_ref[kk:kk + 1, :] = jnp.sum(dwa_ref[pl.ds(8 * kk, 8), :], axis=0, keepdims=True)

    val, gate, hval, hgate = _conv_a_specs(s)
    blk = pl.BlockSpec((_CONV_T, c), lambda i: (i, 0))
    nxt = pl.BlockSpec((_HALO, c), lambda i: (jnp.minimum((i + 1) * hb, s // _HALO - 1), 0))
    wspec = pl.BlockSpec((_CONV_K, c), lambda i: (0, 0))
    return _pcall(
        body, name="conv_a_bwd_conv", grid=(nblk,),
        in_specs=[val, gate, hval, hgate, blk, nxt, wspec],
        out_specs=[pl.BlockSpec((_CONV_T, 2 * c), lambda i: (i, 0)), pl.BlockSpec((_HALO, c), lambda i: (0, 0))],
        out_shape=[jax.ShapeDtypeStruct((s, 2 * c), BF16), jax.ShapeDtypeStruct((_HALO, c), F32)],
        scratch=[pltpu.VMEM((_CONV_T + _HALO, c), F32), pltpu.VMEM((_CONV_T + _HALO, c), F32),
                 pltpu.VMEM((7, _SHIFT_ROWS, c), F32), pltpu.VMEM((7, _SHIFT_ROWS, c), F32),
                 pltpu.VMEM((8 * _HALO, c), F32)],
        rider=rider,
    )(uc, uc, uc, uc, dca, dca, w)


_SC_T = 256
_SC_RC = 32
_SC_LC = 512


def _sc_chunks(d):
    return [(pl.ds(r0, _SC_RC), pl.ds(l0, _SC_LC)) for r0 in range(0, _SC_T, _SC_RC) for l0 in range(0, d, _SC_LC)]


def _short_conv_fwd(u2, w):
    s, d3 = u2.shape
    d = d3 // 3
    hb = _SC_T // _SC_HALO

    def body(b_ref, c_ref, v_ref, hc_ref, hv_ref, w_ref, o_ref, cs_ref):
        i = pl.program_id(0)
        cs_ref[pl.ds(0, _SC_HALO), :] = jnp.where(i > 0, hc_ref[...].astype(F32) * hv_ref[...].astype(F32), 0.0)
        for rows, lanes in _sc_chunks(d):
            cs_ref[pl.ds(_SC_HALO + rows.start, _SC_RC), lanes] = (
                c_ref[rows, lanes].astype(F32) * v_ref[rows, lanes].astype(F32))
        for rows, lanes in _sc_chunks(d):
            taps = [cs_ref[pl.ds(_SC_HALO - 2 + kk + rows.start, _SC_RC), lanes] for kk in range(3)]
            conv = w_ref[0:1, lanes] * taps[0] + w_ref[1:2, lanes] * taps[1] + w_ref[2:3, lanes] * taps[2]
            o_ref[rows, lanes] = (b_ref[rows, lanes].astype(F32) * conv).astype(BF16)

    def col(j):
        return pl.BlockSpec((_SC_T, d), lambda i: (i, j))

    def halo(j):
        return pl.BlockSpec((_SC_HALO, d), lambda i: (jnp.maximum(i * hb - 1, 0), j))

    return _pcall(
        body, name="short_conv_fwd", grid=(s // _SC_T,),
        in_specs=[col(0), col(1), col(2), halo(1), halo(2), pl.BlockSpec((3, d), lambda i: (0, 0))],
        out_specs=pl.BlockSpec((_SC_T, d), lambda i: (i, 0)),
        out_shape=jax.ShapeDtypeStruct((s, d), BF16),
        scratch=[pltpu.VMEM((_SC_T + _SC_HALO, d), F32)])(u2, u2, u2, u2, u2, w)


def _short_conv_bwd(u2, dsc, w, rider=None):
    s, d3 = u2.shape
    d = d3 // 3
    hb = _SC_T // _SC_HALO
    nblk = s // _SC_T

    def body(b_ref, c_ref, v_ref, hc_ref, hv_ref, nb_ref, d_ref, nd_ref, w_ref, du_ref, dw_ref, cs_ref, ds_ref):
        i = pl.program_id(0)
        cs_ref[pl.ds(0, _SC_HALO), :] = jnp.where(i > 0, hc_ref[...].astype(F32) * hv_ref[...].astype(F32), 0.0)
        ds_ref[pl.ds(_SC_T, _SC_HALO), :] = jnp.where(i < nblk - 1, nd_ref[...] * nb_ref[...].astype(F32), 0.0)
        for rows, lanes in _sc_chunks(d):
            cs_ref[pl.ds(_SC_HALO + rows.start, _SC_RC), lanes] = (
                c_ref[rows, lanes].astype(F32) * v_ref[rows, lanes].astype(F32))
            ds_ref[rows, lanes] = d_ref[rows, lanes] * b_ref[rows, lanes].astype(F32)

        @pl.when(i == 0)
        def _():
            dw_ref[...] = jnp.zeros_like(dw_ref)

        for l0 in range(0, d, _SC_LC):
            lanes = pl.ds(l0, _SC_LC)
            dw_acc = [jnp.zeros((8, _SC_LC), F32)] * 3
            for r0 in range(0, _SC_T, _SC_RC):
                rows = pl.ds(r0, _SC_RC)
                taps = [cs_ref[pl.ds(_SC_HALO - 2 + kk + r0, _SC_RC), lanes] for kk in range(3)]
                conv = w_ref[0:1, lanes] * taps[0] + w_ref[1:2, lanes] * taps[1] + w_ref[2:3, lanes] * taps[2]
                dconv = ds_ref[rows, lanes]
                dcv = (w_ref[2:3, lanes] * dconv + w_ref[1:2, lanes] * ds_ref[pl.ds(r0 + 1, _SC_RC), lanes]
                       + w_ref[0:1, lanes] * ds_ref[pl.ds(r0 + 2, _SC_RC), lanes])
                du_ref[rows, lanes] = (d_ref[rows, lanes] * conv).astype(BF16)
                du_ref[rows, pl.ds(d + l0, _SC_LC)] = (dcv * v_ref[rows, lanes].astype(F32)).astype(BF16)
                du_ref[rows, pl.ds(2 * d + l0, _SC_LC)] = (dcv * c_ref[rows, lanes].astype(F32)).astype(BF16)
                for kk in range(3):
                    prod = dconv * taps[kk]
                    dw_acc[kk] = dw_acc[kk] + sum(prod[t:t + 8] for t in range(0, _SC_RC, 8))
            for kk in range(3):
                dw_ref[kk:kk + 1, lanes] += jnp.sum(dw_acc[kk], axis=0, keepdims=True)

    def col(j):
        return pl.BlockSpec((_SC_T, d), lambda i: (i, j))

    def halo(j):
        return pl.BlockSpec((_SC_HALO, d), lambda i: (jnp.maximum(i * hb - 1, 0), j))

    def nxt(j):
        return pl.BlockSpec((_SC_HALO, d), lambda i: (jnp.minimum((i + 1) * hb, s // _SC_HALO - 1), j))

    return _pcall(
        body, name="short_conv_bwd", grid=(nblk,),
        in_specs=[col(0), col(1), col(2), halo(1), halo(2), nxt(0), col(0), nxt(0),
                  pl.BlockSpec((3, d), lambda i: (0, 0))],
        out_specs=[pl.BlockSpec((_SC_T, d3), lambda i: (i, 0)), pl.BlockSpec((8, d), lambda i: (0, 0))],
        out_shape=[jax.ShapeDtypeStruct((s, d3), BF16), jax.ShapeDtypeStruct((8, d), F32)],
        scratch=[pltpu.VMEM((_SC_T + _SC_HALO, d), F32), pltpu.VMEM((_SC_T + _SC_HALO, d), F32)],
        rider=rider,
    )(u2, u2, u2, u2, u2, u2, dsc, dsc, w)


def _bucket_maps():
    a_idx = jnp.arange(_STEPS)[:, None]
    c_idx = jnp.arange(2 * _STEPS)[None, :]
    mdist = jnp.clip(a_idx + _STEPS - c_idx, 0, _STEPS)
    max_exact = _NUM_BUCKETS // 2
    maps = []
    for _, dil in _GROUPS:
        nn = mdist * dil
        nf = jnp.maximum(nn, 1).astype(F32)
        large = max_exact + (jnp.log(nf / max_exact) / math.log(_MAX_DISTANCE / max_exact)
                             * (_NUM_BUCKETS - max_exact)).astype(jnp.int32)
        maps.append(jnp.where(nn < max_exact, nn, jnp.minimum(large, _NUM_BUCKETS - 1)).astype(jnp.int32))
    return jnp.stack(maps, axis=0)


def _bias_expand(rel_bias, buckets):
    nh = rel_bias.shape[1]

    def body(rb_ref, bk_ref, o_ref):
        h = pl.program_id(0)
        bk = bk_ref[0]
        acc = jnp.zeros(bk.shape, F32)
        for b in range(_NUM_BUCKETS):
            acc = jnp.where(bk == b, rb_ref[b, h], acc)
        a = lax.broadcasted_iota(jnp.int32, bk.shape, 0)
        c = lax.broadcasted_iota(jnp.int32, bk.shape, 1)
        mdist = a + _STEPS - c
        o_ref[0] = jnp.where((mdist >= 0) & (mdist <= _STEPS), acc, _NEG)

    return _pcall(
        body, name="bias_expand", grid=(nh,),
        in_specs=[pl.BlockSpec(memory_space=pltpu.SMEM),
                  pl.BlockSpec((1, _STEPS, 2 * _STEPS), lambda h: (h // 8, 0, 0))],
        out_specs=pl.BlockSpec((1, _STEPS, 2 * _STEPS), lambda h: (h, 0, 0)),
        out_shape=jax.ShapeDtypeStruct((nh, _STEPS, 2 * _STEPS), F32))(rel_bias, buckets)


def _bias_reduce(ds_all, buckets):
    nh = ds_all.shape[0]

    def body(ds_ref, bk_ref, o_ref):
        t, bk = ds_ref[0], bk_ref[0]
        rows = lax.broadcasted_iota(jnp.int32, (_NUM_BUCKETS, _LANES), 0)
        out = jnp.zeros((_NUM_BUCKETS, _LANES), F32)
        for b in range(_NUM_BUCKETS):
            out = jnp.where(rows == b, jnp.sum(jnp.where(bk == b, t, 0.0)), out)
        o_ref[0] = out

    blk = pl.BlockSpec((1, _STEPS, 2 * _STEPS), lambda h: (h, 0, 0))
    return _pcall(
        body, name="bias_reduce", grid=(nh,),
        in_specs=[blk, pl.BlockSpec((1, _STEPS, 2 * _STEPS), lambda h: (h // 8, 0, 0))],
        out_specs=pl.BlockSpec((1, _NUM_BUCKETS, _LANES), lambda h: (h, 0, 0)),
        out_shape=jax.ShapeDtypeStruct((nh, _NUM_BUCKETS, _LANES), F32))(ds_all, buckets)


def _sub_residues(dil):
    return 4 if dil % 16 == 0 else 1


def _strided_rows(ref, tmp_ref, p, r, dil):
    sub = _sub_residues(dil)
    if dil == 1:
        return [ref[p]]
    if sub == 1:
        return [ref[p, pl.ds(r, _STEPS, stride=dil), :]]
    tmp_ref[...] = ref[p, pl.ds(r, _STEPS * sub, stride=dil // sub), :]
    return [tmp_ref[pl.ds(q, _STEPS, stride=sub), :] for q in range(sub)]


def _store_strided(ref, tmp_ref, p, r, dil, vals):
    sub = _sub_residues(dil)
    if dil == 1:
        ref[p] = vals[0]
    elif sub == 1:
        ref[p, pl.ds(r, _STEPS, stride=dil), :] = vals[0]
    else:
        for q, val in enumerate(vals):
            tmp_ref[pl.ds(q, _STEPS, stride=sub), :] = val
        ref[p, pl.ds(r, _STEPS * sub, stride=dil // sub), :] = tmp_ref[...]


def _tmp_rows(dil, count):
    sub = _sub_residues(dil)
    return [pltpu.VMEM((_STEPS * sub, _LANES), F32)] * count if sub > 1 else []


def _head_masks():
    lane = lax.broadcasted_iota(jnp.int32, (1, _LANES), 1)
    return [lane < _HEAD_DIM, lane >= _HEAD_DIM]


def _stack_heads(x2, masks):
    return jnp.concatenate([jnp.where(masks[0], x2, 0), jnp.where(masks[1], x2, 0)], axis=0)


def _unstack_heads(y, masks):
    return jnp.where(masks[0], y[:_STEPS], y[_STEPS:])


def _scores(qs2, k2, b_ref, j, first):
    sc = lax.dot_general(qs2, k2, (((1,), (1,)), ((), ())), preferred_element_type=F32)
    sc = sc * (_HEAD_DIM ** -0.5) + jnp.concatenate([b_ref[2 * j], b_ref[2 * j + 1]], axis=0)
    col = lax.broadcasted_iota(jnp.int32, sc.shape, 1)
    return jnp.where(jnp.logical_and(first, col < _STEPS), _NEG, sc)


_PAIRS = _GROUP_COLS // _LANES


def _attn_fwd(uq, uk, uv, bias, g, dil, pp, rider=None):
    s = uq.shape[1]
    rb = _STEPS * dil
    nb = s // rb
    npb = _PAIRS // pp

    sub = _sub_residues(dil)

    def body(q_ref, kc_ref, kp_ref, vc_ref, vp_ref, b_ref, o_ref, l_ref, *tmp):
        tmp = tmp + (None,) * 7
        n, r = pl.program_id(1), pl.program_id(2)
        first = n == 0
        masks = _head_masks()
        for j in range(pp):
            qs = _strided_rows(q_ref, tmp[0], j, r, dil)
            kps, kcs = _strided_rows(kp_ref, tmp[1], j, r, dil), _strided_rows(kc_ref, tmp[2], j, r, dil)
            vps, vcs = _strided_rows(vp_ref, tmp[3], j, r, dil), _strided_rows(vc_ref, tmp[4], j, r, dil)
            o_res, l_res = [], []
            for q in range(sub):
                q2 = qs[q].astype(BF16)
                k2 = jnp.concatenate([kps[q], kcs[q]], axis=0).astype(BF16)
                v2 = jnp.concatenate([vps[q], vcs[q]], axis=0).astype(BF16)
                sc = _scores(_stack_heads(q2, masks), k2, b_ref, j, first)
                mx = jnp.max(sc, axis=-1, keepdims=True)
                p = jnp.exp(sc - mx)
                den = jnp.sum(p, axis=-1, keepdims=True)
                o2 = jnp.dot(p.astype(BF16), v2, preferred_element_type=F32) / den
                o_res.append(_unstack_heads(o2, masks))
                l_res.append(_unstack_heads(jnp.broadcast_to(mx + jnp.log(den), o2.shape), masks))
            _store_strided(o_ref, tmp[5], j, r, dil, o_res)
            _store_strided(l_ref, tmp[6], j, r, dil, l_res)

    cur = pl.BlockSpec((pp, rb, _LANES), lambda hb, n, r: (g * npb + hb, n, 0))
    prev = pl.BlockSpec((pp, rb, _LANES), lambda hb, n, r: (g * npb + hb, jnp.maximum(n - 1, 0), 0))
    bspec = pl.BlockSpec((2 * pp, _STEPS, 2 * _STEPS), lambda hb, n, r: (g * npb + hb, 0, 0))
    ospec = pl.BlockSpec((pp, rb, _LANES), lambda hb, n, r: (hb, n, 0))
    sh = jax.ShapeDtypeStruct((_PAIRS, s, _LANES), F32)
    return _pcall(
        body, name=f"attn_fwd_g{g}", grid=(npb, nb, dil // sub),
        in_specs=[cur, cur, prev, cur, prev, bspec], out_specs=[ospec, ospec], out_shape=[sh, sh],
        scratch=_tmp_rows(dil, 7), rider=rider,
    )(uq, uk, uk, uv, uv, bias)


def _attn_merge(outs, lses, cat):
    s = outs[0].shape[1]
    c = _GROUP_COLS

    def body(o0, o1, o2, l0, l1, l2, cat_in, cat_ref, lse_ref):
        del cat_in
        a0, a1, a2 = l0[...], l1[...], l2[...]
        mx = jnp.maximum(jnp.maximum(a0, a1), a2)
        w0, w1, w2 = jnp.exp(a0 - mx), jnp.exp(a1 - mx), jnp.exp(a2 - mx)
        den = w0 + w1 + w2
        y = ((w0 * o0[...] + w1 * o1[...] + w2 * o2[...]) / den).astype(BF16)
        for p in range(_PAIRS):
            cat_ref[:, p * _LANES:(p + 1) * _LANES] = y[p]
        lse_ref[...] = mx + jnp.log(den)

    blk = pl.BlockSpec((_PAIRS, _ROW_T, _LANES), lambda i: (0, i, 0))
    return _pcall(
        body, name="attn_merge", grid=(s // _ROW_T,),
        in_specs=[blk] * 6 + [_ANY],
        out_specs=[pl.BlockSpec((_ROW_T, c), lambda i: (i, 1)), blk],
        out_shape=[jax.ShapeDtypeStruct(cat.shape, BF16), jax.ShapeDtypeStruct((_PAIRS, s, _LANES), F32)],
        aliases={6: 0})(*outs, *lses, cat)


def _attn_delta(dcat, cat):
    s = dcat.shape[0]
    c = _GROUP_COLS
    seg = (jnp.arange(c)[:, None] // _HEAD_DIM == jnp.arange(c)[None, :] // _HEAD_DIM).astype(BF16)

    def body(dy_ref, y_ref, seg_ref, dl_ref, dys_ref):
        dy = dy_ref[...]
        prod = dy * y_ref[...].astype(F32)
        hi = prod.astype(BF16)
        lo = (prod - hi.astype(F32)).astype(BF16)
        dl = (jnp.dot(hi, seg_ref[...], preferred_element_type=F32)
              + jnp.dot(lo, seg_ref[...], preferred_element_type=F32))
        for p in range(_PAIRS):
            dl_ref[p] = dl[:, p * _LANES:(p + 1) * _LANES]
            dys_ref[p] = dy[:, p * _LANES:(p + 1) * _LANES]

    right = pl.BlockSpec((_ROW_T, c), lambda i: (i, 1))
    blk = pl.BlockSpec((_PAIRS, _ROW_T, _LANES), lambda i: (0, i, 0))
    sh = jax.ShapeDtypeStruct((_PAIRS, s, _LANES), F32)
    return _pcall(
        body, name="attn_delta", grid=(s // _ROW_T,),
        in_specs=[right, right, pl.BlockSpec((c, c), lambda i: (0, 0))],
        out_specs=[blk, blk], out_shape=[sh, sh])(dcat, cat, seg)


def _attn_bwd(uq, uk, uv, dys, lse, delta, bias, prev_grads, g, dil, pp, rider=None):
    s = uq.shape[1]
    rb = _STEPS * dil
    nb = s // rb
    npb = _PAIRS // pp
    scale = _HEAD_DIM ** -0.5

    sub = _sub_residues(dil)

    def body(q_ref, kc_ref, kp_ref, vc_ref, vp_ref, dy_ref, l_ref, dl_ref, b_ref, *rest):
        rest = rest[len(prev_grads):]
        dqkv_ref, dsa_ref, dqc_ref, dkc_ref, dvc_ref = rest[:5]
        dq_ref, dk_ref, dv_ref = dqkv_ref.at[0], dqkv_ref.at[1], dqkv_ref.at[2]
        tmp = rest[5:] + (None,) * 11
        n, r = pl.program_id(1), pl.program_id(2)

        def carry_slot(j, q):
            return ((r + (dil // sub) * q) * pp + j) if sub > 1 else r * pp + j

        @pl.when(jnp.logical_and(n == 0, r == 0))
        def _():
            dsa_ref[...] = jnp.zeros_like(dsa_ref)

        @pl.when(n == 0)
        def _():
            for j in range(pp):
                for q in range(sub):
                    for carry in (dqc_ref, dkc_ref, dvc_ref):
                        carry[carry_slot(j, q)] = jnp.zeros((_STEPS, _LANES), F32)

        @pl.when(n < nb)
        def _():
            first = n == 0
            masks = _head_masks()
            for j in range(pp):
                qs = _strided_rows(q_ref, tmp[0], j, r, dil)
                kps, kcs = _strided_rows(kp_ref, tmp[1], j, r, dil), _strided_rows(kc_ref, tmp[2], j, r, dil)
                vps, vcs = _strided_rows(vp_ref, tmp[3], j, r, dil), _strided_rows(vc_ref, tmp[4], j, r, dil)
                dys_ = _strided_rows(dy_ref, tmp[5], j, r, dil)
                lses = _strided_rows(l_ref, tmp[6], j, r, dil)
                dls = _strided_rows(dl_ref, tmp[7], j, r, dil)
                ds_sum = [jnp.zeros((_STEPS, 2 * _STEPS), F32)] * 2
                dq_res, dk_res, dv_res = [], [], []
                for q in range(sub):
                    q2 = qs[q].astype(BF16)
                    k2 = jnp.concatenate([kps[q], kcs[q]], axis=0).astype(BF16)
                    v2 = jnp.concatenate([vps[q], vcs[q]], axis=0).astype(BF16)
                    dy2 = dys_[q].astype(BF16)
                    qs2, dys2 = _stack_heads(q2, masks), _stack_heads(dy2, masks)
                    per_row = lambda st: jnp.concatenate([st[:, 0:1], st[:, _HEAD_DIM:_HEAD_DIM + 1]], axis=0)
                    sc = _scores(qs2, k2, b_ref, j, first)
                    p = jnp.exp(sc - per_row(lses[q]))
                    dp = lax.dot_general(dys2, v2, (((1,), (1,)), ((), ())), preferred_element_type=F32)
                    ds = p * (dp - per_row(dls[q]))
                    ds_sum[0] = ds_sum[0] + ds[:_STEPS]
                    ds_sum[1] = ds_sum[1] + ds[_STEPS:]
                    dsb = ds.astype(BF16)
                    dq_p = _unstack_heads(jnp.dot(dsb, k2, preferred_element_type=F32), masks)
                    tdn = (((0,), (0,)), ((), ()))
                    dk_p = lax.dot_general(dsb, qs2, tdn, preferred_element_type=F32) * scale
                    dv_p = lax.dot_general(p.astype(BF16), dys2, tdn, preferred_element_type=F32)
                    slot = carry_slot(j, q)
                    dq_res.append(dqc_ref[slot])
                    dk_res.append(dkc_ref[slot] + dk_p[:_STEPS])
                    dv_res.append(dvc_ref[slot] + dv_p[:_STEPS])
                    dqc_ref[slot] = dq_p * scale
                    dkc_ref[slot] = dk_p[_STEPS:]
                    dvc_ref[slot] = dv_p[_STEPS:]
                for hh in range(2):
                    dsa_ref[2 * j + hh] += ds_sum[hh]
                _store_strided(dq_ref, tmp[8], j, r, dil, dq_res)
                _store_strided(dk_ref, tmp[9], j, r, dil, dk_res)
                _store_strided(dv_ref, tmp[10], j, r, dil, dv_res)

        @pl.when(n == nb)
        def _():
            for j in range(pp):
                for ref, carry, t in ((dq_ref, dqc_ref, 8), (dk_ref, dkc_ref, 9), (dv_ref, dvc_ref, 10)):
                    _store_strided(ref, tmp[t], j, r, dil, [carry[carry_slot(j, q)] for q in range(sub)])

    def clamp(n):
        return jnp.minimum(n, nb - 1)

    cur = pl.BlockSpec((pp, rb, _LANES), lambda hb, n, r: (g * npb + hb, clamp(n), 0))
    prev = pl.BlockSpec((pp, rb, _LANES), lambda hb, n, r: (g * npb + hb, jnp.maximum(clamp(n) - 1, 0), 0))
    stat = pl.BlockSpec((pp, rb, _LANES), lambda hb, n, r: (hb, clamp(n), 0))
    bspec = pl.BlockSpec((2 * pp, _STEPS, 2 * _STEPS), lambda hb, n, r: (g * npb + hb, 0, 0))
    late = pl.BlockSpec((3, pp, rb, _LANES), lambda hb, n, r: (0, g * npb + hb, jnp.maximum(n - 1, 0), 0))
    dsspec = pl.BlockSpec((2 * pp, _STEPS, 2 * _STEPS), lambda hb, n, r: (hb, 0, 0))
    np_ = len(prev_grads)
    carry = pltpu.VMEM((dil * pp, _STEPS, _LANES), F32)
    return _pcall(
        body, name=f"attn_bwd_g{g}", grid=(npb, nb + 1, dil // sub),
        in_specs=[cur, cur, prev, cur, prev, stat, stat, stat, bspec] + [_ANY] * np_,
        out_specs=[late, dsspec],
        out_shape=[jax.ShapeDtypeStruct((3, 3 * _PAIRS, s, _LANES), F32),
                   jax.ShapeDtypeStruct((8, _STEPS, 2 * _STEPS), F32)],
        scratch=[carry, carry, carry] + _tmp_rows(dil, 11),
        aliases={9 + t: t for t in range(np_)}, rider=rider,
    )(uq, uk, uk, uv, uv, dys, lse, delta, bias, *prev_grads)


def _place():
    x, y, c = lax.axis_index("x"), lax.axis_index("y"), lax.axis_index("c")
    chips = [(1 - x, y), (x, 1 - y), (1 - x, 1 - y)]
    return x, y, c, chips


def _slab(ref, axis, chip, width):
    start = pl.multiple_of(chip * width, width)
    if axis == 0:
        return ref.at[pl.ds(start, width), :]
    return ref.at[:, pl.ds(start, width)]


def _run_rider(rider, name):
    nin, nout = len(rider.ins), len(rider.out_shapes)

    def body(*refs):
        ins, outs, scr = refs[:nin], refs[nin:nin + nout], refs[nin + nout:]
        rider.start(ins, outs, scr)
        rider.finish(ins, outs, scr)

    return _pcall(body, name=name, in_specs=[_ANY] * nin, out_specs=[_ANY] * nout, out_shape=rider.out_shapes,
                  scratch=rider.scratch)(*rider.ins)


def _gather_halves_rider(shard, axis):
    shape = list(shard.shape)
    shape[axis] *= 4
    full = jax.ShapeDtypeStruct(tuple(shape), shard.dtype)
    half = shard.shape[0] // 2
    width = shard.shape[axis]

    def region(out, chip, core):
        if axis == 0:
            return out.at[pl.ds(pl.multiple_of(chip * width + core * half, half), half), :]
        return out.at[pl.ds(pl.multiple_of(core * half, half), half), pl.ds(pl.multiple_of(chip * width, width), width)]

    def copies(ins, outs, scr):
        send, recv, loc = scr
        (src,), (out,) = ins, outs
        x, y, c, chips = _place()
        mine = 2 * x + y
        own = pltpu.make_async_copy(src, _slab(out, axis, mine, width), loc.at[0])
        my_half = src.at[pl.ds(pl.multiple_of(c * half, half), half), :]
        over_ici, ici_in, to_sib, sib_in = [], [], [], []
        for j, (px, py) in enumerate(chips):
            theirs = 2 * px + py
            ici = dict(send_sem=send.at[j], recv_sem=recv.at[j], device_id=(px, py, c), device_id_type=MESH)
            d2d = dict(send_sem=send.at[3 + j], recv_sem=recv.at[3 + j], device_id=(x, y, 1 - c),
                       device_id_type=MESH)
            over_ici.append(pltpu.make_async_remote_copy(src_ref=my_half, dst_ref=region(out, mine, c), **ici))
            ici_in.append(pltpu.make_async_remote_copy(src_ref=my_half, dst_ref=region(out, theirs, c), **ici))
            to_sib.append(pltpu.make_async_remote_copy(
                src_ref=region(out, theirs, c), dst_ref=region(out, theirs, c), **d2d))
            sib_in.append(pltpu.make_async_remote_copy(
                src_ref=region(out, theirs, c), dst_ref=region(out, theirs, 1 - c), **d2d))
        return own, over_ici, ici_in, to_sib, sib_in

    def start(ins, outs, scr):
        own, over_ici, _, _, _ = copies(ins, outs, scr)
        own.start()
        for cp in over_ici:
            cp.start()

    def finish(ins, outs, scr):
        own, over_ici, ici_in, to_sib, sib_in = copies(ins, outs, scr)
        for j in range(3):
            ici_in[j].wait_recv()
            to_sib[j].start()
        for cp in sib_in:
            cp.wait_recv()
        own.wait()
        for cp in over_ici + to_sib:
            cp.wait_send()

    return _Rider([shard], [full], [pltpu.SemaphoreType.DMA((6,)), pltpu.SemaphoreType.DMA((6,)),
                                    pltpu.SemaphoreType.DMA((1,))], start, finish)


def _join_riders(riders):
    if len(riders) == 1:
        return riders[0]

    def parts(ins, outs, scr):
        pi = po = ps = 0
        for rd in riders:
            ni, no, ns = len(rd.ins), len(rd.out_shapes), len(rd.scratch)
            yield rd, ins[pi:pi + ni], outs[po:po + no], scr[ps:ps + ns]
            pi, po, ps = pi + ni, po + no, ps + ns

    def start(ins, outs, scr):
        for rd, i, o, sc in parts(ins, outs, scr):
            rd.start(i, o, sc)

    def finish(ins, outs, scr):
        for rd, i, o, sc in parts(ins, outs, scr):
            rd.finish(i, o, sc)

    return _Rider(sum((rd.ins for rd in riders), []), sum((rd.out_shapes for rd in riders), []),
                  sum((rd.scratch for rd in riders), []), start, finish)


def _scatter_rider(grads, axes, rows=None):
    nw = len(grads)
    outs_shape = []
    for gr, ax in zip(grads, axes):
        shape = list(gr.shape)
        shape[ax] //= 4
        if rows is not None:
            assert ax == 1
            shape[0] = rows[1] - rows[0]
        outs_shape.append(jax.ShapeDtypeStruct((3,) + tuple(shape), gr.dtype))

    def copies(ins, outs, scr):
        send, recv = scr
        x, y, c, chips = _place()
        cps = []
        for t in range(nw):
            width = ins[t].shape[axes[t]] // 4
            src = ins[t] if rows is None else ins[t].at[pl.ds(rows[0], rows[1] - rows[0]), :]
            for j, (px, py) in enumerate(chips):
                cps.append(pltpu.make_async_remote_copy(
                    src_ref=_slab(src, axes[t], 2 * px + py, width), dst_ref=outs[t].at[j],
                    send_sem=send.at[3 * t + j], recv_sem=recv.at[3 * t + j],
                    device_id=(px, py, c), device_id_type=MESH))
        return cps

    def start(ins, outs, scr):
        for cp in copies(ins, outs, scr):
            cp.start()

    def finish(ins, outs, scr):
        cps = copies(ins, outs, scr)
        for cp in cps:
            cp.wait_recv()
        for cp in cps:
            cp.wait_send()

    return _Rider(grads, outs_shape, [pltpu.SemaphoreType.DMA((3 * nw,)), pltpu.SemaphoreType.DMA((3 * nw,))],
                  start, finish)


def _swap_rider(parts):
    nw = len(parts)

    def copies(ins, outs, scr):
        send, recv = scr
        x, y, c, _ = _place()
        return [pltpu.make_async_remote_copy(
            src_ref=ins[t], dst_ref=outs[t], send_sem=send.at[t], recv_sem=recv.at[t],
            device_id=(x, y, 1 - c), device_id_type=MESH) for t in range(nw)]

    def start(ins, outs, scr):
        for cp in copies(ins, outs, scr):
            cp.start()

    def finish(ins, outs, scr):
        cps = copies(ins, outs, scr)
        for cp in cps:
            cp.wait_recv()
        for cp in cps:
            cp.wait_send()

    return _Rider(parts, [jax.ShapeDtypeStruct(p.shape, p.dtype) for p in parts],
                  [pltpu.SemaphoreType.DMA((nw,)), pltpu.SemaphoreType.DMA((nw,))], start, finish)


def _sum_all_devices(buf, name):
    rows, cols = buf.shape

    def body(in_ref, o_ref, gat_ref, send, recv):
        x, y, c, _ = _place()
        me = 4 * x + 2 * y + c
        gat_ref[me] = in_ref[...]
        started = []
        for mask in range(1, 8):
            fx, fy, fc = (mask >> 2) & 1, (mask >> 1) & 1, mask & 1
            peer = (x + fx * (1 - 2 * x), y + fy * (1 - 2 * y), c + fc * (1 - 2 * c))
            cp = pltpu.make_async_remote_copy(
                src_ref=in_ref, dst_ref=gat_ref.at[me], send_sem=send.at[mask - 1], recv_sem=recv.at[mask - 1],
                device_id=peer, device_id_type=MESH)
            cp.start()
            started.append(cp)
        for cp in started:
            cp.wait_recv()
        for cp in started:
            cp.wait_send()
        acc = gat_ref[0]
        for t in range(1, 8):
            acc = acc + gat_ref[t]
        o_ref[...] = acc

    vm = pl.BlockSpec(memory_space=pltpu.VMEM)
    return _pcall(
        body, name=name, in_specs=[vm], out_specs=vm, out_shape=jax.ShapeDtypeStruct((rows, cols), F32),
        scratch=[pltpu.VMEM((8, rows, cols), F32), pltpu.SemaphoreType.DMA((7,)), pltpu.SemaphoreType.DMA((7,))],
    )(buf)


_UPD_T = 256


def _sum_partials(own, got, name):
    rows, cols = own.shape
    tr = min(_UPD_T, rows)

    def body(own_ref, got_ref, o_ref):
        acc = own_ref[...].astype(F32)
        for j in range(3):
            acc = acc + got_ref[j].astype(F32)
        o_ref[...] = acc

    blk = pl.BlockSpec((tr, cols), lambda i: (i, 0))
    return _pcall(
        body, name=name, grid=(rows // tr,),
        in_specs=[blk, pl.BlockSpec((3, tr, cols), lambda i: (0, i, 0))], out_specs=blk,
        out_shape=jax.ShapeDtypeStruct((rows, cols), F32))(own, got)


def _adamw_math(w, gr, m, v):
    m = _B1 * m + (1.0 - _B1) * gr
    v = _B2 * v + (1.0 - _B2) * (gr * gr)
    m_hat = m / (1.0 - _B1 ** _STEP)
    v_hat = v / (1.0 - _B2 ** _STEP)
    delta = -_LR * (m_hat / (jnp.sqrt(v_hat) + _EPS) + _WD * w)
    return delta, m, v


def _adamw(w, m, v, parts, name):
    rows, cols = w.shape
    tr = min(_UPD_T, rows)
    npart = len(parts)

    def body(w_ref, m_ref, v_ref, *rest):
        p_refs, (g_ref, d_ref, nm_ref, nv_ref) = rest[:npart], rest[npart:]
        gr = p_refs[0][...]
        for p in p_refs[1:]:
            gr = gr + p[...]
        delta, nm, nv = _adamw_math(w_ref[...], gr, m_ref[...], v_ref[...])
        g_ref[...] = gr
        d_ref[...] = delta
        nm_ref[...] = nm
        nv_ref[...] = nv

    blk = pl.BlockSpec((tr, cols), lambda i: (i, 0))
    sh = jax.ShapeDtypeStruct((rows, cols), F32)
    return _pcall(body, name=name, grid=(rows // tr,), in_specs=[blk] * (3 + npart), out_specs=[blk] * 4,
                  out_shape=[sh] * 4)(w, m, v, *parts)


def _adamw_layers(w, m, v, parts, name):
    _, rows, cols = w.shape
    tr = min(_UPD_T, rows)
    npart = len(parts[0])

    def body(w_ref, m_ref, v_ref, *rest):
        p_refs, (g_ref, d_ref, nm_ref, nv_ref) = rest[:2 * npart], rest[2 * npart:]
        grs = []
        for layer in range(2):
            gr = p_refs[layer * npart][...]
            for p in p_refs[layer * npart + 1:(layer + 1) * npart]:
                gr = gr + p[...]
            grs.append(gr)
        gr = jnp.where(pl.program_id(0) == 0, grs[0], grs[1])
        delta, nm, nv = _adamw_math(w_ref[...], gr, m_ref[...], v_ref[...])
        g_ref[...] = gr
        d_ref[...] = delta
        nm_ref[...] = nm
        nv_ref[...] = nv

    blk = pl.BlockSpec((None, tr, cols), lambda l, i: (l, i, 0))

    def part_spec(layer):
        return pl.BlockSpec((tr, cols), lambda l, i: (jnp.where(l == layer, i, 0), 0))

    sh = jax.ShapeDtypeStruct(w.shape, F32)
    return _pcall(
        body, name=name, grid=(2, rows // tr),
        in_specs=[blk] * 3 + [part_spec(0)] * npart + [part_spec(1)] * npart, out_specs=[blk] * 4,
        out_shape=[sh] * 4)(w, m, v, *parts[0], *parts[1])


_PACK_W = 1024


def _pack(arrs, rows):
    flat = []
    for a in arrs:
        f = a.reshape(-1).astype(F32)
        pad = (-f.shape[0]) % _PACK_W
        flat.append(jnp.pad(f, (0, pad)))
    f = jnp.concatenate(flat)
    f = jnp.pad(f, (0, rows * _PACK_W - f.shape[0]))
    return f.reshape(rows, _PACK_W)


def _unpack(buf, shapes):
    flat = buf.reshape(-1)
    out, pos = [], 0
    for sh in shapes:
        size = math.prod(sh)
        out.append(flat[pos:pos + size].reshape(sh))
        pos += size + ((-size) % _PACK_W)
    return out


def _pack_rows(shapes):
    total = sum(-(-math.prod(sh) // _PACK_W) for sh in shapes)
    return -(-total // 8) * 8


def kernel(x, rel_bias, ab_norm, ab_w_in, ab_conv_w, ab_conv_b, ab_ln_g, ab_ln_b, ab_w_out, sc_norm, sc_w_in, sc_conv_w, sc_w_out, mlp_norm, mlp_w_up, mlp_w_down, final_norm, loss_target, m_rel_bias, m_ab_norm, m_ab_w_in, m_ab_conv_w, m_ab_conv_b, m_ab_ln_g, m_ab_ln_b, m_ab_w_out, m_sc_norm, m_sc_w_in, m_sc_conv_w, m_sc_w_out, m_mlp_norm, m_mlp_w_up, m_mlp_w_down, m_final_norm, v_rel_bias, v_ab_norm, v_ab_w_in, v_ab_conv_w, v_ab_conv_b, v_ab_ln_g, v_ab_ln_b, v_ab_w_out, v_sc_norm, v_sc_w_in, v_sc_conv_w, v_sc_w_out, v_mlp_norm, v_mlp_w_up, v_mlp_w_down, v_final_norm):
    s, d = x.shape[1], x.shape[2]
    dff = 4 * d
    c = _GROUP_COLS
    chip = 2 * lax.axis_index("x") + lax.axis_index("y")
    on_c0 = (lax.axis_index("c") == 0).astype(F32)
    h0 = x[0]
    tgt = loss_target[0]

    cw_sh, scn_sh, scw_sh = ab_conv_w[0], sc_norm, sc_conv_w[0]
    conv_w_full = lax.dynamic_update_slice(jnp.zeros((_CONV_K, c), F32), cw_sh * on_c0, (0, chip * cw_sh.shape[1]))
    scn_full = lax.dynamic_update_slice(jnp.zeros((1, d), F32), scn_sh * on_c0, (0, chip * scn_sh.shape[1]))
    scw_full = lax.dynamic_update_slice(jnp.zeros((3, d), F32), scw_sh * on_c0, (0, chip * scw_sh.shape[1]))
    small_shapes = [(_CONV_K, c), (1, d), (3, d)]
    small = _sum_all_devices(_pack([conv_w_full, scn_full, scw_full], _pack_rows(small_shapes)), "gather_small")
    conv_w, sc_g, sc_cw = _unpack(small, small_shapes)

    w_shards = [ab_w_in[0], ab_w_out[0], sc_w_in[0], sc_w_out[0], mlp_w_up[0], mlp_w_up[1],
                mlp_w_down[0], mlp_w_down[1]]
    w_axes = [1, 0, 1, 0, 1, 1, 0, 0]
    wb = [w.astype(BF16) for w in w_shards]
    full_w = [None] * 8

    def gather(idx):
        return _join_riders([_gather_halves_rider(wb[t], w_axes[t]) for t in idx])

    def put(idx, got_w):
        for t, w in zip(idx, got_w):
            full_w[t] = w

    buckets = _bucket_maps()
    bias = _bias_expand(rel_bias, buckets)
    n0, got_w = _rms_fwd(h0, ab_norm, "rms_fwd_ab", rider=_gather_halves_rider(wb[0], w_axes[0]))
    put([0], got_w)
    w_in = full_w[0]
    tm = min(1024, s)
    tm2 = min(2048, s)
    tmh = min(512, s)
    uc = _mm(n0, w_in, "nn", m=s, n=2 * c, k=d, tm=tm2, tn=2 * c, tk=d, out_dtype=BF16, name="proj_conv")
    uq, uk, uv = [], [], []
    for t, (nm, dst) in enumerate(zip("qkv", (uq, uk, uv))):
        res = _mm(n0, w_in, "nn", m=s, n=3 * c, k=d, tm=tm2, tn=c, tk=d, out_dtype=F32, name=f"proj_{nm}",
                  b_off=(0, 2 + 3 * t), split="o", rider=gather([1]) if t == 0 else None)
        if t == 0:
            res, got_w = res
            put([1], got_w)
        dst.append(res)
    uq, uk, uv = uq[0], uk[0], uv[0]
    (cat, ca), got_w = _conv_a_fwd(uc, conv_w, ab_conv_b, ab_ln_g, ab_ln_b, rider=gather([2]))
    put([2], got_w)
    outs, lses = [], []
    for g, (_, dil) in enumerate(_GROUPS):
        idx = ([4], [6], [3, 5])[g]
        (o, l), got_w = _attn_fwd(uq, uk, uv, bias, g, dil, 4 if dil <= 4 else 2, rider=gather(idx))
        put(idx, got_w)
        outs.append(o)
        lses.append(l)
    cat, lse = _attn_merge(outs, lses, cat)
    h1, n1 = _mm(cat, full_w[1], "nn", m=s, n=d, k=d, tm=tm, tn=d, tk=d, out_dtype=(F32, BF16), name="out_ab",
                 epi=_epi_add_rms, extras=(h0,), vecs=(mlp_norm[0:1],))

    def mlp_fwd(h, nrm, layer, next_gain=None, rider=None):
        zr = _mm(nrm, full_w[4 + layer], "nn", m=s, n=dff, k=d, tm=tmh, tn=dff, tk=d, out_dtype=BF16,
                 name=f"mlp_up{layer}", epi=_epi_relu, rider=rider)
        if rider is not None:
            zr, got_r = zr
            put([7], got_r)
        kw = dict(m=s, n=d, k=dff, tm=tmh, tn=d, tk=dff, name=f"mlp_down{layer}", a_pro=_square, extras=(h,))
        if next_gain is None:
            return zr, _mm(zr, full_w[6 + layer], "nn", out_dtype=F32, epi=_epi_add, **kw), None
        hn, nn = _mm(zr, full_w[6 + layer], "nn", out_dtype=(F32, BF16), epi=_epi_add_rms, vecs=(next_gain,), **kw)
        return zr, hn, nn

    zr0, h2, n2 = mlp_fwd(h1, n1, 0, next_gain=sc_g, rider=gather([7]))
    _, w_out, w_si, w_so, w_up0, w_up1, w_dn0, w_dn1 = full_w
    w_up, w_dn = [w_up0, w_up1], [w_dn0, w_dn1]
    u2 = _mm(n2, w_si, "nn", m=s, n=3 * d, k=d, tm=tmh, tn=3 * d, tk=d, out_dtype=BF16, name="proj_sc")
    scv = _short_conv_fwd(u2, sc_cw)
    h3, n3 = _mm(scv, w_so, "nn", m=s, n=d, k=d, tm=tm, tn=d, tk=d, out_dtype=(F32, BF16), name="out_sc",
                 epi=_epi_add_rms, extras=(h2,), vecs=(mlp_norm[1:2],))
    zr1, h4, _ = mlp_fwd(h3, n3, 1)

    dh4, dh4b, g_final, loss_part = _loss_head(h4, tgt, final_norm.reshape(1, d))
    tkw = min(2048, s)

    big_grads, got, sums = [None] * 8, [None] * 8, [None] * 8

    def scatter(t):
        return _scatter_rider([big_grads[t]], [w_axes[t]])

    def own_slab(t):
        width = big_grads[t].shape[w_axes[t]] // 4
        return lax.dynamic_slice_in_dim(big_grads[t], chip * width, width, axis=w_axes[t])

    def arrived(t, got_t):
        got[t] = got_t[0]
        sums[t] = _sum_partials(own_slab(t), got[t], f"sum_partials{t}")

    def mlp_bwd(dh, dhb, h, nrm, zr, layer):
        dz = _mm(dhb, w_dn[layer], "nt", m=s, n=dff, k=d, tm=tmh, tn=dff, tk=d, out_dtype=BF16,
                 name=f"mlp_down{layer}_dx", epi=_epi_relu_sq_bwd, extras=(zr,))
        big_grads[6 + layer] = _mm(zr, dhb, "tn", m=dff, n=d, k=s, tm=1024, tn=d, tk=tkw, out_dtype=BF16,
                                   name=f"mlp_down{layer}_dw", a_pro=_square)
        big_grads[4 + layer] = _mm(nrm, dz, "tn", m=d, n=dff, k=s, tm=d, tn=1024, tk=tkw, out_dtype=BF16,
                                   name=f"mlp_up{layer}_dw")
        res = _mm(dz, w_up[layer], "nt", m=s, n=d, k=dff, tm=tmh, tn=d, tk=dff, out_dtype=(F32, BF16),
                  name=f"mlp_up{layer}_dx", rider=scatter(6) if layer == 0 else None, epi=_epi_rms_bwd,
                  extras=(h, dh), vecs=(mlp_norm[layer:layer + 1],), row_sum=True)
        if layer == 0:
            res, got_t = res
            arrived(6, got_t)
        return res

    dh3, dh3b, g_mn1 = mlp_bwd(dh4, dh4b, h3, n3, zr1, 1)

    dsc = _mm(dh3b, w_so, "nt", m=s, n=d, k=d, tm=tm, tn=d, tk=d, out_dtype=F32, name="out_sc_dx")
    big_grads[3] = _mm(scv, dh3b, "tn", m=d, n=d, k=s, tm=d, tn=d, tk=tkw, out_dtype=BF16, name="out_sc_dw")
    du2, g_sccw8 = _short_conv_bwd(u2, dsc, sc_cw)
    big_grads[2], got_t = _mm(n2, du2, "tn", m=d, n=3 * d, k=s, tm=d, tn=1024, tk=tkw, out_dtype=BF16,
                              name="proj_sc_dw", rider=scatter(3))
    arrived(3, got_t)
    dh2, dh2b, g_scn = _mm(
        du2, w_si, "nt", m=s, n=d, k=3 * d, tm=tmh, tn=d, tk=3 * d, out_dtype=(F32, BF16), name="proj_sc_dx",
        epi=_epi_rms_bwd, extras=(h2, dh3), vecs=(sc_g,), row_sum=True)

    dh1, dh1b, g_mn0 = mlp_bwd(dh2, dh2b, h1, n1, zr0, 0)

    dcat = _mm(dh1b, w_out, "nt", m=s, n=d, k=d, tm=tm, tn=d, tk=d, out_dtype=F32, name="out_ab_dx")
    big_grads[1] = _mm(cat, dh1b, "tn", m=d, n=d, k=s, tm=d, tn=d, tk=tkw, out_dtype=BF16, name="out_ab_dw")
    (dca, conv_stats), got_t = _conv_a_bwd_ln(ca, dcat, ab_ln_g, ab_ln_b, rider=scatter(1))
    arrived(1, got_t)
    (duc, g_cw32), got_t = _conv_a_bwd_conv(uc, dca, conv_w, rider=scatter(4))
    arrived(4, got_t)
    delta, dys = _attn_delta(dcat, cat)

    dqkv, ds_list = [], []
    for g, (_, dil) in enumerate(_GROUPS):
        late = (2, 5, 7)[g]
        (grads, dsa), got_t = _attn_bwd(uq, uk, uv, dys, lse, delta, bias, dqkv, g, dil, 4 if dil <= 4 else 1,
                                        rider=scatter(late))
        arrived(late, got_t)
        dqkv = [grads]
        ds_list.append(dsa)
    g_bias = _bias_reduce(jnp.concatenate(ds_list, axis=0), buckets)[:, :, 0].T
    dqkv = dqkv[0].reshape(9 * _PAIRS, s, _LANES)

    g_in_conv = _mm(n0, duc, "tn", m=d, n=2 * c, k=s, tm=d, tn=2 * c, tk=tkw, out_dtype=BF16,
                    name="proj_ab_dw_conv")
    g_in_qkv, sib_late = _mm(n0, dqkv, "tn", m=d, n=9 * c, k=s, tm=d, tn=3 * c, tk=min(1024, s), out_dtype=BF16,
                             name="proj_ab_dw_qkv", rider=_swap_rider(sums[1:]), split="b")
    big_grads[0] = jnp.concatenate([g_in_conv, g_in_qkv], axis=1)
    cut = d // 4
    dn0, (got_top,) = _mm(duc, w_in, "nt", m=s, n=d, k=2 * c, tm=tm, tn=d, tk=2 * c, out_dtype=F32,
                          name="proj_ab_dx_conv", rider=_scatter_rider([big_grads[0]], [w_axes[0]], rows=(0, cut)))
    dn0, (got_bot,) = _mm(dqkv, w_in[:, 2 * c:], "nt", m=s, n=d, k=9 * c, tm=tm, tn=d, tk=3 * c, out_dtype=F32,
                          name="proj_ab_dx_qkv", epi=_epi_add, extras=(dn0,), split="a",
                          rider=_scatter_rider([big_grads[0]], [w_axes[0]], rows=(cut, d)))
    grad_x, g_abn = _rms_bwd(dn0, h0, ab_norm, dh1, "rms_bwd_ab")
    own0 = own_slab(0)
    sums[0] = jnp.concatenate([_sum_partials(own0[:cut], got_top, "sum_partials0_top"),
                               _sum_partials(own0[cut:], got_bot, "sum_partials0_bottom")], axis=0)
    sib = list(_run_rider(_swap_rider([sums[0]]), "swap_sibling_w_in")) + sib_late

    upd = [_adamw(w_shards[t], mm[0], vv[0], [sums[t], sib[t]], f"adamw{t}")
           for t, (mm, vv) in enumerate(((m_ab_w_in, v_ab_w_in), (m_ab_w_out, v_ab_w_out),
                                         (m_sc_w_in, v_sc_w_in), (m_sc_w_out, v_sc_w_out)))]
    upd_up = _adamw_layers(mlp_w_up, m_mlp_w_up, v_mlp_w_up, [[sums[4], sib[4]], [sums[5], sib[5]]], "adamw_up")
    upd_dn = _adamw_layers(mlp_w_down, m_mlp_w_down, v_mlp_w_down, [[sums[6], sib[6]], [sums[7], sib[7]]],
                           "adamw_down")

    full_shapes = [(_NUM_BUCKETS, rel_bias.shape[1]), (1, d), (_CONV_K, c), (1, c), (1, c), (1, c), (1, d),
                   (3, d), (2, d), (d,), (1, 1)]
    small_grads = [g_bias, g_abn, g_cw32[:_CONV_K], conv_stats[0:1], conv_stats[1:2], conv_stats[2:3], g_scn,
                   g_sccw8[:3], jnp.concatenate([g_mn0, g_mn1], axis=0), g_final.reshape(d), loss_part[0:1, 0:1]]
    tot = _unpack(_sum_all_devices(_pack(small_grads, _pack_rows(full_shapes)), "sum_small"), full_shapes)
    loss = tot.pop()[0, 0]
    for idx, sh in ((2, cw_sh), (6, scn_sh), (7, scw_sh)):
        width = sh.shape[1]
        tot[idx] = lax.dynamic_slice_in_dim(tot[idx], chip * width, width, axis=1)
    sm_w = [rel_bias, ab_norm, cw_sh, ab_conv_b, ab_ln_g, ab_ln_b, scn_sh, scw_sh, mlp_norm, final_norm]
    sm_m = [m_rel_bias, m_ab_norm, m_ab_conv_w[0], m_ab_conv_b, m_ab_ln_g, m_ab_ln_b, m_sc_norm, m_sc_conv_w[0],
            m_mlp_norm, m_final_norm]
    sm_v = [v_rel_bias, v_ab_norm, v_ab_conv_w[0], v_ab_conv_b, v_ab_ln_g, v_ab_ln_b, v_sc_norm, v_sc_conv_w[0],
            v_mlp_norm, v_final_norm]
    sh_shapes = [tuple(t.shape) for t in tot]
    rows = _pack_rows(sh_shapes)
    sm_upd = _adamw(_pack(sm_w, rows), _pack(sm_m, rows), _pack(sm_v, rows), [_pack(tot, rows)], "adamw_small")
    sm_g, sm_d, sm_nm, sm_nv = [_unpack(buf, sh_shapes) for buf in sm_upd]

    def assemble(kind, sm):
        big = [u[kind] for u in upd]
        return [sm[0], sm[1], big[0][None], sm[2][None], sm[3], sm[4], sm[5], big[1][None], sm[6], big[2][None],
                sm[7][None], big[3][None], sm[8], upd_up[kind], upd_dn[kind], sm[9]]

    res = [loss, grad_x[None]]
    for kind, sm in enumerate((sm_g, sm_d, sm_nm, sm_nv)):
        res += assemble(kind, sm)
    return tuple(res)
```

```python
import functools
import math

import jax
import jax.numpy as jnp
from jax import lax
from jax.experimental import pallas as pl
from jax.experimental.pallas import tpu as pltpu

F32 = jnp.float32
BF16 = jnp.bfloat16
MESH = pl.DeviceIdType.MESH

_GROUPS = ((128, 1), (512, 4), (2048, 16))
_STEPS = 128
_HEAD_DIM = 64
_GROUP_COLS = 512
_NUM_BUCKETS = 32
_MAX_DISTANCE = 2048
_CONV_K = 31
_HALO = 32
_SC_HALO = 16
_RMS_EPS = 1e-6
_LN_EPS = 1e-5
_NEG = -1e30
_LANES = 128
_VMEM_LIMIT = 56 * 1024 * 1024

_LR, _B1, _B2, _EPS, _WD, _STEP = 0.001, 0.9, 0.999, 1e-08, 0.01, 10


class _Rider:
    def __init__(self, ins, out_shapes, scratch, start, finish):
        self.ins, self.out_shapes, self.scratch = list(ins), list(out_shapes), list(scratch)
        self.start, self.finish = start, finish


def _pcall(body, *, name, out_shape, in_specs, out_specs, grid=None, scratch=(), aliases=None, rider=None):
    kw = {} if grid is None else {"grid": grid}
    cparams = pltpu.CompilerParams(vmem_limit_bytes=_VMEM_LIMIT)
    if rider is None:
        return pl.pallas_call(
            body, name=name, out_shape=out_shape, in_specs=in_specs, out_specs=out_specs,
            scratch_shapes=list(scratch), input_output_aliases=aliases or {},
            compiler_params=cparams, **kw)
    single = not isinstance(out_specs, (list, tuple))
    ospecs = [out_specs] if single else list(out_specs)
    oshapes = [out_shape] if single else list(out_shape)
    nin, nout, nscr = len(in_specs), len(ospecs), len(scratch)
    rin, rout = len(rider.ins), len(rider.out_shapes)

    def wrapped(*refs):
        h_in, r_in = refs[:nin], refs[nin:nin + rin]
        p = nin + rin
        h_out, r_out = refs[p:p + nout], refs[p + nout:p + nout + rout]
        p += nout + rout
        h_scr, r_scr = refs[p:p + nscr], refs[p + nscr:]
        ids = [pl.program_id(a) for a in range(len(grid))]
        first = functools.reduce(jnp.logical_and, [i == 0 for i in ids])
        last = functools.reduce(jnp.logical_and, [i == g - 1 for i, g in zip(ids, grid)])

        @pl.when(first)
        def _():
            rider.start(r_in, r_out, r_scr)

        body(*h_in, *h_out, *h_scr)

        @pl.when(last)
        def _():
            rider.finish(r_in, r_out, r_scr)

    call = pl.pallas_call(
        wrapped, name=name, out_shape=oshapes + rider.out_shapes,
        in_specs=list(in_specs) + [_ANY] * rin, out_specs=ospecs + [_ANY] * rout,
        scratch_shapes=list(scratch) + rider.scratch, input_output_aliases=aliases or {},
        compiler_params=cparams, **kw)

    def run(*operands):
        res = call(*operands, *rider.ins)
        host = res[0] if single else list(res[:nout])
        return host, list(res[nout:])

    return run


def _sig(x):
    return 1.0 / (1.0 + jnp.exp(-x))


_ANY = pl.BlockSpec(memory_space=pl.ANY)


def _lanes_of(ref):
    parts = [ref[p] for p in range(ref.shape[0])]
    return parts[0] if len(parts) == 1 else jnp.concatenate(parts, axis=1)


def _mm(a, b, mode, *, m, n, k, tm, tn, tk, out_dtype, name, epi=None, extras=(), b_off=(0, 0), rider=None,
        split="", vecs=(), a_pro=None, row_sum=False):
    nk = k // tk
    assert m % tm == 0 and n % tn == 0 and k % tk == 0
    o0, o1 = b_off
    if mode == "nn":
        a_spec = pl.BlockSpec((tm, tk), lambda i, j, kk: (i, kk))
        b_spec = pl.BlockSpec((tk, tn), lambda i, j, kk: (kk + o0, j + o1))
        dn = (((1,), (0,)), ((), ()))
    elif mode == "nt":
        a_spec = pl.BlockSpec((tm, tk), lambda i, j, kk: (i, kk))
        if "a" in split:
            a_spec = pl.BlockSpec((tk // _LANES, tm, _LANES), lambda i, j, kk: (kk, i, 0))
        b_spec = pl.BlockSpec((tn, tk), lambda i, j, kk: (j + o0, kk + o1))
        dn = (((1,), (1,)), ((), ()))
    else:
        a_spec = pl.BlockSpec((tk, tm), lambda i, j, kk: (kk, i))
        b_spec = pl.BlockSpec((tk, tn), lambda i, j, kk: (kk + o0, j + o1))
        if "b" in split:
            b_spec = pl.BlockSpec((tn // _LANES, tk, _LANES), lambda i, j, kk: (j, kk, 0))
        dn = (((0,), (0,)), ((), ()))
    o_spec = pl.BlockSpec((tm, tn), lambda i, j, kk: (i, j))
    e_spec = o_spec
    if "o" in split:
        o_spec = pl.BlockSpec((tn // _LANES, tm, _LANES), lambda i, j, kk: (j, i, 0))
    v_spec = pl.BlockSpec((1, tn), lambda i, j, kk: (0, j))
    ne = len(extras) + len(vecs)
    multi = isinstance(out_dtype, tuple)
    dts = out_dtype if multi else (out_dtype,)
    no = len(dts)
    nr = 1 if row_sum else 0
    assert not row_sum or tn == n

    def body(a_ref, b_ref, *rest):
        ex, o_refs = rest[:ne], rest[ne:ne + no]
        av = _lanes_of(a_ref) if "a" in split else a_ref[...]
        bv = _lanes_of(b_ref) if "b" in split else b_ref[...]
        if av.dtype != BF16:
            av = av.astype(BF16)
        if bv.dtype != BF16:
            bv = bv.astype(BF16)
        if a_pro is not None:
            av = a_pro(av)
        p = lax.dot_general(av, bv, dn, preferred_element_type=F32)

        def fin(x):
            if epi is not None:
                x = epi(x, *[e[...] for e in ex])
            if row_sum:
                row, x = x[-1], (x[:-1] if multi else x[0])
                row_ref = rest[ne + no]

                @pl.when(pl.program_id(0) == 0)
                def _():
                    row_ref[...] = row

                @pl.when(pl.program_id(0) > 0)
                def _():
                    row_ref[...] += row

            for o_ref, val, dt in zip(o_refs, x if multi else (x,), dts):
                if "o" in split:
                    for p in range(tn // _LANES):
                        o_ref[p] = val[:, p * _LANES:(p + 1) * _LANES].astype(dt)
                else:
                    o_ref[...] = val.astype(dt)

        if nk == 1:
            fin(p)
        else:
            acc = rest[ne + no + nr]
            kk = pl.program_id(2)

            @pl.when(kk == 0)
            def _():
                acc[...] = p

            @pl.when(kk > 0)
            def _():
                acc[...] += p

            @pl.when(kk == nk - 1)
            def _():
                fin(acc[...])

    oshape = (n // _LANES, m, _LANES) if "o" in split else (m, n)
    shapes = [jax.ShapeDtypeStruct(oshape, dt) for dt in dts]
    ospecs = [o_spec] * no
    if row_sum:
        shapes.append(jax.ShapeDtypeStruct((1, n), F32))
        ospecs.append(v_spec)
    lone = not multi and not row_sum
    return _pcall(
        body, name=name, grid=(m // tm, n // tn, nk),
        in_specs=[a_spec, b_spec] + [e_spec] * len(extras) + [v_spec] * len(vecs),
        out_specs=ospecs[0] if lone else ospecs, out_shape=shapes[0] if lone else shapes,
        scratch=[pltpu.VMEM((tm, tn), F32)] if nk > 1 else [], rider=rider,
    )(a, b, *extras, *vecs)


def _epi_add(x, r):
    return x + r


def _epi_relu(x):
    return jnp.maximum(x, 0.0)


def _square(x):
    return x * x


def _epi_relu_sq_bwd(da, zr):
    return da * (2.0 * zr.astype(F32))


def _epi_add_rms(x, r, g):
    h = x + r
    return h, h * lax.rsqrt(jnp.mean(h * h, axis=-1, keepdims=True) + _RMS_EPS) * g


def _epi_rms_bwd(dn, h, dh_in, g):
    dx, dg = _rms_bwd_math(dn, h, g)
    dh = dh_in + dx
    return dh, dh, dg


_ROW_T = 512


def _rms_fwd(h, g, name, rider=None):
    s, d = h.shape

    def body(h_ref, g_ref, o_ref):
        x = h_ref[...]
        r = lax.rsqrt(jnp.mean(x * x, axis=-1, keepdims=True) + _RMS_EPS)
        o_ref[...] = (x * r * g_ref[...]).astype(BF16)

    row = pl.BlockSpec((_ROW_T, d), lambda i: (i, 0))
    vec = pl.BlockSpec((1, d), lambda i: (0, 0))
    return _pcall(body, name=name, grid=(s // _ROW_T,), in_specs=[row, vec], out_specs=row,
                  out_shape=jax.ShapeDtypeStruct((s, d), BF16), rider=rider)(h, g)


def _rms_bwd_math(dn, x, g):
    r = lax.rsqrt(jnp.mean(x * x, axis=-1, keepdims=True) + _RMS_EPS)
    xhat = x * r
    dg = jnp.sum(dn * xhat, axis=0, keepdims=True)
    t = dn * g
    dx = r * (t - xhat * jnp.mean(t * xhat, axis=-1, keepdims=True))
    return dx, dg


def _rms_bwd(dn, h, g, dh_in, name):
    s, d = h.shape

    def body(dn_ref, h_ref, g_ref, dhi_ref, dh_ref, dg_ref):
        dx, dg = _rms_bwd_math(dn_ref[...], h_ref[...], g_ref[...])
        dh_ref[...] = dhi_ref[...] + dx

        @pl.when(pl.program_id(0) == 0)
        def _():
            dg_ref[...] = jnp.zeros_like(dg_ref)

        dg_ref[...] += dg

    row = pl.BlockSpec((_ROW_T, d), lambda i: (i, 0))
    vec = pl.BlockSpec((1, d), lambda i: (0, 0))
    return _pcall(
        body, name=name, grid=(s // _ROW_T,), in_specs=[row, row, vec, row], out_specs=[row, vec],
        out_shape=[jax.ShapeDtypeStruct((s, d), F32), jax.ShapeDtypeStruct((1, d), F32)])(dn, h, g, dh_in)


def _loss_head(h, tgt, g):
    s, d = h.shape

    def body(h_ref, t_ref, g_ref, dh_ref, dhb_ref, dg_ref, loss_ref):
        x, gv = h_ref[...], g_ref[...]
        r = lax.rsqrt(jnp.mean(x * x, axis=-1, keepdims=True) + _RMS_EPS)
        err = x * r * gv - t_ref[...]
        part = 0.5 * jnp.sum(jnp.mean(err * err, axis=-1, keepdims=True))
        dx, dg = _rms_bwd_math(err * (1.0 / d), x, gv)
        dh_ref[...] = dx
        dhb_ref[...] = dx.astype(BF16)

        @pl.when(pl.program_id(0) == 0)
        def _():
            dg_ref[...] = jnp.zeros_like(dg_ref)
            loss_ref[...] = jnp.zeros_like(loss_ref)

        dg_ref[...] += dg
        loss_ref[...] += jnp.full(loss_ref.shape, part, F32)

    row = pl.BlockSpec((_ROW_T, d), lambda i: (i, 0))
    vec = pl.BlockSpec((1, d), lambda i: (0, 0))
    one = pl.BlockSpec((1, _LANES), lambda i: (0, 0))
    return _pcall(
        body, name="loss_head", grid=(s // _ROW_T,), in_specs=[row, row, vec], out_specs=[row, row, vec, one],
        out_shape=[jax.ShapeDtypeStruct((s, d), F32), jax.ShapeDtypeStruct((s, d), BF16),
                   jax.ShapeDtypeStruct((1, d), F32), jax.ShapeDtypeStruct((1, _LANES), F32)])(h, tgt, g)


_CONV_T = 256
_CONV_RC = 64


def _conv_a_specs(s):
    c = _GROUP_COLS
    hb = _CONV_T // _HALO
    val = pl.BlockSpec((_CONV_T, c), lambda i: (i, 0))
    gate = pl.BlockSpec((_CONV_T, c), lambda i: (i, 1))
    hval = pl.BlockSpec((_HALO, c), lambda i: (jnp.maximum(i * hb - 1, 0), 0))
    hgate = pl.BlockSpec((_HALO, c), lambda i: (jnp.maximum(i * hb - 1, 0), 1))
    return val, gate, hval, hgate


def _fill_glu(val_ref, gate_ref, hval_ref, hgate_ref, hs_ref):
    i = pl.program_id(0)
    hs_ref[pl.ds(_HALO, _CONV_T), :] = val_ref[...].astype(F32) * _sig(gate_ref[...].astype(F32))
    halo = hval_ref[...].astype(F32) * _sig(hgate_ref[...].astype(F32))
    hs_ref[pl.ds(0, _HALO), :] = jnp.where(i > 0, halo, 0.0)


_SHIFT_ROWS = _CONV_T + _HALO - 8


def _fill_shifts(src_ref, sh_ref):
    for b in range(1, 8):
        sh_ref[b - 1] = src_ref[pl.ds(b, _SHIFT_ROWS), :]


def _tap_rows(src_ref, sh_ref, start, rows, lanes=slice(None)):
    b = start % 8
    if b == 0:
        return src_ref[pl.ds(start, rows), lanes]
    return sh_ref[b - 1, pl.ds(start - b, rows), lanes]


def _conv_rows(hs_ref, sh_ref, w_ref, r0, rows):
    off = _HALO - (_CONV_K - 1)
    acc = jnp.zeros((rows, _GROUP_COLS), F32)
    for kk in range(_CONV_K):
        acc = acc + w_ref[kk:kk + 1, :] * _tap_rows(hs_ref, sh_ref, r0 + off + kk, rows)
    return acc


def _ln_fwd(ca, g, b):
    mu = jnp.mean(ca, axis=-1, keepdims=True)
    xc = ca - mu
    rstd = lax.rsqrt(jnp.mean(xc * xc, axis=-1, keepdims=True) + _LN_EPS)
    xhat = xc * rstd
    return xhat, rstd, xhat * g + b


def _conv_a_fwd(uc, w, cb, lg, lb, rider=None):
    s = uc.shape[0]
    c = _GROUP_COLS

    def body(val_ref, gate_ref, hval_ref, hgate_ref, w_ref, cb_ref, lg_ref, lb_ref, o_ref, ca_ref, hs_ref, sh_ref):
        _fill_glu(val_ref, gate_ref, hval_ref, hgate_ref, hs_ref)
        _fill_shifts(hs_ref, sh_ref)
        for rc in range(_CONV_T // _CONV_RC):
            r0 = rc * _CONV_RC
            ca = _conv_rows(hs_ref, sh_ref, w_ref, r0, _CONV_RC) + cb_ref[...]
            ca_ref[pl.ds(r0, _CONV_RC), :] = ca
            _, _, ln = _ln_fwd(ca, lg_ref[...], lb_ref[...])
            o_ref[pl.ds(r0, _CONV_RC), :] = (ln * _sig(ln)).astype(BF16)

    val, gate, hval, hgate = _conv_a_specs(s)
    wspec = pl.BlockSpec((_CONV_K, c), lambda i: (0, 0))
    vec = pl.BlockSpec((1, c), lambda i: (0, 0))
    blk = pl.BlockSpec((_CONV_T, c), lambda i: (i, 0))
    return _pcall(
        body, name="conv_a_fwd", grid=(s // _CONV_T,),
        in_specs=[val, gate, hval, hgate, wspec, vec, vec, vec],
        out_specs=[blk, blk],
        out_shape=[jax.ShapeDtypeStruct((s, 2 * c), BF16), jax.ShapeDtypeStruct((s, c), F32)],
        scratch=[pltpu.VMEM((_CONV_T + _HALO, c), F32), pltpu.VMEM((7, _SHIFT_ROWS, c), F32)],
        rider=rider)(uc, uc, uc, uc, w, cb, lg, lb)


def _conv_a_bwd_ln(ca_all, dcat, lg, lb, rider=None):
    s = ca_all.shape[0]
    c = _GROUP_COLS

    def body(ca_ref, dy_ref, lg_ref, lb_ref, dca_ref, st_ref):
        @pl.when(pl.program_id(0) == 0)
        def _():
            st_ref[...] = jnp.zeros_like(st_ref)

        for rc in range(_CONV_T // _CONV_RC):
            r0 = rc * _CONV_RC
            ca = ca_ref[pl.ds(r0, _CONV_RC), :]
            xhat, rstd, ln = _ln_fwd(ca, lg_ref[...], lb_ref[...])
            sg = _sig(ln)
            dln = dy_ref[pl.ds(r0, _CONV_RC), :] * (sg * (1.0 + ln * (1.0 - sg)))
            dxh = dln * lg_ref[...]
            dca = rstd * (dxh - jnp.mean(dxh, axis=-1, keepdims=True)
                          - xhat * jnp.mean(dxh * xhat, axis=-1, keepdims=True))
            dca_ref[pl.ds(r0, _CONV_RC), :] = dca
            st_ref[0:1, :] += jnp.sum(dca, axis=0, keepdims=True)
            st_ref[1:2, :] += jnp.sum(dln * xhat, axis=0, keepdims=True)
            st_ref[2:3, :] += jnp.sum(dln, axis=0, keepdims=True)

    blk = pl.BlockSpec((_CONV_T, c), lambda i: (i, 0))
    vec = pl.BlockSpec((1, c), lambda i: (0, 0))
    st = pl.BlockSpec((8, c), lambda i: (0, 0))
    return _pcall(
        body, name="conv_a_bwd_ln", grid=(s // _CONV_T,),
        in_specs=[blk, blk, vec, vec], out_specs=[blk, st],
        out_shape=[jax.ShapeDtypeStruct((s, c), F32), jax.ShapeDtypeStruct((8, c), F32)],
        rider=rider)(ca_all, dcat, lg, lb)


def _conv_a_bwd_conv(uc, dca, w, rider=None):
    s = uc.shape[0]
    c = _GROUP_COLS
    nblk = s // _CONV_T
    hb = _CONV_T // _HALO
    off = _HALO - (_CONV_K - 1)

    def body(val_ref, gate_ref, hval_ref, hgate_ref, d_ref, dn_ref, w_ref, du_ref, dw_ref, hs_ref, ds_ref,
             hsh_ref, dsh_ref, dwa_ref):
        i = pl.program_id(0)
        _fill_glu(val_ref, gate_ref, hval_ref, hgate_ref, hs_ref)
        ds_ref[pl.ds(0, _CONV_T), :] = d_ref[...]
        ds_ref[pl.ds(_CONV_T, _HALO), :] = jnp.where(i < nblk - 1, dn_ref[...], 0.0)
        _fill_shifts(hs_ref, hsh_ref)
        _fill_shifts(ds_ref, dsh_ref)

        @pl.when(i == 0)
        def _():
            dwa_ref[...] = jnp.zeros_like(dwa_ref)

        rows = 32
        for r0 in range(0, _CONV_T, rows):
            dcur = ds_ref[pl.ds(r0, rows), :]
            dh = jnp.zeros((rows, c), F32)
            for kk in range(_CONV_K):
                dh = dh + w_ref[kk:kk + 1, :] * _tap_rows(ds_ref, dsh_ref, r0 + _CONV_K - 1 - kk, rows)
                prod = dcur * _tap_rows(hs_ref, hsh_ref, r0 + off + kk, rows)
                dwa_ref[pl.ds(8 * kk, 8), :] += sum(prod[t:t + 8] for t in range(0, rows, 8))
            v = val_ref[pl.ds(r0, rows), :].astype(F32)
            sg = _sig(gate_ref[pl.ds(r0, rows), :].astype(F32))
            du_ref[pl.ds(r0, rows), pl.ds(0, c)] = (dh * sg).astype(BF16)
            du_ref[pl.ds(r0, rows), pl.ds(c, c)] = (dh * v * sg * (1.0 - sg)).astype(BF16)

        @pl.when(i == nblk - 1)
        def _():
            dw_ref[...] = jnp.zeros_like(dw_ref)
            for kk in range(_CONV_K):
                dw_ref[kk:kk + 1, :] = jnp.sum(dwa_ref[pl.ds(8 * kk, 8), :], axis=0, keepdims=True)

    val, gate, hval, hgate = _conv_a_specs(s)
    blk = pl.BlockSpec((_CONV_T, c), lambda i: (i, 0))
    nxt = pl.BlockSpec((_HALO, c), lambda i: (jnp.minimum((i + 1) * hb, s // _HALO - 1), 0))
    wspec = pl.BlockSpec((_CONV_K, c), lambda i: (0, 0))
    return _pcall(
        body, name="conv_a_bwd_conv", grid=(nblk,),
        in_specs=[val, gate, hval, hgate, blk, nxt, wspec],
        out_specs=[pl.BlockSpec((_CONV_T, 2 * c), lambda i: (i, 0)), pl.BlockSpec((_HALO, c), lambda i: (0, 0))],
        out_shape=[jax.ShapeDtypeStruct((s, 2 * c), BF16), jax.ShapeDtypeStruct((_HALO, c), F32)],
        scratch=[pltpu.VMEM((_CONV_T + _HALO, c), F32), pltpu.VMEM((_CONV_T + _HALO, c), F32),
                 pltpu.VMEM((7, _SHIFT_ROWS, c), F32), pltpu.VMEM((7, _SHIFT_ROWS, c), F32),
                 pltpu.VMEM((8 * _HALO, c), F32)],
        rider=rider,
    )(uc, uc, uc, uc, dca, dca, w)


_SC_T = 256
_SC_RC = 32
_SC_LC = 512


def _sc_chunks(d):
    return [(pl.ds(r0, _SC_RC), pl.ds(l0, _SC_LC)) for r0 in range(0, _SC_T, _SC_RC) for l0 in range(0, d, _SC_LC)]


def _short_conv_fwd(u2, w):
    s, d3 = u2.shape
    d = d3 // 3
    hb = _SC_T // _SC_HALO

    def body(b_ref, c_ref, v_ref, hc_ref, hv_ref, w_ref, o_ref, cs_ref):
        i = pl.program_id(0)
        cs_ref[pl.ds(0, _SC_HALO), :] = jnp.where(i > 0, hc_ref[...].astype(F32) * hv_ref[...].astype(F32), 0.0)
        for rows, lanes in _sc_chunks(d):
            cs_ref[pl.ds(_SC_HALO + rows.start, _SC_RC), lanes] = (
                c_ref[rows, lanes].astype(F32) * v_ref[rows, lanes].astype(F32))
        for rows, lanes in _sc_chunks(d):
            taps = [cs_ref[pl.ds(_SC_HALO - 2 + kk + rows.start, _SC_RC), lanes] for kk in range(3)]
            conv = w_ref[0:1, lanes] * taps[0] + w_ref[1:2, lanes] * taps[1] + w_ref[2:3, lanes] * taps[2]
            o_ref[rows, lanes] = (b_ref[rows, lanes].astype(F32) * conv).astype(BF16)

    def col(j):
        return pl.BlockSpec((_SC_T, d), lambda i: (i, j))

    def halo(j):
        return pl.BlockSpec((_SC_HALO, d), lambda i: (jnp.maximum(i * hb - 1, 0), j))

    return _pcall(
        body, name="short_conv_fwd", grid=(s // _SC_T,),
        in_specs=[col(0), col(1), col(2), halo(1), halo(2), pl.BlockSpec((3, d), lambda i: (0, 0))],
        out_specs=pl.BlockSpec((_SC_T, d), lambda i: (i, 0)),
        out_shape=jax.ShapeDtypeStruct((s, d), BF16),
        scratch=[pltpu.VMEM((_SC_T + _SC_HALO, d), F32)])(u2, u2, u2, u2, u2, w)


def _short_conv_bwd(u2, dsc, w, rider=None):
    s, d3 = u2.shape
    d = d3 // 3
    hb = _SC_T // _SC_HALO
    nblk = s // _SC_T

    def body(b_ref, c_ref, v_ref, hc_ref, hv_ref, nb_ref, d_ref, nd_ref, w_ref, du_ref, dw_ref, cs_ref, ds_ref):
        i = pl.program_id(0)
        cs_ref[pl.ds(0, _SC_HALO), :] = jnp.where(i > 0, hc_ref[...].astype(F32) * hv_ref[...].astype(F32), 0.0)
        ds_ref[pl.ds(_SC_T, _SC_HALO), :] = jnp.where(i < nblk - 1, nd_ref[...] * nb_ref[...].astype(F32), 0.0)
        for rows, lanes in _sc_chunks(d):
            cs_ref[pl.ds(_SC_HALO + rows.start, _SC_RC), lanes] = (
                c_ref[rows, lanes].astype(F32) * v_ref[rows, lanes].astype(F32))
            ds_ref[rows, lanes] = d_ref[rows, lanes] * b_ref[rows, lanes].astype(F32)

        @pl.when(i == 0)
        def _():
            dw_ref[...] = jnp.zeros_like(dw_ref)

        for l0 in range(0, d, _SC_LC):
            lanes = pl.ds(l0, _SC_LC)
            dw_acc = [jnp.zeros((8, _SC_LC), F32)] * 3
            for r0 in range(0, _SC_T, _SC_RC):
                rows = pl.ds(r0, _SC_RC)
                taps = [cs_ref[pl.ds(_SC_HALO - 2 + kk + r0, _SC_RC), lanes] for kk in range(3)]
                conv = w_ref[0:1, lanes] * taps[0] + w_ref[1:2, lanes] * taps[1] + w_ref[2:3, lanes] * taps[2]
                dconv = ds_ref[rows, lanes]
                dcv = (w_ref[2:3, lanes] * dconv + w_ref[1:2, lanes] * ds_ref[pl.ds(r0 + 1, _SC_RC), lanes]
                       + w_ref[0:1, lanes] * ds_ref[pl.ds(r0 + 2, _SC_RC), lanes])
                du_ref[rows, lanes] = (d_ref[rows, lanes] * conv).astype(BF16)
                du_ref[rows, pl.ds(d + l0, _SC_LC)] = (dcv * v_ref[rows, lanes].astype(F32)).astype(BF16)
                du_ref[rows, pl.ds(2 * d + l0, _SC_LC)] = (dcv * c_ref[rows, lanes].astype(F32)).astype(BF16)
                for kk in range(3):
                    prod = dconv * taps[kk]
                    dw_acc[kk] = dw_acc[kk] + sum(prod[t:t + 8] for t in range(0, _SC_RC, 8))
            for kk in range(3):
                dw_ref[kk:kk + 1, lanes] += jnp.sum(dw_acc[kk], axis=0, keepdims=True)

    def col(j):
        return pl.BlockSpec((_SC_T, d), lambda i: (i, j))

    def halo(j):
        return pl.BlockSpec((_SC_HALO, d), lambda i: (jnp.maximum(i * hb - 1, 0), j))

    def nxt(j):
        return pl.BlockSpec((_SC_HALO, d), lambda i: (jnp.minimum((i + 1) * hb, s // _SC_HALO - 1), j))

    return _pcall(
        body, name="short_conv_bwd", grid=(nblk,),
        in_specs=[col(0), col(1), col(2), halo(1), halo(2), nxt(0), col(0), nxt(0),
                  pl.BlockSpec((3, d), lambda i: (0, 0))],
        out_specs=[pl.BlockSpec((_SC_T, d3), lambda i: (i, 0)), pl.BlockSpec((8, d), lambda i: (0, 0))],
        out_shape=[jax.ShapeDtypeStruct((s, d3), BF16), jax.ShapeDtypeStruct((8, d), F32)],
        scratch=[pltpu.VMEM((_SC_T + _SC_HALO, d), F32), pltpu.VMEM((_SC_T + _SC_HALO, d), F32)],
        rider=rider,
    )(u2, u2, u2, u2, u2, u2, dsc, dsc, w)


def _bucket_maps():
    a_idx = jnp.arange(_STEPS)[:, None]
    c_idx = jnp.arange(2 * _STEPS)[None, :]
    mdist = jnp.clip(a_idx + _STEPS - c_idx, 0, _STEPS)
    max_exact = _NUM_BUCKETS // 2
    maps = []
    for _, dil in _GROUPS:
        nn = mdist * dil
        nf = jnp.maximum(nn, 1).astype(F32)
        large = max_exact + (jnp.log(nf / max_exact) / math.log(_MAX_DISTANCE / max_exact)
                             * (_NUM_BUCKETS - max_exact)).astype(jnp.int32)
        maps.append(jnp.where(nn < max_exact, nn, jnp.minimum(large, _NUM_BUCKETS - 1)).astype(jnp.int32))
    return jnp.stack(maps, axis=0)


def _bias_expand(rel_bias, buckets):
    nh = rel_bias.shape[1]

    def body(rb_ref, bk_ref, o_ref):
        h = pl.program_id(0)
        bk = bk_ref[0]
        acc = jnp.zeros(bk.shape, F32)
        for b in range(_NUM_BUCKETS):
            acc = jnp.where(bk == b, rb_ref[b, h], acc)
        a = lax.broadcasted_iota(jnp.int32, bk.shape, 0)
        c = lax.broadcasted_iota(jnp.int32, bk.shape, 1)
        mdist = a + _STEPS - c
        o_ref[0] = jnp.where((mdist >= 0) & (mdist <= _STEPS), acc, _NEG)

    return _pcall(
        body, name="bias_expand", grid=(nh,),
        in_specs=[pl.BlockSpec(memory_space=pltpu.SMEM),
                  pl.BlockSpec((1, _STEPS, 2 * _STEPS), lambda h: (h // 8, 0, 0))],
        out_specs=pl.BlockSpec((1, _STEPS, 2 * _STEPS), lambda h: (h, 0, 0)),
        out_shape=jax.ShapeDtypeStruct((nh, _STEPS, 2 * _STEPS), F32))(rel_bias, buckets)


def _bias_reduce(ds_all, buckets):
    nh = ds_all.shape[0]

    def body(ds_ref, bk_ref, o_ref):
        t, bk = ds_ref[0], bk_ref[0]
        rows = lax.broadcasted_iota(jnp.int32, (_NUM_BUCKETS, _LANES), 0)
        out = jnp.zeros((_NUM_BUCKETS, _LANES), F32)
        for b in range(_NUM_BUCKETS):
            out = jnp.where(rows == b, jnp.sum(jnp.where(bk == b, t, 0.0)), out)
        o_ref[0] = out

    blk = pl.BlockSpec((1, _STEPS, 2 * _STEPS), lambda h: (h, 0, 0))
    return _pcall(
        body, name="bias_reduce", grid=(nh,),
        in_specs=[blk, pl.BlockSpec((1, _STEPS, 2 * _STEPS), lambda h: (h // 8, 0, 0))],
        out_specs=pl.BlockSpec((1, _NUM_BUCKETS, _LANES), lambda h: (h, 0, 0)),
        out_shape=jax.ShapeDtypeStruct((nh, _NUM_BUCKETS, _LANES), F32))(ds_all, buckets)


def _sub_residues(dil):
    return 4 if dil % 16 == 0 else 1


def _strided_rows(ref, tmp_ref, p, r, dil):
    sub = _sub_residues(dil)
    if dil == 1:
        return [ref[p]]
    if sub == 1:
        return [ref[p, pl.ds(r, _STEPS, stride=dil), :]]
    tmp_ref[...] = ref[p, pl.ds(r, _STEPS * sub, stride=dil // sub), :]
    return [tmp_ref[pl.ds(q, _STEPS, stride=sub), :] for q in range(sub)]


def _store_strided(ref, tmp_ref, p, r, dil, vals):
    sub = _sub_residues(dil)
    if dil == 1:
        ref[p] = vals[0]
    elif sub == 1:
        ref[p, pl.ds(r, _STEPS, stride=dil), :] = vals[0]
    else:
        for q, val in enumerate(vals):
            tmp_ref[pl.ds(q, _STEPS, stride=sub), :] = val
        ref[p, pl.ds(r, _STEPS * sub, stride=dil // sub), :] = tmp_ref[...]


def _tmp_rows(dil, count):
    sub = _sub_residues(dil)
    return [pltpu.VMEM((_STEPS * sub, _LANES), F32)] * count if sub > 1 else []


def _head_masks():
    lane = lax.broadcasted_iota(jnp.int32, (1, _LANES), 1)
    return [lane < _HEAD_DIM, lane >= _HEAD_DIM]


def _stack_heads(x2, masks):
    return jnp.concatenate([jnp.where(masks[0], x2, 0), jnp.where(masks[1], x2, 0)], axis=0)


def _unstack_heads(y, masks):
    return jnp.where(masks[0], y[:_STEPS], y[_STEPS:])


def _scores(qs2, k2, b_ref, j, first):
    sc = lax.dot_general(qs2, k2, (((1,), (1,)), ((), ())), preferred_element_type=F32)
    sc = sc * (_HEAD_DIM ** -0.5) + jnp.concatenate([b_ref[2 * j], b_ref[2 * j + 1]], axis=0)
    col = lax.broadcasted_iota(jnp.int32, sc.shape, 1)
    return jnp.where(jnp.logical_and(first, col < _STEPS), _NEG, sc)


_PAIRS = _GROUP_COLS // _LANES


def _attn_fwd(uq, uk, uv, bias, g, dil, pp, rider=None):
    s = uq.shape[1]
    rb = _STEPS * dil
    nb = s // rb
    npb = _PAIRS // pp

    sub = _sub_residues(dil)

    def body(q_ref, kc_ref, kp_ref, vc_ref, vp_ref, b_ref, o_ref, l_ref, *tmp):
        tmp = tmp + (None,) * 7
        n, r = pl.program_id(1), pl.program_id(2)
        first = n == 0
        masks = _head_masks()
        for j in range(pp):
            qs = _strided_rows(q_ref, tmp[0], j, r, dil)
            kps, kcs = _strided_rows(kp_ref, tmp[1], j, r, dil), _strided_rows(kc_ref, tmp[2], j, r, dil)
            vps, vcs = _strided_rows(vp_ref, tmp[3], j, r, dil), _strided_rows(vc_ref, tmp[4], j, r, dil)
            o_res, l_res = [], []
            for q in range(sub):
                q2 = qs[q].astype(BF16)
                k2 = jnp.concatenate([kps[q], kcs[q]], axis=0).astype(BF16)
                v2 = jnp.concatenate([vps[q], vcs[q]], axis=0).astype(BF16)
                sc = _scores(_stack_heads(q2, masks), k2, b_ref, j, first)
                mx = jnp.max(sc, axis=-1, keepdims=True)
                p = jnp.exp(sc - mx)
                den = jnp.sum(p, axis=-1, keepdims=True)
                o2 = jnp.dot(p.astype(BF16), v2, preferred_element_type=F32) / den
                o_res.append(_unstack_heads(o2, masks))
                l_res.append(_unstack_heads(jnp.broadcast_to(mx + jnp.log(den), o2.shape), masks))
            _store_strided(o_ref, tmp[5], j, r, dil, o_res)
            _store_strided(l_ref, tmp[6], j, r, dil, l_res)

    cur = pl.BlockSpec((pp, rb, _LANES), lambda hb, n, r: (g * npb + hb, n, 0))
    prev = pl.BlockSpec((pp, rb, _LANES), lambda hb, n, r: (g * npb + hb, jnp.maximum(n - 1, 0), 0))
    bspec = pl.BlockSpec((2 * pp, _STEPS, 2 * _STEPS), lambda hb, n, r: (g * npb + hb, 0, 0))
    ospec = pl.BlockSpec((pp, rb, _LANES), lambda hb, n, r: (hb, n, 0))
    sh = jax.ShapeDtypeStruct((_PAIRS, s, _LANES), F32)
    return _pcall(
        body, name=f"attn_fwd_g{g}", grid=(npb, nb, dil // sub),
        in_specs=[cur, cur, prev, cur, prev, bspec], out_specs=[ospec, ospec], out_shape=[sh, sh],
        scratch=_tmp_rows(dil, 7), rider=rider,
    )(uq, uk, uk, uv, uv, bias)


def _attn_merge(outs, lses, cat):
    s = outs[0].shape[1]
    c = _GROUP_COLS

    def body(o0, o1, o2, l0, l1, l2, cat_in, cat_ref, lse_ref):
        del cat_in
        a0, a1, a2 = l0[...], l1[...], l2[...]
        mx = jnp.maximum(jnp.maximum(a0, a1), a2)
        w0, w1, w2 = jnp.exp(a0 - mx), jnp.exp(a1 - mx), jnp.exp(a2 - mx)
        den = w0 + w1 + w2
        y = ((w0 * o0[...] + w1 * o1[...] + w2 * o2[...]) / den).astype(BF16)
        for p in range(_PAIRS):
            cat_ref[:, p * _LANES:(p + 1) * _LANES] = y[p]
        lse_ref[...] = mx + jnp.log(den)

    blk = pl.BlockSpec((_PAIRS, _ROW_T, _LANES), lambda i: (0, i, 0))
    return _pcall(
        body, name="attn_merge", grid=(s // _ROW_T,),
        in_specs=[blk] * 6 + [_ANY],
        out_specs=[pl.BlockSpec((_ROW_T, c), lambda i: (i, 1)), blk],
        out_shape=[jax.ShapeDtypeStruct(cat.shape, BF16), jax.ShapeDtypeStruct((_PAIRS, s, _LANES), F32)],
        aliases={6: 0})(*outs, *lses, cat)


def _attn_delta(dcat, cat):
    s = dcat.shape[0]
    c = _GROUP_COLS
    seg = (jnp.arange(c)[:, None] // _HEAD_DIM == jnp.arange(c)[None, :] // _HEAD_DIM).astype(BF16)

    def body(dy_ref, y_ref, seg_ref, dl_ref, dys_ref):
        dy = dy_ref[...]
        prod = dy * y_ref[...].astype(F32)
        hi = prod.astype(BF16)
        lo = (prod - hi.astype(F32)).astype(BF16)
        dl = (jnp.dot(hi, seg_ref[...], preferred_element_type=F32)
              + jnp.dot(lo, seg_ref[...], preferred_element_type=F32))
        for p in range(_PAIRS):
            dl_ref[p] = dl[:, p * _LANES:(p + 1) * _LANES]
            dys_ref[p] = dy[:, p * _LANES:(p + 1) * _LANES]

    right = pl.BlockSpec((_ROW_T, c), lambda i: (i, 1))
    blk = pl.BlockSpec((_PAIRS, _ROW_T, _LANES), lambda i: (0, i, 0))
    sh = jax.ShapeDtypeStruct((_PAIRS, s, _LANES), F32)
    return _pcall(
        body, name="attn_delta", grid=(s // _ROW_T,),
        in_specs=[right, right, pl.BlockSpec((c, c), lambda i: (0, 0))],
        out_specs=[blk, blk], out_shape=[sh, sh])(dcat, cat, seg)


def _attn_bwd(uq, uk, uv, dys, lse, delta, bias, prev_grads, g, dil, pp, rider=None):
    s = uq.shape[1]
    rb = _STEPS * dil
    nb = s // rb
    npb = _PAIRS // pp
    scale = _HEAD_DIM ** -0.5

    sub = _sub_residues(dil)

    def body(q_ref, kc_ref, kp_ref, vc_ref, vp_ref, dy_ref, l_ref, dl_ref, b_ref, *rest):
        rest = rest[len(prev_grads):]
        dqkv_ref, dsa_ref, dqc_ref, dkc_ref, dvc_ref = rest[:5]
        dq_ref, dk_ref, dv_ref = dqkv_ref.at[0], dqkv_ref.at[1], dqkv_ref.at[2]
        tmp = rest[5:] + (None,) * 11
        n, r = pl.program_id(1), pl.program_id(2)

        def carry_slot(j, q):
            return ((r + (dil // sub) * q) * pp + j) if sub > 1 else r * pp + j

        @pl.when(jnp.logical_and(n == 0, r == 0))
        def _():
            dsa_ref[...] = jnp.zeros_like(dsa_ref)

        @pl.when(n == 0)
        def _():
            for j in range(pp):
                for q in range(sub):
                    for carry in (dqc_ref, dkc_ref, dvc_ref):
                        carry[carry_slot(j, q)] = jnp.zeros((_STEPS, _LANES), F32)

        @pl.when(n < nb)
        def _():
            first = n == 0
            masks = _head_masks()
            for j in range(pp):
                qs = _strided_rows(q_ref, tmp[0], j, r, dil)
                kps, kcs = _strided_rows(kp_ref, tmp[1], j, r, dil), _strided_rows(kc_ref, tmp[2], j, r, dil)
                vps, vcs = _strided_rows(vp_ref, tmp[3], j, r, dil), _strided_rows(vc_ref, tmp[4], j, r, dil)
                dys_ = _strided_rows(dy_ref, tmp[5], j, r, dil)
                lses = _strided_rows(l_ref, tmp[6], j, r, dil)
                dls = _strided_rows(dl_ref, tmp[7], j, r, dil)
                ds_sum = [jnp.zeros((_STEPS, 2 * _STEPS), F32)] * 2
                dq_res, dk_res, dv_res = [], [], []
                for q in range(sub):
                    q2 = qs[q].astype(BF16)
                    k2 = jnp.concatenate([kps[q], kcs[q]], axis=0).astype(BF16)
                    v2 = jnp.concatenate([vps[q], vcs[q]], axis=0).astype(BF16)
                    dy2 = dys_[q].astype(BF16)
                    qs2, dys2 = _stack_heads(q2, masks), _stack_heads(dy2, masks)
                    per_row = lambda st: jnp.concatenate([st[:, 0:1], st[:, _HEAD_DIM:_HEAD_DIM + 1]], axis=0)
                    sc = _scores(qs2, k2, b_ref, j, first)
                    p = jnp.exp(sc - per_row(lses[q]))
                    dp = lax.dot_general(dys2, v2, (((1,), (1,)), ((), ())), preferred_element_type=F32)
                    ds = p * (dp - per_row(dls[q]))
                    ds_sum[0] = ds_sum[0] + ds[:_STEPS]
                    ds_sum[1] = ds_sum[1] + ds[_STEPS:]
                    dsb = ds.astype(BF16)
                    dq_p = _unstack_heads(jnp.dot(dsb, k2, preferred_element_type=F32), masks)
                    tdn = (((0,), (0,)), ((), ()))
                    dk_p = lax.dot_general(dsb, qs2, tdn, preferred_element_type=F32) * scale
                    dv_p = lax.dot_general(p.astype(BF16), dys2, tdn, preferred_element_type=F32)
                    slot = carry_slot(j, q)
                    dq_res.append(dqc_ref[slot])
                    dk_res.append(dkc_ref[slot] + dk_p[:_STEPS])
                    dv_res.append(dvc_ref[slot] + dv_p[:_STEPS])
                    dqc_ref[slot] = dq_p * scale
                    dkc_ref[slot] = dk_p[_STEPS:]
                    dvc_ref[slot] = dv_p[_STEPS:]
                for hh in range(2):
                    dsa_ref[2 * j + hh] += ds_sum[hh]
                _store_strided(dq_ref, tmp[8], j, r, dil, dq_res)
                _store_strided(dk_ref, tmp[9], j, r, dil, dk_res)
                _store_strided(dv_ref, tmp[10], j, r, dil, dv_res)

        @pl.when(n == nb)
        def _():
            for j in range(pp):
                for ref, carry, t in ((dq_ref, dqc_ref, 8), (dk_ref, dkc_ref, 9), (dv_ref, dvc_ref, 10)):
                    _store_strided(ref, tmp[t], j, r, dil, [carry[carry_slot(j, q)] for q in range(sub)])

    def clamp(n):
        return jnp.minimum(n, nb - 1)

    cur = pl.BlockSpec((pp, rb, _LANES), lambda hb, n, r: (g * npb + hb, clamp(n), 0))
    prev = pl.BlockSpec((pp, rb, _LANES), lambda hb, n, r: (g * npb + hb, jnp.maximum(clamp(n) - 1, 0), 0))
    stat = pl.BlockSpec((pp, rb, _LANES), lambda hb, n, r: (hb, clamp(n), 0))
    bspec = pl.BlockSpec((2 * pp, _STEPS, 2 * _STEPS), lambda hb, n, r: (g * npb + hb, 0, 0))
    late = pl.BlockSpec((3, pp, rb, _LANES), lambda hb, n, r: (0, g * npb + hb, jnp.maximum(n - 1, 0), 0))
    dsspec = pl.BlockSpec((2 * pp, _STEPS, 2 * _STEPS), lambda hb, n, r: (hb, 0, 0))
    np_ = len(prev_grads)
    carry = pltpu.VMEM((dil * pp, _STEPS, _LANES), F32)
    return _pcall(
        body, name=f"attn_bwd_g{g}", grid=(npb, nb + 1, dil // sub),
        in_specs=[cur, cur, prev, cur, prev, stat, stat, stat, bspec] + [_ANY] * np_,
        out_specs=[late, dsspec],
        out_shape=[jax.ShapeDtypeStruct((3, 3 * _PAIRS, s, _LANES), F32),
                   jax.ShapeDtypeStruct((8, _STEPS, 2 * _STEPS), F32)],
        scratch=[carry, carry, carry] + _tmp_rows(dil, 11),
        aliases={9 + t: t for t in range(np_)}, rider=rider,
    )(uq, uk, uk, uv, uv, dys, lse, delta, bias, *prev_grads)


def _place():
    x, y, c = lax.axis_index("x"), lax.axis_index("y"), lax.axis_index("c")
    chips = [(1 - x, y), (x, 1 - y), (1 - x, 1 - y)]
    return x, y, c, chips


def _slab(ref, axis, chip, width):
    start = pl.multiple_of(chip * width, width)
    if axis == 0:
        return ref.at[pl.ds(start, width), :]
    return ref.at[:, pl.ds(start, width)]


def _run_rider(rider, name):
    nin, nout = len(rider.ins), len(rider.out_shapes)

    def body(*refs):
        ins, outs, scr = refs[:nin], refs[nin:nin + nout], refs[nin + nout:]
        rider.start(ins, outs, scr)
        rider.finish(ins, outs, scr)

    return _pcall(body, name=name, in_specs=[_ANY] * nin, out_specs=[_ANY] * nout, out_shape=rider.out_shapes,
                  scratch=rider.scratch)(*rider.ins)


def _gather_halves_rider(shard, axis):
    shape = list(shard.shape)
    shape[axis] *= 4
    full = jax.ShapeDtypeStruct(tuple(shape), shard.dtype)
    half = shard.shape[0] // 2
    width = shard.shape[axis]

    def region(out, chip, core):
        if axis == 0:
            return out.at[pl.ds(pl.multiple_of(chip * width + core * half, half), half), :]
        return out.at[pl.ds(pl.multiple_of(core * half, half), half), pl.ds(pl.multiple_of(chip * width, width), width)]

    def copies(ins, outs, scr):
        send, recv, loc = scr
        (src,), (out,) = ins, outs
        x, y, c, chips = _place()
        mine = 2 * x + y
        own = pltpu.make_async_copy(src, _slab(out, axis, mine, width), loc.at[0])
        my_half = src.at[pl.ds(pl.multiple_of(c * half, half), half), :]
        over_ici, ici_in, to_sib, sib_in = [], [], [], []
        for j, (px, py) in enumerate(chips):
            theirs = 2 * px + py
            ici = dict(send_sem=send.at[j], recv_sem=recv.at[j], device_id=(px, py, c), device_id_type=MESH)
            d2d = dict(send_sem=send.at[3 + j], recv_sem=recv.at[3 + j], device_id=(x, y, 1 - c),
                       device_id_type=MESH)
            over_ici.append(pltpu.make_async_remote_copy(src_ref=my_half, dst_ref=region(out, mine, c), **ici))
            ici_in.append(pltpu.make_async_remote_copy(src_ref=my_half, dst_ref=region(out, theirs, c), **ici))
            to_sib.append(pltpu.make_async_remote_copy(
                src_ref=region(out, theirs, c), dst_ref=region(out, theirs, c), **d2d))
            sib_in.append(pltpu.make_async_remote_copy(
                src_ref=region(out, theirs, c), dst_ref=region(out, theirs, 1 - c), **d2d))
        return own, over_ici, ici_in, to_sib, sib_in

    def start(ins, outs, scr):
        own, over_ici, _, _, _ = copies(ins, outs, scr)
        own.start()
        for cp in over_ici:
            cp.start()

    def finish(ins, outs, scr):
        own, over_ici, ici_in, to_sib, sib_in = copies(ins, outs, scr)
        for j in range(3):
            ici_in[j].wait_recv()
            to_sib[j].start()
        for cp in sib_in:
            cp.wait_recv()
        own.wait()
        for cp in over_ici + to_sib:
            cp.wait_send()

    return _Rider([shard], [full], [pltpu.SemaphoreType.DMA((6,)), pltpu.SemaphoreType.DMA((6,)),
                                    pltpu.SemaphoreType.DMA((1,))], start, finish)


def _join_riders(riders):
    if len(riders) == 1:
        return riders[0]

    def parts(ins, outs, scr):
        pi = po = ps = 0
        for rd in riders:
            ni, no, ns = len(rd.ins), len(rd.out_shapes), len(rd.scratch)
            yield rd, ins[pi:pi + ni], outs[po:po + no], scr[ps:ps + ns]
            pi, po, ps = pi + ni, po + no, ps + ns

    def start(ins, outs, scr):
        for rd, i, o, sc in parts(ins, outs, scr):
            rd.start(i, o, sc)

    def finish(ins, outs, scr):
        for rd, i, o, sc in parts(ins, outs, scr):
            rd.finish(i, o, sc)

    return _Rider(sum((rd.ins for rd in riders), []), sum((rd.out_shapes for rd in riders), []),
                  sum((rd.scratch for rd in riders), []), start, finish)


def _scatter_rider(grads, axes, rows=None):
    nw = len(grads)
    outs_shape = []
    for gr, ax in zip(grads, axes):
        shape = list(gr.shape)
        shape[ax] //= 4
        if rows is not None:
            assert ax == 1
            shape[0] = rows[1] - rows[0]
        outs_shape.append(jax.ShapeDtypeStruct((4,) + tuple(shape), gr.dtype))

    def copies(ins, outs, scr):
        send, recv, loc = scr
        x, y, c, chips = _place()
        cps, own = [], []
        for t in range(nw):
            width = ins[t].shape[axes[t]] // 4
            src = ins[t] if rows is None else ins[t].at[pl.ds(rows[0], rows[1] - rows[0]), :]
            own.append(pltpu.make_async_copy(_slab(src, axes[t], 2 * x + y, width), outs[t].at[3], loc.at[t]))
            for j, (px, py) in enumerate(chips):
                cps.append(pltpu.make_async_remote_copy(
                    src_ref=_slab(src, axes[t], 2 * px + py, width), dst_ref=outs[t].at[j],
                    send_sem=send.at[3 * t + j], recv_sem=recv.at[3 * t + j],
                    device_id=(px, py, c), device_id_type=MESH))
        return cps, own

    def start(ins, outs, scr):
        cps, own = copies(ins, outs, scr)
        for cp in cps + own:
            cp.start()

    def finish(ins, outs, scr):
        cps, own = copies(ins, outs, scr)
        for cp in cps:
            cp.wait_recv()
        for cp in own:
            cp.wait()
        for cp in cps:
            cp.wait_send()

    return _Rider(grads, outs_shape, [pltpu.SemaphoreType.DMA((3 * nw,)), pltpu.SemaphoreType.DMA((3 * nw,)),
                                      pltpu.SemaphoreType.DMA((nw,))], start, finish)


def _swap_rider(parts):
    nw = len(parts)

    def copies(ins, outs, scr):
        send, recv = scr
        x, y, c, _ = _place()
        return [pltpu.make_async_remote_copy(
            src_ref=ins[t], dst_ref=outs[t], send_sem=send.at[t], recv_sem=recv.at[t],
            device_id=(x, y, 1 - c), device_id_type=MESH) for t in range(nw)]

    def start(ins, outs, scr):
        for cp in copies(ins, outs, scr):
            cp.start()

    def finish(ins, outs, scr):
        cps = copies(ins, outs, scr)
        for cp in cps:
            cp.wait_recv()
        for cp in cps:
            cp.wait_send()

    return _Rider(parts, [jax.ShapeDtypeStruct(p.shape, p.dtype) for p in parts],
                  [pltpu.SemaphoreType.DMA((nw,)), pltpu.SemaphoreType.DMA((nw,))], start, finish)


def _sum_all_devices(buf, name):
    rows, cols = buf.shape

    def body(in_ref, o_ref, gat_ref, send, recv):
        x, y, c, _ = _place()
        me = 4 * x + 2 * y + c
        gat_ref[me] = in_ref[...]
        started = []
        for mask in range(1, 8):
            fx, fy, fc = (mask >> 2) & 1, (mask >> 1) & 1, mask & 1
            peer = (x + fx * (1 - 2 * x), y + fy * (1 - 2 * y), c + fc * (1 - 2 * c))
            cp = pltpu.make_async_remote_copy(
                src_ref=in_ref, dst_ref=gat_ref.at[me], send_sem=send.at[mask - 1], recv_sem=recv.at[mask - 1],
                device_id=peer, device_id_type=MESH)
            cp.start()
            started.append(cp)
        for cp in started:
            cp.wait_recv()
        for cp in started:
            cp.wait_send()
        acc = gat_ref[0]
        for t in range(1, 8):
            acc = acc + gat_ref[t]
        o_ref[...] = acc

    vm = pl.BlockSpec(memory_space=pltpu.VMEM)
    return _pcall(
        body, name=name, in_specs=[vm], out_specs=vm, out_shape=jax.ShapeDtypeStruct((rows, cols), F32),
        scratch=[pltpu.VMEM((8, rows, cols), F32), pltpu.SemaphoreType.DMA((7,)), pltpu.SemaphoreType.DMA((7,))],
    )(buf)


_UPD_T = 256


def _sum_partials(got, name):
    _, rows, cols = got.shape
    tr = min(_UPD_T, rows)

    def body(got_ref, o_ref):
        acc = got_ref[3].astype(F32)
        for j in range(3):
            acc = acc + got_ref[j].astype(F32)
        o_ref[...] = acc

    return _pcall(
        body, name=name, grid=(rows // tr,),
        in_specs=[pl.BlockSpec((4, tr, cols), lambda i: (0, i, 0))], out_specs=pl.BlockSpec((tr, cols), lambda i: (i, 0)),
        out_shape=jax.ShapeDtypeStruct((rows, cols), F32))(got)


def _adamw_math(w, gr, m, v):
    m = _B1 * m + (1.0 - _B1) * gr
    v = _B2 * v + (1.0 - _B2) * (gr * gr)
    m_hat = m / (1.0 - _B1 ** _STEP)
    v_hat = v / (1.0 - _B2 ** _STEP)
    delta = -_LR * (m_hat / (jnp.sqrt(v_hat) + _EPS) + _WD * w)
    return delta, m, v


def _adamw(w, m, v, parts, name):
    rows, cols = w.shape
    tr = min(_UPD_T, rows)
    npart = len(parts)

    def body(w_ref, m_ref, v_ref, *rest):
        p_refs, (g_ref, d_ref, nm_ref, nv_ref) = rest[:npart], rest[npart:]
        gr = p_refs[0][...]
        for p in p_refs[1:]:
            gr = gr + p[...]
        delta, nm, nv = _adamw_math(w_ref[...], gr, m_ref[...], v_ref[...])
        g_ref[...] = gr
        d_ref[...] = delta
        nm_ref[...] = nm
        nv_ref[...] = nv

    blk = pl.BlockSpec((tr, cols), lambda i: (i, 0))
    sh = jax.ShapeDtypeStruct((rows, cols), F32)
    return _pcall(body, name=name, grid=(rows // tr,), in_specs=[blk] * (3 + npart), out_specs=[blk] * 4,
                  out_shape=[sh] * 4)(w, m, v, *parts)


def _adamw_layers(w, m, v, parts, name):
    _, rows, cols = w.shape
    tr = min(_UPD_T, rows)
    npart = len(parts[0])

    def body(w_ref, m_ref, v_ref, *rest):
        p_refs, (g_ref, d_ref, nm_ref, nv_ref) = rest[:2 * npart], rest[2 * npart:]
        grs = []
        for layer in range(2):
            gr = p_refs[layer * npart][...]
            for p in p_refs[layer * npart + 1:(layer + 1) * npart]:
                gr = gr + p[...]
            grs.append(gr)
        gr = jnp.where(pl.program_id(0) == 0, grs[0], grs[1])
        delta, nm, nv = _adamw_math(w_ref[...], gr, m_ref[...], v_ref[...])
        g_ref[...] = gr
        d_ref[...] = delta
        nm_ref[...] = nm
        nv_ref[...] = nv

    blk = pl.BlockSpec((None, tr, cols), lambda l, i: (l, i, 0))

    def part_spec(layer):
        return pl.BlockSpec((tr, cols), lambda l, i: (jnp.where(l == layer, i, 0), 0))

    sh = jax.ShapeDtypeStruct(w.shape, F32)
    return _pcall(
        body, name=name, grid=(2, rows // tr),
        in_specs=[blk] * 3 + [part_spec(0)] * npart + [part_spec(1)] * npart, out_specs=[blk] * 4,
        out_shape=[sh] * 4)(w, m, v, *parts[0], *parts[1])


_PACK_W = 1024


def _pack(arrs, rows):
    flat = []
    for a in arrs:
        f = a.reshape(-1).astype(F32)
        pad = (-f.shape[0]) % _PACK_W
        flat.append(jnp.pad(f, (0, pad)))
    f = jnp.concatenate(flat)
    f = jnp.pad(f, (0, rows * _PACK_W - f.shape[0]))
    return f.reshape(rows, _PACK_W)


def _unpack(buf, shapes):
    flat = buf.reshape(-1)
    out, pos = [], 0
    for sh in shapes:
        size = math.prod(sh)
        out.append(flat[pos:pos + size].reshape(sh))
        pos += size + ((-size) % _PACK_W)
    return out


def _pack_rows(shapes):
    total = sum(-(-math.prod(sh) // _PACK_W) for sh in shapes)
    return -(-total // 8) * 8


def kernel(x, rel_bias, ab_norm, ab_w_in, ab_conv_w, ab_conv_b, ab_ln_g, ab_ln_b, ab_w_out, sc_norm, sc_w_in, sc_conv_w, sc_w_out, mlp_norm, mlp_w_up, mlp_w_down, final_norm, loss_target, m_rel_bias, m_ab_norm, m_ab_w_in, m_ab_conv_w, m_ab_conv_b, m_ab_ln_g, m_ab_ln_b, m_ab_w_out, m_sc_norm, m_sc_w_in, m_sc_conv_w, m_sc_w_out, m_mlp_norm, m_mlp_w_up, m_mlp_w_down, m_final_norm, v_rel_bias, v_ab_norm, v_ab_w_in, v_ab_conv_w, v_ab_conv_b, v_ab_ln_g, v_ab_ln_b, v_ab_w_out, v_sc_norm, v_sc_w_in, v_sc_conv_w, v_sc_w_out, v_mlp_norm, v_mlp_w_up, v_mlp_w_down, v_final_norm):
    s, d = x.shape[1], x.shape[2]
    dff = 4 * d
    c = _GROUP_COLS
    chip = 2 * lax.axis_index("x") + lax.axis_index("y")
    on_c0 = (lax.axis_index("c") == 0).astype(F32)
    h0 = x[0]
    tgt = loss_target[0]

    cw_sh, scn_sh, scw_sh = ab_conv_w[0], sc_norm, sc_conv_w[0]
    conv_w_full = lax.dynamic_update_slice(jnp.zeros((_CONV_K, c), F32), cw_sh * on_c0, (0, chip * cw_sh.shape[1]))
    scn_full = lax.dynamic_update_slice(jnp.zeros((1, d), F32), scn_sh * on_c0, (0, chip * scn_sh.shape[1]))
    scw_full = lax.dynamic_update_slice(jnp.zeros((3, d), F32), scw_sh * on_c0, (0, chip * scw_sh.shape[1]))
    small_shapes = [(_CONV_K, c), (1, d), (3, d)]
    small = _sum_all_devices(_pack([conv_w_full, scn_full, scw_full], _pack_rows(small_shapes)), "gather_small")
    conv_w, sc_g, sc_cw = _unpack(small, small_shapes)

    w_shards = [ab_w_in[0], ab_w_out[0], sc_w_in[0], sc_w_out[0], mlp_w_up[0], mlp_w_up[1],
                mlp_w_down[0], mlp_w_down[1]]
    w_axes = [1, 0, 1, 0, 1, 1, 0, 0]
    wb = [w.astype(BF16) for w in w_shards]
    full_w = [None] * 8

    def gather(idx):
        return _join_riders([_gather_halves_rider(wb[t], w_axes[t]) for t in idx])

    def put(idx, got_w):
        for t, w in zip(idx, got_w):
            full_w[t] = w

    buckets = _bucket_maps()
    bias = _bias_expand(rel_bias, buckets)
    n0, got_w = _rms_fwd(h0, ab_norm, "rms_fwd_ab", rider=_gather_halves_rider(wb[0], w_axes[0]))
    put([0], got_w)
    w_in = full_w[0]
    tm = min(1024, s)
    tm2 = min(2048, s)
    tmh = min(512, s)
    uc = _mm(n0, w_in, "nn", m=s, n=2 * c, k=d, tm=tm2, tn=2 * c, tk=d, out_dtype=BF16, name="proj_conv")
    uq, uk, uv = [], [], []
    for t, (nm, dst) in enumerate(zip("qkv", (uq, uk, uv))):
        res = _mm(n0, w_in, "nn", m=s, n=3 * c, k=d, tm=tm2, tn=c, tk=d, out_dtype=F32, name=f"proj_{nm}",
                  b_off=(0, 2 + 3 * t), split="o", rider=gather([1]) if t == 0 else None)
        if t == 0:
            res, got_w = res
            put([1], got_w)
        dst.append(res)
    uq, uk, uv = uq[0], uk[0], uv[0]
    (cat, ca), got_w = _conv_a_fwd(uc, conv_w, ab_conv_b, ab_ln_g, ab_ln_b, rider=gather([2]))
    put([2], got_w)
    outs, lses = [], []
    for g, (_, dil) in enumerate(_GROUPS):
        idx = ([4], [6], [3, 5])[g]
        (o, l), got_w = _attn_fwd(uq, uk, uv, bias, g, dil, 4 if dil <= 4 else 2, rider=gather(idx))
        put(idx, got_w)
        outs.append(o)
        lses.append(l)
    cat, lse = _attn_merge(outs, lses, cat)
    h1, n1 = _mm(cat, full_w[1], "nn", m=s, n=d, k=d, tm=tm, tn=d, tk=d, out_dtype=(F32, BF16), name="out_ab",
                 epi=_epi_add_rms, extras=(h0,), vecs=(mlp_norm[0:1],))

    def mlp_fwd(h, nrm, layer, next_gain=None, rider=None):
        zr = _mm(nrm, full_w[4 + layer], "nn", m=s, n=dff, k=d, tm=tmh, tn=dff, tk=d, out_dtype=BF16,
                 name=f"mlp_up{layer}", epi=_epi_relu, rider=rider)
        if rider is not None:
            zr, got_r = zr
            put([7], got_r)
        kw = dict(m=s, n=d, k=dff, tm=tmh, tn=d, tk=dff, name=f"mlp_down{layer}", a_pro=_square, extras=(h,))
        if next_gain is None:
            return zr, _mm(zr, full_w[6 + layer], "nn", out_dtype=F32, epi=_epi_add, **kw), None
        hn, nn = _mm(zr, full_w[6 + layer], "nn", out_dtype=(F32, BF16), epi=_epi_add_rms, vecs=(next_gain,), **kw)
        return zr, hn, nn

    zr0, h2, n2 = mlp_fwd(h1, n1, 0, next_gain=sc_g, rider=gather([7]))
    _, w_out, w_si, w_so, w_up0, w_up1, w_dn0, w_dn1 = full_w
    w_up, w_dn = [w_up0, w_up1], [w_dn0, w_dn1]
    u2 = _mm(n2, w_si, "nn", m=s, n=3 * d, k=d, tm=tmh, tn=3 * d, tk=d, out_dtype=BF16, name="proj_sc")
    scv = _short_conv_fwd(u2, sc_cw)
    h3, n3 = _mm(scv, w_so, "nn", m=s, n=d, k=d, tm=tm, tn=d, tk=d, out_dtype=(F32, BF16), name="out_sc",
                 epi=_epi_add_rms, extras=(h2,), vecs=(mlp_norm[1:2],))
    zr1, h4, _ = mlp_fwd(h3, n3, 1)

    dh4, dh4b, g_final, loss_part = _loss_head(h4, tgt, final_norm.reshape(1, d))
    tkw = min(2048, s)

    big_grads, sums = [None] * 8, [None] * 8

    def scatter(t):
        return _scatter_rider([big_grads[t]], [w_axes[t]])

    def arrived(t, got_t):
        sums[t] = _sum_partials(got_t[0], f"sum_partials{t}")

    def mlp_bwd(dh, dhb, h, nrm, zr, layer):
        dz = _mm(dhb, w_dn[layer], "nt", m=s, n=dff, k=d, tm=tmh, tn=dff, tk=d, out_dtype=BF16,
                 name=f"mlp_down{layer}_dx", epi=_epi_relu_sq_bwd, extras=(zr,))
        big_grads[6 + layer] = _mm(zr, dhb, "tn", m=dff, n=d, k=s, tm=1024, tn=d, tk=tkw, out_dtype=BF16,
                                   name=f"mlp_down{layer}_dw", a_pro=_square)
        big_grads[4 + layer] = _mm(nrm, dz, "tn", m=d, n=dff, k=s, tm=d, tn=1024, tk=tkw, out_dtype=BF16,
                                   name=f"mlp_up{layer}_dw")
        res = _mm(dz, w_up[layer], "nt", m=s, n=d, k=dff, tm=tmh, tn=d, tk=dff, out_dtype=(F32, BF16),
                  name=f"mlp_up{layer}_dx", rider=scatter(6) if layer == 0 else None, epi=_epi_rms_bwd,
                  extras=(h, dh), vecs=(mlp_norm[layer:layer + 1],), row_sum=True)
        if layer == 0:
            res, got_t = res
            arrived(6, got_t)
        return res

    dh3, dh3b, g_mn1 = mlp_bwd(dh4, dh4b, h3, n3, zr1, 1)

    dsc = _mm(dh3b, w_so, "nt", m=s, n=d, k=d, tm=tm, tn=d, tk=d, out_dtype=F32, name="out_sc_dx")
    big_grads[3] = _mm(scv, dh3b, "tn", m=d, n=d, k=s, tm=d, tn=d, tk=tkw, out_dtype=BF16, name="out_sc_dw")
    du2, g_sccw8 = _short_conv_bwd(u2, dsc, sc_cw)
    big_grads[2], got_t = _mm(n2, du2, "tn", m=d, n=3 * d, k=s, tm=d, tn=1024, tk=tkw, out_dtype=BF16,
                              name="proj_sc_dw", rider=scatter(3))
    arrived(3, got_t)
    dh2, dh2b, g_scn = _mm(
        du2, w_si, "nt", m=s, n=d, k=3 * d, tm=tmh, tn=d, tk=3 * d, out_dtype=(F32, BF16), name="proj_sc_dx",
        epi=_epi_rms_bwd, extras=(h2, dh3), vecs=(sc_g,), row_sum=True)

    dh1, dh1b, g_mn0 = mlp_bwd(dh2, dh2b, h1, n1, zr0, 0)

    dcat = _mm(dh1b, w_out, "nt", m=s, n=d, k=d, tm=tm, tn=d, tk=d, out_dtype=F32, name="out_ab_dx")
    big_grads[1] = _mm(cat, dh1b, "tn", m=d, n=d, k=s, tm=d, tn=d, tk=tkw, out_dtype=BF16, name="out_ab_dw")
    (dca, conv_stats), got_t = _conv_a_bwd_ln(ca, dcat, ab_ln_g, ab_ln_b, rider=scatter(1))
    arrived(1, got_t)
    (duc, g_cw32), got_t = _conv_a_bwd_conv(uc, dca, conv_w, rider=scatter(4))
    arrived(4, got_t)
    delta, dys = _attn_delta(dcat, cat)

    dqkv, ds_list = [], []
    for g, (_, dil) in enumerate(_GROUPS):
        late = (2, 5, 7)[g]
        (grads, dsa), got_t = _attn_bwd(uq, uk, uv, dys, lse, delta, bias, dqkv, g, dil, 4 if dil <= 4 else 1,
                                        rider=scatter(late))
        arrived(late, got_t)
        dqkv = [grads]
        ds_list.append(dsa)
    g_bias = _bias_reduce(jnp.concatenate(ds_list, axis=0), buckets)[:, :, 0].T
    dqkv = dqkv[0].reshape(9 * _PAIRS, s, _LANES)

    g_in_conv = _mm(n0, duc, "tn", m=d, n=2 * c, k=s, tm=d, tn=2 * c, tk=tkw, out_dtype=BF16,
                    name="proj_ab_dw_conv")
    g_in_qkv, sib_late = _mm(n0, dqkv, "tn", m=d, n=9 * c, k=s, tm=d, tn=3 * c, tk=min(1024, s), out_dtype=BF16,
                             name="proj_ab_dw_qkv", rider=_swap_rider(sums[1:]), split="b")
    big_grads[0] = jnp.concatenate([g_in_conv, g_in_qkv], axis=1)
    cut = d // 4
    dn0, (got_top,) = _mm(duc, w_in, "nt", m=s, n=d, k=2 * c, tm=tm, tn=d, tk=2 * c, out_dtype=F32,
                          name="proj_ab_dx_conv", rider=_scatter_rider([big_grads[0]], [w_axes[0]], rows=(0, cut)))
    dn0, (got_bot,) = _mm(dqkv, w_in[:, 2 * c:], "nt", m=s, n=d, k=9 * c, tm=tm, tn=d, tk=3 * c, out_dtype=F32,
                          name="proj_ab_dx_qkv", epi=_epi_add, extras=(dn0,), split="a",
                          rider=_scatter_rider([big_grads[0]], [w_axes[0]], rows=(cut, d)))
    grad_x, g_abn = _rms_bwd(dn0, h0, ab_norm, dh1, "rms_bwd_ab")
    sums[0] = jnp.concatenate([_sum_partials(got_top, "sum_partials0_top"),
                               _sum_partials(got_bot, "sum_partials0_bottom")], axis=0)
    sib = list(_run_rider(_swap_rider([sums[0]]), "swap_sibling_w_in")) + sib_late

    upd = [_adamw(w_shards[t], mm[0], vv[0], [sums[t], sib[t]], f"adamw{t}")
           for t, (mm, vv) in enumerate(((m_ab_w_in, v_ab_w_in), (m_ab_w_out, v_ab_w_out),
                                         (m_sc_w_in, v_sc_w_in), (m_sc_w_out, v_sc_w_out)))]
    upd_up = _adamw_layers(mlp_w_up, m_mlp_w_up, v_mlp_w_up, [[sums[4], sib[4]], [sums[5], sib[5]]], "adamw_up")
    upd_dn = _adamw_layers(mlp_w_down, m_mlp_w_down, v_mlp_w_down, [[sums[6], sib[6]], [sums[7], sib[7]]],
                           "adamw_down")

    full_shapes = [(_NUM_BUCKETS, rel_bias.shape[1]), (1, d), (_CONV_K, c), (1, c), (1, c), (1, c), (1, d),
                   (3, d), (2, d), (d,), (1, 1)]
    small_grads = [g_bias, g_abn, g_cw32[:_CONV_K], conv_stats[0:1], conv_stats[1:2], conv_stats[2:3], g_scn,
                   g_sccw8[:3], jnp.concatenate([g_mn0, g_mn1], axis=0), g_final.reshape(d), loss_part[0:1, 0:1]]
    tot = _unpack(_sum_all_devices(_pack(small_grads, _pack_rows(full_shapes)), "sum_small"), full_shapes)
    loss = tot.pop()[0, 0]
    for idx, sh in ((2, cw_sh), (6, scn_sh), (7, scw_sh)):
        width = sh.shape[1]
        tot[idx] = lax.dynamic_slice_in_dim(tot[idx], chip * width, width, axis=1)
    sm_w = [rel_bias, ab_norm, cw_sh, ab_conv_b, ab_ln_g, ab_ln_b, scn_sh, scw_sh, mlp_norm, final_norm]
    sm_m = [m_rel_bias, m_ab_norm, m_ab_conv_w[0], m_ab_conv_b, m_ab_ln_g, m_ab_ln_b, m_sc_norm, m_sc_conv_w[0],
            m_mlp_norm, m_final_norm]
    sm_v = [v_rel_bias, v_ab_norm, v_ab_conv_w[0], v_ab_conv_b, v_ab_ln_g, v_ab_ln_b, v_sc_norm, v_sc_conv_w[0],
            v_mlp_norm, v_final_norm]
    sh_shapes = [tuple(t.shape) for t in tot]
    rows = _pack_rows(sh_shapes)
    sm_upd = _adamw(_pack(sm_w, rows), _pack(sm_m, rows), _pack(sm_v, rows), [_pack(tot, rows)], "adamw_small")
    sm_g, sm_d, sm_nm, sm_nv = [_unpack(buf, sh_shapes) for buf in sm_upd]

    def assemble(kind, sm):
        big = [u[kind] for u in upd]
        return [sm[0], sm[1], big[0][None], sm[2][None], sm[3], sm[4], sm[5], big[1][None], sm[6], big[2][None],
                sm[7][None], big[3][None], sm[8], upd_up[kind], upd_dn[kind], sm[9]]

    res = [loss, grad_x[None]]
    for kind, sm in enumerate((sm_g, sm_d, sm_nm, sm_nv)):
        res += assemble(kind, sm)
    return tuple(res)
```

```python
import functools
import math

import jax
import jax.numpy as jnp
from jax import lax
from jax.experimental import pallas as pl
from jax.experimental.pallas import tpu as pltpu

F32 = jnp.float32
BF16 = jnp.bfloat16
MESH = pl.DeviceIdType.MESH

_GROUPS = ((128, 1), (512, 4), (2048, 16))
_STEPS = 128
_HEAD_DIM = 64
_GROUP_COLS = 512
_NUM_BUCKETS = 32
_MAX_DISTANCE = 2048
_CONV_K = 31
_HALO = 32
_SC_HALO = 16
_RMS_EPS = 1e-6
_LN_EPS = 1e-5
_NEG = -1e30
_LANES = 128
_VMEM_LIMIT = 56 * 1024 * 1024

_LR, _B1, _B2, _EPS, _WD, _STEP = 0.001, 0.9, 0.999, 1e-08, 0.01, 10


class _Rider:
    def __init__(self, ins, out_shapes, scratch, start, finish):
        self.ins, self.out_shapes, self.scratch = list(ins), list(out_shapes), list(scratch)
        self.start, self.finish = start, finish


def _pcall(body, *, name, out_shape, in_specs, out_specs, grid=None, scratch=(), aliases=None, rider=None):
    kw = {} if grid is None else {"grid": grid}
    cparams = pltpu.CompilerParams(vmem_limit_bytes=_VMEM_LIMIT)
    if rider is None:
        return pl.pallas_call(
            body, name=name, out_shape=out_shape, in_specs=in_specs, out_specs=out_specs,
            scratch_shapes=list(scratch), input_output_aliases=aliases or {},
            compiler_params=cparams, **kw)
    single = not isinstance(out_specs, (list, tuple))
    ospecs = [out_specs] if single else list(out_specs)
    oshapes = [out_shape] if single else list(out_shape)
    nin, nout, nscr = len(in_specs), len(ospecs), len(scratch)
    rin, rout = len(rider.ins), len(rider.out_shapes)

    def wrapped(*refs):
        h_in, r_in = refs[:nin], refs[nin:nin + rin]
        p = nin + rin
        h_out, r_out = refs[p:p + nout], refs[p + nout:p + nout + rout]
        p += nout + rout
        h_scr, r_scr = refs[p:p + nscr], refs[p + nscr:]
        ids = [pl.program_id(a) for a in range(len(grid))]
        first = functools.reduce(jnp.logical_and, [i == 0 for i in ids])
        last = functools.reduce(jnp.logical_and, [i == g - 1 for i, g in zip(ids, grid)])

        @pl.when(first)
        def _():
            rider.start(r_in, r_out, r_scr)

        body(*h_in, *h_out, *h_scr)

        @pl.when(last)
        def _():
            rider.finish(r_in, r_out, r_scr)

    call = pl.pallas_call(
        wrapped, name=name, out_shape=oshapes + rider.out_shapes,
        in_specs=list(in_specs) + [_ANY] * rin, out_specs=ospecs + [_ANY] * rout,
        scratch_shapes=list(scratch) + rider.scratch, input_output_aliases=aliases or {},
        compiler_params=cparams, **kw)

    def run(*operands):
        res = call(*operands, *rider.ins)
        host = res[0] if single else list(res[:nout])
        return host, list(res[nout:])

    return run


def _sig(x):
    return 1.0 / (1.0 + jnp.exp(-x))


_ANY = pl.BlockSpec(memory_space=pl.ANY)


def _lanes_of(ref):
    parts = [ref[p] for p in range(ref.shape[0])]
    return parts[0] if len(parts) == 1 else jnp.concatenate(parts, axis=1)


def _mm(a, b, mode, *, m, n, k, tm, tn, tk, out_dtype, name, epi=None, extras=(), b_off=(0, 0), rider=None,
        split="", vecs=(), a_pro=None, row_sum=False):
    nk = k // tk
    assert m % tm == 0 and n % tn == 0 and k % tk == 0
    o0, o1 = b_off
    if mode == "nn":
        a_spec = pl.BlockSpec((tm, tk), lambda i, j, kk: (i, kk))
        b_spec = pl.BlockSpec((tk, tn), lambda i, j, kk: (kk + o0, j + o1))
        dn = (((1,), (0,)), ((), ()))
    elif mode == "nt":
        a_spec = pl.BlockSpec((tm, tk), lambda i, j, kk: (i, kk))
        if "a" in split:
            a_spec = pl.BlockSpec((tk // _LANES, tm, _LANES), lambda i, j, kk: (kk, i, 0))
        b_spec = pl.BlockSpec((tn, tk), lambda i, j, kk: (j + o0, kk + o1))
        dn = (((1,), (1,)), ((), ()))
    else:
        a_spec = pl.BlockSpec((tk, tm), lambda i, j, kk: (kk, i))
        b_spec = pl.BlockSpec((tk, tn), lambda i, j, kk: (kk + o0, j + o1))
        if "b" in split:
            b_spec = pl.BlockSpec((tn // _LANES, tk, _LANES), lambda i, j, kk: (j, kk, 0))
        dn = (((0,), (0,)), ((), ()))
    o_spec = pl.BlockSpec((tm, tn), lambda i, j, kk: (i, j))
    e_spec = o_spec
    if "o" in split:
        o_spec = pl.BlockSpec((tn // _LANES, tm, _LANES), lambda i, j, kk: (j, i, 0))
    v_spec = pl.BlockSpec((1, tn), lambda i, j, kk: (0, j))
    ne = len(extras) + len(vecs)
    multi = isinstance(out_dtype, tuple)
    dts = out_dtype if multi else (out_dtype,)
    no = len(dts)
    nr = 1 if row_sum else 0
    assert not row_sum or tn == n

    def body(a_ref, b_ref, *rest):
        ex, o_refs = rest[:ne], rest[ne:ne + no]
        av = _lanes_of(a_ref) if "a" in split else a_ref[...]
        bv = _lanes_of(b_ref) if "b" in split else b_ref[...]
        if av.dtype != BF16:
            av = av.astype(BF16)
        if bv.dtype != BF16:
            bv = bv.astype(BF16)
        if a_pro is not None:
            av = a_pro(av)
        p = lax.dot_general(av, bv, dn, preferred_element_type=F32)

        def fin(x):
            if epi is not None:
                x = epi(x, *[e[...] for e in ex])
            if row_sum:
                row, x = x[-1], (x[:-1] if multi else x[0])
                row_ref = rest[ne + no]

                @pl.when(pl.program_id(0) == 0)
                def _():
                    row_ref[...] = row

                @pl.when(pl.program_id(0) > 0)
                def _():
                    row_ref[...] += row

            for o_ref, val, dt in zip(o_refs, x if multi else (x,), dts):
                if "o" in split:
                    for p in range(tn // _LANES):
                        o_ref[p] = val[:, p * _LANES:(p + 1) * _LANES].astype(dt)
                else:
                    o_ref[...] = val.astype(dt)

        if nk == 1:
            fin(p)
        else:
            acc = rest[ne + no + nr]
            kk = pl.program_id(2)

            @pl.when(kk == 0)
            def _():
                acc[...] = p

            @pl.when(kk > 0)
            def _():
                acc[...] += p

            @pl.when(kk == nk - 1)
            def _():
                fin(acc[...])

    oshape = (n // _LANES, m, _LANES) if "o" in split else (m, n)
    shapes = [jax.ShapeDtypeStruct(oshape, dt) for dt in dts]
    ospecs = [o_spec] * no
    if row_sum:
        shapes.append(jax.ShapeDtypeStruct((1, n), F32))
        ospecs.append(v_spec)
    lone = not multi and not row_sum
    return _pcall(
        body, name=name, grid=(m // tm, n // tn, nk),
        in_specs=[a_spec, b_spec] + [e_spec] * len(extras) + [v_spec] * len(vecs),
        out_specs=ospecs[0] if lone else ospecs, out_shape=shapes[0] if lone else shapes,
        scratch=[pltpu.VMEM((tm, tn), F32)] if nk > 1 else [], rider=rider,
    )(a, b, *extras, *vecs)


def _epi_add(x, r):
    return x + r


def _epi_relu(x):
    return jnp.maximum(x, 0.0)


def _square(x):
    return x * x


def _epi_relu_sq_bwd(da, zr):
    return da * (2.0 * zr.astype(F32))


def _epi_add_rms(x, r, g):
    h = x + r
    return h, h * lax.rsqrt(jnp.mean(h * h, axis=-1, keepdims=True) + _RMS_EPS) * g


def _epi_rms_bwd(dn, h, dh_in, g):
    dx, dg = _rms_bwd_math(dn, h, g)
    dh = dh_in + dx
    return dh, dh, dg


_ROW_T = 512


def _rms_fwd(h, g, name, rider=None):
    s, d = h.shape

    def body(h_ref, g_ref, o_ref):
        x = h_ref[...]
        r = lax.rsqrt(jnp.mean(x * x, axis=-1, keepdims=True) + _RMS_EPS)
        o_ref[...] = (x * r * g_ref[...]).astype(BF16)

    row = pl.BlockSpec((_ROW_T, d), lambda i: (i, 0))
    vec = pl.BlockSpec((1, d), lambda i: (0, 0))
    return _pcall(body, name=name, grid=(s // _ROW_T,), in_specs=[row, vec], out_specs=row,
                  out_shape=jax.ShapeDtypeStruct((s, d), BF16), rider=rider)(h, g)


def _rms_bwd_math(dn, x, g):
    r = lax.rsqrt(jnp.mean(x * x, axis=-1, keepdims=True) + _RMS_EPS)
    xhat = x * r
    dg = jnp.sum(dn * xhat, axis=0, keepdims=True)
    t = dn * g
    dx = r * (t - xhat * jnp.mean(t * xhat, axis=-1, keepdims=True))
    return dx, dg


def _rms_bwd(dn, h, g, dh_in, name):
    s, d = h.shape

    def body(dn_ref, h_ref, g_ref, dhi_ref, dh_ref, dg_ref):
        dx, dg = _rms_bwd_math(dn_ref[...], h_ref[...], g_ref[...])
        dh_ref[...] = dhi_ref[...] + dx

        @pl.when(pl.program_id(0) == 0)
        def _():
            dg_ref[...] = jnp.zeros_like(dg_ref)

        dg_ref[...] += dg

    row = pl.BlockSpec((_ROW_T, d), lambda i: (i, 0))
    vec = pl.BlockSpec((1, d), lambda i: (0, 0))
    return _pcall(
        body, name=name, grid=(s // _ROW_T,), in_specs=[row, row, vec, row], out_specs=[row, vec],
        out_shape=[jax.ShapeDtypeStruct((s, d), F32), jax.ShapeDtypeStruct((1, d), F32)])(dn, h, g, dh_in)


def _loss_head(h, tgt, g):
    s, d = h.shape

    def body(h_ref, t_ref, g_ref, dh_ref, dhb_ref, dg_ref, loss_ref):
        x, gv = h_ref[...], g_ref[...]
        r = lax.rsqrt(jnp.mean(x * x, axis=-1, keepdims=True) + _RMS_EPS)
        err = x * r * gv - t_ref[...]
        part = 0.5 * jnp.sum(jnp.mean(err * err, axis=-1, keepdims=True))
        dx, dg = _rms_bwd_math(err * (1.0 / d), x, gv)
        dh_ref[...] = dx
        dhb_ref[...] = dx.astype(BF16)

        @pl.when(pl.program_id(0) == 0)
        def _():
            dg_ref[...] = jnp.zeros_like(dg_ref)
            loss_ref[...] = jnp.zeros_like(loss_ref)

        dg_ref[...] += dg
        loss_ref[...] += jnp.full(loss_ref.shape, part, F32)

    row = pl.BlockSpec((_ROW_T, d), lambda i: (i, 0))
    vec = pl.BlockSpec((1, d), lambda i: (0, 0))
    one = pl.BlockSpec((1, _LANES), lambda i: (0, 0))
    return _pcall(
        body, name="loss_head", grid=(s // _ROW_T,), in_specs=[row, row, vec], out_specs=[row, row, vec, one],
        out_shape=[jax.ShapeDtypeStruct((s, d), F32), jax.ShapeDtypeStruct((s, d), BF16),
                   jax.ShapeDtypeStruct((1, d), F32), jax.ShapeDtypeStruct((1, _LANES), F32)])(h, tgt, g)


_CONV_T = 256
_CONV_RC = 64


def _conv_a_specs(s):
    c = _GROUP_COLS
    hb = _CONV_T // _HALO
    val = pl.BlockSpec((_CONV_T, c), lambda i: (i, 0))
    gate = pl.BlockSpec((_CONV_T, c), lambda i: (i, 1))
    hval = pl.BlockSpec((_HALO, c), lambda i: (jnp.maximum(i * hb - 1, 0), 0))
    hgate = pl.BlockSpec((_HALO, c), lambda i: (jnp.maximum(i * hb - 1, 0), 1))
    return val, gate, hval, hgate


def _fill_glu(val_ref, gate_ref, hval_ref, hgate_ref, hs_ref):
    i = pl.program_id(0)
    hs_ref[pl.ds(_HALO, _CONV_T), :] = val_ref[...].astype(F32) * _sig(gate_ref[...].astype(F32))
    halo = hval_ref[...].astype(F32) * _sig(hgate_ref[...].astype(F32))
    hs_ref[pl.ds(0, _HALO), :] = jnp.where(i > 0, halo, 0.0)


_SHIFT_ROWS = _CONV_T + _HALO - 8


def _fill_shifts(src_ref, sh_ref):
    for b in range(1, 8):
        sh_ref[b - 1] = src_ref[pl.ds(b, _SHIFT_ROWS), :]


def _tap_rows(src_ref, sh_ref, start, rows, lanes=slice(None)):
    b = start % 8
    if b == 0:
        return src_ref[pl.ds(start, rows), lanes]
    return sh_ref[b - 1, pl.ds(start - b, rows), lanes]


def _conv_rows(hs_ref, sh_ref, w_ref, r0, rows):
    off = _HALO - (_CONV_K - 1)
    acc = jnp.zeros((rows, _GROUP_COLS), F32)
    for kk in range(_CONV_K):
        acc = acc + w_ref[kk:kk + 1, :] * _tap_rows(hs_ref, sh_ref, r0 + off + kk, rows)
    return acc


def _ln_fwd(ca, g, b):
    mu = jnp.mean(ca, axis=-1, keepdims=True)
    xc = ca - mu
    rstd = lax.rsqrt(jnp.mean(xc * xc, axis=-1, keepdims=True) + _LN_EPS)
    xhat = xc * rstd
    return xhat, rstd, xhat * g + b


def _conv_a_fwd(uc, w, cb, lg, lb, rider=None):
    s = uc.shape[0]
    c = _GROUP_COLS

    def body(val_ref, gate_ref, hval_ref, hgate_ref, w_ref, cb_ref, lg_ref, lb_ref, o_ref, ca_ref, hs_ref, sh_ref):
        _fill_glu(val_ref, gate_ref, hval_ref, hgate_ref, hs_ref)
        _fill_shifts(hs_ref, sh_ref)
        for rc in range(_CONV_T // _CONV_RC):
            r0 = rc * _CONV_RC
            ca = _conv_rows(hs_ref, sh_ref, w_ref, r0, _CONV_RC) + cb_ref[...]
            ca_ref[pl.ds(r0, _CONV_RC), :] = ca
            _, _, ln = _ln_fwd(ca, lg_ref[...], lb_ref[...])
            o_ref[pl.ds(r0, _CONV_RC), :] = (ln * _sig(ln)).astype(BF16)

    val, gate, hval, hgate = _conv_a_specs(s)
    wspec = pl.BlockSpec((_CONV_K, c), lambda i: (0, 0))
    vec = pl.BlockSpec((1, c), lambda i: (0, 0))
    blk = pl.BlockSpec((_CONV_T, c), lambda i: (i, 0))
    return _pcall(
        body, name="conv_a_fwd", grid=(s // _CONV_T,),
        in_specs=[val, gate, hval, hgate, wspec, vec, vec, vec],
        out_specs=[blk, blk],
        out_shape=[jax.ShapeDtypeStruct((s, 2 * c), BF16), jax.ShapeDtypeStruct((s, c), F32)],
        scratch=[pltpu.VMEM((_CONV_T + _HALO, c), F32), pltpu.VMEM((7, _SHIFT_ROWS, c), F32)],
        rider=rider)(uc, uc, uc, uc, w, cb, lg, lb)


def _conv_a_bwd_ln(ca_all, dcat, lg, lb, rider=None):
    s = ca_all.shape[0]
    c = _GROUP_COLS

    def body(ca_ref, dy_ref, lg_ref, lb_ref, dca_ref, st_ref):
        @pl.when(pl.program_id(0) == 0)
        def _():
            st_ref[...] = jnp.zeros_like(st_ref)

        for rc in range(_CONV_T // _CONV_RC):
            r0 = rc * _CONV_RC
            ca = ca_ref[pl.ds(r0, _CONV_RC), :]
            xhat, rstd, ln = _ln_fwd(ca, lg_ref[...], lb_ref[...])
            sg = _sig(ln)
            dln = dy_ref[pl.ds(r0, _CONV_RC), :] * (sg * (1.0 + ln * (1.0 - sg)))
            dxh = dln * lg_ref[...]
            dca = rstd * (dxh - jnp.mean(dxh, axis=-1, keepdims=True)
                          - xhat * jnp.mean(dxh * xhat, axis=-1, keepdims=True))
            dca_ref[pl.ds(r0, _CONV_RC), :] = dca
            st_ref[0:1, :] += jnp.sum(dca, axis=0, keepdims=True)
            st_ref[1:2, :] += jnp.sum(dln * xhat, axis=0, keepdims=True)
            st_ref[2:3, :] += jnp.sum(dln, axis=0, keepdims=True)

    blk = pl.BlockSpec((_CONV_T, c), lambda i: (i, 0))
    vec = pl.BlockSpec((1, c), lambda i: (0, 0))
    st = pl.BlockSpec((8, c), lambda i: (0, 0))
    return _pcall(
        body, name="conv_a_bwd_ln", grid=(s // _CONV_T,),
        in_specs=[blk, blk, vec, vec], out_specs=[blk, st],
        out_shape=[jax.ShapeDtypeStruct((s, c), F32), jax.ShapeDtypeStruct((8, c), F32)],
        rider=rider)(ca_all, dcat, lg, lb)


def _conv_a_bwd_conv(uc, dca, w, rider=None):
    s = uc.shape[0]
    c = _GROUP_COLS
    nblk = s // _CONV_T
    hb = _CONV_T // _HALO
    off = _HALO - (_CONV_K - 1)

    def body(val_ref, gate_ref, hval_ref, hgate_ref, d_ref, dn_ref, w_ref, du_ref, dw_ref, hs_ref, ds_ref,
             hsh_ref, dsh_ref, dwa_ref):
        i = pl.program_id(0)
        _fill_glu(val_ref, gate_ref, hval_ref, hgate_ref, hs_ref)
        ds_ref[pl.ds(0, _CONV_T), :] = d_ref[...]
        ds_ref[pl.ds(_CONV_T, _HALO), :] = jnp.where(i < nblk - 1, dn_ref[...], 0.0)
        _fill_shifts(hs_ref, hsh_ref)
        _fill_shifts(ds_ref, dsh_ref)

        @pl.when(i == 0)
        def _():
            dwa_ref[...] = jnp.zeros_like(dwa_ref)

        rows = 32
        for r0 in range(0, _CONV_T, rows):
            dcur = ds_ref[pl.ds(r0, rows), :]
            dh = jnp.zeros((rows, c), F32)
            for kk in range(_CONV_K):
                dh = dh + w_ref[kk:kk + 1, :] * _tap_rows(ds_ref, dsh_ref, r0 + _CONV_K - 1 - kk, rows)
                prod = dcur * _tap_rows(hs_ref, hsh_ref, r0 + off + kk, rows)
                dwa_ref[pl.ds(8 * kk, 8), :] += sum(prod[t:t + 8] for t in range(0, rows, 8))
            v = val_ref[pl.ds(r0, rows), :].astype(F32)
            sg = _sig(gate_ref[pl.ds(r0, rows), :].astype(F32))
            du_ref[pl.ds(r0, rows), pl.ds(0, c)] = (dh * sg).astype(BF16)
            du_ref[pl.ds(r0, rows), pl.ds(c, c)] = (dh * v * sg * (1.0 - sg)).astype(BF16)

        @pl.when(i == nblk - 1)
        def _():
            dw_ref[...] = jnp.zeros_like(dw_ref)
            for kk in range(_CONV_K):
                dw_ref[kk:kk + 1, :] = jnp.sum(dwa_ref[pl.ds(8 * kk, 8), :], axis=0, keepdims=True)

    val, gate, hval, hgate = _conv_a_specs(s)
    blk = pl.BlockSpec((_CONV_T, c), lambda i: (i, 0))
    nxt = pl.BlockSpec((_HALO, c), lambda i: (jnp.minimum((i + 1) * hb, s // _HALO - 1), 0))
    wspec = pl.BlockSpec((_CONV_K, c), lambda i: (0, 0))
    return _pcall(
        body, name="conv_a_bwd_conv", grid=(nblk,),
        in_specs=[val, gate, hval, hgate, blk, nxt, wspec],
        out_specs=[pl.BlockSpec((_CONV_T, 2 * c), lambda i: (i, 0)), pl.BlockSpec((_HALO, c), lambda i: (0, 0))],
        out_shape=[jax.ShapeDtypeStruct((s, 2 * c), BF16), jax.ShapeDtypeStruct((_HALO, c), F32)],
        scratch=[pltpu.VMEM((_CONV_T + _HALO, c), F32), pltpu.VMEM((_CONV_T + _HALO, c), F32),
                 pltpu.VMEM((7, _SHIFT_ROWS, c), F32), pltpu.VMEM((7, _SHIFT_ROWS, c), F32),
                 pltpu.VMEM((8 * _HALO, c), F32)],
        rider=rider,
    )(uc, uc, uc, uc, dca, dca, w)


_SC_T = 256
_SC_RC = 32
_SC_LC = 512


def _sc_chunks(d):
    return [(pl.ds(r0, _SC_RC), pl.ds(l0, _SC_LC)) for r0 in range(0, _SC_T, _SC_RC) for l0 in range(0, d, _SC_LC)]


def _short_conv_fwd(u2, w):
    s, d3 = u2.shape
    d = d3 // 3
    hb = _SC_T // _SC_HALO

    def body(b_ref, c_ref, v_ref, hc_ref, hv_ref, w_ref, o_ref, cs_ref):
        i = pl.program_id(0)
        cs_ref[pl.ds(0, _SC_HALO), :] = jnp.where(i > 0, hc_ref[...].astype(F32) * hv_ref[...].astype(F32), 0.0)
        for rows, lanes in _sc_chunks(d):
            cs_ref[pl.ds(_SC_HALO + rows.start, _SC_RC), lanes] = (
                c_ref[rows, lanes].astype(F32) * v_ref[rows, lanes].astype(F32))
        for rows, lanes in _sc_chunks(d):
            taps = [cs_ref[pl.ds(_SC_HALO - 2 + kk + rows.start, _SC_RC), lanes] for kk in range(3)]
            conv = w_ref[0:1, lanes] * taps[0] + w_ref[1:2, lanes] * taps[1] + w_ref[2:3, lanes] * taps[2]
            o_ref[rows, lanes] = (b_ref[rows, lanes].astype(F32) * conv).astype(BF16)

    def col(j):
        return pl.BlockSpec((_SC_T, d), lambda i: (i, j))

    def halo(j):
        return pl.BlockSpec((_SC_HALO, d), lambda i: (jnp.maximum(i * hb - 1, 0), j))

    return _pcall(
        body, name="short_conv_fwd", grid=(s // _SC_T,),
        in_specs=[col(0), col(1), col(2), halo(1), halo(2), pl.BlockSpec((3, d), lambda i: (0, 0))],
        out_specs=pl.BlockSpec((_SC_T, d), lambda i: (i, 0)),
        out_shape=jax.ShapeDtypeStruct((s, d), BF16),
        scratch=[pltpu.VMEM((_SC_T + _SC_HALO, d), F32)])(u2, u2, u2, u2, u2, w)


def _short_conv_bwd(u2, dsc, w, rider=None):
    s, d3 = u2.shape
    d = d3 // 3
    hb = _SC_T // _SC_HALO
    nblk = s // _SC_T

    def body(b_ref, c_ref, v_ref, hc_ref, hv_ref, nb_ref, d_ref, nd_ref, w_ref, du_ref, dw_ref, cs_ref, ds_ref):
        i = pl.program_id(0)
        cs_ref[pl.ds(0, _SC_HALO), :] = jnp.where(i > 0, hc_ref[...].astype(F32) * hv_ref[...].astype(F32), 0.0)
        ds_ref[pl.ds(_SC_T, _SC_HALO), :] = jnp.where(i < nblk - 1, nd_ref[...] * nb_ref[...].astype(F32), 0.0)
        for rows, lanes in _sc_chunks(d):
            cs_ref[pl.ds(_SC_HALO + rows.start, _SC_RC), lanes] = (
                c_ref[rows, lanes].astype(F32) * v_ref[rows, lanes].astype(F32))
            ds_ref[rows, lanes] = d_ref[rows, lanes] * b_ref[rows, lanes].astype(F32)

        @pl.when(i == 0)
        def _():
            dw_ref[...] = jnp.zeros_like(dw_ref)

        for l0 in range(0, d, _SC_LC):
            lanes = pl.ds(l0, _SC_LC)
            dw_acc = [jnp.zeros((8, _SC_LC), F32)] * 3
            for r0 in range(0, _SC_T, _SC_RC):
                rows = pl.ds(r0, _SC_RC)
                taps = [cs_ref[pl.ds(_SC_HALO - 2 + kk + r0, _SC_RC), lanes] for kk in range(3)]
                conv = w_ref[0:1, lanes] * taps[0] + w_ref[1:2, lanes] * taps[1] + w_ref[2:3, lanes] * taps[2]
                dconv = ds_ref[rows, lanes]
                dcv = (w_ref[2:3, lanes] * dconv + w_ref[1:2, lanes] * ds_ref[pl.ds(r0 + 1, _SC_RC), lanes]
                       + w_ref[0:1, lanes] * ds_ref[pl.ds(r0 + 2, _SC_RC), lanes])
                du_ref[rows, lanes] = (d_ref[rows, lanes] * conv).astype(BF16)
                du_ref[rows, pl.ds(d + l0, _SC_LC)] = (dcv * v_ref[rows, lanes].astype(F32)).astype(BF16)
                du_ref[rows, pl.ds(2 * d + l0, _SC_LC)] = (dcv * c_ref[rows, lanes].astype(F32)).astype(BF16)
                for kk in range(3):
                    prod = dconv * taps[kk]
                    dw_acc[kk] = dw_acc[kk] + sum(prod[t:t + 8] for t in range(0, _SC_RC, 8))
            for kk in range(3):
                dw_ref[kk:kk + 1, lanes] += jnp.sum(dw_acc[kk], axis=0, keepdims=True)

    def col(j):
        return pl.BlockSpec((_SC_T, d), lambda i: (i, j))

    def halo(j):
        return pl.BlockSpec((_SC_HALO, d), lambda i: (jnp.maximum(i * hb - 1, 0), j))

    def nxt(j):
        return pl.BlockSpec((_SC_HALO, d), lambda i: (jnp.minimum((i + 1) * hb, s // _SC_HALO - 1), j))

    return _pcall(
        body, name="short_conv_bwd", grid=(nblk,),
        in_specs=[col(0), col(1), col(2), halo(1), halo(2), nxt(0), col(0), nxt(0),
                  pl.BlockSpec((3, d), lambda i: (0, 0))],
        out_specs=[pl.BlockSpec((_SC_T, d3), lambda i: (i, 0)), pl.BlockSpec((8, d), lambda i: (0, 0))],
        out_shape=[jax.ShapeDtypeStruct((s, d3), BF16), jax.ShapeDtypeStruct((8, d), F32)],
        scratch=[pltpu.VMEM((_SC_T + _SC_HALO, d), F32), pltpu.VMEM((_SC_T + _SC_HALO, d), F32)],
        rider=rider,
    )(u2, u2, u2, u2, u2, u2, dsc, dsc, w)


def _bucket_maps():
    a_idx = jnp.arange(_STEPS)[:, None]
    c_idx = jnp.arange(2 * _STEPS)[None, :]
    mdist = jnp.clip(a_idx + _STEPS - c_idx, 0, _STEPS)
    max_exact = _NUM_BUCKETS // 2
    maps = []
    for _, dil in _GROUPS:
        nn = mdist * dil
        nf = jnp.maximum(nn, 1).astype(F32)
        large = max_exact + (jnp.log(nf / max_exact) / math.log(_MAX_DISTANCE / max_exact)
                             * (_NUM_BUCKETS - max_exact)).astype(jnp.int32)
        maps.append(jnp.where(nn < max_exact, nn, jnp.minimum(large, _NUM_BUCKETS - 1)).astype(jnp.int32))
    return jnp.stack(maps, axis=0)


def _bias_expand(rel_bias, buckets):
    nh = rel_bias.shape[1]

    def body(rb_ref, bk_ref, o_ref):
        h = pl.program_id(0)
        bk = bk_ref[0]
        acc = jnp.zeros(bk.shape, F32)
        for b in range(_NUM_BUCKETS):
            acc = jnp.where(bk == b, rb_ref[b, h], acc)
        a = lax.broadcasted_iota(jnp.int32, bk.shape, 0)
        c = lax.broadcasted_iota(jnp.int32, bk.shape, 1)
        mdist = a + _STEPS - c
        o_ref[0] = jnp.where((mdist >= 0) & (mdist <= _STEPS), acc, _NEG)

    return _pcall(
        body, name="bias_expand", grid=(nh,),
        in_specs=[pl.BlockSpec(memory_space=pltpu.SMEM),
                  pl.BlockSpec((1, _STEPS, 2 * _STEPS), lambda h: (h // 8, 0, 0))],
        out_specs=pl.BlockSpec((1, _STEPS, 2 * _STEPS), lambda h: (h, 0, 0)),
        out_shape=jax.ShapeDtypeStruct((nh, _STEPS, 2 * _STEPS), F32))(rel_bias, buckets)


def _bias_reduce(ds_all, buckets):
    nh = ds_all.shape[0]

    def body(ds_ref, bk_ref, o_ref):
        t, bk = ds_ref[0], bk_ref[0]
        rows = lax.broadcasted_iota(jnp.int32, (_NUM_BUCKETS, _LANES), 0)
        out = jnp.zeros((_NUM_BUCKETS, _LANES), F32)
        for b in range(_NUM_BUCKETS):
            out = jnp.where(rows == b, jnp.sum(jnp.where(bk == b, t, 0.0)), out)
        o_ref[0] = out

    blk = pl.BlockSpec((1, _STEPS, 2 * _STEPS), lambda h: (h, 0, 0))
    return _pcall(
        body, name="bias_reduce", grid=(nh,),
        in_specs=[blk, pl.BlockSpec((1, _STEPS, 2 * _STEPS), lambda h: (h // 8, 0, 0))],
        out_specs=pl.BlockSpec((1, _NUM_BUCKETS, _LANES), lambda h: (h, 0, 0)),
        out_shape=jax.ShapeDtypeStruct((nh, _NUM_BUCKETS, _LANES), F32))(ds_all, buckets)


def _sub_residues(dil):
    return 4 if dil % 16 == 0 else 1


def _strided_rows(ref, tmp_ref, p, r, dil):
    sub = _sub_residues(dil)
    if dil == 1:
        return [ref[p]]
    if sub == 1:
        return [ref[p, pl.ds(r, _STEPS, stride=dil), :]]
    tmp_ref[...] = ref[p, pl.ds(r, _STEPS * sub, stride=dil // sub), :]
    return [tmp_ref[pl.ds(q, _STEPS, stride=sub), :] for q in range(sub)]


def _store_strided(ref, tmp_ref, p, r, dil, vals):
    sub = _sub_residues(dil)
    if dil == 1:
        ref[p] = vals[0]
    elif sub == 1:
        ref[p, pl.ds(r, _STEPS, stride=dil), :] = vals[0]
    else:
        for q, val in enumerate(vals):
            tmp_ref[pl.ds(q, _STEPS, stride=sub), :] = val
        ref[p, pl.ds(r, _STEPS * sub, stride=dil // sub), :] = tmp_ref[...]


def _tmp_rows(dil, count):
    sub = _sub_residues(dil)
    return [pltpu.VMEM((_STEPS * sub, _LANES), F32)] * count if sub > 1 else []


def _head_masks():
    lane = lax.broadcasted_iota(jnp.int32, (1, _LANES), 1)
    return [lane < _HEAD_DIM, lane >= _HEAD_DIM]


def _stack_heads(x2, masks):
    return jnp.concatenate([jnp.where(masks[0], x2, 0), jnp.where(masks[1], x2, 0)], axis=0)


def _unstack_heads(y, masks):
    return jnp.where(masks[0], y[:_STEPS], y[_STEPS:])


def _scores(qs2, k2, b_ref, j, first):
    sc = lax.dot_general(qs2, k2, (((1,), (1,)), ((), ())), preferred_element_type=F32)
    sc = sc * (_HEAD_DIM ** -0.5) + jnp.concatenate([b_ref[2 * j], b_ref[2 * j + 1]], axis=0)
    col = lax.broadcasted_iota(jnp.int32, sc.shape, 1)
    return jnp.where(jnp.logical_and(first, col < _STEPS), _NEG, sc)


_PAIRS = _GROUP_COLS // _LANES


def _attn_fwd(uq, uk, uv, bias, g, dil, pp, rider=None):
    s = uq.shape[1]
    rb = _STEPS * dil
    nb = s // rb
    npb = _PAIRS // pp

    sub = _sub_residues(dil)

    def body(q_ref, kc_ref, kp_ref, vc_ref, vp_ref, b_ref, o_ref, l_ref, *tmp):
        tmp = tmp + (None,) * 7
        n, r = pl.program_id(1), pl.program_id(2)
        first = n == 0
        masks = _head_masks()
        for j in range(pp):
            qs = _strided_rows(q_ref, tmp[0], j, r, dil)
            kps, kcs = _strided_rows(kp_ref, tmp[1], j, r, dil), _strided_rows(kc_ref, tmp[2], j, r, dil)
            vps, vcs = _strided_rows(vp_ref, tmp[3], j, r, dil), _strided_rows(vc_ref, tmp[4], j, r, dil)
            o_res, l_res = [], []
            for q in range(sub):
                q2 = qs[q].astype(BF16)
                k2 = jnp.concatenate([kps[q], kcs[q]], axis=0).astype(BF16)
                v2 = jnp.concatenate([vps[q], vcs[q]], axis=0).astype(BF16)
                sc = _scores(_stack_heads(q2, masks), k2, b_ref, j, first)
                mx = jnp.max(sc, axis=-1, keepdims=True)
                p = jnp.exp(sc - mx)
                den = jnp.sum(p, axis=-1, keepdims=True)
                o2 = jnp.dot(p.astype(BF16), v2, preferred_element_type=F32) / den
                o_res.append(_unstack_heads(o2, masks))
                l_res.append(_unstack_heads(jnp.broadcast_to(mx + jnp.log(den), o2.shape), masks))
            _store_strided(o_ref, tmp[5], j, r, dil, o_res)
            _store_strided(l_ref, tmp[6], j, r, dil, l_res)

    cur = pl.BlockSpec((pp, rb, _LANES), lambda hb, n, r: (g * npb + hb, n, 0))
    prev = pl.BlockSpec((pp, rb, _LANES), lambda hb, n, r: (g * npb + hb, jnp.maximum(n - 1, 0), 0))
    bspec = pl.BlockSpec((2 * pp, _STEPS, 2 * _STEPS), lambda hb, n, r: (g * npb + hb, 0, 0))
    ospec = pl.BlockSpec((pp, rb, _LANES), lambda hb, n, r: (hb, n, 0))
    sh = jax.ShapeDtypeStruct((_PAIRS, s, _LANES), F32)
    return _pcall(
        body, name=f"attn_fwd_g{g}", grid=(npb, nb, dil // sub),
        in_specs=[cur, cur, prev, cur, prev, bspec], out_specs=[ospec, ospec], out_shape=[sh, sh],
        scratch=_tmp_rows(dil, 7), rider=rider,
    )(uq, uk, uk, uv, uv, bias)


def _attn_merge(outs, lses, cat):
    s = outs[0].shape[1]
    c = _GROUP_COLS

    def body(o0, o1, o2, l0, l1, l2, cat_in, cat_ref, lse_ref):
        del cat_in
        a0, a1, a2 = l0[...], l1[...], l2[...]
        mx = jnp.maximum(jnp.maximum(a0, a1), a2)
        w0, w1, w2 = jnp.exp(a0 - mx), jnp.exp(a1 - mx), jnp.exp(a2 - mx)
        den = w0 + w1 + w2
        y = ((w0 * o0[...] + w1 * o1[...] + w2 * o2[...]) / den).astype(BF16)
        for p in range(_PAIRS):
            cat_ref[:, p * _LANES:(p + 1) * _LANES] = y[p]
        lse_ref[...] = mx + jnp.log(den)

    blk = pl.BlockSpec((_PAIRS, _ROW_T, _LANES), lambda i: (0, i, 0))
    return _pcall(
        body, name="attn_merge", grid=(s // _ROW_T,),
        in_specs=[blk] * 6 + [_ANY],
        out_specs=[pl.BlockSpec((_ROW_T, c), lambda i: (i, 1)), blk],
        out_shape=[jax.ShapeDtypeStruct(cat.shape, BF16), jax.ShapeDtypeStruct((_PAIRS, s, _LANES), F32)],
        aliases={6: 0})(*outs, *lses, cat)


def _attn_delta(dcat, cat):
    s = dcat.shape[0]
    c = _GROUP_COLS
    seg = (jnp.arange(c)[:, None] // _HEAD_DIM == jnp.arange(c)[None, :] // _HEAD_DIM).astype(BF16)

    def body(dy_ref, y_ref, seg_ref, dl_ref, dys_ref):
        dy = dy_ref[...]
        prod = dy * y_ref[...].astype(F32)
        hi = prod.astype(BF16)
        lo = (prod - hi.astype(F32)).astype(BF16)
        dl = (jnp.dot(hi, seg_ref[...], preferred_element_type=F32)
              + jnp.dot(lo, seg_ref[...], preferred_element_type=F32))
        for p in range(_PAIRS):
            dl_ref[p] = dl[:, p * _LANES:(p + 1) * _LANES]
            dys_ref[p] = dy[:, p * _LANES:(p + 1) * _LANES]

    right = pl.BlockSpec((_ROW_T, c), lambda i: (i, 1))
    blk = pl.BlockSpec((_PAIRS, _ROW_T, _LANES), lambda i: (0, i, 0))
    sh = jax.ShapeDtypeStruct((_PAIRS, s, _LANES), F32)
    return _pcall(
        body, name="attn_delta", grid=(s // _ROW_T,),
        in_specs=[right, right, pl.BlockSpec((c, c), lambda i: (0, 0))],
        out_specs=[blk, blk], out_shape=[sh, sh])(dcat, cat, seg)


def _attn_bwd(uq, uk, uv, dys, lse, delta, bias, prev_grads, g, dil, pp, rider=None):
    s = uq.shape[1]
    rb = _STEPS * dil
    nb = s // rb
    npb = _PAIRS // pp
    scale = _HEAD_DIM ** -0.5

    sub = _sub_residues(dil)

    def body(q_ref, kc_ref, kp_ref, vc_ref, vp_ref, dy_ref, l_ref, dl_ref, b_ref, *rest):
        rest = rest[len(prev_grads):]
        dqkv_ref, dsa_ref, dqc_ref, dkc_ref, dvc_ref = rest[:5]
        dq_ref, dk_ref, dv_ref = dqkv_ref.at[0], dqkv_ref.at[1], dqkv_ref.at[2]
        tmp = rest[5:] + (None,) * 11
        n, r = pl.program_id(1), pl.program_id(2)

        def carry_slot(j, q):
            return ((r + (dil // sub) * q) * pp + j) if sub > 1 else r * pp + j

        @pl.when(jnp.logical_and(n == 0, r == 0))
        def _():
            dsa_ref[...] = jnp.zeros_like(dsa_ref)

        @pl.when(n == 0)
        def _():
            for j in range(pp):
                for q in range(sub):
                    for carry in (dqc_ref, dkc_ref, dvc_ref):
                        carry[carry_slot(j, q)] = jnp.zeros((_STEPS, _LANES), F32)

        @pl.when(n < nb)
        def _():
            first = n == 0
            masks = _head_masks()
            for j in range(pp):
                qs = _strided_rows(q_ref, tmp[0], j, r, dil)
                kps, kcs = _strided_rows(kp_ref, tmp[1], j, r, dil), _strided_rows(kc_ref, tmp[2], j, r, dil)
                vps, vcs = _strided_rows(vp_ref, tmp[3], j, r, dil), _strided_rows(vc_ref, tmp[4], j, r, dil)
                dys_ = _strided_rows(dy_ref, tmp[5], j, r, dil)
                lses = _strided_rows(l_ref, tmp[6], j, r, dil)
                dls = _strided_rows(dl_ref, tmp[7], j, r, dil)
                ds_sum = [jnp.zeros((_STEPS, 2 * _STEPS), F32)] * 2
                dq_res, dk_res, dv_res = [], [], []
                for q in range(sub):
                    q2 = qs[q].astype(BF16)
                    k2 = jnp.concatenate([kps[q], kcs[q]], axis=0).astype(BF16)
                    v2 = jnp.concatenate([vps[q], vcs[q]], axis=0).astype(BF16)
                    dy2 = dys_[q].astype(BF16)
                    qs2, dys2 = _stack_heads(q2, masks), _stack_heads(dy2, masks)
                    per_row = lambda st: jnp.concatenate([st[:, 0:1], st[:, _HEAD_DIM:_HEAD_DIM + 1]], axis=0)
                    sc = _scores(qs2, k2, b_ref, j, first)
                    p = jnp.exp(sc - per_row(lses[q]))
                    dp = lax.dot_general(dys2, v2, (((1,), (1,)), ((), ())), preferred_element_type=F32)
                    ds = p * (dp - per_row(dls[q]))
                    ds_sum[0] = ds_sum[0] + ds[:_STEPS]
                    ds_sum[1] = ds_sum[1] + ds[_STEPS:]
                    dsb = ds.astype(BF16)
                    dq_p = _unstack_heads(jnp.dot(dsb, k2, preferred_element_type=F32), masks)
                    tdn = (((0,), (0,)), ((), ()))
                    dk_p = lax.dot_general(dsb, qs2, tdn, preferred_element_type=F32) * scale
                    dv_p = lax.dot_general(p.astype(BF16), dys2, tdn, preferred_element_type=F32)
                    slot = carry_slot(j, q)
                    dq_res.append(dqc_ref[slot])
                    dk_res.append(dkc_ref[slot] + dk_p[:_STEPS])
                    dv_res.append(dvc_ref[slot] + dv_p[:_STEPS])
                    dqc_ref[slot] = dq_p * scale
                    dkc_ref[slot] = dk_p[_STEPS:]
                    dvc_ref[slot] = dv_p[_STEPS:]
                for hh in range(2):
                    dsa_ref[2 * j + hh] += ds_sum[hh]
                _store_strided(dq_ref, tmp[8], j, r, dil, dq_res)
                _store_strided(dk_ref, tmp[9], j, r, dil, dk_res)
                _store_strided(dv_ref, tmp[10], j, r, dil, dv_res)

        @pl.when(n == nb)
        def _():
            for j in range(pp):
                for ref, carry, t in ((dq_ref, dqc_ref, 8), (dk_ref, dkc_ref, 9), (dv_ref, dvc_ref, 10)):
                    _store_strided(ref, tmp[t], j, r, dil, [carry[carry_slot(j, q)] for q in range(sub)])

    def clamp(n):
        return jnp.minimum(n, nb - 1)

    cur = pl.BlockSpec((pp, rb, _LANES), lambda hb, n, r: (g * npb + hb, clamp(n), 0))
    prev = pl.BlockSpec((pp, rb, _LANES), lambda hb, n, r: (g * npb + hb, jnp.maximum(clamp(n) - 1, 0), 0))
    stat = pl.BlockSpec((pp, rb, _LANES), lambda hb, n, r: (hb, clamp(n), 0))
    bspec = pl.BlockSpec((2 * pp, _STEPS, 2 * _STEPS), lambda hb, n, r: (g * npb + hb, 0, 0))
    late = pl.BlockSpec((3, pp, rb, _LANES), lambda hb, n, r: (0, g * npb + hb, jnp.maximum(n - 1, 0), 0))
    dsspec = pl.BlockSpec((2 * pp, _STEPS, 2 * _STEPS), lambda hb, n, r: (hb, 0, 0))
    np_ = len(prev_grads)
    carry = pltpu.VMEM((dil * pp, _STEPS, _LANES), F32)
    return _pcall(
        body, name=f"attn_bwd_g{g}", grid=(npb, nb + 1, dil // sub),
        in_specs=[cur, cur, prev, cur, prev, stat, stat, stat, bspec] + [_ANY] * np_,
        out_specs=[late, dsspec],
        out_shape=[jax.ShapeDtypeStruct((3, 3 * _PAIRS, s, _LANES), F32),
                   jax.ShapeDtypeStruct((8, _STEPS, 2 * _STEPS), F32)],
        scratch=[carry, carry, carry] + _tmp_rows(dil, 11),
        aliases={9 + t: t for t in range(np_)}, rider=rider,
    )(uq, uk, uk, uv, uv, dys, lse, delta, bias, *prev_grads)


def _place():
    x, y, c = lax.axis_index("x"), lax.axis_index("y"), lax.axis_index("c")
    chips = [(1 - x, y), (x, 1 - y), (1 - x, 1 - y)]
    return x, y, c, chips


def _slab(ref, axis, chip, width):
    start = pl.multiple_of(chip * width, width)
    if axis == 0:
        return ref.at[pl.ds(start, width), :]
    return ref.at[:, pl.ds(start, width)]


def _run_rider(rider, name):
    nin, nout = len(rider.ins), len(rider.out_shapes)

    def body(*refs):
        ins, outs, scr = refs[:nin], refs[nin:nin + nout], refs[nin + nout:]
        rider.start(ins, outs, scr)
        rider.finish(ins, outs, scr)

    return _pcall(body, name=name, in_specs=[_ANY] * nin, out_specs=[_ANY] * nout, out_shape=rider.out_shapes,
                  scratch=rider.scratch)(*rider.ins)


def _gather_halves_rider(shard, axis):
    shape = list(shard.shape)
    shape[axis] *= 4
    full = jax.ShapeDtypeStruct(tuple(shape), shard.dtype)
    half = shard.shape[0] // 2
    width = shard.shape[axis]

    def region(out, chip, core):
        if axis == 0:
            return out.at[pl.ds(pl.multiple_of(chip * width + core * half, half), half), :]
        return out.at[pl.ds(pl.multiple_of(core * half, half), half), pl.ds(pl.multiple_of(chip * width, width), width)]

    def copies(ins, outs, scr):
        send, recv, loc = scr
        (src,), (out,) = ins, outs
        x, y, c, chips = _place()
        mine = 2 * x + y
        own = pltpu.make_async_copy(src, _slab(out, axis, mine, width), loc.at[0])
        my_half = src.at[pl.ds(pl.multiple_of(c * half, half), half), :]
        over_ici, ici_in, to_sib, sib_in = [], [], [], []
        for j, (px, py) in enumerate(chips):
            theirs = 2 * px + py
            ici = dict(send_sem=send.at[j], recv_sem=recv.at[j], device_id=(px, py, c), device_id_type=MESH)
            d2d = dict(send_sem=send.at[3 + j], recv_sem=recv.at[3 + j], device_id=(x, y, 1 - c),
                       device_id_type=MESH)
            over_ici.append(pltpu.make_async_remote_copy(src_ref=my_half, dst_ref=region(out, mine, c), **ici))
            ici_in.append(pltpu.make_async_remote_copy(src_ref=my_half, dst_ref=region(out, theirs, c), **ici))
            to_sib.append(pltpu.make_async_remote_copy(
                src_ref=region(out, theirs, c), dst_ref=region(out, theirs, c), **d2d))
            sib_in.append(pltpu.make_async_remote_copy(
                src_ref=region(out, theirs, c), dst_ref=region(out, theirs, 1 - c), **d2d))
        return own, over_ici, ici_in, to_sib, sib_in

    def start(ins, outs, scr):
        own, over_ici, _, _, _ = copies(ins, outs, scr)
        own.start()
        for cp in over_ici:
            cp.start()

    def finish(ins, outs, scr):
        own, over_ici, ici_in, to_sib, sib_in = copies(ins, outs, scr)
        for j in range(3):
            ici_in[j].wait_recv()
            to_sib[j].start()
        for cp in sib_in:
            cp.wait_recv()
        own.wait()
        for cp in over_ici + to_sib:
            cp.wait_send()

    return _Rider([shard], [full], [pltpu.SemaphoreType.DMA((6,)), pltpu.SemaphoreType.DMA((6,)),
                                    pltpu.SemaphoreType.DMA((1,))], start, finish)


def _join_riders(riders):
    if len(riders) == 1:
        return riders[0]

    def parts(ins, outs, scr):
        pi = po = ps = 0
        for rd in riders:
            ni, no, ns = len(rd.ins), len(rd.out_shapes), len(rd.scratch)
            yield rd, ins[pi:pi + ni], outs[po:po + no], scr[ps:ps + ns]
            pi, po, ps = pi + ni, po + no, ps + ns

    def start(ins, outs, scr):
        for rd, i, o, sc in parts(ins, outs, scr):
            rd.start(i, o, sc)

    def finish(ins, outs, scr):
        for rd, i, o, sc in parts(ins, outs, scr):
            rd.finish(i, o, sc)

    return _Rider(sum((rd.ins for rd in riders), []), sum((rd.out_shapes for rd in riders), []),
                  sum((rd.scratch for rd in riders), []), start, finish)


def _scatter_rider(grads, axes, rows=None):
    nw = len(grads)
    outs_shape = []
    for gr, ax in zip(grads, axes):
        shape = list(gr.shape)
        shape[ax] //= 4
        if rows is not None:
            assert ax == 1
            shape[0] = rows[1] - rows[0]
        outs_shape.append(jax.ShapeDtypeStruct((4,) + tuple(shape), gr.dtype))

    def copies(ins, outs, scr):
        send, recv, loc = scr
        x, y, c, chips = _place()
        cps, own = [], []
        for t in range(nw):
            width = ins[t].shape[axes[t]] // 4
            src = ins[t] if rows is None else ins[t].at[pl.ds(rows[0], rows[1] - rows[0]), :]
            own.append(pltpu.make_async_copy(_slab(src, axes[t], 2 * x + y, width), outs[t].at[3], loc.at[t]))
            for j, (px, py) in enumerate(chips):
                cps.append(pltpu.make_async_remote_copy(
                    src_ref=_slab(src, axes[t], 2 * px + py, width), dst_ref=outs[t].at[j],
                    send_sem=send.at[3 * t + j], recv_sem=recv.at[3 * t + j],
                    device_id=(px, py, c), device_id_type=MESH))
        return cps, own

    def start(ins, outs, scr):
        cps, own = copies(ins, outs, scr)
        for cp in cps + own:
            cp.start()

    def finish(ins, outs, scr):
        cps, own = copies(ins, outs, scr)
        for cp in cps:
            cp.wait_recv()
        for cp in own:
            cp.wait()
        for cp in cps:
            cp.wait_send()

    return _Rider(grads, outs_shape, [pltpu.SemaphoreType.DMA((3 * nw,)), pltpu.SemaphoreType.DMA((3 * nw,)),
                                      pltpu.SemaphoreType.DMA((nw,))], start, finish)


def _swap_rider(parts):
    nw = len(parts)

    def copies(ins, outs, scr):
        send, recv = scr
        x, y, c, _ = _place()
        return [pltpu.make_async_remote_copy(
            src_ref=ins[t], dst_ref=outs[t], send_sem=send.at[t], recv_sem=recv.at[t],
            device_id=(x, y, 1 - c), device_id_type=MESH) for t in range(nw)]

    def start(ins, outs, scr):
        for cp in copies(ins, outs, scr):
            cp.start()

    def finish(ins, outs, scr):
        cps = copies(ins, outs, scr)
        for cp in cps:
            cp.wait_recv()
        for cp in cps:
            cp.wait_send()

    return _Rider(parts, [jax.ShapeDtypeStruct(p.shape, p.dtype) for p in parts],
                  [pltpu.SemaphoreType.DMA((nw,)), pltpu.SemaphoreType.DMA((nw,))], start, finish)


def _sum_all_devices(buf, name):
    rows, cols = buf.shape

    def body(in_ref, o_ref, gat_ref, send, recv):
        x, y, c, _ = _place()
        me = 4 * x + 2 * y + c
        gat_ref[me] = in_ref[...]
        started = []
        for mask in range(1, 8):
            fx, fy, fc = (mask >> 2) & 1, (mask >> 1) & 1, mask & 1
            peer = (x + fx * (1 - 2 * x), y + fy * (1 - 2 * y), c + fc * (1 - 2 * c))
            cp = pltpu.make_async_remote_copy(
                src_ref=in_ref, dst_ref=gat_ref.at[me], send_sem=send.at[mask - 1], recv_sem=recv.at[mask - 1],
                device_id=peer, device_id_type=MESH)
            cp.start()
            started.append(cp)
        for cp in started:
            cp.wait_recv()
        for cp in started:
            cp.wait_send()
        acc = gat_ref[0]
        for t in range(1, 8):
            acc = acc + gat_ref[t]
        o_ref[...] = acc

    vm = pl.BlockSpec(memory_space=pltpu.VMEM)
    return _pcall(
        body, name=name, in_specs=[vm], out_specs=vm, out_shape=jax.ShapeDtypeStruct((rows, cols), F32),
        scratch=[pltpu.VMEM((8, rows, cols), F32), pltpu.SemaphoreType.DMA((7,)), pltpu.SemaphoreType.DMA((7,))],
    )(buf)


_UPD_T = 256


def _sum_partials(got, name):
    _, rows, cols = got.shape
    tr = min(_UPD_T, rows)

    def body(got_ref, o_ref):
        acc = got_ref[3].astype(F32)
        for j in range(3):
            acc = acc + got_ref[j].astype(F32)
        o_ref[...] = acc

    return _pcall(
        body, name=name, grid=(rows // tr,),
        in_specs=[pl.BlockSpec((4, tr, cols), lambda i: (0, i, 0))], out_specs=pl.BlockSpec((tr, cols), lambda i: (i, 0)),
        out_shape=jax.ShapeDtypeStruct((rows, cols), F32))(got)


def _adamw_math(w, gr, m, v):
    m = _B1 * m + (1.0 - _B1) * gr
    v = _B2 * v + (1.0 - _B2) * (gr * gr)
    m_hat = m / (1.0 - _B1 ** _STEP)
    v_hat = v / (1.0 - _B2 ** _STEP)
    delta = -_LR * (m_hat / (jnp.sqrt(v_hat) + _EPS) + _WD * w)
    return delta, m, v


def _adamw(w, m, v, parts, name):
    rows, cols = w.shape
    tr = min(_UPD_T, rows)
    npart = len(parts)

    def body(w_ref, m_ref, v_ref, *rest):
        p_refs, (g_ref, d_ref, nm_ref, nv_ref) = rest[:npart], rest[npart:]
        gr = p_refs[0][...]
        for p in p_refs[1:]:
            gr = gr + p[...]
        delta, nm, nv = _adamw_math(w_ref[...], gr, m_ref[...], v_ref[...])
        g_ref[...] = gr
        d_ref[...] = delta
        nm_ref[...] = nm
        nv_ref[...] = nv

    blk = pl.BlockSpec((tr, cols), lambda i: (i, 0))
    sh = jax.ShapeDtypeStruct((rows, cols), F32)
    return _pcall(body, name=name, grid=(rows // tr,), in_specs=[blk] * (3 + npart), out_specs=[blk] * 4,
                  out_shape=[sh] * 4)(w, m, v, *parts)


def _adamw_layers(w, m, v, parts, name):
    _, rows, cols = w.shape
    tr = min(_UPD_T, rows)
    npart = len(parts[0])

    def body(w_ref, m_ref, v_ref, *rest):
        p_refs, (g_ref, d_ref, nm_ref, nv_ref) = rest[:2 * npart], rest[2 * npart:]
        grs = []
        for layer in range(2):
            gr = p_refs[layer * npart][...]
            for p in p_refs[layer * npart + 1:(layer + 1) * npart]:
                gr = gr + p[...]
            grs.append(gr)
        gr = jnp.where(pl.program_id(0) == 0, grs[0], grs[1])
        delta, nm, nv = _adamw_math(w_ref[...], gr, m_ref[...], v_ref[...])
        g_ref[...] = gr
        d_ref[...] = delta
        nm_ref[...] = nm
        nv_ref[...] = nv

    blk = pl.BlockSpec((None, tr, cols), lambda l, i: (l, i, 0))

    def part_spec(layer):
        return pl.BlockSpec((tr, cols), lambda l, i: (jnp.where(l == layer, i, 0), 0))

    sh = jax.ShapeDtypeStruct(w.shape, F32)
    return _pcall(
        body, name=name, grid=(2, rows // tr),
        in_specs=[blk] * 3 + [part_spec(0)] * npart + [part_spec(1)] * npart, out_specs=[blk] * 4,
        out_shape=[sh] * 4)(w, m, v, *parts[0], *parts[1])


_PACK_W = 1024


def _pack(arrs, rows):
    flat = []
    for a in arrs:
        f = a.reshape(-1).astype(F32)
        pad = (-f.shape[0]) % _PACK_W
        flat.append(jnp.pad(f, (0, pad)))
    f = jnp.concatenate(flat)
    f = jnp.pad(f, (0, rows * _PACK_W - f.shape[0]))
    return f.reshape(rows, _PACK_W)


def _unpack(buf, shapes):
    flat = buf.reshape(-1)
    out, pos = [], 0
    for sh in shapes:
        size = math.prod(sh)
        out.append(flat[pos:pos + size].reshape(sh))
        pos += size + ((-size) % _PACK_W)
    return out


def _pack_rows(shapes):
    total = sum(-(-math.prod(sh) // _PACK_W) for sh in shapes)
    return -(-total // 8) * 8


def kernel(x, rel_bias, ab_norm, ab_w_in, ab_conv_w, ab_conv_b, ab_ln_g, ab_ln_b, ab_w_out, sc_norm, sc_w_in, sc_conv_w, sc_w_out, mlp_norm, mlp_w_up, mlp_w_down, final_norm, loss_target, m_rel_bias, m_ab_norm, m_ab_w_in, m_ab_conv_w, m_ab_conv_b, m_ab_ln_g, m_ab_ln_b, m_ab_w_out, m_sc_norm, m_sc_w_in, m_sc_conv_w, m_sc_w_out, m_mlp_norm, m_mlp_w_up, m_mlp_w_down, m_final_norm, v_rel_bias, v_ab_norm, v_ab_w_in, v_ab_conv_w, v_ab_conv_b, v_ab_ln_g, v_ab_ln_b, v_ab_w_out, v_sc_norm, v_sc_w_in, v_sc_conv_w, v_sc_w_out, v_mlp_norm, v_mlp_w_up, v_mlp_w_down, v_final_norm):
    s, d = x.shape[1], x.shape[2]
    dff = 4 * d
    c = _GROUP_COLS
    chip = 2 * lax.axis_index("x") + lax.axis_index("y")
    on_c0 = (lax.axis_index("c") == 0).astype(F32)
    h0 = x[0]
    tgt = loss_target[0]

    cw_sh, scn_sh, scw_sh = ab_conv_w[0], sc_norm, sc_conv_w[0]
    conv_w_full = lax.dynamic_update_slice(jnp.zeros((_CONV_K, c), F32), cw_sh * on_c0, (0, chip * cw_sh.shape[1]))
    scn_full = lax.dynamic_update_slice(jnp.zeros((1, d), F32), scn_sh * on_c0, (0, chip * scn_sh.shape[1]))
    scw_full = lax.dynamic_update_slice(jnp.zeros((3, d), F32), scw_sh * on_c0, (0, chip * scw_sh.shape[1]))
    small_shapes = [(_CONV_K, c), (1, d), (3, d)]
    small = _sum_all_devices(_pack([conv_w_full, scn_full, scw_full], _pack_rows(small_shapes)), "gather_small")
    conv_w, sc_g, sc_cw = _unpack(small, small_shapes)

    w_shards = [ab_w_in[0], ab_w_out[0], sc_w_in[0], sc_w_out[0], mlp_w_up[0], mlp_w_up[1],
                mlp_w_down[0], mlp_w_down[1]]
    w_axes = [1, 0, 1, 0, 1, 1, 0, 0]
    wb = [w.astype(BF16) for w in w_shards]
    full_w = [None] * 8

    def gather(idx):
        return _join_riders([_gather_halves_rider(wb[t], w_axes[t]) for t in idx])

    def put(idx, got_w):
        for t, w in zip(idx, got_w):
            full_w[t] = w

    buckets = _bucket_maps()
    bias = _bias_expand(rel_bias, buckets)
    n0, got_w = _rms_fwd(h0, ab_norm, "rms_fwd_ab", rider=_gather_halves_rider(wb[0], w_axes[0]))
    put([0], got_w)
    w_in = full_w[0]
    tm = min(1024, s)
    tm2 = min(2048, s)
    tmh = min(512, s)
    uc = _mm(n0, w_in, "nn", m=s, n=2 * c, k=d, tm=tm2, tn=2 * c, tk=d, out_dtype=BF16, name="proj_conv")
    uq, uk, uv = [], [], []
    for t, (nm, dst) in enumerate(zip("qkv", (uq, uk, uv))):
        res = _mm(n0, w_in, "nn", m=s, n=3 * c, k=d, tm=tm2, tn=c, tk=d, out_dtype=F32, name=f"proj_{nm}",
                  b_off=(0, 2 + 3 * t), split="o", rider=gather([1]) if t == 0 else None)
        if t == 0:
            res, got_w = res
            put([1], got_w)
        dst.append(res)
    uq, uk, uv = uq[0], uk[0], uv[0]
    cat, ca = _conv_a_fwd(uc, conv_w, ab_conv_b, ab_ln_g, ab_ln_b)
    outs, lses = [], []
    for g, (_, dil) in enumerate(_GROUPS):
        idx = ([4, 7], [6], [3, 5, 2])[g]
        (o, l), got_w = _attn_fwd(uq, uk, uv, bias, g, dil, 4 if dil <= 4 else 2, rider=gather(idx))
        put(idx, got_w)
        outs.append(o)
        lses.append(l)
    cat, lse = _attn_merge(outs, lses, cat)
    h1, n1 = _mm(cat, full_w[1], "nn", m=s, n=d, k=d, tm=tm, tn=d, tk=d, out_dtype=(F32, BF16), name="out_ab",
                 epi=_epi_add_rms, extras=(h0,), vecs=(mlp_norm[0:1],))

    def mlp_fwd(h, nrm, layer, next_gain=None):
        zr = _mm(nrm, full_w[4 + layer], "nn", m=s, n=dff, k=d, tm=tmh, tn=dff, tk=d, out_dtype=BF16,
                 name=f"mlp_up{layer}", epi=_epi_relu)
        kw = dict(m=s, n=d, k=dff, tm=tmh, tn=d, tk=dff, name=f"mlp_down{layer}", a_pro=_square, extras=(h,))
        if next_gain is None:
            return zr, _mm(zr, full_w[6 + layer], "nn", out_dtype=F32, epi=_epi_add, **kw), None
        hn, nn = _mm(zr, full_w[6 + layer], "nn", out_dtype=(F32, BF16), epi=_epi_add_rms, vecs=(next_gain,), **kw)
        return zr, hn, nn

    zr0, h2, n2 = mlp_fwd(h1, n1, 0, next_gain=sc_g)
    _, w_out, w_si, w_so, w_up0, w_up1, w_dn0, w_dn1 = full_w
    w_up, w_dn = [w_up0, w_up1], [w_dn0, w_dn1]
    u2 = _mm(n2, w_si, "nn", m=s, n=3 * d, k=d, tm=tmh, tn=3 * d, tk=d, out_dtype=BF16, name="proj_sc")
    scv = _short_conv_fwd(u2, sc_cw)
    h3, n3 = _mm(scv, w_so, "nn", m=s, n=d, k=d, tm=tm, tn=d, tk=d, out_dtype=(F32, BF16), name="out_sc",
                 epi=_epi_add_rms, extras=(h2,), vecs=(mlp_norm[1:2],))
    zr1, h4, _ = mlp_fwd(h3, n3, 1)

    dh4, dh4b, g_final, loss_part = _loss_head(h4, tgt, final_norm.reshape(1, d))
    tkw = min(2048, s)

    big_grads, sums = [None] * 8, [None] * 8

    def scatter(t):
        return _scatter_rider([big_grads[t]], [w_axes[t]])

    def arrived(t, got_t):
        sums[t] = _sum_partials(got_t[0], f"sum_partials{t}")

    def mlp_bwd(dh, dhb, h, nrm, zr, layer):
        dz = _mm(dhb, w_dn[layer], "nt", m=s, n=dff, k=d, tm=tmh, tn=dff, tk=d, out_dtype=BF16,
                 name=f"mlp_down{layer}_dx", epi=_epi_relu_sq_bwd, extras=(zr,))
        big_grads[6 + layer] = _mm(zr, dhb, "tn", m=dff, n=d, k=s, tm=1024, tn=d, tk=tkw, out_dtype=BF16,
                                   name=f"mlp_down{layer}_dw", a_pro=_square)
        big_grads[4 + layer] = _mm(nrm, dz, "tn", m=d, n=dff, k=s, tm=d, tn=1024, tk=tkw, out_dtype=BF16,
                                   name=f"mlp_up{layer}_dw")
        res = _mm(dz, w_up[layer], "nt", m=s, n=d, k=dff, tm=tmh, tn=d, tk=dff, out_dtype=(F32, BF16),
                  name=f"mlp_up{layer}_dx", rider=scatter(6) if layer == 0 else None, epi=_epi_rms_bwd,
                  extras=(h, dh), vecs=(mlp_norm[layer:layer + 1],), row_sum=True)
        if layer == 0:
            res, got_t = res
            arrived(6, got_t)
        return res

    dh3, dh3b, g_mn1 = mlp_bwd(dh4, dh4b, h3, n3, zr1, 1)

    dsc = _mm(dh3b, w_so, "nt", m=s, n=d, k=d, tm=tm, tn=d, tk=d, out_dtype=F32, name="out_sc_dx")
    big_grads[3] = _mm(scv, dh3b, "tn", m=d, n=d, k=s, tm=d, tn=d, tk=tkw, out_dtype=BF16, name="out_sc_dw")
    du2, g_sccw8 = _short_conv_bwd(u2, dsc, sc_cw)
    big_grads[2], got_t = _mm(n2, du2, "tn", m=d, n=3 * d, k=s, tm=d, tn=1024, tk=tkw, out_dtype=BF16,
                              name="proj_sc_dw", rider=scatter(3))
    arrived(3, got_t)
    dh2, dh2b, g_scn = _mm(
        du2, w_si, "nt", m=s, n=d, k=3 * d, tm=tmh, tn=d, tk=3 * d, out_dtype=(F32, BF16), name="proj_sc_dx",
        epi=_epi_rms_bwd, extras=(h2, dh3), vecs=(sc_g,), row_sum=True)

    dh1, dh1b, g_mn0 = mlp_bwd(dh2, dh2b, h1, n1, zr0, 0)

    dcat = _mm(dh1b, w_out, "nt", m=s, n=d, k=d, tm=tm, tn=d, tk=d, out_dtype=F32, name="out_ab_dx")
    big_grads[1] = _mm(cat, dh1b, "tn", m=d, n=d, k=s, tm=d, tn=d, tk=tkw, out_dtype=BF16, name="out_ab_dw")
    (dca, conv_stats), got_t = _conv_a_bwd_ln(ca, dcat, ab_ln_g, ab_ln_b, rider=scatter(1))
    arrived(1, got_t)
    (duc, g_cw32), got_t = _conv_a_bwd_conv(uc, dca, conv_w, rider=scatter(4))
    arrived(4, got_t)
    delta, dys = _attn_delta(dcat, cat)

    dqkv, ds_list = [], []
    for g, (_, dil) in enumerate(_GROUPS):
        late = (2, 5, 7)[g]
        (grads, dsa), got_t = _attn_bwd(uq, uk, uv, dys, lse, delta, bias, dqkv, g, dil, 4 if dil <= 4 else 1,
                                        rider=scatter(late))
        arrived(late, got_t)
        dqkv = [grads]
        ds_list.append(dsa)
    g_bias = _bias_reduce(jnp.concatenate(ds_list, axis=0), buckets)[:, :, 0].T
    dqkv = dqkv[0].reshape(9 * _PAIRS, s, _LANES)

    g_in_conv = _mm(n0, duc, "tn", m=d, n=2 * c, k=s, tm=d, tn=2 * c, tk=tkw, out_dtype=BF16,
                    name="proj_ab_dw_conv")
    g_in_qkv, sib_late = _mm(n0, dqkv, "tn", m=d, n=9 * c, k=s, tm=d, tn=3 * c, tk=min(1024, s), out_dtype=BF16,
                             name="proj_ab_dw_qkv", rider=_swap_rider(sums[1:]), split="b")
    big_grads[0] = jnp.concatenate([g_in_conv, g_in_qkv], axis=1)
    cut = d // 4
    dn0, (got_top,) = _mm(duc, w_in, "nt", m=s, n=d, k=2 * c, tm=tm, tn=d, tk=2 * c, out_dtype=F32,
                          name="proj_ab_dx_conv", rider=_scatter_rider([big_grads[0]], [w_axes[0]], rows=(0, cut)))
    dn0, (got_bot,) = _mm(dqkv, w_in[:, 2 * c:], "nt", m=s, n=d, k=9 * c, tm=tm, tn=d, tk=3 * c, out_dtype=F32,
                          name="proj_ab_dx_qkv", epi=_epi_add, extras=(dn0,), split="a",
                          rider=_scatter_rider([big_grads[0]], [w_axes[0]], rows=(cut, d)))
    grad_x, g_abn = _rms_bwd(dn0, h0, ab_norm, dh1, "rms_bwd_ab")
    sums[0] = jnp.concatenate([_sum_partials(got_top, "sum_partials0_top"),
                               _sum_partials(got_bot, "sum_partials0_bottom")], axis=0)
    sib = list(_run_rider(_swap_rider([sums[0]]), "swap_sibling_w_in")) + sib_late

    upd = [_adamw(w_shards[t], mm[0], vv[0], [sums[t], sib[t]], f"adamw{t}")
           for t, (mm, vv) in enumerate(((m_ab_w_in, v_ab_w_in), (m_ab_w_out, v_ab_w_out),
                                         (m_sc_w_in, v_sc_w_in), (m_sc_w_out, v_sc_w_out)))]
    upd_up = _adamw_layers(mlp_w_up, m_mlp_w_up, v_mlp_w_up, [[sums[4], sib[4]], [sums[5], sib[5]]], "adamw_up")
    upd_dn = _adamw_layers(mlp_w_down, m_mlp_w_down, v_mlp_w_down, [[sums[6], sib[6]], [sums[7], sib[7]]],
                           "adamw_down")

    full_shapes = [(_NUM_BUCKETS, rel_bias.shape[1]), (1, d), (_CONV_K, c), (1, c), (1, c), (1, c), (1, d),
                   (3, d), (2, d), (d,), (1, 1)]
    small_grads = [g_bias, g_abn, g_cw32[:_CONV_K], conv_stats[0:1], conv_stats[1:2], conv_stats[2:3], g_scn,
                   g_sccw8[:3], jnp.concatenate([g_mn0, g_mn1], axis=0), g_final.reshape(d), loss_part[0:1, 0:1]]
    tot = _unpack(_sum_all_devices(_pack(small_grads, _pack_rows(full_shapes)), "sum_small"), full_shapes)
    loss = tot.pop()[0, 0]
    for idx, sh in ((2, cw_sh), (6, scn_sh), (7, scw_sh)):
        width = sh.shape[1]
        tot[idx] = lax.dynamic_slice_in_dim(tot[idx], chip * width, width, axis=1)
    sm_w = [rel_bias, ab_norm, cw_sh, ab_conv_b, ab_ln_g, ab_ln_b, scn_sh, scw_sh, mlp_norm, final_norm]
    sm_m = [m_rel_bias, m_ab_norm, m_ab_conv_w[0], m_ab_conv_b, m_ab_ln_g, m_ab_ln_b, m_sc_norm, m_sc_conv_w[0],
            m_mlp_norm, m_final_norm]
    sm_v = [v_rel_bias, v_ab_norm, v_ab_conv_w[0], v_ab_conv_b, v_ab_ln_g, v_ab_ln_b, v_sc_norm, v_sc_conv_w[0],
            v_mlp_norm, v_final_norm]
    sh_shapes = [tuple(t.shape) for t in tot]
    rows = _pack_rows(sh_shapes)
    sm_upd = _adamw(_pack(sm_w, rows), _pack(sm_m, rows), _pack(sm_v, rows), [_pack(tot, rows)], "adamw_small")
    sm_g, sm_d, sm_nm, sm_nv = [_unpack(buf, sh_shapes) for buf in sm_upd]

    def assemble(kind, sm):
        big = [u[kind] for u in upd]
        return [sm[0], sm[1], big[0][None], sm[2][None], sm[3], sm[4], sm[5], big[1][None], sm[6], big[2][None],
                sm[7][None], big[3][None], sm[8], upd_up[kind], upd_dn[kind], sm[9]]

    res = [loss, grad_x[None]]
    for kind, sm in enumerate((sm_g, sm_d, sm_nm, sm_nv)):
        res += assemble(kind, sm)
    return tuple(res)
```

```python
import functools
import math

import jax
import jax.numpy as jnp
from jax import lax
from jax.experimental import pallas as pl
from jax.experimental.pallas import tpu as pltpu

F32 = jnp.float32
BF16 = jnp.bfloat16
MESH = pl.DeviceIdType.MESH

_GROUPS = ((128, 1), (512, 4), (2048, 16))
_STEPS = 128
_HEAD_DIM = 64
_GROUP_COLS = 512
_NUM_BUCKETS = 32
_MAX_DISTANCE = 2048
_CONV_K = 31
_HALO = 32
_SC_HALO = 16
_RMS_EPS = 1e-6
_LN_EPS = 1e-5
_NEG = -1e30
_LANES = 128
_VMEM_LIMIT = 56 * 1024 * 1024

_LR, _B1, _B2, _EPS, _WD, _STEP = 0.001, 0.9, 0.999, 1e-08, 0.01, 10


class _Rider:
    def __init__(self, ins, out_shapes, scratch, start, finish):
        self.ins, self.out_shapes, self.scratch = list(ins), list(out_shapes), list(scratch)
        self.start, self.finish = start, finish


def _pcall(body, *, name, out_shape, in_specs, out_specs, grid=None, scratch=(), aliases=None, rider=None):
    kw = {} if grid is None else {"grid": grid}
    cparams = pltpu.CompilerParams(vmem_limit_bytes=_VMEM_LIMIT)
    if rider is None:
        return pl.pallas_call(
            body, name=name, out_shape=out_shape, in_specs=in_specs, out_specs=out_specs,
            scratch_shapes=list(scratch), input_output_aliases=aliases or {},
            compiler_params=cparams, **kw)
    single = not isinstance(out_specs, (list, tuple))
    ospecs = [out_specs] if single else list(out_specs)
    oshapes = [out_shape] if single else list(out_shape)
    nin, nout, nscr = len(in_specs), len(ospecs), len(scratch)
    rin, rout = len(rider.ins), len(rider.out_shapes)

    def wrapped(*refs):
        h_in, r_in = refs[:nin], refs[nin:nin + rin]
        p = nin + rin
        h_out, r_out = refs[p:p + nout], refs[p + nout:p + nout + rout]
        p += nout + rout
        h_scr, r_scr = refs[p:p + nscr], refs[p + nscr:]
        ids = [pl.program_id(a) for a in range(len(grid))]
        first = functools.reduce(jnp.logical_and, [i == 0 for i in ids])
        last = functools.reduce(jnp.logical_and, [i == g - 1 for i, g in zip(ids, grid)])

        @pl.when(first)
        def _():
            rider.start(r_in, r_out, r_scr)

        body(*h_in, *h_out, *h_scr)

        @pl.when(last)
        def _():
            rider.finish(r_in, r_out, r_scr)

    call = pl.pallas_call(
        wrapped, name=name, out_shape=oshapes + rider.out_shapes,
        in_specs=list(in_specs) + [_ANY] * rin, out_specs=ospecs + [_ANY] * rout,
        scratch_shapes=list(scratch) + rider.scratch, input_output_aliases=aliases or {},
        compiler_params=cparams, **kw)

    def run(*operands):
        res = call(*operands, *rider.ins)
        host = res[0] if single else list(res[:nout])
        return host, list(res[nout:])

    return run


def _sig(x):
    return 1.0 / (1.0 + jnp.exp(-x))


_ANY = pl.BlockSpec(memory_space=pl.ANY)


def _lanes_of(ref):
    parts = [ref[p] for p in range(ref.shape[0])]
    return parts[0] if len(parts) == 1 else jnp.concatenate(parts, axis=1)


def _mm(a, b, mode, *, m, n, k, tm, tn, tk, out_dtype, name, epi=None, extras=(), b_off=(0, 0), rider=None,
        split="", vecs=(), a_pro=None, row_sum=False):
    nk = k // tk
    assert m % tm == 0 and n % tn == 0 and k % tk == 0
    o0, o1 = b_off
    if mode == "nn":
        a_spec = pl.BlockSpec((tm, tk), lambda i, j, kk: (i, kk))
        b_spec = pl.BlockSpec((tk, tn), lambda i, j, kk: (kk + o0, j + o1))
        dn = (((1,), (0,)), ((), ()))
    elif mode == "nt":
        a_spec = pl.BlockSpec((tm, tk), lambda i, j, kk: (i, kk))
        if "a" in split:
            a_spec = pl.BlockSpec((tk // _LANES, tm, _LANES), lambda i, j, kk: (kk, i, 0))
        b_spec = pl.BlockSpec((tn, tk), lambda i, j, kk: (j + o0, kk + o1))
        dn = (((1,), (1,)), ((), ()))
    else:
        a_spec = pl.BlockSpec((tk, tm), lambda i, j, kk: (kk, i))
        b_spec = pl.BlockSpec((tk, tn), lambda i, j, kk: (kk + o0, j + o1))
        if "b" in split:
            b_spec = pl.BlockSpec((tn // _LANES, tk, _LANES), lambda i, j, kk: (j, kk, 0))
        dn = (((0,), (0,)), ((), ()))
    o_spec = pl.BlockSpec((tm, tn), lambda i, j, kk: (i, j))
    e_spec = o_spec
    if "o" in split:
        o_spec = pl.BlockSpec((tn // _LANES, tm, _LANES), lambda i, j, kk: (j, i, 0))
    v_spec = pl.BlockSpec((1, tn), lambda i, j, kk: (0, j))
    ne = len(extras) + len(vecs)
    multi = isinstance(out_dtype, tuple)
    dts = out_dtype if multi else (out_dtype,)
    no = len(dts)
    nr = 1 if row_sum else 0
    assert not row_sum or tn == n

    def body(a_ref, b_ref, *rest):
        ex, o_refs = rest[:ne], rest[ne:ne + no]
        av = _lanes_of(a_ref) if "a" in split else a_ref[...]
        bv = _lanes_of(b_ref) if "b" in split else b_ref[...]
        if av.dtype != BF16:
            av = av.astype(BF16)
        if bv.dtype != BF16:
            bv = bv.astype(BF16)
        if a_pro is not None:
            av = a_pro(av)
        p = lax.dot_general(av, bv, dn, preferred_element_type=F32)

        def fin(x):
            if epi is not None:
                x = epi(x, *[e[...] for e in ex])
            if row_sum:
                row, x = x[-1], (x[:-1] if multi else x[0])
                row_ref = rest[ne + no]

                @pl.when(pl.program_id(0) == 0)
                def _():
                    row_ref[...] = row

                @pl.when(pl.program_id(0) > 0)
                def _():
                    row_ref[...] += row

            for o_ref, val, dt in zip(o_refs, x if multi else (x,), dts):
                if "o" in split:
                    for p in range(tn // _LANES):
                        o_ref[p] = val[:, p * _LANES:(p + 1) * _LANES].astype(dt)
                else:
                    o_ref[...] = val.astype(dt)

        if nk == 1:
            fin(p)
        else:
            acc = rest[ne + no + nr]
            kk = pl.program_id(2)

            @pl.when(kk == 0)
            def _():
                acc[...] = p

            @pl.when(kk > 0)
            def _():
                acc[...] += p

            @pl.when(kk == nk - 1)
            def _():
                fin(acc[...])

    oshape = (n // _LANES, m, _LANES) if "o" in split else (m, n)
    shapes = [jax.ShapeDtypeStruct(oshape, dt) for dt in dts]
    ospecs = [o_spec] * no
    if row_sum:
        shapes.append(jax.ShapeDtypeStruct((1, n), F32))
        ospecs.append(v_spec)
    lone = not multi and not row_sum
    return _pcall(
        body, name=name, grid=(m // tm, n // tn, nk),
        in_specs=[a_spec, b_spec] + [e_spec] * len(extras) + [v_spec] * len(vecs),
        out_specs=ospecs[0] if lone else ospecs, out_shape=shapes[0] if lone else shapes,
        scratch=[pltpu.VMEM((tm, tn), F32)] if nk > 1 else [], rider=rider,
    )(a, b, *extras, *vecs)


def _epi_add(x, r):
    return x + r


def _epi_relu(x):
    return jnp.maximum(x, 0.0)


def _square(x):
    return x * x


def _epi_relu_sq_bwd(da, zr):
    return da * (2.0 * zr.astype(F32))


def _epi_add_rms(x, r, g):
    h = x + r
    return h, h * lax.rsqrt(jnp.mean(h * h, axis=-1, keepdims=True) + _RMS_EPS) * g


def _epi_rms_bwd(dn, h, dh_in, g):
    dx, dg = _rms_bwd_math(dn, h, g)
    dh = dh_in + dx
    return dh, dh, dg


_ROW_T = 512


def _rms_fwd(h, g, name, rider=None):
    s, d = h.shape

    def body(h_ref, g_ref, o_ref):
        x = h_ref[...]
        r = lax.rsqrt(jnp.mean(x * x, axis=-1, keepdims=True) + _RMS_EPS)
        o_ref[...] = (x * r * g_ref[...]).astype(BF16)

    row = pl.BlockSpec((_ROW_T, d), lambda i: (i, 0))
    vec = pl.BlockSpec((1, d), lambda i: (0, 0))
    return _pcall(body, name=name, grid=(s // _ROW_T,), in_specs=[row, vec], out_specs=row,
                  out_shape=jax.ShapeDtypeStruct((s, d), BF16), rider=rider)(h, g)


def _rms_bwd_math(dn, x, g):
    r = lax.rsqrt(jnp.mean(x * x, axis=-1, keepdims=True) + _RMS_EPS)
    xhat = x * r
    dg = jnp.sum(dn * xhat, axis=0, keepdims=True)
    t = dn * g
    dx = r * (t - xhat * jnp.mean(t * xhat, axis=-1, keepdims=True))
    return dx, dg


def _rms_bwd(dn, h, g, dh_in, name):
    s, d = h.shape

    def body(dn_ref, h_ref, g_ref, dhi_ref, dh_ref, dg_ref):
        dx, dg = _rms_bwd_math(dn_ref[...], h_ref[...], g_ref[...])
        dh_ref[...] = dhi_ref[...] + dx

        @pl.when(pl.program_id(0) == 0)
        def _():
            dg_ref[...] = jnp.zeros_like(dg_ref)

        dg_ref[...] += dg

    row = pl.BlockSpec((_ROW_T, d), lambda i: (i, 0))
    vec = pl.BlockSpec((1, d), lambda i: (0, 0))
    return _pcall(
        body, name=name, grid=(s // _ROW_T,), in_specs=[row, row, vec, row], out_specs=[row, vec],
        out_shape=[jax.ShapeDtypeStruct((s, d), F32), jax.ShapeDtypeStruct((1, d), F32)])(dn, h, g, dh_in)


def _loss_head(h, tgt, g):
    s, d = h.shape

    def body(h_ref, t_ref, g_ref, dh_ref, dhb_ref, dg_ref, loss_ref):
        x, gv = h_ref[...], g_ref[...]
        r = lax.rsqrt(jnp.mean(x * x, axis=-1, keepdims=True) + _RMS_EPS)
        err = x * r * gv - t_ref[...]
        part = 0.5 * jnp.sum(jnp.mean(err * err, axis=-1, keepdims=True))
        dx, dg = _rms_bwd_math(err * (1.0 / d), x, gv)
        dh_ref[...] = dx
        dhb_ref[...] = dx.astype(BF16)

        @pl.when(pl.program_id(0) == 0)
        def _():
            dg_ref[...] = jnp.zeros_like(dg_ref)
            loss_ref[...] = jnp.zeros_like(loss_ref)

        dg_ref[...] += dg
        loss_ref[...] += jnp.full(loss_ref.shape, part, F32)

    row = pl.BlockSpec((_ROW_T, d), lambda i: (i, 0))
    vec = pl.BlockSpec((1, d), lambda i: (0, 0))
    one = pl.BlockSpec((1, _LANES), lambda i: (0, 0))
    return _pcall(
        body, name="loss_head", grid=(s // _ROW_T,), in_specs=[row, row, vec], out_specs=[row, row, vec, one],
        out_shape=[jax.ShapeDtypeStruct((s, d), F32), jax.ShapeDtypeStruct((s, d), BF16),
                   jax.ShapeDtypeStruct((1, d), F32), jax.ShapeDtypeStruct((1, _LANES), F32)])(h, tgt, g)


_CONV_T = 256
_CONV_RC = 64


def _conv_a_specs(s):
    c = _GROUP_COLS
    hb = _CONV_T // _HALO
    val = pl.BlockSpec((_CONV_T, c), lambda i: (i, 0))
    gate = pl.BlockSpec((_CONV_T, c), lambda i: (i, 1))
    hval = pl.BlockSpec((_HALO, c), lambda i: (jnp.maximum(i * hb - 1, 0), 0))
    hgate = pl.BlockSpec((_HALO, c), lambda i: (jnp.maximum(i * hb - 1, 0), 1))
    return val, gate, hval, hgate


def _fill_glu(val_ref, gate_ref, hval_ref, hgate_ref, hs_ref):
    i = pl.program_id(0)
    hs_ref[pl.ds(_HALO, _CONV_T), :] = val_ref[...].astype(F32) * _sig(gate_ref[...].astype(F32))
    halo = hval_ref[...].astype(F32) * _sig(hgate_ref[...].astype(F32))
    hs_ref[pl.ds(0, _HALO), :] = jnp.where(i > 0, halo, 0.0)


_SHIFT_ROWS = _CONV_T + _HALO - 8


def _fill_shifts(src_ref, sh_ref):
    for b in range(1, 8):
        sh_ref[b - 1] = src_ref[pl.ds(b, _SHIFT_ROWS), :]


def _tap_rows(src_ref, sh_ref, start, rows, lanes=slice(None)):
    b = start % 8
    if b == 0:
        return src_ref[pl.ds(start, rows), lanes]
    return sh_ref[b - 1, pl.ds(start - b, rows), lanes]


def _conv_rows(hs_ref, sh_ref, w_ref, r0, rows):
    off = _HALO - (_CONV_K - 1)
    acc = jnp.zeros((rows, _GROUP_COLS), F32)
    for kk in range(_CONV_K):
        acc = acc + w_ref[kk:kk + 1, :] * _tap_rows(hs_ref, sh_ref, r0 + off + kk, rows)
    return acc


def _ln_fwd(ca, g, b):
    mu = jnp.mean(ca, axis=-1, keepdims=True)
    xc = ca - mu
    rstd = lax.rsqrt(jnp.mean(xc * xc, axis=-1, keepdims=True) + _LN_EPS)
    xhat = xc * rstd
    return xhat, rstd, xhat * g + b


def _conv_a_fwd(uc, w, cb, lg, lb, rider=None):
    s = uc.shape[0]
    c = _GROUP_COLS

    def body(val_ref, gate_ref, hval_ref, hgate_ref, w_ref, cb_ref, lg_ref, lb_ref, o_ref, ca_ref, hs_ref, sh_ref):
        _fill_glu(val_ref, gate_ref, hval_ref, hgate_ref, hs_ref)
        _fill_shifts(hs_ref, sh_ref)
        for rc in range(_CONV_T // _CONV_RC):
            r0 = rc * _CONV_RC
            ca = _conv_rows(hs_ref, sh_ref, w_ref, r0, _CONV_RC) + cb_ref[...]
            ca_ref[pl.ds(r0, _CONV_RC), :] = ca
            _, _, ln = _ln_fwd(ca, lg_ref[...], lb_ref[...])
            o_ref[pl.ds(r0, _CONV_RC), :] = (ln * _sig(ln)).astype(BF16)

    val, gate, hval, hgate = _conv_a_specs(s)
    wspec = pl.BlockSpec((_CONV_K, c), lambda i: (0, 0))
    vec = pl.BlockSpec((1, c), lambda i: (0, 0))
    blk = pl.BlockSpec((_CONV_T, c), lambda i: (i, 0))
    return _pcall(
        body, name="conv_a_fwd", grid=(s // _CONV_T,),
        in_specs=[val, gate, hval, hgate, wspec, vec, vec, vec],
        out_specs=[blk, blk],
        out_shape=[jax.ShapeDtypeStruct((s, 2 * c), BF16), jax.ShapeDtypeStruct((s, c), F32)],
        scratch=[pltpu.VMEM((_CONV_T + _HALO, c), F32), pltpu.VMEM((7, _SHIFT_ROWS, c), F32)],
        rider=rider)(uc, uc, uc, uc, w, cb, lg, lb)


def _conv_a_bwd_ln(ca_all, dcat, lg, lb, rider=None):
    s = ca_all.shape[0]
    c = _GROUP_COLS

    def body(ca_ref, dy_ref, lg_ref, lb_ref, dca_ref, st_ref):
        @pl.when(pl.program_id(0) == 0)
        def _():
            st_ref[...] = jnp.zeros_like(st_ref)

        for rc in range(_CONV_T // _CONV_RC):
            r0 = rc * _CONV_RC
            ca = ca_ref[pl.ds(r0, _CONV_RC), :]
            xhat, rstd, ln = _ln_fwd(ca, lg_ref[...], lb_ref[...])
            sg = _sig(ln)
            dln = dy_ref[pl.ds(r0, _CONV_RC), :] * (sg * (1.0 + ln * (1.0 - sg)))
            dxh = dln * lg_ref[...]
            dca = rstd * (dxh - jnp.mean(dxh, axis=-1, keepdims=True)
                          - xhat * jnp.mean(dxh * xhat, axis=-1, keepdims=True))
            dca_ref[pl.ds(r0, _CONV_RC), :] = dca
            st_ref[0:1, :] += jnp.sum(dca, axis=0, keepdims=True)
            st_ref[1:2, :] += jnp.sum(dln * xhat, axis=0, keepdims=True)
            st_ref[2:3, :] += jnp.sum(dln, axis=0, keepdims=True)

    blk = pl.BlockSpec((_CONV_T, c), lambda i: (i, 0))
    vec = pl.BlockSpec((1, c), lambda i: (0, 0))
    st = pl.BlockSpec((8, c), lambda i: (0, 0))
    return _pcall(
        body, name="conv_a_bwd_ln", grid=(s // _CONV_T,),
        in_specs=[blk, blk, vec, vec], out_specs=[blk, st],
        out_shape=[jax.ShapeDtypeStruct((s, c), F32), jax.ShapeDtypeStruct((8, c), F32)],
        rider=rider)(ca_all, dcat, lg, lb)


def _conv_a_bwd_conv(uc, dca, w, rider=None):
    s = uc.shape[0]
    c = _GROUP_COLS
    nblk = s // _CONV_T
    hb = _CONV_T // _HALO
    off = _HALO - (_CONV_K - 1)

    def body(val_ref, gate_ref, hval_ref, hgate_ref, d_ref, dn_ref, w_ref, du_ref, dw_ref, hs_ref, ds_ref,
             hsh_ref, dsh_ref, dwa_ref):
        i = pl.program_id(0)
        _fill_glu(val_ref, gate_ref, hval_ref, hgate_ref, hs_ref)
        ds_ref[pl.ds(0, _CONV_T), :] = d_ref[...]
        ds_ref[pl.ds(_CONV_T, _HALO), :] = jnp.where(i < nblk - 1, dn_ref[...], 0.0)
        _fill_shifts(hs_ref, hsh_ref)
        _fill_shifts(ds_ref, dsh_ref)

        @pl.when(i == 0)
        def _():
            dwa_ref[...] = jnp.zeros_like(dwa_ref)

        rows = 32
        for r0 in range(0, _CONV_T, rows):
            dcur = ds_ref[pl.ds(r0, rows), :]
            dh = jnp.zeros((rows, c), F32)
            for kk in range(_CONV_K):
                dh = dh + w_ref[kk:kk + 1, :] * _tap_rows(ds_ref, dsh_ref, r0 + _CONV_K - 1 - kk, rows)
                prod = dcur * _tap_rows(hs_ref, hsh_ref, r0 + off + kk, rows)
                dwa_ref[pl.ds(8 * kk, 8), :] += sum(prod[t:t + 8] for t in range(0, rows, 8))
            v = val_ref[pl.ds(r0, rows), :].astype(F32)
            sg = _sig(gate_ref[pl.ds(r0, rows), :].astype(F32))
            du_ref[pl.ds(r0, rows), pl.ds(0, c)] = (dh * sg).astype(BF16)
            du_ref[pl.ds(r0, rows), pl.ds(c, c)] = (dh * v * sg * (1.0 - sg)).astype(BF16)

        @pl.when(i == nblk - 1)
        def _():
            dw_ref[...] = jnp.zeros_like(dw_ref)
            for kk in range(_CONV_K):
                dw_ref[kk:kk + 1, :] = jnp.sum(dwa_ref[pl.ds(8 * kk, 8), :], axis=0, keepdims=True)

    val, gate, hval, hgate = _conv_a_specs(s)
    blk = pl.BlockSpec((_CONV_T, c), lambda i: (i, 0))
    nxt = pl.BlockSpec((_HALO, c), lambda i: (jnp.minimum((i + 1) * hb, s // _HALO - 1), 0))
    wspec = pl.BlockSpec((_CONV_K, c), lambda i: (0, 0))
    return _pcall(
        body, name="conv_a_bwd_conv", grid=(nblk,),
        in_specs=[val, gate, hval, hgate, blk, nxt, wspec],
        out_specs=[pl.BlockSpec((_CONV_T, 2 * c), lambda i: (i, 0)), pl.BlockSpec((_HALO, c), lambda i: (0, 0))],
        out_shape=[jax.ShapeDtypeStruct((s, 2 * c), BF16), jax.ShapeDtypeStruct((_HALO, c), F32)],
        scratch=[pltpu.VMEM((_CONV_T + _HALO, c), F32), pltpu.VMEM((_CONV_T + _HALO, c), F32),
                 pltpu.VMEM((7, _SHIFT_ROWS, c), F32), pltpu.VMEM((7, _SHIFT_ROWS, c), F32),
                 pltpu.VMEM((8 * _HALO, c), F32)],
        rider=rider,
    )(uc, uc, uc, uc, dca, dca, w)


_SC_T = 256
_SC_RC = 32
_SC_LC = 512


def _sc_chunks(d):
    return [(pl.ds(r0, _SC_RC), pl.ds(l0, _SC_LC)) for r0 in range(0, _SC_T, _SC_RC) for l0 in range(0, d, _SC_LC)]


def _short_conv_fwd(u2, w):
    s, d3 = u2.shape
    d = d3 // 3
    hb = _SC_T // _SC_HALO

    def body(b_ref, c_ref, v_ref, hc_ref, hv_ref, w_ref, o_ref, cs_ref):
        i = pl.program_id(0)
        cs_ref[pl.ds(0, _SC_HALO), :] = jnp.where(i > 0, hc_ref[...].astype(F32) * hv_ref[...].astype(F32), 0.0)
        for rows, lanes in _sc_chunks(d):
            cs_ref[pl.ds(_SC_HALO + rows.start, _SC_RC), lanes] = (
                c_ref[rows, lanes].astype(F32) * v_ref[rows, lanes].astype(F32))
        for rows, lanes in _sc_chunks(d):
            taps = [cs_ref[pl.ds(_SC_HALO - 2 + kk + rows.start, _SC_RC), lanes] for kk in range(3)]
            conv = w_ref[0:1, lanes] * taps[0] + w_ref[1:2, lanes] * taps[1] + w_ref[2:3, lanes] * taps[2]
            o_ref[rows, lanes] = (b_ref[rows, lanes].astype(F32) * conv).astype(BF16)

    def col(j):
        return pl.BlockSpec((_SC_T, d), lambda i: (i, j))

    def halo(j):
        return pl.BlockSpec((_SC_HALO, d), lambda i: (jnp.maximum(i * hb - 1, 0), j))

    return _pcall(
        body, name="short_conv_fwd", grid=(s // _SC_T,),
        in_specs=[col(0), col(1), col(2), halo(1), halo(2), pl.BlockSpec((3, d), lambda i: (0, 0))],
        out_specs=pl.BlockSpec((_SC_T, d), lambda i: (i, 0)),
        out_shape=jax.ShapeDtypeStruct((s, d), BF16),
        scratch=[pltpu.VMEM((_SC_T + _SC_HALO, d), F32)])(u2, u2, u2, u2, u2, w)


def _short_conv_bwd(u2, dsc, w, rider=None):
    s, d3 = u2.shape
    d = d3 // 3
    hb = _SC_T // _SC_HALO
    nblk = s // _SC_T

    def body(b_ref, c_ref, v_ref, hc_ref, hv_ref, nb_ref, d_ref, nd_ref, w_ref, du_ref, dw_ref, cs_ref, ds_ref):
        i = pl.program_id(0)
        cs_ref[pl.ds(0, _SC_HALO), :] = jnp.where(i > 0, hc_ref[...].astype(F32) * hv_ref[...].astype(F32), 0.0)
        ds_ref[pl.ds(_SC_T, _SC_HALO), :] = jnp.where(i < nblk - 1, nd_ref[...] * nb_ref[...].astype(F32), 0.0)
        for rows, lanes in _sc_chunks(d):
            cs_ref[pl.ds(_SC_HALO + rows.start, _SC_RC), lanes] = (
                c_ref[rows, lanes].astype(F32) * v_ref[rows, lanes].astype(F32))
            ds_ref[rows, lanes] = d_ref[rows, lanes] * b_ref[rows, lanes].astype(F32)

        @pl.when(i == 0)
        def _():
            dw_ref[...] = jnp.zeros_like(dw_ref)

        for l0 in range(0, d, _SC_LC):
            lanes = pl.ds(l0, _SC_LC)
            dw_acc = [jnp.zeros((8, _SC_LC), F32)] * 3
            for r0 in range(0, _SC_T, _SC_RC):
                rows = pl.ds(r0, _SC_RC)
                taps = [cs_ref[pl.ds(_SC_HALO - 2 + kk + r0, _SC_RC), lanes] for kk in range(3)]
                conv = w_ref[0:1, lanes] * taps[0] + w_ref[1:2, lanes] * taps[1] + w_ref[2:3, lanes] * taps[2]
                dconv = ds_ref[rows, lanes]
                dcv = (w_ref[2:3, lanes] * dconv + w_ref[1:2, lanes] * ds_ref[pl.ds(r0 + 1, _SC_RC), lanes]
                       + w_ref[0:1, lanes] * ds_ref[pl.ds(r0 + 2, _SC_RC), lanes])
                du_ref[rows, lanes] = (d_ref[rows, lanes] * conv).astype(BF16)
                du_ref[rows, pl.ds(d + l0, _SC_LC)] = (dcv * v_ref[rows, lanes].astype(F32)).astype(BF16)
                du_ref[rows, pl.ds(2 * d + l0, _SC_LC)] = (dcv * c_ref[rows, lanes].astype(F32)).astype(BF16)
                for kk in range(3):
                    prod = dconv * taps[kk]
                    dw_acc[kk] = dw_acc[kk] + sum(prod[t:t + 8] for t in range(0, _SC_RC, 8))
            for kk in range(3):
                dw_ref[kk:kk + 1, lanes] += jnp.sum(dw_acc[kk], axis=0, keepdims=True)

    def col(j):
        return pl.BlockSpec((_SC_T, d), lambda i: (i, j))

    def halo(j):
        return pl.BlockSpec((_SC_HALO, d), lambda i: (jnp.maximum(i * hb - 1, 0), j))

    def nxt(j):
        return pl.BlockSpec((_SC_HALO, d), lambda i: (jnp.minimum((i + 1) * hb, s // _SC_HALO - 1), j))

    return _pcall(
        body, name="short_conv_bwd", grid=(nblk,),
        in_specs=[col(0), col(1), col(2), halo(1), halo(2), nxt(0), col(0), nxt(0),
                  pl.BlockSpec((3, d), lambda i: (0, 0))],
        out_specs=[pl.BlockSpec((_SC_T, d3), lambda i: (i, 0)), pl.BlockSpec((8, d), lambda i: (0, 0))],
        out_shape=[jax.ShapeDtypeStruct((s, d3), BF16), jax.ShapeDtypeStruct((8, d), F32)],
        scratch=[pltpu.VMEM((_SC_T + _SC_HALO, d), F32), pltpu.VMEM((_SC_T + _SC_HALO, d), F32)],
        rider=rider,
    )(u2, u2, u2, u2, u2, u2, dsc, dsc, w)


def _bucket_maps():
    a_idx = jnp.arange(_STEPS)[:, None]
    c_idx = jnp.arange(2 * _STEPS)[None, :]
    mdist = jnp.clip(a_idx + _STEPS - c_idx, 0, _STEPS)
    max_exact = _NUM_BUCKETS // 2
    maps = []
    for _, dil in _GROUPS:
        nn = mdist * dil
        nf = jnp.maximum(nn, 1).astype(F32)
        large = max_exact + (jnp.log(nf / max_exact) / math.log(_MAX_DISTANCE / max_exact)
                             * (_NUM_BUCKETS - max_exact)).astype(jnp.int32)
        maps.append(jnp.where(nn < max_exact, nn, jnp.minimum(large, _NUM_BUCKETS - 1)).astype(jnp.int32))
    return jnp.stack(maps, axis=0)


def _bias_expand(rel_bias, buckets):
    nh = rel_bias.shape[1]

    def body(rb_ref, bk_ref, o_ref):
        h = pl.program_id(0)
        bk = bk_ref[0]
        acc = jnp.zeros(bk.shape, F32)
        for b in range(_NUM_BUCKETS):
            acc = jnp.where(bk == b, rb_ref[b, h], acc)
        a = lax.broadcasted_iota(jnp.int32, bk.shape, 0)
        c = lax.broadcasted_iota(jnp.int32, bk.shape, 1)
        mdist = a + _STEPS - c
        o_ref[0] = jnp.where((mdist >= 0) & (mdist <= _STEPS), acc, _NEG)

    return _pcall(
        body, name="bias_expand", grid=(nh,),
        in_specs=[pl.BlockSpec(memory_space=pltpu.SMEM),
                  pl.BlockSpec((1, _STEPS, 2 * _STEPS), lambda h: (h // 8, 0, 0))],
        out_specs=pl.BlockSpec((1, _STEPS, 2 * _STEPS), lambda h: (h, 0, 0)),
        out_shape=jax.ShapeDtypeStruct((nh, _STEPS, 2 * _STEPS), F32))(rel_bias, buckets)


def _bias_reduce(ds_all, buckets):
    nh = ds_all.shape[0]

    def body(ds_ref, bk_ref, o_ref):
        t, bk = ds_ref[0], bk_ref[0]
        rows = lax.broadcasted_iota(jnp.int32, (_NUM_BUCKETS, _LANES), 0)
        out = jnp.zeros((_NUM_BUCKETS, _LANES), F32)
        for b in range(_NUM_BUCKETS):
            out = jnp.where(rows == b, jnp.sum(jnp.where(bk == b, t, 0.0)), out)
        o_ref[0] = out

    blk = pl.BlockSpec((1, _STEPS, 2 * _STEPS), lambda h: (h, 0, 0))
    return _pcall(
        body, name="bias_reduce", grid=(nh,),
        in_specs=[blk, pl.BlockSpec((1, _STEPS, 2 * _STEPS), lambda h: (h // 8, 0, 0))],
        out_specs=pl.BlockSpec((1, _NUM_BUCKETS, _LANES), lambda h: (h, 0, 0)),
        out_shape=jax.ShapeDtypeStruct((nh, _NUM_BUCKETS, _LANES), F32))(ds_all, buckets)


def _sub_residues(dil):
    return 4 if dil % 16 == 0 else 1


def _strided_rows(ref, tmp_ref, p, r, dil):
    sub = _sub_residues(dil)
    if dil == 1:
        return [ref[p]]
    if sub == 1:
        return [ref[p, pl.ds(r, _STEPS, stride=dil), :]]
    tmp_ref[...] = ref[p, pl.ds(r, _STEPS * sub, stride=dil // sub), :]
    return [tmp_ref[pl.ds(q, _STEPS, stride=sub), :] for q in range(sub)]


def _store_strided(ref, tmp_ref, p, r, dil, vals):
    sub = _sub_residues(dil)
    if dil == 1:
        ref[p] = vals[0]
    elif sub == 1:
        ref[p, pl.ds(r, _STEPS, stride=dil), :] = vals[0]
    else:
        for q, val in enumerate(vals):
            tmp_ref[pl.ds(q, _STEPS, stride=sub), :] = val
        ref[p, pl.ds(r, _STEPS * sub, stride=dil // sub), :] = tmp_ref[...]


def _tmp_rows(dil, count):
    sub = _sub_residues(dil)
    return [pltpu.VMEM((_STEPS * sub, _LANES), F32)] * count if sub > 1 else []


def _head_masks():
    lane = lax.broadcasted_iota(jnp.int32, (1, _LANES), 1)
    return [lane < _HEAD_DIM, lane >= _HEAD_DIM]


def _stack_heads(x2, masks):
    return jnp.concatenate([jnp.where(masks[0], x2, 0), jnp.where(masks[1], x2, 0)], axis=0)


def _unstack_heads(y, masks):
    return jnp.where(masks[0], y[:_STEPS], y[_STEPS:])


def _scores(qs2, k2, b_ref, j, first):
    sc = lax.dot_general(qs2, k2, (((1,), (1,)), ((), ())), preferred_element_type=F32)
    sc = sc * (_HEAD_DIM ** -0.5) + jnp.concatenate([b_ref[2 * j], b_ref[2 * j + 1]], axis=0)
    col = lax.broadcasted_iota(jnp.int32, sc.shape, 1)
    return jnp.where(jnp.logical_and(first, col < _STEPS), _NEG, sc)


_PAIRS = _GROUP_COLS // _LANES


def _attn_fwd(uq, uk, uv, bias, g, dil, pp, rider=None):
    s = uq.shape[1]
    rb = _STEPS * dil
    nb = s // rb
    npb = _PAIRS // pp

    sub = _sub_residues(dil)

    def body(q_ref, kc_ref, kp_ref, vc_ref, vp_ref, b_ref, o_ref, l_ref, *tmp):
        tmp = tmp + (None,) * 7
        n, r = pl.program_id(1), pl.program_id(2)
        first = n == 0
        masks = _head_masks()
        for j in range(pp):
            qs = _strided_rows(q_ref, tmp[0], j, r, dil)
            kps, kcs = _strided_rows(kp_ref, tmp[1], j, r, dil), _strided_rows(kc_ref, tmp[2], j, r, dil)
            vps, vcs = _strided_rows(vp_ref, tmp[3], j, r, dil), _strided_rows(vc_ref, tmp[4], j, r, dil)
            o_res, l_res = [], []
            for q in range(sub):
                q2 = qs[q].astype(BF16)
                k2 = jnp.concatenate([kps[q], kcs[q]], axis=0).astype(BF16)
                v2 = jnp.concatenate([vps[q], vcs[q]], axis=0).astype(BF16)
                sc = _scores(_stack_heads(q2, masks), k2, b_ref, j, first)
                mx = jnp.max(sc, axis=-1, keepdims=True)
                p = jnp.exp(sc - mx)
                den = jnp.sum(p, axis=-1, keepdims=True)
                o2 = jnp.dot(p.astype(BF16), v2, preferred_element_type=F32) / den
                o_res.append(_unstack_heads(o2, masks))
                l_res.append(_unstack_heads(jnp.broadcast_to(mx + jnp.log(den), o2.shape), masks))
            _store_strided(o_ref, tmp[5], j, r, dil, o_res)
            _store_strided(l_ref, tmp[6], j, r, dil, l_res)

    cur = pl.BlockSpec((pp, rb, _LANES), lambda hb, n, r: (g * npb + hb, n, 0))
    prev = pl.BlockSpec((pp, rb, _LANES), lambda hb, n, r: (g * npb + hb, jnp.maximum(n - 1, 0), 0))
    bspec = pl.BlockSpec((2 * pp, _STEPS, 2 * _STEPS), lambda hb, n, r: (g * npb + hb, 0, 0))
    ospec = pl.BlockSpec((pp, rb, _LANES), lambda hb, n, r: (hb, n, 0))
    sh = jax.ShapeDtypeStruct((_PAIRS, s, _LANES), F32)
    return _pcall(
        body, name=f"attn_fwd_g{g}", grid=(npb, nb, dil // sub),
        in_specs=[cur, cur, prev, cur, prev, bspec], out_specs=[ospec, ospec], out_shape=[sh, sh],
        scratch=_tmp_rows(dil, 7), rider=rider,
    )(uq, uk, uk, uv, uv, bias)


def _attn_merge(outs, lses, cat):
    s = outs[0].shape[1]
    c = _GROUP_COLS

    def body(o0, o1, o2, l0, l1, l2, cat_in, cat_ref, lse_ref):
        del cat_in
        a0, a1, a2 = l0[...], l1[...], l2[...]
        mx = jnp.maximum(jnp.maximum(a0, a1), a2)
        w0, w1, w2 = jnp.exp(a0 - mx), jnp.exp(a1 - mx), jnp.exp(a2 - mx)
        den = w0 + w1 + w2
        y = ((w0 * o0[...] + w1 * o1[...] + w2 * o2[...]) / den).astype(BF16)
        for p in range(_PAIRS):
            cat_ref[:, p * _LANES:(p + 1) * _LANES] = y[p]
        lse_ref[...] = mx + jnp.log(den)

    blk = pl.BlockSpec((_PAIRS, _ROW_T, _LANES), lambda i: (0, i, 0))
    return _pcall(
        body, name="attn_merge", grid=(s // _ROW_T,),
        in_specs=[blk] * 6 + [_ANY],
        out_specs=[pl.BlockSpec((_ROW_T, c), lambda i: (i, 1)), blk],
        out_shape=[jax.ShapeDtypeStruct(cat.shape, BF16), jax.ShapeDtypeStruct((_PAIRS, s, _LANES), F32)],
        aliases={6: 0})(*outs, *lses, cat)


def _attn_delta(dcat, cat):
    s = dcat.shape[0]
    c = _GROUP_COLS
    seg = (jnp.arange(c)[:, None] // _HEAD_DIM == jnp.arange(c)[None, :] // _HEAD_DIM).astype(BF16)

    def body(dy_ref, y_ref, seg_ref, dl_ref, dys_ref):
        dy = dy_ref[...]
        prod = dy * y_ref[...].astype(F32)
        hi = prod.astype(BF16)
        lo = (prod - hi.astype(F32)).astype(BF16)
        dl = (jnp.dot(hi, seg_ref[...], preferred_element_type=F32)
              + jnp.dot(lo, seg_ref[...], preferred_element_type=F32))
        for p in range(_PAIRS):
            dl_ref[p] = dl[:, p * _LANES:(p + 1) * _LANES]
            dys_ref[p] = dy[:, p * _LANES:(p + 1) * _LANES]

    right = pl.BlockSpec((_ROW_T, c), lambda i: (i, 1))
    blk = pl.BlockSpec((_PAIRS, _ROW_T, _LANES), lambda i: (0, i, 0))
    sh = jax.ShapeDtypeStruct((_PAIRS, s, _LANES), F32)
    return _pcall(
        body, name="attn_delta", grid=(s // _ROW_T,),
        in_specs=[right, right, pl.BlockSpec((c, c), lambda i: (0, 0))],
        out_specs=[blk, blk], out_shape=[sh, sh])(dcat, cat, seg)


def _attn_bwd(uq, uk, uv, dys, lse, delta, bias, prev_grads, g, dil, pp, rider=None):
    s = uq.shape[1]
    rb = _STEPS * dil
    nb = s // rb
    npb = _PAIRS // pp
    scale = _HEAD_DIM ** -0.5

    sub = _sub_residues(dil)

    def body(q_ref, kc_ref, kp_ref, vc_ref, vp_ref, dy_ref, l_ref, dl_ref, b_ref, *rest):
        rest = rest[len(prev_grads):]
        dqkv_ref, dsa_ref, dqc_ref, dkc_ref, dvc_ref = rest[:5]
        dq_ref, dk_ref, dv_ref = dqkv_ref.at[0], dqkv_ref.at[1], dqkv_ref.at[2]
        tmp = rest[5:] + (None,) * 11
        n, r = pl.program_id(1), pl.program_id(2)

        def carry_slot(j, q):
            return ((r + (dil // sub) * q) * pp + j) if sub > 1 else r * pp + j

        @pl.when(jnp.logical_and(n == 0, r == 0))
        def _():
            dsa_ref[...] = jnp.zeros_like(dsa_ref)

        @pl.when(n == 0)
        def _():
            for j in range(pp):
                for q in range(sub):
                    for carry in (dqc_ref, dkc_ref, dvc_ref):
                        carry[carry_slot(j, q)] = jnp.zeros((_STEPS, _LANES), F32)

        @pl.when(n < nb)
        def _():
            first = n == 0
            masks = _head_masks()
            for j in range(pp):
                qs = _strided_rows(q_ref, tmp[0], j, r, dil)
                kps, kcs = _strided_rows(kp_ref, tmp[1], j, r, dil), _strided_rows(kc_ref, tmp[2], j, r, dil)
                vps, vcs = _strided_rows(vp_ref, tmp[3], j, r, dil), _strided_rows(vc_ref, tmp[4], j, r, dil)
                dys_ = _strided_rows(dy_ref, tmp[5], j, r, dil)
                lses = _strided_rows(l_ref, tmp[6], j, r, dil)
                dls = _strided_rows(dl_ref, tmp[7], j, r, dil)
                ds_sum = [jnp.zeros((_STEPS, 2 * _STEPS), F32)] * 2
                dq_res, dk_res, dv_res = [], [], []
                for q in range(sub):
                    q2 = qs[q].astype(BF16)
                    k2 = jnp.concatenate([kps[q], kcs[q]], axis=0).astype(BF16)
                    v2 = jnp.concatenate([vps[q], vcs[q]], axis=0).astype(BF16)
                    dy2 = dys_[q].astype(BF16)
                    qs2, dys2 = _stack_heads(q2, masks), _stack_heads(dy2, masks)
                    per_row = lambda st: jnp.concatenate([st[:, 0:1], st[:, _HEAD_DIM:_HEAD_DIM + 1]], axis=0)
                    sc = _scores(qs2, k2, b_ref, j, first)
                    p = jnp.exp(sc - per_row(lses[q]))
                    dp = lax.dot_general(dys2, v2, (((1,), (1,)), ((), ())), preferred_element_type=F32)
                    ds = p * (dp - per_row(dls[q]))
                    ds_sum[0] = ds_sum[0] + ds[:_STEPS]
                    ds_sum[1] = ds_sum[1] + ds[_STEPS:]
                    dsb = ds.astype(BF16)
                    dq_p = _unstack_heads(jnp.dot(dsb, k2, preferred_element_type=F32), masks)
                    tdn = (((0,), (0,)), ((), ()))
                    dk_p = lax.dot_general(dsb, qs2, tdn, preferred_element_type=F32) * scale
                    dv_p = lax.dot_general(p.astype(BF16), dys2, tdn, preferred_element_type=F32)
                    slot = carry_slot(j, q)
                    dq_res.append(dqc_ref[slot])
                    dk_res.append(dkc_ref[slot] + dk_p[:_STEPS])
                    dv_res.append(dvc_ref[slot] + dv_p[:_STEPS])
                    dqc_ref[slot] = dq_p * scale
                    dkc_ref[slot] = dk_p[_STEPS:]
                    dvc_ref[slot] = dv_p[_STEPS:]
                for hh in range(2):
                    dsa_ref[2 * j + hh] += ds_sum[hh]
                _store_strided(dq_ref, tmp[8], j, r, dil, dq_res)
                _store_strided(dk_ref, tmp[9], j, r, dil, dk_res)
                _store_strided(dv_ref, tmp[10], j, r, dil, dv_res)

        @pl.when(n == nb)
        def _():
            for j in range(pp):
                for ref, carry, t in ((dq_ref, dqc_ref, 8), (dk_ref, dkc_ref, 9), (dv_ref, dvc_ref, 10)):
                    _store_strided(ref, tmp[t], j, r, dil, [carry[carry_slot(j, q)] for q in range(sub)])

    def clamp(n):
        return jnp.minimum(n, nb - 1)

    cur = pl.BlockSpec((pp, rb, _LANES), lambda hb, n, r: (g * npb + hb, clamp(n), 0))
    prev = pl.BlockSpec((pp, rb, _LANES), lambda hb, n, r: (g * npb + hb, jnp.maximum(clamp(n) - 1, 0), 0))
    stat = pl.BlockSpec((pp, rb, _LANES), lambda hb, n, r: (hb, clamp(n), 0))
    bspec = pl.BlockSpec((2 * pp, _STEPS, 2 * _STEPS), lambda hb, n, r: (g * npb + hb, 0, 0))
    late = pl.BlockSpec((3, pp, rb, _LANES), lambda hb, n, r: (0, g * npb + hb, jnp.maximum(n - 1, 0), 0))
    dsspec = pl.BlockSpec((2 * pp, _STEPS, 2 * _STEPS), lambda hb, n, r: (hb, 0, 0))
    np_ = len(prev_grads)
    carry = pltpu.VMEM((dil * pp, _STEPS, _LANES), F32)
    return _pcall(
        body, name=f"attn_bwd_g{g}", grid=(npb, nb + 1, dil // sub),
        in_specs=[cur, cur, prev, cur, prev, stat, stat, stat, bspec] + [_ANY] * np_,
        out_specs=[late, dsspec],
        out_shape=[jax.ShapeDtypeStruct((3, 3 * _PAIRS, s, _LANES), F32),
                   jax.ShapeDtypeStruct((8, _STEPS, 2 * _STEPS), F32)],
        scratch=[carry, carry, carry] + _tmp_rows(dil, 11),
        aliases={9 + t: t for t in range(np_)}, rider=rider,
    )(uq, uk, uk, uv, uv, dys, lse, delta, bias, *prev_grads)


def _place():
    x, y, c = lax.axis_index("x"), lax.axis_index("y"), lax.axis_index("c")
    chips = [(1 - x, y), (x, 1 - y), (1 - x, 1 - y)]
    return x, y, c, chips


def _slab(ref, axis, chip, width):
    start = pl.multiple_of(chip * width, width)
    if axis == 0:
        return ref.at[pl.ds(start, width), :]
    return ref.at[:, pl.ds(start, width)]


def _run_rider(rider, name):
    nin, nout = len(rider.ins), len(rider.out_shapes)

    def body(*refs):
        ins, outs, scr = refs[:nin], refs[nin:nin + nout], refs[nin + nout:]
        rider.start(ins, outs, scr)
        rider.finish(ins, outs, scr)

    return _pcall(body, name=name, in_specs=[_ANY] * nin, out_specs=[_ANY] * nout, out_shape=rider.out_shapes,
                  scratch=rider.scratch)(*rider.ins)


def _gather_halves_rider(shard, axis):
    shape = list(shard.shape)
    shape[axis] *= 4
    full = jax.ShapeDtypeStruct(tuple(shape), shard.dtype)
    half = shard.shape[0] // 2
    width = shard.shape[axis]

    def region(out, chip, core):
        if axis == 0:
            return out.at[pl.ds(pl.multiple_of(chip * width + core * half, half), half), :]
        return out.at[pl.ds(pl.multiple_of(core * half, half), half), pl.ds(pl.multiple_of(chip * width, width), width)]

    def copies(ins, outs, scr):
        send, recv, loc = scr
        (src,), (out,) = ins, outs
        x, y, c, chips = _place()
        mine = 2 * x + y
        own = pltpu.make_async_copy(src, _slab(out, axis, mine, width), loc.at[0])
        my_half = src.at[pl.ds(pl.multiple_of(c * half, half), half), :]
        over_ici, ici_in, to_sib, sib_in = [], [], [], []
        for j, (px, py) in enumerate(chips):
            theirs = 2 * px + py
            ici = dict(send_sem=send.at[j], recv_sem=recv.at[j], device_id=(px, py, c), device_id_type=MESH)
            d2d = dict(send_sem=send.at[3 + j], recv_sem=recv.at[3 + j], device_id=(x, y, 1 - c),
                       device_id_type=MESH)
            over_ici.append(pltpu.make_async_remote_copy(src_ref=my_half, dst_ref=region(out, mine, c), **ici))
            ici_in.append(pltpu.make_async_remote_copy(src_ref=my_half, dst_ref=region(out, theirs, c), **ici))
            to_sib.append(pltpu.make_async_remote_copy(
                src_ref=region(out, theirs, c), dst_ref=region(out, theirs, c), **d2d))
            sib_in.append(pltpu.make_async_remote_copy(
                src_ref=region(out, theirs, c), dst_ref=region(out, theirs, 1 - c), **d2d))
        return own, over_ici, ici_in, to_sib, sib_in

    def start(ins, outs, scr):
        own, over_ici, _, _, _ = copies(ins, outs, scr)
        own.start()
        for cp in over_ici:
            cp.start()

    def finish(ins, outs, scr):
        own, over_ici, ici_in, to_sib, sib_in = copies(ins, outs, scr)
        for j in range(3):
            ici_in[j].wait_recv()
            to_sib[j].start()
        for cp in sib_in:
            cp.wait_recv()
        own.wait()
        for cp in over_ici + to_sib:
            cp.wait_send()

    return _Rider([shard], [full], [pltpu.SemaphoreType.DMA((6,)), pltpu.SemaphoreType.DMA((6,)),
                                    pltpu.SemaphoreType.DMA((1,))], start, finish)


def _join_riders(riders):
    if len(riders) == 1:
        return riders[0]

    def parts(ins, outs, scr):
        pi = po = ps = 0
        for rd in riders:
            ni, no, ns = len(rd.ins), len(rd.out_shapes), len(rd.scratch)
            yield rd, ins[pi:pi + ni], outs[po:po + no], scr[ps:ps + ns]
            pi, po, ps = pi + ni, po + no, ps + ns

    def start(ins, outs, scr):
        for rd, i, o, sc in parts(ins, outs, scr):
            rd.start(i, o, sc)

    def finish(ins, outs, scr):
        for rd, i, o, sc in parts(ins, outs, scr):
            rd.finish(i, o, sc)

    return _Rider(sum((rd.ins for rd in riders), []), sum((rd.out_shapes for rd in riders), []),
                  sum((rd.scratch for rd in riders), []), start, finish)


def _scatter_rider(grads, axes, rows=None):
    nw = len(grads)
    outs_shape = []
    for gr, ax in zip(grads, axes):
        shape = list(gr.shape)
        shape[ax] //= 4
        if rows is not None:
            assert ax == 1
            shape[0] = rows[1] - rows[0]
        outs_shape.append(jax.ShapeDtypeStruct((4,) + tuple(shape), gr.dtype))

    def copies(ins, outs, scr):
        send, recv, loc = scr
        x, y, c, chips = _place()
        cps, own = [], []
        for t in range(nw):
            width = ins[t].shape[axes[t]] // 4
            src = ins[t] if rows is None else ins[t].at[pl.ds(rows[0], rows[1] - rows[0]), :]
            own.append(pltpu.make_async_copy(_slab(src, axes[t], 2 * x + y, width), outs[t].at[3], loc.at[t]))
            for j, (px, py) in enumerate(chips):
                cps.append(pltpu.make_async_remote_copy(
                    src_ref=_slab(src, axes[t], 2 * px + py, width), dst_ref=outs[t].at[j],
                    send_sem=send.at[3 * t + j], recv_sem=recv.at[3 * t + j],
                    device_id=(px, py, c), device_id_type=MESH))
        return cps, own

    def start(ins, outs, scr):
        cps, own = copies(ins, outs, scr)
        for cp in cps + own:
            cp.start()

    def finish(ins, outs, scr):
        cps, own = copies(ins, outs, scr)
        for cp in cps:
            cp.wait_recv()
        for cp in own:
            cp.wait()
        for cp in cps:
            cp.wait_send()

    return _Rider(grads, outs_shape, [pltpu.SemaphoreType.DMA((3 * nw,)), pltpu.SemaphoreType.DMA((3 * nw,)),
                                      pltpu.SemaphoreType.DMA((nw,))], start, finish)


def _swap_rider(parts):
    nw = len(parts)

    def copies(ins, outs, scr):
        send, recv = scr
        x, y, c, _ = _place()
        return [pltpu.make_async_remote_copy(
            src_ref=ins[t], dst_ref=outs[t], send_sem=send.at[t], recv_sem=recv.at[t],
            device_id=(x, y, 1 - c), device_id_type=MESH) for t in range(nw)]

    def start(ins, outs, scr):
        for cp in copies(ins, outs, scr):
            cp.start()

    def finish(ins, outs, scr):
        cps = copies(ins, outs, scr)
        for cp in cps:
            cp.wait_recv()
        for cp in cps:
            cp.wait_send()

    return _Rider(parts, [jax.ShapeDtypeStruct(p.shape, p.dtype) for p in parts],
                  [pltpu.SemaphoreType.DMA((nw,)), pltpu.SemaphoreType.DMA((nw,))], start, finish)


def _sum_all_devices(buf, name):
    rows, cols = buf.shape

    def body(in_ref, o_ref, gat_ref, send, recv):
        x, y, c, _ = _place()
        me = 4 * x + 2 * y + c
        gat_ref[me] = in_ref[...]
        started = []
        for mask in range(1, 8):
            fx, fy, fc = (mask >> 2) & 1, (mask >> 1) & 1, mask & 1
            peer = (x + fx * (1 - 2 * x), y + fy * (1 - 2 * y), c + fc * (1 - 2 * c))
            cp = pltpu.make_async_remote_copy(
                src_ref=in_ref, dst_ref=gat_ref.at[me], send_sem=send.at[mask - 1], recv_sem=recv.at[mask - 1],
                device_id=peer, device_id_type=MESH)
            cp.start()
            started.append(cp)
        for cp in started:
            cp.wait_recv()
        for cp in started:
            cp.wait_send()
        acc = gat_ref[0]
        for t in range(1, 8):
            acc = acc + gat_ref[t]
        o_ref[...] = acc

    vm = pl.BlockSpec(memory_space=pltpu.VMEM)
    return _pcall(
        body, name=name, in_specs=[vm], out_specs=vm, out_shape=jax.ShapeDtypeStruct((rows, cols), F32),
        scratch=[pltpu.VMEM((8, rows, cols), F32), pltpu.SemaphoreType.DMA((7,)), pltpu.SemaphoreType.DMA((7,))],
    )(buf)


_UPD_T = 256


def _sum_partials(got, name):
    _, rows, cols = got.shape
    tr = min(_UPD_T, rows)

    def body(got_ref, o_ref):
        acc = got_ref[3].astype(F32)
        for j in range(3):
            acc = acc + got_ref[j].astype(F32)
        o_ref[...] = acc

    return _pcall(
        body, name=name, grid=(rows // tr,),
        in_specs=[pl.BlockSpec((4, tr, cols), lambda i: (0, i, 0))], out_specs=pl.BlockSpec((tr, cols), lambda i: (i, 0)),
        out_shape=jax.ShapeDtypeStruct((rows, cols), F32))(got)


def _adamw_math(w, gr, m, v):
    m = _B1 * m + (1.0 - _B1) * gr
    v = _B2 * v + (1.0 - _B2) * (gr * gr)
    m_hat = m / (1.0 - _B1 ** _STEP)
    v_hat = v / (1.0 - _B2 ** _STEP)
    delta = -_LR * (m_hat / (jnp.sqrt(v_hat) + _EPS) + _WD * w)
    return delta, m, v


def _adamw(w, m, v, parts, name):
    rows, cols = w.shape
    tr = min(_UPD_T, rows)
    npart = len(parts)

    def body(w_ref, m_ref, v_ref, *rest):
        p_refs, (g_ref, d_ref, nm_ref, nv_ref) = rest[:npart], rest[npart:]
        gr = p_refs[0][...]
        for p in p_refs[1:]:
            gr = gr + p[...]
        delta, nm, nv = _adamw_math(w_ref[...], gr, m_ref[...], v_ref[...])
        g_ref[...] = gr
        d_ref[...] = delta
        nm_ref[...] = nm
        nv_ref[...] = nv

    blk = pl.BlockSpec((tr, cols), lambda i: (i, 0))
    sh = jax.ShapeDtypeStruct((rows, cols), F32)
    return _pcall(body, name=name, grid=(rows // tr,), in_specs=[blk] * (3 + npart), out_specs=[blk] * 4,
                  out_shape=[sh] * 4)(w, m, v, *parts)


def _adamw_layers(w, m, v, parts, name):
    _, rows, cols = w.shape
    tr = min(_UPD_T, rows)
    npart = len(parts[0])

    def body(w_ref, m_ref, v_ref, *rest):
        p_refs, (g_ref, d_ref, nm_ref, nv_ref) = rest[:2 * npart], rest[2 * npart:]
        grs = []
        for layer in range(2):
            gr = p_refs[layer * npart][...]
            for p in p_refs[layer * npart + 1:(layer + 1) * npart]:
                gr = gr + p[...]
            grs.append(gr)
        gr = jnp.where(pl.program_id(0) == 0, grs[0], grs[1])
        delta, nm, nv = _adamw_math(w_ref[...], gr, m_ref[...], v_ref[...])
        g_ref[...] = gr
        d_ref[...] = delta
        nm_ref[...] = nm
        nv_ref[...] = nv

    blk = pl.BlockSpec((None, tr, cols), lambda l, i: (l, i, 0))

    def part_spec(layer):
        return pl.BlockSpec((tr, cols), lambda l, i: (jnp.where(l == layer, i, 0), 0))

    sh = jax.ShapeDtypeStruct(w.shape, F32)
    return _pcall(
        body, name=name, grid=(2, rows // tr),
        in_specs=[blk] * 3 + [part_spec(0)] * npart + [part_spec(1)] * npart, out_specs=[blk] * 4,
        out_shape=[sh] * 4)(w, m, v, *parts[0], *parts[1])


_PACK_W = 1024


def _pack(arrs, rows):
    flat = []
    for a in arrs:
        f = a.reshape(-1).astype(F32)
        pad = (-f.shape[0]) % _PACK_W
        flat.append(jnp.pad(f, (0, pad)))
    f = jnp.concatenate(flat)
    f = jnp.pad(f, (0, rows * _PACK_W - f.shape[0]))
    return f.reshape(rows, _PACK_W)


def _unpack(buf, shapes):
    flat = buf.reshape(-1)
    out, pos = [], 0
    for sh in shapes:
        size = math.prod(sh)
        out.append(flat[pos:pos + size].reshape(sh))
        pos += size + ((-size) % _PACK_W)
    return out


def _pack_rows(shapes):
    total = sum(-(-math.prod(sh) // _PACK_W) for sh in shapes)
    return -(-total // 8) * 8


def kernel(x, rel_bias, ab_norm, ab_w_in, ab_conv_w, ab_conv_b, ab_ln_g, ab_ln_b, ab_w_out, sc_norm, sc_w_in, sc_conv_w, sc_w_out, mlp_norm, mlp_w_up, mlp_w_down, final_norm, loss_target, m_rel_bias, m_ab_norm, m_ab_w_in, m_ab_conv_w, m_ab_conv_b, m_ab_ln_g, m_ab_ln_b, m_ab_w_out, m_sc_norm, m_sc_w_in, m_sc_conv_w, m_sc_w_out, m_mlp_norm, m_mlp_w_up, m_mlp_w_down, m_final_norm, v_rel_bias, v_ab_norm, v_ab_w_in, v_ab_conv_w, v_ab_conv_b, v_ab_ln_g, v_ab_ln_b, v_ab_w_out, v_sc_norm, v_sc_w_in, v_sc_conv_w, v_sc_w_out, v_mlp_norm, v_mlp_w_up, v_mlp_w_down, v_final_norm):
    s, d = x.shape[1], x.shape[2]
    dff = 4 * d
    c = _GROUP_COLS
    chip = 2 * lax.axis_index("x") + lax.axis_index("y")
    on_c0 = (lax.axis_index("c") == 0).astype(F32)
    h0 = x[0]
    tgt = loss_target[0]

    cw_sh, scn_sh, scw_sh = ab_conv_w[0], sc_norm, sc_conv_w[0]
    conv_w_full = lax.dynamic_update_slice(jnp.zeros((_CONV_K, c), F32), cw_sh * on_c0, (0, chip * cw_sh.shape[1]))
    scn_full = lax.dynamic_update_slice(jnp.zeros((1, d), F32), scn_sh * on_c0, (0, chip * scn_sh.shape[1]))
    scw_full = lax.dynamic_update_slice(jnp.zeros((3, d), F32), scw_sh * on_c0, (0, chip * scw_sh.shape[1]))
    small_shapes = [(_CONV_K, c), (1, d), (3, d)]
    small = _sum_all_devices(_pack([conv_w_full, scn_full, scw_full], _pack_rows(small_shapes)), "gather_small")
    conv_w, sc_g, sc_cw = _unpack(small, small_shapes)

    w_shards = [ab_w_in[0], ab_w_out[0], sc_w_in[0], sc_w_out[0], mlp_w_up[0], mlp_w_up[1],
                mlp_w_down[0], mlp_w_down[1]]
    w_axes = [1, 0, 1, 0, 1, 1, 0, 0]
    wb = [w.astype(BF16) for w in w_shards]
    full_w = [None] * 8

    def gather(idx):
        return _join_riders([_gather_halves_rider(wb[t], w_axes[t]) for t in idx])

    def put(idx, got_w):
        for t, w in zip(idx, got_w):
            full_w[t] = w

    buckets = _bucket_maps()
    bias = _bias_expand(rel_bias, buckets)
    n0, got_w = _rms_fwd(h0, ab_norm, "rms_fwd_ab", rider=_gather_halves_rider(wb[0], w_axes[0]))
    put([0], got_w)
    w_in = full_w[0]
    tm = min(1024, s)
    tm2 = min(2048, s)
    tmh = min(512, s)
    uc = _mm(n0, w_in, "nn", m=s, n=2 * c, k=d, tm=tm2, tn=2 * c, tk=d, out_dtype=BF16, name="proj_conv")
    uq, uk, uv = [], [], []
    for t, (nm, dst) in enumerate(zip("qkv", (uq, uk, uv))):
        res = _mm(n0, w_in, "nn", m=s, n=3 * c, k=d, tm=tm2, tn=c, tk=d, out_dtype=F32, name=f"proj_{nm}",
                  b_off=(0, 2 + 3 * t), split="o", rider=gather([1]) if t == 0 else None)
        if t == 0:
            res, got_w = res
            put([1], got_w)
        dst.append(res)
    uq, uk, uv = uq[0], uk[0], uv[0]
    (cat, ca), got_w = _conv_a_fwd(uc, conv_w, ab_conv_b, ab_ln_g, ab_ln_b, rider=gather([2]))
    put([2], got_w)
    outs, lses = [], []
    for g, (_, dil) in enumerate(_GROUPS):
        idx = ([4], [6], [3, 5])[g]
        (o, l), got_w = _attn_fwd(uq, uk, uv, bias, g, dil, 4 if dil <= 4 else 2, rider=gather(idx))
        put(idx, got_w)
        outs.append(o)
        lses.append(l)
    cat, lse = _attn_merge(outs, lses, cat)
    h1, n1 = _mm(cat, full_w[1], "nn", m=s, n=d, k=d, tm=tm, tn=d, tk=d, out_dtype=(F32, BF16), name="out_ab",
                 epi=_epi_add_rms, extras=(h0,), vecs=(mlp_norm[0:1],))

    def mlp_fwd(h, nrm, layer, next_gain=None, rider=None):
        zr = _mm(nrm, full_w[4 + layer], "nn", m=s, n=dff, k=d, tm=tmh, tn=dff, tk=d, out_dtype=BF16,
                 name=f"mlp_up{layer}", epi=_epi_relu, rider=rider)
        if rider is not None:
            zr, got_r = zr
            put([7], got_r)
        kw = dict(m=s, n=d, k=dff, tm=tmh, tn=d, tk=dff, name=f"mlp_down{layer}", a_pro=_square, extras=(h,))
        if next_gain is None:
            return zr, _mm(zr, full_w[6 + layer], "nn", out_dtype=F32, epi=_epi_add, **kw), None
        hn, nn = _mm(zr, full_w[6 + layer], "nn", out_dtype=(F32, BF16), epi=_epi_add_rms, vecs=(next_gain,), **kw)
        return zr, hn, nn

    zr0, h2, n2 = mlp_fwd(h1, n1, 0, next_gain=sc_g, rider=gather([7]))
    _, w_out, w_si, w_so, w_up0, w_up1, w_dn0, w_dn1 = full_w
    w_up, w_dn = [w_up0, w_up1], [w_dn0, w_dn1]
    u2 = _mm(n2, w_si, "nn", m=s, n=3 * d, k=d, tm=tmh, tn=3 * d, tk=d, out_dtype=BF16, name="proj_sc")
    scv = _short_conv_fwd(u2, sc_cw)
    h3, n3 = _mm(scv, w_so, "nn", m=s, n=d, k=d, tm=tm, tn=d, tk=d, out_dtype=(F32, BF16), name="out_sc",
                 epi=_epi_add_rms, extras=(h2,), vecs=(mlp_norm[1:2],))
    zr1, h4, _ = mlp_fwd(h3, n3, 1)

    dh4, dh4b, g_final, loss_part = _loss_head(h4, tgt, final_norm.reshape(1, d))
    tkw = min(4096, s)

    big_grads, sums = [None] * 8, [None] * 8

    def scatter(t):
        return _scatter_rider([big_grads[t]], [w_axes[t]])

    def arrived(t, got_t):
        sums[t] = _sum_partials(got_t[0], f"sum_partials{t}")

    def mlp_bwd(dh, dhb, h, nrm, zr, layer):
        dz = _mm(dhb, w_dn[layer], "nt", m=s, n=dff, k=d, tm=tmh, tn=dff, tk=d, out_dtype=BF16,
                 name=f"mlp_down{layer}_dx", epi=_epi_relu_sq_bwd, extras=(zr,))
        big_grads[6 + layer] = _mm(zr, dhb, "tn", m=dff, n=d, k=s, tm=1024, tn=d, tk=tkw, out_dtype=BF16,
                                   name=f"mlp_down{layer}_dw", a_pro=_square)
        big_grads[4 + layer] = _mm(nrm, dz, "tn", m=d, n=dff, k=s, tm=d, tn=1024, tk=tkw, out_dtype=BF16,
                                   name=f"mlp_up{layer}_dw")
        res = _mm(dz, w_up[layer], "nt", m=s, n=d, k=dff, tm=tmh, tn=d, tk=dff, out_dtype=(F32, BF16),
                  name=f"mlp_up{layer}_dx", rider=scatter(6) if layer == 0 else None, epi=_epi_rms_bwd,
                  extras=(h, dh), vecs=(mlp_norm[layer:layer + 1],), row_sum=True)
        if layer == 0:
            res, got_t = res
            arrived(6, got_t)
        return res

    dh3, dh3b, g_mn1 = mlp_bwd(dh4, dh4b, h3, n3, zr1, 1)

    dsc = _mm(dh3b, w_so, "nt", m=s, n=d, k=d, tm=tm, tn=d, tk=d, out_dtype=F32, name="out_sc_dx")
    big_grads[3] = _mm(scv, dh3b, "tn", m=d, n=d, k=s, tm=d, tn=d, tk=tkw, out_dtype=BF16, name="out_sc_dw")
    du2, g_sccw8 = _short_conv_bwd(u2, dsc, sc_cw)
    big_grads[2], got_t = _mm(n2, du2, "tn", m=d, n=3 * d, k=s, tm=d, tn=1024, tk=tkw, out_dtype=BF16,
                              name="proj_sc_dw", rider=scatter(3))
    arrived(3, got_t)
    dh2, dh2b, g_scn = _mm(
        du2, w_si, "nt", m=s, n=d, k=3 * d, tm=tmh, tn=d, tk=3 * d, out_dtype=(F32, BF16), name="proj_sc_dx",
        epi=_epi_rms_bwd, extras=(h2, dh3), vecs=(sc_g,), row_sum=True)

    dh1, dh1b, g_mn0 = mlp_bwd(dh2, dh2b, h1, n1, zr0, 0)

    dcat = _mm(dh1b, w_out, "nt", m=s, n=d, k=d, tm=tm, tn=d, tk=d, out_dtype=F32, name="out_ab_dx")
    big_grads[1] = _mm(cat, dh1b, "tn", m=d, n=d, k=s, tm=d, tn=d, tk=tkw, out_dtype=BF16, name="out_ab_dw")
    (dca, conv_stats), got_t = _conv_a_bwd_ln(ca, dcat, ab_ln_g, ab_ln_b, rider=scatter(1))
    arrived(1, got_t)
    (duc, g_cw32), got_t = _conv_a_bwd_conv(uc, dca, conv_w, rider=scatter(4))
    arrived(4, got_t)
    delta, dys = _attn_delta(dcat, cat)

    dqkv, ds_list = [], []
    for g, (_, dil) in enumerate(_GROUPS):
        late = (2, 5, 7)[g]
        (grads, dsa), got_t = _attn_bwd(uq, uk, uv, dys, lse, delta, bias, dqkv, g, dil, 4 if dil <= 4 else 1,
                                        rider=scatter(late))
        arrived(late, got_t)
        dqkv = [grads]
        ds_list.append(dsa)
    g_bias = _bias_reduce(jnp.concatenate(ds_list, axis=0), buckets)[:, :, 0].T
    dqkv = dqkv[0].reshape(9 * _PAIRS, s, _LANES)

    g_in_conv = _mm(n0, duc, "tn", m=d, n=2 * c, k=s, tm=d, tn=2 * c, tk=tkw, out_dtype=BF16,
                    name="proj_ab_dw_conv")
    g_in_qkv, sib_late = _mm(n0, dqkv, "tn", m=d, n=9 * c, k=s, tm=d, tn=3 * c, tk=min(1024, s), out_dtype=BF16,
                             name="proj_ab_dw_qkv", rider=_swap_rider(sums[1:]), split="b")
    big_grads[0] = jnp.concatenate([g_in_conv, g_in_qkv], axis=1)
    cut = d // 4
    dn0, (got_top,) = _mm(duc, w_in, "nt", m=s, n=d, k=2 * c, tm=tm, tn=d, tk=2 * c, out_dtype=F32,
                          name="proj_ab_dx_conv", rider=_scatter_rider([big_grads[0]], [w_axes[0]], rows=(0, cut)))
    dn0, (got_bot,) = _mm(dqkv, w_in[:, 2 * c:], "nt", m=s, n=d, k=9 * c, tm=tm, tn=d, tk=3 * c, out_dtype=F32,
                          name="proj_ab_dx_qkv", epi=_epi_add, extras=(dn0,), split="a",
                          rider=_scatter_rider([big_grads[0]], [w_axes[0]], rows=(cut, d)))
    grad_x, g_abn = _rms_bwd(dn0, h0, ab_norm, dh1, "rms_bwd_ab")
    sums[0] = jnp.concatenate([_sum_partials(got_top, "sum_partials0_top"),
                               _sum_partials(got_bot, "sum_partials0_bottom")], axis=0)
    sib = list(_run_rider(_swap_rider([sums[0]]), "swap_sibling_w_in")) + sib_late

    upd = [_adamw(w_shards[t], mm[0], vv[0], [sums[t], sib[t]], f"adamw{t}")
           for t, (mm, vv) in enumerate(((m_ab_w_in, v_ab_w_in), (m_ab_w_out, v_ab_w_out),
                                         (m_sc_w_in, v_sc_w_in), (m_sc_w_out, v_sc_w_out)))]
    upd_up = _adamw_layers(mlp_w_up, m_mlp_w_up, v_mlp_w_up, [[sums[4], sib[4]], [sums[5], sib[5]]], "adamw_up")
    upd_dn = _adamw_layers(mlp_w_down, m_mlp_w_down, v_mlp_w_down, [[sums[6], sib[6]], [sums[7], sib[7]]],
                           "adamw_down")

    full_shapes = [(_NUM_BUCKETS, rel_bias.shape[1]), (1, d), (_CONV_K, c), (1, c), (1, c), (1, c), (1, d),
                   (3, d), (2, d), (d,), (1, 1)]
    small_grads = [g_bias, g_abn, g_cw32[:_CONV_K], conv_stats[0:1], conv_stats[1:2], conv_stats[2:3], g_scn,
                   g_sccw8[:3], jnp.concatenate([g_mn0, g_mn1], axis=0), g_final.reshape(d), loss_part[0:1, 0:1]]
    tot = _unpack(_sum_all_devices(_pack(small_grads, _pack_rows(full_shapes)), "sum_small"), full_shapes)
    loss = tot.pop()[0, 0]
    for idx, sh in ((2, cw_sh), (6, scn_sh), (7, scw_sh)):
        width = sh.shape[1]
        tot[idx] = lax.dynamic_slice_in_dim(tot[idx], chip * width, width, axis=1)
    sm_w = [rel_bias, ab_norm, cw_sh, ab_conv_b, ab_ln_g, ab_ln_b, scn_sh, scw_sh, mlp_norm, final_norm]
    sm_m = [m_rel_bias, m_ab_norm, m_ab_conv_w[0], m_ab_conv_b, m_ab_ln_g, m_ab_ln_b, m_sc_norm, m_sc_conv_w[0],
            m_mlp_norm, m_final_norm]
    sm_v = [v_rel_bias, v_ab_norm, v_ab_conv_w[0], v_ab_conv_b, v_ab_ln_g, v_ab_ln_b, v_sc_norm, v_sc_conv_w[0],
            v_mlp_norm, v_final_norm]
    sh_shapes = [tuple(t.shape) for t in tot]
    rows = _pack_rows(sh_shapes)
    sm_upd = _adamw(_pack(sm_w, rows), _pack(sm_m, rows), _pack(sm_v, rows), [_pack(tot, rows)], "adamw_small")
    sm_g, sm_d, sm_nm, sm_nv = [_unpack(buf, sh_shapes) for buf in sm_upd]

    def assemble(kind, sm):
        big = [u[kind] for u in upd]
        return [sm[0], sm[1], big[0][None], sm[2][None], sm[3], sm[4], sm[5], big[1][None], sm[6], big[2][None],
                sm[7][None], big[3][None], sm[8], upd_up[kind], upd_dn[kind], sm[9]]

    res = [loss, grad_x[None]]
    for kind, sm in enumerate((sm_g, sm_d, sm_nm, sm_nv)):
        res += assemble(kind, sm)
    return tuple(res)
```

```python
import functools
import math

import jax
import jax.numpy as jnp
from jax import lax
from jax.experimental import pallas as pl
from jax.experimental.pallas import tpu as pltpu

F32 = jnp.float32
BF16 = jnp.bfloat16
MESH = pl.DeviceIdType.MESH

_GROUPS = ((128, 1), (512, 4), (2048, 16))
_STEPS = 128
_HEAD_DIM = 64
_GROUP_COLS = 512
_NUM_BUCKETS = 32
_MAX_DISTANCE = 2048
_CONV_K = 31
_HALO = 32
_SC_HALO = 16
_RMS_EPS = 1e-6
_LN_EPS = 1e-5
_NEG = -1e30
_LANES = 128
_VMEM_LIMIT = 56 * 1024 * 1024

_LR, _B1, _B2, _EPS, _WD, _STEP = 0.001, 0.9, 0.999, 1e-08, 0.01, 10


class _Rider:
    def __init__(self, ins, out_shapes, scratch, start, finish):
        self.ins, self.out_shapes, self.scratch = list(ins), list(out_shapes), list(scratch)
        self.start, self.finish = start, finish


def _pcall(body, *, name, out_shape, in_specs, out_specs, grid=None, scratch=(), aliases=None, rider=None):
    kw = {} if grid is None else {"grid": grid}
    cparams = pltpu.CompilerParams(vmem_limit_bytes=_VMEM_LIMIT)
    if rider is None:
        return pl.pallas_call(
            body, name=name, out_shape=out_shape, in_specs=in_specs, out_specs=out_specs,
            scratch_shapes=list(scratch), input_output_aliases=aliases or {},
            compiler_params=cparams, **kw)
    single = not isinstance(out_specs, (list, tuple))
    ospecs = [out_specs] if single else list(out_specs)
    oshapes = [out_shape] if single else list(out_shape)
    nin, nout, nscr = len(in_specs), len(ospecs), len(scratch)
    rin, rout = len(rider.ins), len(rider.out_shapes)

    def wrapped(*refs):
        h_in, r_in = refs[:nin], refs[nin:nin + rin]
        p = nin + rin
        h_out, r_out = refs[p:p + nout], refs[p + nout:p + nout + rout]
        p += nout + rout
        h_scr, r_scr = refs[p:p + nscr], refs[p + nscr:]
        ids = [pl.program_id(a) for a in range(len(grid))]
        first = functools.reduce(jnp.logical_and, [i == 0 for i in ids])
        last = functools.reduce(jnp.logical_and, [i == g - 1 for i, g in zip(ids, grid)])

        @pl.when(first)
        def _():
            rider.start(r_in, r_out, r_scr)

        body(*h_in, *h_out, *h_scr)

        @pl.when(last)
        def _():
            rider.finish(r_in, r_out, r_scr)

    call = pl.pallas_call(
        wrapped, name=name, out_shape=oshapes + rider.out_shapes,
        in_specs=list(in_specs) + [_ANY] * rin, out_specs=ospecs + [_ANY] * rout,
        scratch_shapes=list(scratch) + rider.scratch, input_output_aliases=aliases or {},
        compiler_params=cparams, **kw)

    def run(*operands):
        res = call(*operands, *rider.ins)
        host = res[0] if single else list(res[:nout])
        return host, list(res[nout:])

    return run


def _sig(x):
    return 1.0 / (1.0 + jnp.exp(-x))


_ANY = pl.BlockSpec(memory_space=pl.ANY)


def _lanes_of(ref):
    parts = [ref[p] for p in range(ref.shape[0])]
    return parts[0] if len(parts) == 1 else jnp.concatenate(parts, axis=1)


def _mm(a, b, mode, *, m, n, k, tm, tn, tk, out_dtype, name, epi=None, extras=(), b_off=(0, 0), rider=None,
        split="", vecs=(), a_pro=None, row_sum=False):
    nk = k // tk
    assert m % tm == 0 and n % tn == 0 and k % tk == 0
    o0, o1 = b_off
    if mode == "nn":
        a_spec = pl.BlockSpec((tm, tk), lambda i, j, kk: (i, kk))
        b_spec = pl.BlockSpec((tk, tn), lambda i, j, kk: (kk + o0, j + o1))
        dn = (((1,), (0,)), ((), ()))
    elif mode == "nt":
        a_spec = pl.BlockSpec((tm, tk), lambda i, j, kk: (i, kk))
        if "a" in split:
            a_spec = pl.BlockSpec((tk // _LANES, tm, _LANES), lambda i, j, kk: (kk, i, 0))
        b_spec = pl.BlockSpec((tn, tk), lambda i, j, kk: (j + o0, kk + o1))
        dn = (((1,), (1,)), ((), ()))
    else:
        a_spec = pl.BlockSpec((tk, tm), lambda i, j, kk: (kk, i))
        b_spec = pl.BlockSpec((tk, tn), lambda i, j, kk: (kk + o0, j + o1))
        if "b" in split:
            b_spec = pl.BlockSpec((tn // _LANES, tk, _LANES), lambda i, j, kk: (j, kk, 0))
        dn = (((0,), (0,)), ((), ()))
    o_spec = pl.BlockSpec((tm, tn), lambda i, j, kk: (i, j))
    e_spec = o_spec
    if "o" in split:
        o_spec = pl.BlockSpec((tn // _LANES, tm, _LANES), lambda i, j, kk: (j, i, 0))
    v_spec = pl.BlockSpec((1, tn), lambda i, j, kk: (0, j))
    ne = len(extras) + len(vecs)
    multi = isinstance(out_dtype, tuple)
    dts = out_dtype if multi else (out_dtype,)
    no = len(dts)
    nr = 1 if row_sum else 0
    assert not row_sum or tn == n

    def body(a_ref, b_ref, *rest):
        ex, o_refs = rest[:ne], rest[ne:ne + no]
        av = _lanes_of(a_ref) if "a" in split else a_ref[...]
        bv = _lanes_of(b_ref) if "b" in split else b_ref[...]
        if av.dtype != BF16:
            av = av.astype(BF16)
        if bv.dtype != BF16:
            bv = bv.astype(BF16)
        if a_pro is not None:
            av = a_pro(av)
        p = lax.dot_general(av, bv, dn, preferred_element_type=F32)

        def fin(x):
            if epi is not None:
                x = epi(x, *[e[...] for e in ex])
            if row_sum:
                row, x = x[-1], (x[:-1] if multi else x[0])
                row_ref = rest[ne + no]

                @pl.when(pl.program_id(0) == 0)
                def _():
                    row_ref[...] = row

                @pl.when(pl.program_id(0) > 0)
                def _():
                    row_ref[...] += row

            for o_ref, val, dt in zip(o_refs, x if multi else (x,), dts):
                if "o" in split:
                    for p in range(tn // _LANES):
                        o_ref[p] = val[:, p * _LANES:(p + 1) * _LANES].astype(dt)
                else:
                    o_ref[...] = val.astype(dt)

        if nk == 1:
            fin(p)
        else:
            acc = rest[ne + no + nr]
            kk = pl.program_id(2)

            @pl.when(kk == 0)
            def _():
                acc[...] = p

            @pl.when(kk > 0)
            def _():
                acc[...] += p

            @pl.when(kk == nk - 1)
            def _():
                fin(acc[...])

    oshape = (n // _LANES, m, _LANES) if "o" in split else (m, n)
    shapes = [jax.ShapeDtypeStruct(oshape, dt) for dt in dts]
    ospecs = [o_spec] * no
    if row_sum:
        shapes.append(jax.ShapeDtypeStruct((1, n), F32))
        ospecs.append(v_spec)
    lone = not multi and not row_sum
    return _pcall(
        body, name=name, grid=(m // tm, n // tn, nk),
        in_specs=[a_spec, b_spec] + [e_spec] * len(extras) + [v_spec] * len(vecs),
        out_specs=ospecs[0] if lone else ospecs, out_shape=shapes[0] if lone else shapes,
        scratch=[pltpu.VMEM((tm, tn), F32)] if nk > 1 else [], rider=rider,
    )(a, b, *extras, *vecs)


def _epi_add(x, r):
    return x + r


def _epi_relu(x):
    return jnp.maximum(x, 0.0)


def _square(x):
    return x * x


def _epi_relu_sq_bwd(da, zr):
    return da * (2.0 * zr.astype(F32))


def _epi_add_rms(x, r, g):
    h = x + r
    return h, h * lax.rsqrt(jnp.mean(h * h, axis=-1, keepdims=True) + _RMS_EPS) * g


def _epi_rms_bwd(dn, h, dh_in, g):
    dx, dg = _rms_bwd_math(dn, h, g)
    dh = dh_in + dx
    return dh, dh, dg


_ROW_T = 512


def _rms_fwd(h, g, name, rider=None):
    s, d = h.shape

    def body(h_ref, g_ref, o_ref):
        x = h_ref[...]
        r = lax.rsqrt(jnp.mean(x * x, axis=-1, keepdims=True) + _RMS_EPS)
        o_ref[...] = (x * r * g_ref[...]).astype(BF16)

    row = pl.BlockSpec((_ROW_T, d), lambda i: (i, 0))
    vec = pl.BlockSpec((1, d), lambda i: (0, 0))
    return _pcall(body, name=name, grid=(s // _ROW_T,), in_specs=[row, vec], out_specs=row,
                  out_shape=jax.ShapeDtypeStruct((s, d), BF16), rider=rider)(h, g)


def _rms_bwd_math(dn, x, g):
    r = lax.rsqrt(jnp.mean(x * x, axis=-1, keepdims=True) + _RMS_EPS)
    xhat = x * r
    dg = jnp.sum(dn * xhat, axis=0, keepdims=True)
    t = dn * g
    dx = r * (t - xhat * jnp.mean(t * xhat, axis=-1, keepdims=True))
    return dx, dg


def _rms_bwd(dn, h, g, dh_in, name):
    s, d = h.shape

    def body(dn_ref, h_ref, g_ref, dhi_ref, dh_ref, dg_ref):
        dx, dg = _rms_bwd_math(dn_ref[...], h_ref[...], g_ref[...])
        dh_ref[...] = dhi_ref[...] + dx

        @pl.when(pl.program_id(0) == 0)
        def _():
            dg_ref[...] = jnp.zeros_like(dg_ref)

        dg_ref[...] += dg

    row = pl.BlockSpec((_ROW_T, d), lambda i: (i, 0))
    vec = pl.BlockSpec((1, d), lambda i: (0, 0))
    return _pcall(
        body, name=name, grid=(s // _ROW_T,), in_specs=[row, row, vec, row], out_specs=[row, vec],
        out_shape=[jax.ShapeDtypeStruct((s, d), F32), jax.ShapeDtypeStruct((1, d), F32)])(dn, h, g, dh_in)


def _loss_head(h, tgt, g):
    s, d = h.shape

    def body(h_ref, t_ref, g_ref, dh_ref, dhb_ref, dg_ref, loss_ref):
        x, gv = h_ref[...], g_ref[...]
        r = lax.rsqrt(jnp.mean(x * x, axis=-1, keepdims=True) + _RMS_EPS)
        err = x * r * gv - t_ref[...]
        part = 0.5 * jnp.sum(jnp.mean(err * err, axis=-1, keepdims=True))
        dx, dg = _rms_bwd_math(err * (1.0 / d), x, gv)
        dh_ref[...] = dx
        dhb_ref[...] = dx.astype(BF16)

        @pl.when(pl.program_id(0) == 0)
        def _():
            dg_ref[...] = jnp.zeros_like(dg_ref)
            loss_ref[...] = jnp.zeros_like(loss_ref)

        dg_ref[...] += dg
        loss_ref[...] += jnp.full(loss_ref.shape, part, F32)

    row = pl.BlockSpec((_ROW_T, d), lambda i: (i, 0))
    vec = pl.BlockSpec((1, d), lambda i: (0, 0))
    one = pl.BlockSpec((1, _LANES), lambda i: (0, 0))
    return _pcall(
        body, name="loss_head", grid=(s // _ROW_T,), in_specs=[row, row, vec], out_specs=[row, row, vec, one],
        out_shape=[jax.ShapeDtypeStruct((s, d), F32), jax.ShapeDtypeStruct((s, d), BF16),
                   jax.ShapeDtypeStruct((1, d), F32), jax.ShapeDtypeStruct((1, _LANES), F32)])(h, tgt, g)


_CONV_T = 256
_CONV_RC = 64


def _conv_a_specs(s):
    c = _GROUP_COLS
    hb = _CONV_T // _HALO
    val = pl.BlockSpec((_CONV_T, c), lambda i: (i, 0))
    gate = pl.BlockSpec((_CONV_T, c), lambda i: (i, 1))
    hval = pl.BlockSpec((_HALO, c), lambda i: (jnp.maximum(i * hb - 1, 0), 0))
    hgate = pl.BlockSpec((_HALO, c), lambda i: (jnp.maximum(i * hb - 1, 0), 1))
    return val, gate, hval, hgate


def _fill_glu(val_ref, gate_ref, hval_ref, hgate_ref, hs_ref):
    i = pl.program_id(0)
    hs_ref[pl.ds(_HALO, _CONV_T), :] = val_ref[...].astype(F32) * _sig(gate_ref[...].astype(F32))
    halo = hval_ref[...].astype(F32) * _sig(hgate_ref[...].astype(F32))
    hs_ref[pl.ds(0, _HALO), :] = jnp.where(i > 0, halo, 0.0)


_SHIFT_ROWS = _CONV_T + _HALO - 8


def _fill_shifts(src_ref, sh_ref):
    for b in range(1, 8):
        sh_ref[b - 1] = src_ref[pl.ds(b, _SHIFT_ROWS), :]


def _tap_rows(src_ref, sh_ref, start, rows, lanes=slice(None)):
    b = start % 8
    if b == 0:
        return src_ref[pl.ds(start, rows), lanes]
    return sh_ref[b - 1, pl.ds(start - b, rows), lanes]


def _conv_rows(hs_ref, sh_ref, w_ref, r0, rows):
    off = _HALO - (_CONV_K - 1)
    acc = jnp.zeros((rows, _GROUP_COLS), F32)
    for kk in range(_CONV_K):
        acc = acc + w_ref[kk:kk + 1, :] * _tap_rows(hs_ref, sh_ref, r0 + off + kk, rows)
    return acc


def _ln_fwd(ca, g, b):
    mu = jnp.mean(ca, axis=-1, keepdims=True)
    xc = ca - mu
    rstd = lax.rsqrt(jnp.mean(xc * xc, axis=-1, keepdims=True) + _LN_EPS)
    xhat = xc * rstd
    return xhat, rstd, xhat * g + b


def _conv_a_fwd(uc, w, cb, lg, lb, rider=None):
    s = uc.shape[0]
    c = _GROUP_COLS

    def body(val_ref, gate_ref, hval_ref, hgate_ref, w_ref, cb_ref, lg_ref, lb_ref, o_ref, ca_ref, hs_ref, sh_ref):
        _fill_glu(val_ref, gate_ref, hval_ref, hgate_ref, hs_ref)
        _fill_shifts(hs_ref, sh_ref)
        for rc in range(_CONV_T // _CONV_RC):
            r0 = rc * _CONV_RC
            ca = _conv_rows(hs_ref, sh_ref, w_ref, r0, _CONV_RC) + cb_ref[...]
            ca_ref[pl.ds(r0, _CONV_RC), :] = ca
            _, _, ln = _ln_fwd(ca, lg_ref[...], lb_ref[...])
            o_ref[pl.ds(r0, _CONV_RC), :] = (ln * _sig(ln)).astype(BF16)

    val, gate, hval, hgate = _conv_a_specs(s)
    wspec = pl.BlockSpec((_CONV_K, c), lambda i: (0, 0))
    vec = pl.BlockSpec((1, c), lambda i: (0, 0))
    blk = pl.BlockSpec((_CONV_T, c), lambda i: (i, 0))
    return _pcall(
        body, name="conv_a_fwd", grid=(s // _CONV_T,),
        in_specs=[val, gate, hval, hgate, wspec, vec, vec, vec],
        out_specs=[blk, blk],
        out_shape=[jax.ShapeDtypeStruct((s, 2 * c), BF16), jax.ShapeDtypeStruct((s, c), F32)],
        scratch=[pltpu.VMEM((_CONV_T + _HALO, c), F32), pltpu.VMEM((7, _SHIFT_ROWS, c), F32)],
        rider=rider)(uc, uc, uc, uc, w, cb, lg, lb)


def _conv_a_bwd_ln(ca_all, dcat, lg, lb, rider=None):
    s = ca_all.shape[0]
    c = _GROUP_COLS

    def body(ca_ref, dy_ref, lg_ref, lb_ref, dca_ref, st_ref):
        @pl.when(pl.program_id(0) == 0)
        def _():
            st_ref[...] = jnp.zeros_like(st_ref)

        for rc in range(_CONV_T // _CONV_RC):
            r0 = rc * _CONV_RC
            ca = ca_ref[pl.ds(r0, _CONV_RC), :]
            xhat, rstd, ln = _ln_fwd(ca, lg_ref[...], lb_ref[...])
            sg = _sig(ln)
            dln = dy_ref[pl.ds(r0, _CONV_RC), :] * (sg * (1.0 + ln * (1.0 - sg)))
            dxh = dln * lg_ref[...]
            dca = rstd * (dxh - jnp.mean(dxh, axis=-1, keepdims=True)
                          - xhat * jnp.mean(dxh * xhat, axis=-1, keepdims=True))
            dca_ref[pl.ds(r0, _CONV_RC), :] = dca
            st_ref[0:1, :] += jnp.sum(dca, axis=0, keepdims=True)
            st_ref[1:2, :] += jnp.sum(dln * xhat, axis=0, keepdims=True)
            st_ref[2:3, :] += jnp.sum(dln, axis=0, keepdims=True)

    blk = pl.BlockSpec((_CONV_T, c), lambda i: (i, 0))
    vec = pl.BlockSpec((1, c), lambda i: (0, 0))
    st = pl.BlockSpec((8, c), lambda i: (0, 0))
    return _pcall(
        body, name="conv_a_bwd_ln", grid=(s // _CONV_T,),
        in_specs=[blk, blk, vec, vec], out_specs=[blk, st],
        out_shape=[jax.ShapeDtypeStruct((s, c), F32), jax.ShapeDtypeStruct((8, c), F32)],
        rider=rider)(ca_all, dcat, lg, lb)


def _conv_a_bwd_conv(uc, dca, w, rider=None):
    s = uc.shape[0]
    c = _GROUP_COLS
    nblk = s // _CONV_T
    hb = _CONV_T // _HALO
    off = _HALO - (_CONV_K - 1)

    def body(val_ref, gate_ref, hval_ref, hgate_ref, d_ref, dn_ref, w_ref, du_ref, dw_ref, hs_ref, ds_ref,
             hsh_ref, dsh_ref, dwa_ref):
        i = pl.program_id(0)
        _fill_glu(val_ref, gate_ref, hval_ref, hgate_ref, hs_ref)
        ds_ref[pl.ds(0, _CONV_T), :] = d_ref[...]
        ds_ref[pl.ds(_CONV_T, _HALO), :] = jnp.where(i < nblk - 1, dn_ref[...], 0.0)
        _fill_shifts(hs_ref, hsh_ref)
        _fill_shifts(ds_ref, dsh_ref)

        @pl.when(i == 0)
        def _():
            dwa_ref[...] = jnp.zeros_like(dwa_ref)

        rows = 32
        for r0 in range(0, _CONV_T, rows):
            dcur = ds_ref[pl.ds(r0, rows), :]
            dh = jnp.zeros((rows, c), F32)
            for kk in range(_CONV_K):
                dh = dh + w_ref[kk:kk + 1, :] * _tap_rows(ds_ref, dsh_ref, r0 + _CONV_K - 1 - kk, rows)
                prod = dcur * _tap_rows(hs_ref, hsh_ref, r0 + off + kk, rows)
                dwa_ref[pl.ds(8 * kk, 8), :] += sum(prod[t:t + 8] for t in range(0, rows, 8))
            v = val_ref[pl.ds(r0, rows), :].astype(F32)
            sg = _sig(gate_ref[pl.ds(r0, rows), :].astype(F32))
            du_ref[pl.ds(r0, rows), pl.ds(0, c)] = (dh * sg).astype(BF16)
            du_ref[pl.ds(r0, rows), pl.ds(c, c)] = (dh * v * sg * (1.0 - sg)).astype(BF16)

        @pl.when(i == nblk - 1)
        def _():
            dw_ref[...] = jnp.zeros_like(dw_ref)
            for kk in range(_CONV_K):
                dw_ref[kk:kk + 1, :] = jnp.sum(dwa_ref[pl.ds(8 * kk, 8), :], axis=0, keepdims=True)

    val, gate, hval, hgate = _conv_a_specs(s)
    blk = pl.BlockSpec((_CONV_T, c), lambda i: (i, 0))
    nxt = pl.BlockSpec((_HALO, c), lambda i: (jnp.minimum((i + 1) * hb, s // _HALO - 1), 0))
    wspec = pl.BlockSpec((_CONV_K, c), lambda i: (0, 0))
    return _pcall(
        body, name="conv_a_bwd_conv", grid=(nblk,),
        in_specs=[val, gate, hval, hgate, blk, nxt, wspec],
        out_specs=[pl.BlockSpec((_CONV_T, 2 * c), lambda i: (i, 0)), pl.BlockSpec((_HALO, c), lambda i: (0, 0))],
        out_shape=[jax.ShapeDtypeStruct((s, 2 * c), BF16), jax.ShapeDtypeStruct((_HALO, c), F32)],
        scratch=[pltpu.VMEM((_CONV_T + _HALO, c), F32), pltpu.VMEM((_CONV_T + _HALO, c), F32),
                 pltpu.VMEM((7, _SHIFT_ROWS, c), F32), pltpu.VMEM((7, _SHIFT_ROWS, c), F32),
                 pltpu.VMEM((8 * _HALO, c), F32)],
        rider=rider,
    )(uc, uc, uc, uc, dca, dca, w)


_SC_T = 256
_SC_RC = 32
_SC_LC = 512


def _sc_chunks(d):
    return [(pl.ds(r0, _SC_RC), pl.ds(l0, _SC_LC)) for r0 in range(0, _SC_T, _SC_RC) for l0 in range(0, d, _SC_LC)]


def _short_conv_fwd(u2, w):
    s, d3 = u2.shape
    d = d3 // 3
    hb = _SC_T // _SC_HALO

    def body(b_ref, c_ref, v_ref, hc_ref, hv_ref, w_ref, o_ref, cs_ref):
        i = pl.program_id(0)
        cs_ref[pl.ds(0, _SC_HALO), :] = jnp.where(i > 0, hc_ref[...].astype(F32) * hv_ref[...].astype(F32), 0.0)
        for rows, lanes in _sc_chunks(d):
            cs_ref[pl.ds(_SC_HALO + rows.start, _SC_RC), lanes] = (
                c_ref[rows, lanes].astype(F32) * v_ref[rows, lanes].astype(F32))
        for rows, lanes in _sc_chunks(d):
            taps = [cs_ref[pl.ds(_SC_HALO - 2 + kk + rows.start, _SC_RC), lanes] for kk in range(3)]
            conv = w_ref[0:1, lanes] * taps[0] + w_ref[1:2, lanes] * taps[1] + w_ref[2:3, lanes] * taps[2]
            o_ref[rows, lanes] = (b_ref[rows, lanes].astype(F32) * conv).astype(BF16)

    def col(j):
        return pl.BlockSpec((_SC_T, d), lambda i: (i, j))

    def halo(j):
        return pl.BlockSpec((_SC_HALO, d), lambda i: (jnp.maximum(i * hb - 1, 0), j))

    return _pcall(
        body, name="short_conv_fwd", grid=(s // _SC_T,),
        in_specs=[col(0), col(1), col(2), halo(1), halo(2), pl.BlockSpec((3, d), lambda i: (0, 0))],
        out_specs=pl.BlockSpec((_SC_T, d), lambda i: (i, 0)),
        out_shape=jax.ShapeDtypeStruct((s, d), BF16),
        scratch=[pltpu.VMEM((_SC_T + _SC_HALO, d), F32)])(u2, u2, u2, u2, u2, w)


def _short_conv_bwd(u2, dsc, w, rider=None):
    s, d3 = u2.shape
    d = d3 // 3
    hb = _SC_T // _SC_HALO
    nblk = s // _SC_T

    def body(b_ref, c_ref, v_ref, hc_ref, hv_ref, nb_ref, d_ref, nd_ref, w_ref, du_ref, dw_ref, cs_ref, ds_ref):
        i = pl.program_id(0)
        cs_ref[pl.ds(0, _SC_HALO), :] = jnp.where(i > 0, hc_ref[...].astype(F32) * hv_ref[...].astype(F32), 0.0)
        ds_ref[pl.ds(_SC_T, _SC_HALO), :] = jnp.where(i < nblk - 1, nd_ref[...] * nb_ref[...].astype(F32), 0.0)
        for rows, lanes in _sc_chunks(d):
            cs_ref[pl.ds(_SC_HALO + rows.start, _SC_RC), lanes] = (
                c_ref[rows, lanes].astype(F32) * v_ref[rows, lanes].astype(F32))
            ds_ref[rows, lanes] = d_ref[rows, lanes] * b_ref[rows, lanes].astype(F32)

        @pl.when(i == 0)
        def _():
            dw_ref[...] = jnp.zeros_like(dw_ref)

        for l0 in range(0, d, _SC_LC):
            lanes = pl.ds(l0, _SC_LC)
            dw_acc = [jnp.zeros((8, _SC_LC), F32)] * 3
            for r0 in range(0, _SC_T, _SC_RC):
                rows = pl.ds(r0, _SC_RC)
                taps = [cs_ref[pl.ds(_SC_HALO - 2 + kk + r0, _SC_RC), lanes] for kk in range(3)]
                conv = w_ref[0:1, lanes] * taps[0] + w_ref[1:2, lanes] * taps[1] + w_ref[2:3, lanes] * taps[2]
                dconv = ds_ref[rows, lanes]
                dcv = (w_ref[2:3, lanes] * dconv + w_ref[1:2, lanes] * ds_ref[pl.ds(r0 + 1, _SC_RC), lanes]
                       + w_ref[0:1, lanes] * ds_ref[pl.ds(r0 + 2, _SC_RC), lanes])
                du_ref[rows, lanes] = (d_ref[rows, lanes] * conv).astype(BF16)
                du_ref[rows, pl.ds(d + l0, _SC_LC)] = (dcv * v_ref[rows, lanes].astype(F32)).astype(BF16)
                du_ref[rows, pl.ds(2 * d + l0, _SC_LC)] = (dcv * c_ref[rows, lanes].astype(F32)).astype(BF16)
                for kk in range(3):
                    prod = dconv * taps[kk]
                    dw_acc[kk] = dw_acc[kk] + sum(prod[t:t + 8] for t in range(0, _SC_RC, 8))
            for kk in range(3):
                dw_ref[kk:kk + 1, lanes] += jnp.sum(dw_acc[kk], axis=0, keepdims=True)

    def col(j):
        return pl.BlockSpec((_SC_T, d), lambda i: (i, j))

    def halo(j):
        return pl.BlockSpec((_SC_HALO, d), lambda i: (jnp.maximum(i * hb - 1, 0), j))

    def nxt(j):
        return pl.BlockSpec((_SC_HALO, d), lambda i: (jnp.minimum((i + 1) * hb, s // _SC_HALO - 1), j))

    return _pcall(
        body, name="short_conv_bwd", grid=(nblk,),
        in_specs=[col(0), col(1), col(2), halo(1), halo(2), nxt(0), col(0), nxt(0),
                  pl.BlockSpec((3, d), lambda i: (0, 0))],
        out_specs=[pl.BlockSpec((_SC_T, d3), lambda i: (i, 0)), pl.BlockSpec((8, d), lambda i: (0, 0))],
        out_shape=[jax.ShapeDtypeStruct((s, d3), BF16), jax.ShapeDtypeStruct((8, d), F32)],
        scratch=[pltpu.VMEM((_SC_T + _SC_HALO, d), F32), pltpu.VMEM((_SC_T + _SC_HALO, d), F32)],
        rider=rider,
    )(u2, u2, u2, u2, u2, u2, dsc, dsc, w)


def _bucket_maps():
    a_idx = jnp.arange(_STEPS)[:, None]
    c_idx = jnp.arange(2 * _STEPS)[None, :]
    mdist = jnp.clip(a_idx + _STEPS - c_idx, 0, _STEPS)
    max_exact = _NUM_BUCKETS // 2
    maps = []
    for _, dil in _GROUPS:
        nn = mdist * dil
        nf = jnp.maximum(nn, 1).astype(F32)
        large = max_exact + (jnp.log(nf / max_exact) / math.log(_MAX_DISTANCE / max_exact)
                             * (_NUM_BUCKETS - max_exact)).astype(jnp.int32)
        maps.append(jnp.where(nn < max_exact, nn, jnp.minimum(large, _NUM_BUCKETS - 1)).astype(jnp.int32))
    return jnp.stack(maps, axis=0)


def _bias_expand(rel_bias, buckets):
    nh = rel_bias.shape[1]

    def body(rb_ref, bk_ref, o_ref):
        h = pl.program_id(0)
        bk = bk_ref[0]
        acc = jnp.zeros(bk.shape, F32)
        for b in range(_NUM_BUCKETS):
            acc = jnp.where(bk == b, rb_ref[b, h], acc)
        a = lax.broadcasted_iota(jnp.int32, bk.shape, 0)
        c = lax.broadcasted_iota(jnp.int32, bk.shape, 1)
        mdist = a + _STEPS - c
        o_ref[0] = jnp.where((mdist >= 0) & (mdist <= _STEPS), acc, _NEG)

    return _pcall(
        body, name="bias_expand", grid=(nh,),
        in_specs=[pl.BlockSpec(memory_space=pltpu.SMEM),
                  pl.BlockSpec((1, _STEPS, 2 * _STEPS), lambda h: (h // 8, 0, 0))],
        out_specs=pl.BlockSpec((1, _STEPS, 2 * _STEPS), lambda h: (h, 0, 0)),
        out_shape=jax.ShapeDtypeStruct((nh, _STEPS, 2 * _STEPS), F32))(rel_bias, buckets)


def _bias_reduce(ds_all, buckets):
    nh = ds_all.shape[0]

    def body(ds_ref, bk_ref, o_ref):
        t, bk = ds_ref[0], bk_ref[0]
        rows = lax.broadcasted_iota(jnp.int32, (_NUM_BUCKETS, _LANES), 0)
        out = jnp.zeros((_NUM_BUCKETS, _LANES), F32)
        for b in range(_NUM_BUCKETS):
            out = jnp.where(rows == b, jnp.sum(jnp.where(bk == b, t, 0.0)), out)
        o_ref[0] = out

    blk = pl.BlockSpec((1, _STEPS, 2 * _STEPS), lambda h: (h, 0, 0))
    return _pcall(
        body, name="bias_reduce", grid=(nh,),
        in_specs=[blk, pl.BlockSpec((1, _STEPS, 2 * _STEPS), lambda h: (h // 8, 0, 0))],
        out_specs=pl.BlockSpec((1, _NUM_BUCKETS, _LANES), lambda h: (h, 0, 0)),
        out_shape=jax.ShapeDtypeStruct((nh, _NUM_BUCKETS, _LANES), F32))(ds_all, buckets)


def _sub_residues(dil):
    return 4 if dil % 16 == 0 else 1


def _strided_rows(ref, tmp_ref, p, r, dil):
    sub = _sub_residues(dil)
    if dil == 1:
        return [ref[p]]
    if sub == 1:
        return [ref[p, pl.ds(r, _STEPS, stride=dil), :]]
    tmp_ref[...] = ref[p, pl.ds(r, _STEPS * sub, stride=dil // sub), :]
    return [tmp_ref[pl.ds(q, _STEPS, stride=sub), :] for q in range(sub)]


def _store_strided(ref, tmp_ref, p, r, dil, vals):
    sub = _sub_residues(dil)
    if dil == 1:
        ref[p] = vals[0]
    elif sub == 1:
        ref[p, pl.ds(r, _STEPS, stride=dil), :] = vals[0]
    else:
        for q, val in enumerate(vals):
            tmp_ref[pl.ds(q, _STEPS, stride=sub), :] = val
        ref[p, pl.ds(r, _STEPS * sub, stride=dil // sub), :] = tmp_ref[...]


def _tmp_rows(dil, count):
    sub = _sub_residues(dil)
    return [pltpu.VMEM((_STEPS * sub, _LANES), F32)] * count if sub > 1 else []


def _head_masks():
    lane = lax.broadcasted_iota(jnp.int32, (1, _LANES), 1)
    return [lane < _HEAD_DIM, lane >= _HEAD_DIM]


def _stack_heads(x2, masks):
    return jnp.concatenate([jnp.where(masks[0], x2, 0), jnp.where(masks[1], x2, 0)], axis=0)


def _unstack_heads(y, masks):
    return jnp.where(masks[0], y[:_STEPS], y[_STEPS:])


def _scores(qs2, k2, b_ref, j, first):
    sc = lax.dot_general(qs2, k2, (((1,), (1,)), ((), ())), preferred_element_type=F32)
    sc = sc * (_HEAD_DIM ** -0.5) + jnp.concatenate([b_ref[2 * j], b_ref[2 * j + 1]], axis=0)
    col = lax.broadcasted_iota(jnp.int32, sc.shape, 1)
    return jnp.where(jnp.logical_and(first, col < _STEPS), _NEG, sc)


_PAIRS = _GROUP_COLS // _LANES


def _attn_fwd(uq, uk, uv, bias, g, dil, pp, rider=None):
    s = uq.shape[1]
    rb = _STEPS * dil
    nb = s // rb
    npb = _PAIRS // pp

    sub = _sub_residues(dil)

    def compute(q_ref, kc_ref, kp_ref, vc_ref, vp_ref, b_ref, o_ref, l_ref, tmp):
        tmp = tmp + (None,) * 7
        n, r = pl.program_id(1), pl.program_id(2)
        first = n == 0
        masks = _head_masks()
        for j in range(pp):
            qs = _strided_rows(q_ref, tmp[0], j, r, dil)
            kps, kcs = _strided_rows(kp_ref, tmp[1], j, r, dil), _strided_rows(kc_ref, tmp[2], j, r, dil)
            vps, vcs = _strided_rows(vp_ref, tmp[3], j, r, dil), _strided_rows(vc_ref, tmp[4], j, r, dil)
            o_res, l_res = [], []
            for q in range(sub):
                q2 = qs[q].astype(BF16)
                k2 = jnp.concatenate([kps[q], kcs[q]], axis=0).astype(BF16)
                v2 = jnp.concatenate([vps[q], vcs[q]], axis=0).astype(BF16)
                sc = _scores(_stack_heads(q2, masks), k2, b_ref, j, first)
                mx = jnp.max(sc, axis=-1, keepdims=True)
                p = jnp.exp(sc - mx)
                den = jnp.sum(p, axis=-1, keepdims=True)
                o2 = jnp.dot(p.astype(BF16), v2, preferred_element_type=F32) / den
                o_res.append(_unstack_heads(o2, masks))
                l_res.append(_unstack_heads(jnp.broadcast_to(mx + jnp.log(den), o2.shape), masks))
            _store_strided(o_ref, tmp[5], j, r, dil, o_res)
            _store_strided(l_ref, tmp[6], j, r, dil, l_res)

    bspec = pl.BlockSpec((2 * pp, _STEPS, 2 * _STEPS), lambda hb, n, r: (g * npb + hb, 0, 0))
    ospec = pl.BlockSpec((pp, rb, _LANES), lambda hb, n, r: (hb, n, 0))
    sh = jax.ShapeDtypeStruct((_PAIRS, s, _LANES), F32)
    if dil == 1:
        def body(q_ref, kc_ref, kp_ref, vc_ref, vp_ref, b_ref, o_ref, l_ref, *tmp):
            compute(q_ref, kc_ref, kp_ref, vc_ref, vp_ref, b_ref, o_ref, l_ref, tmp)

        cur = pl.BlockSpec((pp, rb, _LANES), lambda hb, n, r: (g * npb + hb, n, 0))
        prev = pl.BlockSpec((pp, rb, _LANES), lambda hb, n, r: (g * npb + hb, jnp.maximum(n - 1, 0), 0))
        return _pcall(
            body, name=f"attn_fwd_g{g}", grid=(npb, nb, dil // sub),
            in_specs=[cur, cur, prev, cur, prev, bspec], out_specs=[ospec, ospec], out_shape=[sh, sh],
            scratch=_tmp_rows(dil, 7), rider=rider,
        )(uq, uk, uk, uv, uv, bias)

    def body(q_hbm, k_hbm, v_hbm, b_ref, o_ref, l_ref, qbuf, kbuf, vbuf, sems, *tmp):
        hb, n, r = pl.program_id(0), pl.program_id(1), pl.program_id(2)
        bufs = ((q_hbm, qbuf, 2, 0), (k_hbm, kbuf, 3, 2), (v_hbm, vbuf, 3, 5))

        def fetch(src, buf, slots, sem0, blk):
            slot = lax.rem(blk, slots)
            return pltpu.make_async_copy(
                src.at[pl.ds((g * npb + hb) * pp, pp), pl.ds(pl.multiple_of(blk * rb, rb), rb), :],
                buf.at[slot], sems.at[sem0 + slot])

        @pl.when(r == 0)
        def _():
            @pl.when(n == 0)
            def _():
                kbuf[2] = jnp.zeros((pp, rb, _LANES), F32)
                vbuf[2] = jnp.zeros((pp, rb, _LANES), F32)
                for b in bufs:
                    fetch(*b, n).start()

            for b in bufs:
                fetch(*b, n).wait()

            @pl.when(n + 1 < nb)
            def _():
                for b in bufs:
                    fetch(*b, n + 1).start()

        kc, kp = lax.rem(n, 3), lax.rem(n + 2, 3)
        compute(qbuf.at[lax.rem(n, 2)], kbuf.at[kc], kbuf.at[kp], vbuf.at[kc], vbuf.at[kp], b_ref, o_ref, l_ref, tmp)

    slot = lambda k: pltpu.VMEM((k, pp, rb, _LANES), F32)
    return _pcall(
        body, name=f"attn_fwd_g{g}", grid=(npb, nb, dil // sub),
        in_specs=[_ANY, _ANY, _ANY, bspec], out_specs=[ospec, ospec], out_shape=[sh, sh],
        scratch=[slot(2), slot(3), slot(3), pltpu.SemaphoreType.DMA((8,))] + _tmp_rows(dil, 7), rider=rider,
    )(uq, uk, uv, bias)


def _attn_merge(outs, lses, cat):
    s = outs[0].shape[1]
    c = _GROUP_COLS

    def body(o0, o1, o2, l0, l1, l2, cat_in, cat_ref, lse_ref):
        del cat_in
        a0, a1, a2 = l0[...], l1[...], l2[...]
        mx = jnp.maximum(jnp.maximum(a0, a1), a2)
        w0, w1, w2 = jnp.exp(a0 - mx), jnp.exp(a1 - mx), jnp.exp(a2 - mx)
        den = w0 + w1 + w2
        y = ((w0 * o0[...] + w1 * o1[...] + w2 * o2[...]) / den).astype(BF16)
        for p in range(_PAIRS):
            cat_ref[:, p * _LANES:(p + 1) * _LANES] = y[p]
        lse_ref[...] = mx + jnp.log(den)

    blk = pl.BlockSpec((_PAIRS, _ROW_T, _LANES), lambda i: (0, i, 0))
    return _pcall(
        body, name="attn_merge", grid=(s // _ROW_T,),
        in_specs=[blk] * 6 + [_ANY],
        out_specs=[pl.BlockSpec((_ROW_T, c), lambda i: (i, 1)), blk],
        out_shape=[jax.ShapeDtypeStruct(cat.shape, BF16), jax.ShapeDtypeStruct((_PAIRS, s, _LANES), F32)],
        aliases={6: 0})(*outs, *lses, cat)


def _attn_delta(dcat, cat):
    s = dcat.shape[0]
    c = _GROUP_COLS
    seg = (jnp.arange(c)[:, None] // _HEAD_DIM == jnp.arange(c)[None, :] // _HEAD_DIM).astype(BF16)

    def body(dy_ref, y_ref, seg_ref, dl_ref, dys_ref):
        dy = dy_ref[...]
        prod = dy * y_ref[...].astype(F32)
        hi = prod.astype(BF16)
        lo = (prod - hi.astype(F32)).astype(BF16)
        dl = (jnp.dot(hi, seg_ref[...], preferred_element_type=F32)
              + jnp.dot(lo, seg_ref[...], preferred_element_type=F32))
        for p in range(_PAIRS):
            dl_ref[p] = dl[:, p * _LANES:(p + 1) * _LANES]
            dys_ref[p] = dy[:, p * _LANES:(p + 1) * _LANES]

    right = pl.BlockSpec((_ROW_T, c), lambda i: (i, 1))
    blk = pl.BlockSpec((_PAIRS, _ROW_T, _LANES), lambda i: (0, i, 0))
    sh = jax.ShapeDtypeStruct((_PAIRS, s, _LANES), F32)
    return _pcall(
        body, name="attn_delta", grid=(s // _ROW_T,),
        in_specs=[right, right, pl.BlockSpec((c, c), lambda i: (0, 0))],
        out_specs=[blk, blk], out_shape=[sh, sh])(dcat, cat, seg)


def _attn_bwd(uq, uk, uv, dys, lse, delta, bias, prev_grads, g, dil, pp, rider=None):
    s = uq.shape[1]
    rb = _STEPS * dil
    nb = s // rb
    npb = _PAIRS // pp
    scale = _HEAD_DIM ** -0.5

    sub = _sub_residues(dil)

    def body(q_ref, kc_ref, kp_ref, vc_ref, vp_ref, dy_ref, l_ref, dl_ref, b_ref, *rest):
        rest = rest[len(prev_grads):]
        dqkv_ref, dsa_ref, dqc_ref, dkc_ref, dvc_ref = rest[:5]
        dq_ref, dk_ref, dv_ref = dqkv_ref.at[0], dqkv_ref.at[1], dqkv_ref.at[2]
        tmp = rest[5:] + (None,) * 11
        n, r = pl.program_id(1), pl.program_id(2)

        def carry_slot(j, q):
            return ((r + (dil // sub) * q) * pp + j) if sub > 1 else r * pp + j

        @pl.when(jnp.logical_and(n == 0, r == 0))
        def _():
            dsa_ref[...] = jnp.zeros_like(dsa_ref)

        @pl.when(n == 0)
        def _():
            for j in range(pp):
                for q in range(sub):
                    for carry in (dqc_ref, dkc_ref, dvc_ref):
                        carry[carry_slot(j, q)] = jnp.zeros((_STEPS, _LANES), F32)

        @pl.when(n < nb)
        def _():
            first = n == 0
            masks = _head_masks()
            for j in range(pp):
                qs = _strided_rows(q_ref, tmp[0], j, r, dil)
                kps, kcs = _strided_rows(kp_ref, tmp[1], j, r, dil), _strided_rows(kc_ref, tmp[2], j, r, dil)
                vps, vcs = _strided_rows(vp_ref, tmp[3], j, r, dil), _strided_rows(vc_ref, tmp[4], j, r, dil)
                dys_ = _strided_rows(dy_ref, tmp[5], j, r, dil)
                lses = _strided_rows(l_ref, tmp[6], j, r, dil)
                dls = _strided_rows(dl_ref, tmp[7], j, r, dil)
                ds_sum = [jnp.zeros((_STEPS, 2 * _STEPS), F32)] * 2
                dq_res, dk_res, dv_res = [], [], []
                for q in range(sub):
                    q2 = qs[q].astype(BF16)
                    k2 = jnp.concatenate([kps[q], kcs[q]], axis=0).astype(BF16)
                    v2 = jnp.concatenate([vps[q], vcs[q]], axis=0).astype(BF16)
                    dy2 = dys_[q].astype(BF16)
                    qs2, dys2 = _stack_heads(q2, masks), _stack_heads(dy2, masks)
                    per_row = lambda st: jnp.concatenate([st[:, 0:1], st[:, _HEAD_DIM:_HEAD_DIM + 1]], axis=0)
                    sc = _scores(qs2, k2, b_ref, j, first)
                    p = jnp.exp(sc - per_row(lses[q]))
                    dp = lax.dot_general(dys2, v2, (((1,), (1,)), ((), ())), preferred_element_type=F32)
                    ds = p * (dp - per_row(dls[q]))
                    ds_sum[0] = ds_sum[0] + ds[:_STEPS]
                    ds_sum[1] = ds_sum[1] + ds[_STEPS:]
                    dsb = ds.astype(BF16)
                    dq_p = _unstack_heads(jnp.dot(dsb, k2, preferred_element_type=F32), masks)
                    tdn = (((0,), (0,)), ((), ()))
                    dk_p = lax.dot_general(dsb, qs2, tdn, preferred_element_type=F32) * scale
                    dv_p = lax.dot_general(p.astype(BF16), dys2, tdn, preferred_element_type=F32)
                    slot = carry_slot(j, q)
                    dq_res.append(dqc_ref[slot])
                    dk_res.append(dkc_ref[slot] + dk_p[:_STEPS])
                    dv_res.append(dvc_ref[slot] + dv_p[:_STEPS])
                    dqc_ref[slot] = dq_p * scale
                    dkc_ref[slot] = dk_p[_STEPS:]
                    dvc_ref[slot] = dv_p[_STEPS:]
                for hh in range(2):
                    dsa_ref[2 * j + hh] += ds_sum[hh]
                _store_strided(dq_ref, tmp[8], j, r, dil, dq_res)
                _store_strided(dk_ref, tmp[9], j, r, dil, dk_res)
                _store_strided(dv_ref, tmp[10], j, r, dil, dv_res)

        @pl.when(n == nb)
        def _():
            for j in range(pp):
                for ref, carry, t in ((dq_ref, dqc_ref, 8), (dk_ref, dkc_ref, 9), (dv_ref, dvc_ref, 10)):
                    _store_strided(ref, tmp[t], j, r, dil, [carry[carry_slot(j, q)] for q in range(sub)])

    def clamp(n):
        return jnp.minimum(n, nb - 1)

    cur = pl.BlockSpec((pp, rb, _LANES), lambda hb, n, r: (g * npb + hb, clamp(n), 0))
    prev = pl.BlockSpec((pp, rb, _LANES), lambda hb, n, r: (g * npb + hb, jnp.maximum(clamp(n) - 1, 0), 0))
    stat = pl.BlockSpec((pp, rb, _LANES), lambda hb, n, r: (hb, clamp(n), 0))
    bspec = pl.BlockSpec((2 * pp, _STEPS, 2 * _STEPS), lambda hb, n, r: (g * npb + hb, 0, 0))
    late = pl.BlockSpec((3, pp, rb, _LANES), lambda hb, n, r: (0, g * npb + hb, jnp.maximum(n - 1, 0), 0))
    dsspec = pl.BlockSpec((2 * pp, _STEPS, 2 * _STEPS), lambda hb, n, r: (hb, 0, 0))
    np_ = len(prev_grads)
    carry = pltpu.VMEM((dil * pp, _STEPS, _LANES), F32)
    return _pcall(
        body, name=f"attn_bwd_g{g}", grid=(npb, nb + 1, dil // sub),
        in_specs=[cur, cur, prev, cur, prev, stat, stat, stat, bspec] + [_ANY] * np_,
        out_specs=[late, dsspec],
        out_shape=[jax.ShapeDtypeStruct((3, 3 * _PAIRS, s, _LANES), F32),
                   jax.ShapeDtypeStruct((8, _STEPS, 2 * _STEPS), F32)],
        scratch=[carry, carry, carry] + _tmp_rows(dil, 11),
        aliases={9 + t: t for t in range(np_)}, rider=rider,
    )(uq, uk, uk, uv, uv, dys, lse, delta, bias, *prev_grads)


def _place():
    x, y, c = lax.axis_index("x"), lax.axis_index("y"), lax.axis_index("c")
    chips = [(1 - x, y), (x, 1 - y), (1 - x, 1 - y)]
    return x, y, c, chips


def _slab(ref, axis, chip, width):
    start = pl.multiple_of(chip * width, width)
    if axis == 0:
        return ref.at[pl.ds(start, width), :]
    return ref.at[:, pl.ds(start, width)]


def _run_rider(rider, name):
    nin, nout = len(rider.ins), len(rider.out_shapes)

    def body(*refs):
        ins, outs, scr = refs[:nin], refs[nin:nin + nout], refs[nin + nout:]
        rider.start(ins, outs, scr)
        rider.finish(ins, outs, scr)

    return _pcall(body, name=name, in_specs=[_ANY] * nin, out_specs=[_ANY] * nout, out_shape=rider.out_shapes,
                  scratch=rider.scratch)(*rider.ins)


def _gather_halves_rider(shard, axis):
    shape = list(shard.shape)
    shape[axis] *= 4
    full = jax.ShapeDtypeStruct(tuple(shape), shard.dtype)
    half = shard.shape[0] // 2
    width = shard.shape[axis]

    def region(out, chip, core):
        if axis == 0:
            return out.at[pl.ds(pl.multiple_of(chip * width + core * half, half), half), :]
        return out.at[pl.ds(pl.multiple_of(core * half, half), half), pl.ds(pl.multiple_of(chip * width, width), width)]

    def copies(ins, outs, scr):
        send, recv, loc = scr
        (src,), (out,) = ins, outs
        x, y, c, chips = _place()
        mine = 2 * x + y
        own = pltpu.make_async_copy(src, _slab(out, axis, mine, width), loc.at[0])
        my_half = src.at[pl.ds(pl.multiple_of(c * half, half), half), :]
        over_ici, ici_in, to_sib, sib_in = [], [], [], []
        for j, (px, py) in enumerate(chips):
            theirs = 2 * px + py
            ici = dict(send_sem=send.at[j], recv_sem=recv.at[j], device_id=(px, py, c), device_id_type=MESH)
            d2d = dict(send_sem=send.at[3 + j], recv_sem=recv.at[3 + j], device_id=(x, y, 1 - c),
                       device_id_type=MESH)
            over_ici.append(pltpu.make_async_remote_copy(src_ref=my_half, dst_ref=region(out, mine, c), **ici))
            ici_in.append(pltpu.make_async_remote_copy(src_ref=my_half, dst_ref=region(out, theirs, c), **ici))
            to_sib.append(pltpu.make_async_remote_copy(
                src_ref=region(out, theirs, c), dst_ref=region(out, theirs, c), **d2d))
            sib_in.append(pltpu.make_async_remote_copy(
                src_ref=region(out, theirs, c), dst_ref=region(out, theirs, 1 - c), **d2d))
        return own, over_ici, ici_in, to_sib, sib_in

    def start(ins, outs, scr):
        own, over_ici, _, _, _ = copies(ins, outs, scr)
        own.start()
        for cp in over_ici:
            cp.start()

    def finish(ins, outs, scr):
        own, over_ici, ici_in, to_sib, sib_in = copies(ins, outs, scr)
        for j in range(3):
            ici_in[j].wait_recv()
            to_sib[j].start()
        for cp in sib_in:
            cp.wait_recv()
        own.wait()
        for cp in over_ici + to_sib:
            cp.wait_send()

    return _Rider([shard], [full], [pltpu.SemaphoreType.DMA((6,)), pltpu.SemaphoreType.DMA((6,)),
                                    pltpu.SemaphoreType.DMA((1,))], start, finish)


def _join_riders(riders):
    if len(riders) == 1:
        return riders[0]

    def parts(ins, outs, scr):
        pi = po = ps = 0
        for rd in riders:
            ni, no, ns = len(rd.ins), len(rd.out_shapes), len(rd.scratch)
            yield rd, ins[pi:pi + ni], outs[po:po + no], scr[ps:ps + ns]
            pi, po, ps = pi + ni, po + no, ps + ns

    def start(ins, outs, scr):
        for rd, i, o, sc in parts(ins, outs, scr):
            rd.start(i, o, sc)

    def finish(ins, outs, scr):
        for rd, i, o, sc in parts(ins, outs, scr):
            rd.finish(i, o, sc)

    return _Rider(sum((rd.ins for rd in riders), []), sum((rd.out_shapes for rd in riders), []),
                  sum((rd.scratch for rd in riders), []), start, finish)


def _scatter_rider(grads, axes, rows=None):
    nw = len(grads)
    outs_shape = []
    for gr, ax in zip(grads, axes):
        shape = list(gr.shape)
        shape[ax] //= 4
        if rows is not None:
            assert ax == 1
            shape[0] = rows[1] - rows[0]
        outs_shape.append(jax.ShapeDtypeStruct((4,) + tuple(shape), gr.dtype))

    def copies(ins, outs, scr):
        send, recv, loc = scr
        x, y, c, chips = _place()
        cps, own = [], []
        for t in range(nw):
            width = ins[t].shape[axes[t]] // 4
            src = ins[t] if rows is None else ins[t].at[pl.ds(rows[0], rows[1] - rows[0]), :]
            own.append(pltpu.make_async_copy(_slab(src, axes[t], 2 * x + y, width), outs[t].at[3], loc.at[t]))
            for j, (px, py) in enumerate(chips):
                cps.append(pltpu.make_async_remote_copy(
                    src_ref=_slab(src, axes[t], 2 * px + py, width), dst_ref=outs[t].at[j],
                    send_sem=send.at[3 * t + j], recv_sem=recv.at[3 * t + j],
                    device_id=(px, py, c), device_id_type=MESH))
        return cps, own

    def start(ins, outs, scr):
        cps, own = copies(ins, outs, scr)
        for cp in cps + own:
            cp.start()

    def finish(ins, outs, scr):
        cps, own = copies(ins, outs, scr)
        for cp in cps:
            cp.wait_recv()
        for cp in own:
            cp.wait()
        for cp in cps:
            cp.wait_send()

    return _Rider(grads, outs_shape, [pltpu.SemaphoreType.DMA((3 * nw,)), pltpu.SemaphoreType.DMA((3 * nw,)),
                                      pltpu.SemaphoreType.DMA((nw,))], start, finish)


def _swap_rider(parts):
    nw = len(parts)

    def copies(ins, outs, scr):
        send, recv = scr
        x, y, c, _ = _place()
        return [pltpu.make_async_remote_copy(
            src_ref=ins[t], dst_ref=outs[t], send_sem=send.at[t], recv_sem=recv.at[t],
            device_id=(x, y, 1 - c), device_id_type=MESH) for t in range(nw)]

    def start(ins, outs, scr):
        for cp in copies(ins, outs, scr):
            cp.start()

    def finish(ins, outs, scr):
        cps = copies(ins, outs, scr)
        for cp in cps:
            cp.wait_recv()
        for cp in cps:
            cp.wait_send()

    return _Rider(parts, [jax.ShapeDtypeStruct(p.shape, p.dtype) for p in parts],
                  [pltpu.SemaphoreType.DMA((nw,)), pltpu.SemaphoreType.DMA((nw,))], start, finish)


def _sum_all_devices(buf, name):
    rows, cols = buf.shape

    def body(in_ref, o_ref, gat_ref, send, recv):
        x, y, c, _ = _place()
        me = 4 * x + 2 * y + c
        gat_ref[me] = in_ref[...]
        started = []
        for mask in range(1, 8):
            fx, fy, fc = (mask >> 2) & 1, (mask >> 1) & 1, mask & 1
            peer = (x + fx * (1 - 2 * x), y + fy * (1 - 2 * y), c + fc * (1 - 2 * c))
            cp = pltpu.make_async_remote_copy(
                src_ref=in_ref, dst_ref=gat_ref.at[me], send_sem=send.at[mask - 1], recv_sem=recv.at[mask - 1],
                device_id=peer, device_id_type=MESH)
            cp.start()
            started.append(cp)
        for cp in started:
            cp.wait_recv()
        for cp in started:
            cp.wait_send()
        acc = gat_ref[0]
        for t in range(1, 8):
            acc = acc + gat_ref[t]
        o_ref[...] = acc

    vm = pl.BlockSpec(memory_space=pltpu.VMEM)
    return _pcall(
        body, name=name, in_specs=[vm], out_specs=vm, out_shape=jax.ShapeDtypeStruct((rows, cols), F32),
        scratch=[pltpu.VMEM((8, rows, cols), F32), pltpu.SemaphoreType.DMA((7,)), pltpu.SemaphoreType.DMA((7,))],
    )(buf)


_UPD_T = 256


def _sum_partials(got, name):
    _, rows, cols = got.shape
    tr = min(_UPD_T, rows)

    def body(got_ref, o_ref):
        acc = got_ref[3].astype(F32)
        for j in range(3):
            acc = acc + got_ref[j].astype(F32)
        o_ref[...] = acc

    return _pcall(
        body, name=name, grid=(rows // tr,),
        in_specs=[pl.BlockSpec((4, tr, cols), lambda i: (0, i, 0))], out_specs=pl.BlockSpec((tr, cols), lambda i: (i, 0)),
        out_shape=jax.ShapeDtypeStruct((rows, cols), F32))(got)


def _adamw_math(w, gr, m, v):
    m = _B1 * m + (1.0 - _B1) * gr
    v = _B2 * v + (1.0 - _B2) * (gr * gr)
    m_hat = m / (1.0 - _B1 ** _STEP)
    v_hat = v / (1.0 - _B2 ** _STEP)
    delta = -_LR * (m_hat / (jnp.sqrt(v_hat) + _EPS) + _WD * w)
    return delta, m, v


def _adamw(w, m, v, parts, name):
    rows, cols = w.shape
    tr = min(_UPD_T, rows)
    npart = len(parts)

    def body(w_ref, m_ref, v_ref, *rest):
        p_refs, (g_ref, d_ref, nm_ref, nv_ref) = rest[:npart], rest[npart:]
        gr = p_refs[0][...]
        for p in p_refs[1:]:
            gr = gr + p[...]
        delta, nm, nv = _adamw_math(w_ref[...], gr, m_ref[...], v_ref[...])
        g_ref[...] = gr
        d_ref[...] = delta
        nm_ref[...] = nm
        nv_ref[...] = nv

    blk = pl.BlockSpec((tr, cols), lambda i: (i, 0))
    sh = jax.ShapeDtypeStruct((rows, cols), F32)
    return _pcall(body, name=name, grid=(rows // tr,), in_specs=[blk] * (3 + npart), out_specs=[blk] * 4,
                  out_shape=[sh] * 4)(w, m, v, *parts)


def _adamw_layers(w, m, v, parts, name):
    _, rows, cols = w.shape
    tr = min(_UPD_T, rows)
    npart = len(parts[0])

    def body(w_ref, m_ref, v_ref, *rest):
        p_refs, (g_ref, d_ref, nm_ref, nv_ref) = rest[:2 * npart], rest[2 * npart:]
        grs = []
        for layer in range(2):
            gr = p_refs[layer * npart][...]
            for p in p_refs[layer * npart + 1:(layer + 1) * npart]:
                gr = gr + p[...]
            grs.append(gr)
        gr = jnp.where(pl.program_id(0) == 0, grs[0], grs[1])
        delta, nm, nv = _adamw_math(w_ref[...], gr, m_ref[...], v_ref[...])
        g_ref[...] = gr
        d_ref[...] = delta
        nm_ref[...] = nm
        nv_ref[...] = nv

    blk = pl.BlockSpec((None, tr, cols), lambda l, i: (l, i, 0))

    def part_spec(layer):
        return pl.BlockSpec((tr, cols), lambda l, i: (jnp.where(l == layer, i, 0), 0))

    sh = jax.ShapeDtypeStruct(w.shape, F32)
    return _pcall(
        body, name=name, grid=(2, rows // tr),
        in_specs=[blk] * 3 + [part_spec(0)] * npart + [part_spec(1)] * npart, out_specs=[blk] * 4,
        out_shape=[sh] * 4)(w, m, v, *parts[0], *parts[1])


_PACK_W = 1024


def _pack(arrs, rows):
    flat = []
    for a in arrs:
        f = a.reshape(-1).astype(F32)
        pad = (-f.shape[0]) % _PACK_W
        flat.append(jnp.pad(f, (0, pad)))
    f = jnp.concatenate(flat)
    f = jnp.pad(f, (0, rows * _PACK_W - f.shape[0]))
    return f.reshape(rows, _PACK_W)


def _unpack(buf, shapes):
    flat = buf.reshape(-1)
    out, pos = [], 0
    for sh in shapes:
        size = math.prod(sh)
        out.append(flat[pos:pos + size].reshape(sh))
        pos += size + ((-size) % _PACK_W)
    return out


def _pack_rows(shapes):
    total = sum(-(-math.prod(sh) // _PACK_W) for sh in shapes)
    return -(-total // 8) * 8


def kernel(x, rel_bias, ab_norm, ab_w_in, ab_conv_w, ab_conv_b, ab_ln_g, ab_ln_b, ab_w_out, sc_norm, sc_w_in, sc_conv_w, sc_w_out, mlp_norm, mlp_w_up, mlp_w_down, final_norm, loss_target, m_rel_bias, m_ab_norm, m_ab_w_in, m_ab_conv_w, m_ab_conv_b, m_ab_ln_g, m_ab_ln_b, m_ab_w_out, m_sc_norm, m_sc_w_in, m_sc_conv_w, m_sc_w_out, m_mlp_norm, m_mlp_w_up, m_mlp_w_down, m_final_norm, v_rel_bias, v_ab_norm, v_ab_w_in, v_ab_conv_w, v_ab_conv_b, v_ab_ln_g, v_ab_ln_b, v_ab_w_out, v_sc_norm, v_sc_w_in, v_sc_conv_w, v_sc_w_out, v_mlp_norm, v_mlp_w_up, v_mlp_w_down, v_final_norm):
    s, d = x.shape[1], x.shape[2]
    dff = 4 * d
    c = _GROUP_COLS
    chip = 2 * lax.axis_index("x") + lax.axis_index("y")
    on_c0 = (lax.axis_index("c") == 0).astype(F32)
    h0 = x[0]
    tgt = loss_target[0]

    cw_sh, scn_sh, scw_sh = ab_conv_w[0], sc_norm, sc_conv_w[0]
    conv_w_full = lax.dynamic_update_slice(jnp.zeros((_CONV_K, c), F32), cw_sh * on_c0, (0, chip * cw_sh.shape[1]))
    scn_full = lax.dynamic_update_slice(jnp.zeros((1, d), F32), scn_sh * on_c0, (0, chip * scn_sh.shape[1]))
    scw_full = lax.dynamic_update_slice(jnp.zeros((3, d), F32), scw_sh * on_c0, (0, chip * scw_sh.shape[1]))
    small_shapes = [(_CONV_K, c), (1, d), (3, d)]
    small = _sum_all_devices(_pack([conv_w_full, scn_full, scw_full], _pack_rows(small_shapes)), "gather_small")
    conv_w, sc_g, sc_cw = _unpack(small, small_shapes)

    w_shards = [ab_w_in[0], ab_w_out[0], sc_w_in[0], sc_w_out[0], mlp_w_up[0], mlp_w_up[1],
                mlp_w_down[0], mlp_w_down[1]]
    w_axes = [1, 0, 1, 0, 1, 1, 0, 0]
    wb = [w.astype(BF16) for w in w_shards]
    full_w = [None] * 8

    def gather(idx):
        return _join_riders([_gather_halves_rider(wb[t], w_axes[t]) for t in idx])

    def put(idx, got_w):
        for t, w in zip(idx, got_w):
            full_w[t] = w

    buckets = _bucket_maps()
    bias = _bias_expand(rel_bias, buckets)
    n0, got_w = _rms_fwd(h0, ab_norm, "rms_fwd_ab", rider=_gather_halves_rider(wb[0], w_axes[0]))
    put([0], got_w)
    w_in = full_w[0]
    tm = min(1024, s)
    tm2 = min(2048, s)
    tmh = min(512, s)
    uc = _mm(n0, w_in, "nn", m=s, n=2 * c, k=d, tm=tm2, tn=2 * c, tk=d, out_dtype=BF16, name="proj_conv")
    uq, uk, uv = [], [], []
    for t, (nm, dst) in enumerate(zip("qkv", (uq, uk, uv))):
        res = _mm(n0, w_in, "nn", m=s, n=3 * c, k=d, tm=tm2, tn=c, tk=d, out_dtype=F32, name=f"proj_{nm}",
                  b_off=(0, 2 + 3 * t), split="o", rider=gather([1]) if t == 0 else None)
        if t == 0:
            res, got_w = res
            put([1], got_w)
        dst.append(res)
    uq, uk, uv = uq[0], uk[0], uv[0]
    (cat, ca), got_w = _conv_a_fwd(uc, conv_w, ab_conv_b, ab_ln_g, ab_ln_b, rider=gather([2]))
    put([2], got_w)
    outs, lses = [], []
    for g, (_, dil) in enumerate(_GROUPS):
        idx = ([4], [6], [3, 5])[g]
        (o, l), got_w = _attn_fwd(uq, uk, uv, bias, g, dil, 4 if dil <= 4 else 2, rider=gather(idx))
        put(idx, got_w)
        outs.append(o)
        lses.append(l)
    cat, lse = _attn_merge(outs, lses, cat)
    h1, n1 = _mm(cat, full_w[1], "nn", m=s, n=d, k=d, tm=tm, tn=d, tk=d, out_dtype=(F32, BF16), name="out_ab",
                 epi=_epi_add_rms, extras=(h0,), vecs=(mlp_norm[0:1],))

    def mlp_fwd(h, nrm, layer, next_gain=None, rider=None):
        zr = _mm(nrm, full_w[4 + layer], "nn", m=s, n=dff, k=d, tm=tmh, tn=dff, tk=d, out_dtype=BF16,
                 name=f"mlp_up{layer}", epi=_epi_relu, rider=rider)
        if rider is not None:
            zr, got_r = zr
            put([7], got_r)
        kw = dict(m=s, n=d, k=dff, tm=tmh, tn=d, tk=dff, name=f"mlp_down{layer}", a_pro=_square, extras=(h,))
        if next_gain is None:
            return zr, _mm(zr, full_w[6 + layer], "nn", out_dtype=F32, epi=_epi_add, **kw), None
        hn, nn = _mm(zr, full_w[6 + layer], "nn", out_dtype=(F32, BF16), epi=_epi_add_rms, vecs=(next_gain,), **kw)
        return zr, hn, nn

    zr0, h2, n2 = mlp_fwd(h1, n1, 0, next_gain=sc_g, rider=gather([7]))
    _, w_out, w_si, w_so, w_up0, w_up1, w_dn0, w_dn1 = full_w
    w_up, w_dn = [w_up0, w_up1], [w_dn0, w_dn1]
    u2 = _mm(n2, w_si, "nn", m=s, n=3 * d, k=d, tm=tmh, tn=3 * d, tk=d, out_dtype=BF16, name="proj_sc")
    scv = _short_conv_fwd(u2, sc_cw)
    h3, n3 = _mm(scv, w_so, "nn", m=s, n=d, k=d, tm=tm, tn=d, tk=d, out_dtype=(F32, BF16), name="out_sc",
                 epi=_epi_add_rms, extras=(h2,), vecs=(mlp_norm[1:2],))
    zr1, h4, _ = mlp_fwd(h3, n3, 1)

    dh4, dh4b, g_final, loss_part = _loss_head(h4, tgt, final_norm.reshape(1, d))
    tkw = min(4096, s)

    big_grads, sums = [None] * 8, [None] * 8

    def scatter(t):
        return _scatter_rider([big_grads[t]], [w_axes[t]])

    def arrived(t, got_t):
        sums[t] = _sum_partials(got_t[0], f"sum_partials{t}")

    def mlp_bwd(dh, dhb, h, nrm, zr, layer):
        dz = _mm(dhb, w_dn[layer], "nt", m=s, n=dff, k=d, tm=tmh, tn=dff, tk=d, out_dtype=BF16,
                 name=f"mlp_down{layer}_dx", epi=_epi_relu_sq_bwd, extras=(zr,))
        big_grads[6 + layer] = _mm(zr, dhb, "tn", m=dff, n=d, k=s, tm=1024, tn=d, tk=tkw, out_dtype=BF16,
                                   name=f"mlp_down{layer}_dw", a_pro=_square)
        big_grads[4 + layer] = _mm(nrm, dz, "tn", m=d, n=dff, k=s, tm=d, tn=1024, tk=tkw, out_dtype=BF16,
                                   name=f"mlp_up{layer}_dw")
        res = _mm(dz, w_up[layer], "nt", m=s, n=d, k=dff, tm=tmh, tn=d, tk=dff, out_dtype=(F32, BF16),
                  name=f"mlp_up{layer}_dx", rider=scatter(6) if layer == 0 else None, epi=_epi_rms_bwd,
                  extras=(h, dh), vecs=(mlp_norm[layer:layer + 1],), row_sum=True)
        if layer == 0:
            res, got_t = res
            arrived(6, got_t)
        return res

    dh3, dh3b, g_mn1 = mlp_bwd(dh4, dh4b, h3, n3, zr1, 1)

    dsc = _mm(dh3b, w_so, "nt", m=s, n=d, k=d, tm=tm, tn=d, tk=d, out_dtype=F32, name="out_sc_dx")
    big_grads[3] = _mm(scv, dh3b, "tn", m=d, n=d, k=s, tm=d, tn=d, tk=tkw, out_dtype=BF16, name="out_sc_dw")
    du2, g_sccw8 = _short_conv_bwd(u2, dsc, sc_cw)
    big_grads[2], got_t = _mm(n2, du2, "tn", m=d, n=3 * d, k=s, tm=d, tn=1024, tk=tkw, out_dtype=BF16,
                              name="proj_sc_dw", rider=scatter(3))
    arrived(3, got_t)
    dh2, dh2b, g_scn = _mm(
        du2, w_si, "nt", m=s, n=d, k=3 * d, tm=tmh, tn=d, tk=3 * d, out_dtype=(F32, BF16), name="proj_sc_dx",
        epi=_epi_rms_bwd, extras=(h2, dh3), vecs=(sc_g,), row_sum=True)

    dh1, dh1b, g_mn0 = mlp_bwd(dh2, dh2b, h1, n1, zr0, 0)

    dcat = _mm(dh1b, w_out, "nt", m=s, n=d, k=d, tm=tm, tn=d, tk=d, out_dtype=F32, name="out_ab_dx")
    big_grads[1] = _mm(cat, dh1b, "tn", m=d, n=d, k=s, tm=d, tn=d, tk=tkw, out_dtype=BF16, name="out_ab_dw")
    (dca, conv_stats), got_t = _conv_a_bwd_ln(ca, dcat, ab_ln_g, ab_ln_b, rider=scatter(1))
    arrived(1, got_t)
    (duc, g_cw32), got_t = _conv_a_bwd_conv(uc, dca, conv_w, rider=scatter(4))
    arrived(4, got_t)
    delta, dys = _attn_delta(dcat, cat)

    dqkv, ds_list = [], []
    for g, (_, dil) in enumerate(_GROUPS):
        late = (2, 5, 7)[g]
        (grads, dsa), got_t = _attn_bwd(uq, uk, uv, dys, lse, delta, bias, dqkv, g, dil, 4 if dil <= 4 else 1,
                                        rider=scatter(late))
        arrived(late, got_t)
        dqkv = [grads]
        ds_list.append(dsa)
    g_bias = _bias_reduce(jnp.concatenate(ds_list, axis=0), buckets)[:, :, 0].T
    dqkv = dqkv[0].reshape(9 * _PAIRS, s, _LANES)

    g_in_conv = _mm(n0, duc, "tn", m=d, n=2 * c, k=s, tm=d, tn=2 * c, tk=tkw, out_dtype=BF16,
                    name="proj_ab_dw_conv")
    g_in_qkv, sib_late = _mm(n0, dqkv, "tn", m=d, n=9 * c, k=s, tm=d, tn=3 * c, tk=min(1024, s), out_dtype=BF16,
                             name="proj_ab_dw_qkv", rider=_swap_rider(sums[1:]), split="b")
    big_grads[0] = jnp.concatenate([g_in_conv, g_in_qkv], axis=1)
    cut = d // 4
    dn0, (got_top,) = _mm(duc, w_in, "nt", m=s, n=d, k=2 * c, tm=tm, tn=d, tk=2 * c, out_dtype=F32,
                          name="proj_ab_dx_conv", rider=_scatter_rider([big_grads[0]], [w_axes[0]], rows=(0, cut)))
    dn0, (got_bot,) = _mm(dqkv, w_in[:, 2 * c:], "nt", m=s, n=d, k=9 * c, tm=tm, tn=d, tk=3 * c, out_dtype=F32,
                          name="proj_ab_dx_qkv", epi=_epi_add, extras=(dn0,), split="a",
                          rider=_scatter_rider([big_grads[0]], [w_axes[0]], rows=(cut, d)))
    grad_x, g_abn = _rms_bwd(dn0, h0, ab_norm, dh1, "rms_bwd_ab")
    sums[0] = jnp.concatenate([_sum_partials(got_top, "sum_partials0_top"),
                               _sum_partials(got_bot, "sum_partials0_bottom")], axis=0)
    sib = list(_run_rider(_swap_rider([sums[0]]), "swap_sibling_w_in")) + sib_late

    upd = [_adamw(w_shards[t], mm[0], vv[0], [sums[t], sib[t]], f"adamw{t}")
           for t, (mm, vv) in enumerate(((m_ab_w_in, v_ab_w_in), (m_ab_w_out, v_ab_w_out),
                                         (m_sc_w_in, v_sc_w_in), (m_sc_w_out, v_sc_w_out)))]
    upd_up = _adamw_layers(mlp_w_up, m_mlp_w_up, v_mlp_w_up, [[sums[4], sib[4]], [sums[5], sib[5]]], "adamw_up")
    upd_dn = _adamw_layers(mlp_w_down, m_mlp_w_down, v_mlp_w_down, [[sums[6], sib[6]], [sums[7], sib[7]]],
                           "adamw_down")

    full_shapes = [(_NUM_BUCKETS, rel_bias.shape[1]), (1, d), (_CONV_K, c), (1, c), (1, c), (1, c), (1, d),
                   (3, d), (2, d), (d,), (1, 1)]
    small_grads = [g_bias, g_abn, g_cw32[:_CONV_K], conv_stats[0:1], conv_stats[1:2], conv_stats[2:3], g_scn,
                   g_sccw8[:3], jnp.concatenate([g_mn0, g_mn1], axis=0), g_final.reshape(d), loss_part[0:1, 0:1]]
    tot = _unpack(_sum_all_devices(_pack(small_grads, _pack_rows(full_shapes)), "sum_small"), full_shapes)
    loss = tot.pop()[0, 0]
    for idx, sh in ((2, cw_sh), (6, scn_sh), (7, scw_sh)):
        width = sh.shape[1]
        tot[idx] = lax.dynamic_slice_in_dim(tot[idx], chip * width, width, axis=1)
    sm_w = [rel_bias, ab_norm, cw_sh, ab_conv_b, ab_ln_g, ab_ln_b, scn_sh, scw_sh, mlp_norm, final_norm]
    sm_m = [m_rel_bias, m_ab_norm, m_ab_conv_w[0], m_ab_conv_b, m_ab_ln_g, m_ab_ln_b, m_sc_norm, m_sc_conv_w[0],
            m_mlp_norm, m_final_norm]
    sm_v = [v_rel_bias, v_ab_norm, v_ab_conv_w[0], v_ab_conv_b, v_ab_ln_g, v_ab_ln_b, v_sc_norm, v_sc_conv_w[0],
            v_mlp_norm, v_final_norm]
    sh_shapes = [tuple(t.shape) for t in tot]
    rows = _pack_rows(sh_shapes)
    sm_upd = _adamw(_pack(sm_w, rows), _pack(sm_m, rows), _pack(sm_v, rows), [_pack(tot, rows)], "adamw_small")
    sm_g, sm_d, sm_nm, sm_nv = [_unpack(buf, sh_shapes) for buf in sm_upd]

    def assemble(kind, sm):
        big = [u[kind] for u in upd]
        return [sm[0], sm[1], big[0][None], sm[2][None], sm[3], sm[4], sm[5], big[1][None], sm[6], big[2][None],
                sm[7][None], big[3][None], sm[8], upd_up[kind], upd_dn[kind], sm[9]]

    res = [loss, grad_x[None]]
    for kind, sm in enumerate((sm_g, sm_d, sm_nm, sm_nv)):
        res += assemble(kind, sm)
    return tuple(res)
```
